```python
import jax, jax.numpy as jnp
from jax import lax
import numpy as np

D_MODEL = 1024
BATCH = 16
SEQ = 2048
DEPTH = 1

PLE_DIM = 256
SSM_EXPAND = 2
D_INNER = SSM_EXPAND * D_MODEL
SSM_HEAD_DIM = 64
SSM_HEADS = D_INNER // SSM_HEAD_DIM
SSM_GROUPS = 4
HEADS_PER_GROUP = SSM_HEADS // SSM_GROUPS
D_STATE = 128
CONV_WIDTH = 4
CHUNK = 128
D_CONV_CH = D_INNER + 2 * SSM_GROUPS * D_STATE
D_POOL = D_MODEL
POOL_WINDOWS = (2, 4, 8, 16)
POOL_GROUPS = len(POOL_WINDOWS)
POOL_GROUP_DIM = D_POOL // POOL_GROUPS
IN_SPLITS = (D_INNER, D_CONV_CH, SSM_HEADS, D_POOL, D_POOL, D_MODEL, D_MODEL)
N_IN = sum(IN_SPLITS)
EPS = 1e-6

kernel_name = "hybrid_ssd_multipool_gated_merge"


def rms_norm(x, g):
    xf = x.astype(jnp.float32)
    y = xf * lax.rsqrt(jnp.mean(xf * xf, axis=-1, keepdims=True) + EPS)
    return (y * g.astype(jnp.float32)).astype(x.dtype)


def causal_depthwise_conv(u, w, b):
    out = lax.conv_general_dilated(
        u, w[:, None, :].astype(u.dtype), window_strides=(1,),
        padding=[(CONV_WIDTH - 1, 0)],
        dimension_numbers=("NWC", "WIO", "NWC"),
        feature_group_count=u.shape[-1])
    return out + b


def ssd_chunked(xs, dt, a_neg, bm, cm):
    bsz, seqlen = xs.shape[0], xs.shape[1]
    nc = seqlen // CHUNK
    dtype = xs.dtype
    x = xs.reshape(bsz, nc, CHUNK, SSM_GROUPS, HEADS_PER_GROUP, SSM_HEAD_DIM)
    dtc = dt.reshape(bsz, nc, CHUNK, SSM_GROUPS, HEADS_PER_GROUP)
    bc = bm.reshape(bsz, nc, CHUNK, SSM_GROUPS, D_STATE)
    cc = cm.reshape(bsz, nc, CHUNK, SSM_GROUPS, D_STATE)
    xdt = x * dtc[..., None]
    a_dt = dtc.astype(jnp.float32) * a_neg.astype(jnp.float32).reshape(SSM_GROUPS, HEADS_PER_GROUP)
    a_cs = jnp.cumsum(a_dt, axis=2)

    seg = a_cs[:, :, :, None] - a_cs[:, :, None, :]
    mask = jnp.tril(jnp.ones((CHUNK, CHUNK), dtype=bool))[:, :, None, None]
    decay_ls = jnp.exp(jnp.where(mask, seg, -jnp.inf)).astype(dtype)
    cb = jnp.einsum("bclgn,bcsgn->bclsg", cc, bc)
    m = cb[..., None] * decay_ls
    y_diag = jnp.einsum("bclsge,bcsgep->bclgep", m, xdt)

    decay_to_end = jnp.exp(a_cs[:, :, -1:] - a_cs).astype(dtype)
    states = jnp.einsum("bclgn,bclgep->bcgepn", bc, xdt * decay_to_end[..., None])
    chunk_decay = jnp.exp(a_cs[:, :, -1]).astype(dtype)

    def step(h, inp):
        s_c, d_c = inp
        return h * d_c[..., None, None] + s_c, h

    h0 = jnp.zeros_like(states[:, 0])
    _, prev = lax.scan(step, h0, (jnp.moveaxis(states, 1, 0), jnp.moveaxis(chunk_decay, 1, 0)))
    prev = jnp.moveaxis(prev, 0, 1)

    y_off = jnp.einsum("bclgn,bcgepn->bclgep", cc, prev) * jnp.exp(a_cs).astype(dtype)[..., None]
    return (y_diag + y_off).reshape(bsz, seqlen, D_INNER)


def causal_multiscale_pool(u, mix_w, mix_b, scale):
    bsz, seqlen = u.shape[0], u.shape[1]
    ug = u.reshape(bsz, seqlen, POOL_GROUPS, POOL_GROUP_DIM)
    csum = jnp.cumsum(ug.astype(jnp.float32), axis=1)
    pos = jnp.arange(1, seqlen + 1, dtype=jnp.float32)
    means = []
    for gi, w in enumerate(POOL_WINDOWS):
        cs = csum[:, :, gi]
        lag = jnp.pad(cs, ((0, 0), (w, 0), (0, 0)))[:, :seqlen]
        means.append((cs - lag) / jnp.minimum(pos, float(w))[None, :, None])
    pooled = jnp.stack(means, axis=2).astype(u.dtype) - ug
    mixed = jnp.einsum("bsgc,gcd->bsgd", pooled, mix_w).reshape(bsz, seqlen, D_POOL) + mix_b
    return mixed * scale


def _fwd_setup_inputs(seed: int = 0) -> dict:
    key = jax.random.key(seed)
    ks = jax.random.split(key, 20)
    f32 = jnp.float32
    nrm = lambda k, shape, s: jax.random.normal(k, shape, f32) * s
    dt0 = jnp.exp(jax.random.uniform(ks[6], (DEPTH, SSM_HEADS), f32, np.log(1e-3), np.log(1e-1)))
    return {
        "x": nrm(ks[0], (BATCH, SEQ, D_MODEL), 1.0),
        "p": nrm(ks[1], (DEPTH, BATCH, SEQ, PLE_DIM), 1.0),
        "norm_g": 1.0 + nrm(ks[2], (DEPTH, D_MODEL), 0.05),
        "w_in": nrm(ks[3], (DEPTH, D_MODEL, N_IN), D_MODEL ** -0.5),
        "conv_w": nrm(ks[4], (DEPTH, CONV_WIDTH, D_CONV_CH), CONV_WIDTH ** -0.5),
        "conv_b": nrm(ks[5], (DEPTH, D_CONV_CH), 0.02),
        "dt_bias": dt0 + jnp.log(-jnp.expm1(-dt0)),
        "a_log": jnp.log(jax.random.uniform(ks[7], (DEPTH, SSM_HEADS), f32, 1.0, 16.0)),
        "d_skip": 1.0 + nrm(ks[8], (DEPTH, SSM_HEADS), 0.1),
        "gnorm_g": 1.0 + nrm(ks[9], (DEPTH, D_INNER), 0.05),
        "pool_mix_w": nrm(ks[10], (DEPTH, POOL_GROUPS, POOL_GROUP_DIM, POOL_GROUP_DIM), POOL_GROUP_DIM ** -0.5),
        "pool_mix_b": nrm(ks[11], (DEPTH, D_POOL), 0.02),
        "pool_scale": 1.0 + nrm(ks[12], (DEPTH, D_POOL), 0.1),
        "w_branch_a": nrm(ks[13], (DEPTH, D_INNER, D_MODEL), D_INNER ** -0.5),
        "w_branch_b": nrm(ks[14], (DEPTH, D_POOL, D_MODEL), D_POOL ** -0.5),
        "w_out": nrm(ks[15], (DEPTH, D_MODEL, D_MODEL), D_MODEL ** -0.5),
        "ple_norm_g": 1.0 + nrm(ks[16], (DEPTH, D_MODEL), 0.05),
        "w_ple_gate": nrm(ks[17], (DEPTH, D_MODEL, D_MODEL), D_MODEL ** -0.5),
        "w_ple_up": nrm(ks[18], (DEPTH, PLE_DIM, D_MODEL), PLE_DIM ** -0.5),
        "final_g": 1.0 + nrm(ks[19], (D_MODEL,), 0.05),
    }


def _fwd_reference(x, p, norm_g, w_in, conv_w, conv_b, dt_bias, a_log, d_skip, gnorm_g,
              pool_mix_w, pool_mix_b, pool_scale, w_branch_a, w_branch_b, w_out,
              ple_norm_g, w_ple_gate, w_ple_up, final_g):
    split_idx = [int(v) for v in np.cumsum(IN_SPLITS)[:-1]]
    bsz, seqlen = x.shape[0], x.shape[1]
    for i in range(DEPTH):
        h = rms_norm(x, norm_g[i])
        proj = h @ w_in[i]
        z, xbc, dt_raw, u, z_pool, g_a, g_b = jnp.split(proj, split_idx, axis=-1)

        xbc = jax.nn.silu(causal_depthwise_conv(xbc, conv_w[i], conv_b[i]))
        x_ssm, b_ssm, c_ssm = jnp.split(xbc, [D_INNER, D_INNER + SSM_GROUPS * D_STATE], axis=-1)
        dt = jax.nn.softplus(dt_raw + dt_bias[i])
        a_neg = -jnp.exp(a_log[i])
        y_a = ssd_chunked(x_ssm, dt, a_neg, b_ssm, c_ssm)
        y_a = y_a + x_ssm * jnp.repeat(d_skip[i], SSM_HEAD_DIM)
        y_a = (y_a * jax.nn.silu(z)).reshape(bsz, seqlen, SSM_GROUPS, D_INNER // SSM_GROUPS)
        y_a = rms_norm(y_a, gnorm_g[i].reshape(SSM_GROUPS, D_INNER // SSM_GROUPS)).reshape(bsz, seqlen, D_INNER)

        y_b = causal_multiscale_pool(u, pool_mix_w[i], pool_mix_b[i], pool_scale[i]) * jax.nn.silu(z_pool)

        merged = jax.nn.sigmoid(g_a) * (y_a @ w_branch_a[i]) + jax.nn.sigmoid(g_b) * (y_b @ w_branch_b[i])
        x = x + merged @ w_out[i]

        gate = jax.nn.sigmoid(rms_norm(x, ple_norm_g[i]) @ w_ple_gate[i])
        x = x + gate * (p[i] @ w_ple_up[i])
    return rms_norm(x, final_g)


import jax as _jax
import jax.numpy as _jnp

TWIN_FORMAT = 'train_step'
FWD_PARAMS = ['x', 'p', 'norm_g', 'w_in', 'conv_w', 'conv_b', 'dt_bias', 'a_log', 'd_skip', 'gnorm_g', 'pool_mix_w', 'pool_mix_b', 'pool_scale', 'w_branch_a', 'w_branch_b', 'w_out', 'ple_norm_g', 'w_ple_gate', 'w_ple_up', 'final_g']
TWIN_WEIGHTS = ['norm_g', 'w_in', 'conv_w', 'conv_b', 'dt_bias', 'a_log', 'd_skip', 'gnorm_g', 'pool_mix_w', 'pool_mix_b', 'pool_scale', 'w_branch_a', 'w_branch_b', 'w_out', 'ple_norm_g', 'w_ple_gate', 'w_ple_up', 'final_g']
TWIN_DIFF_INPUT = 'x'
TWIN_INPUTS = ['x', 'p', 'norm_g', 'w_in', 'conv_w', 'conv_b', 'dt_bias', 'a_log', 'd_skip', 'gnorm_g', 'pool_mix_w', 'pool_mix_b', 'pool_scale', 'w_branch_a', 'w_branch_b', 'w_out', 'ple_norm_g', 'w_ple_gate', 'w_ple_up', 'final_g', 'loss_target', 'm_norm_g', 'm_w_in', 'm_conv_w', 'm_conv_b', 'm_dt_bias', 'm_a_log', 'm_d_skip', 'm_gnorm_g', 'm_pool_mix_w', 'm_pool_mix_b', 'm_pool_scale', 'm_w_branch_a', 'm_w_branch_b', 'm_w_out', 'm_ple_norm_g', 'm_w_ple_gate', 'm_w_ple_up', 'm_final_g', 'v_norm_g', 'v_w_in', 'v_conv_w', 'v_conv_b', 'v_dt_bias', 'v_a_log', 'v_d_skip', 'v_gnorm_g', 'v_pool_mix_w', 'v_pool_mix_b', 'v_pool_scale', 'v_w_branch_a', 'v_w_branch_b', 'v_w_out', 'v_ple_norm_g', 'v_w_ple_gate', 'v_w_ple_up', 'v_final_g']
TWIN_OUTPUTS = ['loss', 'grad_x', 'grad_norm_g', 'grad_w_in', 'grad_conv_w', 'grad_conv_b', 'grad_dt_bias', 'grad_a_log', 'grad_d_skip', 'grad_gnorm_g', 'grad_pool_mix_w', 'grad_pool_mix_b', 'grad_pool_scale', 'grad_w_branch_a', 'grad_w_branch_b', 'grad_w_out', 'grad_ple_norm_g', 'grad_w_ple_gate', 'grad_w_ple_up', 'grad_final_g', 'delta_norm_g', 'delta_w_in', 'delta_conv_w', 'delta_conv_b', 'delta_dt_bias', 'delta_a_log', 'delta_d_skip', 'delta_gnorm_g', 'delta_pool_mix_w', 'delta_pool_mix_b', 'delta_pool_scale', 'delta_w_branch_a', 'delta_w_branch_b', 'delta_w_out', 'delta_ple_norm_g', 'delta_w_ple_gate', 'delta_w_ple_up', 'delta_final_g', 'new_m_norm_g', 'new_m_w_in', 'new_m_conv_w', 'new_m_conv_b', 'new_m_dt_bias', 'new_m_a_log', 'new_m_d_skip', 'new_m_gnorm_g', 'new_m_pool_mix_w', 'new_m_pool_mix_b', 'new_m_pool_scale', 'new_m_w_branch_a', 'new_m_w_branch_b', 'new_m_w_out', 'new_m_ple_norm_g', 'new_m_w_ple_gate', 'new_m_w_ple_up', 'new_m_final_g', 'new_v_norm_g', 'new_v_w_in', 'new_v_conv_w', 'new_v_conv_b', 'new_v_dt_bias', 'new_v_a_log', 'new_v_d_skip', 'new_v_gnorm_g', 'new_v_pool_mix_w', 'new_v_pool_mix_b', 'new_v_pool_scale', 'new_v_w_branch_a', 'new_v_w_branch_b', 'new_v_w_out', 'new_v_ple_norm_g', 'new_v_w_ple_gate', 'new_v_w_ple_up', 'new_v_final_g']
TWIN_LEAF_KINDS = {'loss': 'loss', 'grad_x': 'grad_x', 'grad_norm_g': 'grad_w', 'grad_w_in': 'grad_w', 'grad_conv_w': 'grad_w', 'grad_conv_b': 'grad_w', 'grad_dt_bias': 'grad_w', 'grad_a_log': 'grad_w', 'grad_d_skip': 'grad_w', 'grad_gnorm_g': 'grad_w', 'grad_pool_mix_w': 'grad_w', 'grad_pool_mix_b': 'grad_w', 'grad_pool_scale': 'grad_w', 'grad_w_branch_a': 'grad_w', 'grad_w_branch_b': 'grad_w', 'grad_w_out': 'grad_w', 'grad_ple_norm_g': 'grad_w', 'grad_w_ple_gate': 'grad_w', 'grad_w_ple_up': 'grad_w', 'grad_final_g': 'grad_w', 'delta_norm_g': 'delta_w', 'delta_w_in': 'delta_w', 'delta_conv_w': 'delta_w', 'delta_conv_b': 'delta_w', 'delta_dt_bias': 'delta_w', 'delta_a_log': 'delta_w', 'delta_d_skip': 'delta_w', 'delta_gnorm_g': 'delta_w', 'delta_pool_mix_w': 'delta_w', 'delta_pool_mix_b': 'delta_w', 'delta_pool_scale': 'delta_w', 'delta_w_branch_a': 'delta_w', 'delta_w_branch_b': 'delta_w', 'delta_w_out': 'delta_w', 'delta_ple_norm_g': 'delta_w', 'delta_w_ple_gate': 'delta_w', 'delta_w_ple_up': 'delta_w', 'delta_final_g': 'delta_w', 'new_m_norm_g': 'new_m', 'new_m_w_in': 'new_m', 'new_m_conv_w': 'new_m', 'new_m_conv_b': 'new_m', 'new_m_dt_bias': 'new_m', 'new_m_a_log': 'new_m', 'new_m_d_skip': 'new_m', 'new_m_gnorm_g': 'new_m', 'new_m_pool_mix_w': 'new_m', 'new_m_pool_mix_b': 'new_m', 'new_m_pool_scale': 'new_m', 'new_m_w_branch_a': 'new_m', 'new_m_w_branch_b': 'new_m', 'new_m_w_out': 'new_m', 'new_m_ple_norm_g': 'new_m', 'new_m_w_ple_gate': 'new_m', 'new_m_w_ple_up': 'new_m', 'new_m_final_g': 'new_m', 'new_v_norm_g': 'new_v', 'new_v_w_in': 'new_v', 'new_v_conv_w': 'new_v', 'new_v_conv_b': 'new_v', 'new_v_dt_bias': 'new_v', 'new_v_a_log': 'new_v', 'new_v_d_skip': 'new_v', 'new_v_gnorm_g': 'new_v', 'new_v_pool_mix_w': 'new_v', 'new_v_pool_mix_b': 'new_v', 'new_v_pool_scale': 'new_v', 'new_v_w_branch_a': 'new_v', 'new_v_w_branch_b': 'new_v', 'new_v_w_out': 'new_v', 'new_v_ple_norm_g': 'new_v', 'new_v_w_ple_gate': 'new_v', 'new_v_w_ple_up': 'new_v', 'new_v_final_g': 'new_v'}


def _forward(args):
    return _fwd_reference(*[args[k] for k in FWD_PARAMS])


def _output_shape():
    out = _jax.eval_shape(lambda: _forward(_fwd_setup_inputs(0)))
    return out.shape, out.dtype

N_MICROBATCH = 1
ADAM_LR = 0.001
ADAM_B1 = 0.9
ADAM_B2 = 0.999
ADAM_EPS = 1e-08
ADAM_WD = 0.01
ADAM_STEP = 10
PER_EXAMPLE_BATCH_AXIS = {'x': 0, 'p': 1, 'loss_target': 0}
SHARED_INPUTS = []
_WEIGHT_DTYPES = {'norm_g': _jnp.float32, 'w_in': _jnp.float32, 'conv_w': _jnp.float32, 'conv_b': _jnp.float32, 'dt_bias': _jnp.float32, 'a_log': _jnp.float32, 'd_skip': _jnp.float32, 'gnorm_g': _jnp.float32, 'pool_mix_w': _jnp.float32, 'pool_mix_b': _jnp.float32, 'pool_scale': _jnp.float32, 'w_branch_a': _jnp.float32, 'w_branch_b': _jnp.float32, 'w_out': _jnp.float32, 'ple_norm_g': _jnp.float32, 'w_ple_gate': _jnp.float32, 'w_ple_up': _jnp.float32, 'final_g': _jnp.float32}
MOMENT_SCALE = {'norm_g': 1.361300e-01, 'w_in': 4.413780e-02, 'conv_w': 4.727915e-02, 'conv_b': 6.923046e-02, 'dt_bias': 2.224653e-01, 'a_log': 1.563926e-01, 'd_skip': 5.614121e-01, 'gnorm_g': 5.587268e-02, 'pool_mix_w': 4.018446e-02, 'pool_mix_b': 4.726291e-02, 'pool_scale': 3.958037e-02, 'w_branch_a': 7.692151e-02, 'w_branch_b': 4.031514e-02, 'w_out': 8.741512e-02, 'ple_norm_g': 3.424866e-02, 'w_ple_gate': 3.050777e-02, 'w_ple_up': 7.604520e-02, 'final_g': 3.199328e+01}


def _to_microbatches(a, axis):
    t = _jnp.moveaxis(a, axis, 0)
    t = t.reshape((N_MICROBATCH, t.shape[0] // N_MICROBATCH) + t.shape[1:])
    return _jnp.moveaxis(t, 1, axis + 1)


def setup_inputs(seed: int = 0) -> dict:
    inp = _fwd_setup_inputs(seed)
    key = _jax.random.fold_in(_jax.random.key(seed), 7919)
    shape, _ = _output_shape()
    out = dict(inp)
    out["loss_target"] = _jax.random.normal(_jax.random.fold_in(key, 0), shape, _jnp.float32)
    for i, name in enumerate(TWIN_WEIGHTS):
        w = inp[name].astype(_jnp.float32)
        if MOMENT_SCALE is None:
            s = _jnp.sqrt(_jnp.mean(_jnp.square(w)) + 1e-30)
        else:
            s = MOMENT_SCALE[name]
        km, kv = _jax.random.split(_jax.random.fold_in(key, i + 1))
        out[name] = w
        out["m_" + name] = s * _jax.random.normal(km, w.shape, _jnp.float32)
        out["v_" + name] = (s * s) * _jax.random.uniform(kv, w.shape, _jnp.float32, 0.5, 1.5)
    if N_MICROBATCH > 1:
        for name, axis in PER_EXAMPLE_BATCH_AXIS.items():
            out[name] = _to_microbatches(out[name], axis)
    return {'x': out['x'], 'p': out['p'], 'norm_g': out['norm_g'], 'w_in': out['w_in'], 'conv_w': out['conv_w'], 'conv_b': out['conv_b'], 'dt_bias': out['dt_bias'], 'a_log': out['a_log'], 'd_skip': out['d_skip'], 'gnorm_g': out['gnorm_g'], 'pool_mix_w': out['pool_mix_w'], 'pool_mix_b': out['pool_mix_b'], 'pool_scale': out['pool_scale'], 'w_branch_a': out['w_branch_a'], 'w_branch_b': out['w_branch_b'], 'w_out': out['w_out'], 'ple_norm_g': out['ple_norm_g'], 'w_ple_gate': out['w_ple_gate'], 'w_ple_up': out['w_ple_up'], 'final_g': out['final_g'], 'loss_target': out['loss_target'], 'm_norm_g': out['m_norm_g'], 'm_w_in': out['m_w_in'], 'm_conv_w': out['m_conv_w'], 'm_conv_b': out['m_conv_b'], 'm_dt_bias': out['m_dt_bias'], 'm_a_log': out['m_a_log'], 'm_d_skip': out['m_d_skip'], 'm_gnorm_g': out['m_gnorm_g'], 'm_pool_mix_w': out['m_pool_mix_w'], 'm_pool_mix_b': out['m_pool_mix_b'], 'm_pool_scale': out['m_pool_scale'], 'm_w_branch_a': out['m_w_branch_a'], 'm_w_branch_b': out['m_w_branch_b'], 'm_w_out': out['m_w_out'], 'm_ple_norm_g': out['m_ple_norm_g'], 'm_w_ple_gate': out['m_w_ple_gate'], 'm_w_ple_up': out['m_w_ple_up'], 'm_final_g': out['m_final_g'], 'v_norm_g': out['v_norm_g'], 'v_w_in': out['v_w_in'], 'v_conv_w': out['v_conv_w'], 'v_conv_b': out['v_conv_b'], 'v_dt_bias': out['v_dt_bias'], 'v_a_log': out['v_a_log'], 'v_d_skip': out['v_d_skip'], 'v_gnorm_g': out['v_gnorm_g'], 'v_pool_mix_w': out['v_pool_mix_w'], 'v_pool_mix_b': out['v_pool_mix_b'], 'v_pool_scale': out['v_pool_scale'], 'v_w_branch_a': out['v_w_branch_a'], 'v_w_branch_b': out['v_w_branch_b'], 'v_w_out': out['v_w_out'], 'v_ple_norm_g': out['v_ple_norm_g'], 'v_w_ple_gate': out['v_w_ple_gate'], 'v_w_ple_up': out['v_w_ple_up'], 'v_final_g': out['v_final_g']}


def _loss(weights, diff, rest, loss_target):
    with _jax.named_scope("forward"):
        args = {**rest, TWIN_DIFF_INPUT: diff, **{k: w.astype(_WEIGHT_DTYPES[k]) for k, w in weights.items()}}
        y = _forward(args)
    with _jax.named_scope("loss_head"):
        err = _jnp.square(y.astype(_jnp.float32) - loss_target)
        return 0.5 * _jnp.sum(_jnp.mean(err, axis=-1)) if err.ndim else 0.5 * err


def _adamw(w, g, m, v):
    m = ADAM_B1 * m + (1.0 - ADAM_B1) * g
    v = ADAM_B2 * v + (1.0 - ADAM_B2) * _jnp.square(g)
    m_hat = m / (1.0 - ADAM_B1 ** ADAM_STEP)
    v_hat = v / (1.0 - ADAM_B2 ** ADAM_STEP)
    delta = -ADAM_LR * (m_hat / (_jnp.sqrt(v_hat) + ADAM_EPS) + ADAM_WD * w)
    return delta, m, v


def reference(x, p, norm_g, w_in, conv_w, conv_b, dt_bias, a_log, d_skip, gnorm_g, pool_mix_w, pool_mix_b, pool_scale, w_branch_a, w_branch_b, w_out, ple_norm_g, w_ple_gate, w_ple_up, final_g, loss_target, m_norm_g, m_w_in, m_conv_w, m_conv_b, m_dt_bias, m_a_log, m_d_skip, m_gnorm_g, m_pool_mix_w, m_pool_mix_b, m_pool_scale, m_w_branch_a, m_w_branch_b, m_w_out, m_ple_norm_g, m_w_ple_gate, m_w_ple_up, m_final_g, v_norm_g, v_w_in, v_conv_w, v_conv_b, v_dt_bias, v_a_log, v_d_skip, v_gnorm_g, v_pool_mix_w, v_pool_mix_b, v_pool_scale, v_w_branch_a, v_w_branch_b, v_w_out, v_ple_norm_g, v_w_ple_gate, v_w_ple_up, v_final_g):
    given = dict(x=x, p=p, norm_g=norm_g, w_in=w_in, conv_w=conv_w, conv_b=conv_b, dt_bias=dt_bias, a_log=a_log, d_skip=d_skip, gnorm_g=gnorm_g, pool_mix_w=pool_mix_w, pool_mix_b=pool_mix_b, pool_scale=pool_scale, w_branch_a=w_branch_a, w_branch_b=w_branch_b, w_out=w_out, ple_norm_g=ple_norm_g, w_ple_gate=w_ple_gate, w_ple_up=w_ple_up, final_g=final_g, loss_target=loss_target, m_norm_g=m_norm_g, m_w_in=m_w_in, m_conv_w=m_conv_w, m_conv_b=m_conv_b, m_dt_bias=m_dt_bias, m_a_log=m_a_log, m_d_skip=m_d_skip, m_gnorm_g=m_gnorm_g, m_pool_mix_w=m_pool_mix_w, m_pool_mix_b=m_pool_mix_b, m_pool_scale=m_pool_scale, m_w_branch_a=m_w_branch_a, m_w_branch_b=m_w_branch_b, m_w_out=m_w_out, m_ple_norm_g=m_ple_norm_g, m_w_ple_gate=m_w_ple_gate, m_w_ple_up=m_w_ple_up, m_final_g=m_final_g, v_norm_g=v_norm_g, v_w_in=v_w_in, v_conv_w=v_conv_w, v_conv_b=v_conv_b, v_dt_bias=v_dt_bias, v_a_log=v_a_log, v_d_skip=v_d_skip, v_gnorm_g=v_gnorm_g, v_pool_mix_w=v_pool_mix_w, v_pool_mix_b=v_pool_mix_b, v_pool_scale=v_pool_scale, v_w_branch_a=v_w_branch_a, v_w_branch_b=v_w_branch_b, v_w_out=v_w_out, v_ple_norm_g=v_ple_norm_g, v_w_ple_gate=v_w_ple_gate, v_w_ple_up=v_w_ple_up, v_final_g=v_final_g)
    weights = {n: given[n] for n in TWIN_WEIGHTS}
    shared = {n: given[n] for n in SHARED_INPUTS}
    per_example = {n: given[n] for n in ['x', 'p']}
    grad_fn = _jax.value_and_grad(_loss, argnums=(0, 1))

    def one_microbatch(ex, loss_target):
        ex = dict(ex)
        diff = ex.pop(TWIN_DIFF_INPUT)
        return grad_fn(weights, diff, {**shared, **ex}, loss_target)

    if N_MICROBATCH == 1:
        loss, (grad_w, grad_x) = one_microbatch(per_example, given["loss_target"])
    else:
        def body(carry, xs):
            loss_sum, grad_sum = carry
            l_k, (gw_k, gx_k) = one_microbatch(xs[0], xs[1])
            with _jax.named_scope("update"):
                return (loss_sum + l_k, _jax.tree.map(_jnp.add, grad_sum, gw_k)), gx_k

        init = (_jnp.zeros((), _jnp.float32), _jax.tree.map(_jnp.zeros_like, weights))
        (loss, grad_w), grad_x = _jax.lax.scan(body, init, (per_example, given["loss_target"]))
    with _jax.named_scope("update"):
        delta_w, new_m, new_v = {}, {}, {}
        for n in TWIN_WEIGHTS:
            delta_w[n], new_m[n], new_v[n] = _adamw(weights[n], grad_w[n], given["m_" + n], given["v_" + n])
    return (loss, grad_x, *[grad_w[n] for n in TWIN_WEIGHTS], *[delta_w[n] for n in TWIN_WEIGHTS],
            *[new_m[n] for n in TWIN_WEIGHTS], *[new_v[n] for n in TWIN_WEIGHTS])
```

```python
import functools

import jax
import jax.numpy as jnp
from jax import lax
from jax.experimental import pallas as pl
from jax.experimental.pallas import tpu as pltpu

F32 = jnp.float32
BF16 = jnp.bfloat16
HIGHEST = lax.Precision.HIGHEST
MESH = pl.DeviceIdType.MESH

EPS = 1e-6
HEAD_DIM = 64
SSM_GROUPS = 4
D_STATE = 128
CONV_WIDTH = 4
CHUNK = 128
N_POOL = 4
LANES = 128
N_SHARD = 4

ADAM_LR = 0.001
ADAM_B1 = 0.9
ADAM_B2 = 0.999
ADAM_EPS = 1e-08
ADAM_WD = 0.01
ADAM_STEP = 10

SHARDED = ("w_in", "conv_w", "pool_mix_w", "w_branch_a", "w_branch_b", "w_out", "w_ple_gate", "w_ple_up")
SMALL = ("norm_g", "conv_b", "dt_bias", "a_log", "d_skip", "gnorm_g", "pool_mix_b", "pool_scale",
         "ple_norm_g", "final_g")
WEIGHTS = ("norm_g", "w_in", "conv_w", "conv_b", "dt_bias", "a_log", "d_skip", "gnorm_g", "pool_mix_w",
           "pool_mix_b", "pool_scale", "w_branch_a", "w_branch_b", "w_out", "ple_norm_g", "w_ple_gate",
           "w_ple_up", "final_g")


def _params(sem=None, vmem_mb=48):
    kw = dict(vmem_limit_bytes=vmem_mb << 20)
    if sem is not None:
        kw["dimension_semantics"] = sem
    return pltpu.CompilerParams(**kw)


def _sigmoid(v):
    return 1.0 / (1.0 + jnp.exp(-v))


def _softplus(v):
    return jnp.maximum(v, 0.0) + jnp.log1p(jnp.exp(-jnp.abs(v)))


def _bdot(a, b):
    return jnp.dot(a.astype(BF16), b.astype(BF16), preferred_element_type=F32)


def _bdot_nt(a, b):
    return lax.dot_general(a.astype(BF16), b.astype(BF16), (((1,), (1,)), ((), ())), preferred_element_type=F32)


def _bdot_tn(a, b):
    return lax.dot_general(a.astype(BF16), b.astype(BF16), (((0,), (0,)), ((), ())), preferred_element_type=F32)


def _col_block(col0, width):
    assert col0 % width == 0, (col0, width)
    return col0 // width


def _shift_down(v, j, row):
    return jnp.where(row >= j, pltpu.roll(v, j, 0), 0.0)


def _shift_up(v, j, row):
    n = v.shape[0]
    return jnp.where(row < n - j, pltpu.roll(v, n - j, 0), 0.0)


def _mm(a, w, name, a_blk=0, tm=512, tn=512, out_dtype=F32):
    t = a.shape[0]
    k, n = w.shape
    tm, tn = min(tm, t), min(tn, n)

    def body(a_ref, w_ref, o_ref):
        o_ref[...] = _bdot(a_ref[...], w_ref[...]).astype(o_ref.dtype)

    return pl.pallas_call(
        body, name=name, grid=(t // tm, n // tn),
        in_specs=[pl.BlockSpec((tm, k), lambda i, j: (i, a_blk)), pl.BlockSpec((k, tn), lambda i, j: (0, j))],
        out_specs=pl.BlockSpec((tm, tn), lambda i, j: (i, j)),
        out_shape=jax.ShapeDtypeStruct((t, n), out_dtype),
        compiler_params=_params(("parallel", "parallel")))(a, w)


def _mm_nt(a, w, name, tm=512, tk=1024, out_dtype=F32):
    t, k = a.shape
    n = w.shape[0]
    tm, tk = min(tm, t), min(tk, k)

    def body(a_ref, w_ref, o_ref):
        kk = pl.program_id(1)
        part = _bdot_nt(a_ref[...], w_ref[...])

        @pl.when(kk == 0)
        def _():
            o_ref[...] = part

        @pl.when(kk > 0)
        def _():
            o_ref[...] += part

    return pl.pallas_call(
        body, name=name, grid=(t // tm, k // tk),
        in_specs=[pl.BlockSpec((tm, tk), lambda i, j: (i, j)), pl.BlockSpec((n, tk), lambda i, j: (0, j))],
        out_specs=pl.BlockSpec((tm, n), lambda i, j: (i, 0)),
        out_shape=jax.ShapeDtypeStruct((t, n), out_dtype),
        compiler_params=_params(("parallel", "arbitrary")))(a, w)


def _mm_tn(a, b, name, tn=512, tk=512):
    t, m = a.shape
    n = b.shape[1]
    tn, tk = min(tn, n), min(tk, t)

    def body(a_ref, b_ref, o_ref):
        kk = pl.program_id(1)
        part = _bdot_tn(a_ref[...], b_ref[...])

        @pl.when(kk == 0)
        def _():
            o_ref[...] = part

        @pl.when(kk > 0)
        def _():
            o_ref[...] += part

    return pl.pallas_call(
        body, name=name, grid=(n // tn, t // tk),
        in_specs=[pl.BlockSpec((tk, m), lambda j, kk: (kk, 0)), pl.BlockSpec((tk, tn), lambda j, kk: (kk, j))],
        out_specs=pl.BlockSpec((m, tn), lambda j, kk: (0, j)),
        out_shape=jax.ShapeDtypeStruct((m, n), F32),
        compiler_params=_params(("parallel", "arbitrary")))(a, b)


def _inproj(x2, norm_g, w_main, w_dt, tm=512, tn=512):
    t, d = x2.shape
    n = w_main.shape[1]
    tm, tn = min(tm, t), min(tn, n)

    def body(x_ref, g_ref, w_ref, wdt_ref, proj_ref, dt_ref, h_ref):
        @pl.when(pl.program_id(1) == 0)
        def _():
            xv = x_ref[...]
            r = lax.rsqrt(jnp.mean(xv * xv, axis=-1, keepdims=True) + EPS)
            h = (xv * r * g_ref[...]).astype(BF16)
            h_ref[...] = h
            dt_ref[...] = jnp.dot(h, wdt_ref[...], preferred_element_type=F32)

        proj_ref[...] = jnp.dot(h_ref[...], w_ref[...], preferred_element_type=F32)

    return pl.pallas_call(
        body, name="inproj", grid=(t // tm, n // tn),
        in_specs=[pl.BlockSpec((tm, d), lambda i, j: (i, 0)), pl.BlockSpec((1, d), lambda i, j: (0, 0)),
                  pl.BlockSpec((d, tn), lambda i, j: (0, j)), pl.BlockSpec((d, LANES), lambda i, j: (0, 0))],
        out_specs=[pl.BlockSpec((tm, tn), lambda i, j: (i, j)), pl.BlockSpec((tm, LANES), lambda i, j: (i, 0)),
                   pl.BlockSpec((tm, d), lambda i, j: (i, 0))],
        out_shape=[jax.ShapeDtypeStruct((t, n), F32), jax.ShapeDtypeStruct((t, LANES), F32),
                   jax.ShapeDtypeStruct((t, d), BF16)],
        compiler_params=_params(("parallel", "arbitrary")))(x2, norm_g, w_main, w_dt)


def _conv_fwd(proj, conv_w, conv_b, nb, s, col0, cb=512):
    dc = conv_w.shape[1]
    cb = min(cb, dc)
    blk0 = _col_block(col0, cb)

    def body(x_ref, w_ref, b_ref, o_ref):
        v = x_ref[...]
        row = lax.broadcasted_iota(jnp.int32, v.shape, 0)
        acc = v * w_ref[CONV_WIDTH - 1:CONV_WIDTH, :] + b_ref[...]
        for j in range(1, CONV_WIDTH):
            acc = acc + _shift_down(v, j, row) * w_ref[CONV_WIDTH - 1 - j:CONV_WIDTH - j, :]
        o_ref[...] = acc * _sigmoid(acc)

    return pl.pallas_call(
        body, name="conv_fwd", grid=(nb, dc // cb),
        in_specs=[pl.BlockSpec((s, cb), lambda b, j: (b, blk0 + j)), pl.BlockSpec((CONV_WIDTH, cb), lambda b, j: (0, j)),
                  pl.BlockSpec((1, cb), lambda b, j: (0, j))],
        out_specs=pl.BlockSpec((s, cb), lambda b, j: (b, j)),
        out_shape=jax.ShapeDtypeStruct((nb * s, dc), F32),
        compiler_params=_params(("parallel", "parallel")))(proj, conv_w, conv_b)


def _ssd_consts():
    r = lax.broadcasted_iota(jnp.int32, (CHUNK, CHUNK), 0)
    c = lax.broadcasted_iota(jnp.int32, (CHUNK, CHUNK), 1)
    return (r >= c).astype(F32)


def _ssd_fwd(proj, xbc, dtraw, dtb, alog, dsk, gn, nb, s, di, z_col0):
    t = nb * s
    nc = s // CHUNK
    nh = di // HEAD_DIM
    hpg = nh // SSM_GROUPS
    gw = di // SSM_GROUPS
    gn_w = SSM_GROUPS * D_STATE
    b_blk = _col_block(di, gn_w)
    z_blk = _col_block(z_col0, di)
    L, P, N = CHUNK, HEAD_DIM, D_STATE
    tril = _ssd_consts()

    def body(z_ref, x_ref, b_ref, c_ref, dtr_ref, dtb_ref, alog_ref, dsk_ref, gn_ref, tril_ref,
             ypre_ref, yan_ref, hp_ref, st_ref, yb_ref):
        @pl.when(pl.program_id(1) == 0)
        def _():
            st_ref[...] = jnp.zeros_like(st_ref)

        hp_ref[0] = st_ref[...]
        dt = _softplus(dtr_ref[...] + dtb_ref[...])
        a_dt = dt * (-jnp.exp(alog_ref[...]))
        tri = tril_ref[...]
        a_cs = jnp.dot(tri, a_dt, precision=HIGHEST, preferred_element_type=F32)
        a_cst = lax.dot_general(a_dt, tri, (((0,), (1,)), ((), ())), precision=HIGHEST, preferred_element_type=F32)
        lower = tri > 0.5
        for g in range(SSM_GROUPS):
            bg = b_ref[:, g * N:(g + 1) * N].astype(BF16)
            cg = c_ref[:, g * N:(g + 1) * N].astype(BF16)
            gm = _bdot_nt(cg, bg)
            for e in range(hpg):
                h = g * hpg + e
                hs = slice(h * P, (h + 1) * P)
                a_col = a_cs[:, h:h + 1]
                decay = jnp.where(lower, jnp.exp(a_col - a_cst[h:h + 1, :]), 0.0)
                xh = x_ref[:, hs]
                xdt = xh * dt[:, h:h + 1]
                hprev = st_ref[hs, :]
                yd = _bdot(gm * decay, xdt)
                yo = _bdot_nt(cg, hprev) * jnp.exp(a_col)
                a_last = a_cs[L - 1:L, h:h + 1]
                st_ref[hs, :] = hprev * jnp.exp(a_last) + _bdot_tn(xdt * jnp.exp(a_last - a_col), bg)
                yb_ref[:, hs] = yd + yo + xh * dsk_ref[:, h:h + 1]
        ypre = yb_ref[...]
        ypre_ref[...] = ypre
        zv = z_ref[...]
        v = ypre * zv * _sigmoid(zv)
        for g in range(SSM_GROUPS):
            gs = slice(g * gw, (g + 1) * gw)
            vg = v[:, gs]
            r = lax.rsqrt(jnp.mean(vg * vg, axis=-1, keepdims=True) + EPS)
            yan_ref[:, gs] = vg * r * gn_ref[:, gs]

    row = lambda b, c: b * nc + c
    vec = lambda w: pl.BlockSpec((1, w), lambda b, c: (0, 0))
    return pl.pallas_call(
        body, name="ssd_fwd", grid=(nb, nc),
        in_specs=[pl.BlockSpec((L, di), lambda b, c: (row(b, c), z_blk)),
                  pl.BlockSpec((L, di), lambda b, c: (row(b, c), 0)),
                  pl.BlockSpec((L, gn_w), lambda b, c: (row(b, c), b_blk)),
                  pl.BlockSpec((L, gn_w), lambda b, c: (row(b, c), b_blk + 1)),
                  pl.BlockSpec((L, LANES), lambda b, c: (row(b, c), 0)),
                  vec(LANES), vec(LANES), vec(LANES), vec(di),
                  pl.BlockSpec((L, L), lambda b, c: (0, 0))],
        out_specs=[pl.BlockSpec((L, di), lambda b, c: (row(b, c), 0)),
                   pl.BlockSpec((L, di), lambda b, c: (row(b, c), 0)),
                   pl.BlockSpec((1, nh * P, N), lambda b, c: (row(b, c), 0, 0))],
        out_shape=[jax.ShapeDtypeStruct((t, di), F32), jax.ShapeDtypeStruct((t, di), F32),
                   jax.ShapeDtypeStruct((nb * nc, nh * P, N), F32)],
        scratch_shapes=[pltpu.VMEM((nh * P, N), F32), pltpu.VMEM((L, di), F32)],
        compiler_params=_params(("parallel", "arbitrary")))(proj, xbc, xbc, xbc, dtraw, dtb, alog, dsk, gn, tril)


def _pool_sum(v, g, row, shift):
    s2 = v + shift(v, 1, row)
    s4 = s2 + shift(s2, 2, row)
    s8 = s4 + shift(s4, 4, row)
    s16 = s8 + shift(s8, 8, row)
    return jnp.where(g == 0, s2, jnp.where(g == 1, s4, jnp.where(g == 2, s8, s16)))


def _pool_count(g, row):
    return jnp.minimum(row + 1, jnp.left_shift(2, g)).astype(F32)


def _pool_fwd(proj, mix_w, mix_b, scale, nb, s, col0):
    pgd = mix_w.shape[-1]
    blk0 = _col_block(col0, 2 * pgd)

    def body(uz_ref, w_ref, b_ref, sc_ref, o_ref):
        g = pl.program_id(1)
        u = uz_ref[:, :pgd]
        zp = uz_ref[:, pgd:]
        row = lax.broadcasted_iota(jnp.int32, u.shape, 0)
        pooled = _pool_sum(u, g, row, _shift_down) / _pool_count(g, row) - u
        mixed = _bdot(pooled, w_ref[0]) + b_ref[...]
        o_ref[...] = mixed * sc_ref[...] * zp * _sigmoid(zp)

    return pl.pallas_call(
        body, name="pool_fwd", grid=(nb, N_POOL),
        in_specs=[pl.BlockSpec((s, 2 * pgd), lambda b, g: (b, blk0 + g)),
                  pl.BlockSpec((1, pgd, pgd), lambda b, g: (g, 0, 0)),
                  pl.BlockSpec((1, pgd), lambda b, g: (0, g)), pl.BlockSpec((1, pgd), lambda b, g: (0, g))],
        out_specs=pl.BlockSpec((s, pgd), lambda b, g: (b, g)),
        out_shape=jax.ShapeDtypeStruct((nb * s, N_POOL * pgd), F32),
        compiler_params=_params(("parallel", "parallel")))(proj, mix_w, mix_b, scale)


def _merge_fwd(ya, yb, proj, col0, tm=512):
    t, d = ya.shape
    tm = min(tm, t)
    blk = _col_block(col0, 2 * d)

    def body(ya_ref, yb_ref, g_ref, o_ref):
        o_ref[...] = _sigmoid(g_ref[:, :d]) * ya_ref[...] + _sigmoid(g_ref[:, d:]) * yb_ref[...]

    row = pl.BlockSpec((tm, d), lambda i: (i, 0))
    return pl.pallas_call(
        body, name="merge_fwd", grid=(t // tm,),
        in_specs=[row, row, pl.BlockSpec((tm, 2 * d), lambda i: (i, blk))],
        out_specs=row, out_shape=jax.ShapeDtypeStruct((t, d), F32),
        compiler_params=_params(("parallel",)))(ya, yb, proj)


def _ple_pre(x2, mo, ple_g, tm=512):
    t, d = x2.shape
    tm = min(tm, t)

    def body(x_ref, mo_ref, g_ref, x1_ref, hn_ref):
        x1 = x_ref[...] + mo_ref[...]
        x1_ref[...] = x1
        r = lax.rsqrt(jnp.mean(x1 * x1, axis=-1, keepdims=True) + EPS)
        hn_ref[...] = (x1 * r * g_ref[...]).astype(BF16)

    row = pl.BlockSpec((tm, d), lambda i: (i, 0))
    return pl.pallas_call(
        body, name="ple_pre", grid=(t // tm,),
        in_specs=[row, row, pl.BlockSpec((1, d), lambda i: (0, 0))],
        out_specs=[row, row],
        out_shape=[jax.ShapeDtypeStruct((t, d), F32), jax.ShapeDtypeStruct((t, d), BF16)],
        compiler_params=_params(("parallel",)))(x2, mo, ple_g)


def _tail(x1, pre, pu, tgt, final_g, tm=512):
    t, d = x1.shape
    tm = min(tm, t)

    def body(x1_ref, pre_ref, pu_ref, tgt_ref, g_ref, dx2_ref, dpre_ref, dpu_ref, loss_ref, dg_ref):
        @pl.when(pl.program_id(0) == 0)
        def _():
            loss_ref[...] = jnp.zeros_like(loss_ref)
            dg_ref[...] = jnp.zeros_like(dg_ref)

        gate = _sigmoid(pre_ref[...])
        pu = pu_ref[...]
        x2 = x1_ref[...] + gate * pu
        r = lax.rsqrt(jnp.mean(x2 * x2, axis=-1, keepdims=True) + EPS)
        xn = x2 * r
        fg = g_ref[...]
        err = xn * fg - tgt_ref[...]
        loss_ref[...] += 0.5 * jnp.sum(jnp.mean(err * err, axis=-1, keepdims=True))
        dy = err * (1.0 / d)
        dg_ref[...] += jnp.sum(dy * xn, axis=0, keepdims=True)
        dxn = dy * fg
        dx2 = r * (dxn - xn * jnp.mean(dxn * xn, axis=-1, keepdims=True))
        dx2_ref[...] = dx2
        dpre_ref[...] = dx2 * pu * gate * (1.0 - gate)
        dpu_ref[...] = dx2 * gate

    row = pl.BlockSpec((tm, d), lambda i: (i, 0))
    return pl.pallas_call(
        body, name="tail", grid=(t // tm,),
        in_specs=[row, row, row, row, pl.BlockSpec((1, d), lambda i: (0, 0))],
        out_specs=[row, row, row, pl.BlockSpec((1, LANES), lambda i: (0, 0)), pl.BlockSpec((1, d), lambda i: (0, 0))],
        out_shape=[jax.ShapeDtypeStruct((t, d), F32)] * 3 + [jax.ShapeDtypeStruct((1, LANES), F32),
                                                             jax.ShapeDtypeStruct((1, d), F32)],
        compiler_params=_params(("arbitrary",)))(x1, pre, pu, tgt, final_g)


def _rms_bwd(xin, dhs, dres, g, name, tm=512):
    t, d = xin.shape
    tm = min(tm, t)
    n_dh = len(dhs)

    def body(*refs):
        x_ref, dh_refs, dres_ref, g_ref, dx_ref, dg_ref = refs[0], refs[1:1 + n_dh], *refs[1 + n_dh:]

        @pl.when(pl.program_id(0) == 0)
        def _():
            dg_ref[...] = jnp.zeros_like(dg_ref)

        xv = x_ref[...]
        dh = dh_refs[0][...]
        for ref in dh_refs[1:]:
            dh = dh + ref[...]
        r = lax.rsqrt(jnp.mean(xv * xv, axis=-1, keepdims=True) + EPS)
        xn = xv * r
        dg_ref[...] += jnp.sum(dh * xn, axis=0, keepdims=True)
        dd = dh * g_ref[...]
        dx_ref[...] = dres_ref[...] + r * (dd - xn * jnp.mean(dd * xn, axis=-1, keepdims=True))

    row = pl.BlockSpec((tm, d), lambda i: (i, 0))
    vec = pl.BlockSpec((1, d), lambda i: (0, 0))
    return pl.pallas_call(
        body, name=name, grid=(t // tm,),
        in_specs=[row] * (2 + n_dh) + [vec],
        out_specs=[row, vec],
        out_shape=[jax.ShapeDtypeStruct((t, d), F32), jax.ShapeDtypeStruct((1, d), F32)],
        compiler_params=_params(("arbitrary",)))(xin, *dhs, dres, g)


def _merge_bwd(dm, ya, yb, proj, col0, n_cols, tm=512):
    t, d = ya.shape
    tm = min(tm, t)
    blk = _col_block(col0, 2 * d)

    def body(dm_ref, ya_ref, yb_ref, g_ref, dya_ref, dyb_ref, dg_ref):
        dm_v = dm_ref[...]
        sa = _sigmoid(g_ref[:, :d])
        sb = _sigmoid(g_ref[:, d:])
        dya_ref[...] = dm_v * sa
        dyb_ref[...] = dm_v * sb
        dg_ref[:, :d] = dm_v * ya_ref[...] * sa * (1.0 - sa)
        dg_ref[:, d:] = dm_v * yb_ref[...] * sb * (1.0 - sb)

    row = pl.BlockSpec((tm, d), lambda i: (i, 0))
    gspec = pl.BlockSpec((tm, 2 * d), lambda i: (i, blk))
    return pl.pallas_call(
        body, name="merge_bwd", grid=(t // tm,),
        in_specs=[row, row, row, gspec],
        out_specs=[row, row, gspec],
        out_shape=[jax.ShapeDtypeStruct((t, d), F32), jax.ShapeDtypeStruct((t, d), F32),
                   jax.ShapeDtypeStruct((t, n_cols), F32)],
        compiler_params=_params(("parallel",)))(dm, ya, yb, proj)


def _pool_bwd(proj, dyb, dproj, mix_w, mix_b, scale, nb, s, col0):
    pgd = mix_w.shape[-1]
    blk0 = _col_block(col0, 2 * pgd)

    def body(uz_ref, dy_ref, _, w_ref, b_ref, sc_ref, duz_ref, dw_ref, db_ref, dsc_ref):
        g = pl.program_id(0)

        @pl.when(pl.program_id(1) == 0)
        def _():
            dw_ref[...] = jnp.zeros_like(dw_ref)
            db_ref[...] = jnp.zeros_like(db_ref)
            dsc_ref[...] = jnp.zeros_like(dsc_ref)

        u = uz_ref[:, :pgd]
        zp = uz_ref[:, pgd:]
        row = lax.broadcasted_iota(jnp.int32, u.shape, 0)
        cnt = _pool_count(g, row)
        pooled = _pool_sum(u, g, row, _shift_down) / cnt - u
        wv = w_ref[0]
        mixed = _bdot(pooled, wv) + b_ref[...]
        sg = _sigmoid(zp)
        sz = zp * sg
        dy = dy_ref[...]
        sc = sc_ref[...]
        dsc_ref[...] += jnp.sum(dy * mixed * sz, axis=0, keepdims=True)
        dmixed = dy * sc * sz
        db_ref[...] += jnp.sum(dmixed, axis=0, keepdims=True)
        dw_ref[0] += _bdot_tn(pooled, dmixed)
        dpooled = _bdot_nt(dmixed, wv)
        duz_ref[:, :pgd] = _pool_sum(dpooled / cnt, g, row, _shift_up) - dpooled
        duz_ref[:, pgd:] = dy * mixed * sc * sg * (1.0 + zp * (1.0 - sg))

    uz = pl.BlockSpec((s, 2 * pgd), lambda g, b: (b, blk0 + g))
    vec = pl.BlockSpec((1, pgd), lambda g, b: (0, g))
    wspec = pl.BlockSpec((1, pgd, pgd), lambda g, b: (g, 0, 0))
    return pl.pallas_call(
        body, name="pool_bwd", grid=(N_POOL, nb),
        in_specs=[uz, pl.BlockSpec((s, pgd), lambda g, b: (b, g)), pl.BlockSpec(memory_space=pl.ANY), wspec, vec, vec],
        out_specs=[uz, wspec, vec, vec],
        out_shape=[jax.ShapeDtypeStruct(dproj.shape, F32), jax.ShapeDtypeStruct(mix_w.shape, F32),
                   jax.ShapeDtypeStruct(mix_b.shape, F32), jax.ShapeDtypeStruct(scale.shape, F32)],
        input_output_aliases={2: 0},
        compiler_params=_params(("parallel", "arbitrary")))(proj, dyb, dproj, mix_w, mix_b, scale)


def _ssd_bwd(dyan, ypre, proj, xbc, dtraw, hp, dproj, dtb, alog, dsk, gn, nb, s, di, z_col0):
    t = nb * s
    nc = s // CHUNK
    nh = di // HEAD_DIM
    hpg = nh // SSM_GROUPS
    gw = di // SSM_GROUPS
    gn_w = SSM_GROUPS * D_STATE
    dc = di + 2 * gn_w
    b_blk = _col_block(di, gn_w)
    z_blk = _col_block(z_col0, di)
    L, P, N = CHUNK, HEAD_DIM, D_STATE
    tril = _ssd_consts()

    def body(dy_ref, ypre_ref, z_ref, x_ref, b_ref, c_ref, dtr_ref, hp_ref, _, dtb_ref, alog_ref, dsk_ref, gn_ref,
             tril_ref, dz_ref, ddt_ref, dxbc_ref, dgn_ref, ddsk_ref, dalog_ref, ddtb_ref, dst_ref, dyp_ref):
        @pl.when((pl.program_id(0) == 0) & (pl.program_id(1) == 0))
        def _():
            dgn_ref[...] = jnp.zeros_like(dgn_ref)
            ddsk_ref[...] = jnp.zeros_like(ddsk_ref)
            dalog_ref[...] = jnp.zeros_like(dalog_ref)
            ddtb_ref[...] = jnp.zeros_like(ddtb_ref)

        @pl.when(pl.program_id(1) == 0)
        def _():
            dst_ref[...] = jnp.zeros_like(dst_ref)

        for g in range(SSM_GROUPS):
            gs = slice(g * gw, (g + 1) * gw)
            zv = z_ref[:, gs]
            yp = ypre_ref[:, gs]
            sg = _sigmoid(zv)
            sz = zv * sg
            vg = yp * sz
            r = lax.rsqrt(jnp.mean(vg * vg, axis=-1, keepdims=True) + EPS)
            vn = vg * r
            dyg = dy_ref[:, gs]
            dgn_ref[:, gs] += jnp.sum(dyg * vn, axis=0, keepdims=True)
            dvn = dyg * gn_ref[:, gs]
            dv = r * (dvn - vn * jnp.mean(dvn * vn, axis=-1, keepdims=True))
            dyp_ref[:, gs] = dv * sz
            dz_ref[:, gs] = dv * yp * sg * (1.0 + zv * (1.0 - sg))

        dtpre = dtr_ref[...] + dtb_ref[...]
        dt = _softplus(dtpre)
        a_neg = -jnp.exp(alog_ref[...])
        a_dt = dt * a_neg
        tri = tril_ref[...]
        a_cs = jnp.dot(tri, a_dt, precision=HIGHEST, preferred_element_type=F32)
        a_cst = lax.dot_general(a_dt, tri, (((0,), (1,)), ((), ())), precision=HIGHEST, preferred_element_type=F32)
        lower = tri > 0.5
        lane = lax.broadcasted_iota(jnp.int32, (L, LANES), 1)
        sub = lax.broadcasted_iota(jnp.int32, (LANES, L), 0)
        rowi = lax.broadcasted_iota(jnp.int32, (L, LANES), 0)
        lane1 = lax.broadcasted_iota(jnp.int32, (1, LANES), 1)
        da_c = jnp.zeros((L, LANES), F32)
        da_r = jnp.zeros((LANES, L), F32)
        ddt_c = jnp.zeros((L, LANES), F32)
        ddsk = jnp.zeros((1, LANES), F32)
        for g in range(SSM_GROUPS):
            bg = b_ref[:, g * N:(g + 1) * N].astype(BF16)
            cg = c_ref[:, g * N:(g + 1) * N].astype(BF16)
            gm = _bdot_nt(cg, bg)
            dgm = jnp.zeros((L, L), F32)
            dbg = jnp.zeros((L, N), F32)
            dcg = jnp.zeros((L, N), F32)
            for e in range(hpg):
                h = g * hpg + e
                hs = slice(h * P, (h + 1) * P)
                a_col = a_cs[:, h:h + 1]
                decay = jnp.where(lower, jnp.exp(a_col - a_cst[h:h + 1, :]), 0.0)
                dt_col = dt[:, h:h + 1]
                xh = x_ref[:, hs]
                xdt = xh * dt_col
                m = gm * decay
                dy = dyp_ref[:, hs]
                hprev = hp_ref[0, hs, :]
                dh = dst_ref[hs, :]
                ea = jnp.exp(a_col)
                a_last = a_cs[L - 1:L, h:h + 1]
                e_last = jnp.exp(a_last)
                dte = jnp.exp(a_last - a_col)
                dmd = _bdot_nt(dy, xdt) * decay
                dgm = dgm + dmd
                q = dmd * gm
                ch = _bdot_nt(cg, hprev)
                bds = _bdot_nt(bg, dh)
                xe = xdt * dte
                w = jnp.sum(xe * bds, axis=-1, keepdims=True)
                da_col = jnp.sum(q, axis=-1, keepdims=True) + ea * jnp.sum(dy * ch, axis=-1, keepdims=True) - w
                da_end = jnp.sum(w) + e_last * jnp.sum(dh * hprev)
                dxdt = _bdot_tn(m, dy) + dte * bds
                dye = dy * ea
                dcg = dcg + _bdot(dye, hprev)
                dbg = dbg + _bdot(xe, dh)
                dst_ref[hs, :] = e_last * dh + _bdot_tn(dye, cg)
                dsk_h = dsk_ref[:, h:h + 1]
                dxbc_ref[:, hs] = dxdt * dt_col + dy * dsk_h
                da_c = jnp.where(lane == h, da_col + jnp.where(rowi == L - 1, da_end, 0.0), da_c)
                da_r = jnp.where(sub == h, jnp.sum(q, axis=0, keepdims=True), da_r)
                ddt_c = jnp.where(lane == h, jnp.sum(dxdt * xh, axis=-1, keepdims=True), ddt_c)
                ddsk = jnp.where(lane1 == h, jnp.sum(dy * xh), ddsk)
            dcg = dcg + _bdot(dgm, bg)
            dbg = dbg + _bdot_tn(dgm, cg)
            dxbc_ref[:, di + g * N:di + (g + 1) * N] = dbg
            dxbc_ref[:, di + gn_w + g * N:di + gn_w + (g + 1) * N] = dcg
        da = da_c - da_r.T
        rc = lax.dot_general(tri, da, (((0,), (0,)), ((), ())), precision=HIGHEST, preferred_element_type=F32)
        ddt = a_neg * rc + ddt_c
        ddtraw = ddt * _sigmoid(dtpre)
        ddt_ref[...] = ddtraw
        ddtb_ref[...] += jnp.sum(ddtraw, axis=0, keepdims=True)
        dalog_ref[...] += jnp.sum(dt * rc, axis=0, keepdims=True) * a_neg
        ddsk_ref[...] += ddsk

    row = lambda b, c: b * nc + (nc - 1 - c)
    full = lambda w: pl.BlockSpec((L, w), lambda b, c: (row(b, c), 0))
    zspec = pl.BlockSpec((L, di), lambda b, c: (row(b, c), z_blk))
    vec = lambda w: pl.BlockSpec((1, w), lambda b, c: (0, 0))
    return pl.pallas_call(
        body, name="ssd_bwd", grid=(nb, nc),
        in_specs=[full(di), full(di), zspec, full(di),
                  pl.BlockSpec((L, gn_w), lambda b, c: (row(b, c), b_blk)),
                  pl.BlockSpec((L, gn_w), lambda b, c: (row(b, c), b_blk + 1)),
                  full(LANES),
                  pl.BlockSpec((1, nh * P, N), lambda b, c: (row(b, c), 0, 0)),
                  pl.BlockSpec(memory_space=pl.ANY),
                  vec(LANES), vec(LANES), vec(LANES), vec(di),
                  pl.BlockSpec((L, L), lambda b, c: (0, 0))],
        out_specs=[zspec, full(LANES), full(dc), vec(di), vec(LANES), vec(LANES), vec(LANES)],
        out_shape=[jax.ShapeDtypeStruct(dproj.shape, F32), jax.ShapeDtypeStruct((t, LANES), F32),
                   jax.ShapeDtypeStruct((t, dc), F32), jax.ShapeDtypeStruct((1, di), F32),
                   jax.ShapeDtypeStruct((1, LANES), F32), jax.ShapeDtypeStruct((1, LANES), F32),
                   jax.ShapeDtypeStruct((1, LANES), F32)],
        scratch_shapes=[pltpu.VMEM((nh * P, N), F32), pltpu.VMEM((L, di), F32)],
        input_output_aliases={8: 0},
        compiler_params=_params(("arbitrary", "arbitrary")))(
            dyan, ypre, proj, xbc, xbc, xbc, dtraw, hp, dproj, dtb, alog, dsk, gn, tril)


def _conv_bwd(proj, dxbc, dproj, conv_w, conv_b, nb, s, col0, cb=512):
    dc = conv_w.shape[1]
    cb = min(cb, dc)
    blk0 = _col_block(col0, cb)

    def body(x_ref, dy_ref, _, w_ref, b_ref, dx_ref, dw_ref, db_ref):
        @pl.when(pl.program_id(1) == 0)
        def _():
            dw_ref[...] = jnp.zeros_like(dw_ref)
            db_ref[...] = jnp.zeros_like(db_ref)

        v = x_ref[...]
        row = lax.broadcasted_iota(jnp.int32, v.shape, 0)
        shifted = [v] + [_shift_down(v, j, row) for j in range(1, CONV_WIDTH)]
        acc = b_ref[...] + shifted[0] * w_ref[CONV_WIDTH - 1:CONV_WIDTH, :]
        for j in range(1, CONV_WIDTH):
            acc = acc + shifted[j] * w_ref[CONV_WIDTH - 1 - j:CONV_WIDTH - j, :]
        sg = _sigmoid(acc)
        dacc = dy_ref[...] * sg * (1.0 + acc * (1.0 - sg))
        db_ref[...] += jnp.sum(dacc, axis=0, keepdims=True)
        dx = dacc * w_ref[CONV_WIDTH - 1:CONV_WIDTH, :]
        for j in range(CONV_WIDTH):
            dw_ref[CONV_WIDTH - 1 - j:CONV_WIDTH - j, :] += jnp.sum(dacc * shifted[j], axis=0, keepdims=True)
            if j:
                dx = dx + _shift_up(dacc, j, row) * w_ref[CONV_WIDTH - 1 - j:CONV_WIDTH - j, :]
        dx_ref[...] = dx

    return pl.pallas_call(
        body, name="conv_bwd", grid=(dc // cb, nb),
        in_specs=[pl.BlockSpec((s, cb), lambda j, b: (b, blk0 + j)), pl.BlockSpec((s, cb), lambda j, b: (b, j)),
                  pl.BlockSpec(memory_space=pl.ANY),
                  pl.BlockSpec((CONV_WIDTH, cb), lambda j, b: (0, j)), pl.BlockSpec((1, cb), lambda j, b: (0, j))],
        out_specs=[pl.BlockSpec((s, cb), lambda j, b: (b, blk0 + j)),
                   pl.BlockSpec((CONV_WIDTH, cb), lambda j, b: (0, j)), pl.BlockSpec((1, cb), lambda j, b: (0, j))],
        out_shape=[jax.ShapeDtypeStruct(dproj.shape, F32), jax.ShapeDtypeStruct(conv_w.shape, F32),
                   jax.ShapeDtypeStruct(conv_b.shape, F32)],
        input_output_aliases={2: 0},
        compiler_params=_params(("parallel", "arbitrary")))(proj, dxbc, dproj, conv_w, conv_b)


def _local_step(x, p, tgt, w):
    nb, s, d = x.shape
    t = nb * s
    di = w["w_branch_a"].shape[0]
    nh = di // HEAD_DIM
    gn_w = SSM_GROUPS * D_STATE
    dc = di + 2 * gn_w
    pgd = d // N_POOL
    x2 = x.reshape(t, d)
    p2 = p.reshape(t, p.shape[-1])
    tgt2 = tgt.reshape(t, d)

    w_in = w["w_in"]
    o_xbc, o_dt, o_u = di, di + dc, di + dc + nh
    o_zp, o_ga, o_gb = o_u + d, o_u + 2 * d, o_u + 3 * d
    uz_cols = []
    for g in range(N_POOL):
        uz_cols += [w_in[:, o_u + g * pgd:o_u + (g + 1) * pgd], w_in[:, o_zp + g * pgd:o_zp + (g + 1) * pgd]]
    w_main = jnp.concatenate([w_in[:, o_ga:], w_in[:, :o_dt]] + uz_cols, axis=1).astype(BF16)
    w_dt = jnp.pad(w_in[:, o_dt:o_u], ((0, 0), (0, LANES - nh))).astype(BF16)
    c_g, c_z, c_xbc, c_uz = 0, 2 * d, 2 * d + di, 2 * d + di + dc
    n_main = w_main.shape[1]

    pad_h = lambda v: jnp.pad(v.reshape(1, nh).astype(F32), ((0, 0), (0, LANES - nh)))
    dtb, alog, dsk = pad_h(w["dt_bias"]), pad_h(w["a_log"]), pad_h(w["d_skip"])
    vec = lambda v: v.reshape(1, -1).astype(F32)
    norm_g, gn, conv_b = vec(w["norm_g"]), vec(w["gnorm_g"]), vec(w["conv_b"])
    mix_b, scale, ple_g, final_g = vec(w["pool_mix_b"]), vec(w["pool_scale"]), vec(w["ple_norm_g"]), vec(w["final_g"])
    conv_w = w["conv_w"].astype(F32)
    mix_w = w["pool_mix_w"].astype(BF16)
    wa, wb, wo = w["w_branch_a"].astype(BF16), w["w_branch_b"].astype(BF16), w["w_out"].astype(BF16)
    wpg, wup = w["w_ple_gate"].astype(BF16), w["w_ple_up"].astype(BF16)

    proj, dtraw, h = _inproj(x2, norm_g, w_main, w_dt)
    xbc = _conv_fwd(proj, conv_w, conv_b, nb, s, c_xbc)
    ypre, yan, hp = _ssd_fwd(proj, xbc, dtraw, dtb, alog, dsk, gn, nb, s, di, c_z)
    ybp = _pool_fwd(proj, mix_w, mix_b, scale, nb, s, c_uz)
    ya = _mm(yan, wa, "branch_a")
    yb = _mm(ybp, wb, "branch_b")
    merged = _merge_fwd(ya, yb, proj, c_g)
    mo = _mm(merged, wo, "out_proj")
    x1, hn = _ple_pre(x2, mo, ple_g)
    pre = _mm(hn, wpg, "ple_gate")
    pu = _mm(p2, wup, "ple_up")

    dx2, dpre, dpu, loss, d_final_g = _tail(x1, pre, pu, tgt2, final_g)
    d_wpg = _mm_tn(hn, dpre, "d_w_ple_gate")
    d_wup = _mm_tn(p2, dpu, "d_w_ple_up")
    dhn = _mm_nt(dpre, wpg, "d_hn")
    dx1, d_ple_g = _rms_bwd(x1, [dhn], dx2, ple_g, "ple_bwd")
    d_wo = _mm_tn(merged, dx1, "d_w_out")
    dm = _mm_nt(dx1, wo, "d_merged")
    dya, dyb, dproj = _merge_bwd(dm, ya, yb, proj, c_g, n_main)
    d_wa = _mm_tn(yan, dya, "d_w_branch_a")
    d_wb = _mm_tn(ybp, dyb, "d_w_branch_b")
    dyan = _mm_nt(dya, wa, "d_y_a")
    dybp = _mm_nt(dyb, wb, "d_y_b")
    dproj, d_mix_w, d_mix_b, d_scale = _pool_bwd(proj, dybp, dproj, mix_w, mix_b, scale, nb, s, c_uz)
    dproj, ddt, dxbc, d_gn, d_dsk, d_alog, d_dtb = _ssd_bwd(
        dyan, ypre, proj, xbc, dtraw, hp, dproj, dtb, alog, dsk, gn, nb, s, di, c_z)
    dproj, d_conv_w, d_conv_b = _conv_bwd(proj, dxbc, dproj, conv_w, conv_b, nb, s, c_xbc)
    d_wmain = _mm_tn(h, dproj, "d_w_in")
    d_wdt = _mm_tn(h, ddt, "d_w_dt")
    dh_main = _mm_nt(dproj, w_main, "d_h")
    dh_dt = _mm_nt(ddt, w_dt, "d_h_dt")
    gx, d_norm_g = _rms_bwd(x2, [dh_main, dh_dt], dx1, norm_g, "in_bwd")

    uz = d_wmain[:, c_uz:].reshape(d, N_POOL, 2, pgd)
    d_w_in = jnp.concatenate([d_wmain[:, c_z:c_uz], d_wdt[:, :nh], uz[:, :, 0].reshape(d, d), uz[:, :, 1].reshape(d, d),
                              d_wmain[:, :c_z]], axis=1)
    grads = dict(norm_g=d_norm_g, w_in=d_w_in, conv_w=d_conv_w, conv_b=d_conv_b, dt_bias=d_dtb[:, :nh],
                 a_log=d_alog[:, :nh], d_skip=d_dsk[:, :nh], gnorm_g=d_gn, pool_mix_w=d_mix_w, pool_mix_b=d_mix_b,
                 pool_scale=d_scale, w_branch_a=d_wa, w_branch_b=d_wb, w_out=d_wo, ple_norm_g=d_ple_g,
                 w_ple_gate=d_wpg, w_ple_up=d_wup, final_g=d_final_g)
    return loss[0, 0], gx.reshape(nb, s, d), grads


def _place():
    return lax.axis_index("x"), lax.axis_index("y"), lax.axis_index("c")


def _other_chips(x, y):
    return [(1 - x, y), (x, 1 - y), (1 - x, 1 - y)]


def _gather_shards(wp):
    r = wp.shape[0]
    rh = r // 2

    def body(w_ref, o_ref, local_sem, send_sems, recv_sems):
        x, y, c = _place()
        k = 2 * x + y
        mine = pl.ds(pl.multiple_of(c * rh, 16), rh)
        theirs = pl.ds(pl.multiple_of((1 - c) * rh, 16), rh)
        chips = _other_chips(x, y)

        def copy(idx, src, dst, to):
            return pltpu.make_async_remote_copy(src_ref=src, dst_ref=dst, send_sem=send_sems.at[idx],
                                                recv_sem=recv_sems.at[idx], device_id=to, device_id_type=MESH)

        local = pltpu.make_async_copy(w_ref, o_ref.at[k], local_sem)
        local.start()
        sends = [copy(j, w_ref.at[mine], o_ref.at[k, mine], (px, py, c)) for j, (px, py) in enumerate(chips)]
        for cp in sends:
            cp.start()
        passed = []
        for j, (px, py) in enumerate(chips):
            kj = 2 * px + py
            copy(j, w_ref.at[mine], o_ref.at[kj, mine], (px, py, c)).wait_recv()
            fwd = copy(3 + j, o_ref.at[kj, mine], o_ref.at[kj, mine], (x, y, 1 - c))
            fwd.start()
            passed.append(fwd)
        for j, (px, py) in enumerate(chips):
            kj = 2 * px + py
            copy(3 + j, w_ref.at[theirs], o_ref.at[kj, theirs], (x, y, 1 - c)).wait_recv()
        for cp in sends + passed:
            cp.wait_send()
        local.wait()

    return pl.pallas_call(
        body, name="gather_shards",
        in_specs=[pl.BlockSpec(memory_space=pl.ANY)], out_specs=pl.BlockSpec(memory_space=pl.ANY),
        out_shape=jax.ShapeDtypeStruct((N_SHARD, r, LANES), wp.dtype),
        scratch_shapes=[pltpu.SemaphoreType.DMA, pltpu.SemaphoreType.DMA((6,)), pltpu.SemaphoreType.DMA((6,))],
    )(wp)


def _swap_sibling(v, name):
    def body(v_ref, o_ref, send_sem, recv_sem):
        x, y, c = _place()
        cp = pltpu.make_async_remote_copy(src_ref=v_ref, dst_ref=o_ref, send_sem=send_sem, recv_sem=recv_sem,
                                          device_id=(x, y, 1 - c), device_id_type=MESH)
        cp.start()
        cp.wait()

    return pl.pallas_call(
        body, name=name,
        in_specs=[pl.BlockSpec(memory_space=pl.ANY)], out_specs=pl.BlockSpec(memory_space=pl.ANY),
        out_shape=jax.ShapeDtypeStruct(v.shape, v.dtype),
        scratch_shapes=[pltpu.SemaphoreType.DMA, pltpu.SemaphoreType.DMA],
    )(v)


def _scatter_chips(v):
    _, rh, _ = v.shape

    def body(v_ref, o_ref, send_sems, recv_sems):
        x, y, c = _place()
        copies = []
        for j, (px, py) in enumerate(_other_chips(x, y)):
            cp = pltpu.make_async_remote_copy(src_ref=v_ref.at[2 * px + py], dst_ref=o_ref.at[j],
                                              send_sem=send_sems.at[j], recv_sem=recv_sems.at[j],
                                              device_id=(px, py, c), device_id_type=MESH)
            cp.start()
            copies.append(cp)
        for cp in copies:
            cp.wait()

    return pl.pallas_call(
        body, name="scatter_chips",
        in_specs=[pl.BlockSpec(memory_space=pl.ANY)], out_specs=pl.BlockSpec(memory_space=pl.ANY),
        out_shape=jax.ShapeDtypeStruct((N_SHARD - 1, rh, LANES), v.dtype),
        scratch_shapes=[pltpu.SemaphoreType.DMA((3,)), pltpu.SemaphoreType.DMA((3,))],
    )(v)


def _join_halves(v):
    rh = v.shape[0]

    def body(v_ref, o_ref, local_sem, send_sem, recv_sem):
        x, y, c = _place()
        mine = pl.ds(pl.multiple_of(c * rh, 8), rh)
        theirs = pl.ds(pl.multiple_of((1 - c) * rh, 8), rh)
        local = pltpu.make_async_copy(v_ref, o_ref.at[mine], local_sem)
        local.start()
        cp = pltpu.make_async_remote_copy(src_ref=v_ref, dst_ref=o_ref.at[mine], send_sem=send_sem, recv_sem=recv_sem,
                                          device_id=(x, y, 1 - c), device_id_type=MESH)
        cp.start()
        cp.wait_send()
        pltpu.make_async_remote_copy(src_ref=v_ref, dst_ref=o_ref.at[theirs], send_sem=send_sem, recv_sem=recv_sem,
                                     device_id=(x, y, 1 - c), device_id_type=MESH).wait_recv()
        local.wait()

    return pl.pallas_call(
        body, name="join_halves",
        in_specs=[pl.BlockSpec(memory_space=pl.ANY)], out_specs=pl.BlockSpec(memory_space=pl.ANY),
        out_shape=jax.ShapeDtypeStruct((2 * rh, LANES), v.dtype),
        scratch_shapes=[pltpu.SemaphoreType.DMA, pltpu.SemaphoreType.DMA, pltpu.SemaphoreType.DMA],
    )(v)


def _allreduce_small(v):
    rows = v.shape[0]

    def body(v_ref, o_ref, buf_ref, send_sems, recv_sems):
        x, y, c = _place()
        me = 4 * x + 2 * y + c
        buf_ref[me] = v_ref[...]
        copies = []
        for rel in range(1, 8):
            peer = (x ^ (rel >> 2), y ^ ((rel >> 1) & 1), c ^ (rel & 1))
            cp = pltpu.make_async_remote_copy(src_ref=v_ref, dst_ref=buf_ref.at[me], send_sem=send_sems.at[rel - 1],
                                              recv_sem=recv_sems.at[rel - 1], device_id=peer, device_id_type=MESH)
            cp.start()
            copies.append(cp)
        for rel in range(1, 8):
            peer_id = me ^ rel
            pltpu.make_async_remote_copy(src_ref=v_ref, dst_ref=buf_ref.at[peer_id], send_sem=send_sems.at[rel - 1],
                                         recv_sem=recv_sems.at[rel - 1], device_id=(x, y, c),
                                         device_id_type=MESH).wait_recv()
        for cp in copies:
            cp.wait_send()
        acc = buf_ref[0]
        for i in range(1, 8):
            acc = acc + buf_ref[i]
        o_ref[...] = acc

    return pl.pallas_call(
        body, name="allreduce_small",
        in_specs=[pl.BlockSpec(memory_space=pltpu.VMEM)], out_specs=pl.BlockSpec(memory_space=pltpu.VMEM),
        out_shape=jax.ShapeDtypeStruct(v.shape, F32),
        scratch_shapes=[pltpu.VMEM((8, rows, LANES), F32), pltpu.SemaphoreType.DMA((7,)), pltpu.SemaphoreType.DMA((7,))],
    )(v)


def _row_block(rows, cap=2048):
    best = 8
    for cand in range(8, min(rows, cap) + 1, 8):
        if rows % cand == 0:
            best = cand
    return best


def _add(parts, name):
    rows = parts[0].shape[0]
    rb = _row_block(rows)

    def body(*refs):
        acc = refs[0][...]
        for ref in refs[1:-1]:
            acc = acc + ref[...]
        refs[-1][...] = acc

    spec = pl.BlockSpec((rb, LANES), lambda i: (i, 0))
    return pl.pallas_call(
        body, name=name, grid=(rows // rb,), in_specs=[spec] * len(parts), out_specs=spec,
        out_shape=jax.ShapeDtypeStruct((rows, LANES), F32),
        compiler_params=_params(("parallel",)))(*parts)


def _adamw(wv, g, m, v, name):
    rows = wv.shape[0]
    rb = _row_block(rows)
    c1 = 1.0 - ADAM_B1 ** ADAM_STEP
    c2 = 1.0 - ADAM_B2 ** ADAM_STEP

    def body(w_ref, g_ref, m_ref, v_ref, d_ref, nm_ref, nv_ref):
        gv = g_ref[...]
        nm = ADAM_B1 * m_ref[...] + (1.0 - ADAM_B1) * gv
        nv = ADAM_B2 * v_ref[...] + (1.0 - ADAM_B2) * (gv * gv)
        nm_ref[...] = nm
        nv_ref[...] = nv
        d_ref[...] = -ADAM_LR * ((nm / c1) / (jnp.sqrt(nv / c2) + ADAM_EPS) + ADAM_WD * w_ref[...])

    spec = pl.BlockSpec((rb, LANES), lambda i: (i, 0))
    return pl.pallas_call(
        body, name=name, grid=(rows // rb,), in_specs=[spec] * 4, out_specs=[spec] * 3,
        out_shape=[jax.ShapeDtypeStruct((rows, LANES), F32)] * 3,
        compiler_params=_params(("parallel",)))(wv, g, m, v)


def _pack(flats, row_multiple):
    cat = jnp.concatenate(flats, axis=-1)
    n = cat.shape[-1]
    rows = -(-n // LANES)
    rows = -(-rows // row_multiple) * row_multiple
    cat = jnp.pad(cat, [(0, 0)] * (cat.ndim - 1) + [(0, rows * LANES - n)])
    return cat.reshape(cat.shape[:-1] + (rows, LANES))


def _unpack(packed, shapes):
    flat = packed.reshape(packed.shape[:-2] + (-1,))
    out, off = [], 0
    for shp in shapes:
        n = 1
        for dim in shp:
            n *= dim
        out.append(flat[..., off:off + n].reshape(flat.shape[:-1] + tuple(shp)))
        off += n
    return out


def _shard_axis(name):
    return {"w_in": 1, "conv_w": 1, "pool_mix_w": 1, "w_branch_a": 0, "w_branch_b": 0, "w_out": 0,
            "w_ple_gate": 0, "w_ple_up": 1}[name]


def _assemble(name, shards):
    ax = _shard_axis(name)
    return jnp.concatenate([shards[j] for j in range(N_SHARD)], axis=ax)


def _split(name, full):
    ax = _shard_axis(name)
    parts = jnp.split(full, N_SHARD, axis=ax)
    return jnp.stack([part.reshape(-1) for part in parts])


def kernel(x, p, norm_g, w_in, conv_w, conv_b, dt_bias, a_log, d_skip, gnorm_g, pool_mix_w, pool_mix_b, pool_scale, w_branch_a, w_branch_b, w_out, ple_norm_g, w_ple_gate, w_ple_up, final_g, loss_target, m_norm_g, m_w_in, m_conv_w, m_conv_b, m_dt_bias, m_a_log, m_d_skip, m_gnorm_g, m_pool_mix_w, m_pool_mix_b, m_pool_scale, m_w_branch_a, m_w_branch_b, m_w_out, m_ple_norm_g, m_w_ple_gate, m_w_ple_up, m_final_g, v_norm_g, v_w_in, v_conv_w, v_conv_b, v_dt_bias, v_a_log, v_d_skip, v_gnorm_g, v_pool_mix_w, v_pool_mix_b, v_pool_scale, v_w_branch_a, v_w_branch_b, v_w_out, v_ple_norm_g, v_w_ple_gate, v_w_ple_up, v_final_g):
    wts = dict(norm_g=norm_g, w_in=w_in, conv_w=conv_w, conv_b=conv_b, dt_bias=dt_bias, a_log=a_log, d_skip=d_skip,
               gnorm_g=gnorm_g, pool_mix_w=pool_mix_w, pool_mix_b=pool_mix_b, pool_scale=pool_scale,
               w_branch_a=w_branch_a, w_branch_b=w_branch_b, w_out=w_out, ple_norm_g=ple_norm_g,
               w_ple_gate=w_ple_gate, w_ple_up=w_ple_up, final_g=final_g)
    mom_m = dict(norm_g=m_norm_g, w_in=m_w_in, conv_w=m_conv_w, conv_b=m_conv_b, dt_bias=m_dt_bias, a_log=m_a_log,
                 d_skip=m_d_skip, gnorm_g=m_gnorm_g, pool_mix_w=m_pool_mix_w, pool_mix_b=m_pool_mix_b,
                 pool_scale=m_pool_scale, w_branch_a=m_w_branch_a, w_branch_b=m_w_branch_b, w_out=m_w_out,
                 ple_norm_g=m_ple_norm_g, w_ple_gate=m_w_ple_gate, w_ple_up=m_w_ple_up, final_g=m_final_g)
    mom_v = dict(norm_g=v_norm_g, w_in=v_w_in, conv_w=v_conv_w, conv_b=v_conv_b, dt_bias=v_dt_bias, a_log=v_a_log,
                 d_skip=v_d_skip, gnorm_g=v_gnorm_g, pool_mix_w=v_pool_mix_w, pool_mix_b=v_pool_mix_b,
                 pool_scale=v_pool_scale, w_branch_a=v_w_branch_a, w_branch_b=v_w_branch_b, w_out=v_w_out,
                 ple_norm_g=v_ple_norm_g, w_ple_gate=v_w_ple_gate, w_ple_up=v_w_ple_up, final_g=v_final_g)
    c = lax.axis_index("c")
    k = 2 * lax.axis_index("x") + lax.axis_index("y")

    shard_shapes = [wts[n].shape[1:] for n in SHARDED]
    small_shapes = [wts[n].shape for n in SMALL] + [(1,)]

    w_pack = _pack([wts[n].reshape(-1).astype(BF16) for n in SHARDED], 32)
    gathered = _unpack(_gather_shards(w_pack), shard_shapes)
    full = {n: _assemble(n, g) for n, g in zip(SHARDED, gathered)}
    for n in SMALL:
        full[n] = wts[n].reshape(wts[n].shape[-1:]) if n == "final_g" else wts[n][0]

    loss, grad_x, grads = _local_step(x, p[0], loss_target, full)

    g_pack = _pack([_split(n, grads[n]) for n in SHARDED], 32)
    rh = g_pack.shape[1] // 2
    keep = lax.dynamic_slice_in_dim(g_pack, c * rh, rh, axis=1)
    give = lax.dynamic_slice_in_dim(g_pack, (1 - c) * rh, rh, axis=1)
    got = _swap_sibling(give.reshape(N_SHARD * rh, LANES), "swap_halves")
    pair = _add([keep.reshape(N_SHARD * rh, LANES), got], "add_pair").reshape(N_SHARD, rh, LANES)
    landed = _scatter_chips(pair)
    own = lax.dynamic_index_in_dim(pair, k, axis=0, keepdims=False)
    half_sum = _add([own, landed[0], landed[1], landed[2]], "add_chips")
    g_shard = _join_halves(half_sum)

    small = _pack([grads[n].reshape(-1) for n in SMALL] + [loss.reshape(1)], 8)
    small_sum = _allreduce_small(small)

    def packed(src):
        return (_pack([src[n].reshape(-1) for n in SHARDED], 32),
                _pack([src[n].reshape(-1) for n in SMALL] + [jnp.zeros((1,), F32)], 8))

    (w_big, w_sm), (m_big, m_sm), (v_big, v_sm) = packed(wts), packed(mom_m), packed(mom_v)
    big_out = [g_shard] + list(_adamw(w_big, g_shard, m_big, v_big, "adamw_shards"))
    small_out = [small_sum] + list(_adamw(w_sm, small_sum, m_sm, v_sm, "adamw_small"))

    outs = {}
    for kind, big, sm in zip(("grad", "delta", "new_m", "new_v"), big_out, small_out):
        for n, val in zip(SHARDED, _unpack(big, shard_shapes)):
            outs[kind, n] = val.reshape(wts[n].shape)
        for n, val in zip(SMALL, _unpack(sm, small_shapes)[:-1]):
            outs[kind, n] = val
    loss_total = _unpack(small_sum, small_shapes)[-1][0]
    return (loss_total, grad_x, *[outs[kind, n] for kind in ("grad", "delta", "new_m", "new_v") for n in WEIGHTS])
```

```python
import functools

import jax
import jax.numpy as jnp
from jax import lax
from jax.experimental import pallas as pl
from jax.experimental.pallas import tpu as pltpu

F32 = jnp.float32
BF16 = jnp.bfloat16
HIGHEST = lax.Precision.HIGHEST
MESH = pl.DeviceIdType.MESH

EPS = 1e-6
HEAD_DIM = 64
SSM_GROUPS = 4
D_STATE = 128
CONV_WIDTH = 4
CHUNK = 128
N_POOL = 4
LANES = 128
N_SHARD = 4

ADAM_LR = 0.001
ADAM_B1 = 0.9
ADAM_B2 = 0.999
ADAM_EPS = 1e-08
ADAM_WD = 0.01
ADAM_STEP = 10

BIG = ("w_in", "pool_mix_w", "w_branch_a", "w_branch_b", "w_out", "w_ple_gate", "w_ple_up")
SMALL = ("norm_g", "conv_b", "dt_bias", "a_log", "d_skip", "gnorm_g", "pool_mix_b", "pool_scale",
         "ple_norm_g", "final_g")
WEIGHTS = ("norm_g", "w_in", "conv_w", "conv_b", "dt_bias", "a_log", "d_skip", "gnorm_g", "pool_mix_w",
           "pool_mix_b", "pool_scale", "w_branch_a", "w_branch_b", "w_out", "ple_norm_g", "w_ple_gate",
           "w_ple_up", "final_g")


def _params(sem=None, vmem_mb=56):
    kw = dict(vmem_limit_bytes=vmem_mb << 20)
    if sem is not None:
        kw["dimension_semantics"] = sem
    return pltpu.CompilerParams(**kw)


def _sigmoid(v):
    return 1.0 / (1.0 + jnp.exp(-v))


def _softplus(v):
    return jnp.maximum(v, 0.0) + jnp.log1p(jnp.exp(-jnp.abs(v)))


def _bdot(a, b):
    return jnp.dot(a.astype(BF16), b.astype(BF16), preferred_element_type=F32)


def _bdot_nt(a, b):
    return lax.dot_general(a.astype(BF16), b.astype(BF16), (((1,), (1,)), ((), ())), preferred_element_type=F32)


def _bdot_tn(a, b):
    return lax.dot_general(a.astype(BF16), b.astype(BF16), (((0,), (0,)), ((), ())), preferred_element_type=F32)


def _col_block(col0, width):
    assert col0 % width == 0, (col0, width)
    return col0 // width


def _tile(n, cap):
    if n <= cap:
        return n
    best = None
    for cand in range(8, cap + 1, 8):
        if n % cand == 0:
            best = cand
    assert best is not None, (n, cap)
    return best


def _shift_down(v, j, row):
    return jnp.where(row >= j, pltpu.roll(v, j, 0), 0.0)


def _shift_up(v, j, row):
    n = v.shape[0]
    return jnp.where(row < n - j, pltpu.roll(v, n - j, 0), 0.0)


def _mm(a, w, name, tm=1024, tn=1024):
    t, k = a.shape
    tm = min(tm, t)
    blocked = w.ndim == 3
    if blocked:
        nblk, _, tn = w.shape
        n = nblk * tn
        w_spec = pl.BlockSpec((1, k, tn), lambda i, j: (j, 0, 0))
    else:
        n = w.shape[1]
        tn = min(tn, n)
        w_spec = pl.BlockSpec((k, tn), lambda i, j: (0, j))

    def body(a_ref, w_ref, o_ref):
        wv = w_ref[0] if blocked else w_ref[...]
        o_ref[...] = _bdot(a_ref[...], wv)

    return pl.pallas_call(
        body, name=name, grid=(t // tm, n // tn),
        in_specs=[pl.BlockSpec((tm, k), lambda i, j: (i, 0)), w_spec],
        out_specs=pl.BlockSpec((tm, tn), lambda i, j: (i, j)),
        out_shape=jax.ShapeDtypeStruct((t, n), F32),
        compiler_params=_params(("parallel", "parallel")))(a, w)


def _mm_nt(a, w, name, tm=1024, tk=1024):
    t, k = a.shape
    n = w.shape[0]
    tm, tk = min(tm, t), min(tk, k)

    def body(a_ref, w_ref, o_ref):
        kk = pl.program_id(1)
        part = _bdot_nt(a_ref[...], w_ref[...])

        @pl.when(kk == 0)
        def _():
            o_ref[...] = part

        @pl.when(kk > 0)
        def _():
            o_ref[...] += part

    return pl.pallas_call(
        body, name=name, grid=(t // tm, k // tk),
        in_specs=[pl.BlockSpec((tm, tk), lambda i, j: (i, j)), pl.BlockSpec((n, tk), lambda i, j: (0, j))],
        out_specs=pl.BlockSpec((tm, n), lambda i, j: (i, 0)),
        out_shape=jax.ShapeDtypeStruct((t, n), F32),
        compiler_params=_params(("parallel", "arbitrary")))(a, w)


def _mm_tn(a, b, name, tn=1024, tk=1024, col_blocks=False):
    t, m = a.shape
    n = b.shape[1]
    tn, tk = min(tn, n), min(tk, t)

    def body(a_ref, b_ref, o_ref):
        kk = pl.program_id(1)
        part = _bdot_tn(a_ref[...], b_ref[...])
        part = part[None] if col_blocks else part

        @pl.when(kk == 0)
        def _():
            o_ref[...] = part

        @pl.when(kk > 0)
        def _():
            o_ref[...] += part

    if col_blocks:
        out_spec = pl.BlockSpec((1, m, tn), lambda j, kk: (j, 0, 0))
        out_shape = jax.ShapeDtypeStruct((n // tn, m, tn), F32)
    else:
        out_spec = pl.BlockSpec((m, tn), lambda j, kk: (0, j))
        out_shape = jax.ShapeDtypeStruct((m, n), F32)
    return pl.pallas_call(
        body, name=name, grid=(n // tn, t // tk),
        in_specs=[pl.BlockSpec((tk, m), lambda j, kk: (kk, 0)), pl.BlockSpec((tk, tn), lambda j, kk: (kk, j))],
        out_specs=out_spec, out_shape=out_shape,
        compiler_params=_params(("parallel", "arbitrary")))(a, b)


def _w_in_pieces(d, di, dc, nh, shard_w):
    pgd = d // N_POOL
    o_dt, o_u = di + dc, di + dc + nh
    o_zp, o_ga, o_gb = o_u + d, o_u + 2 * d, o_u + 3 * d
    c_z, c_uz = 2 * d, 2 * d + di + dc
    runs = [(False, 0, o_ga, d), (False, d, o_gb, d), (False, c_z, 0, di + dc), (True, 0, o_dt, nh)]
    for g in range(N_POOL):
        runs.append((False, c_uz + 2 * g * pgd, o_u + g * pgd, pgd))
        runs.append((False, c_uz + (2 * g + 1) * pgd, o_zp + g * pgd, pgd))
    pieces = []
    for is_dt, dst, src, n in runs:
        while n > 0:
            k, off = divmod(src, shard_w)
            m = min(n, shard_w - off)
            pieces.append((is_dt, dst, k, off, m))
            dst, src, n = dst + m, src + m, n - m
    return pieces


def _regroup_w_in(w_sh, d, di, dc, nh, rb=256):
    _, rows, sw = w_sh.shape
    n_main = 4 * d + di + dc
    pieces = _w_in_pieces(d, di, dc, nh, sw)
    rb = min(rb, rows)

    def body(w_ref, main_ref, dt_ref):
        dt_ref[...] = jnp.zeros_like(dt_ref)
        for is_dt, dst, k, off, m in pieces:
            out = dt_ref if is_dt else main_ref
            out[:, dst:dst + m] = w_ref[k, :, off:off + m]

    return pl.pallas_call(
        body, name="regroup_w_in", grid=(rows // rb,),
        in_specs=[pl.BlockSpec((N_SHARD, rb, sw), lambda i: (0, i, 0))],
        out_specs=[pl.BlockSpec((rb, n_main), lambda i: (i, 0)), pl.BlockSpec((rb, LANES), lambda i: (i, 0))],
        out_shape=[jax.ShapeDtypeStruct((rows, n_main), w_sh.dtype), jax.ShapeDtypeStruct((rows, LANES), w_sh.dtype)],
        compiler_params=_params(("parallel",)))(w_sh)


def _ungroup_w_in(d_main, d_dt, d, di, dc, nh, rb=128):
    rows, n_main = d_main.shape
    sw = (n_main + nh) // N_SHARD
    pieces = _w_in_pieces(d, di, dc, nh, sw)
    rb = min(rb, rows)

    def body(main_ref, dt_ref, o_ref):
        for is_dt, dst, k, off, m in pieces:
            src = dt_ref if is_dt else main_ref
            o_ref[k, :, off:off + m] = src[:, dst:dst + m]

    return pl.pallas_call(
        body, name="ungroup_w_in", grid=(rows // rb,),
        in_specs=[pl.BlockSpec((rb, n_main), lambda i: (i, 0)), pl.BlockSpec((rb, LANES), lambda i: (i, 0))],
        out_specs=pl.BlockSpec((N_SHARD, rb, sw), lambda i: (0, i, 0)),
        out_shape=jax.ShapeDtypeStruct((N_SHARD, rows, sw), F32),
        compiler_params=_params(("parallel",)))(d_main, d_dt)


def _inproj(x2, norm_g, w_main, w_dt, tm=1024, tn=1024):
    t, d = x2.shape
    n = w_main.shape[1]
    tm, tn = min(tm, t), min(tn, n)

    def body(x_ref, g_ref, w_ref, wdt_ref, proj_ref, dt_ref, h_ref):
        @pl.when(pl.program_id(1) == 0)
        def _():
            xv = x_ref[...]
            r = lax.rsqrt(jnp.mean(xv * xv, axis=-1, keepdims=True) + EPS)
            h = (xv * r * g_ref[...]).astype(BF16)
            h_ref[...] = h
            dt_ref[...] = jnp.dot(h, wdt_ref[...].astype(BF16), preferred_element_type=F32)

        proj_ref[...] = jnp.dot(h_ref[...], w_ref[...].astype(BF16), preferred_element_type=F32)

    return pl.pallas_call(
        body, name="inproj", grid=(t // tm, n // tn),
        in_specs=[pl.BlockSpec((tm, d), lambda i, j: (i, 0)), pl.BlockSpec((1, d), lambda i, j: (0, 0)),
                  pl.BlockSpec((d, tn), lambda i, j: (0, j)), pl.BlockSpec((d, LANES), lambda i, j: (0, 0))],
        out_specs=[pl.BlockSpec((tm, tn), lambda i, j: (i, j)), pl.BlockSpec((tm, LANES), lambda i, j: (i, 0)),
                   pl.BlockSpec((tm, d), lambda i, j: (i, 0))],
        out_shape=[jax.ShapeDtypeStruct((t, n), F32), jax.ShapeDtypeStruct((t, LANES), F32),
                   jax.ShapeDtypeStruct((t, d), BF16)],
        compiler_params=_params(("parallel", "arbitrary")))(x2, norm_g, w_main, w_dt)


def _conv_w_spec(conv_w, cb, j_axis):
    sw = conv_w.shape[2]
    assert sw % cb == 0, (sw, cb)
    per = sw // cb
    return N_SHARD * per, pl.BlockSpec((1, CONV_WIDTH, cb), lambda *ij: (ij[j_axis] // per, 0, ij[j_axis] % per))


def _conv_taps(v, w_ref, b_ref, row):
    shifted = [v] + [_shift_down(v, j, row) for j in range(1, CONV_WIDTH)]
    acc = b_ref[...] + shifted[0] * w_ref[0, CONV_WIDTH - 1:CONV_WIDTH, :]
    for j in range(1, CONV_WIDTH):
        acc = acc + shifted[j] * w_ref[0, CONV_WIDTH - 1 - j:CONV_WIDTH - j, :]
    return acc, shifted


def _conv_fwd(proj, conv_w, conv_b, nb, s, col0, cb=256):
    n_blk, w_spec = _conv_w_spec(conv_w, cb, 1)
    blk0 = _col_block(col0, cb)

    def body(x_ref, w_ref, b_ref, o_ref):
        v = x_ref[...]
        row = lax.broadcasted_iota(jnp.int32, v.shape, 0)
        acc, _ = _conv_taps(v, w_ref, b_ref, row)
        o_ref[...] = acc * _sigmoid(acc)

    return pl.pallas_call(
        body, name="conv_fwd", grid=(nb, n_blk),
        in_specs=[pl.BlockSpec((s, cb), lambda b, j: (b, blk0 + j)), w_spec, pl.BlockSpec((1, cb), lambda b, j: (0, j))],
        out_specs=pl.BlockSpec((s, cb), lambda b, j: (b, j)),
        out_shape=jax.ShapeDtypeStruct((nb * s, n_blk * cb), F32),
        compiler_params=_params(("parallel", "parallel")))(proj, conv_w, conv_b)


def _ssd_consts(di):
    r = lax.broadcasted_iota(jnp.int32, (CHUNK, CHUNK), 0)
    c = lax.broadcasted_iota(jnp.int32, (CHUNK, CHUNK), 1)
    tril = (r >= c).astype(F32)
    head = lax.broadcasted_iota(jnp.int32, (LANES, di), 0)
    chan = lax.broadcasted_iota(jnp.int32, (LANES, di), 1) // HEAD_DIM
    expand = (head == chan).astype(BF16)
    return tril, expand, expand.T


def _expand(v, e):
    v1 = v.astype(BF16)
    r1 = v - v1.astype(F32)
    v2 = r1.astype(BF16)
    v3 = (r1 - v2.astype(F32)).astype(BF16)
    dot = lambda a: jnp.dot(a, e, preferred_element_type=F32)
    return dot(v1) + dot(v2) + dot(v3)


def _head_sum(t, et, terms=2):
    acc = None
    for _ in range(terms):
        tb = t.astype(BF16)
        part = jnp.dot(tb, et, preferred_element_type=F32)
        acc = part if acc is None else acc + part
        t = t - tb.astype(F32)
    return acc


def _ssd_scalars(dtr_ref, dtb_ref, alog_ref, tri):
    dtpre = dtr_ref[...] + dtb_ref[...]
    dt = _softplus(dtpre)
    a_neg = -jnp.exp(alog_ref[...])
    a_dt = dt * a_neg
    a_cs = jnp.dot(tri, a_dt, precision=HIGHEST, preferred_element_type=F32)
    a_cst = lax.dot_general(a_dt, tri, (((0,), (1,)), ((), ())), precision=HIGHEST, preferred_element_type=F32)
    return dtpre, dt, a_neg, a_cs, a_cst


def _ssd_fwd(proj, xbc, dtraw, dtb, alog, dskx, gn, nb, s, di, z_col0):
    t = nb * s
    nc = s // CHUNK
    hpg = di // HEAD_DIM // SSM_GROUPS
    gw = di // SSM_GROUPS
    gn_w = SSM_GROUPS * D_STATE
    b_blk = _col_block(di, gn_w)
    z_blk = _col_block(z_col0, di)
    L, P, N = CHUNK, HEAD_DIM, D_STATE
    tril, expand, _ = _ssd_consts(di)

    def body(z_ref, x_ref, b_ref, c_ref, dtr_ref, dtb_ref, alog_ref, dskx_ref, gn_ref, tril_ref, e_ref,
             ypre_ref, yan_ref, hp_ref, st_ref, yd_ref, xdt_ref):
        @pl.when(pl.program_id(1) == 0)
        def _():
            st_ref[...] = jnp.zeros_like(st_ref)

        hp_ref[0] = st_ref[...]
        tri = tril_ref[...]
        _, dt, _, a_cs, a_cst = _ssd_scalars(dtr_ref, dtb_ref, alog_ref, tri)
        ev = e_ref[...]
        a_exp = _expand(a_cs, ev)
        xv = x_ref[...]
        xdt = xv * _expand(dt, ev)
        xdt_ref[...] = xdt
        a_last = a_exp[L - 1:L, :]
        xe = xdt * jnp.exp(a_last - a_exp)
        ea = jnp.exp(a_exp)
        e_last = jnp.exp(a_last)
        lower = tri > 0.5
        for g in range(SSM_GROUPS):
            gs = slice(g * gw, (g + 1) * gw)
            bg = b_ref[:, g * N:(g + 1) * N].astype(BF16)
            cg = c_ref[:, g * N:(g + 1) * N].astype(BF16)
            gm = _bdot_nt(cg, bg)
            ht = st_ref[:, gs]
            ch = _bdot(cg, ht)
            for e in range(hpg):
                h = g * hpg + e
                hs = slice(h * P, (h + 1) * P)
                decay = jnp.where(lower, jnp.exp(a_cs[:, h:h + 1] - a_cst[h:h + 1, :]), 0.0)
                yd_ref[:, hs] = _bdot(gm * decay, xdt_ref[:, hs])
            st_ref[:, gs] = ht * e_last[:, gs] + _bdot_tn(bg, xe[:, gs])
            ypre = yd_ref[:, gs] + ea[:, gs] * ch + xv[:, gs] * dskx_ref[:, gs]
            ypre_ref[:, gs] = ypre
            zv = z_ref[:, gs]
            v = ypre * zv * _sigmoid(zv)
            r = lax.rsqrt(jnp.mean(v * v, axis=-1, keepdims=True) + EPS)
            yan_ref[:, gs] = v * r * gn_ref[:, gs]

    row = lambda b, c: b * nc + c
    vec = lambda w: pl.BlockSpec((1, w), lambda b, c: (0, 0))
    return pl.pallas_call(
        body, name="ssd_fwd", grid=(nb, nc),
        in_specs=[pl.BlockSpec((L, di), lambda b, c: (row(b, c), z_blk)),
                  pl.BlockSpec((L, di), lambda b, c: (row(b, c), 0)),
                  pl.BlockSpec((L, gn_w), lambda b, c: (row(b, c), b_blk)),
                  pl.BlockSpec((L, gn_w), lambda b, c: (row(b, c), b_blk + 1)),
                  pl.BlockSpec((L, LANES), lambda b, c: (row(b, c), 0)),
                  vec(LANES), vec(LANES), vec(di), vec(di),
                  pl.BlockSpec((L, L), lambda b, c: (0, 0)),
                  pl.BlockSpec((LANES, di), lambda b, c: (0, 0))],
        out_specs=[pl.BlockSpec((L, di), lambda b, c: (row(b, c), 0)),
                   pl.BlockSpec((L, di), lambda b, c: (row(b, c), 0)),
                   pl.BlockSpec((1, N, di), lambda b, c: (row(b, c), 0, 0))],
        out_shape=[jax.ShapeDtypeStruct((t, di), F32), jax.ShapeDtypeStruct((t, di), F32),
                   jax.ShapeDtypeStruct((nb * nc, N, di), F32)],
        scratch_shapes=[pltpu.VMEM((N, di), F32), pltpu.VMEM((L, di), F32), pltpu.VMEM((L, di), F32)],
        compiler_params=_params(("parallel", "arbitrary")))(
            proj, xbc, xbc, xbc, dtraw, dtb, alog, dskx, gn, tril, expand)


def _pool_sum(v, g, row, shift):
    s2 = v + shift(v, 1, row)
    s4 = s2 + shift(s2, 2, row)
    s8 = s4 + shift(s4, 4, row)
    s16 = s8 + shift(s8, 8, row)
    return jnp.where(g == 0, s2, jnp.where(g == 1, s4, jnp.where(g == 2, s8, s16)))


def _pool_count(g, row):
    return jnp.minimum(row + 1, jnp.left_shift(2, g)).astype(F32)


def _pool_fwd(proj, mix_w, mix_b, scale, nb, s, col0):
    pgd = mix_w.shape[-1]
    blk0 = _col_block(col0, 2 * pgd)

    def body(uz_ref, w_ref, b_ref, sc_ref, o_ref):
        g = pl.program_id(1)
        u = uz_ref[:, :pgd]
        zp = uz_ref[:, pgd:]
        row = lax.broadcasted_iota(jnp.int32, u.shape, 0)
        pooled = _pool_sum(u, g, row, _shift_down) / _pool_count(g, row) - u
        mixed = _bdot(pooled, w_ref[:, 0].reshape(pgd, pgd)) + b_ref[...]
        o_ref[...] = mixed * sc_ref[...] * zp * _sigmoid(zp)

    return pl.pallas_call(
        body, name="pool_fwd", grid=(nb, N_POOL),
        in_specs=[pl.BlockSpec((s, 2 * pgd), lambda b, g: (b, blk0 + g)),
                  pl.BlockSpec((N_SHARD, 1, pgd // N_SHARD, pgd), lambda b, g: (0, g, 0, 0)),
                  pl.BlockSpec((1, pgd), lambda b, g: (0, g)), pl.BlockSpec((1, pgd), lambda b, g: (0, g))],
        out_specs=pl.BlockSpec((s, pgd), lambda b, g: (b, g)),
        out_shape=jax.ShapeDtypeStruct((nb * s, N_POOL * pgd), F32),
        compiler_params=_params(("parallel", "parallel")))(proj, mix_w, mix_b, scale)


def _merge_fwd(ya, yb, proj, col0, tm=512):
    t, d = ya.shape
    tm = min(tm, t)
    blk = _col_block(col0, 2 * d)

    def body(ya_ref, yb_ref, g_ref, o_ref):
        o_ref[...] = _sigmoid(g_ref[:, :d]) * ya_ref[...] + _sigmoid(g_ref[:, d:]) * yb_ref[...]

    row = pl.BlockSpec((tm, d), lambda i: (i, 0))
    return pl.pallas_call(
        body, name="merge_fwd", grid=(t // tm,),
        in_specs=[row, row, pl.BlockSpec((tm, 2 * d), lambda i: (i, blk))],
        out_specs=row, out_shape=jax.ShapeDtypeStruct((t, d), F32),
        compiler_params=_params(("parallel",)))(ya, yb, proj)


def _ple_pre(x2, mo, ple_g, tm=512):
    t, d = x2.shape
    tm = min(tm, t)

    def body(x_ref, mo_ref, g_ref, x1_ref, hn_ref):
        x1 = x_ref[...] + mo_ref[...]
        x1_ref[...] = x1
        r = lax.rsqrt(jnp.mean(x1 * x1, axis=-1, keepdims=True) + EPS)
        hn_ref[...] = (x1 * r * g_ref[...]).astype(BF16)

    row = pl.BlockSpec((tm, d), lambda i: (i, 0))
    return pl.pallas_call(
        body, name="ple_pre", grid=(t // tm,),
        in_specs=[row, row, pl.BlockSpec((1, d), lambda i: (0, 0))],
        out_specs=[row, row],
        out_shape=[jax.ShapeDtypeStruct((t, d), F32), jax.ShapeDtypeStruct((t, d), BF16)],
        compiler_params=_params(("parallel",)))(x2, mo, ple_g)


def _tail(x1, pre, pu, tgt, final_g, tm=512):
    t, d = x1.shape
    tm = min(tm, t)

    def body(x1_ref, pre_ref, pu_ref, tgt_ref, g_ref, dx2_ref, dpre_ref, dpu_ref, loss_ref, dg_ref):
        @pl.when(pl.program_id(0) == 0)
        def _():
            loss_ref[...] = jnp.zeros_like(loss_ref)
            dg_ref[...] = jnp.zeros_like(dg_ref)

        gate = _sigmoid(pre_ref[...])
        pu = pu_ref[...]
        x2 = x1_ref[...] + gate * pu
        r = lax.rsqrt(jnp.mean(x2 * x2, axis=-1, keepdims=True) + EPS)
        xn = x2 * r
        fg = g_ref[...]
        err = xn * fg - tgt_ref[...]
        loss_ref[...] += 0.5 * jnp.sum(jnp.mean(err * err, axis=-1, keepdims=True))
        dy = err * (1.0 / d)
        dg_ref[...] += jnp.sum(dy * xn, axis=0, keepdims=True)
        dxn = dy * fg
        dx2 = r * (dxn - xn * jnp.mean(dxn * xn, axis=-1, keepdims=True))
        dx2_ref[...] = dx2
        dpre_ref[...] = dx2 * pu * gate * (1.0 - gate)
        dpu_ref[...] = dx2 * gate

    row = pl.BlockSpec((tm, d), lambda i: (i, 0))
    return pl.pallas_call(
        body, name="tail", grid=(t // tm,),
        in_specs=[row, row, row, row, pl.BlockSpec((1, d), lambda i: (0, 0))],
        out_specs=[row, row, row, pl.BlockSpec((1, LANES), lambda i: (0, 0)), pl.BlockSpec((1, d), lambda i: (0, 0))],
        out_shape=[jax.ShapeDtypeStruct((t, d), F32)] * 3 + [jax.ShapeDtypeStruct((1, LANES), F32),
                                                             jax.ShapeDtypeStruct((1, d), F32)],
        compiler_params=_params(("arbitrary",)))(x1, pre, pu, tgt, final_g)


def _rms_bwd(xin, dhs, dres, g, name, tm=512):
    t, d = xin.shape
    tm = min(tm, t)
    n_dh = len(dhs)

    def body(*refs):
        x_ref, dh_refs, dres_ref, g_ref, dx_ref, dg_ref = refs[0], refs[1:1 + n_dh], *refs[1 + n_dh:]

        @pl.when(pl.program_id(0) == 0)
        def _():
            dg_ref[...] = jnp.zeros_like(dg_ref)

        xv = x_ref[...]
        dh = dh_refs[0][...]
        for ref in dh_refs[1:]:
            dh = dh + ref[...]
        r = lax.rsqrt(jnp.mean(xv * xv, axis=-1, keepdims=True) + EPS)
        xn = xv * r
        dg_ref[...] += jnp.sum(dh * xn, axis=0, keepdims=True)
        dd = dh * g_ref[...]
        dx_ref[...] = dres_ref[...] + r * (dd - xn * jnp.mean(dd * xn, axis=-1, keepdims=True))

    row = pl.BlockSpec((tm, d), lambda i: (i, 0))
    vec = pl.BlockSpec((1, d), lambda i: (0, 0))
    return pl.pallas_call(
        body, name=name, grid=(t // tm,),
        in_specs=[row] * (2 + n_dh) + [vec],
        out_specs=[row, vec],
        out_shape=[jax.ShapeDtypeStruct((t, d), F32), jax.ShapeDtypeStruct((1, d), F32)],
        compiler_params=_params(("arbitrary",)))(xin, *dhs, dres, g)


def _merge_bwd(dm, ya, yb, proj, col0, n_cols, tm=512):
    t, d = ya.shape
    tm = min(tm, t)
    blk = _col_block(col0, 2 * d)

    def body(dm_ref, ya_ref, yb_ref, g_ref, dya_ref, dyb_ref, dg_ref):
        dm_v = dm_ref[...]
        sa = _sigmoid(g_ref[:, :d])
        sb = _sigmoid(g_ref[:, d:])
        dya_ref[...] = dm_v * sa
        dyb_ref[...] = dm_v * sb
        dg_ref[:, :d] = dm_v * ya_ref[...] * sa * (1.0 - sa)
        dg_ref[:, d:] = dm_v * yb_ref[...] * sb * (1.0 - sb)

    row = pl.BlockSpec((tm, d), lambda i: (i, 0))
    gspec = pl.BlockSpec((tm, 2 * d), lambda i: (i, blk))
    return pl.pallas_call(
        body, name="merge_bwd", grid=(t // tm,),
        in_specs=[row, row, row, gspec],
        out_specs=[row, row, gspec],
        out_shape=[jax.ShapeDtypeStruct((t, d), F32), jax.ShapeDtypeStruct((t, d), F32),
                   jax.ShapeDtypeStruct((t, n_cols), F32)],
        compiler_params=_params(("parallel",)))(dm, ya, yb, proj)


def _pool_bwd(proj, dyb, dproj, mix_w, mix_b, scale, nb, s, col0):
    pgd = mix_w.shape[-1]
    blk0 = _col_block(col0, 2 * pgd)

    def body(uz_ref, dy_ref, _, w_ref, b_ref, sc_ref, duz_ref, dw_ref, db_ref, dsc_ref):
        g = pl.program_id(0)

        @pl.when(pl.program_id(1) == 0)
        def _():
            dw_ref[...] = jnp.zeros_like(dw_ref)
            db_ref[...] = jnp.zeros_like(db_ref)
            dsc_ref[...] = jnp.zeros_like(dsc_ref)

        u = uz_ref[:, :pgd]
        zp = uz_ref[:, pgd:]
        row = lax.broadcasted_iota(jnp.int32, u.shape, 0)
        cnt = _pool_count(g, row)
        pooled = _pool_sum(u, g, row, _shift_down) / cnt - u
        wv = w_ref[:, 0].reshape(pgd, pgd)
        mixed = _bdot(pooled, wv) + b_ref[...]
        sg = _sigmoid(zp)
        sz = zp * sg
        dy = dy_ref[...]
        sc = sc_ref[...]
        dsc_ref[...] += jnp.sum(dy * mixed * sz, axis=0, keepdims=True)
        dmixed = dy * sc * sz
        db_ref[...] += jnp.sum(dmixed, axis=0, keepdims=True)
        dw_ref[:, 0] += _bdot_tn(pooled, dmixed).reshape(N_SHARD, pgd // N_SHARD, pgd)
        dpooled = _bdot_nt(dmixed, wv)
        duz_ref[:, :pgd] = _pool_sum(dpooled / cnt, g, row, _shift_up) - dpooled
        duz_ref[:, pgd:] = dy * mixed * sc * sg * (1.0 + zp * (1.0 - sg))

    uz = pl.BlockSpec((s, 2 * pgd), lambda g, b: (b, blk0 + g))
    vec = pl.BlockSpec((1, pgd), lambda g, b: (0, g))
    wspec = pl.BlockSpec((N_SHARD, 1, pgd // N_SHARD, pgd), lambda g, b: (0, g, 0, 0))
    return pl.pallas_call(
        body, name="pool_bwd", grid=(N_POOL, nb),
        in_specs=[uz, pl.BlockSpec((s, pgd), lambda g, b: (b, g)), pl.BlockSpec(memory_space=pl.ANY), wspec, vec, vec],
        out_specs=[uz, wspec, vec, vec],
        out_shape=[jax.ShapeDtypeStruct(dproj.shape, F32), jax.ShapeDtypeStruct(mix_w.shape, F32),
                   jax.ShapeDtypeStruct(mix_b.shape, F32), jax.ShapeDtypeStruct(scale.shape, F32)],
        input_output_aliases={2: 0},
        compiler_params=_params(("parallel", "arbitrary")))(proj, dyb, dproj, mix_w, mix_b, scale)


def _ssd_bwd(dyan, ypre, proj, xbc, dtraw, hp, dproj, dtb, alog, dskx, gn, nb, s, di, z_col0):
    t = nb * s
    nc = s // CHUNK
    hpg = di // HEAD_DIM // SSM_GROUPS
    gw = di // SSM_GROUPS
    gn_w = SSM_GROUPS * D_STATE
    dc = di + 2 * gn_w
    b_blk = _col_block(di, gn_w)
    z_blk = _col_block(z_col0, di)
    L, P, N = CHUNK, HEAD_DIM, D_STATE
    tril, expand, expand_t = _ssd_consts(di)

    def body(dy_ref, ypre_ref, z_ref, x_ref, b_ref, c_ref, dtr_ref, hp_ref, _, dtb_ref, alog_ref, dskx_ref, gn_ref,
             tril_ref, e_ref, et_ref, dz_ref, ddt_ref, dxbc_ref, dgn_ref, ddsk_ref, dalog_ref, ddtb_ref,
             dst_ref, dyp_ref, xdt_ref, dxm_ref, t1_ref, t2_ref, t3_ref, aux_ref):
        @pl.when((pl.program_id(0) == 0) & (pl.program_id(1) == 0))
        def _():
            dgn_ref[...] = jnp.zeros_like(dgn_ref)
            ddsk_ref[...] = jnp.zeros_like(ddsk_ref)
            dalog_ref[...] = jnp.zeros_like(dalog_ref)
            ddtb_ref[...] = jnp.zeros_like(ddtb_ref)

        @pl.when(pl.program_id(1) == 0)
        def _():
            dst_ref[...] = jnp.zeros_like(dst_ref)

        tri = tril_ref[...]
        dtpre, dt, a_neg, a_cs, a_cst = _ssd_scalars(dtr_ref, dtb_ref, alog_ref, tri)
        ev = e_ref[...]
        a_exp = _expand(a_cs, ev)
        dt_exp = _expand(dt, ev)
        xv = x_ref[...]
        xdt = xv * dt_exp
        xdt_ref[...] = xdt
        a_last = a_exp[L - 1:L, :]
        dte = jnp.exp(a_last - a_exp)
        xe = xdt * dte
        ea = jnp.exp(a_exp)
        e_last = jnp.exp(a_last)
        lower = tri > 0.5
        aux_ref[...] = jnp.zeros_like(aux_ref)
        for g in range(SSM_GROUPS):
            gs = slice(g * gw, (g + 1) * gw)
            zv = z_ref[:, gs]
            yp = ypre_ref[:, gs]
            sg = _sigmoid(zv)
            sz = zv * sg
            vg = yp * sz
            r = lax.rsqrt(jnp.mean(vg * vg, axis=-1, keepdims=True) + EPS)
            vn = vg * r
            dyg = dy_ref[:, gs]
            dgn_ref[:, gs] += jnp.sum(dyg * vn, axis=0, keepdims=True)
            dvn = dyg * gn_ref[:, gs]
            dv = r * (dvn - vn * jnp.mean(dvn * vn, axis=-1, keepdims=True))
            dy = dv * sz
            dyp_ref[:, gs] = dy
            dz_ref[:, gs] = dv * yp * sg * (1.0 + zv * (1.0 - sg))
            bg = b_ref[:, g * N:(g + 1) * N].astype(BF16)
            cg = c_ref[:, g * N:(g + 1) * N].astype(BF16)
            gm = _bdot_nt(cg, bg)
            ht = hp_ref[0, :, gs]
            dht = dst_ref[:, gs]
            bds = _bdot(bg, dht)
            dye = dy * ea[:, gs]
            xe_g = xe[:, gs]
            dcg = _bdot_nt(dye, ht)
            dbg = _bdot_nt(xe_g, dht)
            dst_ref[:, gs] = e_last[:, gs] * dht + _bdot_tn(cg, dye)
            dgm = jnp.zeros((L, L), F32)
            for e in range(hpg):
                h = g * hpg + e
                hs = slice(h * P, (h + 1) * P)
                decay = jnp.where(lower, jnp.exp(a_cs[:, h:h + 1] - a_cst[h:h + 1, :]), 0.0)
                dy_h = dyp_ref[:, hs]
                dgm = dgm + _bdot_nt(dy_h, xdt_ref[:, hs]) * decay
                dxm_ref[:, hs] = _bdot_tn(gm * decay, dy_h)
            dxbc_ref[:, di + g * N:di + (g + 1) * N] = dbg + _bdot_tn(dgm, cg)
            dxbc_ref[:, di + gn_w + g * N:di + gn_w + (g + 1) * N] = dcg + _bdot(dgm, bg)
            dxm = dxm_ref[:, gs]
            x_g = xv[:, gs]
            dskx = dskx_ref[:, gs]
            xeb = xe_g * bds
            dxdt = dxm + dte[:, gs] * bds
            dxbc_ref[:, gs] = dxdt * dt_exp[:, gs] + dy * dskx
            each = ea[:, gs] * _bdot(cg, ht)
            y_diag = yp - x_g * dskx - each
            rnd = lambda v: v.astype(BF16).astype(F32)
            t1_ref[:, gs] = rnd(dy) * y_diag + dy * each - rnd(xdt[:, gs]) * dxm - xeb
            t2_ref[:, gs] = xeb
            t3_ref[:, gs] = dxdt * x_g
            aux_ref[0:1, gs] = jnp.sum(dht * ht, axis=0, keepdims=True)
            aux_ref[1:2, gs] = jnp.sum(dy * x_g, axis=0, keepdims=True)
        etv = et_ref[...]
        w_end = _head_sum(t2_ref[...], etv)
        aux = _head_sum(aux_ref[...], etv)
        rowi = lax.broadcasted_iota(jnp.int32, (L, LANES), 0)
        end = jnp.sum(w_end, axis=0, keepdims=True) + aux[0:1, :] * jnp.exp(a_cs[L - 1:L, :])
        da = _head_sum(t1_ref[...], etv, terms=3) + jnp.where(rowi == L - 1, end, 0.0)
        rc = lax.dot_general(tri, da, (((0,), (0,)), ((), ())), precision=HIGHEST, preferred_element_type=F32)
        ddt = a_neg * rc + _head_sum(t3_ref[...], etv)
        ddtraw = ddt * _sigmoid(dtpre)
        ddt_ref[...] = ddtraw
        ddtb_ref[...] += jnp.sum(ddtraw, axis=0, keepdims=True)
        dalog_ref[...] += jnp.sum(dt * rc, axis=0, keepdims=True) * a_neg
        ddsk_ref[...] += aux[1:2, :]

    row = lambda b, c: b * nc + (nc - 1 - c)
    full = lambda w: pl.BlockSpec((L, w), lambda b, c: (row(b, c), 0))
    zspec = pl.BlockSpec((L, di), lambda b, c: (row(b, c), z_blk))
    vec = lambda w: pl.BlockSpec((1, w), lambda b, c: (0, 0))
    slab = lambda shape: pltpu.VMEM(shape, F32)
    return pl.pallas_call(
        body, name="ssd_bwd", grid=(nb, nc),
        in_specs=[full(di), full(di), zspec, full(di),
                  pl.BlockSpec((L, gn_w), lambda b, c: (row(b, c), b_blk)),
                  pl.BlockSpec((L, gn_w), lambda b, c: (row(b, c), b_blk + 1)),
                  full(LANES),
                  pl.BlockSpec((1, N, di), lambda b, c: (row(b, c), 0, 0)),
                  pl.BlockSpec(memory_space=pl.ANY),
                  vec(LANES), vec(LANES), vec(di), vec(di),
                  pl.BlockSpec((L, L), lambda b, c: (0, 0)),
                  pl.BlockSpec((LANES, di), lambda b, c: (0, 0)),
                  pl.BlockSpec((di, LANES), lambda b, c: (0, 0))],
        out_specs=[zspec, full(LANES), full(dc), vec(di), vec(LANES), vec(LANES), vec(LANES)],
        out_shape=[jax.ShapeDtypeStruct(dproj.shape, F32), jax.ShapeDtypeStruct((t, LANES), F32),
                   jax.ShapeDtypeStruct((t, dc), F32), jax.ShapeDtypeStruct((1, di), F32),
                   jax.ShapeDtypeStruct((1, LANES), F32), jax.ShapeDtypeStruct((1, LANES), F32),
                   jax.ShapeDtypeStruct((1, LANES), F32)],
        scratch_shapes=[slab((N, di)), slab((L, di)), slab((L, di)), slab((L, di)), slab((L, di)), slab((L, di)),
                        slab((L, di)), slab((8, di))],
        input_output_aliases={8: 0},
        compiler_params=_params(("arbitrary", "arbitrary")))(
            dyan, ypre, proj, xbc, xbc, xbc, dtraw, hp, dproj, dtb, alog, dskx, gn, tril, expand, expand_t)


def _conv_bwd(proj, dxbc, dproj, conv_w, conv_b, nb, s, col0, cb=256):
    n_blk, w_spec = _conv_w_spec(conv_w, cb, 0)
    blk0 = _col_block(col0, cb)

    def body(x_ref, dy_ref, _, w_ref, b_ref, dx_ref, dw_ref, db_ref):
        @pl.when(pl.program_id(1) == 0)
        def _():
            dw_ref[...] = jnp.zeros_like(dw_ref)
            db_ref[...] = jnp.zeros_like(db_ref)

        v = x_ref[...]
        row = lax.broadcasted_iota(jnp.int32, v.shape, 0)
        acc, shifted = _conv_taps(v, w_ref, b_ref, row)
        sg = _sigmoid(acc)
        dacc = dy_ref[...] * sg * (1.0 + acc * (1.0 - sg))
        db_ref[...] += jnp.sum(dacc, axis=0, keepdims=True)
        dx = dacc * w_ref[0, CONV_WIDTH - 1:CONV_WIDTH, :]
        for j in range(CONV_WIDTH):
            dw_ref[0, CONV_WIDTH - 1 - j:CONV_WIDTH - j, :] += jnp.sum(dacc * shifted[j], axis=0, keepdims=True)
            if j:
                dx = dx + _shift_up(dacc, j, row) * w_ref[0, CONV_WIDTH - 1 - j:CONV_WIDTH - j, :]
        dx_ref[...] = dx

    return pl.pallas_call(
        body, name="conv_bwd", grid=(n_blk, nb),
        in_specs=[pl.BlockSpec((s, cb), lambda j, b: (b, blk0 + j)), pl.BlockSpec((s, cb), lambda j, b: (b, j)),
                  pl.BlockSpec(memory_space=pl.ANY), w_spec, pl.BlockSpec((1, cb), lambda j, b: (0, j))],
        out_specs=[pl.BlockSpec((s, cb), lambda j, b: (b, blk0 + j)), w_spec, pl.BlockSpec((1, cb), lambda j, b: (0, j))],
        out_shape=[jax.ShapeDtypeStruct(dproj.shape, F32), jax.ShapeDtypeStruct(conv_w.shape, F32),
                   jax.ShapeDtypeStruct(conv_b.shape, F32)],
        input_output_aliases={2: 0},
        compiler_params=_params(("parallel", "arbitrary")))(proj, dxbc, dproj, conv_w, conv_b)


def _local_step(x, p, tgt, wg, small):
    nb, s, d = x.shape
    t = nb * s
    di = N_SHARD * wg["w_branch_a"].shape[1]
    nh = di // HEAD_DIM
    gn_w = SSM_GROUPS * D_STATE
    dc = di + 2 * gn_w
    pgd = d // N_POOL
    x2 = x.reshape(t, d)
    p2 = p.reshape(t, p.shape[-1])
    tgt2 = tgt.reshape(t, d)

    w_main, w_dt = _regroup_w_in(wg["w_in"], d, di, dc, nh)
    c_g, c_z, c_xbc, c_uz = 0, 2 * d, 2 * d + di, 2 * d + di + dc
    n_main = w_main.shape[1]

    pad_h = lambda v: jnp.pad(v.reshape(1, nh).astype(F32), ((0, 0), (0, LANES - nh)))
    dtb, alog = pad_h(small["dt_bias"]), pad_h(small["a_log"])
    dskx = jnp.repeat(small["d_skip"].reshape(1, nh).astype(F32), HEAD_DIM, axis=1)
    vec = lambda v: v.reshape(1, -1).astype(F32)
    norm_g, gn, conv_b = vec(small["norm_g"]), vec(small["gnorm_g"]), vec(small["conv_b"])
    mix_b, scale = vec(small["pool_mix_b"]), vec(small["pool_scale"])
    ple_g, final_g = vec(small["ple_norm_g"]), vec(small["final_g"])
    conv_w = wg["conv_w"]
    mix_w = wg["pool_mix_w"].reshape(N_SHARD, N_POOL, pgd // N_SHARD, pgd)
    rows = lambda v: v.reshape(-1, v.shape[-1])
    wa, wb, wo, wpg = rows(wg["w_branch_a"]), rows(wg["w_branch_b"]), rows(wg["w_out"]), rows(wg["w_ple_gate"])
    wup = wg["w_ple_up"]

    proj, dtraw, h = _inproj(x2, norm_g, w_main, w_dt)
    xbc = _conv_fwd(proj, conv_w, conv_b, nb, s, c_xbc)
    ypre, yan, hp = _ssd_fwd(proj, xbc, dtraw, dtb, alog, dskx, gn, nb, s, di, c_z)
    ybp = _pool_fwd(proj, mix_w, mix_b, scale, nb, s, c_uz)
    ya = _mm(yan, wa, "branch_a")
    yb = _mm(ybp, wb, "branch_b")
    merged = _merge_fwd(ya, yb, proj, c_g)
    mo = _mm(merged, wo, "out_proj")
    x1, hn = _ple_pre(x2, mo, ple_g)
    pre = _mm(hn, wpg, "ple_gate")
    pu = _mm(p2, wup, "ple_up")

    dx2, dpre, dpu, loss, d_final_g = _tail(x1, pre, pu, tgt2, final_g)
    d_wpg = _mm_tn(hn, dpre, "d_w_ple_gate")
    d_wup = _mm_tn(p2, dpu, "d_w_ple_up", tn=wup.shape[-1], col_blocks=True)
    dhn = _mm_nt(dpre, wpg, "d_hn")
    dx1, d_ple_g = _rms_bwd(x1, [dhn], dx2, ple_g, "ple_bwd")
    d_wo = _mm_tn(merged, dx1, "d_w_out")
    dm = _mm_nt(dx1, wo, "d_merged")
    dya, dyb, dproj = _merge_bwd(dm, ya, yb, proj, c_g, n_main)
    d_wa = _mm_tn(yan, dya, "d_w_branch_a")
    d_wb = _mm_tn(ybp, dyb, "d_w_branch_b")
    dyan = _mm_nt(dya, wa, "d_y_a")
    dybp = _mm_nt(dyb, wb, "d_y_b")
    dproj, d_mix_w, d_mix_b, d_scale = _pool_bwd(proj, dybp, dproj, mix_w, mix_b, scale, nb, s, c_uz)
    dproj, ddt, dxbc, d_gn, d_dsk, d_alog, d_dtb = _ssd_bwd(
        dyan, ypre, proj, xbc, dtraw, hp, dproj, dtb, alog, dskx, gn, nb, s, di, c_z)
    dproj, d_conv_w, d_conv_b = _conv_bwd(proj, dxbc, dproj, conv_w, conv_b, nb, s, c_xbc)
    d_wmain = _mm_tn(h, dproj, "d_w_in")
    d_wdt = _mm_tn(h, ddt, "d_w_dt")
    dh_main = _mm_nt(dproj, w_main, "d_h")
    dh_dt = _mm_nt(ddt, w_dt, "d_h_dt")
    gx, d_norm_g = _rms_bwd(x2, [dh_main, dh_dt], dx1, norm_g, "in_bwd")
    d_w_in = _ungroup_w_in(d_wmain, d_wdt, d, di, dc, nh)

    shard_major = lambda v: v.reshape(N_SHARD, v.shape[0] // N_SHARD, v.shape[1])
    grads = dict(norm_g=d_norm_g, w_in=d_w_in, conv_w=d_conv_w, conv_b=d_conv_b, dt_bias=d_dtb[:, :nh],
                 a_log=d_alog[:, :nh], d_skip=d_dsk[:, :nh], gnorm_g=d_gn,
                 pool_mix_w=d_mix_w.reshape(N_SHARD, pgd, pgd), pool_mix_b=d_mix_b, pool_scale=d_scale,
                 w_branch_a=shard_major(d_wa), w_branch_b=shard_major(d_wb), w_out=shard_major(d_wo),
                 ple_norm_g=d_ple_g, w_ple_gate=shard_major(d_wpg), w_ple_up=d_wup, final_g=d_final_g)
    return loss[0, 0], gx.reshape(nb, s, d), grads


def _place():
    return lax.axis_index("x"), lax.axis_index("y"), lax.axis_index("c")


def _other_chips(x, y):
    return [(1 - x, y), (x, 1 - y), (1 - x, 1 - y)]


def _halves(c, rows, align):
    rh = rows // 2
    assert rows % 2 == 0 and rh % align == 0, (rows, align)
    return (pl.ds(pl.multiple_of(c * rh, align), rh), pl.ds(pl.multiple_of((1 - c) * rh, align), rh))


HBM = pl.BlockSpec(memory_space=pl.ANY)


def _gather_weights(split, whole):
    n_split, n_all = len(split), len(split) + len(whole)

    def body(*refs):
        ins, outs = refs[:n_all], refs[n_all:2 * n_all]
        local_sems, send_sems, recv_sems = refs[2 * n_all:]
        x, y, c = _place()
        k = 2 * x + y
        chips = _other_chips(x, y)

        def copy(idx, src, dst, to):
            return pltpu.make_async_remote_copy(src_ref=src, dst_ref=dst, send_sem=send_sems.at[idx],
                                                recv_sem=recv_sems.at[idx], device_id=to, device_id_type=MESH)

        local = [pltpu.make_async_copy(ins[i], outs[i].at[k], local_sems.at[i]) for i in range(n_all)]
        for cp in local:
            cp.start()
        started = []
        for i in range(n_all):
            for j, (px, py) in enumerate(chips):
                if i < n_split:
                    mine, _ = _halves(c, split[i].shape[0], 16)
                    cp = copy(6 * i + j, ins[i].at[mine], outs[i].at[k, mine], (px, py, c))
                else:
                    cp = copy(6 * n_split + 3 * (i - n_split) + j, ins[i], outs[i].at[k], (px, py, c))
                cp.start()
                started.append(cp)
        for i in range(n_split):
            mine, _ = _halves(c, split[i].shape[0], 16)
            for j, (px, py) in enumerate(chips):
                kj = 2 * px + py
                copy(6 * i + j, ins[i].at[mine], outs[i].at[kj, mine], (px, py, c)).wait_recv()
                cp = copy(6 * i + 3 + j, outs[i].at[kj, mine], outs[i].at[kj, mine], (x, y, 1 - c))
                cp.start()
                started.append(cp)
        for i in range(n_split, n_all):
            for j, (px, py) in enumerate(chips):
                copy(6 * n_split + 3 * (i - n_split) + j, ins[i], outs[i].at[2 * px + py], (px, py, c)).wait_recv()
        for i in range(n_split):
            _, theirs = _halves(c, split[i].shape[0], 16)
            for j, (px, py) in enumerate(chips):
                kj = 2 * px + py
                copy(6 * i + 3 + j, ins[i].at[theirs], outs[i].at[kj, theirs], (x, y, 1 - c)).wait_recv()
        for cp in started:
            cp.wait_send()
        for cp in local:
            cp.wait()

    arrays = list(split) + list(whole)
    n_sem = 6 * n_split + 3 * len(whole)
    return pl.pallas_call(
        body, name="gather_weights",
        in_specs=[HBM] * n_all, out_specs=[HBM] * n_all,
        out_shape=[jax.ShapeDtypeStruct((N_SHARD,) + a.shape, a.dtype) for a in arrays],
        scratch_shapes=[pltpu.SemaphoreType.DMA((n_all,)), pltpu.SemaphoreType.DMA((n_sem,)),
                        pltpu.SemaphoreType.DMA((n_sem,))],
    )(*arrays)


def _swap_halves(gs):
    n = len(gs)

    def body(*refs):
        ins, outs, send_sems, recv_sems = refs[:n], refs[n:2 * n], refs[2 * n], refs[2 * n + 1]
        x, y, c = _place()
        copies = []
        for i in range(n):
            _, theirs = _halves(c, gs[i].shape[1], 8)
            cp = pltpu.make_async_remote_copy(src_ref=ins[i].at[:, theirs], dst_ref=outs[i], send_sem=send_sems.at[i],
                                              recv_sem=recv_sems.at[i], device_id=(x, y, 1 - c), device_id_type=MESH)
            cp.start()
            copies.append(cp)
        for cp in copies:
            cp.wait()

    return pl.pallas_call(
        body, name="swap_halves", in_specs=[HBM] * n, out_specs=[HBM] * n,
        out_shape=[jax.ShapeDtypeStruct((g.shape[0], g.shape[1] // 2, g.shape[2]), g.dtype) for g in gs],
        scratch_shapes=[pltpu.SemaphoreType.DMA((n,)), pltpu.SemaphoreType.DMA((n,))],
    )(*gs)


def _scatter_chips(ps):
    n = len(ps)

    def body(*refs):
        ins, outs, send_sems, recv_sems = refs[:n], refs[n:2 * n], refs[2 * n], refs[2 * n + 1]
        x, y, c = _place()
        copies = []
        for i in range(n):
            for j, (px, py) in enumerate(_other_chips(x, y)):
                cp = pltpu.make_async_remote_copy(src_ref=ins[i].at[2 * px + py], dst_ref=outs[i].at[j],
                                                  send_sem=send_sems.at[3 * i + j], recv_sem=recv_sems.at[3 * i + j],
                                                  device_id=(px, py, c), device_id_type=MESH)
                cp.start()
                copies.append(cp)
        for cp in copies:
            cp.wait()

    return pl.pallas_call(
        body, name="scatter_chips", in_specs=[HBM] * n, out_specs=[HBM] * n,
        out_shape=[jax.ShapeDtypeStruct((N_SHARD - 1,) + v.shape[1:], v.dtype) for v in ps],
        scratch_shapes=[pltpu.SemaphoreType.DMA((3 * n,)), pltpu.SemaphoreType.DMA((3 * n,))],
    )(*ps)


def _join_halves(vs):
    n = len(vs)

    def body(*refs):
        ins, outs, local_sems, send_sems, recv_sems = refs[:n], refs[n:2 * n], *refs[2 * n:]
        x, y, c = _place()
        started = []
        for i in range(n):
            mine, _ = _halves(c, 2 * vs[i].shape[0], 8)
            local = pltpu.make_async_copy(ins[i], outs[i].at[mine], local_sems.at[i])
            local.start()
            cp = pltpu.make_async_remote_copy(src_ref=ins[i], dst_ref=outs[i].at[mine], send_sem=send_sems.at[i],
                                              recv_sem=recv_sems.at[i], device_id=(x, y, 1 - c), device_id_type=MESH)
            cp.start()
            started.append((local, cp))
        for i in range(n):
            _, theirs = _halves(c, 2 * vs[i].shape[0], 8)
            pltpu.make_async_remote_copy(src_ref=ins[i], dst_ref=outs[i].at[theirs], send_sem=send_sems.at[i],
                                         recv_sem=recv_sems.at[i], device_id=(x, y, 1 - c),
                                         device_id_type=MESH).wait_recv()
        for local, cp in started:
            cp.wait_send()
            local.wait()

    return pl.pallas_call(
        body, name="join_halves", in_specs=[HBM] * n, out_specs=[HBM] * n,
        out_shape=[jax.ShapeDtypeStruct((2 * v.shape[0], v.shape[1]), v.dtype) for v in vs],
        scratch_shapes=[pltpu.SemaphoreType.DMA((n,)), pltpu.SemaphoreType.DMA((n,)), pltpu.SemaphoreType.DMA((n,))],
    )(*vs)


def _allreduce_small(v):
    rows = v.shape[0]

    def body(v_ref, o_ref, buf_ref, send_sems, recv_sems):
        x, y, c = _place()
        me = 4 * x + 2 * y + c
        buf_ref[me] = v_ref[...]
        copies = []
        for rel in range(1, 8):
            peer = (x ^ (rel >> 2), y ^ ((rel >> 1) & 1), c ^ (rel & 1))
            cp = pltpu.make_async_remote_copy(src_ref=v_ref, dst_ref=buf_ref.at[me], send_sem=send_sems.at[rel - 1],
                                              recv_sem=recv_sems.at[rel - 1], device_id=peer, device_id_type=MESH)
            cp.start()
            copies.append(cp)
        for rel in range(1, 8):
            peer_id = me ^ rel
            pltpu.make_async_remote_copy(src_ref=v_ref, dst_ref=buf_ref.at[peer_id], send_sem=send_sems.at[rel - 1],
                                         recv_sem=recv_sems.at[rel - 1], device_id=(x, y, c),
                                         device_id_type=MESH).wait_recv()
        for cp in copies:
            cp.wait_send()
        acc = buf_ref[0]
        for i in range(1, 8):
            acc = acc + buf_ref[i]
        o_ref[...] = acc

    return pl.pallas_call(
        body, name="allreduce_small",
        in_specs=[pl.BlockSpec(memory_space=pltpu.VMEM)], out_specs=pl.BlockSpec(memory_space=pltpu.VMEM),
        out_shape=jax.ShapeDtypeStruct(v.shape, F32),
        scratch_shapes=[pltpu.VMEM((8, rows, LANES), F32), pltpu.SemaphoreType.DMA((7,)), pltpu.SemaphoreType.DMA((7,))],
    )(v)


def _add_pair(g, got, c, name):
    _, rh, cols = got.shape
    rb = _tile(rh, 256)
    nrb = rh // rb

    def body(c_ref, g_ref, got_ref, o_ref):
        o_ref[...] = g_ref[...] + got_ref[...]

    spec = pl.BlockSpec((1, rb, cols), lambda j, i, c_ref: (j, i, 0))
    return pl.pallas_call(
        body, name=name,
        grid_spec=pltpu.PrefetchScalarGridSpec(
            num_scalar_prefetch=1, grid=(N_SHARD, nrb),
            in_specs=[pl.BlockSpec((1, rb, cols), lambda j, i, c_ref: (j, c_ref[0] * nrb + i, 0)), spec],
            out_specs=spec),
        out_shape=jax.ShapeDtypeStruct(got.shape, F32),
        compiler_params=_params(("parallel", "parallel")))(c.reshape(1), g, got)


def _add_chips(pair, landed, k, name):
    _, rh, cols = pair.shape
    rb = _tile(rh, 256)

    def body(k_ref, p_ref, l_ref, o_ref):
        o_ref[...] = p_ref[0] + l_ref[0] + l_ref[1] + l_ref[2]

    return pl.pallas_call(
        body, name=name,
        grid_spec=pltpu.PrefetchScalarGridSpec(
            num_scalar_prefetch=1, grid=(rh // rb,),
            in_specs=[pl.BlockSpec((1, rb, cols), lambda i, k_ref: (k_ref[0], i, 0)),
                      pl.BlockSpec((N_SHARD - 1, rb, cols), lambda i, k_ref: (0, i, 0))],
            out_specs=pl.BlockSpec((rb, cols), lambda i, k_ref: (i, 0))),
        out_shape=jax.ShapeDtypeStruct((rh, cols), F32),
        compiler_params=_params(("parallel",)))(k.reshape(1), pair, landed)


def _adamw(wv, g, m, v, name):
    rows, cols = wv.shape
    rb = _tile(rows, 256)
    c1 = 1.0 - ADAM_B1 ** ADAM_STEP
    c2 = 1.0 - ADAM_B2 ** ADAM_STEP

    def body(w_ref, g_ref, m_ref, v_ref, d_ref, nm_ref, nv_ref):
        gv = g_ref[...]
        nm = ADAM_B1 * m_ref[...] + (1.0 - ADAM_B1) * gv
        nv = ADAM_B2 * v_ref[...] + (1.0 - ADAM_B2) * (gv * gv)
        nm_ref[...] = nm
        nv_ref[...] = nv
        d_ref[...] = -ADAM_LR * ((nm / c1) / (jnp.sqrt(nv / c2) + ADAM_EPS) + ADAM_WD * w_ref[...])

    spec = pl.BlockSpec((rb, cols), lambda i: (i, 0))
    return pl.pallas_call(
        body, name=name, grid=(rows // rb,), in_specs=[spec] * 4, out_specs=[spec] * 3,
        out_shape=[jax.ShapeDtypeStruct((rows, cols), F32)] * 3,
        compiler_params=_params(("parallel",)))(wv, g, m, v)


def _pack(flats):
    cat = jnp.concatenate([f.reshape(-1) for f in flats])
    n = cat.shape[0]
    rows = -(-n // (8 * LANES)) * 8
    return jnp.pad(cat, (0, rows * LANES - n)).reshape(rows, LANES)


def _unpack(packed, shapes):
    flat = packed.reshape(-1)
    out, off = [], 0
    for shp in shapes:
        n = 1
        for dim in shp:
            n *= dim
        out.append(flat[off:off + n].reshape(shp))
        off += n
    return out


def kernel(x, p, norm_g, w_in, conv_w, conv_b, dt_bias, a_log, d_skip, gnorm_g, pool_mix_w, pool_mix_b, pool_scale, w_branch_a, w_branch_b, w_out, ple_norm_g, w_ple_gate, w_ple_up, final_g, loss_target, m_norm_g, m_w_in, m_conv_w, m_conv_b, m_dt_bias, m_a_log, m_d_skip, m_gnorm_g, m_pool_mix_w, m_pool_mix_b, m_pool_scale, m_w_branch_a, m_w_branch_b, m_w_out, m_ple_norm_g, m_w_ple_gate, m_w_ple_up, m_final_g, v_norm_g, v_w_in, v_conv_w, v_conv_b, v_dt_bias, v_a_log, v_d_skip, v_gnorm_g, v_pool_mix_w, v_pool_mix_b, v_pool_scale, v_w_branch_a, v_w_branch_b, v_w_out, v_ple_norm_g, v_w_ple_gate, v_w_ple_up, v_final_g):
    wts = dict(norm_g=norm_g, w_in=w_in, conv_w=conv_w, conv_b=conv_b, dt_bias=dt_bias, a_log=a_log, d_skip=d_skip,
               gnorm_g=gnorm_g, pool_mix_w=pool_mix_w, pool_mix_b=pool_mix_b, pool_scale=pool_scale,
               w_branch_a=w_branch_a, w_branch_b=w_branch_b, w_out=w_out, ple_norm_g=ple_norm_g,
               w_ple_gate=w_ple_gate, w_ple_up=w_ple_up, final_g=final_g)
    mom_m = dict(norm_g=m_norm_g, w_in=m_w_in, conv_w=m_conv_w, conv_b=m_conv_b, dt_bias=m_dt_bias, a_log=m_a_log,
                 d_skip=m_d_skip, gnorm_g=m_gnorm_g, pool_mix_w=m_pool_mix_w, pool_mix_b=m_pool_mix_b,
                 pool_scale=m_pool_scale, w_branch_a=m_w_branch_a, w_branch_b=m_w_branch_b, w_out=m_w_out,
                 ple_norm_g=m_ple_norm_g, w_ple_gate=m_w_ple_gate, w_ple_up=m_w_ple_up, final_g=m_final_g)
    mom_v = dict(norm_g=v_norm_g, w_in=v_w_in, conv_w=v_conv_w, conv_b=v_conv_b, dt_bias=v_dt_bias, a_log=v_a_log,
                 d_skip=v_d_skip, gnorm_g=v_gnorm_g, pool_mix_w=v_pool_mix_w, pool_mix_b=v_pool_mix_b,
                 pool_scale=v_pool_scale, w_branch_a=v_w_branch_a, w_branch_b=v_w_branch_b, w_out=v_w_out,
                 ple_norm_g=v_ple_norm_g, w_ple_gate=v_w_ple_gate, w_ple_up=v_w_ple_up, final_g=v_final_g)
    c = lax.axis_index("c")
    k = 2 * lax.axis_index("x") + lax.axis_index("y")
    flat2 = lambda a: a.reshape(-1, a.shape[-1])

    gathered = _gather_weights([flat2(wts[n]).astype(BF16) for n in BIG], [flat2(conv_w)])
    wg = dict(zip(BIG + ("conv_w",), gathered))

    loss, grad_x, grads = _local_step(x, p[0], loss_target, wg, {n: wts[n] for n in SMALL})

    g_big = [grads[n] for n in BIG]
    got = _swap_halves(g_big)
    pairs = [_add_pair(g, o, c, "add_pair_" + n) for n, g, o in zip(BIG, g_big, got)]
    landed = _scatter_chips(pairs)
    half_sums = [_add_chips(pr, ld, k, "add_chips_" + n) for n, pr, ld in zip(BIG, pairs, landed)]
    g_shards = dict(zip(BIG, _join_halves(half_sums)))

    conv_shape = flat2(conv_w).shape
    small_sum = _allreduce_small(_pack([grads[n] for n in SMALL] + [grads["conv_w"], loss]))
    small_shapes = [wts[n].shape for n in SMALL] + [(N_SHARD,) + conv_shape, (1,)]
    small_g = _unpack(small_sum, small_shapes)
    g_conv = lax.dynamic_index_in_dim(small_g[-2], k, axis=0, keepdims=False)

    outs = {}
    for n in BIG:
        delta, new_m, new_v = _adamw(flat2(wts[n]), g_shards[n], flat2(mom_m[n]), flat2(mom_v[n]), "adamw_" + n)
        for kind, val in zip(("grad", "delta", "new_m", "new_v"), (g_shards[n], delta, new_m, new_v)):
            outs[kind, n] = val.reshape(wts[n].shape)
    names = SMALL + ("conv_w",)
    g_sm = _pack(small_g[:len(SMALL)] + [g_conv])
    sm = _adamw(_pack([wts[n] for n in names]), g_sm, _pack([mom_m[n] for n in names]),
                _pack([mom_v[n] for n in names]), "adamw_small")
    sm_shapes = [wts[n].shape for n in names]
    for kind, val in zip(("grad", "delta", "new_m", "new_v"), (g_sm,) + tuple(sm)):
        for n, piece in zip(names, _unpack(val, sm_shapes)):
            outs[kind, n] = piece
    return (small_g[-1][0], grad_x, *[outs[kind, n] for kind in ("grad", "delta", "new_m", "new_v") for n in WEIGHTS])
```

```python
import functools

import jax
import jax.numpy as jnp
from jax import lax
from jax.experimental import pallas as pl
from jax.experimental.pallas import tpu as pltpu

F32 = jnp.float32
BF16 = jnp.bfloat16
HIGHEST = lax.Precision.HIGHEST
MESH = pl.DeviceIdType.MESH

EPS = 1e-6
HEAD_DIM = 64
SSM_GROUPS = 4
D_STATE = 128
CONV_WIDTH = 4
CHUNK = 128
N_POOL = 4
LANES = 128
N_SHARD = 4

ADAM_LR = 0.001
ADAM_B1 = 0.9
ADAM_B2 = 0.999
ADAM_EPS = 1e-08
ADAM_WD = 0.01
ADAM_STEP = 10

BIG = ("w_in", "pool_mix_w", "w_branch_a", "w_branch_b", "w_out", "w_ple_gate", "w_ple_up")
SMALL = ("norm_g", "conv_b", "dt_bias", "a_log", "d_skip", "gnorm_g", "pool_mix_b", "pool_scale",
         "ple_norm_g", "final_g")
WEIGHTS = ("norm_g", "w_in", "conv_w", "conv_b", "dt_bias", "a_log", "d_skip", "gnorm_g", "pool_mix_w",
           "pool_mix_b", "pool_scale", "w_branch_a", "w_branch_b", "w_out", "ple_norm_g", "w_ple_gate",
           "w_ple_up", "final_g")


def _params(sem=None, vmem_mb=56):
    kw = dict(vmem_limit_bytes=vmem_mb << 20)
    if sem is not None:
        kw["dimension_semantics"] = sem
    return pltpu.CompilerParams(**kw)


def _sigmoid(v):
    return 1.0 / (1.0 + jnp.exp(-v))


def _softplus(v):
    return jnp.maximum(v, 0.0) + jnp.log1p(jnp.exp(-jnp.abs(v)))


def _bdot(a, b):
    return jnp.dot(a.astype(BF16), b.astype(BF16), preferred_element_type=F32)


def _bdot_nt(a, b):
    return lax.dot_general(a.astype(BF16), b.astype(BF16), (((1,), (1,)), ((), ())), preferred_element_type=F32)


def _bdot_tn(a, b):
    return lax.dot_general(a.astype(BF16), b.astype(BF16), (((0,), (0,)), ((), ())), preferred_element_type=F32)


def _col_block(col0, width):
    assert col0 % width == 0, (col0, width)
    return col0 // width


def _tile(n, cap):
    if n <= cap:
        return n
    best = None
    for cand in range(8, cap + 1, 8):
        if n % cand == 0:
            best = cand
    assert best is not None, (n, cap)
    return best


def _shift_down(v, j, row):
    return jnp.where(row >= j, pltpu.roll(v, j, 0), 0.0)


def _shift_up(v, j, row):
    n = v.shape[0]
    return jnp.where(row < n - j, pltpu.roll(v, n - j, 0), 0.0)


def _mm(a, w, name, tm=1024, tn=1024):
    t, k = a.shape
    tm = min(tm, t)
    blocked = w.ndim == 3
    if blocked:
        nblk, _, tn = w.shape
        n = nblk * tn
        w_spec = pl.BlockSpec((1, k, tn), lambda i, j: (j, 0, 0))
    else:
        n = w.shape[1]
        tn = min(tn, n)
        w_spec = pl.BlockSpec((k, tn), lambda i, j: (0, j))

    def body(a_ref, w_ref, o_ref):
        wv = w_ref[0] if blocked else w_ref[...]
        o_ref[...] = _bdot(a_ref[...], wv)

    return pl.pallas_call(
        body, name=name, grid=(t // tm, n // tn),
        in_specs=[pl.BlockSpec((tm, k), lambda i, j: (i, 0)), w_spec],
        out_specs=pl.BlockSpec((tm, tn), lambda i, j: (i, j)),
        out_shape=jax.ShapeDtypeStruct((t, n), F32),
        compiler_params=_params(("parallel", "parallel")))(a, w)


def _mm_nt(a, w, name, tm=1024, tk=1024):
    t, k = a.shape
    n = w.shape[0]
    tm, tk = min(tm, t), min(tk, k)

    def body(a_ref, w_ref, o_ref):
        kk = pl.program_id(1)
        part = _bdot_nt(a_ref[...], w_ref[...])

        @pl.when(kk == 0)
        def _():
            o_ref[...] = part

        @pl.when(kk > 0)
        def _():
            o_ref[...] += part

    return pl.pallas_call(
        body, name=name, grid=(t // tm, k // tk),
        in_specs=[pl.BlockSpec((tm, tk), lambda i, j: (i, j)), pl.BlockSpec((n, tk), lambda i, j: (0, j))],
        out_specs=pl.BlockSpec((tm, n), lambda i, j: (i, 0)),
        out_shape=jax.ShapeDtypeStruct((t, n), F32),
        compiler_params=_params(("parallel", "arbitrary")))(a, w)


def _mm_tn(a, b, name, tn=1024, tk=1024, col_blocks=False):
    t, m = a.shape
    n = b.shape[1]
    tn, tk = min(tn, n), min(tk, t)

    def body(a_ref, b_ref, o_ref):
        kk = pl.program_id(1)
        part = _bdot_tn(a_ref[...], b_ref[...])
        part = part[None] if col_blocks else part

        @pl.when(kk == 0)
        def _():
            o_ref[...] = part

        @pl.when(kk > 0)
        def _():
            o_ref[...] += part

    if col_blocks:
        out_spec = pl.BlockSpec((1, m, tn), lambda j, kk: (j, 0, 0))
        out_shape = jax.ShapeDtypeStruct((n // tn, m, tn), F32)
    else:
        out_spec = pl.BlockSpec((m, tn), lambda j, kk: (0, j))
        out_shape = jax.ShapeDtypeStruct((m, n), F32)
    return pl.pallas_call(
        body, name=name, grid=(n // tn, t // tk),
        in_specs=[pl.BlockSpec((tk, m), lambda j, kk: (kk, 0)), pl.BlockSpec((tk, tn), lambda j, kk: (kk, j))],
        out_specs=out_spec, out_shape=out_shape,
        compiler_params=_params(("parallel", "arbitrary")))(a, b)


def _w_in_pieces(d, di, dc, nh, shard_w):
    pgd = d // N_POOL
    o_dt, o_u = di + dc, di + dc + nh
    o_zp, o_ga, o_gb = o_u + d, o_u + 2 * d, o_u + 3 * d
    c_z, c_uz = 2 * d, 2 * d + di + dc
    runs = [(False, 0, o_ga, d), (False, d, o_gb, d), (False, c_z, 0, di + dc), (True, 0, o_dt, nh)]
    for g in range(N_POOL):
        runs.append((False, c_uz + 2 * g * pgd, o_u + g * pgd, pgd))
        runs.append((False, c_uz + (2 * g + 1) * pgd, o_zp + g * pgd, pgd))
    pieces = []
    for is_dt, dst, src, n in runs:
        while n > 0:
            k, off = divmod(src, shard_w)
            m = min(n, shard_w - off)
            pieces.append((is_dt, dst, k, off, m))
            dst, src, n = dst + m, src + m, n - m
    return pieces


def _regroup_w_in(w_sh, d, di, dc, nh, rb=256):
    _, rows, sw = w_sh.shape
    n_main = 4 * d + di + dc
    pieces = _w_in_pieces(d, di, dc, nh, sw)
    rb = min(rb, rows)

    def body(w_ref, main_ref, dt_ref):
        dt_ref[...] = jnp.zeros_like(dt_ref)
        for is_dt, dst, k, off, m in pieces:
            out = dt_ref if is_dt else main_ref
            out[:, dst:dst + m] = w_ref[k, :, off:off + m]

    return pl.pallas_call(
        body, name="regroup_w_in", grid=(rows // rb,),
        in_specs=[pl.BlockSpec((N_SHARD, rb, sw), lambda i: (0, i, 0))],
        out_specs=[pl.BlockSpec((rb, n_main), lambda i: (i, 0)), pl.BlockSpec((rb, LANES), lambda i: (i, 0))],
        out_shape=[jax.ShapeDtypeStruct((rows, n_main), w_sh.dtype), jax.ShapeDtypeStruct((rows, LANES), w_sh.dtype)],
        compiler_params=_params(("parallel",)))(w_sh)


def _ungroup_w_in(d_main, d_dt, d, di, dc, nh, rb=128):
    rows, n_main = d_main.shape
    sw = (n_main + nh) // N_SHARD
    pieces = _w_in_pieces(d, di, dc, nh, sw)
    rb = min(rb, rows)

    def body(main_ref, dt_ref, o_ref):
        for is_dt, dst, k, off, m in pieces:
            src = dt_ref if is_dt else main_ref
            o_ref[k, :, off:off + m] = src[:, dst:dst + m]

    return pl.pallas_call(
        body, name="ungroup_w_in", grid=(rows // rb,),
        in_specs=[pl.BlockSpec((rb, n_main), lambda i: (i, 0)), pl.BlockSpec((rb, LANES), lambda i: (i, 0))],
        out_specs=pl.BlockSpec((N_SHARD, rb, sw), lambda i: (0, i, 0)),
        out_shape=jax.ShapeDtypeStruct((N_SHARD, rows, sw), F32),
        compiler_params=_params(("parallel",)))(d_main, d_dt)


def _inproj(x2, norm_g, w_main, w_dt, tm=1024, tn=1024):
    t, d = x2.shape
    n = w_main.shape[1]
    tm, tn = min(tm, t), min(tn, n)

    def body(x_ref, g_ref, w_ref, wdt_ref, proj_ref, dt_ref, h_ref):
        @pl.when(pl.program_id(1) == 0)
        def _():
            xv = x_ref[...]
            r = lax.rsqrt(jnp.mean(xv * xv, axis=-1, keepdims=True) + EPS)
            h = (xv * r * g_ref[...]).astype(BF16)
            h_ref[...] = h
            dt_ref[...] = jnp.dot(h, wdt_ref[...].astype(BF16), preferred_element_type=F32)

        proj_ref[...] = jnp.dot(h_ref[...], w_ref[...].astype(BF16), preferred_element_type=F32)

    return pl.pallas_call(
        body, name="inproj", grid=(t // tm, n // tn),
        in_specs=[pl.BlockSpec((tm, d), lambda i, j: (i, 0)), pl.BlockSpec((1, d), lambda i, j: (0, 0)),
                  pl.BlockSpec((d, tn), lambda i, j: (0, j)), pl.BlockSpec((d, LANES), lambda i, j: (0, 0))],
        out_specs=[pl.BlockSpec((tm, tn), lambda i, j: (i, j)), pl.BlockSpec((tm, LANES), lambda i, j: (i, 0)),
                   pl.BlockSpec((tm, d), lambda i, j: (i, 0))],
        out_shape=[jax.ShapeDtypeStruct((t, n), F32), jax.ShapeDtypeStruct((t, LANES), F32),
                   jax.ShapeDtypeStruct((t, d), BF16)],
        compiler_params=_params(("parallel", "arbitrary")))(x2, norm_g, w_main, w_dt)


def _conv_w_spec(conv_w, cb, j_axis):
    sw = conv_w.shape[2]
    assert sw % cb == 0, (sw, cb)
    per = sw // cb
    return N_SHARD * per, pl.BlockSpec((1, CONV_WIDTH, cb), lambda *ij: (ij[j_axis] // per, 0, ij[j_axis] % per))


def _conv_taps(v, w_ref, b_ref, row):
    shifted = [v] + [_shift_down(v, j, row) for j in range(1, CONV_WIDTH)]
    acc = b_ref[...] + shifted[0] * w_ref[0, CONV_WIDTH - 1:CONV_WIDTH, :]
    for j in range(1, CONV_WIDTH):
        acc = acc + shifted[j] * w_ref[0, CONV_WIDTH - 1 - j:CONV_WIDTH - j, :]
    return acc, shifted


def _conv_fwd(proj, conv_w, conv_b, nb, s, col0, cb=256):
    n_blk, w_spec = _conv_w_spec(conv_w, cb, 1)
    blk0 = _col_block(col0, cb)

    def body(x_ref, w_ref, b_ref, o_ref):
        v = x_ref[...]
        row = lax.broadcasted_iota(jnp.int32, v.shape, 0)
        acc, _ = _conv_taps(v, w_ref, b_ref, row)
        o_ref[...] = acc * _sigmoid(acc)

    return pl.pallas_call(
        body, name="conv_fwd", grid=(nb, n_blk),
        in_specs=[pl.BlockSpec((s, cb), lambda b, j: (b, blk0 + j)), w_spec, pl.BlockSpec((1, cb), lambda b, j: (0, j))],
        out_specs=pl.BlockSpec((s, cb), lambda b, j: (b, j)),
        out_shape=jax.ShapeDtypeStruct((nb * s, n_blk * cb), F32),
        compiler_params=_params(("parallel", "parallel")))(proj, conv_w, conv_b)


def _ssd_consts(di):
    r = lax.broadcasted_iota(jnp.int32, (CHUNK, CHUNK), 0)
    c = lax.broadcasted_iota(jnp.int32, (CHUNK, CHUNK), 1)
    tril = (r >= c).astype(F32)
    head = lax.broadcasted_iota(jnp.int32, (LANES, di), 0)
    chan = lax.broadcasted_iota(jnp.int32, (LANES, di), 1) // HEAD_DIM
    expand = (head == chan).astype(BF16)
    return tril, expand, expand.T


def _expand(v, e):
    v1 = v.astype(BF16)
    r1 = v - v1.astype(F32)
    v2 = r1.astype(BF16)
    v3 = (r1 - v2.astype(F32)).astype(BF16)
    dot = lambda a: jnp.dot(a, e, preferred_element_type=F32)
    return dot(v1) + dot(v2) + dot(v3)


def _head_sum(t, et, terms=2):
    acc = None
    for _ in range(terms):
        tb = t.astype(BF16)
        part = jnp.dot(tb, et, preferred_element_type=F32)
        acc = part if acc is None else acc + part
        t = t - tb.astype(F32)
    return acc


def _ssd_scalars(dtr_ref, dtb_ref, alog_ref, tri):
    dtpre = dtr_ref[...] + dtb_ref[...]
    dt = _softplus(dtpre)
    a_neg = -jnp.exp(alog_ref[...])
    a_dt = dt * a_neg
    a_cs = jnp.dot(tri, a_dt, precision=HIGHEST, preferred_element_type=F32)
    a_cst = lax.dot_general(a_dt, tri, (((0,), (1,)), ((), ())), precision=HIGHEST, preferred_element_type=F32)
    return dtpre, dt, a_neg, a_cs, a_cst


def _ssd_fwd(proj, xbc, dtraw, dtb, alog, dskx, gn, nb, s, di, z_col0):
    t = nb * s
    nc = s // CHUNK
    hpg = di // HEAD_DIM // SSM_GROUPS
    gw = di // SSM_GROUPS
    gn_w = SSM_GROUPS * D_STATE
    b_blk = _col_block(di, gn_w)
    z_blk = _col_block(z_col0, di)
    L, P, N = CHUNK, HEAD_DIM, D_STATE
    tril, expand, _ = _ssd_consts(di)

    def body(z_ref, x_ref, b_ref, c_ref, dtr_ref, dtb_ref, alog_ref, dskx_ref, gn_ref, tril_ref, e_ref,
             ypre_ref, yan_ref, hp_ref, st_ref, yd_ref, xdt_ref):
        @pl.when(pl.program_id(1) == 0)
        def _():
            st_ref[...] = jnp.zeros_like(st_ref)

        hp_ref[0] = st_ref[...]
        tri = tril_ref[...]
        _, dt, _, a_cs, a_cst = _ssd_scalars(dtr_ref, dtb_ref, alog_ref, tri)
        ev = e_ref[...]
        a_exp = _expand(a_cs, ev)
        xv = x_ref[...]
        xdt = xv * _expand(dt, ev)
        xdt_ref[...] = xdt
        a_last = a_exp[L - 1:L, :]
        xe = xdt * jnp.exp(a_last - a_exp)
        ea = jnp.exp(a_exp)
        e_last = jnp.exp(a_last)
        lower = tri > 0.5
        for g in range(SSM_GROUPS):
            gs = slice(g * gw, (g + 1) * gw)
            bg = b_ref[:, g * N:(g + 1) * N].astype(BF16)
            cg = c_ref[:, g * N:(g + 1) * N].astype(BF16)
            gm = _bdot_nt(cg, bg)
            ht = st_ref[:, gs]
            ch = _bdot(cg, ht)
            for e in range(hpg):
                h = g * hpg + e
                hs = slice(h * P, (h + 1) * P)
                decay = jnp.where(lower, jnp.exp(a_cs[:, h:h + 1] - a_cst[h:h + 1, :]), 0.0)
                yd_ref[:, hs] = _bdot(gm * decay, xdt_ref[:, hs])
            st_ref[:, gs] = ht * e_last[:, gs] + _bdot_tn(bg, xe[:, gs])
            ypre = yd_ref[:, gs] + ea[:, gs] * ch + xv[:, gs] * dskx_ref[:, gs]
            ypre_ref[:, gs] = ypre
            zv = z_ref[:, gs]
            v = ypre * zv * _sigmoid(zv)
            r = lax.rsqrt(jnp.mean(v * v, axis=-1, keepdims=True) + EPS)
            yan_ref[:, gs] = v * r * gn_ref[:, gs]

    row = lambda b, c: b * nc + c
    vec = lambda w: pl.BlockSpec((1, w), lambda b, c: (0, 0))
    return pl.pallas_call(
        body, name="ssd_fwd", grid=(nb, nc),
        in_specs=[pl.BlockSpec((L, di), lambda b, c: (row(b, c), z_blk)),
                  pl.BlockSpec((L, di), lambda b, c: (row(b, c), 0)),
                  pl.BlockSpec((L, gn_w), lambda b, c: (row(b, c), b_blk)),
                  pl.BlockSpec((L, gn_w), lambda b, c: (row(b, c), b_blk + 1)),
                  pl.BlockSpec((L, LANES), lambda b, c: (row(b, c), 0)),
                  vec(LANES), vec(LANES), vec(di), vec(di),
                  pl.BlockSpec((L, L), lambda b, c: (0, 0)),
                  pl.BlockSpec((LANES, di), lambda b, c: (0, 0))],
        out_specs=[pl.BlockSpec((L, di), lambda b, c: (row(b, c), 0)),
                   pl.BlockSpec((L, di), lambda b, c: (row(b, c), 0)),
                   pl.BlockSpec((1, N, di), lambda b, c: (row(b, c), 0, 0))],
        out_shape=[jax.ShapeDtypeStruct((t, di), F32), jax.ShapeDtypeStruct((t, di), F32),
                   jax.ShapeDtypeStruct((nb * nc, N, di), F32)],
        scratch_shapes=[pltpu.VMEM((N, di), F32), pltpu.VMEM((L, di), F32), pltpu.VMEM((L, di), F32)],
        compiler_params=_params(("parallel", "arbitrary")))(
            proj, xbc, xbc, xbc, dtraw, dtb, alog, dskx, gn, tril, expand)


def _pool_sum(v, g, row, shift):
    s2 = v + shift(v, 1, row)
    s4 = s2 + shift(s2, 2, row)
    s8 = s4 + shift(s4, 4, row)
    s16 = s8 + shift(s8, 8, row)
    return jnp.where(g == 0, s2, jnp.where(g == 1, s4, jnp.where(g == 2, s8, s16)))


def _pool_count(g, row):
    return jnp.minimum(row + 1, jnp.left_shift(2, g)).astype(F32)


def _pool_fwd(proj, mix_w, mix_b, scale, nb, s, col0):
    pgd = mix_w.shape[-1]
    blk0 = _col_block(col0, 2 * pgd)

    def body(uz_ref, w_ref, b_ref, sc_ref, o_ref):
        g = pl.program_id(1)
        u = uz_ref[:, :pgd]
        zp = uz_ref[:, pgd:]
        row = lax.broadcasted_iota(jnp.int32, u.shape, 0)
        pooled = _pool_sum(u, g, row, _shift_down) / _pool_count(g, row) - u
        mixed = _bdot(pooled, w_ref[:, 0].reshape(pgd, pgd)) + b_ref[...]
        o_ref[...] = mixed * sc_ref[...] * zp * _sigmoid(zp)

    return pl.pallas_call(
        body, name="pool_fwd", grid=(nb, N_POOL),
        in_specs=[pl.BlockSpec((s, 2 * pgd), lambda b, g: (b, blk0 + g)),
                  pl.BlockSpec((N_SHARD, 1, pgd // N_SHARD, pgd), lambda b, g: (0, g, 0, 0)),
                  pl.BlockSpec((1, pgd), lambda b, g: (0, g)), pl.BlockSpec((1, pgd), lambda b, g: (0, g))],
        out_specs=pl.BlockSpec((s, pgd), lambda b, g: (b, g)),
        out_shape=jax.ShapeDtypeStruct((nb * s, N_POOL * pgd), F32),
        compiler_params=_params(("parallel", "parallel")))(proj, mix_w, mix_b, scale)


def _merge_fwd(ya, yb, proj, col0, tm=512):
    t, d = ya.shape
    tm = min(tm, t)
    blk = _col_block(col0, 2 * d)

    def body(ya_ref, yb_ref, g_ref, o_ref):
        o_ref[...] = _sigmoid(g_ref[:, :d]) * ya_ref[...] + _sigmoid(g_ref[:, d:]) * yb_ref[...]

    row = pl.BlockSpec((tm, d), lambda i: (i, 0))
    return pl.pallas_call(
        body, name="merge_fwd", grid=(t // tm,),
        in_specs=[row, row, pl.BlockSpec((tm, 2 * d), lambda i: (i, blk))],
        out_specs=row, out_shape=jax.ShapeDtypeStruct((t, d), F32),
        compiler_params=_params(("parallel",)))(ya, yb, proj)


def _ple_pre(x2, mo, ple_g, tm=512):
    t, d = x2.shape
    tm = min(tm, t)

    def body(x_ref, mo_ref, g_ref, x1_ref, hn_ref):
        x1 = x_ref[...] + mo_ref[...]
        x1_ref[...] = x1
        r = lax.rsqrt(jnp.mean(x1 * x1, axis=-1, keepdims=True) + EPS)
        hn_ref[...] = (x1 * r * g_ref[...]).astype(BF16)

    row = pl.BlockSpec((tm, d), lambda i: (i, 0))
    return pl.pallas_call(
        body, name="ple_pre", grid=(t // tm,),
        in_specs=[row, row, pl.BlockSpec((1, d), lambda i: (0, 0))],
        out_specs=[row, row],
        out_shape=[jax.ShapeDtypeStruct((t, d), F32), jax.ShapeDtypeStruct((t, d), BF16)],
        compiler_params=_params(("parallel",)))(x2, mo, ple_g)


def _tail(x1, pre, pu, tgt, final_g, tm=512):
    t, d = x1.shape
    tm = min(tm, t)

    def body(x1_ref, pre_ref, pu_ref, tgt_ref, g_ref, dx2_ref, dpre_ref, dpu_ref, loss_ref, dg_ref):
        @pl.when(pl.program_id(0) == 0)
        def _():
            loss_ref[...] = jnp.zeros_like(loss_ref)
            dg_ref[...] = jnp.zeros_like(dg_ref)

        gate = _sigmoid(pre_ref[...])
        pu = pu_ref[...]
        x2 = x1_ref[...] + gate * pu
        r = lax.rsqrt(jnp.mean(x2 * x2, axis=-1, keepdims=True) + EPS)
        xn = x2 * r
        fg = g_ref[...]
        err = xn * fg - tgt_ref[...]
        loss_ref[...] += 0.5 * jnp.sum(jnp.mean(err * err, axis=-1, keepdims=True))
        dy = err * (1.0 / d)
        dg_ref[...] += jnp.sum(dy * xn, axis=0, keepdims=True)
        dxn = dy * fg
        dx2 = r * (dxn - xn * jnp.mean(dxn * xn, axis=-1, keepdims=True))
        dx2_ref[...] = dx2
        dpre_ref[...] = dx2 * pu * gate * (1.0 - gate)
        dpu_ref[...] = dx2 * gate

    row = pl.BlockSpec((tm, d), lambda i: (i, 0))
    return pl.pallas_call(
        body, name="tail", grid=(t // tm,),
        in_specs=[row, row, row, row, pl.BlockSpec((1, d), lambda i: (0, 0))],
        out_specs=[row, row, row, pl.BlockSpec((1, LANES), lambda i: (0, 0)), pl.BlockSpec((1, d), lambda i: (0, 0))],
        out_shape=[jax.ShapeDtypeStruct((t, d), F32)] * 3 + [jax.ShapeDtypeStruct((1, LANES), F32),
                                                             jax.ShapeDtypeStruct((1, d), F32)],
        compiler_params=_params(("arbitrary",)))(x1, pre, pu, tgt, final_g)


def _rms_bwd(xin, dhs, dres, g, name, tm=512):
    t, d = xin.shape
    tm = min(tm, t)
    n_dh = len(dhs)

    def body(*refs):
        x_ref, dh_refs, dres_ref, g_ref, dx_ref, dg_ref = refs[0], refs[1:1 + n_dh], *refs[1 + n_dh:]

        @pl.when(pl.program_id(0) == 0)
        def _():
            dg_ref[...] = jnp.zeros_like(dg_ref)

        xv = x_ref[...]
        dh = dh_refs[0][...]
        for ref in dh_refs[1:]:
            dh = dh + ref[...]
        r = lax.rsqrt(jnp.mean(xv * xv, axis=-1, keepdims=True) + EPS)
        xn = xv * r
        dg_ref[...] += jnp.sum(dh * xn, axis=0, keepdims=True)
        dd = dh * g_ref[...]
        dx_ref[...] = dres_ref[...] + r * (dd - xn * jnp.mean(dd * xn, axis=-1, keepdims=True))

    row = pl.BlockSpec((tm, d), lambda i: (i, 0))
    vec = pl.BlockSpec((1, d), lambda i: (0, 0))
    return pl.pallas_call(
        body, name=name, grid=(t // tm,),
        in_specs=[row] * (2 + n_dh) + [vec],
        out_specs=[row, vec],
        out_shape=[jax.ShapeDtypeStruct((t, d), F32), jax.ShapeDtypeStruct((1, d), F32)],
        compiler_params=_params(("arbitrary",)))(xin, *dhs, dres, g)


def _merge_bwd(dm, ya, yb, proj, col0, n_cols, tm=512):
    t, d = ya.shape
    tm = min(tm, t)
    blk = _col_block(col0, 2 * d)

    def body(dm_ref, ya_ref, yb_ref, g_ref, dya_ref, dyb_ref, dg_ref):
        dm_v = dm_ref[...]
        sa = _sigmoid(g_ref[:, :d])
        sb = _sigmoid(g_ref[:, d:])
        dya_ref[...] = dm_v * sa
        dyb_ref[...] = dm_v * sb
        dg_ref[:, :d] = dm_v * ya_ref[...] * sa * (1.0 - sa)
        dg_ref[:, d:] = dm_v * yb_ref[...] * sb * (1.0 - sb)

    row = pl.BlockSpec((tm, d), lambda i: (i, 0))
    gspec = pl.BlockSpec((tm, 2 * d), lambda i: (i, blk))
    return pl.pallas_call(
        body, name="merge_bwd", grid=(t // tm,),
        in_specs=[row, row, row, gspec],
        out_specs=[row, row, gspec],
        out_shape=[jax.ShapeDtypeStruct((t, d), F32), jax.ShapeDtypeStruct((t, d), F32),
                   jax.ShapeDtypeStruct((t, n_cols), F32)],
        compiler_params=_params(("parallel",)))(dm, ya, yb, proj)


def _pool_bwd(proj, dyb, dproj, mix_w, mix_b, scale, nb, s, col0):
    pgd = mix_w.shape[-1]
    blk0 = _col_block(col0, 2 * pgd)

    def body(uz_ref, dy_ref, _, w_ref, b_ref, sc_ref, duz_ref, dw_ref, db_ref, dsc_ref):
        g = pl.program_id(0)

        @pl.when(pl.program_id(1) == 0)
        def _():
            dw_ref[...] = jnp.zeros_like(dw_ref)
            db_ref[...] = jnp.zeros_like(db_ref)
            dsc_ref[...] = jnp.zeros_like(dsc_ref)

        u = uz_ref[:, :pgd]
        zp = uz_ref[:, pgd:]
        row = lax.broadcasted_iota(jnp.int32, u.shape, 0)
        cnt = _pool_count(g, row)
        pooled = _pool_sum(u, g, row, _shift_down) / cnt - u
        wv = w_ref[:, 0].reshape(pgd, pgd)
        mixed = _bdot(pooled, wv) + b_ref[...]
        sg = _sigmoid(zp)
        sz = zp * sg
        dy = dy_ref[...]
        sc = sc_ref[...]
        dsc_ref[...] += jnp.sum(dy * mixed * sz, axis=0, keepdims=True)
        dmixed = dy * sc * sz
        db_ref[...] += jnp.sum(dmixed, axis=0, keepdims=True)
        dw_ref[:, 0] += _bdot_tn(pooled, dmixed).reshape(N_SHARD, pgd // N_SHARD, pgd)
        dpooled = _bdot_nt(dmixed, wv)
        duz_ref[:, :pgd] = _pool_sum(dpooled / cnt, g, row, _shift_up) - dpooled
        duz_ref[:, pgd:] = dy * mixed * sc * sg * (1.0 + zp * (1.0 - sg))

    uz = pl.BlockSpec((s, 2 * pgd), lambda g, b: (b, blk0 + g))
    vec = pl.BlockSpec((1, pgd), lambda g, b: (0, g))
    wspec = pl.BlockSpec((N_SHARD, 1, pgd // N_SHARD, pgd), lambda g, b: (0, g, 0, 0))
    return pl.pallas_call(
        body, name="pool_bwd", grid=(N_POOL, nb),
        in_specs=[uz, pl.BlockSpec((s, pgd), lambda g, b: (b, g)), pl.BlockSpec(memory_space=pl.ANY), wspec, vec, vec],
        out_specs=[uz, wspec, vec, vec],
        out_shape=[jax.ShapeDtypeStruct(dproj.shape, F32), jax.ShapeDtypeStruct(mix_w.shape, F32),
                   jax.ShapeDtypeStruct(mix_b.shape, F32), jax.ShapeDtypeStruct(scale.shape, F32)],
        input_output_aliases={2: 0},
        compiler_params=_params(("parallel", "arbitrary")))(proj, dyb, dproj, mix_w, mix_b, scale)


def _ssd_bwd(dyan, ypre, proj, xbc, dtraw, hp, dproj, dtb, alog, dskx, gn, nb, s, di, z_col0):
    t = nb * s
    nc = s // CHUNK
    hpg = di // HEAD_DIM // SSM_GROUPS
    gw = di // SSM_GROUPS
    gn_w = SSM_GROUPS * D_STATE
    dc = di + 2 * gn_w
    b_blk = _col_block(di, gn_w)
    z_blk = _col_block(z_col0, di)
    L, P, N = CHUNK, HEAD_DIM, D_STATE
    tril, expand, expand_t = _ssd_consts(di)

    def body(dy_ref, ypre_ref, z_ref, x_ref, b_ref, c_ref, dtr_ref, hp_ref, _, dtb_ref, alog_ref, dskx_ref, gn_ref,
             tril_ref, e_ref, et_ref, dz_ref, ddt_ref, dxbc_ref, dgn_ref, ddsk_ref, dalog_ref, ddtb_ref,
             dst_ref, dyp_ref, xdt_ref, dxm_ref, t1_ref, t2_ref, t3_ref, aux_ref):
        @pl.when((pl.program_id(0) == 0) & (pl.program_id(1) == 0))
        def _():
            dgn_ref[...] = jnp.zeros_like(dgn_ref)
            ddsk_ref[...] = jnp.zeros_like(ddsk_ref)
            dalog_ref[...] = jnp.zeros_like(dalog_ref)
            ddtb_ref[...] = jnp.zeros_like(ddtb_ref)

        @pl.when(pl.program_id(1) == 0)
        def _():
            dst_ref[...] = jnp.zeros_like(dst_ref)

        tri = tril_ref[...]
        dtpre, dt, a_neg, a_cs, a_cst = _ssd_scalars(dtr_ref, dtb_ref, alog_ref, tri)
        ev = e_ref[...]
        a_exp = _expand(a_cs, ev)
        dt_exp = _expand(dt, ev)
        xv = x_ref[...]
        xdt = xv * dt_exp
        xdt_ref[...] = xdt
        a_last = a_exp[L - 1:L, :]
        dte = jnp.exp(a_last - a_exp)
        xe = xdt * dte
        ea = jnp.exp(a_exp)
        e_last = jnp.exp(a_last)
        lower = tri > 0.5
        aux_ref[...] = jnp.zeros_like(aux_ref)
        for g in range(SSM_GROUPS):
            gs = slice(g * gw, (g + 1) * gw)
            zv = z_ref[:, gs]
            yp = ypre_ref[:, gs]
            sg = _sigmoid(zv)
            sz = zv * sg
            vg = yp * sz
            r = lax.rsqrt(jnp.mean(vg * vg, axis=-1, keepdims=True) + EPS)
            vn = vg * r
            dyg = dy_ref[:, gs]
            dgn_ref[:, gs] += jnp.sum(dyg * vn, axis=0, keepdims=True)
            dvn = dyg * gn_ref[:, gs]
            dv = r * (dvn - vn * jnp.mean(dvn * vn, axis=-1, keepdims=True))
            dy = dv * sz
            dyp_ref[:, gs] = dy
            dz_ref[:, gs] = dv * yp * sg * (1.0 + zv * (1.0 - sg))
            bg = b_ref[:, g * N:(g + 1) * N].astype(BF16)
            cg = c_ref[:, g * N:(g + 1) * N].astype(BF16)
            gm = _bdot_nt(cg, bg)
            ht = hp_ref[0, :, gs]
            dht = dst_ref[:, gs]
            bds = _bdot(bg, dht)
            dye = dy * ea[:, gs]
            xe_g = xe[:, gs]
            dcg = _bdot_nt(dye, ht)
            dbg = _bdot_nt(xe_g, dht)
            dst_ref[:, gs] = e_last[:, gs] * dht + _bdot_tn(cg, dye)
            dgm = jnp.zeros((L, L), F32)
            for e in range(hpg):
                h = g * hpg + e
                hs = slice(h * P, (h + 1) * P)
                decay = jnp.where(lower, jnp.exp(a_cs[:, h:h + 1] - a_cst[h:h + 1, :]), 0.0)
                dy_h = dyp_ref[:, hs]
                dgm = dgm + _bdot_nt(dy_h, xdt_ref[:, hs]) * decay
                dxm_ref[:, hs] = _bdot_tn(gm * decay, dy_h)
            dxbc_ref[:, di + g * N:di + (g + 1) * N] = dbg + _bdot_tn(dgm, cg)
            dxbc_ref[:, di + gn_w + g * N:di + gn_w + (g + 1) * N] = dcg + _bdot(dgm, bg)
            dxm = dxm_ref[:, gs]
            x_g = xv[:, gs]
            dskx = dskx_ref[:, gs]
            xeb = xe_g * bds
            dxdt = dxm + dte[:, gs] * bds
            dxbc_ref[:, gs] = dxdt * dt_exp[:, gs] + dy * dskx
            each = ea[:, gs] * _bdot(cg, ht)
            y_diag = yp - x_g * dskx - each
            rnd = lambda v: v.astype(BF16).astype(F32)
            t1_ref[:, gs] = rnd(dy) * y_diag + dy * each - rnd(xdt[:, gs]) * dxm - xeb
            t2_ref[:, gs] = xeb
            t3_ref[:, gs] = dxdt * x_g
            aux_ref[0:1, gs] = jnp.sum(dht * ht, axis=0, keepdims=True)
            aux_ref[1:2, gs] = jnp.sum(dy * x_g, axis=0, keepdims=True)
        etv = et_ref[...]
        w_end = _head_sum(t2_ref[...], etv)
        aux = _head_sum(aux_ref[...], etv)
        rowi = lax.broadcasted_iota(jnp.int32, (L, LANES), 0)
        end = jnp.sum(w_end, axis=0, keepdims=True) + aux[0:1, :] * jnp.exp(a_cs[L - 1:L, :])
        da = _head_sum(t1_ref[...], etv, terms=3) + jnp.where(rowi == L - 1, end, 0.0)
        rc = lax.dot_general(tri, da, (((0,), (0,)), ((), ())), precision=HIGHEST, preferred_element_type=F32)
        ddt = a_neg * rc + _head_sum(t3_ref[...], etv)
        ddtraw = ddt * _sigmoid(dtpre)
        ddt_ref[...] = ddtraw
        ddtb_ref[...] += jnp.sum(ddtraw, axis=0, keepdims=True)
        dalog_ref[...] += jnp.sum(dt * rc, axis=0, keepdims=True) * a_neg
        ddsk_ref[...] += aux[1:2, :]

    row = lambda b, c: b * nc + (nc - 1 - c)
    full = lambda w: pl.BlockSpec((L, w), lambda b, c: (row(b, c), 0))
    zspec = pl.BlockSpec((L, di), lambda b, c: (row(b, c), z_blk))
    vec = lambda w: pl.BlockSpec((1, w), lambda b, c: (0, 0))
    slab = lambda shape: pltpu.VMEM(shape, F32)
    return pl.pallas_call(
        body, name="ssd_bwd", grid=(nb, nc),
        in_specs=[full(di), full(di), zspec, full(di),
                  pl.BlockSpec((L, gn_w), lambda b, c: (row(b, c), b_blk)),
                  pl.BlockSpec((L, gn_w), lambda b, c: (row(b, c), b_blk + 1)),
                  full(LANES),
                  pl.BlockSpec((1, N, di), lambda b, c: (row(b, c), 0, 0)),
                  pl.BlockSpec(memory_space=pl.ANY),
                  vec(LANES), vec(LANES), vec(di), vec(di),
                  pl.BlockSpec((L, L), lambda b, c: (0, 0)),
                  pl.BlockSpec((LANES, di), lambda b, c: (0, 0)),
                  pl.BlockSpec((di, LANES), lambda b, c: (0, 0))],
        out_specs=[zspec, full(LANES), full(dc), vec(di), vec(LANES), vec(LANES), vec(LANES)],
        out_shape=[jax.ShapeDtypeStruct(dproj.shape, F32), jax.ShapeDtypeStruct((t, LANES), F32),
                   jax.ShapeDtypeStruct((t, dc), F32), jax.ShapeDtypeStruct((1, di), F32),
                   jax.ShapeDtypeStruct((1, LANES), F32), jax.ShapeDtypeStruct((1, LANES), F32),
                   jax.ShapeDtypeStruct((1, LANES), F32)],
        scratch_shapes=[slab((N, di)), slab((L, di)), slab((L, di)), slab((L, di)), slab((L, di)), slab((L, di)),
                        slab((L, di)), slab((8, di))],
        input_output_aliases={8: 0},
        compiler_params=_params(("arbitrary", "arbitrary")))(
            dyan, ypre, proj, xbc, xbc, xbc, dtraw, hp, dproj, dtb, alog, dskx, gn, tril, expand, expand_t)


def _conv_bwd(proj, dxbc, dproj, conv_w, conv_b, nb, s, col0, cb=256):
    n_blk, w_spec = _conv_w_spec(conv_w, cb, 0)
    blk0 = _col_block(col0, cb)

    def body(x_ref, dy_ref, _, w_ref, b_ref, dx_ref, dw_ref, db_ref):
        @pl.when(pl.program_id(1) == 0)
        def _():
            dw_ref[...] = jnp.zeros_like(dw_ref)
            db_ref[...] = jnp.zeros_like(db_ref)

        v = x_ref[...]
        row = lax.broadcasted_iota(jnp.int32, v.shape, 0)
        acc, shifted = _conv_taps(v, w_ref, b_ref, row)
        sg = _sigmoid(acc)
        dacc = dy_ref[...] * sg * (1.0 + acc * (1.0 - sg))
        db_ref[...] += jnp.sum(dacc, axis=0, keepdims=True)
        dx = dacc * w_ref[0, CONV_WIDTH - 1:CONV_WIDTH, :]
        for j in range(CONV_WIDTH):
            dw_ref[0, CONV_WIDTH - 1 - j:CONV_WIDTH - j, :] += jnp.sum(dacc * shifted[j], axis=0, keepdims=True)
            if j:
                dx = dx + _shift_up(dacc, j, row) * w_ref[0, CONV_WIDTH - 1 - j:CONV_WIDTH - j, :]
        dx_ref[...] = dx

    return pl.pallas_call(
        body, name="conv_bwd", grid=(n_blk, nb),
        in_specs=[pl.BlockSpec((s, cb), lambda j, b: (b, blk0 + j)), pl.BlockSpec((s, cb), lambda j, b: (b, j)),
                  pl.BlockSpec(memory_space=pl.ANY), w_spec, pl.BlockSpec((1, cb), lambda j, b: (0, j))],
        out_specs=[pl.BlockSpec((s, cb), lambda j, b: (b, blk0 + j)), w_spec, pl.BlockSpec((1, cb), lambda j, b: (0, j))],
        out_shape=[jax.ShapeDtypeStruct(dproj.shape, F32), jax.ShapeDtypeStruct(conv_w.shape, F32),
                   jax.ShapeDtypeStruct(conv_b.shape, F32)],
        input_output_aliases={2: 0},
        compiler_params=_params(("parallel", "arbitrary")))(proj, dxbc, dproj, conv_w, conv_b)


def _local_step(x, p, tgt, wg, small):
    nb, s, d = x.shape
    t = nb * s
    di = N_SHARD * wg["w_branch_a"].shape[1]
    nh = di // HEAD_DIM
    gn_w = SSM_GROUPS * D_STATE
    dc = di + 2 * gn_w
    pgd = d // N_POOL
    x2 = x.reshape(t, d)
    p2 = p.reshape(t, p.shape[-1])
    tgt2 = tgt.reshape(t, d)

    w_main, w_dt = _regroup_w_in(wg["w_in"], d, di, dc, nh)
    c_g, c_z, c_xbc, c_uz = 0, 2 * d, 2 * d + di, 2 * d + di + dc
    n_main = w_main.shape[1]

    pad_h = lambda v: jnp.pad(v.reshape(1, nh).astype(F32), ((0, 0), (0, LANES - nh)))
    dtb, alog = pad_h(small["dt_bias"]), pad_h(small["a_log"])
    dskx = jnp.repeat(small["d_skip"].reshape(1, nh).astype(F32), HEAD_DIM, axis=1)
    vec = lambda v: v.reshape(1, -1).astype(F32)
    norm_g, gn, conv_b = vec(small["norm_g"]), vec(small["gnorm_g"]), vec(small["conv_b"])
    mix_b, scale = vec(small["pool_mix_b"]), vec(small["pool_scale"])
    ple_g, final_g = vec(small["ple_norm_g"]), vec(small["final_g"])
    conv_w = wg["conv_w"]
    mix_w = wg["pool_mix_w"].reshape(N_SHARD, N_POOL, pgd // N_SHARD, pgd)
    rows = lambda v: v.reshape(-1, v.shape[-1])
    wa, wb, wo, wpg = rows(wg["w_branch_a"]), rows(wg["w_branch_b"]), rows(wg["w_out"]), rows(wg["w_ple_gate"])
    wup = wg["w_ple_up"]

    proj, dtraw, h = _inproj(x2, norm_g, w_main, w_dt)
    xbc = _conv_fwd(proj, conv_w, conv_b, nb, s, c_xbc)
    ypre, yan, hp = _ssd_fwd(proj, xbc, dtraw, dtb, alog, dskx, gn, nb, s, di, c_z)
    ybp = _pool_fwd(proj, mix_w, mix_b, scale, nb, s, c_uz)
    ya = _mm(yan, wa, "branch_a")
    yb = _mm(ybp, wb, "branch_b")
    merged = _merge_fwd(ya, yb, proj, c_g)
    mo = _mm(merged, wo, "out_proj")
    x1, hn = _ple_pre(x2, mo, ple_g)
    pre = _mm(hn, wpg, "ple_gate")
    pu = _mm(p2, wup, "ple_up")

    dx2, dpre, dpu, loss, d_final_g = _tail(x1, pre, pu, tgt2, final_g)
    d_wpg = _mm_tn(hn, dpre, "d_w_ple_gate")
    d_wup = _mm_tn(p2, dpu, "d_w_ple_up", tn=wup.shape[-1], col_blocks=True)
    dhn = _mm_nt(dpre, wpg, "d_hn")
    dx1, d_ple_g = _rms_bwd(x1, [dhn], dx2, ple_g, "ple_bwd")
    d_wo = _mm_tn(merged, dx1, "d_w_out")
    dm = _mm_nt(dx1, wo, "d_merged")
    dya, dyb, dproj = _merge_bwd(dm, ya, yb, proj, c_g, n_main)
    d_wa = _mm_tn(yan, dya, "d_w_branch_a")
    d_wb = _mm_tn(ybp, dyb, "d_w_branch_b")
    dyan = _mm_nt(dya, wa, "d_y_a")
    dybp = _mm_nt(dyb, wb, "d_y_b")
    dproj, d_mix_w, d_mix_b, d_scale = _pool_bwd(proj, dybp, dproj, mix_w, mix_b, scale, nb, s, c_uz)
    dproj, ddt, dxbc, d_gn, d_dsk, d_alog, d_dtb = _ssd_bwd(
        dyan, ypre, proj, xbc, dtraw, hp, dproj, dtb, alog, dskx, gn, nb, s, di, c_z)
    dproj, d_conv_w, d_conv_b = _conv_bwd(proj, dxbc, dproj, conv_w, conv_b, nb, s, c_xbc)
    d_wmain = _mm_tn(h, dproj, "d_w_in")
    d_wdt = _mm_tn(h, ddt, "d_w_dt")
    dh_main = _mm_nt(dproj, w_main, "d_h")
    dh_dt = _mm_nt(ddt, w_dt, "d_h_dt")
    gx, d_norm_g = _rms_bwd(x2, [dh_main, dh_dt], dx1, norm_g, "in_bwd")
    d_w_in = _ungroup_w_in(d_wmain, d_wdt, d, di, dc, nh)

    shard_major = lambda v: v.reshape(N_SHARD, v.shape[0] // N_SHARD, v.shape[1])
    grads = dict(norm_g=d_norm_g, w_in=d_w_in, conv_w=d_conv_w, conv_b=d_conv_b, dt_bias=d_dtb[:, :nh],
                 a_log=d_alog[:, :nh], d_skip=d_dsk[:, :nh], gnorm_g=d_gn,
                 pool_mix_w=d_mix_w.reshape(N_SHARD, pgd, pgd), pool_mix_b=d_mix_b, pool_scale=d_scale,
                 w_branch_a=shard_major(d_wa), w_branch_b=shard_major(d_wb), w_out=shard_major(d_wo),
                 ple_norm_g=d_ple_g, w_ple_gate=shard_major(d_wpg), w_ple_up=d_wup, final_g=d_final_g)
    return loss[0, 0], gx.reshape(nb, s, d), grads


def _place():
    return lax.axis_index("x"), lax.axis_index("y"), lax.axis_index("c")


def _other_chips(x, y):
    return [(1 - x, y), (x, 1 - y), (1 - x, 1 - y)]


def _halves(c, rows, align):
    rh = rows // 2
    assert rows % 2 == 0 and rh % align == 0, (rows, align)
    return (pl.ds(pl.multiple_of(c * rh, align), rh), pl.ds(pl.multiple_of((1 - c) * rh, align), rh))


HBM = pl.BlockSpec(memory_space=pl.ANY)


def _into_slot(w2, k, dtype, name):
    rows, cols = w2.shape
    rb = _tile(rows, 256)

    def body(k_ref, w_ref, o_ref):
        o_ref[0] = w_ref[...].astype(dtype)

    return pl.pallas_call(
        body, name=name,
        grid_spec=pltpu.PrefetchScalarGridSpec(
            num_scalar_prefetch=1, grid=(rows // rb,),
            in_specs=[pl.BlockSpec((rb, cols), lambda i, k_ref: (i, 0))],
            out_specs=pl.BlockSpec((1, rb, cols), lambda i, k_ref: (k_ref[0], i, 0))),
        out_shape=jax.ShapeDtypeStruct((N_SHARD, rows, cols), dtype),
        compiler_params=_params(("parallel",)))(k.reshape(1), w2)


def _gather_weights(split, whole):
    n_split, n_all = len(split), len(split) + len(whole)

    def body(*refs):
        bufs = refs[n_all:2 * n_all]
        send_sems, recv_sems = refs[2 * n_all:]
        x, y, c = _place()
        k = 2 * x + y
        chips = _other_chips(x, y)

        def copy(idx, block, to):
            return pltpu.make_async_remote_copy(src_ref=block, dst_ref=block, send_sem=send_sems.at[idx],
                                                recv_sem=recv_sems.at[idx], device_id=to, device_id_type=MESH)

        def block(i, shard, rows):
            return bufs[i].at[shard, rows] if i < n_split else bufs[i].at[shard]

        def sem(i, j):
            return 6 * i + j if i < n_split else 6 * n_split + 3 * (i - n_split) + j

        started = []
        for i in range(n_all):
            mine, _ = _halves(c, bufs[i].shape[1], 16) if i < n_split else (None, None)
            for j, (px, py) in enumerate(chips):
                started.append(copy(sem(i, j), block(i, k, mine), (px, py, c)))
                started[-1].start()
        for i in range(n_all):
            mine, _ = _halves(c, bufs[i].shape[1], 16) if i < n_split else (None, None)
            for j, (px, py) in enumerate(chips):
                landed = block(i, 2 * px + py, mine)
                copy(sem(i, j), landed, (px, py, c)).wait_recv()
                if i < n_split:
                    started.append(copy(sem(i, 3 + j), landed, (x, y, 1 - c)))
                    started[-1].start()
        for i in range(n_split):
            _, theirs = _halves(c, bufs[i].shape[1], 16)
            for j, (px, py) in enumerate(chips):
                copy(sem(i, 3 + j), block(i, 2 * px + py, theirs), (x, y, 1 - c)).wait_recv()
        for cp in started:
            cp.wait_send()

    arrays = list(split) + list(whole)
    n_sem = 6 * n_split + 3 * len(whole)
    return pl.pallas_call(
        body, name="gather_weights",
        in_specs=[HBM] * n_all, out_specs=[HBM] * n_all,
        out_shape=[jax.ShapeDtypeStruct(a.shape, a.dtype) for a in arrays],
        input_output_aliases={i: i for i in range(n_all)},
        scratch_shapes=[pltpu.SemaphoreType.DMA((n_sem,)), pltpu.SemaphoreType.DMA((n_sem,))],
    )(*arrays)


def _swap_halves(gs):
    n = len(gs)

    def body(*refs):
        ins, outs, send_sems, recv_sems = refs[:n], refs[n:2 * n], refs[2 * n], refs[2 * n + 1]
        x, y, c = _place()
        copies = []
        for i in range(n):
            _, theirs = _halves(c, gs[i].shape[1], 8)
            cp = pltpu.make_async_remote_copy(src_ref=ins[i].at[:, theirs], dst_ref=outs[i], send_sem=send_sems.at[i],
                                              recv_sem=recv_sems.at[i], device_id=(x, y, 1 - c), device_id_type=MESH)
            cp.start()
            copies.append(cp)
        for cp in copies:
            cp.wait()

    return pl.pallas_call(
        body, name="swap_halves", in_specs=[HBM] * n, out_specs=[HBM] * n,
        out_shape=[jax.ShapeDtypeStruct((g.shape[0], g.shape[1] // 2, g.shape[2]), g.dtype) for g in gs],
        scratch_shapes=[pltpu.SemaphoreType.DMA((n,)), pltpu.SemaphoreType.DMA((n,))],
    )(*gs)


def _scatter_chips(ps):
    n = len(ps)

    def body(*refs):
        ins, outs, send_sems, recv_sems = refs[:n], refs[n:2 * n], refs[2 * n], refs[2 * n + 1]
        x, y, c = _place()
        copies = []
        for i in range(n):
            for j, (px, py) in enumerate(_other_chips(x, y)):
                cp = pltpu.make_async_remote_copy(src_ref=ins[i].at[2 * px + py], dst_ref=outs[i].at[j],
                                                  send_sem=send_sems.at[3 * i + j], recv_sem=recv_sems.at[3 * i + j],
                                                  device_id=(px, py, c), device_id_type=MESH)
                cp.start()
                copies.append(cp)
        for cp in copies:
            cp.wait()

    return pl.pallas_call(
        body, name="scatter_chips", in_specs=[HBM] * n, out_specs=[HBM] * n,
        out_shape=[jax.ShapeDtypeStruct((N_SHARD - 1,) + v.shape[1:], v.dtype) for v in ps],
        scratch_shapes=[pltpu.SemaphoreType.DMA((3 * n,)), pltpu.SemaphoreType.DMA((3 * n,))],
    )(*ps)


def _join_halves(vs):
    n = len(vs)

    def body(*refs):
        bufs, send_sems, recv_sems = refs[n:2 * n], refs[2 * n], refs[2 * n + 1]
        x, y, c = _place()

        def copy(i, rows):
            return pltpu.make_async_remote_copy(src_ref=bufs[i].at[rows], dst_ref=bufs[i].at[rows],
                                                send_sem=send_sems.at[i], recv_sem=recv_sems.at[i],
                                                device_id=(x, y, 1 - c), device_id_type=MESH)

        halves = [_halves(c, bufs[i].shape[0], 8) for i in range(n)]
        sends = [copy(i, halves[i][0]) for i in range(n)]
        for cp in sends:
            cp.start()
        for i in range(n):
            copy(i, halves[i][1]).wait_recv()
        for cp in sends:
            cp.wait_send()

    return pl.pallas_call(
        body, name="join_halves", in_specs=[HBM] * n, out_specs=[HBM] * n,
        out_shape=[jax.ShapeDtypeStruct(v.shape, v.dtype) for v in vs],
        input_output_aliases={i: i for i in range(n)},
        scratch_shapes=[pltpu.SemaphoreType.DMA((n,)), pltpu.SemaphoreType.DMA((n,))],
    )(*vs)


def _allreduce_small(v):
    rows = v.shape[0]

    def body(v_ref, o_ref, buf_ref, send_sems, recv_sems):
        x, y, c = _place()
        me = 4 * x + 2 * y + c
        buf_ref[me] = v_ref[...]
        copies = []
        for rel in range(1, 8):
            peer = (x ^ (rel >> 2), y ^ ((rel >> 1) & 1), c ^ (rel & 1))
            cp = pltpu.make_async_remote_copy(src_ref=v_ref, dst_ref=buf_ref.at[me], send_sem=send_sems.at[rel - 1],
                                              recv_sem=recv_sems.at[rel - 1], device_id=peer, device_id_type=MESH)
            cp.start()
            copies.append(cp)
        for rel in range(1, 8):
            peer_id = me ^ rel
            pltpu.make_async_remote_copy(src_ref=v_ref, dst_ref=buf_ref.at[peer_id], send_sem=send_sems.at[rel - 1],
                                         recv_sem=recv_sems.at[rel - 1], device_id=(x, y, c),
                                         device_id_type=MESH).wait_recv()
        for cp in copies:
            cp.wait_send()
        acc = buf_ref[0]
        for i in range(1, 8):
            acc = acc + buf_ref[i]
        o_ref[...] = acc

    return pl.pallas_call(
        body, name="allreduce_small",
        in_specs=[pl.BlockSpec(memory_space=pltpu.VMEM)], out_specs=pl.BlockSpec(memory_space=pltpu.VMEM),
        out_shape=jax.ShapeDtypeStruct(v.shape, F32),
        scratch_shapes=[pltpu.VMEM((8, rows, LANES), F32), pltpu.SemaphoreType.DMA((7,)), pltpu.SemaphoreType.DMA((7,))],
    )(v)


def _add_pair(g, got, c, name):
    _, rh, cols = got.shape
    rb = _tile(rh, 256)
    nrb = rh // rb

    def body(c_ref, g_ref, got_ref, o_ref):
        o_ref[...] = (g_ref[...] + got_ref[...]).astype(BF16)

    spec = pl.BlockSpec((1, rb, cols), lambda j, i, c_ref: (j, i, 0))
    return pl.pallas_call(
        body, name=name,
        grid_spec=pltpu.PrefetchScalarGridSpec(
            num_scalar_prefetch=1, grid=(N_SHARD, nrb),
            in_specs=[pl.BlockSpec((1, rb, cols), lambda j, i, c_ref: (j, c_ref[0] * nrb + i, 0)), spec],
            out_specs=spec),
        out_shape=jax.ShapeDtypeStruct(got.shape, BF16),
        compiler_params=_params(("parallel", "parallel")))(c.reshape(1), g, got)


def _add_chips(g, got, landed, k, c, name):
    _, rh, cols = got.shape
    rb = _tile(rh, 256)
    nrb = rh // rb

    def body(kc_ref, g_ref, got_ref, l_ref, o_ref):
        own = g_ref[0] + got_ref[0]
        o_ref[...] = own + l_ref[0].astype(F32) + l_ref[1].astype(F32) + l_ref[2].astype(F32)

    half_c = lambda i, kc: (kc[1] * nrb + i, 0)
    return pl.pallas_call(
        body, name=name,
        grid_spec=pltpu.PrefetchScalarGridSpec(
            num_scalar_prefetch=1, grid=(nrb,),
            in_specs=[pl.BlockSpec((1, rb, cols), lambda i, kc: (kc[0],) + half_c(i, kc)),
                      pl.BlockSpec((1, rb, cols), lambda i, kc: (kc[0], i, 0)),
                      pl.BlockSpec((N_SHARD - 1, rb, cols), lambda i, kc: (0, i, 0))],
            out_specs=pl.BlockSpec((rb, cols), half_c)),
        out_shape=jax.ShapeDtypeStruct((2 * rh, cols), F32),
        compiler_params=_params(("parallel",)))(jnp.stack([k, c]), g, got, landed)


def _adamw(wv, g, m, v, name):
    rows, cols = wv.shape
    rb = _tile(rows, 256)
    c1 = 1.0 - ADAM_B1 ** ADAM_STEP
    c2 = 1.0 - ADAM_B2 ** ADAM_STEP

    def body(w_ref, g_ref, m_ref, v_ref, go_ref, d_ref, nm_ref, nv_ref):
        gv = g_ref[...]
        go_ref[...] = gv
        nm = ADAM_B1 * m_ref[...] + (1.0 - ADAM_B1) * gv
        nv = ADAM_B2 * v_ref[...] + (1.0 - ADAM_B2) * (gv * gv)
        nm_ref[...] = nm
        nv_ref[...] = nv
        d_ref[...] = -ADAM_LR * ((nm / c1) / (jnp.sqrt(nv / c2) + ADAM_EPS) + ADAM_WD * w_ref[...])

    spec = pl.BlockSpec((rb, cols), lambda i: (i, 0))
    return pl.pallas_call(
        body, name=name, grid=(rows // rb,), in_specs=[spec] * 4, out_specs=[spec] * 4,
        out_shape=[jax.ShapeDtypeStruct((rows, cols), F32)] * 4,
        compiler_params=_params(("parallel",)))(wv, g, m, v)


def _pack(flats):
    cat = jnp.concatenate([f.reshape(-1) for f in flats])
    n = cat.shape[0]
    rows = -(-n // (8 * LANES)) * 8
    return jnp.pad(cat, (0, rows * LANES - n)).reshape(rows, LANES)


def _unpack(packed, shapes):
    flat = packed.reshape(-1)
    out, off = [], 0
    for shp in shapes:
        n = 1
        for dim in shp:
            n *= dim
        out.append(flat[off:off + n].reshape(shp))
        off += n
    return out


def kernel(x, p, norm_g, w_in, conv_w, conv_b, dt_bias, a_log, d_skip, gnorm_g, pool_mix_w, pool_mix_b, pool_scale, w_branch_a, w_branch_b, w_out, ple_norm_g, w_ple_gate, w_ple_up, final_g, loss_target, m_norm_g, m_w_in, m_conv_w, m_conv_b, m_dt_bias, m_a_log, m_d_skip, m_gnorm_g, m_pool_mix_w, m_pool_mix_b, m_pool_scale, m_w_branch_a, m_w_branch_b, m_w_out, m_ple_norm_g, m_w_ple_gate, m_w_ple_up, m_final_g, v_norm_g, v_w_in, v_conv_w, v_conv_b, v_dt_bias, v_a_log, v_d_skip, v_gnorm_g, v_pool_mix_w, v_pool_mix_b, v_pool_scale, v_w_branch_a, v_w_branch_b, v_w_out, v_ple_norm_g, v_w_ple_gate, v_w_ple_up, v_final_g):
    wts = dict(norm_g=norm_g, w_in=w_in, conv_w=conv_w, conv_b=conv_b, dt_bias=dt_bias, a_log=a_log, d_skip=d_skip,
               gnorm_g=gnorm_g, pool_mix_w=pool_mix_w, pool_mix_b=pool_mix_b, pool_scale=pool_scale,
               w_branch_a=w_branch_a, w_branch_b=w_branch_b, w_out=w_out, ple_norm_g=ple_norm_g,
               w_ple_gate=w_ple_gate, w_ple_up=w_ple_up, final_g=final_g)
    mom_m = dict(norm_g=m_norm_g, w_in=m_w_in, conv_w=m_conv_w, conv_b=m_conv_b, dt_bias=m_dt_bias, a_log=m_a_log,
                 d_skip=m_d_skip, gnorm_g=m_gnorm_g, pool_mix_w=m_pool_mix_w, pool_mix_b=m_pool_mix_b,
                 pool_scale=m_pool_scale, w_branch_a=m_w_branch_a, w_branch_b=m_w_branch_b, w_out=m_w_out,
                 ple_norm_g=m_ple_norm_g, w_ple_gate=m_w_ple_gate, w_ple_up=m_w_ple_up, final_g=m_final_g)
    mom_v = dict(norm_g=v_norm_g, w_in=v_w_in, conv_w=v_conv_w, conv_b=v_conv_b, dt_bias=v_dt_bias, a_log=v_a_log,
                 d_skip=v_d_skip, gnorm_g=v_gnorm_g, pool_mix_w=v_pool_mix_w, pool_mix_b=v_pool_mix_b,
                 pool_scale=v_pool_scale, w_branch_a=v_w_branch_a, w_branch_b=v_w_branch_b, w_out=v_w_out,
                 ple_norm_g=v_ple_norm_g, w_ple_gate=v_w_ple_gate, w_ple_up=v_w_ple_up, final_g=v_final_g)
    c = lax.axis_index("c")
    k = 2 * lax.axis_index("x") + lax.axis_index("y")
    flat2 = lambda a: a.reshape(-1, a.shape[-1])

    gathered = _gather_weights([_into_slot(flat2(wts[n]), k, BF16, "slot_" + n) for n in BIG],
                               [_into_slot(flat2(conv_w), k, F32, "slot_conv_w")])
    wg = dict(zip(BIG + ("conv_w",), gathered))

    loss, grad_x, grads = _local_step(x, p[0], loss_target, wg, {n: wts[n] for n in SMALL})

    g_big = [grads[n] for n in BIG]
    got = _swap_halves(g_big)
    pairs = [_add_pair(g, o, c, "add_pair_" + n) for n, g, o in zip(BIG, g_big, got)]
    landed = _scatter_chips(pairs)
    half_sums = [_add_chips(g, o, ld, k, c, "add_chips_" + n) for n, g, o, ld in zip(BIG, g_big, got, landed)]
    g_shards = dict(zip(BIG, _join_halves(half_sums)))

    conv_shape = flat2(conv_w).shape
    small_sum = _allreduce_small(_pack([grads[n] for n in SMALL] + [grads["conv_w"], loss]))
    small_shapes = [wts[n].shape for n in SMALL] + [(N_SHARD,) + conv_shape, (1,)]
    small_g = _unpack(small_sum, small_shapes)
    g_conv = lax.dynamic_index_in_dim(small_g[-2], k, axis=0, keepdims=False)

    outs = {}
    for n in BIG:
        vals = _adamw(flat2(wts[n]), g_shards[n], flat2(mom_m[n]), flat2(mom_v[n]), "adamw_" + n)
        for kind, val in zip(("grad", "delta", "new_m", "new_v"), vals):
            outs[kind, n] = val.reshape(wts[n].shape)
    names = SMALL + ("conv_w",)
    sm = _adamw(_pack([wts[n] for n in names]), _pack(small_g[:len(SMALL)] + [g_conv]),
                _pack([mom_m[n] for n in names]), _pack([mom_v[n] for n in names]), "adamw_small")
    sm_shapes = [wts[n].shape for n in names]
    for kind, val in zip(("grad", "delta", "new_m", "new_v"), sm):
        for n, piece in zip(names, _unpack(val, sm_shapes)):
            outs[kind, n] = piece
    return (small_g[-1][0], grad_x, *[outs[kind, n] for kind in ("grad", "delta", "new_m", "new_v") for n in WEIGHTS])
```

```python
import functools

import jax
import jax.numpy as jnp
from jax import lax
from jax.experimental import pallas as pl
from jax.experimental.pallas import tpu as pltpu

F32 = jnp.float32
BF16 = jnp.bfloat16
HIGHEST = lax.Precision.HIGHEST
MESH = pl.DeviceIdType.MESH

EPS = 1e-6
HEAD_DIM = 64
SSM_GROUPS = 4
D_STATE = 128
CONV_WIDTH = 4
CHUNK = 128
N_POOL = 4
LANES = 128
N_SHARD = 4

ADAM_LR = 0.001
ADAM_B1 = 0.9
ADAM_B2 = 0.999
ADAM_EPS = 1e-08
ADAM_WD = 0.01
ADAM_STEP = 10

BIG = ("w_in", "pool_mix_w", "w_branch_a", "w_branch_b", "w_out", "w_ple_gate", "w_ple_up")
SMALL = ("norm_g", "conv_b", "dt_bias", "a_log", "d_skip", "gnorm_g", "pool_mix_b", "pool_scale",
         "ple_norm_g", "final_g")
WEIGHTS = ("norm_g", "w_in", "conv_w", "conv_b", "dt_bias", "a_log", "d_skip", "gnorm_g", "pool_mix_w",
           "pool_mix_b", "pool_scale", "w_branch_a", "w_branch_b", "w_out", "ple_norm_g", "w_ple_gate",
           "w_ple_up", "final_g")


def _params(sem=None, vmem_mb=56):
    kw = dict(vmem_limit_bytes=vmem_mb << 20)
    if sem is not None:
        kw["dimension_semantics"] = sem
    return pltpu.CompilerParams(**kw)


def _sigmoid(v):
    return 1.0 / (1.0 + jnp.exp(-v))


def _softplus(v):
    return jnp.maximum(v, 0.0) + jnp.log1p(jnp.exp(-jnp.abs(v)))


def _bdot(a, b):
    return jnp.dot(a.astype(BF16), b.astype(BF16), preferred_element_type=F32)


def _bdot_nt(a, b):
    return lax.dot_general(a.astype(BF16), b.astype(BF16), (((1,), (1,)), ((), ())), preferred_element_type=F32)


def _bdot_tn(a, b):
    return lax.dot_general(a.astype(BF16), b.astype(BF16), (((0,), (0,)), ((), ())), preferred_element_type=F32)


def _col_block(col0, width):
    assert col0 % width == 0, (col0, width)
    return col0 // width


def _tile(n, cap):
    if n <= cap:
        return n
    best = None
    for cand in range(8, cap + 1, 8):
        if n % cand == 0:
            best = cand
    assert best is not None, (n, cap)
    return best


def _shift_down(v, j, row):
    return jnp.where(row >= j, pltpu.roll(v, j, 0), 0.0)


def _shift_up(v, j, row):
    n = v.shape[0]
    return jnp.where(row < n - j, pltpu.roll(v, n - j, 0), 0.0)


def _mm(a, w, name, tm=1024, tn=1024):
    t, k = a.shape
    tm = min(tm, t)
    blocked = w.ndim == 3
    if blocked:
        nblk, _, tn = w.shape
        n = nblk * tn
        w_spec = pl.BlockSpec((1, k, tn), lambda i, j: (j, 0, 0))
    else:
        n = w.shape[1]
        tn = min(tn, n)
        w_spec = pl.BlockSpec((k, tn), lambda i, j: (0, j))

    def body(a_ref, w_ref, o_ref):
        wv = w_ref[0] if blocked else w_ref[...]
        o_ref[...] = _bdot(a_ref[...], wv)

    return pl.pallas_call(
        body, name=name, grid=(t // tm, n // tn),
        in_specs=[pl.BlockSpec((tm, k), lambda i, j: (i, 0)), w_spec],
        out_specs=pl.BlockSpec((tm, tn), lambda i, j: (i, j)),
        out_shape=jax.ShapeDtypeStruct((t, n), F32),
        compiler_params=_params(("parallel", "parallel")))(a, w)


def _mm_nt(a, w, name, tm=1024, tk=1024):
    t, k = a.shape
    n = w.shape[0]
    tm, tk = min(tm, t), min(tk, k)

    def body(a_ref, w_ref, o_ref):
        kk = pl.program_id(1)
        part = _bdot_nt(a_ref[...], w_ref[...])

        @pl.when(kk == 0)
        def _():
            o_ref[...] = part

        @pl.when(kk > 0)
        def _():
            o_ref[...] += part

    return pl.pallas_call(
        body, name=name, grid=(t // tm, k // tk),
        in_specs=[pl.BlockSpec((tm, tk), lambda i, j: (i, j)), pl.BlockSpec((n, tk), lambda i, j: (0, j))],
        out_specs=pl.BlockSpec((tm, n), lambda i, j: (i, 0)),
        out_shape=jax.ShapeDtypeStruct((t, n), F32),
        compiler_params=_params(("parallel", "arbitrary")))(a, w)


def _mm_tn(a, b, name, tn=1024, tk=1024, col_blocks=False):
    t, m = a.shape
    n = b.shape[1]
    tn, tk = min(tn, n), min(tk, t)

    def body(a_ref, b_ref, o_ref):
        kk = pl.program_id(1)
        part = _bdot_tn(a_ref[...], b_ref[...])
        part = part[None] if col_blocks else part

        @pl.when(kk == 0)
        def _():
            o_ref[...] = part

        @pl.when(kk > 0)
        def _():
            o_ref[...] += part

    if col_blocks:
        out_spec = pl.BlockSpec((1, m, tn), lambda j, kk: (j, 0, 0))
        out_shape = jax.ShapeDtypeStruct((n // tn, m, tn), F32)
    else:
        out_spec = pl.BlockSpec((m, tn), lambda j, kk: (0, j))
        out_shape = jax.ShapeDtypeStruct((m, n), F32)
    return pl.pallas_call(
        body, name=name, grid=(n // tn, t // tk),
        in_specs=[pl.BlockSpec((tk, m), lambda j, kk: (kk, 0)), pl.BlockSpec((tk, tn), lambda j, kk: (kk, j))],
        out_specs=out_spec, out_shape=out_shape,
        compiler_params=_params(("parallel", "arbitrary")))(a, b)


def _w_in_pieces(d, di, dc, nh, shard_w):
    pgd = d // N_POOL
    o_dt, o_u = di + dc, di + dc + nh
    o_zp, o_ga, o_gb = o_u + d, o_u + 2 * d, o_u + 3 * d
    c_z, c_uz = 2 * d, 2 * d + di + dc
    runs = [(False, 0, o_ga, d), (False, d, o_gb, d), (False, c_z, 0, di + dc), (True, 0, o_dt, nh)]
    for g in range(N_POOL):
        runs.append((False, c_uz + 2 * g * pgd, o_u + g * pgd, pgd))
        runs.append((False, c_uz + (2 * g + 1) * pgd, o_zp + g * pgd, pgd))
    pieces = []
    for is_dt, dst, src, n in runs:
        while n > 0:
            k, off = divmod(src, shard_w)
            m = min(n, shard_w - off)
            pieces.append((is_dt, dst, k, off, m))
            dst, src, n = dst + m, src + m, n - m
    return pieces


def _regroup_w_in(w_sh, d, di, dc, nh, rb=256):
    _, rows, sw = w_sh.shape
    n_main = 4 * d + di + dc
    pieces = _w_in_pieces(d, di, dc, nh, sw)
    rb = min(rb, rows)

    def body(w_ref, main_ref, dt_ref):
        dt_ref[...] = jnp.zeros_like(dt_ref)
        for is_dt, dst, k, off, m in pieces:
            out = dt_ref if is_dt else main_ref
            out[:, dst:dst + m] = w_ref[k, :, off:off + m]

    return pl.pallas_call(
        body, name="regroup_w_in", grid=(rows // rb,),
        in_specs=[pl.BlockSpec((N_SHARD, rb, sw), lambda i: (0, i, 0))],
        out_specs=[pl.BlockSpec((rb, n_main), lambda i: (i, 0)), pl.BlockSpec((rb, LANES), lambda i: (i, 0))],
        out_shape=[jax.ShapeDtypeStruct((rows, n_main), w_sh.dtype), jax.ShapeDtypeStruct((rows, LANES), w_sh.dtype)],
        compiler_params=_params(("parallel",)))(w_sh)


def _ungroup_w_in(d_main, d_dt, d, di, dc, nh, rb=128):
    rows, n_main = d_main.shape
    sw = (n_main + nh) // N_SHARD
    pieces = _w_in_pieces(d, di, dc, nh, sw)
    rb = min(rb, rows)

    def body(main_ref, dt_ref, o_ref):
        for is_dt, dst, k, off, m in pieces:
            src = dt_ref if is_dt else main_ref
            o_ref[k, :, off:off + m] = src[:, dst:dst + m]

    return pl.pallas_call(
        body, name="ungroup_w_in", grid=(rows // rb,),
        in_specs=[pl.BlockSpec((rb, n_main), lambda i: (i, 0)), pl.BlockSpec((rb, LANES), lambda i: (i, 0))],
        out_specs=pl.BlockSpec((N_SHARD, rb, sw), lambda i: (0, i, 0)),
        out_shape=jax.ShapeDtypeStruct((N_SHARD, rows, sw), F32),
        compiler_params=_params(("parallel",)))(d_main, d_dt)


def _inproj(x2, norm_g, w_main, w_dt, tm=1024, tn=1024):
    t, d = x2.shape
    n = w_main.shape[1]
    tm, tn = min(tm, t), min(tn, n)

    def body(x_ref, g_ref, w_ref, wdt_ref, proj_ref, dt_ref, h_ref):
        @pl.when(pl.program_id(1) == 0)
        def _():
            xv = x_ref[...]
            r = lax.rsqrt(jnp.mean(xv * xv, axis=-1, keepdims=True) + EPS)
            h = (xv * r * g_ref[...]).astype(BF16)
            h_ref[...] = h
            dt_ref[...] = jnp.dot(h, wdt_ref[...].astype(BF16), preferred_element_type=F32)

        proj_ref[...] = jnp.dot(h_ref[...], w_ref[...].astype(BF16), preferred_element_type=F32)

    return pl.pallas_call(
        body, name="inproj", grid=(t // tm, n // tn),
        in_specs=[pl.BlockSpec((tm, d), lambda i, j: (i, 0)), pl.BlockSpec((1, d), lambda i, j: (0, 0)),
                  pl.BlockSpec((d, tn), lambda i, j: (0, j)), pl.BlockSpec((d, LANES), lambda i, j: (0, 0))],
        out_specs=[pl.BlockSpec((tm, tn), lambda i, j: (i, j)), pl.BlockSpec((tm, LANES), lambda i, j: (i, 0)),
                   pl.BlockSpec((tm, d), lambda i, j: (i, 0))],
        out_shape=[jax.ShapeDtypeStruct((t, n), F32), jax.ShapeDtypeStruct((t, LANES), F32),
                   jax.ShapeDtypeStruct((t, d), BF16)],
        compiler_params=_params(("parallel", "arbitrary")))(x2, norm_g, w_main, w_dt)


def _conv_w_spec(conv_w, cb, j_axis):
    sw = conv_w.shape[2]
    assert sw % cb == 0, (sw, cb)
    per = sw // cb
    return N_SHARD * per, pl.BlockSpec((1, CONV_WIDTH, cb), lambda *ij: (ij[j_axis] // per, 0, ij[j_axis] % per))


def _conv_taps(v, w_ref, b_ref, row):
    shifted = [v] + [_shift_down(v, j, row) for j in range(1, CONV_WIDTH)]
    acc = b_ref[...] + shifted[0] * w_ref[0, CONV_WIDTH - 1:CONV_WIDTH, :]
    for j in range(1, CONV_WIDTH):
        acc = acc + shifted[j] * w_ref[0, CONV_WIDTH - 1 - j:CONV_WIDTH - j, :]
    return acc, shifted


def _conv_fwd(proj, conv_w, conv_b, nb, s, col0, cb=256):
    n_blk, w_spec = _conv_w_spec(conv_w, cb, 1)
    blk0 = _col_block(col0, cb)

    def body(x_ref, w_ref, b_ref, o_ref):
        v = x_ref[...]
        row = lax.broadcasted_iota(jnp.int32, v.shape, 0)
        acc, _ = _conv_taps(v, w_ref, b_ref, row)
        o_ref[...] = acc * _sigmoid(acc)

    return pl.pallas_call(
        body, name="conv_fwd", grid=(nb, n_blk),
        in_specs=[pl.BlockSpec((s, cb), lambda b, j: (b, blk0 + j)), w_spec, pl.BlockSpec((1, cb), lambda b, j: (0, j))],
        out_specs=pl.BlockSpec((s, cb), lambda b, j: (b, j)),
        out_shape=jax.ShapeDtypeStruct((nb * s, n_blk * cb), F32),
        compiler_params=_params(("parallel", "parallel")))(proj, conv_w, conv_b)


def _ssd_consts(di):
    r = lax.broadcasted_iota(jnp.int32, (CHUNK, CHUNK), 0)
    c = lax.broadcasted_iota(jnp.int32, (CHUNK, CHUNK), 1)
    tril = (r >= c).astype(F32)
    head = lax.broadcasted_iota(jnp.int32, (LANES, di), 0)
    chan = lax.broadcasted_iota(jnp.int32, (LANES, di), 1) // HEAD_DIM
    expand = (head == chan).astype(BF16)
    return tril, expand, expand.T


def _expand(v, e):
    v1 = v.astype(BF16)
    r1 = v - v1.astype(F32)
    v2 = r1.astype(BF16)
    v3 = (r1 - v2.astype(F32)).astype(BF16)
    dot = lambda a: jnp.dot(a, e, preferred_element_type=F32)
    return dot(v1) + dot(v2) + dot(v3)


def _head_sum(t, et, terms=2):
    acc = None
    for _ in range(terms):
        tb = t.astype(BF16)
        part = jnp.dot(tb, et, preferred_element_type=F32)
        acc = part if acc is None else acc + part
        t = t - tb.astype(F32)
    return acc


def _ssd_scalars(dtr_ref, dtb_ref, alog_ref, tri):
    dtpre = dtr_ref[...] + dtb_ref[...]
    dt = _softplus(dtpre)
    a_neg = -jnp.exp(alog_ref[...])
    a_dt = dt * a_neg
    a_cs = jnp.dot(tri, a_dt, precision=HIGHEST, preferred_element_type=F32)
    a_cst = lax.dot_general(a_dt, tri, (((0,), (1,)), ((), ())), precision=HIGHEST, preferred_element_type=F32)
    return dtpre, dt, a_neg, a_cs, a_cst


def _ssd_fwd(proj, xbc, dtraw, dtb, alog, dskx, gn, nb, s, di, z_col0):
    t = nb * s
    nc = s // CHUNK
    hpg = di // HEAD_DIM // SSM_GROUPS
    gw = di // SSM_GROUPS
    gn_w = SSM_GROUPS * D_STATE
    b_blk = _col_block(di, gn_w)
    z_blk = _col_block(z_col0, di)
    L, P, N = CHUNK, HEAD_DIM, D_STATE
    tril, expand, _ = _ssd_consts(di)

    def body(z_ref, x_ref, b_ref, c_ref, dtr_ref, dtb_ref, alog_ref, dskx_ref, gn_ref, tril_ref, e_ref,
             ypre_ref, yan_ref, hp_ref, st_ref, yd_ref, xdt_ref):
        @pl.when(pl.program_id(1) == 0)
        def _():
            st_ref[...] = jnp.zeros_like(st_ref)

        hp_ref[0] = st_ref[...]
        tri = tril_ref[...]
        _, dt, _, a_cs, a_cst = _ssd_scalars(dtr_ref, dtb_ref, alog_ref, tri)
        ev = e_ref[...]
        a_exp = _expand(a_cs, ev)
        xv = x_ref[...]
        xdt = xv * _expand(dt, ev)
        xdt_ref[...] = xdt
        a_last = a_exp[L - 1:L, :]
        xe = xdt * jnp.exp(a_last - a_exp)
        ea = jnp.exp(a_exp)
        e_last = jnp.exp(a_last)
        lower = tri > 0.5
        for g in range(SSM_GROUPS):
            gs = slice(g * gw, (g + 1) * gw)
            bg = b_ref[:, g * N:(g + 1) * N].astype(BF16)
            cg = c_ref[:, g * N:(g + 1) * N].astype(BF16)
            gm = _bdot_nt(cg, bg)
            ht = st_ref[:, gs]
            ch = _bdot(cg, ht)
            for e in range(hpg):
                h = g * hpg + e
                hs = slice(h * P, (h + 1) * P)
                decay = jnp.where(lower, jnp.exp(a_cs[:, h:h + 1] - a_cst[h:h + 1, :]), 0.0)
                yd_ref[:, hs] = _bdot(gm * decay, xdt_ref[:, hs])
            st_ref[:, gs] = ht * e_last[:, gs] + _bdot_tn(bg, xe[:, gs])
            ypre = yd_ref[:, gs] + ea[:, gs] * ch + xv[:, gs] * dskx_ref[:, gs]
            ypre_ref[:, gs] = ypre
            zv = z_ref[:, gs]
            v = ypre * zv * _sigmoid(zv)
            r = lax.rsqrt(jnp.mean(v * v, axis=-1, keepdims=True) + EPS)
            yan_ref[:, gs] = (v * r * gn_ref[:, gs]).astype(BF16)

    row = lambda b, c: b * nc + c
    vec = lambda w: pl.BlockSpec((1, w), lambda b, c: (0, 0))
    return pl.pallas_call(
        body, name="ssd_fwd", grid=(nb, nc),
        in_specs=[pl.BlockSpec((L, di), lambda b, c: (row(b, c), z_blk)),
                  pl.BlockSpec((L, di), lambda b, c: (row(b, c), 0)),
                  pl.BlockSpec((L, gn_w), lambda b, c: (row(b, c), b_blk)),
                  pl.BlockSpec((L, gn_w), lambda b, c: (row(b, c), b_blk + 1)),
                  pl.BlockSpec((L, LANES), lambda b, c: (row(b, c), 0)),
                  vec(LANES), vec(LANES), vec(di), vec(di),
                  pl.BlockSpec((L, L), lambda b, c: (0, 0)),
                  pl.BlockSpec((LANES, di), lambda b, c: (0, 0))],
        out_specs=[pl.BlockSpec((L, di), lambda b, c: (row(b, c), 0)),
                   pl.BlockSpec((L, di), lambda b, c: (row(b, c), 0)),
                   pl.BlockSpec((1, N, di), lambda b, c: (row(b, c), 0, 0))],
        out_shape=[jax.ShapeDtypeStruct((t, di), F32), jax.ShapeDtypeStruct((t, di), BF16),
                   jax.ShapeDtypeStruct((nb * nc, N, di), F32)],
        scratch_shapes=[pltpu.VMEM((N, di), F32), pltpu.VMEM((L, di), F32), pltpu.VMEM((L, di), F32)],
        compiler_params=_params(("parallel", "arbitrary")))(
            proj, xbc, xbc, xbc, dtraw, dtb, alog, dskx, gn, tril, expand)


def _pool_sum(v, g, row, shift):
    s2 = v + shift(v, 1, row)
    s4 = s2 + shift(s2, 2, row)
    s8 = s4 + shift(s4, 4, row)
    s16 = s8 + shift(s8, 8, row)
    return jnp.where(g == 0, s2, jnp.where(g == 1, s4, jnp.where(g == 2, s8, s16)))


def _pool_count(g, row):
    return jnp.minimum(row + 1, jnp.left_shift(2, g)).astype(F32)


def _pool_fwd(proj, mix_w, mix_b, scale, nb, s, col0):
    pgd = mix_w.shape[-1]
    blk0 = _col_block(col0, 2 * pgd)

    def body(uz_ref, w_ref, b_ref, sc_ref, o_ref):
        g = pl.program_id(1)
        u = uz_ref[:, :pgd]
        zp = uz_ref[:, pgd:]
        row = lax.broadcasted_iota(jnp.int32, u.shape, 0)
        pooled = _pool_sum(u, g, row, _shift_down) / _pool_count(g, row) - u
        mixed = _bdot(pooled, w_ref[:, 0].reshape(pgd, pgd)) + b_ref[...]
        o_ref[...] = (mixed * sc_ref[...] * zp * _sigmoid(zp)).astype(BF16)

    return pl.pallas_call(
        body, name="pool_fwd", grid=(nb, N_POOL),
        in_specs=[pl.BlockSpec((s, 2 * pgd), lambda b, g: (b, blk0 + g)),
                  pl.BlockSpec((N_SHARD, 1, pgd // N_SHARD, pgd), lambda b, g: (0, g, 0, 0)),
                  pl.BlockSpec((1, pgd), lambda b, g: (0, g)), pl.BlockSpec((1, pgd), lambda b, g: (0, g))],
        out_specs=pl.BlockSpec((s, pgd), lambda b, g: (b, g)),
        out_shape=jax.ShapeDtypeStruct((nb * s, N_POOL * pgd), BF16),
        compiler_params=_params(("parallel", "parallel")))(proj, mix_w, mix_b, scale)


def _merge_fwd(ya, yb, proj, col0, tm=512):
    t, d = ya.shape
    tm = min(tm, t)
    blk = _col_block(col0, 2 * d)

    def body(ya_ref, yb_ref, g_ref, o_ref):
        o_ref[...] = (_sigmoid(g_ref[:, :d]) * ya_ref[...] + _sigmoid(g_ref[:, d:]) * yb_ref[...]).astype(BF16)

    row = pl.BlockSpec((tm, d), lambda i: (i, 0))
    return pl.pallas_call(
        body, name="merge_fwd", grid=(t // tm,),
        in_specs=[row, row, pl.BlockSpec((tm, 2 * d), lambda i: (i, blk))],
        out_specs=row, out_shape=jax.ShapeDtypeStruct((t, d), BF16),
        compiler_params=_params(("parallel",)))(ya, yb, proj)


def _ple_pre(x2, mo, ple_g, tm=512):
    t, d = x2.shape
    tm = min(tm, t)

    def body(x_ref, mo_ref, g_ref, x1_ref, hn_ref):
        x1 = x_ref[...] + mo_ref[...]
        x1_ref[...] = x1
        r = lax.rsqrt(jnp.mean(x1 * x1, axis=-1, keepdims=True) + EPS)
        hn_ref[...] = (x1 * r * g_ref[...]).astype(BF16)

    row = pl.BlockSpec((tm, d), lambda i: (i, 0))
    return pl.pallas_call(
        body, name="ple_pre", grid=(t // tm,),
        in_specs=[row, row, pl.BlockSpec((1, d), lambda i: (0, 0))],
        out_specs=[row, row],
        out_shape=[jax.ShapeDtypeStruct((t, d), F32), jax.ShapeDtypeStruct((t, d), BF16)],
        compiler_params=_params(("parallel",)))(x2, mo, ple_g)


def _tail(x1, pre, pu, tgt, final_g, tm=512):
    t, d = x1.shape
    tm = min(tm, t)

    def body(x1_ref, pre_ref, pu_ref, tgt_ref, g_ref, dx2_ref, dpre_ref, dpu_ref, loss_ref, dg_ref):
        @pl.when(pl.program_id(0) == 0)
        def _():
            loss_ref[...] = jnp.zeros_like(loss_ref)
            dg_ref[...] = jnp.zeros_like(dg_ref)

        gate = _sigmoid(pre_ref[...])
        pu = pu_ref[...]
        x2 = x1_ref[...] + gate * pu
        r = lax.rsqrt(jnp.mean(x2 * x2, axis=-1, keepdims=True) + EPS)
        xn = x2 * r
        fg = g_ref[...]
        err = xn * fg - tgt_ref[...]
        loss_ref[...] += 0.5 * jnp.sum(jnp.mean(err * err, axis=-1, keepdims=True))
        dy = err * (1.0 / d)
        dg_ref[...] += jnp.sum(dy * xn, axis=0, keepdims=True)
        dxn = dy * fg
        dx2 = r * (dxn - xn * jnp.mean(dxn * xn, axis=-1, keepdims=True))
        dx2_ref[...] = dx2
        dpre_ref[...] = (dx2 * pu * gate * (1.0 - gate)).astype(BF16)
        dpu_ref[...] = (dx2 * gate).astype(BF16)

    row = pl.BlockSpec((tm, d), lambda i: (i, 0))
    return pl.pallas_call(
        body, name="tail", grid=(t // tm,),
        in_specs=[row, row, row, row, pl.BlockSpec((1, d), lambda i: (0, 0))],
        out_specs=[row, row, row, pl.BlockSpec((1, LANES), lambda i: (0, 0)), pl.BlockSpec((1, d), lambda i: (0, 0))],
        out_shape=[jax.ShapeDtypeStruct((t, d), F32)] + [jax.ShapeDtypeStruct((t, d), BF16)] * 2 + [
            jax.ShapeDtypeStruct((1, LANES), F32),
                                                             jax.ShapeDtypeStruct((1, d), F32)],
        compiler_params=_params(("arbitrary",)))(x1, pre, pu, tgt, final_g)


def _rms_bwd(xin, dhs, dres, g, name, tm=512):
    t, d = xin.shape
    tm = min(tm, t)
    n_dh = len(dhs)

    def body(*refs):
        x_ref, dh_refs, dres_ref, g_ref, dx_ref, dg_ref = refs[0], refs[1:1 + n_dh], *refs[1 + n_dh:]

        @pl.when(pl.program_id(0) == 0)
        def _():
            dg_ref[...] = jnp.zeros_like(dg_ref)

        xv = x_ref[...]
        dh = dh_refs[0][...]
        for ref in dh_refs[1:]:
            dh = dh + ref[...]
        r = lax.rsqrt(jnp.mean(xv * xv, axis=-1, keepdims=True) + EPS)
        xn = xv * r
        dg_ref[...] += jnp.sum(dh * xn, axis=0, keepdims=True)
        dd = dh * g_ref[...]
        dx_ref[...] = dres_ref[...] + r * (dd - xn * jnp.mean(dd * xn, axis=-1, keepdims=True))

    row = pl.BlockSpec((tm, d), lambda i: (i, 0))
    vec = pl.BlockSpec((1, d), lambda i: (0, 0))
    return pl.pallas_call(
        body, name=name, grid=(t // tm,),
        in_specs=[row] * (2 + n_dh) + [vec],
        out_specs=[row, vec],
        out_shape=[jax.ShapeDtypeStruct((t, d), F32), jax.ShapeDtypeStruct((1, d), F32)],
        compiler_params=_params(("arbitrary",)))(xin, *dhs, dres, g)


def _merge_bwd(dm, ya, yb, proj, col0, n_cols, tm=512):
    t, d = ya.shape
    tm = min(tm, t)
    blk = _col_block(col0, 2 * d)

    def body(dm_ref, ya_ref, yb_ref, g_ref, dya_ref, dyb_ref, dg_ref):
        dm_v = dm_ref[...]
        sa = _sigmoid(g_ref[:, :d])
        sb = _sigmoid(g_ref[:, d:])
        dya_ref[...] = (dm_v * sa).astype(BF16)
        dyb_ref[...] = (dm_v * sb).astype(BF16)
        dg_ref[:, :d] = (dm_v * ya_ref[...] * sa * (1.0 - sa)).astype(BF16)
        dg_ref[:, d:] = (dm_v * yb_ref[...] * sb * (1.0 - sb)).astype(BF16)

    row = pl.BlockSpec((tm, d), lambda i: (i, 0))
    gspec = pl.BlockSpec((tm, 2 * d), lambda i: (i, blk))
    return pl.pallas_call(
        body, name="merge_bwd", grid=(t // tm,),
        in_specs=[row, row, row, gspec],
        out_specs=[row, row, gspec],
        out_shape=[jax.ShapeDtypeStruct((t, d), BF16), jax.ShapeDtypeStruct((t, d), BF16),
                   jax.ShapeDtypeStruct((t, n_cols), BF16)],
        compiler_params=_params(("parallel",)))(dm, ya, yb, proj)


def _pool_bwd(proj, dyb, dproj, mix_w, mix_b, scale, nb, s, col0):
    pgd = mix_w.shape[-1]
    blk0 = _col_block(col0, 2 * pgd)

    def body(uz_ref, dy_ref, _, w_ref, b_ref, sc_ref, duz_ref, dw_ref, db_ref, dsc_ref):
        g = pl.program_id(0)

        @pl.when(pl.program_id(1) == 0)
        def _():
            dw_ref[...] = jnp.zeros_like(dw_ref)
            db_ref[...] = jnp.zeros_like(db_ref)
            dsc_ref[...] = jnp.zeros_like(dsc_ref)

        u = uz_ref[:, :pgd]
        zp = uz_ref[:, pgd:]
        row = lax.broadcasted_iota(jnp.int32, u.shape, 0)
        cnt = _pool_count(g, row)
        pooled = _pool_sum(u, g, row, _shift_down) / cnt - u
        wv = w_ref[:, 0].reshape(pgd, pgd)
        mixed = _bdot(pooled, wv) + b_ref[...]
        sg = _sigmoid(zp)
        sz = zp * sg
        dy = dy_ref[...]
        sc = sc_ref[...]
        dsc_ref[...] += jnp.sum(dy * mixed * sz, axis=0, keepdims=True)
        dmixed = dy * sc * sz
        db_ref[...] += jnp.sum(dmixed, axis=0, keepdims=True)
        dw_ref[:, 0] += _bdot_tn(pooled, dmixed).reshape(N_SHARD, pgd // N_SHARD, pgd)
        dpooled = _bdot_nt(dmixed, wv)
        duz_ref[:, :pgd] = (_pool_sum(dpooled / cnt, g, row, _shift_up) - dpooled).astype(BF16)
        duz_ref[:, pgd:] = (dy * mixed * sc * sg * (1.0 + zp * (1.0 - sg))).astype(BF16)

    uz = pl.BlockSpec((s, 2 * pgd), lambda g, b: (b, blk0 + g))
    vec = pl.BlockSpec((1, pgd), lambda g, b: (0, g))
    wspec = pl.BlockSpec((N_SHARD, 1, pgd // N_SHARD, pgd), lambda g, b: (0, g, 0, 0))
    return pl.pallas_call(
        body, name="pool_bwd", grid=(N_POOL, nb),
        in_specs=[uz, pl.BlockSpec((s, pgd), lambda g, b: (b, g)), pl.BlockSpec(memory_space=pl.ANY), wspec, vec, vec],
        out_specs=[uz, wspec, vec, vec],
        out_shape=[jax.ShapeDtypeStruct(dproj.shape, dproj.dtype), jax.ShapeDtypeStruct(mix_w.shape, F32),
                   jax.ShapeDtypeStruct(mix_b.shape, F32), jax.ShapeDtypeStruct(scale.shape, F32)],
        input_output_aliases={2: 0},
        compiler_params=_params(("parallel", "arbitrary")))(proj, dyb, dproj, mix_w, mix_b, scale)


def _ssd_bwd(dyan, ypre, proj, xbc, dtraw, hp, dproj, dtb, alog, dskx, gn, nb, s, di, z_col0):
    t = nb * s
    nc = s // CHUNK
    hpg = di // HEAD_DIM // SSM_GROUPS
    gw = di // SSM_GROUPS
    gn_w = SSM_GROUPS * D_STATE
    dc = di + 2 * gn_w
    b_blk = _col_block(di, gn_w)
    z_blk = _col_block(z_col0, di)
    L, P, N = CHUNK, HEAD_DIM, D_STATE
    tril, expand, expand_t = _ssd_consts(di)

    def body(dy_ref, ypre_ref, z_ref, x_ref, b_ref, c_ref, dtr_ref, hp_ref, _, dtb_ref, alog_ref, dskx_ref, gn_ref,
             tril_ref, e_ref, et_ref, dz_ref, ddt_ref, dxbc_ref, dgn_ref, ddsk_ref, dalog_ref, ddtb_ref,
             dst_ref, dyp_ref, xdt_ref, dxm_ref, t1_ref, t2_ref, t3_ref, aux_ref):
        @pl.when((pl.program_id(0) == 0) & (pl.program_id(1) == 0))
        def _():
            dgn_ref[...] = jnp.zeros_like(dgn_ref)
            ddsk_ref[...] = jnp.zeros_like(ddsk_ref)
            dalog_ref[...] = jnp.zeros_like(dalog_ref)
            ddtb_ref[...] = jnp.zeros_like(ddtb_ref)

        @pl.when(pl.program_id(1) == 0)
        def _():
            dst_ref[...] = jnp.zeros_like(dst_ref)

        tri = tril_ref[...]
        dtpre, dt, a_neg, a_cs, a_cst = _ssd_scalars(dtr_ref, dtb_ref, alog_ref, tri)
        ev = e_ref[...]
        a_exp = _expand(a_cs, ev)
        dt_exp = _expand(dt, ev)
        xv = x_ref[...]
        xdt = xv * dt_exp
        xdt_ref[...] = xdt
        a_last = a_exp[L - 1:L, :]
        dte = jnp.exp(a_last - a_exp)
        xe = xdt * dte
        ea = jnp.exp(a_exp)
        e_last = jnp.exp(a_last)
        lower = tri > 0.5
        aux_ref[...] = jnp.zeros_like(aux_ref)
        for g in range(SSM_GROUPS):
            gs = slice(g * gw, (g + 1) * gw)
            zv = z_ref[:, gs]
            yp = ypre_ref[:, gs]
            sg = _sigmoid(zv)
            sz = zv * sg
            vg = yp * sz
            r = lax.rsqrt(jnp.mean(vg * vg, axis=-1, keepdims=True) + EPS)
            vn = vg * r
            dyg = dy_ref[:, gs]
            dgn_ref[:, gs] += jnp.sum(dyg * vn, axis=0, keepdims=True)
            dvn = dyg * gn_ref[:, gs]
            dv = r * (dvn - vn * jnp.mean(dvn * vn, axis=-1, keepdims=True))
            dy = dv * sz
            dyp_ref[:, gs] = dy
            dz_ref[:, gs] = (dv * yp * sg * (1.0 + zv * (1.0 - sg))).astype(BF16)
            bg = b_ref[:, g * N:(g + 1) * N].astype(BF16)
            cg = c_ref[:, g * N:(g + 1) * N].astype(BF16)
            gm = _bdot_nt(cg, bg)
            ht = hp_ref[0, :, gs]
            dht = dst_ref[:, gs]
            bds = _bdot(bg, dht)
            dye = dy * ea[:, gs]
            xe_g = xe[:, gs]
            dcg = _bdot_nt(dye, ht)
            dbg = _bdot_nt(xe_g, dht)
            dst_ref[:, gs] = e_last[:, gs] * dht + _bdot_tn(cg, dye)
            dgm = jnp.zeros((L, L), F32)
            for e in range(hpg):
                h = g * hpg + e
                hs = slice(h * P, (h + 1) * P)
                decay = jnp.where(lower, jnp.exp(a_cs[:, h:h + 1] - a_cst[h:h + 1, :]), 0.0)
                dy_h = dyp_ref[:, hs]
                dgm = dgm + _bdot_nt(dy_h, xdt_ref[:, hs]) * decay
                dxm_ref[:, hs] = _bdot_tn(gm * decay, dy_h)
            dxbc_ref[:, di + g * N:di + (g + 1) * N] = dbg + _bdot_tn(dgm, cg)
            dxbc_ref[:, di + gn_w + g * N:di + gn_w + (g + 1) * N] = dcg + _bdot(dgm, bg)
            dxm = dxm_ref[:, gs]
            x_g = xv[:, gs]
            dskx = dskx_ref[:, gs]
            xeb = xe_g * bds
            dxdt = dxm + dte[:, gs] * bds
            dxbc_ref[:, gs] = dxdt * dt_exp[:, gs] + dy * dskx
            each = ea[:, gs] * _bdot(cg, ht)
            y_diag = yp - x_g * dskx - each
            rnd = lambda v: v.astype(BF16).astype(F32)
            t1_ref[:, gs] = rnd(dy) * y_diag + dy * each - rnd(xdt[:, gs]) * dxm - xeb
            t2_ref[:, gs] = xeb
            t3_ref[:, gs] = dxdt * x_g
            aux_ref[0:1, gs] = jnp.sum(dht * ht, axis=0, keepdims=True)
            aux_ref[1:2, gs] = jnp.sum(dy * x_g, axis=0, keepdims=True)
        etv = et_ref[...]
        w_end = _head_sum(t2_ref[...], etv)
        aux = _head_sum(aux_ref[...], etv)
        rowi = lax.broadcasted_iota(jnp.int32, (L, LANES), 0)
        end = jnp.sum(w_end, axis=0, keepdims=True) + aux[0:1, :] * jnp.exp(a_cs[L - 1:L, :])
        da = _head_sum(t1_ref[...], etv, terms=3) + jnp.where(rowi == L - 1, end, 0.0)
        rc = lax.dot_general(tri, da, (((0,), (0,)), ((), ())), precision=HIGHEST, preferred_element_type=F32)
        ddt = a_neg * rc + _head_sum(t3_ref[...], etv)
        ddtraw = ddt * _sigmoid(dtpre)
        ddt_ref[...] = ddtraw.astype(BF16)
        ddtb_ref[...] += jnp.sum(ddtraw, axis=0, keepdims=True)
        dalog_ref[...] += jnp.sum(dt * rc, axis=0, keepdims=True) * a_neg
        ddsk_ref[...] += aux[1:2, :]

    row = lambda b, c: b * nc + (nc - 1 - c)
    full = lambda w: pl.BlockSpec((L, w), lambda b, c: (row(b, c), 0))
    zspec = pl.BlockSpec((L, di), lambda b, c: (row(b, c), z_blk))
    vec = lambda w: pl.BlockSpec((1, w), lambda b, c: (0, 0))
    slab = lambda shape: pltpu.VMEM(shape, F32)
    return pl.pallas_call(
        body, name="ssd_bwd", grid=(nb, nc),
        in_specs=[full(di), full(di), zspec, full(di),
                  pl.BlockSpec((L, gn_w), lambda b, c: (row(b, c), b_blk)),
                  pl.BlockSpec((L, gn_w), lambda b, c: (row(b, c), b_blk + 1)),
                  full(LANES),
                  pl.BlockSpec((1, N, di), lambda b, c: (row(b, c), 0, 0)),
                  pl.BlockSpec(memory_space=pl.ANY),
                  vec(LANES), vec(LANES), vec(di), vec(di),
                  pl.BlockSpec((L, L), lambda b, c: (0, 0)),
                  pl.BlockSpec((LANES, di), lambda b, c: (0, 0)),
                  pl.BlockSpec((di, LANES), lambda b, c: (0, 0))],
        out_specs=[zspec, full(LANES), full(dc), vec(di), vec(LANES), vec(LANES), vec(LANES)],
        out_shape=[jax.ShapeDtypeStruct(dproj.shape, dproj.dtype), jax.ShapeDtypeStruct((t, LANES), BF16),
                   jax.ShapeDtypeStruct((t, dc), F32), jax.ShapeDtypeStruct((1, di), F32),
                   jax.ShapeDtypeStruct((1, LANES), F32), jax.ShapeDtypeStruct((1, LANES), F32),
                   jax.ShapeDtypeStruct((1, LANES), F32)],
        scratch_shapes=[slab((N, di)), slab((L, di)), slab((L, di)), slab((L, di)), slab((L, di)), slab((L, di)),
                        slab((L, di)), slab((8, di))],
        input_output_aliases={8: 0},
        compiler_params=_params(("arbitrary", "arbitrary")))(
            dyan, ypre, proj, xbc, xbc, xbc, dtraw, hp, dproj, dtb, alog, dskx, gn, tril, expand, expand_t)


def _conv_bwd(proj, dxbc, dproj, conv_w, conv_b, nb, s, col0, cb=256):
    n_blk, w_spec = _conv_w_spec(conv_w, cb, 0)
    blk0 = _col_block(col0, cb)

    def body(x_ref, dy_ref, _, w_ref, b_ref, dx_ref, dw_ref, db_ref):
        @pl.when(pl.program_id(1) == 0)
        def _():
            dw_ref[...] = jnp.zeros_like(dw_ref)
            db_ref[...] = jnp.zeros_like(db_ref)

        v = x_ref[...]
        row = lax.broadcasted_iota(jnp.int32, v.shape, 0)
        acc, shifted = _conv_taps(v, w_ref, b_ref, row)
        sg = _sigmoid(acc)
        dacc = dy_ref[...] * sg * (1.0 + acc * (1.0 - sg))
        db_ref[...] += jnp.sum(dacc, axis=0, keepdims=True)
        dx = dacc * w_ref[0, CONV_WIDTH - 1:CONV_WIDTH, :]
        for j in range(CONV_WIDTH):
            dw_ref[0, CONV_WIDTH - 1 - j:CONV_WIDTH - j, :] += jnp.sum(dacc * shifted[j], axis=0, keepdims=True)
            if j:
                dx = dx + _shift_up(dacc, j, row) * w_ref[0, CONV_WIDTH - 1 - j:CONV_WIDTH - j, :]
        dx_ref[...] = dx.astype(BF16)

    return pl.pallas_call(
        body, name="conv_bwd", grid=(n_blk, nb),
        in_specs=[pl.BlockSpec((s, cb), lambda j, b: (b, blk0 + j)), pl.BlockSpec((s, cb), lambda j, b: (b, j)),
                  pl.BlockSpec(memory_space=pl.ANY), w_spec, pl.BlockSpec((1, cb), lambda j, b: (0, j))],
        out_specs=[pl.BlockSpec((s, cb), lambda j, b: (b, blk0 + j)), w_spec, pl.BlockSpec((1, cb), lambda j, b: (0, j))],
        out_shape=[jax.ShapeDtypeStruct(dproj.shape, dproj.dtype), jax.ShapeDtypeStruct(conv_w.shape, F32),
                   jax.ShapeDtypeStruct(conv_b.shape, F32)],
        input_output_aliases={2: 0},
        compiler_params=_params(("parallel", "arbitrary")))(proj, dxbc, dproj, conv_w, conv_b)


def _local_step(x, p, tgt, wg, small):
    nb, s, d = x.shape
    t = nb * s
    di = N_SHARD * wg["w_branch_a"].shape[1]
    nh = di // HEAD_DIM
    gn_w = SSM_GROUPS * D_STATE
    dc = di + 2 * gn_w
    pgd = d // N_POOL
    x2 = x.reshape(t, d)
    p2 = p.reshape(t, p.shape[-1])
    tgt2 = tgt.reshape(t, d)

    w_main, w_dt = _regroup_w_in(wg["w_in"], d, di, dc, nh)
    c_g, c_z, c_xbc, c_uz = 0, 2 * d, 2 * d + di, 2 * d + di + dc
    n_main = w_main.shape[1]

    pad_h = lambda v: jnp.pad(v.reshape(1, nh).astype(F32), ((0, 0), (0, LANES - nh)))
    dtb, alog = pad_h(small["dt_bias"]), pad_h(small["a_log"])
    dskx = jnp.repeat(small["d_skip"].reshape(1, nh).astype(F32), HEAD_DIM, axis=1)
    vec = lambda v: v.reshape(1, -1).astype(F32)
    norm_g, gn, conv_b = vec(small["norm_g"]), vec(small["gnorm_g"]), vec(small["conv_b"])
    mix_b, scale = vec(small["pool_mix_b"]), vec(small["pool_scale"])
    ple_g, final_g = vec(small["ple_norm_g"]), vec(small["final_g"])
    conv_w = wg["conv_w"]
    mix_w = wg["pool_mix_w"].reshape(N_SHARD, N_POOL, pgd // N_SHARD, pgd)
    rows = lambda v: v.reshape(-1, v.shape[-1])
    wa, wb, wo, wpg = rows(wg["w_branch_a"]), rows(wg["w_branch_b"]), rows(wg["w_out"]), rows(wg["w_ple_gate"])
    wup = wg["w_ple_up"]

    proj, dtraw, h = _inproj(x2, norm_g, w_main, w_dt)
    xbc = _conv_fwd(proj, conv_w, conv_b, nb, s, c_xbc)
    ypre, yan, hp = _ssd_fwd(proj, xbc, dtraw, dtb, alog, dskx, gn, nb, s, di, c_z)
    ybp = _pool_fwd(proj, mix_w, mix_b, scale, nb, s, c_uz)
    ya = _mm(yan, wa, "branch_a")
    yb = _mm(ybp, wb, "branch_b")
    merged = _merge_fwd(ya, yb, proj, c_g)
    mo = _mm(merged, wo, "out_proj")
    x1, hn = _ple_pre(x2, mo, ple_g)
    pre = _mm(hn, wpg, "ple_gate")
    pu = _mm(p2, wup, "ple_up")

    dx2, dpre, dpu, loss, d_final_g = _tail(x1, pre, pu, tgt2, final_g)
    d_wpg = _mm_tn(hn, dpre, "d_w_ple_gate")
    d_wup = _mm_tn(p2, dpu, "d_w_ple_up", tn=wup.shape[-1], col_blocks=True)
    dhn = _mm_nt(dpre, wpg, "d_hn")
    dx1, d_ple_g = _rms_bwd(x1, [dhn], dx2, ple_g, "ple_bwd")
    d_wo = _mm_tn(merged, dx1, "d_w_out")
    dm = _mm_nt(dx1, wo, "d_merged")
    dya, dyb, dproj = _merge_bwd(dm, ya, yb, proj, c_g, n_main)
    d_wa = _mm_tn(yan, dya, "d_w_branch_a")
    d_wb = _mm_tn(ybp, dyb, "d_w_branch_b")
    dyan = _mm_nt(dya, wa, "d_y_a")
    dybp = _mm_nt(dyb, wb, "d_y_b")
    dproj, d_mix_w, d_mix_b, d_scale = _pool_bwd(proj, dybp, dproj, mix_w, mix_b, scale, nb, s, c_uz)
    dproj, ddt, dxbc, d_gn, d_dsk, d_alog, d_dtb = _ssd_bwd(
        dyan, ypre, proj, xbc, dtraw, hp, dproj, dtb, alog, dskx, gn, nb, s, di, c_z)
    dproj, d_conv_w, d_conv_b = _conv_bwd(proj, dxbc, dproj, conv_w, conv_b, nb, s, c_xbc)
    d_wmain = _mm_tn(h, dproj, "d_w_in")
    d_wdt = _mm_tn(h, ddt, "d_w_dt")
    dh_main = _mm_nt(dproj, w_main, "d_h")
    dh_dt = _mm_nt(ddt, w_dt, "d_h_dt")
    gx, d_norm_g = _rms_bwd(x2, [dh_main, dh_dt], dx1, norm_g, "in_bwd")
    d_w_in = _ungroup_w_in(d_wmain, d_wdt, d, di, dc, nh)

    shard_major = lambda v: v.reshape(N_SHARD, v.shape[0] // N_SHARD, v.shape[1])
    grads = dict(norm_g=d_norm_g, w_in=d_w_in, conv_w=d_conv_w, conv_b=d_conv_b, dt_bias=d_dtb[:, :nh],
                 a_log=d_alog[:, :nh], d_skip=d_dsk[:, :nh], gnorm_g=d_gn,
                 pool_mix_w=d_mix_w.reshape(N_SHARD, pgd, pgd), pool_mix_b=d_mix_b, pool_scale=d_scale,
                 w_branch_a=shard_major(d_wa), w_branch_b=shard_major(d_wb), w_out=shard_major(d_wo),
                 ple_norm_g=d_ple_g, w_ple_gate=shard_major(d_wpg), w_ple_up=d_wup, final_g=d_final_g)
    return loss[0, 0], gx.reshape(nb, s, d), grads


def _place():
    return lax.axis_index("x"), lax.axis_index("y"), lax.axis_index("c")


def _other_chips(x, y):
    return [(1 - x, y), (x, 1 - y), (1 - x, 1 - y)]


def _halves(c, rows, align):
    rh = rows // 2
    assert rows % 2 == 0 and rh % align == 0, (rows, align)
    return (pl.ds(pl.multiple_of(c * rh, align), rh), pl.ds(pl.multiple_of((1 - c) * rh, align), rh))


HBM = pl.BlockSpec(memory_space=pl.ANY)


def _into_slot(w2, k, dtype, name):
    rows, cols = w2.shape
    rb = _tile(rows, 256)

    def body(k_ref, w_ref, o_ref):
        o_ref[0] = w_ref[...].astype(dtype)

    return pl.pallas_call(
        body, name=name,
        grid_spec=pltpu.PrefetchScalarGridSpec(
            num_scalar_prefetch=1, grid=(rows // rb,),
            in_specs=[pl.BlockSpec((rb, cols), lambda i, k_ref: (i, 0))],
            out_specs=pl.BlockSpec((1, rb, cols), lambda i, k_ref: (k_ref[0], i, 0))),
        out_shape=jax.ShapeDtypeStruct((N_SHARD, rows, cols), dtype),
        compiler_params=_params(("parallel",)))(k.reshape(1), w2)


def _gather_weights(split, whole):
    n_split, n_all = len(split), len(split) + len(whole)

    def body(*refs):
        bufs = refs[n_all:2 * n_all]
        send_sems, recv_sems = refs[2 * n_all:]
        x, y, c = _place()
        k = 2 * x + y
        chips = _other_chips(x, y)

        def copy(idx, block, to):
            return pltpu.make_async_remote_copy(src_ref=block, dst_ref=block, send_sem=send_sems.at[idx],
                                                recv_sem=recv_sems.at[idx], device_id=to, device_id_type=MESH)

        def block(i, shard, rows):
            return bufs[i].at[shard, rows] if i < n_split else bufs[i].at[shard]

        def sem(i, j):
            return 6 * i + j if i < n_split else 6 * n_split + 3 * (i - n_split) + j

        started = []
        for i in range(n_all):
            mine, _ = _halves(c, bufs[i].shape[1], 16) if i < n_split else (None, None)
            for j, (px, py) in enumerate(chips):
                started.append(copy(sem(i, j), block(i, k, mine), (px, py, c)))
                started[-1].start()
        for i in range(n_all):
            mine, _ = _halves(c, bufs[i].shape[1], 16) if i < n_split else (None, None)
            for j, (px, py) in enumerate(chips):
                landed = block(i, 2 * px + py, mine)
                copy(sem(i, j), landed, (px, py, c)).wait_recv()
                if i < n_split:
                    started.append(copy(sem(i, 3 + j), landed, (x, y, 1 - c)))
                    started[-1].start()
        for i in range(n_split):
            _, theirs = _halves(c, bufs[i].shape[1], 16)
            for j, (px, py) in enumerate(chips):
                copy(sem(i, 3 + j), block(i, 2 * px + py, theirs), (x, y, 1 - c)).wait_recv()
        for cp in started:
            cp.wait_send()

    arrays = list(split) + list(whole)
    n_sem = 6 * n_split + 3 * len(whole)
    return pl.pallas_call(
        body, name="gather_weights",
        in_specs=[HBM] * n_all, out_specs=[HBM] * n_all,
        out_shape=[jax.ShapeDtypeStruct(a.shape, a.dtype) for a in arrays],
        input_output_aliases={i: i for i in range(n_all)},
        scratch_shapes=[pltpu.SemaphoreType.DMA((n_sem,)), pltpu.SemaphoreType.DMA((n_sem,))],
    )(*arrays)


def _swap_halves(gs):
    n = len(gs)

    def body(*refs):
        ins, outs, send_sems, recv_sems = refs[:n], refs[n:2 * n], refs[2 * n], refs[2 * n + 1]
        x, y, c = _place()
        copies = []
        for i in range(n):
            _, theirs = _halves(c, gs[i].shape[1], 8)
            cp = pltpu.make_async_remote_copy(src_ref=ins[i].at[:, theirs], dst_ref=outs[i], send_sem=send_sems.at[i],
                                              recv_sem=recv_sems.at[i], device_id=(x, y, 1 - c), device_id_type=MESH)
            cp.start()
            copies.append(cp)
        for cp in copies:
            cp.wait()

    return pl.pallas_call(
        body, name="swap_halves", in_specs=[HBM] * n, out_specs=[HBM] * n,
        out_shape=[jax.ShapeDtypeStruct((g.shape[0], g.shape[1] // 2, g.shape[2]), g.dtype) for g in gs],
        scratch_shapes=[pltpu.SemaphoreType.DMA((n,)), pltpu.SemaphoreType.DMA((n,))],
    )(*gs)


def _scatter_chips(ps):
    n = len(ps)

    def body(*refs):
        ins, outs, send_sems, recv_sems = refs[:n], refs[n:2 * n], refs[2 * n], refs[2 * n + 1]
        x, y, c = _place()
        copies = []
        for i in range(n):
            for j, (px, py) in enumerate(_other_chips(x, y)):
                cp = pltpu.make_async_remote_copy(src_ref=ins[i].at[2 * px + py], dst_ref=outs[i].at[j],
                                                  send_sem=send_sems.at[3 * i + j], recv_sem=recv_sems.at[3 * i + j],
                                                  device_id=(px, py, c), device_id_type=MESH)
                cp.start()
                copies.append(cp)
        for cp in copies:
            cp.wait()

    return pl.pallas_call(
        body, name="scatter_chips", in_specs=[HBM] * n, out_specs=[HBM] * n,
        out_shape=[jax.ShapeDtypeStruct((N_SHARD - 1,) + v.shape[1:], v.dtype) for v in ps],
        scratch_shapes=[pltpu.SemaphoreType.DMA((3 * n,)), pltpu.SemaphoreType.DMA((3 * n,))],
    )(*ps)


def _join_halves(vs):
    n = len(vs)

    def body(*refs):
        bufs, send_sems, recv_sems = refs[n:2 * n], refs[2 * n], refs[2 * n + 1]
        x, y, c = _place()

        def copy(i, rows):
            return pltpu.make_async_remote_copy(src_ref=bufs[i].at[rows], dst_ref=bufs[i].at[rows],
                                                send_sem=send_sems.at[i], recv_sem=recv_sems.at[i],
                                                device_id=(x, y, 1 - c), device_id_type=MESH)

        halves = [_halves(c, bufs[i].shape[0], 8) for i in range(n)]
        sends = [copy(i, halves[i][0]) for i in range(n)]
        for cp in sends:
            cp.start()
        for i in range(n):
            copy(i, halves[i][1]).wait_recv()
        for cp in sends:
            cp.wait_send()

    return pl.pallas_call(
        body, name="join_halves", in_specs=[HBM] * n, out_specs=[HBM] * n,
        out_shape=[jax.ShapeDtypeStruct(v.shape, v.dtype) for v in vs],
        input_output_aliases={i: i for i in range(n)},
        scratch_shapes=[pltpu.SemaphoreType.DMA((n,)), pltpu.SemaphoreType.DMA((n,))],
    )(*vs)


def _allreduce_small(v):
    rows = v.shape[0]

    def body(v_ref, o_ref, buf_ref, send_sems, recv_sems):
        x, y, c = _place()
        me = 4 * x + 2 * y + c
        buf_ref[me] = v_ref[...]
        copies = []
        for rel in range(1, 8):
            peer = (x ^ (rel >> 2), y ^ ((rel >> 1) & 1), c ^ (rel & 1))
            cp = pltpu.make_async_remote_copy(src_ref=v_ref, dst_ref=buf_ref.at[me], send_sem=send_sems.at[rel - 1],
                                              recv_sem=recv_sems.at[rel - 1], device_id=peer, device_id_type=MESH)
            cp.start()
            copies.append(cp)
        for rel in range(1, 8):
            peer_id = me ^ rel
            pltpu.make_async_remote_copy(src_ref=v_ref, dst_ref=buf_ref.at[peer_id], send_sem=send_sems.at[rel - 1],
                                         recv_sem=recv_sems.at[rel - 1], device_id=(x, y, c),
                                         device_id_type=MESH).wait_recv()
        for cp in copies:
            cp.wait_send()
        acc = buf_ref[0]
        for i in range(1, 8):
            acc = acc + buf_ref[i]
        o_ref[...] = acc

    return pl.pallas_call(
        body, name="allreduce_small",
        in_specs=[pl.BlockSpec(memory_space=pltpu.VMEM)], out_specs=pl.BlockSpec(memory_space=pltpu.VMEM),
        out_shape=jax.ShapeDtypeStruct(v.shape, F32),
        scratch_shapes=[pltpu.VMEM((8, rows, LANES), F32), pltpu.SemaphoreType.DMA((7,)), pltpu.SemaphoreType.DMA((7,))],
    )(v)


def _add_pair(g, got, c, name):
    _, rh, cols = got.shape
    rb = _tile(rh, 256)
    nrb = rh // rb

    def body(c_ref, g_ref, got_ref, o_ref):
        o_ref[...] = (g_ref[...] + got_ref[...]).astype(BF16)

    spec = pl.BlockSpec((1, rb, cols), lambda j, i, c_ref: (j, i, 0))
    return pl.pallas_call(
        body, name=name,
        grid_spec=pltpu.PrefetchScalarGridSpec(
            num_scalar_prefetch=1, grid=(N_SHARD, nrb),
            in_specs=[pl.BlockSpec((1, rb, cols), lambda j, i, c_ref: (j, c_ref[0] * nrb + i, 0)), spec],
            out_specs=spec),
        out_shape=jax.ShapeDtypeStruct(got.shape, BF16),
        compiler_params=_params(("parallel", "parallel")))(c.reshape(1), g, got)


def _add_chips(g, got, landed, k, c, name):
    _, rh, cols = got.shape
    rb = _tile(rh, 256)
    nrb = rh // rb

    def body(kc_ref, g_ref, got_ref, l_ref, o_ref):
        own = g_ref[0] + got_ref[0]
        o_ref[...] = own + l_ref[0].astype(F32) + l_ref[1].astype(F32) + l_ref[2].astype(F32)

    half_c = lambda i, kc: (kc[1] * nrb + i, 0)
    return pl.pallas_call(
        body, name=name,
        grid_spec=pltpu.PrefetchScalarGridSpec(
            num_scalar_prefetch=1, grid=(nrb,),
            in_specs=[pl.BlockSpec((1, rb, cols), lambda i, kc: (kc[0],) + half_c(i, kc)),
                      pl.BlockSpec((1, rb, cols), lambda i, kc: (kc[0], i, 0)),
                      pl.BlockSpec((N_SHARD - 1, rb, cols), lambda i, kc: (0, i, 0))],
            out_specs=pl.BlockSpec((rb, cols), half_c)),
        out_shape=jax.ShapeDtypeStruct((2 * rh, cols), F32),
        compiler_params=_params(("parallel",)))(jnp.stack([k, c]), g, got, landed)


def _adamw(wv, g, m, v, name):
    rows, cols = wv.shape
    rb = _tile(rows, 256)
    c1 = 1.0 - ADAM_B1 ** ADAM_STEP
    c2 = 1.0 - ADAM_B2 ** ADAM_STEP

    def body(w_ref, g_ref, m_ref, v_ref, go_ref, d_ref, nm_ref, nv_ref):
        gv = g_ref[...]
        go_ref[...] = gv
        nm = ADAM_B1 * m_ref[...] + (1.0 - ADAM_B1) * gv
        nv = ADAM_B2 * v_ref[...] + (1.0 - ADAM_B2) * (gv * gv)
        nm_ref[...] = nm
        nv_ref[...] = nv
        d_ref[...] = -ADAM_LR * ((nm / c1) / (jnp.sqrt(nv / c2) + ADAM_EPS) + ADAM_WD * w_ref[...])

    spec = pl.BlockSpec((rb, cols), lambda i: (i, 0))
    return pl.pallas_call(
        body, name=name, grid=(rows // rb,), in_specs=[spec] * 4, out_specs=[spec] * 4,
        out_shape=[jax.ShapeDtypeStruct((rows, cols), F32)] * 4,
        compiler_params=_params(("parallel",)))(wv, g, m, v)


def _pack(flats):
    cat = jnp.concatenate([f.reshape(-1) for f in flats])
    n = cat.shape[0]
    rows = -(-n // (8 * LANES)) * 8
    return jnp.pad(cat, (0, rows * LANES - n)).reshape(rows, LANES)


def _unpack(packed, shapes):
    flat = packed.reshape(-1)
    out, off = [], 0
    for shp in shapes:
        n = 1
        for dim in shp:
            n *= dim
        out.append(flat[off:off + n].reshape(shp))
        off += n
    return out


def kernel(x, p, norm_g, w_in, conv_w, conv_b, dt_bias, a_log, d_skip, gnorm_g, pool_mix_w, pool_mix_b, pool_scale, w_branch_a, w_branch_b, w_out, ple_norm_g, w_ple_gate, w_ple_up, final_g, loss_target, m_norm_g, m_w_in, m_conv_w, m_conv_b, m_dt_bias, m_a_log, m_d_skip, m_gnorm_g, m_pool_mix_w, m_pool_mix_b, m_pool_scale, m_w_branch_a, m_w_branch_b, m_w_out, m_ple_norm_g, m_w_ple_gate, m_w_ple_up, m_final_g, v_norm_g, v_w_in, v_conv_w, v_conv_b, v_dt_bias, v_a_log, v_d_skip, v_gnorm_g, v_pool_mix_w, v_pool_mix_b, v_pool_scale, v_w_branch_a, v_w_branch_b, v_w_out, v_ple_norm_g, v_w_ple_gate, v_w_ple_up, v_final_g):
    wts = dict(norm_g=norm_g, w_in=w_in, conv_w=conv_w, conv_b=conv_b, dt_bias=dt_bias, a_log=a_log, d_skip=d_skip,
               gnorm_g=gnorm_g, pool_mix_w=pool_mix_w, pool_mix_b=pool_mix_b, pool_scale=pool_scale,
               w_branch_a=w_branch_a, w_branch_b=w_branch_b, w_out=w_out, ple_norm_g=ple_norm_g,
               w_ple_gate=w_ple_gate, w_ple_up=w_ple_up, final_g=final_g)
    mom_m = dict(norm_g=m_norm_g, w_in=m_w_in, conv_w=m_conv_w, conv_b=m_conv_b, dt_bias=m_dt_bias, a_log=m_a_log,
                 d_skip=m_d_skip, gnorm_g=m_gnorm_g, pool_mix_w=m_pool_mix_w, pool_mix_b=m_pool_mix_b,
                 pool_scale=m_pool_scale, w_branch_a=m_w_branch_a, w_branch_b=m_w_branch_b, w_out=m_w_out,
                 ple_norm_g=m_ple_norm_g, w_ple_gate=m_w_ple_gate, w_ple_up=m_w_ple_up, final_g=m_final_g)
    mom_v = dict(norm_g=v_norm_g, w_in=v_w_in, conv_w=v_conv_w, conv_b=v_conv_b, dt_bias=v_dt_bias, a_log=v_a_log,
                 d_skip=v_d_skip, gnorm_g=v_gnorm_g, pool_mix_w=v_pool_mix_w, pool_mix_b=v_pool_mix_b,
                 pool_scale=v_pool_scale, w_branch_a=v_w_branch_a, w_branch_b=v_w_branch_b, w_out=v_w_out,
                 ple_norm_g=v_ple_norm_g, w_ple_gate=v_w_ple_gate, w_ple_up=v_w_ple_up, final_g=v_final_g)
    c = lax.axis_index("c")
    k = 2 * lax.axis_index("x") + lax.axis_index("y")
    flat2 = lambda a: a.reshape(-1, a.shape[-1])

    gathered = _gather_weights([_into_slot(flat2(wts[n]), k, BF16, "slot_" + n) for n in BIG],
                               [_into_slot(flat2(conv_w), k, F32, "slot_conv_w")])
    wg = dict(zip(BIG + ("conv_w",), gathered))

    loss, grad_x, grads = _local_step(x, p[0], loss_target, wg, {n: wts[n] for n in SMALL})

    g_big = [grads[n] for n in BIG]
    got = _swap_halves(g_big)
    pairs = [_add_pair(g, o, c, "add_pair_" + n) for n, g, o in zip(BIG, g_big, got)]
    landed = _scatter_chips(pairs)
    half_sums = [_add_chips(g, o, ld, k, c, "add_chips_" + n) for n, g, o, ld in zip(BIG, g_big, got, landed)]
    g_shards = dict(zip(BIG, _join_halves(half_sums)))

    conv_shape = flat2(conv_w).shape
    small_sum = _allreduce_small(_pack([grads[n] for n in SMALL] + [grads["conv_w"], loss]))
    small_shapes = [wts[n].shape for n in SMALL] + [(N_SHARD,) + conv_shape, (1,)]
    small_g = _unpack(small_sum, small_shapes)
    g_conv = lax.dynamic_index_in_dim(small_g[-2], k, axis=0, keepdims=False)

    outs = {}
    for n in BIG:
        vals = _adamw(flat2(wts[n]), g_shards[n], flat2(mom_m[n]), flat2(mom_v[n]), "adamw_" + n)
        for kind, val in zip(("grad", "delta", "new_m", "new_v"), vals):
            outs[kind, n] = val.reshape(wts[n].shape)
    names = SMALL + ("conv_w",)
    sm = _adamw(_pack([wts[n] for n in names]), _pack(small_g[:len(SMALL)] + [g_conv]),
                _pack([mom_m[n] for n in names]), _pack([mom_v[n] for n in names]), "adamw_small")
    sm_shapes = [wts[n].shape for n in names]
    for kind, val in zip(("grad", "delta", "new_m", "new_v"), sm):
        for n, piece in zip(names, _unpack(val, sm_shapes)):
            outs[kind, n] = piece
    return (small_g[-1][0], grad_x, *[outs[kind, n] for kind in ("grad", "delta", "new_m", "new_v") for n in WEIGHTS])
```

```python
import functools

import jax
import jax.numpy as jnp
from jax import lax
from jax.experimental import pallas as pl
from jax.experimental.pallas import tpu as pltpu

F32 = jnp.float32
BF16 = jnp.bfloat16
HIGHEST = lax.Precision.HIGHEST
MESH = pl.DeviceIdType.MESH

EPS = 1e-6
HEAD_DIM = 64
SSM_GROUPS = 4
D_STATE = 128
CONV_WIDTH = 4
CHUNK = 128
N_POOL = 4
LANES = 128
N_SHARD = 4

ADAM_LR = 0.001
ADAM_B1 = 0.9
ADAM_B2 = 0.999
ADAM_EPS = 1e-08
ADAM_WD = 0.01
ADAM_STEP = 10

BIG = ("w_in", "pool_mix_w", "w_branch_a", "w_branch_b", "w_out", "w_ple_gate", "w_ple_up")
SMALL = ("norm_g", "conv_b", "dt_bias", "a_log", "d_skip", "gnorm_g", "pool_mix_b", "pool_scale",
         "ple_norm_g", "final_g")
WEIGHTS = ("norm_g", "w_in", "conv_w", "conv_b", "dt_bias", "a_log", "d_skip", "gnorm_g", "pool_mix_w",
           "pool_mix_b", "pool_scale", "w_branch_a", "w_branch_b", "w_out", "ple_norm_g", "w_ple_gate",
           "w_ple_up", "final_g")


def _params(sem=None, vmem_mb=56):
    kw = dict(vmem_limit_bytes=vmem_mb << 20)
    if sem is not None:
        kw["dimension_semantics"] = sem
    return pltpu.CompilerParams(**kw)


def _sigmoid(v):
    return 1.0 / (1.0 + jnp.exp(-v))


def _softplus(v):
    return jnp.maximum(v, 0.0) + jnp.log1p(jnp.exp(-jnp.abs(v)))


def _bdot(a, b):
    return jnp.dot(a.astype(BF16), b.astype(BF16), preferred_element_type=F32)


def _bdot_nt(a, b):
    return lax.dot_general(a.astype(BF16), b.astype(BF16), (((1,), (1,)), ((), ())), preferred_element_type=F32)


def _bdot_tn(a, b):
    return lax.dot_general(a.astype(BF16), b.astype(BF16), (((0,), (0,)), ((), ())), preferred_element_type=F32)


def _col_block(col0, width):
    assert col0 % width == 0, (col0, width)
    return col0 // width


def _tile(n, cap):
    if n <= cap:
        return n
    best = None
    for cand in range(8, cap + 1, 8):
        if n % cand == 0:
            best = cand
    assert best is not None, (n, cap)
    return best


def _shift_down(v, j, row):
    return jnp.where(row >= j, pltpu.roll(v, j, 0), 0.0)


def _shift_up(v, j, row):
    n = v.shape[0]
    return jnp.where(row < n - j, pltpu.roll(v, n - j, 0), 0.0)


def _mm(a, w, name, tm=1024, tn=1024):
    t, k = a.shape
    tm = min(tm, t)
    blocked = w.ndim == 3
    if blocked:
        nblk, _, tn = w.shape
        n = nblk * tn
        w_spec = pl.BlockSpec((1, k, tn), lambda i, j: (j, 0, 0))
    else:
        n = w.shape[1]
        tn = min(tn, n)
        w_spec = pl.BlockSpec((k, tn), lambda i, j: (0, j))

    def body(a_ref, w_ref, o_ref):
        wv = w_ref[0] if blocked else w_ref[...]
        o_ref[...] = _bdot(a_ref[...], wv)

    return pl.pallas_call(
        body, name=name, grid=(t // tm, n // tn),
        in_specs=[pl.BlockSpec((tm, k), lambda i, j: (i, 0)), w_spec],
        out_specs=pl.BlockSpec((tm, tn), lambda i, j: (i, j)),
        out_shape=jax.ShapeDtypeStruct((t, n), F32),
        compiler_params=_params(("parallel", "parallel")))(a, w)


def _mm_nt(a, w, name, tm=1024, tk=1024):
    t, k = a.shape
    n = w.shape[0]
    tm, tk = min(tm, t), min(tk, k)

    def body(a_ref, w_ref, o_ref):
        kk = pl.program_id(1)
        part = _bdot_nt(a_ref[...], w_ref[...])

        @pl.when(kk == 0)
        def _():
            o_ref[...] = part

        @pl.when(kk > 0)
        def _():
            o_ref[...] += part

    return pl.pallas_call(
        body, name=name, grid=(t // tm, k // tk),
        in_specs=[pl.BlockSpec((tm, tk), lambda i, j: (i, j)), pl.BlockSpec((n, tk), lambda i, j: (0, j))],
        out_specs=pl.BlockSpec((tm, n), lambda i, j: (i, 0)),
        out_shape=jax.ShapeDtypeStruct((t, n), F32),
        compiler_params=_params(("parallel", "arbitrary")))(a, w)


def _mm_tn(a, b, name, tn=1024, tk=1024, col_blocks=False):
    t, m = a.shape
    n = b.shape[1]
    tn, tk = min(tn, n), min(tk, t)

    def body(a_ref, b_ref, o_ref):
        kk = pl.program_id(1)
        part = _bdot_tn(a_ref[...], b_ref[...])
        part = part[None] if col_blocks else part

        @pl.when(kk == 0)
        def _():
            o_ref[...] = part

        @pl.when(kk > 0)
        def _():
            o_ref[...] += part

    if col_blocks:
        out_spec = pl.BlockSpec((1, m, tn), lambda j, kk: (j, 0, 0))
        out_shape = jax.ShapeDtypeStruct((n // tn, m, tn), F32)
    else:
        out_spec = pl.BlockSpec((m, tn), lambda j, kk: (0, j))
        out_shape = jax.ShapeDtypeStruct((m, n), F32)
    return pl.pallas_call(
        body, name=name, grid=(n // tn, t // tk),
        in_specs=[pl.BlockSpec((tk, m), lambda j, kk: (kk, 0)), pl.BlockSpec((tk, tn), lambda j, kk: (kk, j))],
        out_specs=out_spec, out_shape=out_shape,
        compiler_params=_params(("parallel", "arbitrary")))(a, b)


def _w_in_pieces(d, di, dc, nh, shard_w):
    pgd = d // N_POOL
    o_dt, o_u = di + dc, di + dc + nh
    o_zp, o_ga, o_gb = o_u + d, o_u + 2 * d, o_u + 3 * d
    c_z, c_uz = 2 * d, 2 * d + di + dc
    runs = [(False, 0, o_ga, d), (False, d, o_gb, d), (False, c_z, 0, di + dc), (True, 0, o_dt, nh)]
    for g in range(N_POOL):
        runs.append((False, c_uz + 2 * g * pgd, o_u + g * pgd, pgd))
        runs.append((False, c_uz + (2 * g + 1) * pgd, o_zp + g * pgd, pgd))
    pieces = []
    for is_dt, dst, src, n in runs:
        while n > 0:
            k, off = divmod(src, shard_w)
            m = min(n, shard_w - off)
            pieces.append((is_dt, dst, k, off, m))
            dst, src, n = dst + m, src + m, n - m
    return pieces


def _regroup_w_in(w_sh, d, di, dc, nh, rb=256):
    _, rows, sw = w_sh.shape
    n_main = 4 * d + di + dc
    pieces = _w_in_pieces(d, di, dc, nh, sw)
    rb = min(rb, rows)

    def body(w_ref, main_ref, dt_ref):
        dt_ref[...] = jnp.zeros_like(dt_ref)
        for is_dt, dst, k, off, m in pieces:
            out = dt_ref if is_dt else main_ref
            out[:, dst:dst + m] = w_ref[k, :, off:off + m]

    return pl.pallas_call(
        body, name="regroup_w_in", grid=(rows // rb,),
        in_specs=[pl.BlockSpec((N_SHARD, rb, sw), lambda i: (0, i, 0))],
        out_specs=[pl.BlockSpec((rb, n_main), lambda i: (i, 0)), pl.BlockSpec((rb, LANES), lambda i: (i, 0))],
        out_shape=[jax.ShapeDtypeStruct((rows, n_main), w_sh.dtype), jax.ShapeDtypeStruct((rows, LANES), w_sh.dtype)],
        compiler_params=_params(("parallel",)))(w_sh)


def _ungroup_w_in(d_main, d_dt, d, di, dc, nh, rb=128):
    rows, n_main = d_main.shape
    sw = (n_main + nh) // N_SHARD
    pieces = _w_in_pieces(d, di, dc, nh, sw)
    rb = min(rb, rows)

    def body(main_ref, dt_ref, o_ref):
        for is_dt, dst, k, off, m in pieces:
            src = dt_ref if is_dt else main_ref
            o_ref[k, :, off:off + m] = src[:, dst:dst + m]

    return pl.pallas_call(
        body, name="ungroup_w_in", grid=(rows // rb,),
        in_specs=[pl.BlockSpec((rb, n_main), lambda i: (i, 0)), pl.BlockSpec((rb, LANES), lambda i: (i, 0))],
        out_specs=pl.BlockSpec((N_SHARD, rb, sw), lambda i: (0, i, 0)),
        out_shape=jax.ShapeDtypeStruct((N_SHARD, rows, sw), F32),
        compiler_params=_params(("parallel",)))(d_main, d_dt)


def _inproj(x2, norm_g, w_main, w_dt, tm=1024, tn=1024):
    t, d = x2.shape
    n = w_main.shape[1]
    tm, tn = min(tm, t), min(tn, n)

    def body(x_ref, g_ref, w_ref, wdt_ref, proj_ref, dt_ref, h_ref):
        @pl.when(pl.program_id(1) == 0)
        def _():
            xv = x_ref[...]
            r = lax.rsqrt(jnp.mean(xv * xv, axis=-1, keepdims=True) + EPS)
            h = (xv * r * g_ref[...]).astype(BF16)
            h_ref[...] = h
            dt_ref[...] = jnp.dot(h, wdt_ref[...].astype(BF16), preferred_element_type=F32)

        proj_ref[...] = jnp.dot(h_ref[...], w_ref[...].astype(BF16), preferred_element_type=F32)

    return pl.pallas_call(
        body, name="inproj", grid=(t // tm, n // tn),
        in_specs=[pl.BlockSpec((tm, d), lambda i, j: (i, 0)), pl.BlockSpec((1, d), lambda i, j: (0, 0)),
                  pl.BlockSpec((d, tn), lambda i, j: (0, j)), pl.BlockSpec((d, LANES), lambda i, j: (0, 0))],
        out_specs=[pl.BlockSpec((tm, tn), lambda i, j: (i, j)), pl.BlockSpec((tm, LANES), lambda i, j: (i, 0)),
                   pl.BlockSpec((tm, d), lambda i, j: (i, 0))],
        out_shape=[jax.ShapeDtypeStruct((t, n), F32), jax.ShapeDtypeStruct((t, LANES), F32),
                   jax.ShapeDtypeStruct((t, d), BF16)],
        compiler_params=_params(("parallel", "arbitrary")))(x2, norm_g, w_main, w_dt)


def _conv_w_spec(conv_w, cb, j_axis):
    sw = conv_w.shape[2]
    assert sw % cb == 0, (sw, cb)
    per = sw // cb
    return N_SHARD * per, pl.BlockSpec((1, CONV_WIDTH, cb), lambda *ij: (ij[j_axis] // per, 0, ij[j_axis] % per))


CONV_ROWS = 64
CONV_HALO = 8


def _conv_taps(x_ref, t0, rc):
    if t0 == 0:
        cur = x_ref[0:rc, :]
        row = lax.broadcasted_iota(jnp.int32, cur.shape, 0)
        return [cur] + [_shift_down(cur, j, row) for j in range(1, CONV_WIDTH)]
    ext = x_ref[t0 - CONV_HALO:t0 + rc, :]
    return [ext[CONV_HALO:]] + [pltpu.roll(ext, j, 0)[CONV_HALO:] for j in range(1, CONV_WIDTH)]


def _conv_weights(w_ref):
    return [w_ref[0, CONV_WIDTH - 1 - j:CONV_WIDTH - j, :] for j in range(CONV_WIDTH)]


def _conv_pre(taps, wts, bias):
    acc = bias + taps[0] * wts[0]
    for j in range(1, CONV_WIDTH):
        acc = acc + taps[j] * wts[j]
    return acc


def _conv_fwd(proj, conv_w, conv_b, nb, s, col0, cb=256):
    n_blk, w_spec = _conv_w_spec(conv_w, cb, 1)
    blk0 = _col_block(col0, cb)
    rc = min(CONV_ROWS, s)

    def body(x_ref, w_ref, b_ref, o_ref):
        wts, bias = _conv_weights(w_ref), b_ref[...]
        for t0 in range(0, s, rc):
            acc = _conv_pre(_conv_taps(x_ref, t0, rc), wts, bias)
            o_ref[t0:t0 + rc, :] = acc * _sigmoid(acc)

    return pl.pallas_call(
        body, name="conv_fwd", grid=(nb, n_blk),
        in_specs=[pl.BlockSpec((s, cb), lambda b, j: (b, blk0 + j)), w_spec, pl.BlockSpec((1, cb), lambda b, j: (0, j))],
        out_specs=pl.BlockSpec((s, cb), lambda b, j: (b, j)),
        out_shape=jax.ShapeDtypeStruct((nb * s, n_blk * cb), F32),
        compiler_params=_params(("parallel", "parallel")))(proj, conv_w, conv_b)


def _ssd_consts(di):
    r = lax.broadcasted_iota(jnp.int32, (CHUNK, CHUNK), 0)
    c = lax.broadcasted_iota(jnp.int32, (CHUNK, CHUNK), 1)
    tril = (r >= c).astype(F32)
    head = lax.broadcasted_iota(jnp.int32, (LANES, di), 0)
    chan = lax.broadcasted_iota(jnp.int32, (LANES, di), 1) // HEAD_DIM
    expand = (head == chan).astype(BF16)
    return tril, expand, expand.T


def _expand(v, e, terms=3):
    acc = None
    for _ in range(terms):
        vb = v.astype(BF16)
        part = jnp.dot(vb, e, preferred_element_type=F32)
        acc = part if acc is None else acc + part
        v = v - vb.astype(F32)
    return acc


def _head_sum(t, et, terms=2):
    acc = None
    for _ in range(terms):
        tb = t.astype(BF16)
        part = jnp.dot(tb, et, preferred_element_type=F32)
        acc = part if acc is None else acc + part
        t = t - tb.astype(F32)
    return acc


def _ssd_scalars(dtr_ref, dtb_ref, alog_ref, tri):
    dtpre = dtr_ref[...] + dtb_ref[...]
    dt = _softplus(dtpre)
    a_neg = -jnp.exp(alog_ref[...])
    a_dt = dt * a_neg
    a_cs = jnp.dot(tri, a_dt, precision=HIGHEST, preferred_element_type=F32)
    a_cst = lax.dot_general(a_dt, tri, (((0,), (1,)), ((), ())), precision=HIGHEST, preferred_element_type=F32)
    return dtpre, dt, a_neg, a_cs, a_cst


def _ssd_fwd(proj, xbc, dtraw, dtb, alog, dskx, gn, nb, s, di, z_col0):
    t = nb * s
    nc = s // CHUNK
    hpg = di // HEAD_DIM // SSM_GROUPS
    gw = di // SSM_GROUPS
    gn_w = SSM_GROUPS * D_STATE
    b_blk = _col_block(di, gn_w)
    z_blk = _col_block(z_col0, di)
    L, P, N = CHUNK, HEAD_DIM, D_STATE
    tril, expand, _ = _ssd_consts(di)

    def body(z_ref, x_ref, b_ref, c_ref, dtr_ref, dtb_ref, alog_ref, dskx_ref, gn_ref, tril_ref, e_ref,
             ypre_ref, yan_ref, hp_ref, st_ref, yd_ref, xdt_ref):
        @pl.when(pl.program_id(1) == 0)
        def _():
            st_ref[...] = jnp.zeros_like(st_ref)

        hp_ref[0] = st_ref[...]
        tri = tril_ref[...]
        _, dt, _, a_cs, a_cst = _ssd_scalars(dtr_ref, dtb_ref, alog_ref, tri)
        ev = e_ref[...]
        a_exp = _expand(a_cs, ev)
        xv = x_ref[...]
        xdt = xv * _expand(dt, ev)
        xdt_ref[...] = xdt
        a_last = a_exp[L - 1:L, :]
        xe = xdt * jnp.exp(a_last - a_exp)
        ea = jnp.exp(a_exp)
        e_last = jnp.exp(a_last)
        lower = tri > 0.5
        for g in range(SSM_GROUPS):
            gs = slice(g * gw, (g + 1) * gw)
            bg = b_ref[:, g * N:(g + 1) * N].astype(BF16)
            cg = c_ref[:, g * N:(g + 1) * N].astype(BF16)
            gm = _bdot_nt(cg, bg)
            ht = st_ref[:, gs]
            ch = _bdot(cg, ht)
            for e in range(hpg):
                h = g * hpg + e
                hs = slice(h * P, (h + 1) * P)
                decay = jnp.where(lower, jnp.exp(a_cs[:, h:h + 1] - a_cst[h:h + 1, :]), 0.0)
                yd_ref[:, hs] = _bdot(gm * decay, xdt_ref[:, hs])
            st_ref[:, gs] = ht * e_last[:, gs] + _bdot_tn(bg, xe[:, gs])
            ypre = yd_ref[:, gs] + ea[:, gs] * ch + xv[:, gs] * dskx_ref[:, gs]
            ypre_ref[:, gs] = ypre
            zv = z_ref[:, gs]
            v = ypre * zv * _sigmoid(zv)
            r = lax.rsqrt(jnp.mean(v * v, axis=-1, keepdims=True) + EPS)
            yan_ref[:, gs] = (v * r * gn_ref[:, gs]).astype(BF16)

    row = lambda b, c: b * nc + c
    vec = lambda w: pl.BlockSpec((1, w), lambda b, c: (0, 0))
    return pl.pallas_call(
        body, name="ssd_fwd", grid=(nb, nc),
        in_specs=[pl.BlockSpec((L, di), lambda b, c: (row(b, c), z_blk)),
                  pl.BlockSpec((L, di), lambda b, c: (row(b, c), 0)),
                  pl.BlockSpec((L, gn_w), lambda b, c: (row(b, c), b_blk)),
                  pl.BlockSpec((L, gn_w), lambda b, c: (row(b, c), b_blk + 1)),
                  pl.BlockSpec((L, LANES), lambda b, c: (row(b, c), 0)),
                  vec(LANES), vec(LANES), vec(di), vec(di),
                  pl.BlockSpec((L, L), lambda b, c: (0, 0)),
                  pl.BlockSpec((LANES, di), lambda b, c: (0, 0))],
        out_specs=[pl.BlockSpec((L, di), lambda b, c: (row(b, c), 0)),
                   pl.BlockSpec((L, di), lambda b, c: (row(b, c), 0)),
                   pl.BlockSpec((1, N, di), lambda b, c: (row(b, c), 0, 0))],
        out_shape=[jax.ShapeDtypeStruct((t, di), F32), jax.ShapeDtypeStruct((t, di), BF16),
                   jax.ShapeDtypeStruct((nb * nc, N, di), F32)],
        scratch_shapes=[pltpu.VMEM((N, di), F32), pltpu.VMEM((L, di), F32), pltpu.VMEM((L, di), F32)],
        compiler_params=_params(("parallel", "arbitrary")))(
            proj, xbc, xbc, xbc, dtraw, dtb, alog, dskx, gn, tril, expand)


def _pool_sum(v, g, row, shift):
    s2 = v + shift(v, 1, row)
    s4 = s2 + shift(s2, 2, row)
    s8 = s4 + shift(s4, 4, row)
    s16 = s8 + shift(s8, 8, row)
    return jnp.where(g == 0, s2, jnp.where(g == 1, s4, jnp.where(g == 2, s8, s16)))


def _pool_count(g, row):
    return jnp.minimum(row + 1, jnp.left_shift(2, g)).astype(F32)


def _pool_fwd(proj, mix_w, mix_b, scale, nb, s, col0):
    pgd = mix_w.shape[-1]
    blk0 = _col_block(col0, 2 * pgd)

    def body(uz_ref, w_ref, b_ref, sc_ref, o_ref):
        g = pl.program_id(1)
        u = uz_ref[:, :pgd]
        zp = uz_ref[:, pgd:]
        row = lax.broadcasted_iota(jnp.int32, u.shape, 0)
        pooled = _pool_sum(u, g, row, _shift_down) / _pool_count(g, row) - u
        mixed = _bdot(pooled, w_ref[:, 0].reshape(pgd, pgd)) + b_ref[...]
        o_ref[...] = (mixed * sc_ref[...] * zp * _sigmoid(zp)).astype(BF16)

    return pl.pallas_call(
        body, name="pool_fwd", grid=(nb, N_POOL),
        in_specs=[pl.BlockSpec((s, 2 * pgd), lambda b, g: (b, blk0 + g)),
                  pl.BlockSpec((N_SHARD, 1, pgd // N_SHARD, pgd), lambda b, g: (0, g, 0, 0)),
                  pl.BlockSpec((1, pgd), lambda b, g: (0, g)), pl.BlockSpec((1, pgd), lambda b, g: (0, g))],
        out_specs=pl.BlockSpec((s, pgd), lambda b, g: (b, g)),
        out_shape=jax.ShapeDtypeStruct((nb * s, N_POOL * pgd), BF16),
        compiler_params=_params(("parallel", "parallel")))(proj, mix_w, mix_b, scale)


def _merge_fwd(ya, yb, proj, col0, tm=512):
    t, d = ya.shape
    tm = min(tm, t)
    blk = _col_block(col0, 2 * d)

    def body(ya_ref, yb_ref, g_ref, o_ref):
        o_ref[...] = (_sigmoid(g_ref[:, :d]) * ya_ref[...] + _sigmoid(g_ref[:, d:]) * yb_ref[...]).astype(BF16)

    row = pl.BlockSpec((tm, d), lambda i: (i, 0))
    return pl.pallas_call(
        body, name="merge_fwd", grid=(t // tm,),
        in_specs=[row, row, pl.BlockSpec((tm, 2 * d), lambda i: (i, blk))],
        out_specs=row, out_shape=jax.ShapeDtypeStruct((t, d), BF16),
        compiler_params=_params(("parallel",)))(ya, yb, proj)


def _ple_pre(x2, mo, ple_g, tm=512):
    t, d = x2.shape
    tm = min(tm, t)

    def body(x_ref, mo_ref, g_ref, x1_ref, hn_ref):
        x1 = x_ref[...] + mo_ref[...]
        x1_ref[...] = x1
        r = lax.rsqrt(jnp.mean(x1 * x1, axis=-1, keepdims=True) + EPS)
        hn_ref[...] = (x1 * r * g_ref[...]).astype(BF16)

    row = pl.BlockSpec((tm, d), lambda i: (i, 0))
    return pl.pallas_call(
        body, name="ple_pre", grid=(t // tm,),
        in_specs=[row, row, pl.BlockSpec((1, d), lambda i: (0, 0))],
        out_specs=[row, row],
        out_shape=[jax.ShapeDtypeStruct((t, d), F32), jax.ShapeDtypeStruct((t, d), BF16)],
        compiler_params=_params(("parallel",)))(x2, mo, ple_g)


def _tail(x1, pre, pu, tgt, final_g, tm=512):
    t, d = x1.shape
    tm = min(tm, t)

    def body(x1_ref, pre_ref, pu_ref, tgt_ref, g_ref, dx2_ref, dpre_ref, dpu_ref, loss_ref, dg_ref):
        @pl.when(pl.program_id(0) == 0)
        def _():
            loss_ref[...] = jnp.zeros_like(loss_ref)
            dg_ref[...] = jnp.zeros_like(dg_ref)

        gate = _sigmoid(pre_ref[...])
        pu = pu_ref[...]
        x2 = x1_ref[...] + gate * pu
        r = lax.rsqrt(jnp.mean(x2 * x2, axis=-1, keepdims=True) + EPS)
        xn = x2 * r
        fg = g_ref[...]
        err = xn * fg - tgt_ref[...]
        loss_ref[...] += 0.5 * jnp.sum(jnp.mean(err * err, axis=-1, keepdims=True))
        dy = err * (1.0 / d)
        dg_ref[...] += jnp.sum(dy * xn, axis=0, keepdims=True)
        dxn = dy * fg
        dx2 = r * (dxn - xn * jnp.mean(dxn * xn, axis=-1, keepdims=True))
        dx2_ref[...] = dx2
        dpre_ref[...] = (dx2 * pu * gate * (1.0 - gate)).astype(BF16)
        dpu_ref[...] = (dx2 * gate).astype(BF16)

    row = pl.BlockSpec((tm, d), lambda i: (i, 0))
    return pl.pallas_call(
        body, name="tail", grid=(t // tm,),
        in_specs=[row, row, row, row, pl.BlockSpec((1, d), lambda i: (0, 0))],
        out_specs=[row, row, row, pl.BlockSpec((1, LANES), lambda i: (0, 0)), pl.BlockSpec((1, d), lambda i: (0, 0))],
        out_shape=[jax.ShapeDtypeStruct((t, d), F32)] + [jax.ShapeDtypeStruct((t, d), BF16)] * 2 + [
            jax.ShapeDtypeStruct((1, LANES), F32),
                                                             jax.ShapeDtypeStruct((1, d), F32)],
        compiler_params=_params(("arbitrary",)))(x1, pre, pu, tgt, final_g)


def _rms_bwd(xin, dhs, dres, g, name, tm=512):
    t, d = xin.shape
    tm = min(tm, t)
    n_dh = len(dhs)

    def body(*refs):
        x_ref, dh_refs, dres_ref, g_ref, dx_ref, dg_ref = refs[0], refs[1:1 + n_dh], *refs[1 + n_dh:]

        @pl.when(pl.program_id(0) == 0)
        def _():
            dg_ref[...] = jnp.zeros_like(dg_ref)

        xv = x_ref[...]
        dh = dh_refs[0][...]
        for ref in dh_refs[1:]:
            dh = dh + ref[...]
        r = lax.rsqrt(jnp.mean(xv * xv, axis=-1, keepdims=True) + EPS)
        xn = xv * r
        dg_ref[...] += jnp.sum(dh * xn, axis=0, keepdims=True)
        dd = dh * g_ref[...]
        dx_ref[...] = dres_ref[...] + r * (dd - xn * jnp.mean(dd * xn, axis=-1, keepdims=True))

    row = pl.BlockSpec((tm, d), lambda i: (i, 0))
    vec = pl.BlockSpec((1, d), lambda i: (0, 0))
    return pl.pallas_call(
        body, name=name, grid=(t // tm,),
        in_specs=[row] * (2 + n_dh) + [vec],
        out_specs=[row, vec],
        out_shape=[jax.ShapeDtypeStruct((t, d), F32), jax.ShapeDtypeStruct((1, d), F32)],
        compiler_params=_params(("arbitrary",)))(xin, *dhs, dres, g)


def _merge_bwd(dm, ya, yb, proj, col0, n_cols, tm=512):
    t, d = ya.shape
    tm = min(tm, t)
    blk = _col_block(col0, 2 * d)

    def body(dm_ref, ya_ref, yb_ref, g_ref, dya_ref, dyb_ref, dg_ref):
        dm_v = dm_ref[...]
        sa = _sigmoid(g_ref[:, :d])
        sb = _sigmoid(g_ref[:, d:])
        dya_ref[...] = (dm_v * sa).astype(BF16)
        dyb_ref[...] = (dm_v * sb).astype(BF16)
        dg_ref[:, :d] = (dm_v * ya_ref[...] * sa * (1.0 - sa)).astype(BF16)
        dg_ref[:, d:] = (dm_v * yb_ref[...] * sb * (1.0 - sb)).astype(BF16)

    row = pl.BlockSpec((tm, d), lambda i: (i, 0))
    gspec = pl.BlockSpec((tm, 2 * d), lambda i: (i, blk))
    return pl.pallas_call(
        body, name="merge_bwd", grid=(t // tm,),
        in_specs=[row, row, row, gspec],
        out_specs=[row, row, gspec],
        out_shape=[jax.ShapeDtypeStruct((t, d), BF16), jax.ShapeDtypeStruct((t, d), BF16),
                   jax.ShapeDtypeStruct((t, n_cols), BF16)],
        compiler_params=_params(("parallel",)))(dm, ya, yb, proj)


def _pool_bwd(proj, dyb, dproj, mix_w, mix_b, scale, nb, s, col0):
    pgd = mix_w.shape[-1]
    blk0 = _col_block(col0, 2 * pgd)

    def body(uz_ref, dy_ref, _, w_ref, b_ref, sc_ref, duz_ref, dw_ref, db_ref, dsc_ref):
        g = pl.program_id(0)

        @pl.when(pl.program_id(1) == 0)
        def _():
            dw_ref[...] = jnp.zeros_like(dw_ref)
            db_ref[...] = jnp.zeros_like(db_ref)
            dsc_ref[...] = jnp.zeros_like(dsc_ref)

        u = uz_ref[:, :pgd]
        zp = uz_ref[:, pgd:]
        row = lax.broadcasted_iota(jnp.int32, u.shape, 0)
        cnt = _pool_count(g, row)
        pooled = _pool_sum(u, g, row, _shift_down) / cnt - u
        wv = w_ref[:, 0].reshape(pgd, pgd)
        mixed = _bdot(pooled, wv) + b_ref[...]
        sg = _sigmoid(zp)
        sz = zp * sg
        dy = dy_ref[...]
        sc = sc_ref[...]
        dsc_ref[...] += jnp.sum(dy * mixed * sz, axis=0, keepdims=True)
        dmixed = dy * sc * sz
        db_ref[...] += jnp.sum(dmixed, axis=0, keepdims=True)
        dw_ref[:, 0] += _bdot_tn(pooled, dmixed).reshape(N_SHARD, pgd // N_SHARD, pgd)
        dpooled = _bdot_nt(dmixed, wv)
        duz_ref[:, :pgd] = (_pool_sum(dpooled / cnt, g, row, _shift_up) - dpooled).astype(BF16)
        duz_ref[:, pgd:] = (dy * mixed * sc * sg * (1.0 + zp * (1.0 - sg))).astype(BF16)

    uz = pl.BlockSpec((s, 2 * pgd), lambda g, b: (b, blk0 + g))
    vec = pl.BlockSpec((1, pgd), lambda g, b: (0, g))
    wspec = pl.BlockSpec((N_SHARD, 1, pgd // N_SHARD, pgd), lambda g, b: (0, g, 0, 0))
    return pl.pallas_call(
        body, name="pool_bwd", grid=(N_POOL, nb),
        in_specs=[uz, pl.BlockSpec((s, pgd), lambda g, b: (b, g)), pl.BlockSpec(memory_space=pl.ANY), wspec, vec, vec],
        out_specs=[uz, wspec, vec, vec],
        out_shape=[jax.ShapeDtypeStruct(dproj.shape, dproj.dtype), jax.ShapeDtypeStruct(mix_w.shape, F32),
                   jax.ShapeDtypeStruct(mix_b.shape, F32), jax.ShapeDtypeStruct(scale.shape, F32)],
        input_output_aliases={2: 0},
        compiler_params=_params(("parallel", "arbitrary")))(proj, dyb, dproj, mix_w, mix_b, scale)


def _ssd_bwd(dyan, ypre, proj, xbc, dtraw, hp, dproj, dtb, alog, dskx, gn, nb, s, di, z_col0):
    t = nb * s
    nc = s // CHUNK
    hpg = di // HEAD_DIM // SSM_GROUPS
    gw = di // SSM_GROUPS
    gn_w = SSM_GROUPS * D_STATE
    dc = di + 2 * gn_w
    b_blk = _col_block(di, gn_w)
    z_blk = _col_block(z_col0, di)
    L, P, N = CHUNK, HEAD_DIM, D_STATE
    tril, expand, expand_t = _ssd_consts(di)

    def body(dy_ref, ypre_ref, z_ref, x_ref, b_ref, c_ref, dtr_ref, hp_ref, _, dtb_ref, alog_ref, dskx_ref, gn_ref,
             tril_ref, e_ref, et_ref, dz_ref, ddt_ref, dxbc_ref, dgn_ref, ddsk_ref, dalog_ref, ddtb_ref,
             dst_ref, dyp_ref, xdt_ref, dxm_ref, t1_ref, t2_ref, t3_ref, aux_ref):
        @pl.when((pl.program_id(0) == 0) & (pl.program_id(1) == 0))
        def _():
            dgn_ref[...] = jnp.zeros_like(dgn_ref)
            ddsk_ref[...] = jnp.zeros_like(ddsk_ref)
            dalog_ref[...] = jnp.zeros_like(dalog_ref)
            ddtb_ref[...] = jnp.zeros_like(ddtb_ref)

        @pl.when(pl.program_id(1) == 0)
        def _():
            dst_ref[...] = jnp.zeros_like(dst_ref)

        tri = tril_ref[...]
        dtpre, dt, a_neg, a_cs, a_cst = _ssd_scalars(dtr_ref, dtb_ref, alog_ref, tri)
        ev = e_ref[...]
        a_exp = _expand(a_cs, ev)
        dt_exp = _expand(dt, ev)
        xv = x_ref[...]
        xdt = xv * dt_exp
        xdt_ref[...] = xdt
        a_last = a_exp[L - 1:L, :]
        dte = jnp.exp(a_last - a_exp)
        xe = xdt * dte
        ea = jnp.exp(a_exp)
        e_last = jnp.exp(a_last)
        lower = tri > 0.5
        aux_ref[...] = jnp.zeros_like(aux_ref)
        for g in range(SSM_GROUPS):
            gs = slice(g * gw, (g + 1) * gw)
            zv = z_ref[:, gs]
            yp = ypre_ref[:, gs]
            sg = _sigmoid(zv)
            sz = zv * sg
            vg = yp * sz
            r = lax.rsqrt(jnp.mean(vg * vg, axis=-1, keepdims=True) + EPS)
            vn = vg * r
            dyg = dy_ref[:, gs]
            dgn_ref[:, gs] += jnp.sum(dyg * vn, axis=0, keepdims=True)
            dvn = dyg * gn_ref[:, gs]
            dv = r * (dvn - vn * jnp.mean(dvn * vn, axis=-1, keepdims=True))
            dy = dv * sz
            dyp_ref[:, gs] = dy
            dz_ref[:, gs] = (dv * yp * sg * (1.0 + zv * (1.0 - sg))).astype(BF16)
            bg = b_ref[:, g * N:(g + 1) * N].astype(BF16)
            cg = c_ref[:, g * N:(g + 1) * N].astype(BF16)
            gm = _bdot_nt(cg, bg)
            ht = hp_ref[0, :, gs]
            dht = dst_ref[:, gs]
            bds = _bdot(bg, dht)
            dye = dy * ea[:, gs]
            xe_g = xe[:, gs]
            dcg = _bdot_nt(dye, ht)
            dbg = _bdot_nt(xe_g, dht)
            dst_ref[:, gs] = e_last[:, gs] * dht + _bdot_tn(cg, dye)
            dgm = jnp.zeros((L, L), F32)
            for e in range(hpg):
                h = g * hpg + e
                hs = slice(h * P, (h + 1) * P)
                decay = jnp.where(lower, jnp.exp(a_cs[:, h:h + 1] - a_cst[h:h + 1, :]), 0.0)
                dy_h = dyp_ref[:, hs]
                dgm = dgm + _bdot_nt(dy_h, xdt_ref[:, hs]) * decay
                dxm_ref[:, hs] = _bdot_tn(gm * decay, dy_h)
            dxbc_ref[:, di + g * N:di + (g + 1) * N] = dbg + _bdot_tn(dgm, cg)
            dxbc_ref[:, di + gn_w + g * N:di + gn_w + (g + 1) * N] = dcg + _bdot(dgm, bg)
            dxm = dxm_ref[:, gs]
            x_g = xv[:, gs]
            dskx = dskx_ref[:, gs]
            xeb = xe_g * bds
            dxdt = dxm + dte[:, gs] * bds
            dxbc_ref[:, gs] = dxdt * dt_exp[:, gs] + dy * dskx
            each = ea[:, gs] * _bdot(cg, ht)
            y_diag = yp - x_g * dskx - each
            rnd = lambda v: v.astype(BF16).astype(F32)
            t1_ref[:, gs] = rnd(dy) * y_diag + dy * each - rnd(xdt[:, gs]) * dxm - xeb
            t2_ref[:, gs] = xeb
            t3_ref[:, gs] = dxdt * x_g
            aux_ref[0:1, gs] = jnp.sum(dht * ht, axis=0, keepdims=True)
            aux_ref[1:2, gs] = jnp.sum(dy * x_g, axis=0, keepdims=True)
        etv = et_ref[...]
        w_end = _head_sum(t2_ref[...], etv)
        aux = _head_sum(aux_ref[...], etv)
        rowi = lax.broadcasted_iota(jnp.int32, (L, LANES), 0)
        end = jnp.sum(w_end, axis=0, keepdims=True) + aux[0:1, :] * jnp.exp(a_cs[L - 1:L, :])
        da = _head_sum(t1_ref[...], etv, terms=3) + jnp.where(rowi == L - 1, end, 0.0)
        rc = lax.dot_general(tri, da, (((0,), (0,)), ((), ())), precision=HIGHEST, preferred_element_type=F32)
        ddt = a_neg * rc + _head_sum(t3_ref[...], etv)
        ddtraw = ddt * _sigmoid(dtpre)
        ddt_ref[...] = ddtraw.astype(BF16)
        ddtb_ref[...] += jnp.sum(ddtraw, axis=0, keepdims=True)
        dalog_ref[...] += jnp.sum(dt * rc, axis=0, keepdims=True) * a_neg
        ddsk_ref[...] += aux[1:2, :]

    row = lambda b, c: b * nc + (nc - 1 - c)
    full = lambda w: pl.BlockSpec((L, w), lambda b, c: (row(b, c), 0))
    zspec = pl.BlockSpec((L, di), lambda b, c: (row(b, c), z_blk))
    vec = lambda w: pl.BlockSpec((1, w), lambda b, c: (0, 0))
    slab = lambda shape: pltpu.VMEM(shape, F32)
    return pl.pallas_call(
        body, name="ssd_bwd", grid=(nb, nc),
        in_specs=[full(di), full(di), zspec, full(di),
                  pl.BlockSpec((L, gn_w), lambda b, c: (row(b, c), b_blk)),
                  pl.BlockSpec((L, gn_w), lambda b, c: (row(b, c), b_blk + 1)),
                  full(LANES),
                  pl.BlockSpec((1, N, di), lambda b, c: (row(b, c), 0, 0)),
                  pl.BlockSpec(memory_space=pl.ANY),
                  vec(LANES), vec(LANES), vec(di), vec(di),
                  pl.BlockSpec((L, L), lambda b, c: (0, 0)),
                  pl.BlockSpec((LANES, di), lambda b, c: (0, 0)),
                  pl.BlockSpec((di, LANES), lambda b, c: (0, 0))],
        out_specs=[zspec, full(LANES), full(dc), vec(di), vec(LANES), vec(LANES), vec(LANES)],
        out_shape=[jax.ShapeDtypeStruct(dproj.shape, dproj.dtype), jax.ShapeDtypeStruct((t, LANES), BF16),
                   jax.ShapeDtypeStruct((t, dc), F32), jax.ShapeDtypeStruct((1, di), F32),
                   jax.ShapeDtypeStruct((1, LANES), F32), jax.ShapeDtypeStruct((1, LANES), F32),
                   jax.ShapeDtypeStruct((1, LANES), F32)],
        scratch_shapes=[slab((N, di)), slab((L, di)), slab((L, di)), slab((L, di)), slab((L, di)), slab((L, di)),
                        slab((L, di)), slab((8, di))],
        input_output_aliases={8: 0},
        compiler_params=_params(("arbitrary", "arbitrary")))(
            dyan, ypre, proj, xbc, xbc, xbc, dtraw, hp, dproj, dtb, alog, dskx, gn, tril, expand, expand_t)


def _conv_bwd(proj, dxbc, dproj, conv_w, conv_b, nb, s, col0, cb=256):
    n_blk, w_spec = _conv_w_spec(conv_w, cb, 0)
    blk0 = _col_block(col0, cb)
    rc = min(CONV_ROWS, s)

    def body(x_ref, dy_ref, _, w_ref, b_ref, dx_ref, dw_ref, db_ref, dacc_ref):
        @pl.when(pl.program_id(1) == 0)
        def _():
            dw_ref[...] = jnp.zeros_like(dw_ref)
            db_ref[...] = jnp.zeros_like(db_ref)

        wts, bias = _conv_weights(w_ref), b_ref[...]
        fold = lambda v: v.reshape(rc // 8, 8, cb).sum(axis=0)
        db8 = jnp.zeros((8, cb), F32)
        dw8 = [jnp.zeros((8, cb), F32) for _ in range(CONV_WIDTH)]
        for t0 in range(0, s, rc):
            taps = _conv_taps(x_ref, t0, rc)
            acc = _conv_pre(taps, wts, bias)
            sg = _sigmoid(acc)
            dacc = dy_ref[t0:t0 + rc, :] * sg * (1.0 + acc * (1.0 - sg))
            dacc_ref[t0:t0 + rc, :] = dacc
            db8 = db8 + fold(dacc)
            dw8 = [dw8[j] + fold(dacc * taps[j]) for j in range(CONV_WIDTH)]
        db_ref[...] += jnp.sum(db8, axis=0, keepdims=True)
        for j in range(CONV_WIDTH):
            dw_ref[0, CONV_WIDTH - 1 - j:CONV_WIDTH - j, :] += jnp.sum(dw8[j], axis=0, keepdims=True)
        for t0 in range(0, s, rc):
            if t0 + rc < s:
                n = rc + CONV_HALO
                win = dacc_ref[t0:t0 + n, :]
                ups = [win[:rc]] + [pltpu.roll(win, n - j, 0)[:rc] for j in range(1, CONV_WIDTH)]
            else:
                cur = dacc_ref[t0:t0 + rc, :]
                row = lax.broadcasted_iota(jnp.int32, cur.shape, 0)
                ups = [cur] + [_shift_up(cur, j, row) for j in range(1, CONV_WIDTH)]
            dx = ups[0] * wts[0]
            for j in range(1, CONV_WIDTH):
                dx = dx + ups[j] * wts[j]
            dx_ref[t0:t0 + rc, :] = dx.astype(BF16)

    return pl.pallas_call(
        body, name="conv_bwd", grid=(n_blk, nb),
        in_specs=[pl.BlockSpec((s, cb), lambda j, b: (b, blk0 + j)), pl.BlockSpec((s, cb), lambda j, b: (b, j)),
                  pl.BlockSpec(memory_space=pl.ANY), w_spec, pl.BlockSpec((1, cb), lambda j, b: (0, j))],
        out_specs=[pl.BlockSpec((s, cb), lambda j, b: (b, blk0 + j)), w_spec, pl.BlockSpec((1, cb), lambda j, b: (0, j))],
        out_shape=[jax.ShapeDtypeStruct(dproj.shape, dproj.dtype), jax.ShapeDtypeStruct(conv_w.shape, F32),
                   jax.ShapeDtypeStruct(conv_b.shape, F32)],
        scratch_shapes=[pltpu.VMEM((s, cb), F32)],
        input_output_aliases={2: 0},
        compiler_params=_params(("parallel", "arbitrary")))(proj, dxbc, dproj, conv_w, conv_b)


def _local_step(x, p, tgt, wg, small):
    nb, s, d = x.shape
    t = nb * s
    di = N_SHARD * wg["w_branch_a"].shape[1]
    nh = di // HEAD_DIM
    gn_w = SSM_GROUPS * D_STATE
    dc = di + 2 * gn_w
    pgd = d // N_POOL
    x2 = x.reshape(t, d)
    p2 = p.reshape(t, p.shape[-1])
    tgt2 = tgt.reshape(t, d)

    w_main, w_dt = _regroup_w_in(wg["w_in"], d, di, dc, nh)
    c_g, c_z, c_xbc, c_uz = 0, 2 * d, 2 * d + di, 2 * d + di + dc
    n_main = w_main.shape[1]

    pad_h = lambda v: jnp.pad(v.reshape(1, nh).astype(F32), ((0, 0), (0, LANES - nh)))
    dtb, alog = pad_h(small["dt_bias"]), pad_h(small["a_log"])
    dskx = jnp.repeat(small["d_skip"].reshape(1, nh).astype(F32), HEAD_DIM, axis=1)
    vec = lambda v: v.reshape(1, -1).astype(F32)
    norm_g, gn, conv_b = vec(small["norm_g"]), vec(small["gnorm_g"]), vec(small["conv_b"])
    mix_b, scale = vec(small["pool_mix_b"]), vec(small["pool_scale"])
    ple_g, final_g = vec(small["ple_norm_g"]), vec(small["final_g"])
    conv_w = wg["conv_w"]
    mix_w = wg["pool_mix_w"].reshape(N_SHARD, N_POOL, pgd // N_SHARD, pgd)
    rows = lambda v: v.reshape(-1, v.shape[-1])
    wa, wb, wo, wpg = rows(wg["w_branch_a"]), rows(wg["w_branch_b"]), rows(wg["w_out"]), rows(wg["w_ple_gate"])
    wup = wg["w_ple_up"]

    proj, dtraw, h = _inproj(x2, norm_g, w_main, w_dt)
    xbc = _conv_fwd(proj, conv_w, conv_b, nb, s, c_xbc)
    ypre, yan, hp = _ssd_fwd(proj, xbc, dtraw, dtb, alog, dskx, gn, nb, s, di, c_z)
    ybp = _pool_fwd(proj, mix_w, mix_b, scale, nb, s, c_uz)
    ya = _mm(yan, wa, "branch_a")
    yb = _mm(ybp, wb, "branch_b")
    merged = _merge_fwd(ya, yb, proj, c_g)
    mo = _mm(merged, wo, "out_proj")
    x1, hn = _ple_pre(x2, mo, ple_g)
    pre = _mm(hn, wpg, "ple_gate")
    pu = _mm(p2, wup, "ple_up")

    dx2, dpre, dpu, loss, d_final_g = _tail(x1, pre, pu, tgt2, final_g)
    d_wpg = _mm_tn(hn, dpre, "d_w_ple_gate")
    d_wup = _mm_tn(p2, dpu, "d_w_ple_up", tn=wup.shape[-1], col_blocks=True)
    dhn = _mm_nt(dpre, wpg, "d_hn")
    dx1, d_ple_g = _rms_bwd(x1, [dhn], dx2, ple_g, "ple_bwd")
    d_wo = _mm_tn(merged, dx1, "d_w_out")
    dm = _mm_nt(dx1, wo, "d_merged")
    dya, dyb, dproj = _merge_bwd(dm, ya, yb, proj, c_g, n_main)
    d_wa = _mm_tn(yan, dya, "d_w_branch_a")
    d_wb = _mm_tn(ybp, dyb, "d_w_branch_b")
    dyan = _mm_nt(dya, wa, "d_y_a")
    dybp = _mm_nt(dyb, wb, "d_y_b")
    dproj, d_mix_w, d_mix_b, d_scale = _pool_bwd(proj, dybp, dproj, mix_w, mix_b, scale, nb, s, c_uz)
    dproj, ddt, dxbc, d_gn, d_dsk, d_alog, d_dtb = _ssd_bwd(
        dyan, ypre, proj, xbc, dtraw, hp, dproj, dtb, alog, dskx, gn, nb, s, di, c_z)
    dproj, d_conv_w, d_conv_b = _conv_bwd(proj, dxbc, dproj, conv_w, conv_b, nb, s, c_xbc)
    d_wmain = _mm_tn(h, dproj, "d_w_in")
    d_wdt = _mm_tn(h, ddt, "d_w_dt")
    dh_main = _mm_nt(dproj, w_main, "d_h")
    dh_dt = _mm_nt(ddt, w_dt, "d_h_dt")
    gx, d_norm_g = _rms_bwd(x2, [dh_main, dh_dt], dx1, norm_g, "in_bwd")
    d_w_in = _ungroup_w_in(d_wmain, d_wdt, d, di, dc, nh)

    shard_major = lambda v: v.reshape(N_SHARD, v.shape[0] // N_SHARD, v.shape[1])
    grads = dict(norm_g=d_norm_g, w_in=d_w_in, conv_w=d_conv_w, conv_b=d_conv_b, dt_bias=d_dtb[:, :nh],
                 a_log=d_alog[:, :nh], d_skip=d_dsk[:, :nh], gnorm_g=d_gn,
                 pool_mix_w=d_mix_w.reshape(N_SHARD, pgd, pgd), pool_mix_b=d_mix_b, pool_scale=d_scale,
                 w_branch_a=shard_major(d_wa), w_branch_b=shard_major(d_wb), w_out=shard_major(d_wo),
                 ple_norm_g=d_ple_g, w_ple_gate=shard_major(d_wpg), w_ple_up=d_wup, final_g=d_final_g)
    return loss[0, 0], gx.reshape(nb, s, d), grads


def _place():
    return lax.axis_index("x"), lax.axis_index("y"), lax.axis_index("c")


def _other_chips(x, y):
    return [(1 - x, y), (x, 1 - y), (1 - x, 1 - y)]


def _halves(c, rows, align):
    rh = rows // 2
    assert rows % 2 == 0 and rh % align == 0, (rows, align)
    return (pl.ds(pl.multiple_of(c * rh, align), rh), pl.ds(pl.multiple_of((1 - c) * rh, align), rh))


HBM = pl.BlockSpec(memory_space=pl.ANY)


def _into_slot(w2, k, dtype, name):
    rows, cols = w2.shape
    rb = _tile(rows, 256)

    def body(k_ref, w_ref, o_ref):
        o_ref[0] = w_ref[...].astype(dtype)

    return pl.pallas_call(
        body, name=name,
        grid_spec=pltpu.PrefetchScalarGridSpec(
            num_scalar_prefetch=1, grid=(rows // rb,),
            in_specs=[pl.BlockSpec((rb, cols), lambda i, k_ref: (i, 0))],
            out_specs=pl.BlockSpec((1, rb, cols), lambda i, k_ref: (k_ref[0], i, 0))),
        out_shape=jax.ShapeDtypeStruct((N_SHARD, rows, cols), dtype),
        compiler_params=_params(("parallel",)))(k.reshape(1), w2)


def _gather_weights(split, whole):
    n_split, n_all = len(split), len(split) + len(whole)

    def body(*refs):
        bufs = refs[n_all:2 * n_all]
        send_sems, recv_sems = refs[2 * n_all:]
        x, y, c = _place()
        k = 2 * x + y
        chips = _other_chips(x, y)

        def copy(idx, block, to):
            return pltpu.make_async_remote_copy(src_ref=block, dst_ref=block, send_sem=send_sems.at[idx],
                                                recv_sem=recv_sems.at[idx], device_id=to, device_id_type=MESH)

        def block(i, shard, rows):
            return bufs[i].at[shard, rows] if i < n_split else bufs[i].at[shard]

        def sem(i, j):
            return 6 * i + j if i < n_split else 6 * n_split + 3 * (i - n_split) + j

        started = []
        for i in range(n_all):
            mine, _ = _halves(c, bufs[i].shape[1], 16) if i < n_split else (None, None)
            for j, (px, py) in enumerate(chips):
                started.append(copy(sem(i, j), block(i, k, mine), (px, py, c)))
                started[-1].start()
        for i in range(n_all):
            mine, _ = _halves(c, bufs[i].shape[1], 16) if i < n_split else (None, None)
            for j, (px, py) in enumerate(chips):
                landed = block(i, 2 * px + py, mine)
                copy(sem(i, j), landed, (px, py, c)).wait_recv()
                if i < n_split:
                    started.append(copy(sem(i, 3 + j), landed, (x, y, 1 - c)))
                    started[-1].start()
        for i in range(n_split):
            _, theirs = _halves(c, bufs[i].shape[1], 16)
            for j, (px, py) in enumerate(chips):
                copy(sem(i, 3 + j), block(i, 2 * px + py, theirs), (x, y, 1 - c)).wait_recv()
        for cp in started:
            cp.wait_send()

    arrays = list(split) + list(whole)
    n_sem = 6 * n_split + 3 * len(whole)
    return pl.pallas_call(
        body, name="gather_weights",
        in_specs=[HBM] * n_all, out_specs=[HBM] * n_all,
        out_shape=[jax.ShapeDtypeStruct(a.shape, a.dtype) for a in arrays],
        input_output_aliases={i: i for i in range(n_all)},
        scratch_shapes=[pltpu.SemaphoreType.DMA((n_sem,)), pltpu.SemaphoreType.DMA((n_sem,))],
    )(*arrays)


def _swap_halves(gs):
    n = len(gs)

    def body(*refs):
        ins, outs, send_sems, recv_sems = refs[:n], refs[n:2 * n], refs[2 * n], refs[2 * n + 1]
        x, y, c = _place()
        copies = []
        for i in range(n):
            _, theirs = _halves(c, gs[i].shape[1], 8)
            cp = pltpu.make_async_remote_copy(src_ref=ins[i].at[:, theirs], dst_ref=outs[i], send_sem=send_sems.at[i],
                                              recv_sem=recv_sems.at[i], device_id=(x, y, 1 - c), device_id_type=MESH)
            cp.start()
            copies.append(cp)
        for cp in copies:
            cp.wait()

    return pl.pallas_call(
        body, name="swap_halves", in_specs=[HBM] * n, out_specs=[HBM] * n,
        out_shape=[jax.ShapeDtypeStruct((g.shape[0], g.shape[1] // 2, g.shape[2]), g.dtype) for g in gs],
        scratch_shapes=[pltpu.SemaphoreType.DMA((n,)), pltpu.SemaphoreType.DMA((n,))],
    )(*gs)


def _scatter_chips(ps):
    n = len(ps)

    def body(*refs):
        ins, outs, send_sems, recv_sems = refs[:n], refs[n:2 * n], refs[2 * n], refs[2 * n + 1]
        x, y, c = _place()
        copies = []
        for i in range(n):
            for j, (px, py) in enumerate(_other_chips(x, y)):
                cp = pltpu.make_async_remote_copy(src_ref=ins[i].at[2 * px + py], dst_ref=outs[i].at[j],
                                                  send_sem=send_sems.at[3 * i + j], recv_sem=recv_sems.at[3 * i + j],
                                                  device_id=(px, py, c), device_id_type=MESH)
                cp.start()
                copies.append(cp)
        for cp in copies:
            cp.wait()

    return pl.pallas_call(
        body, name="scatter_chips", in_specs=[HBM] * n, out_specs=[HBM] * n,
        out_shape=[jax.ShapeDtypeStruct((N_SHARD - 1,) + v.shape[1:], v.dtype) for v in ps],
        scratch_shapes=[pltpu.SemaphoreType.DMA((3 * n,)), pltpu.SemaphoreType.DMA((3 * n,))],
    )(*ps)


def _join_halves(vs):
    n = len(vs)

    def body(*refs):
        bufs, send_sems, recv_sems = refs[n:2 * n], refs[2 * n], refs[2 * n + 1]
        x, y, c = _place()

        def copy(i, rows):
            return pltpu.make_async_remote_copy(src_ref=bufs[i].at[rows], dst_ref=bufs[i].at[rows],
                                                send_sem=send_sems.at[i], recv_sem=recv_sems.at[i],
                                                device_id=(x, y, 1 - c), device_id_type=MESH)

        halves = [_halves(c, bufs[i].shape[0], 8) for i in range(n)]
        sends = [copy(i, halves[i][0]) for i in range(n)]
        for cp in sends:
            cp.start()
        for i in range(n):
            copy(i, halves[i][1]).wait_recv()
        for cp in sends:
            cp.wait_send()

    return pl.pallas_call(
        body, name="join_halves", in_specs=[HBM] * n, out_specs=[HBM] * n,
        out_shape=[jax.ShapeDtypeStruct(v.shape, v.dtype) for v in vs],
        input_output_aliases={i: i for i in range(n)},
        scratch_shapes=[pltpu.SemaphoreType.DMA((n,)), pltpu.SemaphoreType.DMA((n,))],
    )(*vs)


def _allreduce_small(v):
    rows = v.shape[0]

    def body(v_ref, o_ref, buf_ref, send_sems, recv_sems):
        x, y, c = _place()
        me = 4 * x + 2 * y + c
        buf_ref[me] = v_ref[...]
        copies = []
        for rel in range(1, 8):
            peer = (x ^ (rel >> 2), y ^ ((rel >> 1) & 1), c ^ (rel & 1))
            cp = pltpu.make_async_remote_copy(src_ref=v_ref, dst_ref=buf_ref.at[me], send_sem=send_sems.at[rel - 1],
                                              recv_sem=recv_sems.at[rel - 1], device_id=peer, device_id_type=MESH)
            cp.start()
            copies.append(cp)
        for rel in range(1, 8):
            peer_id = me ^ rel
            pltpu.make_async_remote_copy(src_ref=v_ref, dst_ref=buf_ref.at[peer_id], send_sem=send_sems.at[rel - 1],
                                         recv_sem=recv_sems.at[rel - 1], device_id=(x, y, c),
                                         device_id_type=MESH).wait_recv()
        for cp in copies:
            cp.wait_send()
        acc = buf_ref[0]
        for i in range(1, 8):
            acc = acc + buf_ref[i]
        o_ref[...] = acc

    return pl.pallas_call(
        body, name="allreduce_small",
        in_specs=[pl.BlockSpec(memory_space=pltpu.VMEM)], out_specs=pl.BlockSpec(memory_space=pltpu.VMEM),
        out_shape=jax.ShapeDtypeStruct(v.shape, F32),
        scratch_shapes=[pltpu.VMEM((8, rows, LANES), F32), pltpu.SemaphoreType.DMA((7,)), pltpu.SemaphoreType.DMA((7,))],
    )(v)


def _add_pair(g, got, c, name):
    _, rh, cols = got.shape
    rb = _tile(rh, 256)
    nrb = rh // rb

    def body(c_ref, g_ref, got_ref, o_ref):
        o_ref[...] = (g_ref[...] + got_ref[...]).astype(BF16)

    spec = pl.BlockSpec((1, rb, cols), lambda j, i, c_ref: (j, i, 0))
    return pl.pallas_call(
        body, name=name,
        grid_spec=pltpu.PrefetchScalarGridSpec(
            num_scalar_prefetch=1, grid=(N_SHARD, nrb),
            in_specs=[pl.BlockSpec((1, rb, cols), lambda j, i, c_ref: (j, c_ref[0] * nrb + i, 0)), spec],
            out_specs=spec),
        out_shape=jax.ShapeDtypeStruct(got.shape, BF16),
        compiler_params=_params(("parallel", "parallel")))(c.reshape(1), g, got)


def _add_chips(g, got, landed, k, c, name):
    _, rh, cols = got.shape
    rb = _tile(rh, 256)
    nrb = rh // rb

    def body(kc_ref, g_ref, got_ref, l_ref, o_ref):
        own = g_ref[0] + got_ref[0]
        o_ref[...] = own + l_ref[0].astype(F32) + l_ref[1].astype(F32) + l_ref[2].astype(F32)

    half_c = lambda i, kc: (kc[1] * nrb + i, 0)
    return pl.pallas_call(
        body, name=name,
        grid_spec=pltpu.PrefetchScalarGridSpec(
            num_scalar_prefetch=1, grid=(nrb,),
            in_specs=[pl.BlockSpec((1, rb, cols), lambda i, kc: (kc[0],) + half_c(i, kc)),
                      pl.BlockSpec((1, rb, cols), lambda i, kc: (kc[0], i, 0)),
                      pl.BlockSpec((N_SHARD - 1, rb, cols), lambda i, kc: (0, i, 0))],
            out_specs=pl.BlockSpec((rb, cols), half_c)),
        out_shape=jax.ShapeDtypeStruct((2 * rh, cols), F32),
        compiler_params=_params(("parallel",)))(jnp.stack([k, c]), g, got, landed)


def _adamw(wv, g, m, v, name):
    rows, cols = wv.shape
    rb = _tile(rows, 256)
    c1 = 1.0 - ADAM_B1 ** ADAM_STEP
    c2 = 1.0 - ADAM_B2 ** ADAM_STEP

    def body(w_ref, g_ref, m_ref, v_ref, go_ref, d_ref, nm_ref, nv_ref):
        gv = g_ref[...]
        go_ref[...] = gv
        nm = ADAM_B1 * m_ref[...] + (1.0 - ADAM_B1) * gv
        nv = ADAM_B2 * v_ref[...] + (1.0 - ADAM_B2) * (gv * gv)
        nm_ref[...] = nm
        nv_ref[...] = nv
        d_ref[...] = -ADAM_LR * ((nm / c1) / (jnp.sqrt(nv / c2) + ADAM_EPS) + ADAM_WD * w_ref[...])

    spec = pl.BlockSpec((rb, cols), lambda i: (i, 0))
    return pl.pallas_call(
        body, name=name, grid=(rows // rb,), in_specs=[spec] * 4, out_specs=[spec] * 4,
        out_shape=[jax.ShapeDtypeStruct((rows, cols), F32)] * 4,
        compiler_params=_params(("parallel",)))(wv, g, m, v)


def _pack(flats):
    cat = jnp.concatenate([f.reshape(-1) for f in flats])
    n = cat.shape[0]
    rows = -(-n // (8 * LANES)) * 8
    return jnp.pad(cat, (0, rows * LANES - n)).reshape(rows, LANES)


def _unpack(packed, shapes):
    flat = packed.reshape(-1)
    out, off = [], 0
    for shp in shapes:
        n = 1
        for dim in shp:
            n *= dim
        out.append(flat[off:off + n].reshape(shp))
        off += n
    return out


def kernel(x, p, norm_g, w_in, conv_w, conv_b, dt_bias, a_log, d_skip, gnorm_g, pool_mix_w, pool_mix_b, pool_scale, w_branch_a, w_branch_b, w_out, ple_norm_g, w_ple_gate, w_ple_up, final_g, loss_target, m_norm_g, m_w_in, m_conv_w, m_conv_b, m_dt_bias, m_a_log, m_d_skip, m_gnorm_g, m_pool_mix_w, m_pool_mix_b, m_pool_scale, m_w_branch_a, m_w_branch_b, m_w_out, m_ple_norm_g, m_w_ple_gate, m_w_ple_up, m_final_g, v_norm_g, v_w_in, v_conv_w, v_conv_b, v_dt_bias, v_a_log, v_d_skip, v_gnorm_g, v_pool_mix_w, v_pool_mix_b, v_pool_scale, v_w_branch_a, v_w_branch_b, v_w_out, v_ple_norm_g, v_w_ple_gate, v_w_ple_up, v_final_g):
    wts = dict(norm_g=norm_g, w_in=w_in, conv_w=conv_w, conv_b=conv_b, dt_bias=dt_bias, a_log=a_log, d_skip=d_skip,
               gnorm_g=gnorm_g, pool_mix_w=pool_mix_w, pool_mix_b=pool_mix_b, pool_scale=pool_scale,
               w_branch_a=w_branch_a, w_branch_b=w_branch_b, w_out=w_out, ple_norm_g=ple_norm_g,
               w_ple_gate=w_ple_gate, w_ple_up=w_ple_up, final_g=final_g)
    mom_m = dict(norm_g=m_norm_g, w_in=m_w_in, conv_w=m_conv_w, conv_b=m_conv_b, dt_bias=m_dt_bias, a_log=m_a_log,
                 d_skip=m_d_skip, gnorm_g=m_gnorm_g, pool_mix_w=m_pool_mix_w, pool_mix_b=m_pool_mix_b,
                 pool_scale=m_pool_scale, w_branch_a=m_w_branch_a, w_branch_b=m_w_branch_b, w_out=m_w_out,
                 ple_norm_g=m_ple_norm_g, w_ple_gate=m_w_ple_gate, w_ple_up=m_w_ple_up, final_g=m_final_g)
    mom_v = dict(norm_g=v_norm_g, w_in=v_w_in, conv_w=v_conv_w, conv_b=v_conv_b, dt_bias=v_dt_bias, a_log=v_a_log,
                 d_skip=v_d_skip, gnorm_g=v_gnorm_g, pool_mix_w=v_pool_mix_w, pool_mix_b=v_pool_mix_b,
                 pool_scale=v_pool_scale, w_branch_a=v_w_branch_a, w_branch_b=v_w_branch_b, w_out=v_w_out,
                 ple_norm_g=v_ple_norm_g, w_ple_gate=v_w_ple_gate, w_ple_up=v_w_ple_up, final_g=v_final_g)
    c = lax.axis_index("c")
    k = 2 * lax.axis_index("x") + lax.axis_index("y")
    flat2 = lambda a: a.reshape(-1, a.shape[-1])

    gathered = _gather_weights([_into_slot(flat2(wts[n]), k, BF16, "slot_" + n) for n in BIG],
                               [_into_slot(flat2(conv_w), k, F32, "slot_conv_w")])
    wg = dict(zip(BIG + ("conv_w",), gathered))

    loss, grad_x, grads = _local_step(x, p[0], loss_target, wg, {n: wts[n] for n in SMALL})

    g_big = [grads[n] for n in BIG]
    got = _swap_halves(g_big)
    pairs = [_add_pair(g, o, c, "add_pair_" + n) for n, g, o in zip(BIG, g_big, got)]
    landed = _scatter_chips(pairs)
    half_sums = [_add_chips(g, o, ld, k, c, "add_chips_" + n) for n, g, o, ld in zip(BIG, g_big, got, landed)]
    g_shards = dict(zip(BIG, _join_halves(half_sums)))

    conv_shape = flat2(conv_w).shape
    small_sum = _allreduce_small(_pack([grads[n] for n in SMALL] + [grads["conv_w"], loss]))
    small_shapes = [wts[n].shape for n in SMALL] + [(N_SHARD,) + conv_shape, (1,)]
    small_g = _unpack(small_sum, small_shapes)
    g_conv = lax.dynamic_index_in_dim(small_g[-2], k, axis=0, keepdims=False)

    outs = {}
    for n in BIG:
        vals = _adamw(flat2(wts[n]), g_shards[n], flat2(mom_m[n]), flat2(mom_v[n]), "adamw_" + n)
        for kind, val in zip(("grad", "delta", "new_m", "new_v"), vals):
            outs[kind, n] = val.reshape(wts[n].shape)
    names = SMALL + ("conv_w",)
    sm = _adamw(_pack([wts[n] for n in names]), _pack(small_g[:len(SMALL)] + [g_conv]),
                _pack([mom_m[n] for n in names]), _pack([mom_v[n] for n in names]), "adamw_small")
    sm_shapes = [wts[n].shape for n in names]
    for kind, val in zip(("grad", "delta", "new_m", "new_v"), sm):
        for n, piece in zip(names, _unpack(val, sm_shapes)):
            outs[kind, n] = piece
    return (small_g[-1][0], grad_x, *[outs[kind, n] for kind in ("grad", "delta", "new_m", "new_v") for n in WEIGHTS])
```

```python
import functools

import jax
import jax.numpy as jnp
from jax import lax
from jax.experimental import pallas as pl
from jax.experimental.pallas import tpu as pltpu

F32 = jnp.float32
BF16 = jnp.bfloat16
HIGHEST = lax.Precision.HIGHEST
MESH = pl.DeviceIdType.MESH

EPS = 1e-6
HEAD_DIM = 64
SSM_GROUPS = 4
D_STATE = 128
CONV_WIDTH = 4
CHUNK = 128
N_POOL = 4
LANES = 128
N_SHARD = 4

ADAM_LR = 0.001
ADAM_B1 = 0.9
ADAM_B2 = 0.999
ADAM_EPS = 1e-08
ADAM_WD = 0.01
ADAM_STEP = 10

BIG = ("w_in", "pool_mix_w", "w_branch_a", "w_branch_b", "w_out", "w_ple_gate", "w_ple_up")
REST = BIG[1:]
SMALL = ("norm_g", "conv_b", "dt_bias", "a_log", "d_skip", "gnorm_g", "pool_mix_b", "pool_scale",
         "ple_norm_g", "final_g")
WEIGHTS = ("norm_g", "w_in", "conv_w", "conv_b", "dt_bias", "a_log", "d_skip", "gnorm_g", "pool_mix_w",
           "pool_mix_b", "pool_scale", "w_branch_a", "w_branch_b", "w_out", "ple_norm_g", "w_ple_gate",
           "w_ple_up", "final_g")


def _params(sem=None, vmem_mb=56):
    kw = dict(vmem_limit_bytes=vmem_mb << 20)
    if sem is not None:
        kw["dimension_semantics"] = sem
    return pltpu.CompilerParams(**kw)


def _sigmoid(v):
    return 1.0 / (1.0 + jnp.exp(-v))


def _softplus(v):
    return jnp.maximum(v, 0.0) + jnp.log1p(jnp.exp(-jnp.abs(v)))


def _bdot(a, b):
    return jnp.dot(a.astype(BF16), b.astype(BF16), preferred_element_type=F32)


def _bdot_nt(a, b):
    return lax.dot_general(a.astype(BF16), b.astype(BF16), (((1,), (1,)), ((), ())), preferred_element_type=F32)


def _bdot_tn(a, b):
    return lax.dot_general(a.astype(BF16), b.astype(BF16), (((0,), (0,)), ((), ())), preferred_element_type=F32)


def _col_block(col0, width):
    assert col0 % width == 0, (col0, width)
    return col0 // width


def _tile(n, cap):
    if n <= cap:
        return n
    best = None
    for cand in range(8, cap + 1, 8):
        if n % cand == 0:
            best = cand
    assert best is not None, (n, cap)
    return best


def _shift_down(v, j, row):
    return jnp.where(row >= j, pltpu.roll(v, j, 0), 0.0)


def _shift_up(v, j, row):
    n = v.shape[0]
    return jnp.where(row < n - j, pltpu.roll(v, n - j, 0), 0.0)


def _mm(a, w, name, tm=1024, tn=1024):
    t, k = a.shape
    tm = min(tm, t)
    blocked = w.ndim == 3
    if blocked:
        nblk, _, tn = w.shape
        n = nblk * tn
        w_spec = pl.BlockSpec((1, k, tn), lambda i, j: (j, 0, 0))
    else:
        n = w.shape[1]
        tn = min(tn, n)
        w_spec = pl.BlockSpec((k, tn), lambda i, j: (0, j))

    def body(a_ref, w_ref, o_ref):
        wv = w_ref[0] if blocked else w_ref[...]
        o_ref[...] = _bdot(a_ref[...], wv)

    return pl.pallas_call(
        body, name=name, grid=(t // tm, n // tn),
        in_specs=[pl.BlockSpec((tm, k), lambda i, j: (i, 0)), w_spec],
        out_specs=pl.BlockSpec((tm, tn), lambda i, j: (i, j)),
        out_shape=jax.ShapeDtypeStruct((t, n), F32),
        compiler_params=_params(("parallel", "parallel")))(a, w)


def _mm_nt(a, w, name, tm=1024, tk=1024):
    t, k = a.shape
    n = w.shape[0]
    tm, tk = min(tm, t), min(tk, k)

    def body(a_ref, w_ref, o_ref):
        kk = pl.program_id(1)
        part = _bdot_nt(a_ref[...], w_ref[...])

        @pl.when(kk == 0)
        def _():
            o_ref[...] = part

        @pl.when(kk > 0)
        def _():
            o_ref[...] += part

    return pl.pallas_call(
        body, name=name, grid=(t // tm, k // tk),
        in_specs=[pl.BlockSpec((tm, tk), lambda i, j: (i, j)), pl.BlockSpec((n, tk), lambda i, j: (0, j))],
        out_specs=pl.BlockSpec((tm, n), lambda i, j: (i, 0)),
        out_shape=jax.ShapeDtypeStruct((t, n), F32),
        compiler_params=_params(("parallel", "arbitrary")))(a, w)


def _mm_tn(a, b, name, tn=1024, tk=1024, col_blocks=False):
    t, m = a.shape
    n = b.shape[1]
    tn, tk = min(tn, n), min(tk, t)

    def body(a_ref, b_ref, o_ref):
        kk = pl.program_id(1)
        part = _bdot_tn(a_ref[...], b_ref[...])
        part = part[None] if col_blocks else part

        @pl.when(kk == 0)
        def _():
            o_ref[...] = part

        @pl.when(kk > 0)
        def _():
            o_ref[...] += part

    if col_blocks:
        out_spec = pl.BlockSpec((1, m, tn), lambda j, kk: (j, 0, 0))
        out_shape = jax.ShapeDtypeStruct((n // tn, m, tn), F32)
    else:
        out_spec = pl.BlockSpec((m, tn), lambda j, kk: (0, j))
        out_shape = jax.ShapeDtypeStruct((m, n), F32)
    return pl.pallas_call(
        body, name=name, grid=(n // tn, t // tk),
        in_specs=[pl.BlockSpec((tk, m), lambda j, kk: (kk, 0)), pl.BlockSpec((tk, tn), lambda j, kk: (kk, j))],
        out_specs=out_spec, out_shape=out_shape,
        compiler_params=_params(("parallel", "arbitrary")))(a, b)


def _w_in_pieces(d, di, dc, nh, shard_w):
    pgd = d // N_POOL
    o_dt, o_u = di + dc, di + dc + nh
    o_zp, o_ga, o_gb = o_u + d, o_u + 2 * d, o_u + 3 * d
    c_z, c_uz = 2 * d, 2 * d + di + dc
    runs = [(False, 0, o_ga, d), (False, d, o_gb, d), (False, c_z, 0, di + dc), (True, 0, o_dt, nh)]
    for g in range(N_POOL):
        runs.append((False, c_uz + 2 * g * pgd, o_u + g * pgd, pgd))
        runs.append((False, c_uz + (2 * g + 1) * pgd, o_zp + g * pgd, pgd))
    pieces = []
    for is_dt, dst, src, n in runs:
        while n > 0:
            k, off = divmod(src, shard_w)
            m = min(n, shard_w - off)
            pieces.append((is_dt, dst, k, off, m))
            dst, src, n = dst + m, src + m, n - m
    return pieces


def _regroup_w_in(w_sh, d, di, dc, nh, rb=256):
    _, rows, sw = w_sh.shape
    n_main = 4 * d + di + dc
    pieces = _w_in_pieces(d, di, dc, nh, sw)
    rb = min(rb, rows)

    def body(w_ref, main_ref, dt_ref):
        dt_ref[...] = jnp.zeros_like(dt_ref)
        for is_dt, dst, k, off, m in pieces:
            out = dt_ref if is_dt else main_ref
            out[:, dst:dst + m] = w_ref[k, :, off:off + m]

    return pl.pallas_call(
        body, name="regroup_w_in", grid=(rows // rb,),
        in_specs=[pl.BlockSpec((N_SHARD, rb, sw), lambda i: (0, i, 0))],
        out_specs=[pl.BlockSpec((rb, n_main), lambda i: (i, 0)), pl.BlockSpec((rb, LANES), lambda i: (i, 0))],
        out_shape=[jax.ShapeDtypeStruct((rows, n_main), w_sh.dtype), jax.ShapeDtypeStruct((rows, LANES), w_sh.dtype)],
        compiler_params=_params(("parallel",)))(w_sh)


def _ungroup_w_in(d_main, d_dt, d, di, dc, nh, rb=128):
    rows, n_main = d_main.shape
    sw = (n_main + nh) // N_SHARD
    pieces = _w_in_pieces(d, di, dc, nh, sw)
    rb = min(rb, rows)

    def body(main_ref, dt_ref, o_ref):
        for is_dt, dst, k, off, m in pieces:
            src = dt_ref if is_dt else main_ref
            o_ref[k, :, off:off + m] = src[:, dst:dst + m]

    return pl.pallas_call(
        body, name="ungroup_w_in", grid=(rows // rb,),
        in_specs=[pl.BlockSpec((rb, n_main), lambda i: (i, 0)), pl.BlockSpec((rb, LANES), lambda i: (i, 0))],
        out_specs=pl.BlockSpec((N_SHARD, rb, sw), lambda i: (0, i, 0)),
        out_shape=jax.ShapeDtypeStruct((N_SHARD, rows, sw), F32),
        compiler_params=_params(("parallel",)))(d_main, d_dt)


def _inproj(x2, norm_g, w_main, w_dt, tm=1024, tn=1024):
    t, d = x2.shape
    n = w_main.shape[1]
    tm, tn = min(tm, t), min(tn, n)

    def body(x_ref, g_ref, w_ref, wdt_ref, proj_ref, dt_ref, h_ref):
        @pl.when(pl.program_id(1) == 0)
        def _():
            xv = x_ref[...]
            r = lax.rsqrt(jnp.mean(xv * xv, axis=-1, keepdims=True) + EPS)
            h = (xv * r * g_ref[...]).astype(BF16)
            h_ref[...] = h
            dt_ref[...] = jnp.dot(h, wdt_ref[...].astype(BF16), preferred_element_type=F32)

        proj_ref[...] = jnp.dot(h_ref[...], w_ref[...].astype(BF16), preferred_element_type=F32)

    return pl.pallas_call(
        body, name="inproj", grid=(t // tm, n // tn),
        in_specs=[pl.BlockSpec((tm, d), lambda i, j: (i, 0)), pl.BlockSpec((1, d), lambda i, j: (0, 0)),
                  pl.BlockSpec((d, tn), lambda i, j: (0, j)), pl.BlockSpec((d, LANES), lambda i, j: (0, 0))],
        out_specs=[pl.BlockSpec((tm, tn), lambda i, j: (i, j)), pl.BlockSpec((tm, LANES), lambda i, j: (i, 0)),
                   pl.BlockSpec((tm, d), lambda i, j: (i, 0))],
        out_shape=[jax.ShapeDtypeStruct((t, n), F32), jax.ShapeDtypeStruct((t, LANES), F32),
                   jax.ShapeDtypeStruct((t, d), BF16)],
        compiler_params=_params(("parallel", "arbitrary")))(x2, norm_g, w_main, w_dt)


def _conv_w_spec(conv_w, cb, j_axis):
    sw = conv_w.shape[2]
    assert sw % cb == 0, (sw, cb)
    per = sw // cb
    return N_SHARD * per, pl.BlockSpec((1, CONV_WIDTH, cb), lambda *ij: (ij[j_axis] // per, 0, ij[j_axis] % per))


CONV_ROWS = 64
CONV_HALO = 8


def _conv_taps(x_ref, t0, rc):
    if t0 == 0:
        cur = x_ref[0:rc, :]
        row = lax.broadcasted_iota(jnp.int32, cur.shape, 0)
        return [cur] + [_shift_down(cur, j, row) for j in range(1, CONV_WIDTH)]
    ext = x_ref[t0 - CONV_HALO:t0 + rc, :]
    return [ext[CONV_HALO:]] + [pltpu.roll(ext, j, 0)[CONV_HALO:] for j in range(1, CONV_WIDTH)]


def _conv_weights(w_ref):
    return [w_ref[0, CONV_WIDTH - 1 - j:CONV_WIDTH - j, :] for j in range(CONV_WIDTH)]


def _conv_pre(taps, wts, bias):
    acc = bias + taps[0] * wts[0]
    for j in range(1, CONV_WIDTH):
        acc = acc + taps[j] * wts[j]
    return acc


def _conv_fwd(proj, conv_w, conv_b, nb, s, col0, cb=256):
    n_blk, w_spec = _conv_w_spec(conv_w, cb, 1)
    blk0 = _col_block(col0, cb)
    rc = min(CONV_ROWS, s)

    def body(x_ref, w_ref, b_ref, o_ref):
        wts, bias = _conv_weights(w_ref), b_ref[...]
        for t0 in range(0, s, rc):
            acc = _conv_pre(_conv_taps(x_ref, t0, rc), wts, bias)
            o_ref[t0:t0 + rc, :] = acc * _sigmoid(acc)

    return pl.pallas_call(
        body, name="conv_fwd", grid=(nb, n_blk),
        in_specs=[pl.BlockSpec((s, cb), lambda b, j: (b, blk0 + j)), w_spec, pl.BlockSpec((1, cb), lambda b, j: (0, j))],
        out_specs=pl.BlockSpec((s, cb), lambda b, j: (b, j)),
        out_shape=jax.ShapeDtypeStruct((nb * s, n_blk * cb), F32),
        compiler_params=_params(("parallel", "parallel")))(proj, conv_w, conv_b)


def _ssd_consts(di):
    r = lax.broadcasted_iota(jnp.int32, (CHUNK, CHUNK), 0)
    c = lax.broadcasted_iota(jnp.int32, (CHUNK, CHUNK), 1)
    tril = (r >= c).astype(F32)
    head = lax.broadcasted_iota(jnp.int32, (LANES, di), 0)
    chan = lax.broadcasted_iota(jnp.int32, (LANES, di), 1) // HEAD_DIM
    expand = (head == chan).astype(BF16)
    return tril, expand, expand.T


def _expand(v, e, terms=3):
    acc = None
    for _ in range(terms):
        vb = v.astype(BF16)
        part = jnp.dot(vb, e, preferred_element_type=F32)
        acc = part if acc is None else acc + part
        v = v - vb.astype(F32)
    return acc


def _head_sum(t, et, terms=2):
    acc = None
    for _ in range(terms):
        tb = t.astype(BF16)
        part = jnp.dot(tb, et, preferred_element_type=F32)
        acc = part if acc is None else acc + part
        t = t - tb.astype(F32)
    return acc


def _ssd_scalars(dtr_ref, dtb_ref, alog_ref, tri):
    dtpre = dtr_ref[...] + dtb_ref[...]
    dt = _softplus(dtpre)
    a_neg = -jnp.exp(alog_ref[...])
    a_dt = dt * a_neg
    a_cs = jnp.dot(tri, a_dt, precision=HIGHEST, preferred_element_type=F32)
    a_cst = lax.dot_general(a_dt, tri, (((0,), (1,)), ((), ())), precision=HIGHEST, preferred_element_type=F32)
    return dtpre, dt, a_neg, a_cs, a_cst


def _ssd_fwd(proj, xbc, dtraw, dtb, alog, dskx, gn, nb, s, di, z_col0):
    t = nb * s
    nc = s // CHUNK
    hpg = di // HEAD_DIM // SSM_GROUPS
    gw = di // SSM_GROUPS
    gn_w = SSM_GROUPS * D_STATE
    b_blk = _col_block(di, gn_w)
    z_blk = _col_block(z_col0, di)
    L, P, N = CHUNK, HEAD_DIM, D_STATE
    tril, expand, _ = _ssd_consts(di)

    def body(z_ref, x_ref, b_ref, c_ref, dtr_ref, dtb_ref, alog_ref, dskx_ref, gn_ref, tril_ref, e_ref,
             ypre_ref, yan_ref, hp_ref, st_ref, yd_ref, xdt_ref):
        @pl.when(pl.program_id(1) == 0)
        def _():
            st_ref[...] = jnp.zeros_like(st_ref)

        hp_ref[0] = st_ref[...]
        tri = tril_ref[...]
        _, dt, _, a_cs, a_cst = _ssd_scalars(dtr_ref, dtb_ref, alog_ref, tri)
        ev = e_ref[...]
        a_exp = _expand(a_cs, ev)
        xv = x_ref[...]
        xdt = xv * _expand(dt, ev)
        xdt_ref[...] = xdt
        a_last = a_exp[L - 1:L, :]
        xe = xdt * jnp.exp(a_last - a_exp)
        ea = jnp.exp(a_exp)
        e_last = jnp.exp(a_last)
        lower = tri > 0.5
        for g in range(SSM_GROUPS):
            gs = slice(g * gw, (g + 1) * gw)
            bg = b_ref[:, g * N:(g + 1) * N].astype(BF16)
            cg = c_ref[:, g * N:(g + 1) * N].astype(BF16)
            gm = _bdot_nt(cg, bg)
            ht = st_ref[:, gs]
            ch = _bdot(cg, ht)
            for e in range(hpg):
                h = g * hpg + e
                hs = slice(h * P, (h + 1) * P)
                decay = jnp.where(lower, jnp.exp(a_cs[:, h:h + 1] - a_cst[h:h + 1, :]), 0.0)
                yd_ref[:, hs] = _bdot(gm * decay, xdt_ref[:, hs])
            st_ref[:, gs] = ht * e_last[:, gs] + _bdot_tn(bg, xe[:, gs])
            ypre = yd_ref[:, gs] + ea[:, gs] * ch + xv[:, gs] * dskx_ref[:, gs]
            ypre_ref[:, gs] = ypre
            zv = z_ref[:, gs]
            v = ypre * zv * _sigmoid(zv)
            r = lax.rsqrt(jnp.mean(v * v, axis=-1, keepdims=True) + EPS)
            yan_ref[:, gs] = (v * r * gn_ref[:, gs]).astype(BF16)

    row = lambda b, c: b * nc + c
    vec = lambda w: pl.BlockSpec((1, w), lambda b, c: (0, 0))
    return pl.pallas_call(
        body, name="ssd_fwd", grid=(nb, nc),
        in_specs=[pl.BlockSpec((L, di), lambda b, c: (row(b, c), z_blk)),
                  pl.BlockSpec((L, di), lambda b, c: (row(b, c), 0)),
                  pl.BlockSpec((L, gn_w), lambda b, c: (row(b, c), b_blk)),
                  pl.BlockSpec((L, gn_w), lambda b, c: (row(b, c), b_blk + 1)),
                  pl.BlockSpec((L, LANES), lambda b, c: (row(b, c), 0)),
                  vec(LANES), vec(LANES), vec(di), vec(di),
                  pl.BlockSpec((L, L), lambda b, c: (0, 0)),
                  pl.BlockSpec((LANES, di), lambda b, c: (0, 0))],
        out_specs=[pl.BlockSpec((L, di), lambda b, c: (row(b, c), 0)),
                   pl.BlockSpec((L, di), lambda b, c: (row(b, c), 0)),
                   pl.BlockSpec((1, N, di), lambda b, c: (row(b, c), 0, 0))],
        out_shape=[jax.ShapeDtypeStruct((t, di), F32), jax.ShapeDtypeStruct((t, di), BF16),
                   jax.ShapeDtypeStruct((nb * nc, N, di), F32)],
        scratch_shapes=[pltpu.VMEM((N, di), F32), pltpu.VMEM((L, di), F32), pltpu.VMEM((L, di), F32)],
        compiler_params=_params(("parallel", "arbitrary")))(
            proj, xbc, xbc, xbc, dtraw, dtb, alog, dskx, gn, tril, expand)


def _pool_sum(v, g, row, shift):
    s2 = v + shift(v, 1, row)
    s4 = s2 + shift(s2, 2, row)
    s8 = s4 + shift(s4, 4, row)
    s16 = s8 + shift(s8, 8, row)
    return jnp.where(g == 0, s2, jnp.where(g == 1, s4, jnp.where(g == 2, s8, s16)))


def _pool_count(g, row):
    return jnp.minimum(row + 1, jnp.left_shift(2, g)).astype(F32)


def _pool_fwd(proj, mix_w, mix_b, scale, nb, s, col0):
    pgd = mix_w.shape[-1]
    blk0 = _col_block(col0, 2 * pgd)

    def body(uz_ref, w_ref, b_ref, sc_ref, o_ref):
        g = pl.program_id(1)
        u = uz_ref[:, :pgd]
        zp = uz_ref[:, pgd:]
        row = lax.broadcasted_iota(jnp.int32, u.shape, 0)
        pooled = _pool_sum(u, g, row, _shift_down) / _pool_count(g, row) - u
        mixed = _bdot(pooled, w_ref[:, 0].reshape(pgd, pgd)) + b_ref[...]
        o_ref[...] = (mixed * sc_ref[...] * zp * _sigmoid(zp)).astype(BF16)

    return pl.pallas_call(
        body, name="pool_fwd", grid=(nb, N_POOL),
        in_specs=[pl.BlockSpec((s, 2 * pgd), lambda b, g: (b, blk0 + g)),
                  pl.BlockSpec((N_SHARD, 1, pgd // N_SHARD, pgd), lambda b, g: (0, g, 0, 0)),
                  pl.BlockSpec((1, pgd), lambda b, g: (0, g)), pl.BlockSpec((1, pgd), lambda b, g: (0, g))],
        out_specs=pl.BlockSpec((s, pgd), lambda b, g: (b, g)),
        out_shape=jax.ShapeDtypeStruct((nb * s, N_POOL * pgd), BF16),
        compiler_params=_params(("parallel", "parallel")))(proj, mix_w, mix_b, scale)


def _merge_fwd(ya, yb, proj, col0, tm=512):
    t, d = ya.shape
    tm = min(tm, t)
    blk = _col_block(col0, 2 * d)

    def body(ya_ref, yb_ref, g_ref, o_ref):
        o_ref[...] = (_sigmoid(g_ref[:, :d]) * ya_ref[...] + _sigmoid(g_ref[:, d:]) * yb_ref[...]).astype(BF16)

    row = pl.BlockSpec((tm, d), lambda i: (i, 0))
    return pl.pallas_call(
        body, name="merge_fwd", grid=(t // tm,),
        in_specs=[row, row, pl.BlockSpec((tm, 2 * d), lambda i: (i, blk))],
        out_specs=row, out_shape=jax.ShapeDtypeStruct((t, d), BF16),
        compiler_params=_params(("parallel",)))(ya, yb, proj)


def _ple_pre(x2, mo, ple_g, tm=512):
    t, d = x2.shape
    tm = min(tm, t)

    def body(x_ref, mo_ref, g_ref, x1_ref, hn_ref):
        x1 = x_ref[...] + mo_ref[...]
        x1_ref[...] = x1
        r = lax.rsqrt(jnp.mean(x1 * x1, axis=-1, keepdims=True) + EPS)
        hn_ref[...] = (x1 * r * g_ref[...]).astype(BF16)

    row = pl.BlockSpec((tm, d), lambda i: (i, 0))
    return pl.pallas_call(
        body, name="ple_pre", grid=(t // tm,),
        in_specs=[row, row, pl.BlockSpec((1, d), lambda i: (0, 0))],
        out_specs=[row, row],
        out_shape=[jax.ShapeDtypeStruct((t, d), F32), jax.ShapeDtypeStruct((t, d), BF16)],
        compiler_params=_params(("parallel",)))(x2, mo, ple_g)


def _tail(x1, pre, pu, tgt, final_g, tm=512):
    t, d = x1.shape
    tm = min(tm, t)

    def body(x1_ref, pre_ref, pu_ref, tgt_ref, g_ref, dx2_ref, dpre_ref, dpu_ref, loss_ref, dg_ref):
        @pl.when(pl.program_id(0) == 0)
        def _():
            loss_ref[...] = jnp.zeros_like(loss_ref)
            dg_ref[...] = jnp.zeros_like(dg_ref)

        gate = _sigmoid(pre_ref[...])
        pu = pu_ref[...]
        x2 = x1_ref[...] + gate * pu
        r = lax.rsqrt(jnp.mean(x2 * x2, axis=-1, keepdims=True) + EPS)
        xn = x2 * r
        fg = g_ref[...]
        err = xn * fg - tgt_ref[...]
        loss_ref[...] += 0.5 * jnp.sum(jnp.mean(err * err, axis=-1, keepdims=True))
        dy = err * (1.0 / d)
        dg_ref[...] += jnp.sum(dy * xn, axis=0, keepdims=True)
        dxn = dy * fg
        dx2 = r * (dxn - xn * jnp.mean(dxn * xn, axis=-1, keepdims=True))
        dx2_ref[...] = dx2
        dpre_ref[...] = (dx2 * pu * gate * (1.0 - gate)).astype(BF16)
        dpu_ref[...] = (dx2 * gate).astype(BF16)

    row = pl.BlockSpec((tm, d), lambda i: (i, 0))
    return pl.pallas_call(
        body, name="tail", grid=(t // tm,),
        in_specs=[row, row, row, row, pl.BlockSpec((1, d), lambda i: (0, 0))],
        out_specs=[row, row, row, pl.BlockSpec((1, LANES), lambda i: (0, 0)), pl.BlockSpec((1, d), lambda i: (0, 0))],
        out_shape=[jax.ShapeDtypeStruct((t, d), F32)] + [jax.ShapeDtypeStruct((t, d), BF16)] * 2 + [
            jax.ShapeDtypeStruct((1, LANES), F32),
                                                             jax.ShapeDtypeStruct((1, d), F32)],
        compiler_params=_params(("arbitrary",)))(x1, pre, pu, tgt, final_g)


def _rms_bwd(xin, dhs, dres, g, name, tm=512):
    t, d = xin.shape
    tm = min(tm, t)
    n_dh = len(dhs)

    def body(*refs):
        x_ref, dh_refs, dres_ref, g_ref, dx_ref, dg_ref = refs[0], refs[1:1 + n_dh], *refs[1 + n_dh:]

        @pl.when(pl.program_id(0) == 0)
        def _():
            dg_ref[...] = jnp.zeros_like(dg_ref)

        xv = x_ref[...]
        dh = dh_refs[0][...]
        for ref in dh_refs[1:]:
            dh = dh + ref[...]
        r = lax.rsqrt(jnp.mean(xv * xv, axis=-1, keepdims=True) + EPS)
        xn = xv * r
        dg_ref[...] += jnp.sum(dh * xn, axis=0, keepdims=True)
        dd = dh * g_ref[...]
        dx_ref[...] = dres_ref[...] + r * (dd - xn * jnp.mean(dd * xn, axis=-1, keepdims=True))

    row = pl.BlockSpec((tm, d), lambda i: (i, 0))
    vec = pl.BlockSpec((1, d), lambda i: (0, 0))
    return pl.pallas_call(
        body, name=name, grid=(t // tm,),
        in_specs=[row] * (2 + n_dh) + [vec],
        out_specs=[row, vec],
        out_shape=[jax.ShapeDtypeStruct((t, d), F32), jax.ShapeDtypeStruct((1, d), F32)],
        compiler_params=_params(("arbitrary",)))(xin, *dhs, dres, g)


def _merge_bwd(dm, ya, yb, proj, col0, n_cols, tm=512):
    t, d = ya.shape
    tm = min(tm, t)
    blk = _col_block(col0, 2 * d)

    def body(dm_ref, ya_ref, yb_ref, g_ref, dya_ref, dyb_ref, dg_ref):
        dm_v = dm_ref[...]
        sa = _sigmoid(g_ref[:, :d])
        sb = _sigmoid(g_ref[:, d:])
        dya_ref[...] = (dm_v * sa).astype(BF16)
        dyb_ref[...] = (dm_v * sb).astype(BF16)
        dg_ref[:, :d] = (dm_v * ya_ref[...] * sa * (1.0 - sa)).astype(BF16)
        dg_ref[:, d:] = (dm_v * yb_ref[...] * sb * (1.0 - sb)).astype(BF16)

    row = pl.BlockSpec((tm, d), lambda i: (i, 0))
    gspec = pl.BlockSpec((tm, 2 * d), lambda i: (i, blk))
    return pl.pallas_call(
        body, name="merge_bwd", grid=(t // tm,),
        in_specs=[row, row, row, gspec],
        out_specs=[row, row, gspec],
        out_shape=[jax.ShapeDtypeStruct((t, d), BF16), jax.ShapeDtypeStruct((t, d), BF16),
                   jax.ShapeDtypeStruct((t, n_cols), BF16)],
        compiler_params=_params(("parallel",)))(dm, ya, yb, proj)


def _pool_bwd(proj, dyb, dproj, mix_w, mix_b, scale, nb, s, col0):
    pgd = mix_w.shape[-1]
    blk0 = _col_block(col0, 2 * pgd)

    def body(uz_ref, dy_ref, _, w_ref, b_ref, sc_ref, duz_ref, dw_ref, db_ref, dsc_ref):
        g = pl.program_id(0)

        @pl.when(pl.program_id(1) == 0)
        def _():
            dw_ref[...] = jnp.zeros_like(dw_ref)
            db_ref[...] = jnp.zeros_like(db_ref)
            dsc_ref[...] = jnp.zeros_like(dsc_ref)

        u = uz_ref[:, :pgd]
        zp = uz_ref[:, pgd:]
        row = lax.broadcasted_iota(jnp.int32, u.shape, 0)
        cnt = _pool_count(g, row)
        pooled = _pool_sum(u, g, row, _shift_down) / cnt - u
        wv = w_ref[:, 0].reshape(pgd, pgd)
        mixed = _bdot(pooled, wv) + b_ref[...]
        sg = _sigmoid(zp)
        sz = zp * sg
        dy = dy_ref[...]
        sc = sc_ref[...]
        dsc_ref[...] += jnp.sum(dy * mixed * sz, axis=0, keepdims=True)
        dmixed = dy * sc * sz
        db_ref[...] += jnp.sum(dmixed, axis=0, keepdims=True)
        dw_ref[:, 0] += _bdot_tn(pooled, dmixed).reshape(N_SHARD, pgd // N_SHARD, pgd)
        dpooled = _bdot_nt(dmixed, wv)
        duz_ref[:, :pgd] = (_pool_sum(dpooled / cnt, g, row, _shift_up) - dpooled).astype(BF16)
        duz_ref[:, pgd:] = (dy * mixed * sc * sg * (1.0 + zp * (1.0 - sg))).astype(BF16)

    uz = pl.BlockSpec((s, 2 * pgd), lambda g, b: (b, blk0 + g))
    vec = pl.BlockSpec((1, pgd), lambda g, b: (0, g))
    wspec = pl.BlockSpec((N_SHARD, 1, pgd // N_SHARD, pgd), lambda g, b: (0, g, 0, 0))
    return pl.pallas_call(
        body, name="pool_bwd", grid=(N_POOL, nb),
        in_specs=[uz, pl.BlockSpec((s, pgd), lambda g, b: (b, g)), pl.BlockSpec(memory_space=pl.ANY), wspec, vec, vec],
        out_specs=[uz, wspec, vec, vec],
        out_shape=[jax.ShapeDtypeStruct(dproj.shape, dproj.dtype), jax.ShapeDtypeStruct(mix_w.shape, F32),
                   jax.ShapeDtypeStruct(mix_b.shape, F32), jax.ShapeDtypeStruct(scale.shape, F32)],
        input_output_aliases={2: 0},
        compiler_params=_params(("parallel", "arbitrary")))(proj, dyb, dproj, mix_w, mix_b, scale)


def _ssd_bwd(dyan, ypre, proj, xbc, dtraw, hp, dproj, dtb, alog, dskx, gn, nb, s, di, z_col0):
    t = nb * s
    nc = s // CHUNK
    hpg = di // HEAD_DIM // SSM_GROUPS
    gw = di // SSM_GROUPS
    gn_w = SSM_GROUPS * D_STATE
    dc = di + 2 * gn_w
    b_blk = _col_block(di, gn_w)
    z_blk = _col_block(z_col0, di)
    L, P, N = CHUNK, HEAD_DIM, D_STATE
    tril, expand, expand_t = _ssd_consts(di)

    def body(dy_ref, ypre_ref, z_ref, x_ref, b_ref, c_ref, dtr_ref, hp_ref, _, dtb_ref, alog_ref, dskx_ref, gn_ref,
             tril_ref, e_ref, et_ref, dz_ref, ddt_ref, dxbc_ref, dgn_ref, ddsk_ref, dalog_ref, ddtb_ref,
             dst_ref, dyp_ref, xdt_ref, dxm_ref, t1_ref, t2_ref, t3_ref, aux_ref):
        @pl.when((pl.program_id(0) == 0) & (pl.program_id(1) == 0))
        def _():
            dgn_ref[...] = jnp.zeros_like(dgn_ref)
            ddsk_ref[...] = jnp.zeros_like(ddsk_ref)
            dalog_ref[...] = jnp.zeros_like(dalog_ref)
            ddtb_ref[...] = jnp.zeros_like(ddtb_ref)

        @pl.when(pl.program_id(1) == 0)
        def _():
            dst_ref[...] = jnp.zeros_like(dst_ref)

        tri = tril_ref[...]
        dtpre, dt, a_neg, a_cs, a_cst = _ssd_scalars(dtr_ref, dtb_ref, alog_ref, tri)
        ev = e_ref[...]
        a_exp = _expand(a_cs, ev)
        dt_exp = _expand(dt, ev)
        xv = x_ref[...]
        xdt = xv * dt_exp
        xdt_ref[...] = xdt
        a_last = a_exp[L - 1:L, :]
        dte = jnp.exp(a_last - a_exp)
        xe = xdt * dte
        ea = jnp.exp(a_exp)
        e_last = jnp.exp(a_last)
        lower = tri > 0.5
        aux_ref[...] = jnp.zeros_like(aux_ref)
        for g in range(SSM_GROUPS):
            gs = slice(g * gw, (g + 1) * gw)
            zv = z_ref[:, gs]
            yp = ypre_ref[:, gs]
            sg = _sigmoid(zv)
            sz = zv * sg
            vg = yp * sz
            r = lax.rsqrt(jnp.mean(vg * vg, axis=-1, keepdims=True) + EPS)
            vn = vg * r
            dyg = dy_ref[:, gs]
            dgn_ref[:, gs] += jnp.sum(dyg * vn, axis=0, keepdims=True)
            dvn = dyg * gn_ref[:, gs]
            dv = r * (dvn - vn * jnp.mean(dvn * vn, axis=-1, keepdims=True))
            dy = dv * sz
            dyp_ref[:, gs] = dy
            dz_ref[:, gs] = (dv * yp * sg * (1.0 + zv * (1.0 - sg))).astype(BF16)
            bg = b_ref[:, g * N:(g + 1) * N].astype(BF16)
            cg = c_ref[:, g * N:(g + 1) * N].astype(BF16)
            gm = _bdot_nt(cg, bg)
            ht = hp_ref[0, :, gs]
            dht = dst_ref[:, gs]
            bds = _bdot(bg, dht)
            dye = dy * ea[:, gs]
            xe_g = xe[:, gs]
            dcg = _bdot_nt(dye, ht)
            dbg = _bdot_nt(xe_g, dht)
            dst_ref[:, gs] = e_last[:, gs] * dht + _bdot_tn(cg, dye)
            dgm = jnp.zeros((L, L), F32)
            for e in range(hpg):
                h = g * hpg + e
                hs = slice(h * P, (h + 1) * P)
                decay = jnp.where(lower, jnp.exp(a_cs[:, h:h + 1] - a_cst[h:h + 1, :]), 0.0)
                dy_h = dyp_ref[:, hs]
                dgm = dgm + _bdot_nt(dy_h, xdt_ref[:, hs]) * decay
                dxm_ref[:, hs] = _bdot_tn(gm * decay, dy_h)
            dxbc_ref[:, di + g * N:di + (g + 1) * N] = dbg + _bdot_tn(dgm, cg)
            dxbc_ref[:, di + gn_w + g * N:di + gn_w + (g + 1) * N] = dcg + _bdot(dgm, bg)
            dxm = dxm_ref[:, gs]
            x_g = xv[:, gs]
            dskx = dskx_ref[:, gs]
            xeb = xe_g * bds
            dxdt = dxm + dte[:, gs] * bds
            dxbc_ref[:, gs] = dxdt * dt_exp[:, gs] + dy * dskx
            each = ea[:, gs] * _bdot(cg, ht)
            y_diag = yp - x_g * dskx - each
            rnd = lambda v: v.astype(BF16).astype(F32)
            t1_ref[:, gs] = rnd(dy) * y_diag + dy * each - rnd(xdt[:, gs]) * dxm - xeb
            t2_ref[:, gs] = xeb
            t3_ref[:, gs] = dxdt * x_g
            aux_ref[0:1, gs] = jnp.sum(dht * ht, axis=0, keepdims=True)
            aux_ref[1:2, gs] = jnp.sum(dy * x_g, axis=0, keepdims=True)
        etv = et_ref[...]
        w_end = _head_sum(t2_ref[...], etv)
        aux = _head_sum(aux_ref[...], etv)
        rowi = lax.broadcasted_iota(jnp.int32, (L, LANES), 0)
        end = jnp.sum(w_end, axis=0, keepdims=True) + aux[0:1, :] * jnp.exp(a_cs[L - 1:L, :])
        da = _head_sum(t1_ref[...], etv, terms=3) + jnp.where(rowi == L - 1, end, 0.0)
        rc = lax.dot_general(tri, da, (((0,), (0,)), ((), ())), precision=HIGHEST, preferred_element_type=F32)
        ddt = a_neg * rc + _head_sum(t3_ref[...], etv)
        ddtraw = ddt * _sigmoid(dtpre)
        ddt_ref[...] = ddtraw.astype(BF16)
        ddtb_ref[...] += jnp.sum(ddtraw, axis=0, keepdims=True)
        dalog_ref[...] += jnp.sum(dt * rc, axis=0, keepdims=True) * a_neg
        ddsk_ref[...] += aux[1:2, :]

    row = lambda b, c: b * nc + (nc - 1 - c)
    full = lambda w: pl.BlockSpec((L, w), lambda b, c: (row(b, c), 0))
    zspec = pl.BlockSpec((L, di), lambda b, c: (row(b, c), z_blk))
    vec = lambda w: pl.BlockSpec((1, w), lambda b, c: (0, 0))
    slab = lambda shape: pltpu.VMEM(shape, F32)
    return pl.pallas_call(
        body, name="ssd_bwd", grid=(nb, nc),
        in_specs=[full(di), full(di), zspec, full(di),
                  pl.BlockSpec((L, gn_w), lambda b, c: (row(b, c), b_blk)),
                  pl.BlockSpec((L, gn_w), lambda b, c: (row(b, c), b_blk + 1)),
                  full(LANES),
                  pl.BlockSpec((1, N, di), lambda b, c: (row(b, c), 0, 0)),
                  pl.BlockSpec(memory_space=pl.ANY),
                  vec(LANES), vec(LANES), vec(di), vec(di),
                  pl.BlockSpec((L, L), lambda b, c: (0, 0)),
                  pl.BlockSpec((LANES, di), lambda b, c: (0, 0)),
                  pl.BlockSpec((di, LANES), lambda b, c: (0, 0))],
        out_specs=[zspec, full(LANES), full(dc), vec(di), vec(LANES), vec(LANES), vec(LANES)],
        out_shape=[jax.ShapeDtypeStruct(dproj.shape, dproj.dtype), jax.ShapeDtypeStruct((t, LANES), BF16),
                   jax.ShapeDtypeStruct((t, dc), F32), jax.ShapeDtypeStruct((1, di), F32),
                   jax.ShapeDtypeStruct((1, LANES), F32), jax.ShapeDtypeStruct((1, LANES), F32),
                   jax.ShapeDtypeStruct((1, LANES), F32)],
        scratch_shapes=[slab((N, di)), slab((L, di)), slab((L, di)), slab((L, di)), slab((L, di)), slab((L, di)),
                        slab((L, di)), slab((8, di))],
        input_output_aliases={8: 0},
        compiler_params=_params(("arbitrary", "arbitrary")))(
            dyan, ypre, proj, xbc, xbc, xbc, dtraw, hp, dproj, dtb, alog, dskx, gn, tril, expand, expand_t)


def _conv_bwd(proj, dxbc, dproj, conv_w, conv_b, nb, s, col0, cb=256):
    n_blk, w_spec = _conv_w_spec(conv_w, cb, 0)
    blk0 = _col_block(col0, cb)
    rc = min(CONV_ROWS, s)

    def body(x_ref, dy_ref, _, w_ref, b_ref, dx_ref, dw_ref, db_ref, dacc_ref):
        @pl.when(pl.program_id(1) == 0)
        def _():
            dw_ref[...] = jnp.zeros_like(dw_ref)
            db_ref[...] = jnp.zeros_like(db_ref)

        wts, bias = _conv_weights(w_ref), b_ref[...]
        fold = lambda v: v.reshape(rc // 8, 8, cb).sum(axis=0)
        db8 = jnp.zeros((8, cb), F32)
        dw8 = [jnp.zeros((8, cb), F32) for _ in range(CONV_WIDTH)]
        for t0 in range(0, s, rc):
            taps = _conv_taps(x_ref, t0, rc)
            acc = _conv_pre(taps, wts, bias)
            sg = _sigmoid(acc)
            dacc = dy_ref[t0:t0 + rc, :] * sg * (1.0 + acc * (1.0 - sg))
            dacc_ref[t0:t0 + rc, :] = dacc
            db8 = db8 + fold(dacc)
            dw8 = [dw8[j] + fold(dacc * taps[j]) for j in range(CONV_WIDTH)]
        db_ref[...] += jnp.sum(db8, axis=0, keepdims=True)
        for j in range(CONV_WIDTH):
            dw_ref[0, CONV_WIDTH - 1 - j:CONV_WIDTH - j, :] += jnp.sum(dw8[j], axis=0, keepdims=True)
        for t0 in range(0, s, rc):
            if t0 + rc < s:
                n = rc + CONV_HALO
                win = dacc_ref[t0:t0 + n, :]
                ups = [win[:rc]] + [pltpu.roll(win, n - j, 0)[:rc] for j in range(1, CONV_WIDTH)]
            else:
                cur = dacc_ref[t0:t0 + rc, :]
                row = lax.broadcasted_iota(jnp.int32, cur.shape, 0)
                ups = [cur] + [_shift_up(cur, j, row) for j in range(1, CONV_WIDTH)]
            dx = ups[0] * wts[0]
            for j in range(1, CONV_WIDTH):
                dx = dx + ups[j] * wts[j]
            dx_ref[t0:t0 + rc, :] = dx.astype(BF16)

    return pl.pallas_call(
        body, name="conv_bwd", grid=(n_blk, nb),
        in_specs=[pl.BlockSpec((s, cb), lambda j, b: (b, blk0 + j)), pl.BlockSpec((s, cb), lambda j, b: (b, j)),
                  pl.BlockSpec(memory_space=pl.ANY), w_spec, pl.BlockSpec((1, cb), lambda j, b: (0, j))],
        out_specs=[pl.BlockSpec((s, cb), lambda j, b: (b, blk0 + j)), w_spec, pl.BlockSpec((1, cb), lambda j, b: (0, j))],
        out_shape=[jax.ShapeDtypeStruct(dproj.shape, dproj.dtype), jax.ShapeDtypeStruct(conv_w.shape, F32),
                   jax.ShapeDtypeStruct(conv_b.shape, F32)],
        scratch_shapes=[pltpu.VMEM((s, cb), F32)],
        input_output_aliases={2: 0},
        compiler_params=_params(("parallel", "arbitrary")))(proj, dxbc, dproj, conv_w, conv_b)


def _local_step(x, p, tgt, wg, small, rest_weights, early_grads):
    nb, s, d = x.shape
    t = nb * s
    gn_w = SSM_GROUPS * D_STATE
    dc = N_SHARD * wg["conv_w"].shape[2]
    di = dc - 2 * gn_w
    nh = di // HEAD_DIM
    pgd = d // N_POOL
    x2 = x.reshape(t, d)
    p2 = p.reshape(t, p.shape[-1])
    tgt2 = tgt.reshape(t, d)

    w_main, w_dt = _regroup_w_in(wg["w_in"], d, di, dc, nh)
    c_g, c_z, c_xbc, c_uz = 0, 2 * d, 2 * d + di, 2 * d + di + dc
    n_main = w_main.shape[1]

    pad_h = lambda v: jnp.pad(v.reshape(1, nh).astype(F32), ((0, 0), (0, LANES - nh)))
    dtb, alog = pad_h(small["dt_bias"]), pad_h(small["a_log"])
    dskx = jnp.repeat(small["d_skip"].reshape(1, nh).astype(F32), HEAD_DIM, axis=1)
    vec = lambda v: v.reshape(1, -1).astype(F32)
    norm_g, gn, conv_b = vec(small["norm_g"]), vec(small["gnorm_g"]), vec(small["conv_b"])
    mix_b, scale = vec(small["pool_mix_b"]), vec(small["pool_scale"])
    ple_g, final_g = vec(small["ple_norm_g"]), vec(small["final_g"])
    conv_w = wg["conv_w"]

    proj, dtraw, h = _inproj(x2, norm_g, w_main, w_dt)
    xbc = _conv_fwd(proj, conv_w, conv_b, nb, s, c_xbc)
    ypre, yan, hp = _ssd_fwd(proj, xbc, dtraw, dtb, alog, dskx, gn, nb, s, di, c_z)
    wr = rest_weights(yan)
    mix_w = wr["pool_mix_w"].reshape(N_SHARD, N_POOL, pgd // N_SHARD, pgd)
    rows = lambda v: v.reshape(-1, v.shape[-1])
    wa, wb, wo, wpg = rows(wr["w_branch_a"]), rows(wr["w_branch_b"]), rows(wr["w_out"]), rows(wr["w_ple_gate"])
    wup = wr["w_ple_up"]
    ybp = _pool_fwd(proj, mix_w, mix_b, scale, nb, s, c_uz)
    ya = _mm(yan, wa, "branch_a")
    yb = _mm(ybp, wb, "branch_b")
    merged = _merge_fwd(ya, yb, proj, c_g)
    mo = _mm(merged, wo, "out_proj")
    x1, hn = _ple_pre(x2, mo, ple_g)
    pre = _mm(hn, wpg, "ple_gate")
    pu = _mm(p2, wup, "ple_up")

    dx2, dpre, dpu, loss, d_final_g = _tail(x1, pre, pu, tgt2, final_g)
    d_wpg = _mm_tn(hn, dpre, "d_w_ple_gate")
    d_wup = _mm_tn(p2, dpu, "d_w_ple_up", tn=wup.shape[-1], col_blocks=True)
    dhn = _mm_nt(dpre, wpg, "d_hn")
    dx1, d_ple_g = _rms_bwd(x1, [dhn], dx2, ple_g, "ple_bwd")
    d_wo = _mm_tn(merged, dx1, "d_w_out")
    dm = _mm_nt(dx1, wo, "d_merged")
    dya, dyb, dproj = _merge_bwd(dm, ya, yb, proj, c_g, n_main)
    d_wa = _mm_tn(yan, dya, "d_w_branch_a")
    d_wb = _mm_tn(ybp, dyb, "d_w_branch_b")
    dyan = _mm_nt(dya, wa, "d_y_a")
    dybp = _mm_nt(dyb, wb, "d_y_b")
    dproj, d_mix_w, d_mix_b, d_scale = _pool_bwd(proj, dybp, dproj, mix_w, mix_b, scale, nb, s, c_uz)
    shard_major = lambda v: v.reshape(N_SHARD, v.shape[0] // N_SHARD, v.shape[1])
    early = dict(pool_mix_w=d_mix_w.reshape(N_SHARD, pgd, pgd), w_branch_a=shard_major(d_wa),
                 w_branch_b=shard_major(d_wb), w_out=shard_major(d_wo), w_ple_gate=shard_major(d_wpg),
                 w_ple_up=d_wup)
    token = early_grads(early)
    dproj, ddt, dxbc, d_gn, d_dsk, d_alog, d_dtb = _ssd_bwd(
        dyan, ypre, proj, xbc, dtraw, hp, dproj, dtb + token[0:1, 0:1], alog, dskx, gn, nb, s, di, c_z)
    dproj, d_conv_w, d_conv_b = _conv_bwd(proj, dxbc, dproj, conv_w, conv_b, nb, s, c_xbc)
    d_wmain = _mm_tn(h, dproj, "d_w_in")
    d_wdt = _mm_tn(h, ddt, "d_w_dt")
    dh_main = _mm_nt(dproj, w_main, "d_h")
    dh_dt = _mm_nt(ddt, w_dt, "d_h_dt")
    gx, d_norm_g = _rms_bwd(x2, [dh_main, dh_dt], dx1, norm_g, "in_bwd")
    d_w_in = _ungroup_w_in(d_wmain, d_wdt, d, di, dc, nh)

    grads = dict(norm_g=d_norm_g, w_in=d_w_in, conv_w=d_conv_w, conv_b=d_conv_b, dt_bias=d_dtb[:, :nh],
                 a_log=d_alog[:, :nh], d_skip=d_dsk[:, :nh], gnorm_g=d_gn, pool_mix_b=d_mix_b, pool_scale=d_scale,
                 ple_norm_g=d_ple_g, final_g=d_final_g, **early)
    return loss[0, 0], gx.reshape(nb, s, d), grads


def _place():
    return lax.axis_index("x"), lax.axis_index("y"), lax.axis_index("c")


def _other_chips(x, y):
    return [(1 - x, y), (x, 1 - y), (1 - x, 1 - y)]


def _halves(c, rows, align):
    rh = rows // 2
    assert rows % 2 == 0 and rh % align == 0, (rows, align)
    return (pl.ds(pl.multiple_of(c * rh, align), rh), pl.ds(pl.multiple_of((1 - c) * rh, align), rh))


HBM = pl.BlockSpec(memory_space=pl.ANY)


def _into_slot(w2, k, dtype, name):
    rows, cols = w2.shape
    rb = _tile(rows, 256)

    def body(k_ref, w_ref, o_ref):
        o_ref[0] = w_ref[...].astype(dtype)

    return pl.pallas_call(
        body, name=name,
        grid_spec=pltpu.PrefetchScalarGridSpec(
            num_scalar_prefetch=1, grid=(rows // rb,),
            in_specs=[pl.BlockSpec((rb, cols), lambda i, k_ref: (i, 0))],
            out_specs=pl.BlockSpec((1, rb, cols), lambda i, k_ref: (k_ref[0], i, 0))),
        out_shape=jax.ShapeDtypeStruct((N_SHARD, rows, cols), dtype),
        compiler_params=_params(("parallel",)))(k.reshape(1), w2)


def _gather_weights(split, whole):
    n_split, n_all = len(split), len(split) + len(whole)

    def body(*refs):
        bufs = refs[n_all:2 * n_all]
        send_sems, recv_sems = refs[2 * n_all:]
        x, y, c = _place()
        k = 2 * x + y
        chips = _other_chips(x, y)

        def copy(idx, block, to):
            return pltpu.make_async_remote_copy(src_ref=block, dst_ref=block, send_sem=send_sems.at[idx],
                                                recv_sem=recv_sems.at[idx], device_id=to, device_id_type=MESH)

        def block(i, shard, rows):
            return bufs[i].at[shard, rows] if i < n_split else bufs[i].at[shard]

        def sem(i, j):
            return 6 * i + j if i < n_split else 6 * n_split + 3 * (i - n_split) + j

        started = []
        for i in range(n_all):
            mine, _ = _halves(c, bufs[i].shape[1], 16) if i < n_split else (None, None)
            for j, (px, py) in enumerate(chips):
                started.append(copy(sem(i, j), block(i, k, mine), (px, py, c)))
                started[-1].start()
        for i in range(n_all):
            mine, _ = _halves(c, bufs[i].shape[1], 16) if i < n_split else (None, None)
            for j, (px, py) in enumerate(chips):
                landed = block(i, 2 * px + py, mine)
                copy(sem(i, j), landed, (px, py, c)).wait_recv()
                if i < n_split:
                    started.append(copy(sem(i, 3 + j), landed, (x, y, 1 - c)))
                    started[-1].start()
        for i in range(n_split):
            _, theirs = _halves(c, bufs[i].shape[1], 16)
            for j, (px, py) in enumerate(chips):
                copy(sem(i, 3 + j), block(i, 2 * px + py, theirs), (x, y, 1 - c)).wait_recv()
        for cp in started:
            cp.wait_send()

    arrays = list(split) + list(whole)
    n_sem = 6 * n_split + 3 * len(whole)
    return pl.pallas_call(
        body, name="gather_weights",
        in_specs=[HBM] * n_all, out_specs=[HBM] * n_all,
        out_shape=[jax.ShapeDtypeStruct(a.shape, a.dtype) for a in arrays],
        input_output_aliases={i: i for i in range(n_all)},
        scratch_shapes=[pltpu.SemaphoreType.DMA((n_sem,)), pltpu.SemaphoreType.DMA((n_sem,))],
    )(*arrays)


def _swap_halves(gs):
    n = len(gs)

    def body(*refs):
        ins, outs, send_sems, recv_sems = refs[:n], refs[n:2 * n], refs[2 * n], refs[2 * n + 1]
        x, y, c = _place()
        copies = []
        for i in range(n):
            _, theirs = _halves(c, gs[i].shape[1], 8)
            cp = pltpu.make_async_remote_copy(src_ref=ins[i].at[:, theirs], dst_ref=outs[i], send_sem=send_sems.at[i],
                                              recv_sem=recv_sems.at[i], device_id=(x, y, 1 - c), device_id_type=MESH)
            cp.start()
            copies.append(cp)
        for cp in copies:
            cp.wait()

    return pl.pallas_call(
        body, name="swap_halves", in_specs=[HBM] * n, out_specs=[HBM] * n,
        out_shape=[jax.ShapeDtypeStruct((g.shape[0], g.shape[1] // 2, g.shape[2]), g.dtype) for g in gs],
        scratch_shapes=[pltpu.SemaphoreType.DMA((n,)), pltpu.SemaphoreType.DMA((n,))],
    )(*gs)


def _scatter_chips(ps):
    n = len(ps)

    def body(*refs):
        ins, outs, send_sems, recv_sems = refs[:n], refs[n:2 * n], refs[2 * n], refs[2 * n + 1]
        x, y, c = _place()
        copies = []
        for i in range(n):
            for j, (px, py) in enumerate(_other_chips(x, y)):
                cp = pltpu.make_async_remote_copy(src_ref=ins[i].at[2 * px + py], dst_ref=outs[i].at[j],
                                                  send_sem=send_sems.at[3 * i + j], recv_sem=recv_sems.at[3 * i + j],
                                                  device_id=(px, py, c), device_id_type=MESH)
                cp.start()
                copies.append(cp)
        for cp in copies:
            cp.wait()

    return pl.pallas_call(
        body, name="scatter_chips", in_specs=[HBM] * n, out_specs=[HBM] * n,
        out_shape=[jax.ShapeDtypeStruct((N_SHARD - 1,) + v.shape[1:], v.dtype) for v in ps],
        scratch_shapes=[pltpu.SemaphoreType.DMA((3 * n,)), pltpu.SemaphoreType.DMA((3 * n,))],
    )(*ps)


def _join_halves(vs):
    n = len(vs)

    def body(*refs):
        bufs, send_sems, recv_sems = refs[n:2 * n], refs[2 * n], refs[2 * n + 1]
        x, y, c = _place()

        def copy(i, rows):
            return pltpu.make_async_remote_copy(src_ref=bufs[i].at[rows], dst_ref=bufs[i].at[rows],
                                                send_sem=send_sems.at[i], recv_sem=recv_sems.at[i],
                                                device_id=(x, y, 1 - c), device_id_type=MESH)

        halves = [_halves(c, bufs[i].shape[0], 8) for i in range(n)]
        sends = [copy(i, halves[i][0]) for i in range(n)]
        for cp in sends:
            cp.start()
        for i in range(n):
            copy(i, halves[i][1]).wait_recv()
        for cp in sends:
            cp.wait_send()

    return pl.pallas_call(
        body, name="join_halves", in_specs=[HBM] * n, out_specs=[HBM] * n,
        out_shape=[jax.ShapeDtypeStruct(v.shape, v.dtype) for v in vs],
        input_output_aliases={i: i for i in range(n)},
        scratch_shapes=[pltpu.SemaphoreType.DMA((n,)), pltpu.SemaphoreType.DMA((n,))],
    )(*vs)


SEM = pl.BlockSpec(memory_space=pltpu.SEMAPHORE)
IN_HBM = pl.BlockSpec(memory_space=pltpu.HBM)
SPLIT_EFFECT = pltpu.SideEffectType.DATAFLOW_SIDE_EFFECTING


def _split_copies(plan, refs, send_sems, recv_sems):
    pairs = []
    for idx, (src, dst, landing, to) in enumerate(plan(refs)):
        mk = lambda d: pltpu.make_async_remote_copy(src_ref=src, dst_ref=d, send_sem=send_sems.at[idx],
                                                    recv_sem=recv_sems.at[idx], device_id=to, device_id_type=MESH)
        pairs.append((mk(dst), mk(landing)))
    return pairs


def _split_start(name, bufs, after, plan, n_copies):
    n = len(bufs)

    def body(*refs):
        send_sems, recv_sems, token = refs[n + 1], refs[n + 2], refs[-1]
        for send, _ in _split_copies(plan, refs[:n], send_sems, recv_sems):
            send.start()
        token[...] = jnp.zeros_like(token)

    sems = pltpu.SemaphoreType.DMA((n_copies,))
    out = pl.pallas_call(
        body, name=name,
        in_specs=[IN_HBM] * n + [HBM],
        out_specs=[SEM, SEM] + [IN_HBM] * n + [pl.BlockSpec(memory_space=pltpu.VMEM)],
        out_shape=[sems, sems] + [pltpu.HBM(b.shape, b.dtype) for b in bufs] + [jax.ShapeDtypeStruct((8, LANES), F32)],
        input_output_aliases={i: 2 + i for i in range(n)},
        compiler_params=pltpu.CompilerParams(has_side_effects=SPLIT_EFFECT),
    )(*[pltpu.with_memory_space_constraint(b, pltpu.HBM) for b in bufs], after)
    return out[0], out[1], out[2:2 + n], out[-1]


def _split_wait(name, bufs, send_sems, recv_sems, after, plan):
    n = len(bufs)

    def body(*refs):
        for send, recv in _split_copies(plan, refs[:n], refs[n], refs[n + 1]):
            send.wait_send()
            recv.wait_recv()

    return pl.pallas_call(
        body, name=name,
        in_specs=[IN_HBM] * n + [SEM, SEM, HBM],
        out_specs=[IN_HBM] * n,
        out_shape=[pltpu.HBM(b.shape, b.dtype) for b in bufs],
        input_output_aliases={i: i for i in range(n)},
        compiler_params=pltpu.CompilerParams(has_side_effects=SPLIT_EFFECT),
    )(*bufs, send_sems, recv_sems, after)


def _gather_plan(n):
    def plan(refs):
        x, y, c = _place()
        k = 2 * x + y
        return [(refs[i].at[k], refs[i].at[k], refs[i].at[2 * px + py], (px, py, c))
                for i in range(n) for px, py in _other_chips(x, y)]
    return plan


def _scatter_plan(n):
    def plan(refs):
        x, y, c = _place()
        return [(refs[i].at[2 * px + py], refs[n + i].at[j], refs[n + i].at[j], (px, py, c))
                for i in range(n) for j, (px, py) in enumerate(_other_chips(x, y))]
    return plan


def _swap_sibling(vs):
    n = len(vs)

    def body(*refs):
        ins, outs, send_sems, recv_sems = refs[:n], refs[n:2 * n], refs[2 * n], refs[2 * n + 1]
        x, y, c = _place()
        copies = [pltpu.make_async_remote_copy(src_ref=ins[i], dst_ref=outs[i], send_sem=send_sems.at[i],
                                               recv_sem=recv_sems.at[i], device_id=(x, y, 1 - c), device_id_type=MESH)
                  for i in range(n)]
        for cp in copies:
            cp.start()
        for cp in copies:
            cp.wait()

    return pl.pallas_call(
        body, name="swap_sibling", in_specs=[HBM] * n, out_specs=[HBM] * n,
        out_shape=[jax.ShapeDtypeStruct(v.shape, v.dtype) for v in vs],
        scratch_shapes=[pltpu.SemaphoreType.DMA((n,)), pltpu.SemaphoreType.DMA((n,))],
    )(*vs)


def _to_bf16(g, name):
    _, rows, cols = g.shape
    rb = _tile(rows, 256)

    def body(g_ref, o_ref):
        o_ref[...] = g_ref[...].astype(BF16)

    spec = pl.BlockSpec((1, rb, cols), lambda j, i: (j, i, 0))
    return pl.pallas_call(
        body, name=name, grid=(N_SHARD, rows // rb), in_specs=[spec], out_specs=spec,
        out_shape=jax.ShapeDtypeStruct(g.shape, BF16),
        compiler_params=_params(("parallel", "parallel")))(g)


def _add_landed(g, landed, k, name):
    _, rows, cols = g.shape
    rb = _tile(rows, 256)

    def body(k_ref, g_ref, l_ref, o_ref):
        o_ref[...] = g_ref[0] + l_ref[0].astype(F32) + l_ref[1].astype(F32) + l_ref[2].astype(F32)

    return pl.pallas_call(
        body, name=name,
        grid_spec=pltpu.PrefetchScalarGridSpec(
            num_scalar_prefetch=1, grid=(rows // rb,),
            in_specs=[pl.BlockSpec((1, rb, cols), lambda i, k_ref: (k_ref[0], i, 0)),
                      pl.BlockSpec((N_SHARD - 1, rb, cols), lambda i, k_ref: (0, i, 0))],
            out_specs=pl.BlockSpec((rb, cols), lambda i, k_ref: (i, 0))),
        out_shape=jax.ShapeDtypeStruct((rows, cols), F32),
        compiler_params=_params(("parallel",)))(k.reshape(1), g, landed)


def _allreduce_small(v):
    rows = v.shape[0]

    def body(v_ref, o_ref, buf_ref, send_sems, recv_sems):
        x, y, c = _place()
        me = 4 * x + 2 * y + c
        buf_ref[me] = v_ref[...]
        copies = []
        for rel in range(1, 8):
            peer = (x ^ (rel >> 2), y ^ ((rel >> 1) & 1), c ^ (rel & 1))
            cp = pltpu.make_async_remote_copy(src_ref=v_ref, dst_ref=buf_ref.at[me], send_sem=send_sems.at[rel - 1],
                                              recv_sem=recv_sems.at[rel - 1], device_id=peer, device_id_type=MESH)
            cp.start()
            copies.append(cp)
        for rel in range(1, 8):
            peer_id = me ^ rel
            pltpu.make_async_remote_copy(src_ref=v_ref, dst_ref=buf_ref.at[peer_id], send_sem=send_sems.at[rel - 1],
                                         recv_sem=recv_sems.at[rel - 1], device_id=(x, y, c),
                                         device_id_type=MESH).wait_recv()
        for cp in copies:
            cp.wait_send()
        acc = buf_ref[0]
        for i in range(1, 8):
            acc = acc + buf_ref[i]
        o_ref[...] = acc

    return pl.pallas_call(
        body, name="allreduce_small",
        in_specs=[pl.BlockSpec(memory_space=pltpu.VMEM)], out_specs=pl.BlockSpec(memory_space=pltpu.VMEM),
        out_shape=jax.ShapeDtypeStruct(v.shape, F32),
        scratch_shapes=[pltpu.VMEM((8, rows, LANES), F32), pltpu.SemaphoreType.DMA((7,)), pltpu.SemaphoreType.DMA((7,))],
    )(v)


def _add_pair(g, got, c, name):
    _, rh, cols = got.shape
    rb = _tile(rh, 256)
    nrb = rh // rb

    def body(c_ref, g_ref, got_ref, o_ref):
        o_ref[...] = (g_ref[...] + got_ref[...]).astype(BF16)

    spec = pl.BlockSpec((1, rb, cols), lambda j, i, c_ref: (j, i, 0))
    return pl.pallas_call(
        body, name=name,
        grid_spec=pltpu.PrefetchScalarGridSpec(
            num_scalar_prefetch=1, grid=(N_SHARD, nrb),
            in_specs=[pl.BlockSpec((1, rb, cols), lambda j, i, c_ref: (j, c_ref[0] * nrb + i, 0)), spec],
            out_specs=spec),
        out_shape=jax.ShapeDtypeStruct(got.shape, BF16),
        compiler_params=_params(("parallel", "parallel")))(c.reshape(1), g, got)


def _add_chips(g, got, landed, k, c, name):
    _, rh, cols = got.shape
    rb = _tile(rh, 256)
    nrb = rh // rb

    def body(kc_ref, g_ref, got_ref, l_ref, o_ref):
        own = g_ref[0] + got_ref[0]
        o_ref[...] = own + l_ref[0].astype(F32) + l_ref[1].astype(F32) + l_ref[2].astype(F32)

    half_c = lambda i, kc: (kc[1] * nrb + i, 0)
    return pl.pallas_call(
        body, name=name,
        grid_spec=pltpu.PrefetchScalarGridSpec(
            num_scalar_prefetch=1, grid=(nrb,),
            in_specs=[pl.BlockSpec((1, rb, cols), lambda i, kc: (kc[0],) + half_c(i, kc)),
                      pl.BlockSpec((1, rb, cols), lambda i, kc: (kc[0], i, 0)),
                      pl.BlockSpec((N_SHARD - 1, rb, cols), lambda i, kc: (0, i, 0))],
            out_specs=pl.BlockSpec((rb, cols), half_c)),
        out_shape=jax.ShapeDtypeStruct((2 * rh, cols), F32),
        compiler_params=_params(("parallel",)))(jnp.stack([k, c]), g, got, landed)


def _adamw(wv, gs, m, v, name):
    rows, cols = wv.shape
    rb = _tile(rows, 256)
    c1 = 1.0 - ADAM_B1 ** ADAM_STEP
    c2 = 1.0 - ADAM_B2 ** ADAM_STEP
    n_g = len(gs)

    def body(*refs):
        w_ref, g_refs, (m_ref, v_ref, go_ref, d_ref, nm_ref, nv_ref) = refs[0], refs[1:1 + n_g], refs[1 + n_g:]
        gv = g_refs[0][...]
        for ref in g_refs[1:]:
            gv = gv + ref[...]
        go_ref[...] = gv
        nm = ADAM_B1 * m_ref[...] + (1.0 - ADAM_B1) * gv
        nv = ADAM_B2 * v_ref[...] + (1.0 - ADAM_B2) * (gv * gv)
        nm_ref[...] = nm
        nv_ref[...] = nv
        d_ref[...] = -ADAM_LR * ((nm / c1) / (jnp.sqrt(nv / c2) + ADAM_EPS) + ADAM_WD * w_ref[...])

    spec = pl.BlockSpec((rb, cols), lambda i: (i, 0))
    return pl.pallas_call(
        body, name=name, grid=(rows // rb,), in_specs=[spec] * (3 + n_g), out_specs=[spec] * 4,
        out_shape=[jax.ShapeDtypeStruct((rows, cols), F32)] * 4,
        compiler_params=_params(("parallel",)))(wv, *gs, m, v)


def _pack(flats):
    cat = jnp.concatenate([f.reshape(-1) for f in flats])
    n = cat.shape[0]
    rows = -(-n // (8 * LANES)) * 8
    return jnp.pad(cat, (0, rows * LANES - n)).reshape(rows, LANES)


def _unpack(packed, shapes):
    flat = packed.reshape(-1)
    out, off = [], 0
    for shp in shapes:
        n = 1
        for dim in shp:
            n *= dim
        out.append(flat[off:off + n].reshape(shp))
        off += n
    return out


def kernel(x, p, norm_g, w_in, conv_w, conv_b, dt_bias, a_log, d_skip, gnorm_g, pool_mix_w, pool_mix_b, pool_scale, w_branch_a, w_branch_b, w_out, ple_norm_g, w_ple_gate, w_ple_up, final_g, loss_target, m_norm_g, m_w_in, m_conv_w, m_conv_b, m_dt_bias, m_a_log, m_d_skip, m_gnorm_g, m_pool_mix_w, m_pool_mix_b, m_pool_scale, m_w_branch_a, m_w_branch_b, m_w_out, m_ple_norm_g, m_w_ple_gate, m_w_ple_up, m_final_g, v_norm_g, v_w_in, v_conv_w, v_conv_b, v_dt_bias, v_a_log, v_d_skip, v_gnorm_g, v_pool_mix_w, v_pool_mix_b, v_pool_scale, v_w_branch_a, v_w_branch_b, v_w_out, v_ple_norm_g, v_w_ple_gate, v_w_ple_up, v_final_g):
    wts = dict(norm_g=norm_g, w_in=w_in, conv_w=conv_w, conv_b=conv_b, dt_bias=dt_bias, a_log=a_log, d_skip=d_skip,
               gnorm_g=gnorm_g, pool_mix_w=pool_mix_w, pool_mix_b=pool_mix_b, pool_scale=pool_scale,
               w_branch_a=w_branch_a, w_branch_b=w_branch_b, w_out=w_out, ple_norm_g=ple_norm_g,
               w_ple_gate=w_ple_gate, w_ple_up=w_ple_up, final_g=final_g)
    mom_m = dict(norm_g=m_norm_g, w_in=m_w_in, conv_w=m_conv_w, conv_b=m_conv_b, dt_bias=m_dt_bias, a_log=m_a_log,
                 d_skip=m_d_skip, gnorm_g=m_gnorm_g, pool_mix_w=m_pool_mix_w, pool_mix_b=m_pool_mix_b,
                 pool_scale=m_pool_scale, w_branch_a=m_w_branch_a, w_branch_b=m_w_branch_b, w_out=m_w_out,
                 ple_norm_g=m_ple_norm_g, w_ple_gate=m_w_ple_gate, w_ple_up=m_w_ple_up, final_g=m_final_g)
    mom_v = dict(norm_g=v_norm_g, w_in=v_w_in, conv_w=v_conv_w, conv_b=v_conv_b, dt_bias=v_dt_bias, a_log=v_a_log,
                 d_skip=v_d_skip, gnorm_g=v_gnorm_g, pool_mix_w=v_pool_mix_w, pool_mix_b=v_pool_mix_b,
                 pool_scale=v_pool_scale, w_branch_a=v_w_branch_a, w_branch_b=v_w_branch_b, w_out=v_w_out,
                 ple_norm_g=v_ple_norm_g, w_ple_gate=v_w_ple_gate, w_ple_up=v_w_ple_up, final_g=v_final_g)
    c = lax.axis_index("c")
    k = 2 * lax.axis_index("x") + lax.axis_index("y")
    flat2 = lambda a: a.reshape(-1, a.shape[-1])

    slots = {n: _into_slot(flat2(wts[n]), k, BF16, "slot_" + n) for n in BIG}
    w_in_g, conv_g = _gather_weights([slots["w_in"]], [_into_slot(flat2(conv_w), k, F32, "slot_conv_w")])
    n_rest = len(REST)
    gsend, grecv, gbufs, gtoken = _split_start("gather_rest_start", [slots[n] for n in REST], conv_g,
                                               _gather_plan(n_rest), 3 * n_rest)

    def rest_weights(after):
        return dict(zip(REST, _split_wait("gather_rest_wait", gbufs, gsend, grecv, after, _gather_plan(n_rest))))

    flying = {}

    def early_grads(early):
        sends = [_to_bf16(early[n], "bf16_" + n) for n in REST]
        lands = [pltpu.with_memory_space_constraint(lax.empty((N_SHARD - 1,) + v.shape[1:], BF16), pltpu.HBM)
                 for v in sends]
        ssend, srecv, sbufs, stoken = _split_start("scatter_rest_start", sends + lands, early[REST[0]],
                                                   _scatter_plan(n_rest), 3 * n_rest)
        flying.update(send=ssend, recv=srecv, bufs=sbufs)
        return stoken

    small = {n: wts[n] for n in SMALL}
    small["norm_g"] = norm_g + gtoken[0, 0]
    loss, grad_x, grads = _local_step(x, p[0], loss_target, dict(w_in=w_in_g, conv_w=conv_g), small,
                                      rest_weights, early_grads)

    g_w_in = grads["w_in"]
    got = _swap_halves([g_w_in])[0]
    pair = _add_pair(g_w_in, got, c, "add_pair_w_in")
    landed = _scatter_chips([pair])[0]
    w_in_sum = _join_halves([_add_chips(g_w_in, got, landed, k, c, "add_chips_w_in")])[0]

    sbufs = _split_wait("scatter_rest_wait", flying["bufs"], flying["send"], flying["recv"], g_w_in,
                        _scatter_plan(n_rest))
    mine = [_add_landed(grads[n], ld, k, "add_landed_" + n) for n, ld in zip(REST, sbufs[n_rest:])]
    theirs = _swap_sibling(mine)
    g_sums = dict(zip(REST, zip(mine, theirs)))
    g_sums["w_in"] = (w_in_sum,)

    conv_shape = flat2(conv_w).shape
    small_sum = _allreduce_small(_pack([grads[n] for n in SMALL] + [grads["conv_w"], loss]))
    small_shapes = [wts[n].shape for n in SMALL] + [(N_SHARD,) + conv_shape, (1,)]
    small_g = _unpack(small_sum, small_shapes)
    g_conv = lax.dynamic_index_in_dim(small_g[-2], k, axis=0, keepdims=False)

    outs = {}
    for n in BIG:
        vals = _adamw(flat2(wts[n]), g_sums[n], flat2(mom_m[n]), flat2(mom_v[n]), "adamw_" + n)
        for kind, val in zip(("grad", "delta", "new_m", "new_v"), vals):
            outs[kind, n] = val.reshape(wts[n].shape)
    names = SMALL + ("conv_w",)
    sm = _adamw(_pack([wts[n] for n in names]), (_pack(small_g[:len(SMALL)] + [g_conv]),),
                _pack([mom_m[n] for n in names]), _pack([mom_v[n] for n in names]), "adamw_small")
    sm_shapes = [wts[n].shape for n in names]
    for kind, val in zip(("grad", "delta", "new_m", "new_v"), sm):
        for n, piece in zip(names, _unpack(val, sm_shapes)):
            outs[kind, n] = piece
    return (small_g[-1][0], grad_x, *[outs[kind, n] for kind in ("grad", "delta", "new_m", "new_v") for n in WEIGHTS])
```

```python
import functools

import jax
import jax.numpy as jnp
from jax import lax
from jax.experimental import pallas as pl
from jax.experimental.pallas import tpu as pltpu

F32 = jnp.float32
BF16 = jnp.bfloat16
HIGHEST = lax.Precision.HIGHEST
MESH = pl.DeviceIdType.MESH

EPS = 1e-6
HEAD_DIM = 64
SSM_GROUPS = 4
D_STATE = 128
CONV_WIDTH = 4
CHUNK = 128
N_POOL = 4
LANES = 128
N_SHARD = 4

ADAM_LR = 0.001
ADAM_B1 = 0.9
ADAM_B2 = 0.999
ADAM_EPS = 1e-08
ADAM_WD = 0.01
ADAM_STEP = 10

BIG = ("w_in", "pool_mix_w", "w_branch_a", "w_branch_b", "w_out", "w_ple_gate", "w_ple_up")
REST = BIG[1:]
SMALL = ("norm_g", "conv_b", "dt_bias", "a_log", "d_skip", "gnorm_g", "pool_mix_b", "pool_scale",
         "ple_norm_g", "final_g")
WEIGHTS = ("norm_g", "w_in", "conv_w", "conv_b", "dt_bias", "a_log", "d_skip", "gnorm_g", "pool_mix_w",
           "pool_mix_b", "pool_scale", "w_branch_a", "w_branch_b", "w_out", "ple_norm_g", "w_ple_gate",
           "w_ple_up", "final_g")


def _params(sem=None, vmem_mb=56):
    kw = dict(vmem_limit_bytes=vmem_mb << 20)
    if sem is not None:
        kw["dimension_semantics"] = sem
    return pltpu.CompilerParams(**kw)


def _sigmoid(v):
    return 1.0 / (1.0 + jnp.exp(-v))


def _softplus(v):
    return jnp.maximum(v, 0.0) + jnp.log1p(jnp.exp(-jnp.abs(v)))


def _bdot(a, b):
    return jnp.dot(a.astype(BF16), b.astype(BF16), preferred_element_type=F32)


def _bdot_nt(a, b):
    return lax.dot_general(a.astype(BF16), b.astype(BF16), (((1,), (1,)), ((), ())), preferred_element_type=F32)


def _bdot_tn(a, b):
    return lax.dot_general(a.astype(BF16), b.astype(BF16), (((0,), (0,)), ((), ())), preferred_element_type=F32)


def _col_block(col0, width):
    assert col0 % width == 0, (col0, width)
    return col0 // width


def _tile(n, cap):
    if n <= cap:
        return n
    best = None
    for cand in range(8, cap + 1, 8):
        if n % cand == 0:
            best = cand
    assert best is not None, (n, cap)
    return best


def _shift_down(v, j, row):
    return jnp.where(row >= j, pltpu.roll(v, j, 0), 0.0)


def _shift_up(v, j, row):
    n = v.shape[0]
    return jnp.where(row < n - j, pltpu.roll(v, n - j, 0), 0.0)


def _mm(a, w, name, tm=1024, tn=1024):
    t, k = a.shape
    tm = min(tm, t)
    blocked = w.ndim == 3
    if blocked:
        nblk, _, tn = w.shape
        n = nblk * tn
        w_spec = pl.BlockSpec((1, k, tn), lambda i, j: (j, 0, 0))
    else:
        n = w.shape[1]
        tn = min(tn, n)
        w_spec = pl.BlockSpec((k, tn), lambda i, j: (0, j))

    def body(a_ref, w_ref, o_ref):
        wv = w_ref[0] if blocked else w_ref[...]
        o_ref[...] = _bdot(a_ref[...], wv)

    return pl.pallas_call(
        body, name=name, grid=(t // tm, n // tn),
        in_specs=[pl.BlockSpec((tm, k), lambda i, j: (i, 0)), w_spec],
        out_specs=pl.BlockSpec((tm, tn), lambda i, j: (i, j)),
        out_shape=jax.ShapeDtypeStruct((t, n), F32),
        compiler_params=_params(("parallel", "parallel")))(a, w)


def _mm_nt(a, w, name, tm=1024, tk=1024, after=None):
    t, k = a.shape
    n = w.shape[0]
    tm, tk = min(tm, t), min(tk, k)
    extra = [] if after is None else [after]

    def body(a_ref, w_ref, *rest):
        o_ref = rest[-1]
        kk = pl.program_id(1)
        part = _bdot_nt(a_ref[...], w_ref[...])

        @pl.when(kk == 0)
        def _():
            o_ref[...] = part

        @pl.when(kk > 0)
        def _():
            o_ref[...] += part

    return pl.pallas_call(
        body, name=name, grid=(t // tm, k // tk),
        in_specs=[pl.BlockSpec((tm, tk), lambda i, j: (i, j)), pl.BlockSpec((n, tk), lambda i, j: (0, j))]
        + [pl.BlockSpec((8, LANES), lambda i, j: (0, 0))] * len(extra),
        out_specs=pl.BlockSpec((tm, n), lambda i, j: (i, 0)),
        out_shape=jax.ShapeDtypeStruct((t, n), F32),
        compiler_params=_params(("parallel", "arbitrary")))(a, w, *extra)


def _mm_tn(a, b, name, tn=1024, tk=1024, col_blocks=False):
    t, m = a.shape
    n = b.shape[1]
    tn, tk = min(tn, n), min(tk, t)

    def body(a_ref, b_ref, o_ref):
        kk = pl.program_id(1)
        part = _bdot_tn(a_ref[...], b_ref[...])
        part = part[None] if col_blocks else part

        @pl.when(kk == 0)
        def _():
            o_ref[...] = part

        @pl.when(kk > 0)
        def _():
            o_ref[...] += part

    if col_blocks:
        out_spec = pl.BlockSpec((1, m, tn), lambda j, kk: (j, 0, 0))
        out_shape = jax.ShapeDtypeStruct((n // tn, m, tn), F32)
    else:
        out_spec = pl.BlockSpec((m, tn), lambda j, kk: (0, j))
        out_shape = jax.ShapeDtypeStruct((m, n), F32)
    return pl.pallas_call(
        body, name=name, grid=(n // tn, t // tk),
        in_specs=[pl.BlockSpec((tk, m), lambda j, kk: (kk, 0)), pl.BlockSpec((tk, tn), lambda j, kk: (kk, j))],
        out_specs=out_spec, out_shape=out_shape,
        compiler_params=_params(("parallel", "arbitrary")))(a, b)


def _w_in_pieces(d, di, dc, nh, shard_w):
    pgd = d // N_POOL
    o_dt, o_u = di + dc, di + dc + nh
    o_zp, o_ga, o_gb = o_u + d, o_u + 2 * d, o_u + 3 * d
    c_z, c_uz = 2 * d, 2 * d + di + dc
    runs = [(False, 0, o_ga, d), (False, d, o_gb, d), (False, c_z, 0, di + dc), (True, 0, o_dt, nh)]
    for g in range(N_POOL):
        runs.append((False, c_uz + 2 * g * pgd, o_u + g * pgd, pgd))
        runs.append((False, c_uz + (2 * g + 1) * pgd, o_zp + g * pgd, pgd))
    pieces = []
    for is_dt, dst, src, n in runs:
        while n > 0:
            k, off = divmod(src, shard_w)
            m = min(n, shard_w - off)
            pieces.append((is_dt, dst, k, off, m))
            dst, src, n = dst + m, src + m, n - m
    return pieces


def _regroup_w_in(w_sh, d, di, dc, nh, rb=256):
    _, rows, sw = w_sh.shape
    n_main = 4 * d + di + dc
    pieces = _w_in_pieces(d, di, dc, nh, sw)
    rb = min(rb, rows)

    def body(w_ref, main_ref, dt_ref):
        dt_ref[...] = jnp.zeros_like(dt_ref)
        for is_dt, dst, k, off, m in pieces:
            out = dt_ref if is_dt else main_ref
            out[:, dst:dst + m] = w_ref[k, :, off:off + m]

    return pl.pallas_call(
        body, name="regroup_w_in", grid=(rows // rb,),
        in_specs=[pl.BlockSpec((N_SHARD, rb, sw), lambda i: (0, i, 0))],
        out_specs=[pl.BlockSpec((rb, n_main), lambda i: (i, 0)), pl.BlockSpec((rb, LANES), lambda i: (i, 0))],
        out_shape=[jax.ShapeDtypeStruct((rows, n_main), w_sh.dtype), jax.ShapeDtypeStruct((rows, LANES), w_sh.dtype)],
        compiler_params=_params(("parallel",)))(w_sh)


def _ungroup_w_in(d_main, d_dt, d, di, dc, nh, rb=128):
    rows, n_main = d_main.shape
    sw = (n_main + nh) // N_SHARD
    pieces = _w_in_pieces(d, di, dc, nh, sw)
    rb = min(rb, rows)

    def body(main_ref, dt_ref, o_ref):
        for is_dt, dst, k, off, m in pieces:
            src = dt_ref if is_dt else main_ref
            o_ref[k, :, off:off + m] = src[:, dst:dst + m]

    return pl.pallas_call(
        body, name="ungroup_w_in", grid=(rows // rb,),
        in_specs=[pl.BlockSpec((rb, n_main), lambda i: (i, 0)), pl.BlockSpec((rb, LANES), lambda i: (i, 0))],
        out_specs=pl.BlockSpec((N_SHARD, rb, sw), lambda i: (0, i, 0)),
        out_shape=jax.ShapeDtypeStruct((N_SHARD, rows, sw), F32),
        compiler_params=_params(("parallel",)))(d_main, d_dt)


def _inproj(x2, norm_g, w_main, w_dt, tm=1024, tn=1024):
    t, d = x2.shape
    n = w_main.shape[1]
    tm, tn = min(tm, t), min(tn, n)

    def body(x_ref, g_ref, w_ref, wdt_ref, proj_ref, dt_ref, h_ref):
        @pl.when(pl.program_id(1) == 0)
        def _():
            xv = x_ref[...]
            r = lax.rsqrt(jnp.mean(xv * xv, axis=-1, keepdims=True) + EPS)
            h = (xv * r * g_ref[...]).astype(BF16)
            h_ref[...] = h
            dt_ref[...] = jnp.dot(h, wdt_ref[...].astype(BF16), preferred_element_type=F32)

        proj_ref[...] = jnp.dot(h_ref[...], w_ref[...].astype(BF16), preferred_element_type=F32)

    return pl.pallas_call(
        body, name="inproj", grid=(t // tm, n // tn),
        in_specs=[pl.BlockSpec((tm, d), lambda i, j: (i, 0)), pl.BlockSpec((1, d), lambda i, j: (0, 0)),
                  pl.BlockSpec((d, tn), lambda i, j: (0, j)), pl.BlockSpec((d, LANES), lambda i, j: (0, 0))],
        out_specs=[pl.BlockSpec((tm, tn), lambda i, j: (i, j)), pl.BlockSpec((tm, LANES), lambda i, j: (i, 0)),
                   pl.BlockSpec((tm, d), lambda i, j: (i, 0))],
        out_shape=[jax.ShapeDtypeStruct((t, n), F32), jax.ShapeDtypeStruct((t, LANES), F32),
                   jax.ShapeDtypeStruct((t, d), BF16)],
        compiler_params=_params(("parallel", "arbitrary")))(x2, norm_g, w_main, w_dt)


def _conv_w_spec(conv_w, cb, j_axis):
    sw = conv_w.shape[2]
    assert sw % cb == 0, (sw, cb)
    per = sw // cb
    return N_SHARD * per, pl.BlockSpec((1, CONV_WIDTH, cb), lambda *ij: (ij[j_axis] // per, 0, ij[j_axis] % per))


CONV_ROWS = 64
CONV_HALO = 8


def _conv_taps(x_ref, t0, rc):
    if t0 == 0:
        cur = x_ref[0:rc, :]
        row = lax.broadcasted_iota(jnp.int32, cur.shape, 0)
        return [cur] + [_shift_down(cur, j, row) for j in range(1, CONV_WIDTH)]
    ext = x_ref[t0 - CONV_HALO:t0 + rc, :]
    return [ext[CONV_HALO:]] + [pltpu.roll(ext, j, 0)[CONV_HALO:] for j in range(1, CONV_WIDTH)]


def _conv_weights(w_ref):
    return [w_ref[0, CONV_WIDTH - 1 - j:CONV_WIDTH - j, :] for j in range(CONV_WIDTH)]


def _conv_pre(taps, wts, bias):
    acc = bias + taps[0] * wts[0]
    for j in range(1, CONV_WIDTH):
        acc = acc + taps[j] * wts[j]
    return acc


def _conv_fwd(proj, conv_w, conv_b, nb, s, col0, cb=256):
    n_blk, w_spec = _conv_w_spec(conv_w, cb, 1)
    blk0 = _col_block(col0, cb)
    rc = min(CONV_ROWS, s)

    def body(x_ref, w_ref, b_ref, o_ref):
        wts, bias = _conv_weights(w_ref), b_ref[...]
        for t0 in range(0, s, rc):
            acc = _conv_pre(_conv_taps(x_ref, t0, rc), wts, bias)
            o_ref[t0:t0 + rc, :] = acc * _sigmoid(acc)

    return pl.pallas_call(
        body, name="conv_fwd", grid=(nb, n_blk),
        in_specs=[pl.BlockSpec((s, cb), lambda b, j: (b, blk0 + j)), w_spec, pl.BlockSpec((1, cb), lambda b, j: (0, j))],
        out_specs=pl.BlockSpec((s, cb), lambda b, j: (b, j)),
        out_shape=jax.ShapeDtypeStruct((nb * s, n_blk * cb), F32),
        compiler_params=_params(("parallel", "parallel")))(proj, conv_w, conv_b)


def _ssd_consts(di):
    r = lax.broadcasted_iota(jnp.int32, (CHUNK, CHUNK), 0)
    c = lax.broadcasted_iota(jnp.int32, (CHUNK, CHUNK), 1)
    tril = (r >= c).astype(F32)
    head = lax.broadcasted_iota(jnp.int32, (LANES, di), 0)
    chan = lax.broadcasted_iota(jnp.int32, (LANES, di), 1) // HEAD_DIM
    expand = (head == chan).astype(BF16)
    return tril, expand, expand.T


def _expand(v, e, terms=3):
    acc = None
    for _ in range(terms):
        vb = v.astype(BF16)
        part = jnp.dot(vb, e, preferred_element_type=F32)
        acc = part if acc is None else acc + part
        v = v - vb.astype(F32)
    return acc


def _head_sum(t, et, terms=2):
    acc = None
    for _ in range(terms):
        tb = t.astype(BF16)
        part = jnp.dot(tb, et, preferred_element_type=F32)
        acc = part if acc is None else acc + part
        t = t - tb.astype(F32)
    return acc


def _ssd_scalars(dtr_ref, dtb_ref, alog_ref, tri):
    dtpre = dtr_ref[...] + dtb_ref[...]
    dt = _softplus(dtpre)
    a_neg = -jnp.exp(alog_ref[...])
    a_dt = dt * a_neg
    a_cs = jnp.dot(tri, a_dt, precision=HIGHEST, preferred_element_type=F32)
    a_cst = lax.dot_general(a_dt, tri, (((0,), (1,)), ((), ())), precision=HIGHEST, preferred_element_type=F32)
    return dtpre, dt, a_neg, a_cs, a_cst


def _ssd_fwd(proj, xbc, dtraw, dtb, alog, dskx, gn, nb, s, di, z_col0):
    t = nb * s
    nc = s // CHUNK
    hpg = di // HEAD_DIM // SSM_GROUPS
    gw = di // SSM_GROUPS
    gn_w = SSM_GROUPS * D_STATE
    b_blk = _col_block(di, gn_w)
    z_blk = _col_block(z_col0, di)
    L, P, N = CHUNK, HEAD_DIM, D_STATE
    tril, expand, _ = _ssd_consts(di)

    def body(z_ref, x_ref, b_ref, c_ref, dtr_ref, dtb_ref, alog_ref, dskx_ref, gn_ref, tril_ref, e_ref,
             ypre_ref, yan_ref, hp_ref, st_ref, yd_ref, xdt_ref):
        @pl.when(pl.program_id(1) == 0)
        def _():
            st_ref[...] = jnp.zeros_like(st_ref)

        hp_ref[0] = st_ref[...]
        tri = tril_ref[...]
        _, dt, _, a_cs, a_cst = _ssd_scalars(dtr_ref, dtb_ref, alog_ref, tri)
        ev = e_ref[...]
        a_exp = _expand(a_cs, ev)
        xv = x_ref[...]
        xdt = xv * _expand(dt, ev)
        xdt_ref[...] = xdt
        a_last = a_exp[L - 1:L, :]
        xe = xdt * jnp.exp(a_last - a_exp)
        ea = jnp.exp(a_exp)
        e_last = jnp.exp(a_last)
        lower = tri > 0.5
        for g in range(SSM_GROUPS):
            gs = slice(g * gw, (g + 1) * gw)
            bg = b_ref[:, g * N:(g + 1) * N].astype(BF16)
            cg = c_ref[:, g * N:(g + 1) * N].astype(BF16)
            gm = _bdot_nt(cg, bg)
            ht = st_ref[:, gs]
            ch = _bdot(cg, ht)
            for e in range(hpg):
                h = g * hpg + e
                hs = slice(h * P, (h + 1) * P)
                decay = jnp.where(lower, jnp.exp(a_cs[:, h:h + 1] - a_cst[h:h + 1, :]), 0.0)
                yd_ref[:, hs] = _bdot(gm * decay, xdt_ref[:, hs])
            st_ref[:, gs] = ht * e_last[:, gs] + _bdot_tn(bg, xe[:, gs])
            ypre = yd_ref[:, gs] + ea[:, gs] * ch + xv[:, gs] * dskx_ref[:, gs]
            ypre_ref[:, gs] = ypre
            zv = z_ref[:, gs]
            v = ypre * zv * _sigmoid(zv)
            r = lax.rsqrt(jnp.mean(v * v, axis=-1, keepdims=True) + EPS)
            yan_ref[:, gs] = (v * r * gn_ref[:, gs]).astype(BF16)

    row = lambda b, c: b * nc + c
    vec = lambda w: pl.BlockSpec((1, w), lambda b, c: (0, 0))
    return pl.pallas_call(
        body, name="ssd_fwd", grid=(nb, nc),
        in_specs=[pl.BlockSpec((L, di), lambda b, c: (row(b, c), z_blk)),
                  pl.BlockSpec((L, di), lambda b, c: (row(b, c), 0)),
                  pl.BlockSpec((L, gn_w), lambda b, c: (row(b, c), b_blk)),
                  pl.BlockSpec((L, gn_w), lambda b, c: (row(b, c), b_blk + 1)),
                  pl.BlockSpec((L, LANES), lambda b, c: (row(b, c), 0)),
                  vec(LANES), vec(LANES), vec(di), vec(di),
                  pl.BlockSpec((L, L), lambda b, c: (0, 0)),
                  pl.BlockSpec((LANES, di), lambda b, c: (0, 0))],
        out_specs=[pl.BlockSpec((L, di), lambda b, c: (row(b, c), 0)),
                   pl.BlockSpec((L, di), lambda b, c: (row(b, c), 0)),
                   pl.BlockSpec((1, N, di), lambda b, c: (row(b, c), 0, 0))],
        out_shape=[jax.ShapeDtypeStruct((t, di), F32), jax.ShapeDtypeStruct((t, di), BF16),
                   jax.ShapeDtypeStruct((nb * nc, N, di), F32)],
        scratch_shapes=[pltpu.VMEM((N, di), F32), pltpu.VMEM((L, di), F32), pltpu.VMEM((L, di), F32)],
        compiler_params=_params(("parallel", "arbitrary")))(
            proj, xbc, xbc, xbc, dtraw, dtb, alog, dskx, gn, tril, expand)


def _pool_sum(v, g, row, shift):
    s2 = v + shift(v, 1, row)
    s4 = s2 + shift(s2, 2, row)
    s8 = s4 + shift(s4, 4, row)
    s16 = s8 + shift(s8, 8, row)
    return jnp.where(g == 0, s2, jnp.where(g == 1, s4, jnp.where(g == 2, s8, s16)))


def _pool_count(g, row):
    return jnp.minimum(row + 1, jnp.left_shift(2, g)).astype(F32)


def _pool_fwd(proj, mix_w, mix_b, scale, nb, s, col0):
    pgd = mix_w.shape[-1]
    blk0 = _col_block(col0, 2 * pgd)

    def body(uz_ref, w_ref, b_ref, sc_ref, o_ref):
        g = pl.program_id(1)
        u = uz_ref[:, :pgd]
        zp = uz_ref[:, pgd:]
        row = lax.broadcasted_iota(jnp.int32, u.shape, 0)
        pooled = _pool_sum(u, g, row, _shift_down) / _pool_count(g, row) - u
        mixed = _bdot(pooled, w_ref[:, 0].reshape(pgd, pgd)) + b_ref[...]
        o_ref[...] = (mixed * sc_ref[...] * zp * _sigmoid(zp)).astype(BF16)

    return pl.pallas_call(
        body, name="pool_fwd", grid=(nb, N_POOL),
        in_specs=[pl.BlockSpec((s, 2 * pgd), lambda b, g: (b, blk0 + g)),
                  pl.BlockSpec((N_SHARD, 1, pgd // N_SHARD, pgd), lambda b, g: (0, g, 0, 0)),
                  pl.BlockSpec((1, pgd), lambda b, g: (0, g)), pl.BlockSpec((1, pgd), lambda b, g: (0, g))],
        out_specs=pl.BlockSpec((s, pgd), lambda b, g: (b, g)),
        out_shape=jax.ShapeDtypeStruct((nb * s, N_POOL * pgd), BF16),
        compiler_params=_params(("parallel", "parallel")))(proj, mix_w, mix_b, scale)


def _merge_fwd(ya, yb, proj, col0, tm=512):
    t, d = ya.shape
    tm = min(tm, t)
    blk = _col_block(col0, 2 * d)

    def body(ya_ref, yb_ref, g_ref, o_ref):
        o_ref[...] = (_sigmoid(g_ref[:, :d]) * ya_ref[...] + _sigmoid(g_ref[:, d:]) * yb_ref[...]).astype(BF16)

    row = pl.BlockSpec((tm, d), lambda i: (i, 0))
    return pl.pallas_call(
        body, name="merge_fwd", grid=(t // tm,),
        in_specs=[row, row, pl.BlockSpec((tm, 2 * d), lambda i: (i, blk))],
        out_specs=row, out_shape=jax.ShapeDtypeStruct((t, d), BF16),
        compiler_params=_params(("parallel",)))(ya, yb, proj)


def _ple_pre(x2, mo, ple_g, tm=512):
    t, d = x2.shape
    tm = min(tm, t)

    def body(x_ref, mo_ref, g_ref, x1_ref, hn_ref):
        x1 = x_ref[...] + mo_ref[...]
        x1_ref[...] = x1
        r = lax.rsqrt(jnp.mean(x1 * x1, axis=-1, keepdims=True) + EPS)
        hn_ref[...] = (x1 * r * g_ref[...]).astype(BF16)

    row = pl.BlockSpec((tm, d), lambda i: (i, 0))
    return pl.pallas_call(
        body, name="ple_pre", grid=(t // tm,),
        in_specs=[row, row, pl.BlockSpec((1, d), lambda i: (0, 0))],
        out_specs=[row, row],
        out_shape=[jax.ShapeDtypeStruct((t, d), F32), jax.ShapeDtypeStruct((t, d), BF16)],
        compiler_params=_params(("parallel",)))(x2, mo, ple_g)


def _tail(x1, pre, pu, tgt, final_g, tm=512):
    t, d = x1.shape
    tm = min(tm, t)

    def body(x1_ref, pre_ref, pu_ref, tgt_ref, g_ref, dx2_ref, dpre_ref, dpu_ref, loss_ref, dg_ref):
        @pl.when(pl.program_id(0) == 0)
        def _():
            loss_ref[...] = jnp.zeros_like(loss_ref)
            dg_ref[...] = jnp.zeros_like(dg_ref)

        gate = _sigmoid(pre_ref[...])
        pu = pu_ref[...]
        x2 = x1_ref[...] + gate * pu
        r = lax.rsqrt(jnp.mean(x2 * x2, axis=-1, keepdims=True) + EPS)
        xn = x2 * r
        fg = g_ref[...]
        err = xn * fg - tgt_ref[...]
        loss_ref[...] += 0.5 * jnp.sum(jnp.mean(err * err, axis=-1, keepdims=True))
        dy = err * (1.0 / d)
        dg_ref[...] += jnp.sum(dy * xn, axis=0, keepdims=True)
        dxn = dy * fg
        dx2 = r * (dxn - xn * jnp.mean(dxn * xn, axis=-1, keepdims=True))
        dx2_ref[...] = dx2
        dpre_ref[...] = (dx2 * pu * gate * (1.0 - gate)).astype(BF16)
        dpu_ref[...] = (dx2 * gate).astype(BF16)

    row = pl.BlockSpec((tm, d), lambda i: (i, 0))
    return pl.pallas_call(
        body, name="tail", grid=(t // tm,),
        in_specs=[row, row, row, row, pl.BlockSpec((1, d), lambda i: (0, 0))],
        out_specs=[row, row, row, pl.BlockSpec((1, LANES), lambda i: (0, 0)), pl.BlockSpec((1, d), lambda i: (0, 0))],
        out_shape=[jax.ShapeDtypeStruct((t, d), F32)] + [jax.ShapeDtypeStruct((t, d), BF16)] * 2 + [
            jax.ShapeDtypeStruct((1, LANES), F32),
                                                             jax.ShapeDtypeStruct((1, d), F32)],
        compiler_params=_params(("arbitrary",)))(x1, pre, pu, tgt, final_g)


def _rms_bwd(xin, dhs, dres, g, name, tm=512):
    t, d = xin.shape
    tm = min(tm, t)
    n_dh = len(dhs)

    def body(*refs):
        x_ref, dh_refs, dres_ref, g_ref, dx_ref, dg_ref = refs[0], refs[1:1 + n_dh], *refs[1 + n_dh:]

        @pl.when(pl.program_id(0) == 0)
        def _():
            dg_ref[...] = jnp.zeros_like(dg_ref)

        xv = x_ref[...]
        dh = dh_refs[0][...]
        for ref in dh_refs[1:]:
            dh = dh + ref[...]
        r = lax.rsqrt(jnp.mean(xv * xv, axis=-1, keepdims=True) + EPS)
        xn = xv * r
        dg_ref[...] += jnp.sum(dh * xn, axis=0, keepdims=True)
        dd = dh * g_ref[...]
        dx_ref[...] = dres_ref[...] + r * (dd - xn * jnp.mean(dd * xn, axis=-1, keepdims=True))

    row = pl.BlockSpec((tm, d), lambda i: (i, 0))
    vec = pl.BlockSpec((1, d), lambda i: (0, 0))
    return pl.pallas_call(
        body, name=name, grid=(t // tm,),
        in_specs=[row] * (2 + n_dh) + [vec],
        out_specs=[row, vec],
        out_shape=[jax.ShapeDtypeStruct((t, d), F32), jax.ShapeDtypeStruct((1, d), F32)],
        compiler_params=_params(("arbitrary",)))(xin, *dhs, dres, g)


def _merge_bwd(dm, ya, yb, proj, col0, n_cols, tm=512):
    t, d = ya.shape
    tm = min(tm, t)
    blk = _col_block(col0, 2 * d)

    def body(dm_ref, ya_ref, yb_ref, g_ref, dya_ref, dyb_ref, dg_ref):
        dm_v = dm_ref[...]
        sa = _sigmoid(g_ref[:, :d])
        sb = _sigmoid(g_ref[:, d:])
        dya_ref[...] = (dm_v * sa).astype(BF16)
        dyb_ref[...] = (dm_v * sb).astype(BF16)
        dg_ref[:, :d] = (dm_v * ya_ref[...] * sa * (1.0 - sa)).astype(BF16)
        dg_ref[:, d:] = (dm_v * yb_ref[...] * sb * (1.0 - sb)).astype(BF16)

    row = pl.BlockSpec((tm, d), lambda i: (i, 0))
    gspec = pl.BlockSpec((tm, 2 * d), lambda i: (i, blk))
    return pl.pallas_call(
        body, name="merge_bwd", grid=(t // tm,),
        in_specs=[row, row, row, gspec],
        out_specs=[row, row, gspec],
        out_shape=[jax.ShapeDtypeStruct((t, d), BF16), jax.ShapeDtypeStruct((t, d), BF16),
                   jax.ShapeDtypeStruct((t, n_cols), BF16)],
        compiler_params=_params(("parallel",)))(dm, ya, yb, proj)


def _pool_bwd(proj, dyb, dproj, mix_w, mix_b, scale, nb, s, col0):
    pgd = mix_w.shape[-1]
    blk0 = _col_block(col0, 2 * pgd)

    def body(uz_ref, dy_ref, _, w_ref, b_ref, sc_ref, duz_ref, dw_ref, db_ref, dsc_ref):
        g = pl.program_id(0)

        @pl.when(pl.program_id(1) == 0)
        def _():
            dw_ref[...] = jnp.zeros_like(dw_ref)
            db_ref[...] = jnp.zeros_like(db_ref)
            dsc_ref[...] = jnp.zeros_like(dsc_ref)

        u = uz_ref[:, :pgd]
        zp = uz_ref[:, pgd:]
        row = lax.broadcasted_iota(jnp.int32, u.shape, 0)
        cnt = _pool_count(g, row)
        pooled = _pool_sum(u, g, row, _shift_down) / cnt - u
        wv = w_ref[:, 0].reshape(pgd, pgd)
        mixed = _bdot(pooled, wv) + b_ref[...]
        sg = _sigmoid(zp)
        sz = zp * sg
        dy = dy_ref[...]
        sc = sc_ref[...]
        dsc_ref[...] += jnp.sum(dy * mixed * sz, axis=0, keepdims=True)
        dmixed = dy * sc * sz
        db_ref[...] += jnp.sum(dmixed, axis=0, keepdims=True)
        dw_ref[:, 0] += _bdot_tn(pooled, dmixed).reshape(N_SHARD, pgd // N_SHARD, pgd)
        dpooled = _bdot_nt(dmixed, wv)
        duz_ref[:, :pgd] = (_pool_sum(dpooled / cnt, g, row, _shift_up) - dpooled).astype(BF16)
        duz_ref[:, pgd:] = (dy * mixed * sc * sg * (1.0 + zp * (1.0 - sg))).astype(BF16)

    uz = pl.BlockSpec((s, 2 * pgd), lambda g, b: (b, blk0 + g))
    vec = pl.BlockSpec((1, pgd), lambda g, b: (0, g))
    wspec = pl.BlockSpec((N_SHARD, 1, pgd // N_SHARD, pgd), lambda g, b: (0, g, 0, 0))
    return pl.pallas_call(
        body, name="pool_bwd", grid=(N_POOL, nb),
        in_specs=[uz, pl.BlockSpec((s, pgd), lambda g, b: (b, g)), pl.BlockSpec(memory_space=pl.ANY), wspec, vec, vec],
        out_specs=[uz, wspec, vec, vec],
        out_shape=[jax.ShapeDtypeStruct(dproj.shape, dproj.dtype), jax.ShapeDtypeStruct(mix_w.shape, F32),
                   jax.ShapeDtypeStruct(mix_b.shape, F32), jax.ShapeDtypeStruct(scale.shape, F32)],
        input_output_aliases={2: 0},
        compiler_params=_params(("parallel", "arbitrary")))(proj, dyb, dproj, mix_w, mix_b, scale)


def _ssd_bwd(dyan, ypre, proj, xbc, dtraw, hp, dproj, dtb, alog, dskx, gn, nb, s, di, z_col0):
    t = nb * s
    nc = s // CHUNK
    hpg = di // HEAD_DIM // SSM_GROUPS
    gw = di // SSM_GROUPS
    gn_w = SSM_GROUPS * D_STATE
    dc = di + 2 * gn_w
    b_blk = _col_block(di, gn_w)
    z_blk = _col_block(z_col0, di)
    L, P, N = CHUNK, HEAD_DIM, D_STATE
    tril, expand, expand_t = _ssd_consts(di)

    def body(dy_ref, ypre_ref, z_ref, x_ref, b_ref, c_ref, dtr_ref, hp_ref, _, dtb_ref, alog_ref, dskx_ref, gn_ref,
             tril_ref, e_ref, et_ref, dz_ref, ddt_ref, dxbc_ref, dgn_ref, ddsk_ref, dalog_ref, ddtb_ref,
             dst_ref, dyp_ref, xdt_ref, dxm_ref, t1_ref, t2_ref, t3_ref, aux_ref):
        @pl.when((pl.program_id(0) == 0) & (pl.program_id(1) == 0))
        def _():
            dgn_ref[...] = jnp.zeros_like(dgn_ref)
            ddsk_ref[...] = jnp.zeros_like(ddsk_ref)
            dalog_ref[...] = jnp.zeros_like(dalog_ref)
            ddtb_ref[...] = jnp.zeros_like(ddtb_ref)

        @pl.when(pl.program_id(1) == 0)
        def _():
            dst_ref[...] = jnp.zeros_like(dst_ref)

        tri = tril_ref[...]
        dtpre, dt, a_neg, a_cs, a_cst = _ssd_scalars(dtr_ref, dtb_ref, alog_ref, tri)
        ev = e_ref[...]
        a_exp = _expand(a_cs, ev)
        dt_exp = _expand(dt, ev)
        xv = x_ref[...]
        xdt = xv * dt_exp
        xdt_ref[...] = xdt
        a_last = a_exp[L - 1:L, :]
        dte = jnp.exp(a_last - a_exp)
        xe = xdt * dte
        ea = jnp.exp(a_exp)
        e_last = jnp.exp(a_last)
        lower = tri > 0.5
        aux_ref[...] = jnp.zeros_like(aux_ref)
        for g in range(SSM_GROUPS):
            gs = slice(g * gw, (g + 1) * gw)
            zv = z_ref[:, gs]
            yp = ypre_ref[:, gs]
            sg = _sigmoid(zv)
            sz = zv * sg
            vg = yp * sz
            r = lax.rsqrt(jnp.mean(vg * vg, axis=-1, keepdims=True) + EPS)
            vn = vg * r
            dyg = dy_ref[:, gs]
            dgn_ref[:, gs] += jnp.sum(dyg * vn, axis=0, keepdims=True)
            dvn = dyg * gn_ref[:, gs]
            dv = r * (dvn - vn * jnp.mean(dvn * vn, axis=-1, keepdims=True))
            dy = dv * sz
            dyp_ref[:, gs] = dy
            dz_ref[:, gs] = (dv * yp * sg * (1.0 + zv * (1.0 - sg))).astype(BF16)
            bg = b_ref[:, g * N:(g + 1) * N].astype(BF16)
            cg = c_ref[:, g * N:(g + 1) * N].astype(BF16)
            gm = _bdot_nt(cg, bg)
            ht = hp_ref[0, :, gs]
            dht = dst_ref[:, gs]
            bds = _bdot(bg, dht)
            dye = dy * ea[:, gs]
            xe_g = xe[:, gs]
            dcg = _bdot_nt(dye, ht)
            dbg = _bdot_nt(xe_g, dht)
            dst_ref[:, gs] = e_last[:, gs] * dht + _bdot_tn(cg, dye)
            dgm = jnp.zeros((L, L), F32)
            for e in range(hpg):
                h = g * hpg + e
                hs = slice(h * P, (h + 1) * P)
                decay = jnp.where(lower, jnp.exp(a_cs[:, h:h + 1] - a_cst[h:h + 1, :]), 0.0)
                dy_h = dyp_ref[:, hs]
                dgm = dgm + _bdot_nt(dy_h, xdt_ref[:, hs]) * decay
                dxm_ref[:, hs] = _bdot_tn(gm * decay, dy_h)
            dxbc_ref[:, di + g * N:di + (g + 1) * N] = dbg + _bdot_tn(dgm, cg)
            dxbc_ref[:, di + gn_w + g * N:di + gn_w + (g + 1) * N] = dcg + _bdot(dgm, bg)
            dxm = dxm_ref[:, gs]
            x_g = xv[:, gs]
            dskx = dskx_ref[:, gs]
            xeb = xe_g * bds
            dxdt = dxm + dte[:, gs] * bds
            dxbc_ref[:, gs] = dxdt * dt_exp[:, gs] + dy * dskx
            each = ea[:, gs] * _bdot(cg, ht)
            y_diag = yp - x_g * dskx - each
            rnd = lambda v: v.astype(BF16).astype(F32)
            t1_ref[:, gs] = rnd(dy) * y_diag + dy * each - rnd(xdt[:, gs]) * dxm - xeb
            t2_ref[:, gs] = xeb
            t3_ref[:, gs] = dxdt * x_g
            aux_ref[0:1, gs] = jnp.sum(dht * ht, axis=0, keepdims=True)
            aux_ref[1:2, gs] = jnp.sum(dy * x_g, axis=0, keepdims=True)
        etv = et_ref[...]
        w_end = _head_sum(t2_ref[...], etv)
        aux = _head_sum(aux_ref[...], etv)
        rowi = lax.broadcasted_iota(jnp.int32, (L, LANES), 0)
        end = jnp.sum(w_end, axis=0, keepdims=True) + aux[0:1, :] * jnp.exp(a_cs[L - 1:L, :])
        da = _head_sum(t1_ref[...], etv, terms=3) + jnp.where(rowi == L - 1, end, 0.0)
        rc = lax.dot_general(tri, da, (((0,), (0,)), ((), ())), precision=HIGHEST, preferred_element_type=F32)
        ddt = a_neg * rc + _head_sum(t3_ref[...], etv)
        ddtraw = ddt * _sigmoid(dtpre)
        ddt_ref[...] = ddtraw.astype(BF16)
        ddtb_ref[...] += jnp.sum(ddtraw, axis=0, keepdims=True)
        dalog_ref[...] += jnp.sum(dt * rc, axis=0, keepdims=True) * a_neg
        ddsk_ref[...] += aux[1:2, :]

    row = lambda b, c: b * nc + (nc - 1 - c)
    full = lambda w: pl.BlockSpec((L, w), lambda b, c: (row(b, c), 0))
    zspec = pl.BlockSpec((L, di), lambda b, c: (row(b, c), z_blk))
    vec = lambda w: pl.BlockSpec((1, w), lambda b, c: (0, 0))
    slab = lambda shape: pltpu.VMEM(shape, F32)
    return pl.pallas_call(
        body, name="ssd_bwd", grid=(nb, nc),
        in_specs=[full(di), full(di), zspec, full(di),
                  pl.BlockSpec((L, gn_w), lambda b, c: (row(b, c), b_blk)),
                  pl.BlockSpec((L, gn_w), lambda b, c: (row(b, c), b_blk + 1)),
                  full(LANES),
                  pl.BlockSpec((1, N, di), lambda b, c: (row(b, c), 0, 0)),
                  pl.BlockSpec(memory_space=pl.ANY),
                  vec(LANES), vec(LANES), vec(di), vec(di),
                  pl.BlockSpec((L, L), lambda b, c: (0, 0)),
                  pl.BlockSpec((LANES, di), lambda b, c: (0, 0)),
                  pl.BlockSpec((di, LANES), lambda b, c: (0, 0))],
        out_specs=[zspec, full(LANES), full(dc), vec(di), vec(LANES), vec(LANES), vec(LANES)],
        out_shape=[jax.ShapeDtypeStruct(dproj.shape, dproj.dtype), jax.ShapeDtypeStruct((t, LANES), BF16),
                   jax.ShapeDtypeStruct((t, dc), F32), jax.ShapeDtypeStruct((1, di), F32),
                   jax.ShapeDtypeStruct((1, LANES), F32), jax.ShapeDtypeStruct((1, LANES), F32),
                   jax.ShapeDtypeStruct((1, LANES), F32)],
        scratch_shapes=[slab((N, di)), slab((L, di)), slab((L, di)), slab((L, di)), slab((L, di)), slab((L, di)),
                        slab((L, di)), slab((8, di))],
        input_output_aliases={8: 0},
        compiler_params=_params(("arbitrary", "arbitrary")))(
            dyan, ypre, proj, xbc, xbc, xbc, dtraw, hp, dproj, dtb, alog, dskx, gn, tril, expand, expand_t)


def _conv_bwd(proj, dxbc, dproj, conv_w, conv_b, nb, s, col0, cb=256):
    n_blk, w_spec = _conv_w_spec(conv_w, cb, 0)
    blk0 = _col_block(col0, cb)
    rc = min(CONV_ROWS, s)

    def body(x_ref, dy_ref, _, w_ref, b_ref, dx_ref, dw_ref, db_ref, dacc_ref):
        @pl.when(pl.program_id(1) == 0)
        def _():
            dw_ref[...] = jnp.zeros_like(dw_ref)
            db_ref[...] = jnp.zeros_like(db_ref)

        wts, bias = _conv_weights(w_ref), b_ref[...]
        fold = lambda v: v.reshape(rc // 8, 8, cb).sum(axis=0)
        db8 = jnp.zeros((8, cb), F32)
        dw8 = [jnp.zeros((8, cb), F32) for _ in range(CONV_WIDTH)]
        for t0 in range(0, s, rc):
            taps = _conv_taps(x_ref, t0, rc)
            acc = _conv_pre(taps, wts, bias)
            sg = _sigmoid(acc)
            dacc = dy_ref[t0:t0 + rc, :] * sg * (1.0 + acc * (1.0 - sg))
            dacc_ref[t0:t0 + rc, :] = dacc
            db8 = db8 + fold(dacc)
            dw8 = [dw8[j] + fold(dacc * taps[j]) for j in range(CONV_WIDTH)]
        db_ref[...] += jnp.sum(db8, axis=0, keepdims=True)
        for j in range(CONV_WIDTH):
            dw_ref[0, CONV_WIDTH - 1 - j:CONV_WIDTH - j, :] += jnp.sum(dw8[j], axis=0, keepdims=True)
        for t0 in range(0, s, rc):
            if t0 + rc < s:
                n = rc + CONV_HALO
                win = dacc_ref[t0:t0 + n, :]
                ups = [win[:rc]] + [pltpu.roll(win, n - j, 0)[:rc] for j in range(1, CONV_WIDTH)]
            else:
                cur = dacc_ref[t0:t0 + rc, :]
                row = lax.broadcasted_iota(jnp.int32, cur.shape, 0)
                ups = [cur] + [_shift_up(cur, j, row) for j in range(1, CONV_WIDTH)]
            dx = ups[0] * wts[0]
            for j in range(1, CONV_WIDTH):
                dx = dx + ups[j] * wts[j]
            dx_ref[t0:t0 + rc, :] = dx.astype(BF16)

    return pl.pallas_call(
        body, name="conv_bwd", grid=(n_blk, nb),
        in_specs=[pl.BlockSpec((s, cb), lambda j, b: (b, blk0 + j)), pl.BlockSpec((s, cb), lambda j, b: (b, j)),
                  pl.BlockSpec(memory_space=pl.ANY), w_spec, pl.BlockSpec((1, cb), lambda j, b: (0, j))],
        out_specs=[pl.BlockSpec((s, cb), lambda j, b: (b, blk0 + j)), w_spec, pl.BlockSpec((1, cb), lambda j, b: (0, j))],
        out_shape=[jax.ShapeDtypeStruct(dproj.shape, dproj.dtype), jax.ShapeDtypeStruct(conv_w.shape, F32),
                   jax.ShapeDtypeStruct(conv_b.shape, F32)],
        scratch_shapes=[pltpu.VMEM((s, cb), F32)],
        input_output_aliases={2: 0},
        compiler_params=_params(("parallel", "arbitrary")))(proj, dxbc, dproj, conv_w, conv_b)


def _local_step(x, p, tgt, wg, small, rest_weights, early_grads, w_in_grad):
    nb, s, d = x.shape
    t = nb * s
    gn_w = SSM_GROUPS * D_STATE
    dc = N_SHARD * wg["conv_w"].shape[2]
    di = dc - 2 * gn_w
    nh = di // HEAD_DIM
    pgd = d // N_POOL
    x2 = x.reshape(t, d)
    p2 = p.reshape(t, p.shape[-1])
    tgt2 = tgt.reshape(t, d)

    w_main, w_dt = _regroup_w_in(wg["w_in"], d, di, dc, nh)
    c_g, c_z, c_xbc, c_uz = 0, 2 * d, 2 * d + di, 2 * d + di + dc
    n_main = w_main.shape[1]

    pad_h = lambda v: jnp.pad(v.reshape(1, nh).astype(F32), ((0, 0), (0, LANES - nh)))
    dtb, alog = pad_h(small["dt_bias"]), pad_h(small["a_log"])
    dskx = jnp.repeat(small["d_skip"].reshape(1, nh).astype(F32), HEAD_DIM, axis=1)
    vec = lambda v: v.reshape(1, -1).astype(F32)
    norm_g, gn, conv_b = vec(small["norm_g"]), vec(small["gnorm_g"]), vec(small["conv_b"])
    mix_b, scale = vec(small["pool_mix_b"]), vec(small["pool_scale"])
    ple_g, final_g = vec(small["ple_norm_g"]), vec(small["final_g"])
    conv_w = wg["conv_w"]

    proj, dtraw, h = _inproj(x2, norm_g, w_main, w_dt)
    xbc = _conv_fwd(proj, conv_w, conv_b, nb, s, c_xbc)
    ypre, yan, hp = _ssd_fwd(proj, xbc, dtraw, dtb, alog, dskx, gn, nb, s, di, c_z)
    wr = rest_weights(yan)
    mix_w = wr["pool_mix_w"].reshape(N_SHARD, N_POOL, pgd // N_SHARD, pgd)
    rows = lambda v: v.reshape(-1, v.shape[-1])
    wa, wb, wo, wpg = rows(wr["w_branch_a"]), rows(wr["w_branch_b"]), rows(wr["w_out"]), rows(wr["w_ple_gate"])
    wup = wr["w_ple_up"]
    ybp = _pool_fwd(proj, mix_w, mix_b, scale, nb, s, c_uz)
    ya = _mm(yan, wa, "branch_a")
    yb = _mm(ybp, wb, "branch_b")
    merged = _merge_fwd(ya, yb, proj, c_g)
    mo = _mm(merged, wo, "out_proj")
    x1, hn = _ple_pre(x2, mo, ple_g)
    pre = _mm(hn, wpg, "ple_gate")
    pu = _mm(p2, wup, "ple_up")

    dx2, dpre, dpu, loss, d_final_g = _tail(x1, pre, pu, tgt2, final_g)
    d_wpg = _mm_tn(hn, dpre, "d_w_ple_gate")
    d_wup = _mm_tn(p2, dpu, "d_w_ple_up", tn=wup.shape[-1], col_blocks=True)
    dhn = _mm_nt(dpre, wpg, "d_hn")
    dx1, d_ple_g = _rms_bwd(x1, [dhn], dx2, ple_g, "ple_bwd")
    d_wo = _mm_tn(merged, dx1, "d_w_out")
    dm = _mm_nt(dx1, wo, "d_merged")
    dya, dyb, dproj = _merge_bwd(dm, ya, yb, proj, c_g, n_main)
    d_wa = _mm_tn(yan, dya, "d_w_branch_a")
    d_wb = _mm_tn(ybp, dyb, "d_w_branch_b")
    dyan = _mm_nt(dya, wa, "d_y_a")
    dybp = _mm_nt(dyb, wb, "d_y_b")
    dproj, d_mix_w, d_mix_b, d_scale = _pool_bwd(proj, dybp, dproj, mix_w, mix_b, scale, nb, s, c_uz)
    shard_major = lambda v: v.reshape(N_SHARD, v.shape[0] // N_SHARD, v.shape[1])
    early = dict(pool_mix_w=d_mix_w.reshape(N_SHARD, pgd, pgd), w_branch_a=shard_major(d_wa),
                 w_branch_b=shard_major(d_wb), w_out=shard_major(d_wo), w_ple_gate=shard_major(d_wpg),
                 w_ple_up=d_wup)
    token = early_grads(early)
    dproj, ddt, dxbc, d_gn, d_dsk, d_alog, d_dtb = _ssd_bwd(
        dyan, ypre, proj, xbc, dtraw, hp, dproj, dtb + token[0:1, 0:1], alog, dskx, gn, nb, s, di, c_z)
    dproj, d_conv_w, d_conv_b = _conv_bwd(proj, dxbc, dproj, conv_w, conv_b, nb, s, c_xbc)
    d_wmain = _mm_tn(h, dproj, "d_w_in")
    d_wdt = _mm_tn(h, ddt, "d_w_dt")
    d_w_in = _ungroup_w_in(d_wmain, d_wdt, d, di, dc, nh)
    token = w_in_grad(d_w_in)
    dh_main = _mm_nt(dproj, w_main, "d_h", after=token)
    dh_dt = _mm_nt(ddt, w_dt, "d_h_dt")
    gx, d_norm_g = _rms_bwd(x2, [dh_main, dh_dt], dx1, norm_g, "in_bwd")

    grads = dict(norm_g=d_norm_g, w_in=d_w_in, conv_w=d_conv_w, conv_b=d_conv_b, dt_bias=d_dtb[:, :nh],
                 a_log=d_alog[:, :nh], d_skip=d_dsk[:, :nh], gnorm_g=d_gn, pool_mix_b=d_mix_b, pool_scale=d_scale,
                 ple_norm_g=d_ple_g, final_g=d_final_g, **early)
    return loss[0, 0], gx.reshape(nb, s, d), grads


def _place():
    return lax.axis_index("x"), lax.axis_index("y"), lax.axis_index("c")


def _other_chips(x, y):
    return [(1 - x, y), (x, 1 - y), (1 - x, 1 - y)]


def _halves(c, rows, align):
    rh = rows // 2
    assert rows % 2 == 0 and rh % align == 0, (rows, align)
    return (pl.ds(pl.multiple_of(c * rh, align), rh), pl.ds(pl.multiple_of((1 - c) * rh, align), rh))


HBM = pl.BlockSpec(memory_space=pl.ANY)


def _into_slot(w2, k, dtype, name):
    rows, cols = w2.shape
    rb = _tile(rows, 256)

    def body(k_ref, w_ref, o_ref):
        o_ref[0] = w_ref[...].astype(dtype)

    return pl.pallas_call(
        body, name=name,
        grid_spec=pltpu.PrefetchScalarGridSpec(
            num_scalar_prefetch=1, grid=(rows // rb,),
            in_specs=[pl.BlockSpec((rb, cols), lambda i, k_ref: (i, 0))],
            out_specs=pl.BlockSpec((1, rb, cols), lambda i, k_ref: (k_ref[0], i, 0))),
        out_shape=jax.ShapeDtypeStruct((N_SHARD, rows, cols), dtype),
        compiler_params=_params(("parallel",)))(k.reshape(1), w2)


def _gather_weights(split, whole):
    n_split, n_all = len(split), len(split) + len(whole)

    def body(*refs):
        bufs = refs[n_all:2 * n_all]
        send_sems, recv_sems = refs[2 * n_all:]
        x, y, c = _place()
        k = 2 * x + y
        chips = _other_chips(x, y)

        def copy(idx, block, to):
            return pltpu.make_async_remote_copy(src_ref=block, dst_ref=block, send_sem=send_sems.at[idx],
                                                recv_sem=recv_sems.at[idx], device_id=to, device_id_type=MESH)

        def block(i, shard, rows):
            return bufs[i].at[shard, rows] if i < n_split else bufs[i].at[shard]

        def sem(i, j):
            return 6 * i + j if i < n_split else 6 * n_split + 3 * (i - n_split) + j

        started = []
        for i in range(n_all):
            mine, _ = _halves(c, bufs[i].shape[1], 16) if i < n_split else (None, None)
            for j, (px, py) in enumerate(chips):
                started.append(copy(sem(i, j), block(i, k, mine), (px, py, c)))
                started[-1].start()
        for i in range(n_all):
            mine, _ = _halves(c, bufs[i].shape[1], 16) if i < n_split else (None, None)
            for j, (px, py) in enumerate(chips):
                landed = block(i, 2 * px + py, mine)
                copy(sem(i, j), landed, (px, py, c)).wait_recv()
                if i < n_split:
                    started.append(copy(sem(i, 3 + j), landed, (x, y, 1 - c)))
                    started[-1].start()
        for i in range(n_split):
            _, theirs = _halves(c, bufs[i].shape[1], 16)
            for j, (px, py) in enumerate(chips):
                copy(sem(i, 3 + j), block(i, 2 * px + py, theirs), (x, y, 1 - c)).wait_recv()
        for cp in started:
            cp.wait_send()

    arrays = list(split) + list(whole)
    n_sem = 6 * n_split + 3 * len(whole)
    return pl.pallas_call(
        body, name="gather_weights",
        in_specs=[HBM] * n_all, out_specs=[HBM] * n_all,
        out_shape=[jax.ShapeDtypeStruct(a.shape, a.dtype) for a in arrays],
        input_output_aliases={i: i for i in range(n_all)},
        scratch_shapes=[pltpu.SemaphoreType.DMA((n_sem,)), pltpu.SemaphoreType.DMA((n_sem,))],
    )(*arrays)


def _swap_halves(gs):
    n = len(gs)

    def body(*refs):
        ins, outs, send_sems, recv_sems = refs[:n], refs[n:2 * n], refs[2 * n], refs[2 * n + 1]
        x, y, c = _place()
        copies = []
        for i in range(n):
            _, theirs = _halves(c, gs[i].shape[1], 8)
            cp = pltpu.make_async_remote_copy(src_ref=ins[i].at[:, theirs], dst_ref=outs[i], send_sem=send_sems.at[i],
                                              recv_sem=recv_sems.at[i], device_id=(x, y, 1 - c), device_id_type=MESH)
            cp.start()
            copies.append(cp)
        for cp in copies:
            cp.wait()

    return pl.pallas_call(
        body, name="swap_halves", in_specs=[HBM] * n, out_specs=[HBM] * n,
        out_shape=[jax.ShapeDtypeStruct((g.shape[0], g.shape[1] // 2, g.shape[2]), g.dtype) for g in gs],
        scratch_shapes=[pltpu.SemaphoreType.DMA((n,)), pltpu.SemaphoreType.DMA((n,))],
    )(*gs)


def _scatter_chips(ps):
    n = len(ps)

    def body(*refs):
        ins, outs, send_sems, recv_sems = refs[:n], refs[n:2 * n], refs[2 * n], refs[2 * n + 1]
        x, y, c = _place()
        copies = []
        for i in range(n):
            for j, (px, py) in enumerate(_other_chips(x, y)):
                cp = pltpu.make_async_remote_copy(src_ref=ins[i].at[2 * px + py], dst_ref=outs[i].at[j],
                                                  send_sem=send_sems.at[3 * i + j], recv_sem=recv_sems.at[3 * i + j],
                                                  device_id=(px, py, c), device_id_type=MESH)
                cp.start()
                copies.append(cp)
        for cp in copies:
            cp.wait()

    return pl.pallas_call(
        body, name="scatter_chips", in_specs=[HBM] * n, out_specs=[HBM] * n,
        out_shape=[jax.ShapeDtypeStruct((N_SHARD - 1,) + v.shape[1:], v.dtype) for v in ps],
        scratch_shapes=[pltpu.SemaphoreType.DMA((3 * n,)), pltpu.SemaphoreType.DMA((3 * n,))],
    )(*ps)


def _join_halves(vs):
    n = len(vs)

    def body(*refs):
        bufs, send_sems, recv_sems = refs[n:2 * n], refs[2 * n], refs[2 * n + 1]
        x, y, c = _place()

        def copy(i, rows):
            return pltpu.make_async_remote_copy(src_ref=bufs[i].at[rows], dst_ref=bufs[i].at[rows],
                                                send_sem=send_sems.at[i], recv_sem=recv_sems.at[i],
                                                device_id=(x, y, 1 - c), device_id_type=MESH)

        halves = [_halves(c, bufs[i].shape[0], 8) for i in range(n)]
        sends = [copy(i, halves[i][0]) for i in range(n)]
        for cp in sends:
            cp.start()
        for i in range(n):
            copy(i, halves[i][1]).wait_recv()
        for cp in sends:
            cp.wait_send()

    return pl.pallas_call(
        body, name="join_halves", in_specs=[HBM] * n, out_specs=[HBM] * n,
        out_shape=[jax.ShapeDtypeStruct(v.shape, v.dtype) for v in vs],
        input_output_aliases={i: i for i in range(n)},
        scratch_shapes=[pltpu.SemaphoreType.DMA((n,)), pltpu.SemaphoreType.DMA((n,))],
    )(*vs)


SEM = pl.BlockSpec(memory_space=pltpu.SEMAPHORE)
IN_HBM = pl.BlockSpec(memory_space=pltpu.HBM)
SPLIT_EFFECT = pltpu.SideEffectType.DATAFLOW_SIDE_EFFECTING


def _split_copies(plan, refs, send_sems, recv_sems):
    pairs = []
    for idx, (src, dst, landing, to) in enumerate(plan(refs)):
        mk = lambda d: pltpu.make_async_remote_copy(src_ref=src, dst_ref=d, send_sem=send_sems.at[idx],
                                                    recv_sem=recv_sems.at[idx], device_id=to, device_id_type=MESH)
        pairs.append((mk(dst), mk(landing)))
    return pairs


def _split_start(name, bufs, after, plan, n_copies):
    n = len(bufs)

    def body(*refs):
        send_sems, recv_sems, token = refs[n + 1], refs[n + 2], refs[-1]
        for send, _ in _split_copies(plan, refs[:n], send_sems, recv_sems):
            send.start()
        token[...] = jnp.zeros_like(token)

    sems = pltpu.SemaphoreType.DMA((n_copies,))
    out = pl.pallas_call(
        body, name=name,
        in_specs=[IN_HBM] * n + [HBM],
        out_specs=[SEM, SEM] + [IN_HBM] * n + [pl.BlockSpec(memory_space=pltpu.VMEM)],
        out_shape=[sems, sems] + [pltpu.HBM(b.shape, b.dtype) for b in bufs] + [jax.ShapeDtypeStruct((8, LANES), F32)],
        input_output_aliases={i: 2 + i for i in range(n)},
        compiler_params=pltpu.CompilerParams(has_side_effects=SPLIT_EFFECT),
    )(*[pltpu.with_memory_space_constraint(b, pltpu.HBM) for b in bufs], after)
    return out[0], out[1], out[2:2 + n], out[-1]


def _split_wait(name, bufs, send_sems, recv_sems, after, plan):
    n = len(bufs)

    def body(*refs):
        for send, recv in _split_copies(plan, refs[:n], refs[n], refs[n + 1]):
            send.wait_send()
            recv.wait_recv()

    return pl.pallas_call(
        body, name=name,
        in_specs=[IN_HBM] * n + [SEM, SEM, HBM],
        out_specs=[IN_HBM] * n,
        out_shape=[pltpu.HBM(b.shape, b.dtype) for b in bufs],
        input_output_aliases={i: i for i in range(n)},
        compiler_params=pltpu.CompilerParams(has_side_effects=SPLIT_EFFECT),
    )(*bufs, send_sems, recv_sems, after)


def _gather_plan(n):
    def plan(refs):
        x, y, c = _place()
        k = 2 * x + y
        return [(refs[i].at[k], refs[i].at[k], refs[i].at[2 * px + py], (px, py, c))
                for i in range(n) for px, py in _other_chips(x, y)]
    return plan


def _scatter_plan(n):
    def plan(refs):
        x, y, c = _place()
        return [(refs[i].at[2 * px + py], refs[n + i].at[j], refs[n + i].at[j], (px, py, c))
                for i in range(n) for j, (px, py) in enumerate(_other_chips(x, y))]
    return plan


def _swap_sibling(vs):
    n = len(vs)

    def body(*refs):
        ins, outs, send_sems, recv_sems = refs[:n], refs[n:2 * n], refs[2 * n], refs[2 * n + 1]
        x, y, c = _place()
        copies = [pltpu.make_async_remote_copy(src_ref=ins[i], dst_ref=outs[i], send_sem=send_sems.at[i],
                                               recv_sem=recv_sems.at[i], device_id=(x, y, 1 - c), device_id_type=MESH)
                  for i in range(n)]
        for cp in copies:
            cp.start()
        for cp in copies:
            cp.wait()

    return pl.pallas_call(
        body, name="swap_sibling", in_specs=[HBM] * n, out_specs=[HBM] * n,
        out_shape=[jax.ShapeDtypeStruct(v.shape, v.dtype) for v in vs],
        scratch_shapes=[pltpu.SemaphoreType.DMA((n,)), pltpu.SemaphoreType.DMA((n,))],
    )(*vs)


def _to_bf16(g, name):
    _, rows, cols = g.shape
    rb = _tile(rows, 256)

    def body(g_ref, o_ref):
        o_ref[...] = g_ref[...].astype(BF16)

    spec = pl.BlockSpec((1, rb, cols), lambda j, i: (j, i, 0))
    return pl.pallas_call(
        body, name=name, grid=(N_SHARD, rows // rb), in_specs=[spec], out_specs=spec,
        out_shape=jax.ShapeDtypeStruct(g.shape, BF16),
        compiler_params=_params(("parallel", "parallel")))(g)


def _add_landed(g, landed, k, name):
    _, rows, cols = g.shape
    rb = _tile(rows, 256)

    def body(k_ref, g_ref, l_ref, o_ref):
        o_ref[...] = g_ref[0] + l_ref[0].astype(F32) + l_ref[1].astype(F32) + l_ref[2].astype(F32)

    return pl.pallas_call(
        body, name=name,
        grid_spec=pltpu.PrefetchScalarGridSpec(
            num_scalar_prefetch=1, grid=(rows // rb,),
            in_specs=[pl.BlockSpec((1, rb, cols), lambda i, k_ref: (k_ref[0], i, 0)),
                      pl.BlockSpec((N_SHARD - 1, rb, cols), lambda i, k_ref: (0, i, 0))],
            out_specs=pl.BlockSpec((rb, cols), lambda i, k_ref: (i, 0))),
        out_shape=jax.ShapeDtypeStruct((rows, cols), F32),
        compiler_params=_params(("parallel",)))(k.reshape(1), g, landed)


def _allreduce_small(v):
    rows = v.shape[0]

    def body(v_ref, o_ref, buf_ref, send_sems, recv_sems):
        x, y, c = _place()
        me = 4 * x + 2 * y + c
        buf_ref[me] = v_ref[...]
        copies = []
        for rel in range(1, 8):
            peer = (x ^ (rel >> 2), y ^ ((rel >> 1) & 1), c ^ (rel & 1))
            cp = pltpu.make_async_remote_copy(src_ref=v_ref, dst_ref=buf_ref.at[me], send_sem=send_sems.at[rel - 1],
                                              recv_sem=recv_sems.at[rel - 1], device_id=peer, device_id_type=MESH)
            cp.start()
            copies.append(cp)
        for rel in range(1, 8):
            peer_id = me ^ rel
            pltpu.make_async_remote_copy(src_ref=v_ref, dst_ref=buf_ref.at[peer_id], send_sem=send_sems.at[rel - 1],
                                         recv_sem=recv_sems.at[rel - 1], device_id=(x, y, c),
                                         device_id_type=MESH).wait_recv()
        for cp in copies:
            cp.wait_send()
        acc = buf_ref[0]
        for i in range(1, 8):
            acc = acc + buf_ref[i]
        o_ref[...] = acc

    return pl.pallas_call(
        body, name="allreduce_small",
        in_specs=[pl.BlockSpec(memory_space=pltpu.VMEM)], out_specs=pl.BlockSpec(memory_space=pltpu.VMEM),
        out_shape=jax.ShapeDtypeStruct(v.shape, F32),
        scratch_shapes=[pltpu.VMEM((8, rows, LANES), F32), pltpu.SemaphoreType.DMA((7,)), pltpu.SemaphoreType.DMA((7,))],
    )(v)


def _add_pair(g, got, c, name):
    _, rh, cols = got.shape
    rb = _tile(rh, 256)
    nrb = rh // rb

    def body(c_ref, g_ref, got_ref, o_ref):
        o_ref[...] = (g_ref[...] + got_ref[...]).astype(BF16)

    spec = pl.BlockSpec((1, rb, cols), lambda j, i, c_ref: (j, i, 0))
    return pl.pallas_call(
        body, name=name,
        grid_spec=pltpu.PrefetchScalarGridSpec(
            num_scalar_prefetch=1, grid=(N_SHARD, nrb),
            in_specs=[pl.BlockSpec((1, rb, cols), lambda j, i, c_ref: (j, c_ref[0] * nrb + i, 0)), spec],
            out_specs=spec),
        out_shape=jax.ShapeDtypeStruct(got.shape, BF16),
        compiler_params=_params(("parallel", "parallel")))(c.reshape(1), g, got)


def _add_chips(g, got, landed, k, c, name):
    _, rh, cols = got.shape
    rb = _tile(rh, 256)
    nrb = rh // rb

    def body(kc_ref, g_ref, got_ref, l_ref, o_ref):
        own = g_ref[0] + got_ref[0]
        o_ref[...] = own + l_ref[0].astype(F32) + l_ref[1].astype(F32) + l_ref[2].astype(F32)

    half_c = lambda i, kc: (kc[1] * nrb + i, 0)
    return pl.pallas_call(
        body, name=name,
        grid_spec=pltpu.PrefetchScalarGridSpec(
            num_scalar_prefetch=1, grid=(nrb,),
            in_specs=[pl.BlockSpec((1, rb, cols), lambda i, kc: (kc[0],) + half_c(i, kc)),
                      pl.BlockSpec((1, rb, cols), lambda i, kc: (kc[0], i, 0)),
                      pl.BlockSpec((N_SHARD - 1, rb, cols), lambda i, kc: (0, i, 0))],
            out_specs=pl.BlockSpec((rb, cols), half_c)),
        out_shape=jax.ShapeDtypeStruct((2 * rh, cols), F32),
        compiler_params=_params(("parallel",)))(jnp.stack([k, c]), g, got, landed)


def _adamw(wv, gs, m, v, name):
    rows, cols = wv.shape
    rb = _tile(rows, 256)
    c1 = 1.0 - ADAM_B1 ** ADAM_STEP
    c2 = 1.0 - ADAM_B2 ** ADAM_STEP
    n_g = len(gs)

    def body(*refs):
        w_ref, g_refs, (m_ref, v_ref, go_ref, d_ref, nm_ref, nv_ref) = refs[0], refs[1:1 + n_g], refs[1 + n_g:]
        gv = g_refs[0][...]
        for ref in g_refs[1:]:
            gv = gv + ref[...]
        go_ref[...] = gv
        nm = ADAM_B1 * m_ref[...] + (1.0 - ADAM_B1) * gv
        nv = ADAM_B2 * v_ref[...] + (1.0 - ADAM_B2) * (gv * gv)
        nm_ref[...] = nm
        nv_ref[...] = nv
        d_ref[...] = -ADAM_LR * ((nm / c1) / (jnp.sqrt(nv / c2) + ADAM_EPS) + ADAM_WD * w_ref[...])

    spec = pl.BlockSpec((rb, cols), lambda i: (i, 0))
    return pl.pallas_call(
        body, name=name, grid=(rows // rb,), in_specs=[spec] * (3 + n_g), out_specs=[spec] * 4,
        out_shape=[jax.ShapeDtypeStruct((rows, cols), F32)] * 4,
        compiler_params=_params(("parallel",)))(wv, *gs, m, v)


def _pack(flats):
    cat = jnp.concatenate([f.reshape(-1) for f in flats])
    n = cat.shape[0]
    rows = -(-n // (8 * LANES)) * 8
    return jnp.pad(cat, (0, rows * LANES - n)).reshape(rows, LANES)


def _unpack(packed, shapes):
    flat = packed.reshape(-1)
    out, off = [], 0
    for shp in shapes:
        n = 1
        for dim in shp:
            n *= dim
        out.append(flat[off:off + n].reshape(shp))
        off += n
    return out


def kernel(x, p, norm_g, w_in, conv_w, conv_b, dt_bias, a_log, d_skip, gnorm_g, pool_mix_w, pool_mix_b, pool_scale, w_branch_a, w_branch_b, w_out, ple_norm_g, w_ple_gate, w_ple_up, final_g, loss_target, m_norm_g, m_w_in, m_conv_w, m_conv_b, m_dt_bias, m_a_log, m_d_skip, m_gnorm_g, m_pool_mix_w, m_pool_mix_b, m_pool_scale, m_w_branch_a, m_w_branch_b, m_w_out, m_ple_norm_g, m_w_ple_gate, m_w_ple_up, m_final_g, v_norm_g, v_w_in, v_conv_w, v_conv_b, v_dt_bias, v_a_log, v_d_skip, v_gnorm_g, v_pool_mix_w, v_pool_mix_b, v_pool_scale, v_w_branch_a, v_w_branch_b, v_w_out, v_ple_norm_g, v_w_ple_gate, v_w_ple_up, v_final_g):
    wts = dict(norm_g=norm_g, w_in=w_in, conv_w=conv_w, conv_b=conv_b, dt_bias=dt_bias, a_log=a_log, d_skip=d_skip,
               gnorm_g=gnorm_g, pool_mix_w=pool_mix_w, pool_mix_b=pool_mix_b, pool_scale=pool_scale,
               w_branch_a=w_branch_a, w_branch_b=w_branch_b, w_out=w_out, ple_norm_g=ple_norm_g,
               w_ple_gate=w_ple_gate, w_ple_up=w_ple_up, final_g=final_g)
    mom_m = dict(norm_g=m_norm_g, w_in=m_w_in, conv_w=m_conv_w, conv_b=m_conv_b, dt_bias=m_dt_bias, a_log=m_a_log,
                 d_skip=m_d_skip, gnorm_g=m_gnorm_g, pool_mix_w=m_pool_mix_w, pool_mix_b=m_pool_mix_b,
                 pool_scale=m_pool_scale, w_branch_a=m_w_branch_a, w_branch_b=m_w_branch_b, w_out=m_w_out,
                 ple_norm_g=m_ple_norm_g, w_ple_gate=m_w_ple_gate, w_ple_up=m_w_ple_up, final_g=m_final_g)
    mom_v = dict(norm_g=v_norm_g, w_in=v_w_in, conv_w=v_conv_w, conv_b=v_conv_b, dt_bias=v_dt_bias, a_log=v_a_log,
                 d_skip=v_d_skip, gnorm_g=v_gnorm_g, pool_mix_w=v_pool_mix_w, pool_mix_b=v_pool_mix_b,
                 pool_scale=v_pool_scale, w_branch_a=v_w_branch_a, w_branch_b=v_w_branch_b, w_out=v_w_out,
                 ple_norm_g=v_ple_norm_g, w_ple_gate=v_w_ple_gate, w_ple_up=v_w_ple_up, final_g=v_final_g)
    c = lax.axis_index("c")
    k = 2 * lax.axis_index("x") + lax.axis_index("y")
    flat2 = lambda a: a.reshape(-1, a.shape[-1])

    slots = {n: _into_slot(flat2(wts[n]), k, BF16, "slot_" + n) for n in BIG}
    w_in_g, conv_g = _gather_weights([slots["w_in"]], [_into_slot(flat2(conv_w), k, F32, "slot_conv_w")])
    n_rest = len(REST)
    gsend, grecv, gbufs, gtoken = _split_start("gather_rest_start", [slots[n] for n in REST], conv_g,
                                               _gather_plan(n_rest), 3 * n_rest)

    def rest_weights(after):
        return dict(zip(REST, _split_wait("gather_rest_wait", gbufs, gsend, grecv, after, _gather_plan(n_rest))))

    flying = {}

    def early_grads(early):
        sends = [_to_bf16(early[n], "bf16_" + n) for n in REST]
        lands = [pltpu.with_memory_space_constraint(lax.empty((N_SHARD - 1,) + v.shape[1:], BF16), pltpu.HBM)
                 for v in sends]
        ssend, srecv, sbufs, stoken = _split_start("scatter_rest_start", sends + lands, early[REST[0]],
                                                   _scatter_plan(n_rest), 3 * n_rest)
        flying.update(send=ssend, recv=srecv, bufs=sbufs)
        return stoken

    def w_in_grad(g_w_in):
        got = _swap_halves([g_w_in])[0]
        pair = _add_pair(g_w_in, got, c, "add_pair_w_in")
        land = pltpu.with_memory_space_constraint(lax.empty((N_SHARD - 1,) + pair.shape[1:], BF16), pltpu.HBM)
        wsend, wrecv, wbufs, wtoken = _split_start("scatter_w_in_start", [pair, land], got, _scatter_plan(1), 3)
        flying.update(w_send=wsend, w_recv=wrecv, w_bufs=wbufs, w_got=got)
        return wtoken

    small = {n: wts[n] for n in SMALL}
    small["norm_g"] = norm_g + gtoken[0, 0]
    loss, grad_x, grads = _local_step(x, p[0], loss_target, dict(w_in=w_in_g, conv_w=conv_g), small,
                                      rest_weights, early_grads, w_in_grad)
    g_w_in = grads["w_in"]
    landed = _split_wait("scatter_w_in_wait", flying["w_bufs"], flying["w_send"], flying["w_recv"], grad_x,
                         _scatter_plan(1))[1]
    w_in_sum = _join_halves([_add_chips(g_w_in, flying["w_got"], landed, k, c, "add_chips_w_in")])[0]

    sbufs = _split_wait("scatter_rest_wait", flying["bufs"], flying["send"], flying["recv"], g_w_in,
                        _scatter_plan(n_rest))
    mine = [_add_landed(grads[n], ld, k, "add_landed_" + n) for n, ld in zip(REST, sbufs[n_rest:])]
    theirs = _swap_sibling(mine)
    g_sums = dict(zip(REST, zip(mine, theirs)))
    g_sums["w_in"] = (w_in_sum,)

    conv_shape = flat2(conv_w).shape
    small_sum = _allreduce_small(_pack([grads[n] for n in SMALL] + [grads["conv_w"], loss]))
    small_shapes = [wts[n].shape for n in SMALL] + [(N_SHARD,) + conv_shape, (1,)]
    small_g = _unpack(small_sum, small_shapes)
    g_conv = lax.dynamic_index_in_dim(small_g[-2], k, axis=0, keepdims=False)

    outs = {}
    for n in BIG:
        vals = _adamw(flat2(wts[n]), g_sums[n], flat2(mom_m[n]), flat2(mom_v[n]), "adamw_" + n)
        for kind, val in zip(("grad", "delta", "new_m", "new_v"), vals):
            outs[kind, n] = val.reshape(wts[n].shape)
    names = SMALL + ("conv_w",)
    sm = _adamw(_pack([wts[n] for n in names]), (_pack(small_g[:len(SMALL)] + [g_conv]),),
                _pack([mom_m[n] for n in names]), _pack([mom_v[n] for n in names]), "adamw_small")
    sm_shapes = [wts[n].shape for n in names]
    for kind, val in zip(("grad", "delta", "new_m", "new_v"), sm):
        for n, piece in zip(names, _unpack(val, sm_shapes)):
            outs[kind, n] = piece
    return (small_g[-1][0], grad_x, *[outs[kind, n] for kind in ("grad", "delta", "new_m", "new_v") for n in WEIGHTS])
```

```python
import functools

import jax
import jax.numpy as jnp
from jax import lax
from jax.experimental import pallas as pl
from jax.experimental.pallas import tpu as pltpu

F32 = jnp.float32
BF16 = jnp.bfloat16
HIGHEST = lax.Precision.HIGHEST
MESH = pl.DeviceIdType.MESH

EPS = 1e-6
HEAD_DIM = 64
SSM_GROUPS = 4
D_STATE = 128
CONV_WIDTH = 4
CHUNK = 128
N_POOL = 4
LANES = 128
N_SHARD = 4

ADAM_LR = 0.001
ADAM_B1 = 0.9
ADAM_B2 = 0.999
ADAM_EPS = 1e-08
ADAM_WD = 0.01
ADAM_STEP = 10

BIG = ("w_in", "pool_mix_w", "w_branch_a", "w_branch_b", "w_out", "w_ple_gate", "w_ple_up")
REST = BIG[1:]
SMALL = ("norm_g", "conv_b", "dt_bias", "a_log", "d_skip", "gnorm_g", "pool_mix_b", "pool_scale",
         "ple_norm_g", "final_g")
WEIGHTS = ("norm_g", "w_in", "conv_w", "conv_b", "dt_bias", "a_log", "d_skip", "gnorm_g", "pool_mix_w",
           "pool_mix_b", "pool_scale", "w_branch_a", "w_branch_b", "w_out", "ple_norm_g", "w_ple_gate",
           "w_ple_up", "final_g")


def _params(sem=None, vmem_mb=56):
    kw = dict(vmem_limit_bytes=vmem_mb << 20)
    if sem is not None:
        kw["dimension_semantics"] = sem
    return pltpu.CompilerParams(**kw)


def _sigmoid(v):
    return 1.0 / (1.0 + jnp.exp(-v))


def _softplus(v):
    return jnp.maximum(v, 0.0) + jnp.log1p(jnp.exp(-jnp.abs(v)))


def _bdot(a, b):
    return jnp.dot(a.astype(BF16), b.astype(BF16), preferred_element_type=F32)


def _bdot_nt(a, b):
    return lax.dot_general(a.astype(BF16), b.astype(BF16), (((1,), (1,)), ((), ())), preferred_element_type=F32)


def _bdot_tn(a, b):
    return lax.dot_general(a.astype(BF16), b.astype(BF16), (((0,), (0,)), ((), ())), preferred_element_type=F32)


def _col_block(col0, width):
    assert col0 % width == 0, (col0, width)
    return col0 // width


def _tile(n, cap):
    if n <= cap:
        return n
    best = None
    for cand in range(8, cap + 1, 8):
        if n % cand == 0:
            best = cand
    assert best is not None, (n, cap)
    return best


def _shift_down(v, j, row):
    return jnp.where(row >= j, pltpu.roll(v, j, 0), 0.0)


def _shift_up(v, j, row):
    n = v.shape[0]
    return jnp.where(row < n - j, pltpu.roll(v, n - j, 0), 0.0)


def _mm(a, w, name, tm=1024, tn=1024):
    t, k = a.shape
    tm = min(tm, t)
    blocked = w.ndim == 3
    if blocked:
        nblk, _, tn = w.shape
        n = nblk * tn
        w_spec = pl.BlockSpec((1, k, tn), lambda i, j: (j, 0, 0))
    else:
        n = w.shape[1]
        tn = min(tn, n)
        w_spec = pl.BlockSpec((k, tn), lambda i, j: (0, j))

    def body(a_ref, w_ref, o_ref):
        wv = w_ref[0] if blocked else w_ref[...]
        o_ref[...] = _bdot(a_ref[...], wv)

    return pl.pallas_call(
        body, name=name, grid=(t // tm, n // tn),
        in_specs=[pl.BlockSpec((tm, k), lambda i, j: (i, 0)), w_spec],
        out_specs=pl.BlockSpec((tm, tn), lambda i, j: (i, j)),
        out_shape=jax.ShapeDtypeStruct((t, n), F32),
        compiler_params=_params(("parallel", "parallel")))(a, w)


def _mm_nt(a, w, name, tm=1024, tk=1024):
    t, k = a.shape
    n = w.shape[0]
    tm, tk = min(tm, t), min(tk, k)

    def body(a_ref, w_ref, o_ref):
        kk = pl.program_id(1)
        part = _bdot_nt(a_ref[...], w_ref[...])

        @pl.when(kk == 0)
        def _():
            o_ref[...] = part

        @pl.when(kk > 0)
        def _():
            o_ref[...] += part

    return pl.pallas_call(
        body, name=name, grid=(t // tm, k // tk),
        in_specs=[pl.BlockSpec((tm, tk), lambda i, j: (i, j)), pl.BlockSpec((n, tk), lambda i, j: (0, j))],
        out_specs=pl.BlockSpec((tm, n), lambda i, j: (i, 0)),
        out_shape=jax.ShapeDtypeStruct((t, n), F32),
        compiler_params=_params(("parallel", "arbitrary")))(a, w)


def _mm_tn(a, b, name, tn=1024, tk=1024, col_blocks=False):
    t, m = a.shape
    n = b.shape[1]
    tn, tk = min(tn, n), min(tk, t)

    def body(a_ref, b_ref, o_ref):
        kk = pl.program_id(1)
        part = _bdot_tn(a_ref[...], b_ref[...])
        part = part[None] if col_blocks else part

        @pl.when(kk == 0)
        def _():
            o_ref[...] = part

        @pl.when(kk > 0)
        def _():
            o_ref[...] += part

    if col_blocks:
        out_spec = pl.BlockSpec((1, m, tn), lambda j, kk: (j, 0, 0))
        out_shape = jax.ShapeDtypeStruct((n // tn, m, tn), F32)
    else:
        out_spec = pl.BlockSpec((m, tn), lambda j, kk: (0, j))
        out_shape = jax.ShapeDtypeStruct((m, n), F32)
    return pl.pallas_call(
        body, name=name, grid=(n // tn, t // tk),
        in_specs=[pl.BlockSpec((tk, m), lambda j, kk: (kk, 0)), pl.BlockSpec((tk, tn), lambda j, kk: (kk, j))],
        out_specs=out_spec, out_shape=out_shape,
        compiler_params=_params(("parallel", "arbitrary")))(a, b)


def _w_in_pieces(d, di, dc, nh, shard_w):
    pgd = d // N_POOL
    o_dt, o_u = di + dc, di + dc + nh
    o_zp, o_ga, o_gb = o_u + d, o_u + 2 * d, o_u + 3 * d
    c_z, c_uz = 2 * d, 2 * d + di + dc
    runs = [(False, 0, o_ga, d), (False, d, o_gb, d), (False, c_z, 0, di + dc), (True, 0, o_dt, nh)]
    for g in range(N_POOL):
        runs.append((False, c_uz + 2 * g * pgd, o_u + g * pgd, pgd))
        runs.append((False, c_uz + (2 * g + 1) * pgd, o_zp + g * pgd, pgd))
    pieces = []
    for is_dt, dst, src, n in runs:
        while n > 0:
            k, off = divmod(src, shard_w)
            m = min(n, shard_w - off)
            pieces.append((is_dt, dst, k, off, m))
            dst, src, n = dst + m, src + m, n - m
    return pieces


def _regroup_w_in(w_sh, d, di, dc, nh, rb=256):
    _, rows, sw = w_sh.shape
    n_main = 4 * d + di + dc
    pieces = _w_in_pieces(d, di, dc, nh, sw)
    rb = min(rb, rows)

    def body(w_ref, main_ref, dt_ref):
        dt_ref[...] = jnp.zeros_like(dt_ref)
        for is_dt, dst, k, off, m in pieces:
            out = dt_ref if is_dt else main_ref
            out[:, dst:dst + m] = w_ref[k, :, off:off + m]

    return pl.pallas_call(
        body, name="regroup_w_in", grid=(rows // rb,),
        in_specs=[pl.BlockSpec((N_SHARD, rb, sw), lambda i: (0, i, 0))],
        out_specs=[pl.BlockSpec((rb, n_main), lambda i: (i, 0)), pl.BlockSpec((rb, LANES), lambda i: (i, 0))],
        out_shape=[jax.ShapeDtypeStruct((rows, n_main), w_sh.dtype), jax.ShapeDtypeStruct((rows, LANES), w_sh.dtype)],
        compiler_params=_params(("parallel",)))(w_sh)


def _ungroup_w_in(d_main, d_dt, d, di, dc, nh, rb=128):
    rows, n_main = d_main.shape
    sw = (n_main + nh) // N_SHARD
    pieces = _w_in_pieces(d, di, dc, nh, sw)
    rb = min(rb, rows)

    def body(main_ref, dt_ref, o_ref):
        for is_dt, dst, k, off, m in pieces:
            src = dt_ref if is_dt else main_ref
            o_ref[k, :, off:off + m] = src[:, dst:dst + m]

    return pl.pallas_call(
        body, name="ungroup_w_in", grid=(rows // rb,),
        in_specs=[pl.BlockSpec((rb, n_main), lambda i: (i, 0)), pl.BlockSpec((rb, LANES), lambda i: (i, 0))],
        out_specs=pl.BlockSpec((N_SHARD, rb, sw), lambda i: (0, i, 0)),
        out_shape=jax.ShapeDtypeStruct((N_SHARD, rows, sw), F32),
        compiler_params=_params(("parallel",)))(d_main, d_dt)


def _inproj(x2, norm_g, w_main, w_dt, tm=1024, tn=1024):
    t, d = x2.shape
    n = w_main.shape[1]
    tm, tn = min(tm, t), min(tn, n)

    def body(x_ref, g_ref, w_ref, wdt_ref, proj_ref, dt_ref, h_ref):
        @pl.when(pl.program_id(1) == 0)
        def _():
            xv = x_ref[...]
            r = lax.rsqrt(jnp.mean(xv * xv, axis=-1, keepdims=True) + EPS)
            h = (xv * r * g_ref[...]).astype(BF16)
            h_ref[...] = h
            dt_ref[...] = jnp.dot(h, wdt_ref[...].astype(BF16), preferred_element_type=F32)

        proj_ref[...] = jnp.dot(h_ref[...], w_ref[...].astype(BF16), preferred_element_type=F32)

    return pl.pallas_call(
        body, name="inproj", grid=(t // tm, n // tn),
        in_specs=[pl.BlockSpec((tm, d), lambda i, j: (i, 0)), pl.BlockSpec((1, d), lambda i, j: (0, 0)),
                  pl.BlockSpec((d, tn), lambda i, j: (0, j)), pl.BlockSpec((d, LANES), lambda i, j: (0, 0))],
        out_specs=[pl.BlockSpec((tm, tn), lambda i, j: (i, j)), pl.BlockSpec((tm, LANES), lambda i, j: (i, 0)),
                   pl.BlockSpec((tm, d), lambda i, j: (i, 0))],
        out_shape=[jax.ShapeDtypeStruct((t, n), F32), jax.ShapeDtypeStruct((t, LANES), F32),
                   jax.ShapeDtypeStruct((t, d), BF16)],
        compiler_params=_params(("parallel", "arbitrary")))(x2, norm_g, w_main, w_dt)


def _conv_w_spec(conv_w, cb, j_axis):
    sw = conv_w.shape[2]
    assert sw % cb == 0, (sw, cb)
    per = sw // cb
    return N_SHARD * per, pl.BlockSpec((1, CONV_WIDTH, cb), lambda *ij: (ij[j_axis] // per, 0, ij[j_axis] % per))


CONV_ROWS = 64
CONV_HALO = 8


def _conv_taps(x_ref, t0, rc):
    if t0 == 0:
        cur = x_ref[0:rc, :]
        row = lax.broadcasted_iota(jnp.int32, cur.shape, 0)
        return [cur] + [_shift_down(cur, j, row) for j in range(1, CONV_WIDTH)]
    ext = x_ref[t0 - CONV_HALO:t0 + rc, :]
    return [ext[CONV_HALO:]] + [pltpu.roll(ext, j, 0)[CONV_HALO:] for j in range(1, CONV_WIDTH)]


def _conv_weights(w_ref):
    return [w_ref[0, CONV_WIDTH - 1 - j:CONV_WIDTH - j, :] for j in range(CONV_WIDTH)]


def _conv_pre(taps, wts, bias):
    acc = bias + taps[0] * wts[0]
    for j in range(1, CONV_WIDTH):
        acc = acc + taps[j] * wts[j]
    return acc


def _conv_fwd(proj, conv_w, conv_b, nb, s, col0, cb=256):
    n_blk, w_spec = _conv_w_spec(conv_w, cb, 1)
    blk0 = _col_block(col0, cb)
    rc = min(CONV_ROWS, s)

    def body(x_ref, w_ref, b_ref, o_ref):
        wts, bias = _conv_weights(w_ref), b_ref[...]
        for t0 in range(0, s, rc):
            acc = _conv_pre(_conv_taps(x_ref, t0, rc), wts, bias)
            o_ref[t0:t0 + rc, :] = acc * _sigmoid(acc)

    return pl.pallas_call(
        body, name="conv_fwd", grid=(nb, n_blk),
        in_specs=[pl.BlockSpec((s, cb), lambda b, j: (b, blk0 + j)), w_spec, pl.BlockSpec((1, cb), lambda b, j: (0, j))],
        out_specs=pl.BlockSpec((s, cb), lambda b, j: (b, j)),
        out_shape=jax.ShapeDtypeStruct((nb * s, n_blk * cb), F32),
        compiler_params=_params(("parallel", "parallel")))(proj, conv_w, conv_b)


def _ssd_consts(di):
    r = lax.broadcasted_iota(jnp.int32, (CHUNK, CHUNK), 0)
    c = lax.broadcasted_iota(jnp.int32, (CHUNK, CHUNK), 1)
    tril = (r >= c).astype(F32)
    head = lax.broadcasted_iota(jnp.int32, (LANES, di), 0)
    chan = lax.broadcasted_iota(jnp.int32, (LANES, di), 1) // HEAD_DIM
    expand = (head == chan).astype(BF16)
    return tril, expand, expand.T


def _expand(v, e, terms=3):
    acc = None
    for _ in range(terms):
        vb = v.astype(BF16)
        part = jnp.dot(vb, e, preferred_element_type=F32)
        acc = part if acc is None else acc + part
        v = v - vb.astype(F32)
    return acc


def _head_sum(t, et, terms=2):
    acc = None
    for _ in range(terms):
        tb = t.astype(BF16)
        part = jnp.dot(tb, et, preferred_element_type=F32)
        acc = part if acc is None else acc + part
        t = t - tb.astype(F32)
    return acc


def _ssd_scalars(dtr_ref, dtb_ref, alog_ref, tri):
    dtpre = dtr_ref[...] + dtb_ref[...]
    dt = _softplus(dtpre)
    a_neg = -jnp.exp(alog_ref[...])
    a_dt = dt * a_neg
    a_cs = jnp.dot(tri, a_dt, precision=HIGHEST, preferred_element_type=F32)
    a_cst = lax.dot_general(a_dt, tri, (((0,), (1,)), ((), ())), precision=HIGHEST, preferred_element_type=F32)
    return dtpre, dt, a_neg, a_cs, a_cst


def _ssd_fwd(proj, xbc, dtraw, dtb, alog, dskx, gn, nb, s, di, z_col0):
    t = nb * s
    nc = s // CHUNK
    hpg = di // HEAD_DIM // SSM_GROUPS
    gw = di // SSM_GROUPS
    gn_w = SSM_GROUPS * D_STATE
    b_blk = _col_block(di, gn_w)
    z_blk = _col_block(z_col0, di)
    L, P, N = CHUNK, HEAD_DIM, D_STATE
    tril, expand, _ = _ssd_consts(di)

    def body(z_ref, x_ref, b_ref, c_ref, dtr_ref, dtb_ref, alog_ref, dskx_ref, gn_ref, tril_ref, e_ref,
             ypre_ref, yan_ref, hp_ref, st_ref, yd_ref, xdt_ref):
        @pl.when(pl.program_id(1) == 0)
        def _():
            st_ref[...] = jnp.zeros_like(st_ref)

        hp_ref[0] = st_ref[...]
        tri = tril_ref[...]
        _, dt, _, a_cs, a_cst = _ssd_scalars(dtr_ref, dtb_ref, alog_ref, tri)
        ev = e_ref[...]
        a_exp = _expand(a_cs, ev)
        xv = x_ref[...]
        xdt = xv * _expand(dt, ev)
        xdt_ref[...] = xdt
        a_last = a_exp[L - 1:L, :]
        xe = xdt * jnp.exp(a_last - a_exp)
        ea = jnp.exp(a_exp)
        e_last = jnp.exp(a_last)
        lower = tri > 0.5
        for g in range(SSM_GROUPS):
            gs = slice(g * gw, (g + 1) * gw)
            bg = b_ref[:, g * N:(g + 1) * N].astype(BF16)
            cg = c_ref[:, g * N:(g + 1) * N].astype(BF16)
            gm = _bdot_nt(cg, bg)
            ht = st_ref[:, gs]
            ch = _bdot(cg, ht)
            for e in range(hpg):
                h = g * hpg + e
                hs = slice(h * P, (h + 1) * P)
                decay = jnp.where(lower, jnp.exp(a_cs[:, h:h + 1] - a_cst[h:h + 1, :]), 0.0)
                yd_ref[:, hs] = _bdot(gm * decay, xdt_ref[:, hs])
            st_ref[:, gs] = ht * e_last[:, gs] + _bdot_tn(bg, xe[:, gs])
            ypre = yd_ref[:, gs] + ea[:, gs] * ch + xv[:, gs] * dskx_ref[:, gs]
            ypre_ref[:, gs] = ypre
            zv = z_ref[:, gs]
            v = ypre * zv * _sigmoid(zv)
            r = lax.rsqrt(jnp.mean(v * v, axis=-1, keepdims=True) + EPS)
            yan_ref[:, gs] = (v * r * gn_ref[:, gs]).astype(BF16)

    row = lambda b, c: b * nc + c
    vec = lambda w: pl.BlockSpec((1, w), lambda b, c: (0, 0))
    return pl.pallas_call(
        body, name="ssd_fwd", grid=(nb, nc),
        in_specs=[pl.BlockSpec((L, di), lambda b, c: (row(b, c), z_blk)),
                  pl.BlockSpec((L, di), lambda b, c: (row(b, c), 0)),
                  pl.BlockSpec((L, gn_w), lambda b, c: (row(b, c), b_blk)),
                  pl.BlockSpec((L, gn_w), lambda b, c: (row(b, c), b_blk + 1)),
                  pl.BlockSpec((L, LANES), lambda b, c: (row(b, c), 0)),
                  vec(LANES), vec(LANES), vec(di), vec(di),
                  pl.BlockSpec((L, L), lambda b, c: (0, 0)),
                  pl.BlockSpec((LANES, di), lambda b, c: (0, 0))],
        out_specs=[pl.BlockSpec((L, di), lambda b, c: (row(b, c), 0)),
                   pl.BlockSpec((L, di), lambda b, c: (row(b, c), 0)),
                   pl.BlockSpec((1, N, di), lambda b, c: (row(b, c), 0, 0))],
        out_shape=[jax.ShapeDtypeStruct((t, di), F32), jax.ShapeDtypeStruct((t, di), BF16),
                   jax.ShapeDtypeStruct((nb * nc, N, di), F32)],
        scratch_shapes=[pltpu.VMEM((N, di), F32), pltpu.VMEM((L, di), F32), pltpu.VMEM((L, di), F32)],
        compiler_params=_params(("parallel", "arbitrary")))(
            proj, xbc, xbc, xbc, dtraw, dtb, alog, dskx, gn, tril, expand)


def _pool_sum(v, g, row, shift):
    s2 = v + shift(v, 1, row)
    s4 = s2 + shift(s2, 2, row)
    s8 = s4 + shift(s4, 4, row)
    s16 = s8 + shift(s8, 8, row)
    return jnp.where(g == 0, s2, jnp.where(g == 1, s4, jnp.where(g == 2, s8, s16)))


def _pool_count(g, row):
    return jnp.minimum(row + 1, jnp.left_shift(2, g)).astype(F32)


def _pool_fwd(proj, mix_w, mix_b, scale, nb, s, col0):
    pgd = mix_w.shape[-1]
    blk0 = _col_block(col0, 2 * pgd)

    def body(uz_ref, w_ref, b_ref, sc_ref, o_ref):
        g = pl.program_id(1)
        u = uz_ref[:, :pgd]
        zp = uz_ref[:, pgd:]
        row = lax.broadcasted_iota(jnp.int32, u.shape, 0)
        pooled = _pool_sum(u, g, row, _shift_down) / _pool_count(g, row) - u
        mixed = _bdot(pooled, w_ref[:, 0].reshape(pgd, pgd)) + b_ref[...]
        o_ref[...] = (mixed * sc_ref[...] * zp * _sigmoid(zp)).astype(BF16)

    return pl.pallas_call(
        body, name="pool_fwd", grid=(nb, N_POOL),
        in_specs=[pl.BlockSpec((s, 2 * pgd), lambda b, g: (b, blk0 + g)),
                  pl.BlockSpec((N_SHARD, 1, pgd // N_SHARD, pgd), lambda b, g: (0, g, 0, 0)),
                  pl.BlockSpec((1, pgd), lambda b, g: (0, g)), pl.BlockSpec((1, pgd), lambda b, g: (0, g))],
        out_specs=pl.BlockSpec((s, pgd), lambda b, g: (b, g)),
        out_shape=jax.ShapeDtypeStruct((nb * s, N_POOL * pgd), BF16),
        compiler_params=_params(("parallel", "parallel")))(proj, mix_w, mix_b, scale)


def _mid_fwd(ya, yb, proj, col0, x2, p2, tgt, wo, wpg, wup, ple_g, final_g, tm=256):
    t, d = ya.shape
    tm = min(tm, t)
    blk = _col_block(col0, 2 * d)
    n_up, pdim, up_w = wup.shape

    def body(ya_ref, yb_ref, g_ref, x_ref, p_ref, tgt_ref, wo_ref, wpg_ref, wup_ref, pg_ref, g_fin_ref,
             merged_ref, hn_ref, dpre_ref, dpu_ref, x1_ref, dx2_ref, loss_ref, dg_ref):
        @pl.when(pl.program_id(0) == 0)
        def _():
            loss_ref[...] = jnp.zeros_like(loss_ref)
            dg_ref[...] = jnp.zeros_like(dg_ref)

        merged = (_sigmoid(g_ref[:, :d]) * ya_ref[...] + _sigmoid(g_ref[:, d:]) * yb_ref[...]).astype(BF16)
        merged_ref[...] = merged
        x1 = x_ref[...] + jnp.dot(merged, wo_ref[...], preferred_element_type=F32)
        x1_ref[...] = x1
        r1 = lax.rsqrt(jnp.mean(x1 * x1, axis=-1, keepdims=True) + EPS)
        hn = (x1 * r1 * pg_ref[...]).astype(BF16)
        hn_ref[...] = hn
        gate = _sigmoid(jnp.dot(hn, wpg_ref[...], preferred_element_type=F32))
        pb = p_ref[...].astype(BF16)
        pu = jnp.concatenate([jnp.dot(pb, wup_ref[j], preferred_element_type=F32) for j in range(n_up)], axis=1)
        x2 = x1 + gate * pu
        r = lax.rsqrt(jnp.mean(x2 * x2, axis=-1, keepdims=True) + EPS)
        xn = x2 * r
        fg = g_fin_ref[...]
        err = xn * fg - tgt_ref[...]
        loss_ref[...] += 0.5 * jnp.sum(jnp.mean(err * err, axis=-1, keepdims=True))
        dy = err * (1.0 / d)
        dg_ref[...] += jnp.sum(dy * xn, axis=0, keepdims=True)
        dxn = dy * fg
        dx2 = r * (dxn - xn * jnp.mean(dxn * xn, axis=-1, keepdims=True))
        dx2_ref[...] = dx2
        dpre_ref[...] = (dx2 * pu * gate * (1.0 - gate)).astype(BF16)
        dpu_ref[...] = (dx2 * gate).astype(BF16)

    row = pl.BlockSpec((tm, d), lambda i: (i, 0))
    vec = pl.BlockSpec((1, d), lambda i: (0, 0))
    whole = lambda a: pl.BlockSpec(a.shape, lambda i: (0,) * a.ndim)
    return pl.pallas_call(
        body, name="mid_fwd", grid=(t // tm,),
        in_specs=[row, row, pl.BlockSpec((tm, 2 * d), lambda i: (i, blk)), row,
                  pl.BlockSpec((tm, pdim), lambda i: (i, 0)), row, whole(wo), whole(wpg), whole(wup), vec, vec],
        out_specs=[row] * 6 + [pl.BlockSpec((1, LANES), lambda i: (0, 0)), vec],
        out_shape=[jax.ShapeDtypeStruct((t, d), BF16)] * 4 + [jax.ShapeDtypeStruct((t, d), F32)] * 2 + [
            jax.ShapeDtypeStruct((1, LANES), F32), jax.ShapeDtypeStruct((1, d), F32)],
        compiler_params=_params(("arbitrary",)))(ya, yb, proj, x2, p2, tgt, wo, wpg, wup, ple_g, final_g)


def _rms_grad(xv, dh, g):
    r = lax.rsqrt(jnp.mean(xv * xv, axis=-1, keepdims=True) + EPS)
    xn = xv * r
    dd = dh * g
    return r * (dd - xn * jnp.mean(dd * xn, axis=-1, keepdims=True)), jnp.sum(dh * xn, axis=0, keepdims=True)


def _mid_bwd(dpre, dx2, x1, ya, yb, proj, col0, wpg, wo, ple_g, n_cols, tm=256):
    t, d = ya.shape
    tm = min(tm, t)
    blk = _col_block(col0, 2 * d)

    def body(dpre_ref, dx2_ref, x1_ref, ya_ref, yb_ref, g_ref, wpg_ref, wo_ref, pg_ref,
             dx1_ref, dya_ref, dyb_ref, dg_ref, dpg_ref):
        @pl.when(pl.program_id(0) == 0)
        def _():
            dpg_ref[...] = jnp.zeros_like(dpg_ref)

        dhn = _bdot_nt(dpre_ref[...], wpg_ref[...])
        dx, dpg = _rms_grad(x1_ref[...], dhn, pg_ref[...])
        dpg_ref[...] += dpg
        dx1 = dx2_ref[...] + dx
        dx1_ref[...] = dx1
        dm_v = _bdot_nt(dx1, wo_ref[...])
        sa = _sigmoid(g_ref[:, :d])
        sb = _sigmoid(g_ref[:, d:])
        dya_ref[...] = (dm_v * sa).astype(BF16)
        dyb_ref[...] = (dm_v * sb).astype(BF16)
        dg_ref[:, :d] = (dm_v * ya_ref[...] * sa * (1.0 - sa)).astype(BF16)
        dg_ref[:, d:] = (dm_v * yb_ref[...] * sb * (1.0 - sb)).astype(BF16)

    row = pl.BlockSpec((tm, d), lambda i: (i, 0))
    vec = pl.BlockSpec((1, d), lambda i: (0, 0))
    gspec = pl.BlockSpec((tm, 2 * d), lambda i: (i, blk))
    whole = lambda a: pl.BlockSpec(a.shape, lambda i: (0,) * a.ndim)
    return pl.pallas_call(
        body, name="mid_bwd", grid=(t // tm,),
        in_specs=[row, row, row, row, row, gspec, whole(wpg), whole(wo), vec],
        out_specs=[row, row, row, gspec, vec],
        out_shape=[jax.ShapeDtypeStruct((t, d), F32), jax.ShapeDtypeStruct((t, d), BF16),
                   jax.ShapeDtypeStruct((t, d), BF16), jax.ShapeDtypeStruct((t, n_cols), BF16),
                   jax.ShapeDtypeStruct((1, d), F32)],
        compiler_params=_params(("arbitrary",)))(dpre, dx2, x1, ya, yb, proj, wpg, wo, ple_g)


def _in_bwd(dproj, w_main, ddt, w_dt, x2, dx1, norm_g, after, tm=1024, tk=1024):
    t, k = dproj.shape
    d = x2.shape[1]
    tm, tk = min(tm, t), min(tk, k)
    nk = k // tk

    def body(a_ref, w_ref, ddt_ref, wdt_ref, x_ref, dres_ref, g_ref, _, gx_ref, dg_ref, acc_ref):
        kk = pl.program_id(1)

        @pl.when((pl.program_id(0) == 0) & (kk == 0))
        def _():
            dg_ref[...] = jnp.zeros_like(dg_ref)

        part = _bdot_nt(a_ref[...], w_ref[...])

        @pl.when(kk == 0)
        def _():
            acc_ref[...] = part

        @pl.when(kk > 0)
        def _():
            acc_ref[...] += part

        @pl.when(kk == nk - 1)
        def _():
            dh = acc_ref[...] + _bdot_nt(ddt_ref[...], wdt_ref[...])
            dx, dg = _rms_grad(x_ref[...], dh, g_ref[...])
            dg_ref[...] += dg
            gx_ref[...] = dres_ref[...] + dx

    row = pl.BlockSpec((tm, d), lambda i, j: (i, 0))
    vec = pl.BlockSpec((1, d), lambda i, j: (0, 0))
    return pl.pallas_call(
        body, name="in_bwd", grid=(t // tm, nk),
        in_specs=[pl.BlockSpec((tm, tk), lambda i, j: (i, j)), pl.BlockSpec((d, tk), lambda i, j: (0, j)),
                  pl.BlockSpec((tm, LANES), lambda i, j: (i, 0)), pl.BlockSpec((d, LANES), lambda i, j: (0, 0)),
                  row, row, vec, pl.BlockSpec((8, LANES), lambda i, j: (0, 0))],
        out_specs=[row, vec],
        out_shape=[jax.ShapeDtypeStruct((t, d), F32), jax.ShapeDtypeStruct((1, d), F32)],
        scratch_shapes=[pltpu.VMEM((tm, d), F32)],
        compiler_params=_params(("arbitrary", "arbitrary")))(dproj, w_main, ddt, w_dt, x2, dx1, norm_g, after)


def _pool_bwd(proj, dyb, dproj, mix_w, mix_b, scale, nb, s, col0):
    pgd = mix_w.shape[-1]
    blk0 = _col_block(col0, 2 * pgd)

    def body(uz_ref, dy_ref, _, w_ref, b_ref, sc_ref, duz_ref, dw_ref, db_ref, dsc_ref):
        g = pl.program_id(0)

        @pl.when(pl.program_id(1) == 0)
        def _():
            dw_ref[...] = jnp.zeros_like(dw_ref)
            db_ref[...] = jnp.zeros_like(db_ref)
            dsc_ref[...] = jnp.zeros_like(dsc_ref)

        u = uz_ref[:, :pgd]
        zp = uz_ref[:, pgd:]
        row = lax.broadcasted_iota(jnp.int32, u.shape, 0)
        cnt = _pool_count(g, row)
        pooled = _pool_sum(u, g, row, _shift_down) / cnt - u
        wv = w_ref[:, 0].reshape(pgd, pgd)
        mixed = _bdot(pooled, wv) + b_ref[...]
        sg = _sigmoid(zp)
        sz = zp * sg
        dy = dy_ref[...]
        sc = sc_ref[...]
        dsc_ref[...] += jnp.sum(dy * mixed * sz, axis=0, keepdims=True)
        dmixed = dy * sc * sz
        db_ref[...] += jnp.sum(dmixed, axis=0, keepdims=True)
        dw_ref[:, 0] += _bdot_tn(pooled, dmixed).reshape(N_SHARD, pgd // N_SHARD, pgd)
        dpooled = _bdot_nt(dmixed, wv)
        duz_ref[:, :pgd] = (_pool_sum(dpooled / cnt, g, row, _shift_up) - dpooled).astype(BF16)
        duz_ref[:, pgd:] = (dy * mixed * sc * sg * (1.0 + zp * (1.0 - sg))).astype(BF16)

    uz = pl.BlockSpec((s, 2 * pgd), lambda g, b: (b, blk0 + g))
    vec = pl.BlockSpec((1, pgd), lambda g, b: (0, g))
    wspec = pl.BlockSpec((N_SHARD, 1, pgd // N_SHARD, pgd), lambda g, b: (0, g, 0, 0))
    return pl.pallas_call(
        body, name="pool_bwd", grid=(N_POOL, nb),
        in_specs=[uz, pl.BlockSpec((s, pgd), lambda g, b: (b, g)), pl.BlockSpec(memory_space=pl.ANY), wspec, vec, vec],
        out_specs=[uz, wspec, vec, vec],
        out_shape=[jax.ShapeDtypeStruct(dproj.shape, dproj.dtype), jax.ShapeDtypeStruct(mix_w.shape, F32),
                   jax.ShapeDtypeStruct(mix_b.shape, F32), jax.ShapeDtypeStruct(scale.shape, F32)],
        input_output_aliases={2: 0},
        compiler_params=_params(("parallel", "arbitrary")))(proj, dyb, dproj, mix_w, mix_b, scale)


def _ssd_bwd(dyan, ypre, proj, xbc, dtraw, hp, dproj, dtb, alog, dskx, gn, nb, s, di, z_col0):
    t = nb * s
    nc = s // CHUNK
    hpg = di // HEAD_DIM // SSM_GROUPS
    gw = di // SSM_GROUPS
    gn_w = SSM_GROUPS * D_STATE
    dc = di + 2 * gn_w
    b_blk = _col_block(di, gn_w)
    z_blk = _col_block(z_col0, di)
    L, P, N = CHUNK, HEAD_DIM, D_STATE
    tril, expand, expand_t = _ssd_consts(di)

    def body(dy_ref, ypre_ref, z_ref, x_ref, b_ref, c_ref, dtr_ref, hp_ref, _, dtb_ref, alog_ref, dskx_ref, gn_ref,
             tril_ref, e_ref, et_ref, dz_ref, ddt_ref, dxbc_ref, dgn_ref, ddsk_ref, dalog_ref, ddtb_ref,
             dst_ref, dyp_ref, xdt_ref, dxm_ref, t1_ref, t2_ref, t3_ref, aux_ref):
        @pl.when((pl.program_id(0) == 0) & (pl.program_id(1) == 0))
        def _():
            dgn_ref[...] = jnp.zeros_like(dgn_ref)
            ddsk_ref[...] = jnp.zeros_like(ddsk_ref)
            dalog_ref[...] = jnp.zeros_like(dalog_ref)
            ddtb_ref[...] = jnp.zeros_like(ddtb_ref)

        @pl.when(pl.program_id(1) == 0)
        def _():
            dst_ref[...] = jnp.zeros_like(dst_ref)

        tri = tril_ref[...]
        dtpre, dt, a_neg, a_cs, a_cst = _ssd_scalars(dtr_ref, dtb_ref, alog_ref, tri)
        ev = e_ref[...]
        a_exp = _expand(a_cs, ev)
        dt_exp = _expand(dt, ev)
        xv = x_ref[...]
        xdt = xv * dt_exp
        xdt_ref[...] = xdt
        a_last = a_exp[L - 1:L, :]
        dte = jnp.exp(a_last - a_exp)
        xe = xdt * dte
        ea = jnp.exp(a_exp)
        e_last = jnp.exp(a_last)
        lower = tri > 0.5
        aux_ref[...] = jnp.zeros_like(aux_ref)
        for g in range(SSM_GROUPS):
            gs = slice(g * gw, (g + 1) * gw)
            zv = z_ref[:, gs]
            yp = ypre_ref[:, gs]
            sg = _sigmoid(zv)
            sz = zv * sg
            vg = yp * sz
            r = lax.rsqrt(jnp.mean(vg * vg, axis=-1, keepdims=True) + EPS)
            vn = vg * r
            dyg = dy_ref[:, gs]
            dgn_ref[:, gs] += jnp.sum(dyg * vn, axis=0, keepdims=True)
            dvn = dyg * gn_ref[:, gs]
            dv = r * (dvn - vn * jnp.mean(dvn * vn, axis=-1, keepdims=True))
            dy = dv * sz
            dyp_ref[:, gs] = dy
            dz_ref[:, gs] = (dv * yp * sg * (1.0 + zv * (1.0 - sg))).astype(BF16)
            bg = b_ref[:, g * N:(g + 1) * N].astype(BF16)
            cg = c_ref[:, g * N:(g + 1) * N].astype(BF16)
            gm = _bdot_nt(cg, bg)
            ht = hp_ref[0, :, gs]
            dht = dst_ref[:, gs]
            bds = _bdot(bg, dht)
            dye = dy * ea[:, gs]
            xe_g = xe[:, gs]
            dcg = _bdot_nt(dye, ht)
            dbg = _bdot_nt(xe_g, dht)
            dst_ref[:, gs] = e_last[:, gs] * dht + _bdot_tn(cg, dye)
            dgm = jnp.zeros((L, L), F32)
            for e in range(hpg):
                h = g * hpg + e
                hs = slice(h * P, (h + 1) * P)
                decay = jnp.where(lower, jnp.exp(a_cs[:, h:h + 1] - a_cst[h:h + 1, :]), 0.0)
                dy_h = dyp_ref[:, hs]
                dgm = dgm + _bdot_nt(dy_h, xdt_ref[:, hs]) * decay
                dxm_ref[:, hs] = _bdot_tn(gm * decay, dy_h)
            dxbc_ref[:, di + g * N:di + (g + 1) * N] = dbg + _bdot_tn(dgm, cg)
            dxbc_ref[:, di + gn_w + g * N:di + gn_w + (g + 1) * N] = dcg + _bdot(dgm, bg)
            dxm = dxm_ref[:, gs]
            x_g = xv[:, gs]
            dskx = dskx_ref[:, gs]
            xeb = xe_g * bds
            dxdt = dxm + dte[:, gs] * bds
            dxbc_ref[:, gs] = dxdt * dt_exp[:, gs] + dy * dskx
            each = ea[:, gs] * _bdot(cg, ht)
            y_diag = yp - x_g * dskx - each
            rnd = lambda v: v.astype(BF16).astype(F32)
            t1_ref[:, gs] = rnd(dy) * y_diag + dy * each - rnd(xdt[:, gs]) * dxm - xeb
            t2_ref[:, gs] = xeb
            t3_ref[:, gs] = dxdt * x_g
            aux_ref[0:1, gs] = jnp.sum(dht * ht, axis=0, keepdims=True)
            aux_ref[1:2, gs] = jnp.sum(dy * x_g, axis=0, keepdims=True)
        etv = et_ref[...]
        w_end = _head_sum(t2_ref[...], etv)
        aux = _head_sum(aux_ref[...], etv)
        rowi = lax.broadcasted_iota(jnp.int32, (L, LANES), 0)
        end = jnp.sum(w_end, axis=0, keepdims=True) + aux[0:1, :] * jnp.exp(a_cs[L - 1:L, :])
        da = _head_sum(t1_ref[...], etv, terms=3) + jnp.where(rowi == L - 1, end, 0.0)
        rc = lax.dot_general(tri, da, (((0,), (0,)), ((), ())), precision=HIGHEST, preferred_element_type=F32)
        ddt = a_neg * rc + _head_sum(t3_ref[...], etv)
        ddtraw = ddt * _sigmoid(dtpre)
        ddt_ref[...] = ddtraw.astype(BF16)
        ddtb_ref[...] += jnp.sum(ddtraw, axis=0, keepdims=True)
        dalog_ref[...] += jnp.sum(dt * rc, axis=0, keepdims=True) * a_neg
        ddsk_ref[...] += aux[1:2, :]

    row = lambda b, c: b * nc + (nc - 1 - c)
    full = lambda w: pl.BlockSpec((L, w), lambda b, c: (row(b, c), 0))
    zspec = pl.BlockSpec((L, di), lambda b, c: (row(b, c), z_blk))
    vec = lambda w: pl.BlockSpec((1, w), lambda b, c: (0, 0))
    slab = lambda shape: pltpu.VMEM(shape, F32)
    return pl.pallas_call(
        body, name="ssd_bwd", grid=(nb, nc),
        in_specs=[full(di), full(di), zspec, full(di),
                  pl.BlockSpec((L, gn_w), lambda b, c: (row(b, c), b_blk)),
                  pl.BlockSpec((L, gn_w), lambda b, c: (row(b, c), b_blk + 1)),
                  full(LANES),
                  pl.BlockSpec((1, N, di), lambda b, c: (row(b, c), 0, 0)),
                  pl.BlockSpec(memory_space=pl.ANY),
                  vec(LANES), vec(LANES), vec(di), vec(di),
                  pl.BlockSpec((L, L), lambda b, c: (0, 0)),
                  pl.BlockSpec((LANES, di), lambda b, c: (0, 0)),
                  pl.BlockSpec((di, LANES), lambda b, c: (0, 0))],
        out_specs=[zspec, full(LANES), full(dc), vec(di), vec(LANES), vec(LANES), vec(LANES)],
        out_shape=[jax.ShapeDtypeStruct(dproj.shape, dproj.dtype), jax.ShapeDtypeStruct((t, LANES), BF16),
                   jax.ShapeDtypeStruct((t, dc), F32), jax.ShapeDtypeStruct((1, di), F32),
                   jax.ShapeDtypeStruct((1, LANES), F32), jax.ShapeDtypeStruct((1, LANES), F32),
                   jax.ShapeDtypeStruct((1, LANES), F32)],
        scratch_shapes=[slab((N, di)), slab((L, di)), slab((L, di)), slab((L, di)), slab((L, di)), slab((L, di)),
                        slab((L, di)), slab((8, di))],
        input_output_aliases={8: 0},
        compiler_params=_params(("arbitrary", "arbitrary")))(
            dyan, ypre, proj, xbc, xbc, xbc, dtraw, hp, dproj, dtb, alog, dskx, gn, tril, expand, expand_t)


def _conv_bwd(proj, dxbc, dproj, conv_w, conv_b, nb, s, col0, cb=256):
    n_blk, w_spec = _conv_w_spec(conv_w, cb, 0)
    blk0 = _col_block(col0, cb)
    rc = min(CONV_ROWS, s)

    def body(x_ref, dy_ref, _, w_ref, b_ref, dx_ref, dw_ref, db_ref, dacc_ref):
        @pl.when(pl.program_id(1) == 0)
        def _():
            dw_ref[...] = jnp.zeros_like(dw_ref)
            db_ref[...] = jnp.zeros_like(db_ref)

        wts, bias = _conv_weights(w_ref), b_ref[...]
        fold = lambda v: v.reshape(rc // 8, 8, cb).sum(axis=0)
        db8 = jnp.zeros((8, cb), F32)
        dw8 = [jnp.zeros((8, cb), F32) for _ in range(CONV_WIDTH)]
        for t0 in range(0, s, rc):
            taps = _conv_taps(x_ref, t0, rc)
            acc = _conv_pre(taps, wts, bias)
            sg = _sigmoid(acc)
            dacc = dy_ref[t0:t0 + rc, :] * sg * (1.0 + acc * (1.0 - sg))
            dacc_ref[t0:t0 + rc, :] = dacc
            db8 = db8 + fold(dacc)
            dw8 = [dw8[j] + fold(dacc * taps[j]) for j in range(CONV_WIDTH)]
        db_ref[...] += jnp.sum(db8, axis=0, keepdims=True)
        for j in range(CONV_WIDTH):
            dw_ref[0, CONV_WIDTH - 1 - j:CONV_WIDTH - j, :] += jnp.sum(dw8[j], axis=0, keepdims=True)
        for t0 in range(0, s, rc):
            if t0 + rc < s:
                n = rc + CONV_HALO
                win = dacc_ref[t0:t0 + n, :]
                ups = [win[:rc]] + [pltpu.roll(win, n - j, 0)[:rc] for j in range(1, CONV_WIDTH)]
            else:
                cur = dacc_ref[t0:t0 + rc, :]
                row = lax.broadcasted_iota(jnp.int32, cur.shape, 0)
                ups = [cur] + [_shift_up(cur, j, row) for j in range(1, CONV_WIDTH)]
            dx = ups[0] * wts[0]
            for j in range(1, CONV_WIDTH):
                dx = dx + ups[j] * wts[j]
            dx_ref[t0:t0 + rc, :] = dx.astype(BF16)

    return pl.pallas_call(
        body, name="conv_bwd", grid=(n_blk, nb),
        in_specs=[pl.BlockSpec((s, cb), lambda j, b: (b, blk0 + j)), pl.BlockSpec((s, cb), lambda j, b: (b, j)),
                  pl.BlockSpec(memory_space=pl.ANY), w_spec, pl.BlockSpec((1, cb), lambda j, b: (0, j))],
        out_specs=[pl.BlockSpec((s, cb), lambda j, b: (b, blk0 + j)), w_spec, pl.BlockSpec((1, cb), lambda j, b: (0, j))],
        out_shape=[jax.ShapeDtypeStruct(dproj.shape, dproj.dtype), jax.ShapeDtypeStruct(conv_w.shape, F32),
                   jax.ShapeDtypeStruct(conv_b.shape, F32)],
        scratch_shapes=[pltpu.VMEM((s, cb), F32)],
        input_output_aliases={2: 0},
        compiler_params=_params(("parallel", "arbitrary")))(proj, dxbc, dproj, conv_w, conv_b)


def _local_step(x, p, tgt, wg, small, rest_weights, early_grads, w_in_grad):
    nb, s, d = x.shape
    t = nb * s
    gn_w = SSM_GROUPS * D_STATE
    dc = N_SHARD * wg["conv_w"].shape[2]
    di = dc - 2 * gn_w
    nh = di // HEAD_DIM
    pgd = d // N_POOL
    x2 = x.reshape(t, d)
    p2 = p.reshape(t, p.shape[-1])
    tgt2 = tgt.reshape(t, d)

    w_main, w_dt = _regroup_w_in(wg["w_in"], d, di, dc, nh)
    c_g, c_z, c_xbc, c_uz = 0, 2 * d, 2 * d + di, 2 * d + di + dc
    n_main = w_main.shape[1]

    pad_h = lambda v: jnp.pad(v.reshape(1, nh).astype(F32), ((0, 0), (0, LANES - nh)))
    dtb, alog = pad_h(small["dt_bias"]), pad_h(small["a_log"])
    dskx = jnp.repeat(small["d_skip"].reshape(1, nh).astype(F32), HEAD_DIM, axis=1)
    vec = lambda v: v.reshape(1, -1).astype(F32)
    norm_g, gn, conv_b = vec(small["norm_g"]), vec(small["gnorm_g"]), vec(small["conv_b"])
    mix_b, scale = vec(small["pool_mix_b"]), vec(small["pool_scale"])
    ple_g, final_g = vec(small["ple_norm_g"]), vec(small["final_g"])
    conv_w = wg["conv_w"]

    proj, dtraw, h = _inproj(x2, norm_g, w_main, w_dt)
    xbc = _conv_fwd(proj, conv_w, conv_b, nb, s, c_xbc)
    ypre, yan, hp = _ssd_fwd(proj, xbc, dtraw, dtb, alog, dskx, gn, nb, s, di, c_z)
    wr = rest_weights(yan)
    mix_w = wr["pool_mix_w"].reshape(N_SHARD, N_POOL, pgd // N_SHARD, pgd)
    rows = lambda v: v.reshape(-1, v.shape[-1])
    wa, wb, wo, wpg = rows(wr["w_branch_a"]), rows(wr["w_branch_b"]), rows(wr["w_out"]), rows(wr["w_ple_gate"])
    wup = wr["w_ple_up"]
    ybp = _pool_fwd(proj, mix_w, mix_b, scale, nb, s, c_uz)
    ya = _mm(yan, wa, "branch_a")
    yb = _mm(ybp, wb, "branch_b")
    merged, hn, dpre, dpu, x1, dx2, loss, d_final_g = _mid_fwd(
        ya, yb, proj, c_g, x2, p2, tgt2, wo, wpg, wup, ple_g, final_g)

    d_wpg = _mm_tn(hn, dpre, "d_w_ple_gate")
    d_wup = _mm_tn(p2, dpu, "d_w_ple_up", tn=wup.shape[-1], col_blocks=True)
    dx1, dya, dyb, dproj, d_ple_g = _mid_bwd(dpre, dx2, x1, ya, yb, proj, c_g, wpg, wo, ple_g, n_main)
    d_wo = _mm_tn(merged, dx1, "d_w_out")
    d_wa = _mm_tn(yan, dya, "d_w_branch_a")
    d_wb = _mm_tn(ybp, dyb, "d_w_branch_b")
    dyan = _mm_nt(dya, wa, "d_y_a")
    dybp = _mm_nt(dyb, wb, "d_y_b")
    dproj, d_mix_w, d_mix_b, d_scale = _pool_bwd(proj, dybp, dproj, mix_w, mix_b, scale, nb, s, c_uz)
    shard_major = lambda v: v.reshape(N_SHARD, v.shape[0] // N_SHARD, v.shape[1])
    early = dict(pool_mix_w=d_mix_w.reshape(N_SHARD, pgd, pgd), w_branch_a=shard_major(d_wa),
                 w_branch_b=shard_major(d_wb), w_out=shard_major(d_wo), w_ple_gate=shard_major(d_wpg),
                 w_ple_up=d_wup)
    token = early_grads(early)
    dproj, ddt, dxbc, d_gn, d_dsk, d_alog, d_dtb = _ssd_bwd(
        dyan, ypre, proj, xbc, dtraw, hp, dproj, dtb + token[0:1, 0:1], alog, dskx, gn, nb, s, di, c_z)
    dproj, d_conv_w, d_conv_b = _conv_bwd(proj, dxbc, dproj, conv_w, conv_b, nb, s, c_xbc)
    d_wmain = _mm_tn(h, dproj, "d_w_in")
    d_wdt = _mm_tn(h, ddt, "d_w_dt")
    d_w_in = _ungroup_w_in(d_wmain, d_wdt, d, di, dc, nh)
    token = w_in_grad(d_w_in)
    gx, d_norm_g = _in_bwd(dproj, w_main, ddt, w_dt, x2, dx1, norm_g, token)

    grads = dict(norm_g=d_norm_g, w_in=d_w_in, conv_w=d_conv_w, conv_b=d_conv_b, dt_bias=d_dtb[:, :nh],
                 a_log=d_alog[:, :nh], d_skip=d_dsk[:, :nh], gnorm_g=d_gn, pool_mix_b=d_mix_b, pool_scale=d_scale,
                 ple_norm_g=d_ple_g, final_g=d_final_g, **early)
    return loss[0, 0], gx.reshape(nb, s, d), grads


def _place():
    return lax.axis_index("x"), lax.axis_index("y"), lax.axis_index("c")


def _other_chips(x, y):
    return [(1 - x, y), (x, 1 - y), (1 - x, 1 - y)]


def _halves(c, rows, align):
    rh = rows // 2
    assert rows % 2 == 0 and rh % align == 0, (rows, align)
    return (pl.ds(pl.multiple_of(c * rh, align), rh), pl.ds(pl.multiple_of((1 - c) * rh, align), rh))


HBM = pl.BlockSpec(memory_space=pl.ANY)


def _into_slot(w2, k, dtype, name):
    rows, cols = w2.shape
    rb = _tile(rows, 256)

    def body(k_ref, w_ref, o_ref):
        o_ref[0] = w_ref[...].astype(dtype)

    return pl.pallas_call(
        body, name=name,
        grid_spec=pltpu.PrefetchScalarGridSpec(
            num_scalar_prefetch=1, grid=(rows // rb,),
            in_specs=[pl.BlockSpec((rb, cols), lambda i, k_ref: (i, 0))],
            out_specs=pl.BlockSpec((1, rb, cols), lambda i, k_ref: (k_ref[0], i, 0))),
        out_shape=jax.ShapeDtypeStruct((N_SHARD, rows, cols), dtype),
        compiler_params=_params(("parallel",)))(k.reshape(1), w2)


def _gather_weights(split, whole):
    n_split, n_all = len(split), len(split) + len(whole)

    def body(*refs):
        bufs = refs[n_all:2 * n_all]
        send_sems, recv_sems = refs[2 * n_all:]
        x, y, c = _place()
        k = 2 * x + y
        chips = _other_chips(x, y)

        def copy(idx, block, to):
            return pltpu.make_async_remote_copy(src_ref=block, dst_ref=block, send_sem=send_sems.at[idx],
                                                recv_sem=recv_sems.at[idx], device_id=to, device_id_type=MESH)

        def block(i, shard, rows):
            return bufs[i].at[shard, rows] if i < n_split else bufs[i].at[shard]

        def sem(i, j):
            return 6 * i + j if i < n_split else 6 * n_split + 3 * (i - n_split) + j

        started = []
        for i in range(n_all):
            mine, _ = _halves(c, bufs[i].shape[1], 16) if i < n_split else (None, None)
            for j, (px, py) in enumerate(chips):
                started.append(copy(sem(i, j), block(i, k, mine), (px, py, c)))
                started[-1].start()
        for i in range(n_all):
            mine, _ = _halves(c, bufs[i].shape[1], 16) if i < n_split else (None, None)
            for j, (px, py) in enumerate(chips):
                landed = block(i, 2 * px + py, mine)
                copy(sem(i, j), landed, (px, py, c)).wait_recv()
                if i < n_split:
                    started.append(copy(sem(i, 3 + j), landed, (x, y, 1 - c)))
                    started[-1].start()
        for i in range(n_split):
            _, theirs = _halves(c, bufs[i].shape[1], 16)
            for j, (px, py) in enumerate(chips):
                copy(sem(i, 3 + j), block(i, 2 * px + py, theirs), (x, y, 1 - c)).wait_recv()
        for cp in started:
            cp.wait_send()

    arrays = list(split) + list(whole)
    n_sem = 6 * n_split + 3 * len(whole)
    return pl.pallas_call(
        body, name="gather_weights",
        in_specs=[HBM] * n_all, out_specs=[HBM] * n_all,
        out_shape=[jax.ShapeDtypeStruct(a.shape, a.dtype) for a in arrays],
        input_output_aliases={i: i for i in range(n_all)},
        scratch_shapes=[pltpu.SemaphoreType.DMA((n_sem,)), pltpu.SemaphoreType.DMA((n_sem,))],
    )(*arrays)


def _swap_halves(gs):
    n = len(gs)

    def body(*refs):
        ins, outs, send_sems, recv_sems = refs[:n], refs[n:2 * n], refs[2 * n], refs[2 * n + 1]
        x, y, c = _place()
        copies = []
        for i in range(n):
            _, theirs = _halves(c, gs[i].shape[1], 8)
            cp = pltpu.make_async_remote_copy(src_ref=ins[i].at[:, theirs], dst_ref=outs[i], send_sem=send_sems.at[i],
                                              recv_sem=recv_sems.at[i], device_id=(x, y, 1 - c), device_id_type=MESH)
            cp.start()
            copies.append(cp)
        for cp in copies:
            cp.wait()

    return pl.pallas_call(
        body, name="swap_halves", in_specs=[HBM] * n, out_specs=[HBM] * n,
        out_shape=[jax.ShapeDtypeStruct((g.shape[0], g.shape[1] // 2, g.shape[2]), g.dtype) for g in gs],
        scratch_shapes=[pltpu.SemaphoreType.DMA((n,)), pltpu.SemaphoreType.DMA((n,))],
    )(*gs)


def _join_halves(vs):
    n = len(vs)

    def body(*refs):
        bufs, send_sems, recv_sems = refs[n:2 * n], refs[2 * n], refs[2 * n + 1]
        x, y, c = _place()

        def copy(i, rows):
            return pltpu.make_async_remote_copy(src_ref=bufs[i].at[rows], dst_ref=bufs[i].at[rows],
                                                send_sem=send_sems.at[i], recv_sem=recv_sems.at[i],
                                                device_id=(x, y, 1 - c), device_id_type=MESH)

        halves = [_halves(c, bufs[i].shape[0], 8) for i in range(n)]
        sends = [copy(i, halves[i][0]) for i in range(n)]
        for cp in sends:
            cp.start()
        for i in range(n):
            copy(i, halves[i][1]).wait_recv()
        for cp in sends:
            cp.wait_send()

    return pl.pallas_call(
        body, name="join_halves", in_specs=[HBM] * n, out_specs=[HBM] * n,
        out_shape=[jax.ShapeDtypeStruct(v.shape, v.dtype) for v in vs],
        input_output_aliases={i: i for i in range(n)},
        scratch_shapes=[pltpu.SemaphoreType.DMA((n,)), pltpu.SemaphoreType.DMA((n,))],
    )(*vs)


SEM = pl.BlockSpec(memory_space=pltpu.SEMAPHORE)
IN_HBM = pl.BlockSpec(memory_space=pltpu.HBM)
SPLIT_EFFECT = pltpu.SideEffectType.DATAFLOW_SIDE_EFFECTING


def _split_copies(plan, refs, send_sems, recv_sems):
    pairs = []
    for idx, (src, dst, landing, to) in enumerate(plan(refs)):
        mk = lambda d: pltpu.make_async_remote_copy(src_ref=src, dst_ref=d, send_sem=send_sems.at[idx],
                                                    recv_sem=recv_sems.at[idx], device_id=to, device_id_type=MESH)
        pairs.append((mk(dst), mk(landing)))
    return pairs


def _split_start(name, bufs, after, plan, n_copies):
    n = len(bufs)

    def body(*refs):
        send_sems, recv_sems, token = refs[n + 1], refs[n + 2], refs[-1]
        for send, _ in _split_copies(plan, refs[:n], send_sems, recv_sems):
            send.start()
        token[...] = jnp.zeros_like(token)

    sems = pltpu.SemaphoreType.DMA((n_copies,))
    out = pl.pallas_call(
        body, name=name,
        in_specs=[IN_HBM] * n + [HBM],
        out_specs=[SEM, SEM] + [IN_HBM] * n + [pl.BlockSpec(memory_space=pltpu.VMEM)],
        out_shape=[sems, sems] + [pltpu.HBM(b.shape, b.dtype) for b in bufs] + [jax.ShapeDtypeStruct((8, LANES), F32)],
        input_output_aliases={i: 2 + i for i in range(n)},
        compiler_params=pltpu.CompilerParams(has_side_effects=SPLIT_EFFECT),
    )(*[pltpu.with_memory_space_constraint(b, pltpu.HBM) for b in bufs], after)
    return out[0], out[1], out[2:2 + n], out[-1]


def _split_wait(name, bufs, send_sems, recv_sems, after, plan):
    n = len(bufs)

    def body(*refs):
        for send, recv in _split_copies(plan, refs[:n], refs[n], refs[n + 1]):
            send.wait_send()
            recv.wait_recv()

    return pl.pallas_call(
        body, name=name,
        in_specs=[IN_HBM] * n + [SEM, SEM, HBM],
        out_specs=[IN_HBM] * n,
        out_shape=[pltpu.HBM(b.shape, b.dtype) for b in bufs],
        input_output_aliases={i: i for i in range(n)},
        compiler_params=pltpu.CompilerParams(has_side_effects=SPLIT_EFFECT),
    )(*bufs, send_sems, recv_sems, after)


def _gather_plan(n):
    def plan(refs):
        x, y, c = _place()
        k = 2 * x + y
        return [(refs[i].at[k], refs[i].at[k], refs[i].at[2 * px + py], (px, py, c))
                for i in range(n) for px, py in _other_chips(x, y)]
    return plan


def _scatter_plan(n):
    def plan(refs):
        x, y, c = _place()
        return [(refs[i].at[2 * px + py], refs[n + i].at[j], refs[n + i].at[j], (px, py, c))
                for i in range(n) for j, (px, py) in enumerate(_other_chips(x, y))]
    return plan


def _swap_sibling(vs):
    n = len(vs)

    def body(*refs):
        ins, outs, send_sems, recv_sems = refs[:n], refs[n:2 * n], refs[2 * n], refs[2 * n + 1]
        x, y, c = _place()
        copies = [pltpu.make_async_remote_copy(src_ref=ins[i], dst_ref=outs[i], send_sem=send_sems.at[i],
                                               recv_sem=recv_sems.at[i], device_id=(x, y, 1 - c), device_id_type=MESH)
                  for i in range(n)]
        for cp in copies:
            cp.start()
        for cp in copies:
            cp.wait()

    return pl.pallas_call(
        body, name="swap_sibling", in_specs=[HBM] * n, out_specs=[HBM] * n,
        out_shape=[jax.ShapeDtypeStruct(v.shape, v.dtype) for v in vs],
        scratch_shapes=[pltpu.SemaphoreType.DMA((n,)), pltpu.SemaphoreType.DMA((n,))],
    )(*vs)


def _to_bf16(g, name):
    _, rows, cols = g.shape
    rb = _tile(rows, 256)

    def body(g_ref, o_ref):
        o_ref[...] = g_ref[...].astype(BF16)

    spec = pl.BlockSpec((1, rb, cols), lambda j, i: (j, i, 0))
    return pl.pallas_call(
        body, name=name, grid=(N_SHARD, rows // rb), in_specs=[spec], out_specs=spec,
        out_shape=jax.ShapeDtypeStruct(g.shape, BF16),
        compiler_params=_params(("parallel", "parallel")))(g)


def _add_landed(g, landed, k, name):
    _, rows, cols = g.shape
    rb = _tile(rows, 256)

    def body(k_ref, g_ref, l_ref, o_ref):
        o_ref[...] = g_ref[0] + l_ref[0].astype(F32) + l_ref[1].astype(F32) + l_ref[2].astype(F32)

    return pl.pallas_call(
        body, name=name,
        grid_spec=pltpu.PrefetchScalarGridSpec(
            num_scalar_prefetch=1, grid=(rows // rb,),
            in_specs=[pl.BlockSpec((1, rb, cols), lambda i, k_ref: (k_ref[0], i, 0)),
                      pl.BlockSpec((N_SHARD - 1, rb, cols), lambda i, k_ref: (0, i, 0))],
            out_specs=pl.BlockSpec((rb, cols), lambda i, k_ref: (i, 0))),
        out_shape=jax.ShapeDtypeStruct((rows, cols), F32),
        compiler_params=_params(("parallel",)))(k.reshape(1), g, landed)


def _allreduce_small(v):
    rows = v.shape[0]

    def body(v_ref, o_ref, buf_ref, send_sems, recv_sems):
        x, y, c = _place()
        me = 4 * x + 2 * y + c
        buf_ref[me] = v_ref[...]
        copies = []
        for rel in range(1, 8):
            peer = (x ^ (rel >> 2), y ^ ((rel >> 1) & 1), c ^ (rel & 1))
            cp = pltpu.make_async_remote_copy(src_ref=v_ref, dst_ref=buf_ref.at[me], send_sem=send_sems.at[rel - 1],
                                              recv_sem=recv_sems.at[rel - 1], device_id=peer, device_id_type=MESH)
            cp.start()
            copies.append(cp)
        for rel in range(1, 8):
            peer_id = me ^ rel
            pltpu.make_async_remote_copy(src_ref=v_ref, dst_ref=buf_ref.at[peer_id], send_sem=send_sems.at[rel - 1],
                                         recv_sem=recv_sems.at[rel - 1], device_id=(x, y, c),
                                         device_id_type=MESH).wait_recv()
        for cp in copies:
            cp.wait_send()
        acc = buf_ref[0]
        for i in range(1, 8):
            acc = acc + buf_ref[i]
        o_ref[...] = acc

    return pl.pallas_call(
        body, name="allreduce_small",
        in_specs=[pl.BlockSpec(memory_space=pltpu.VMEM)], out_specs=pl.BlockSpec(memory_space=pltpu.VMEM),
        out_shape=jax.ShapeDtypeStruct(v.shape, F32),
        scratch_shapes=[pltpu.VMEM((8, rows, LANES), F32), pltpu.SemaphoreType.DMA((7,)), pltpu.SemaphoreType.DMA((7,))],
    )(v)


def _add_pair(g, got, c, name):
    _, rh, cols = got.shape
    rb = _tile(rh, 256)
    nrb = rh // rb

    def body(c_ref, g_ref, got_ref, o_ref):
        o_ref[...] = (g_ref[...] + got_ref[...]).astype(BF16)

    spec = pl.BlockSpec((1, rb, cols), lambda j, i, c_ref: (j, i, 0))
    return pl.pallas_call(
        body, name=name,
        grid_spec=pltpu.PrefetchScalarGridSpec(
            num_scalar_prefetch=1, grid=(N_SHARD, nrb),
            in_specs=[pl.BlockSpec((1, rb, cols), lambda j, i, c_ref: (j, c_ref[0] * nrb + i, 0)), spec],
            out_specs=spec),
        out_shape=jax.ShapeDtypeStruct(got.shape, BF16),
        compiler_params=_params(("parallel", "parallel")))(c.reshape(1), g, got)


def _add_chips(g, got, landed, k, c, name):
    _, rh, cols = got.shape
    rb = _tile(rh, 256)
    nrb = rh // rb

    def body(kc_ref, g_ref, got_ref, l_ref, o_ref):
        own = g_ref[0] + got_ref[0]
        o_ref[...] = own + l_ref[0].astype(F32) + l_ref[1].astype(F32) + l_ref[2].astype(F32)

    half_c = lambda i, kc: (kc[1] * nrb + i, 0)
    return pl.pallas_call(
        body, name=name,
        grid_spec=pltpu.PrefetchScalarGridSpec(
            num_scalar_prefetch=1, grid=(nrb,),
            in_specs=[pl.BlockSpec((1, rb, cols), lambda i, kc: (kc[0],) + half_c(i, kc)),
                      pl.BlockSpec((1, rb, cols), lambda i, kc: (kc[0], i, 0)),
                      pl.BlockSpec((N_SHARD - 1, rb, cols), lambda i, kc: (0, i, 0))],
            out_specs=pl.BlockSpec((rb, cols), half_c)),
        out_shape=jax.ShapeDtypeStruct((2 * rh, cols), F32),
        compiler_params=_params(("parallel",)))(jnp.stack([k, c]), g, got, landed)


def _adamw(wv, gs, m, v, name):
    rows, cols = wv.shape
    rb = _tile(rows, 256)
    c1 = 1.0 - ADAM_B1 ** ADAM_STEP
    c2 = 1.0 - ADAM_B2 ** ADAM_STEP
    n_g = len(gs)

    def body(*refs):
        w_ref, g_refs, (m_ref, v_ref, go_ref, d_ref, nm_ref, nv_ref) = refs[0], refs[1:1 + n_g], refs[1 + n_g:]
        gv = g_refs[0][...]
        for ref in g_refs[1:]:
            gv = gv + ref[...]
        go_ref[...] = gv
        nm = ADAM_B1 * m_ref[...] + (1.0 - ADAM_B1) * gv
        nv = ADAM_B2 * v_ref[...] + (1.0 - ADAM_B2) * (gv * gv)
        nm_ref[...] = nm
        nv_ref[...] = nv
        d_ref[...] = -ADAM_LR * ((nm / c1) / (jnp.sqrt(nv / c2) + ADAM_EPS) + ADAM_WD * w_ref[...])

    spec = pl.BlockSpec((rb, cols), lambda i: (i, 0))
    return pl.pallas_call(
        body, name=name, grid=(rows // rb,), in_specs=[spec] * (3 + n_g), out_specs=[spec] * 4,
        out_shape=[jax.ShapeDtypeStruct((rows, cols), F32)] * 4,
        compiler_params=_params(("parallel",)))(wv, *gs, m, v)


def _pack(flats):
    cat = jnp.concatenate([f.reshape(-1) for f in flats])
    n = cat.shape[0]
    rows = -(-n // (8 * LANES)) * 8
    return jnp.pad(cat, (0, rows * LANES - n)).reshape(rows, LANES)


def _unpack(packed, shapes):
    flat = packed.reshape(-1)
    out, off = [], 0
    for shp in shapes:
        n = 1
        for dim in shp:
            n *= dim
        out.append(flat[off:off + n].reshape(shp))
        off += n
    return out


def kernel(x, p, norm_g, w_in, conv_w, conv_b, dt_bias, a_log, d_skip, gnorm_g, pool_mix_w, pool_mix_b, pool_scale, w_branch_a, w_branch_b, w_out, ple_norm_g, w_ple_gate, w_ple_up, final_g, loss_target, m_norm_g, m_w_in, m_conv_w, m_conv_b, m_dt_bias, m_a_log, m_d_skip, m_gnorm_g, m_pool_mix_w, m_pool_mix_b, m_pool_scale, m_w_branch_a, m_w_branch_b, m_w_out, m_ple_norm_g, m_w_ple_gate, m_w_ple_up, m_final_g, v_norm_g, v_w_in, v_conv_w, v_conv_b, v_dt_bias, v_a_log, v_d_skip, v_gnorm_g, v_pool_mix_w, v_pool_mix_b, v_pool_scale, v_w_branch_a, v_w_branch_b, v_w_out, v_ple_norm_g, v_w_ple_gate, v_w_ple_up, v_final_g):
    wts = dict(norm_g=norm_g, w_in=w_in, conv_w=conv_w, conv_b=conv_b, dt_bias=dt_bias, a_log=a_log, d_skip=d_skip,
               gnorm_g=gnorm_g, pool_mix_w=pool_mix_w, pool_mix_b=pool_mix_b, pool_scale=pool_scale,
               w_branch_a=w_branch_a, w_branch_b=w_branch_b, w_out=w_out, ple_norm_g=ple_norm_g,
               w_ple_gate=w_ple_gate, w_ple_up=w_ple_up, final_g=final_g)
    mom_m = dict(norm_g=m_norm_g, w_in=m_w_in, conv_w=m_conv_w, conv_b=m_conv_b, dt_bias=m_dt_bias, a_log=m_a_log,
                 d_skip=m_d_skip, gnorm_g=m_gnorm_g, pool_mix_w=m_pool_mix_w, pool_mix_b=m_pool_mix_b,
                 pool_scale=m_pool_scale, w_branch_a=m_w_branch_a, w_branch_b=m_w_branch_b, w_out=m_w_out,
                 ple_norm_g=m_ple_norm_g, w_ple_gate=m_w_ple_gate, w_ple_up=m_w_ple_up, final_g=m_final_g)
    mom_v = dict(norm_g=v_norm_g, w_in=v_w_in, conv_w=v_conv_w, conv_b=v_conv_b, dt_bias=v_dt_bias, a_log=v_a_log,
                 d_skip=v_d_skip, gnorm_g=v_gnorm_g, pool_mix_w=v_pool_mix_w, pool_mix_b=v_pool_mix_b,
                 pool_scale=v_pool_scale, w_branch_a=v_w_branch_a, w_branch_b=v_w_branch_b, w_out=v_w_out,
                 ple_norm_g=v_ple_norm_g, w_ple_gate=v_w_ple_gate, w_ple_up=v_w_ple_up, final_g=v_final_g)
    c = lax.axis_index("c")
    k = 2 * lax.axis_index("x") + lax.axis_index("y")
    flat2 = lambda a: a.reshape(-1, a.shape[-1])

    slots = {n: _into_slot(flat2(wts[n]), k, BF16, "slot_" + n) for n in BIG}
    w_in_g, conv_g = _gather_weights([slots["w_in"]], [_into_slot(flat2(conv_w), k, F32, "slot_conv_w")])
    n_rest = len(REST)
    gsend, grecv, gbufs, gtoken = _split_start("gather_rest_start", [slots[n] for n in REST], conv_g,
                                               _gather_plan(n_rest), 3 * n_rest)

    def rest_weights(after):
        return dict(zip(REST, _split_wait("gather_rest_wait", gbufs, gsend, grecv, after, _gather_plan(n_rest))))

    flying = {}

    def early_grads(early):
        sends = [_to_bf16(early[n], "bf16_" + n) for n in REST]
        lands = [pltpu.with_memory_space_constraint(lax.empty((N_SHARD - 1,) + v.shape[1:], BF16), pltpu.HBM)
                 for v in sends]
        ssend, srecv, sbufs, stoken = _split_start("scatter_rest_start", sends + lands, early[REST[0]],
                                                   _scatter_plan(n_rest), 3 * n_rest)
        flying.update(send=ssend, recv=srecv, bufs=sbufs)
        return stoken

    def w_in_grad(g_w_in):
        got = _swap_halves([g_w_in])[0]
        pair = _add_pair(g_w_in, got, c, "add_pair_w_in")
        land = pltpu.with_memory_space_constraint(lax.empty((N_SHARD - 1,) + pair.shape[1:], BF16), pltpu.HBM)
        wsend, wrecv, wbufs, wtoken = _split_start("scatter_w_in_start", [pair, land], got, _scatter_plan(1), 3)
        flying.update(w_send=wsend, w_recv=wrecv, w_bufs=wbufs, w_got=got)
        return wtoken

    small = {n: wts[n] for n in SMALL}
    small["norm_g"] = norm_g + gtoken[0, 0]
    loss, grad_x, grads = _local_step(x, p[0], loss_target, dict(w_in=w_in_g, conv_w=conv_g), small,
                                      rest_weights, early_grads, w_in_grad)
    g_w_in = grads["w_in"]
    landed = _split_wait("scatter_w_in_wait", flying["w_bufs"], flying["w_send"], flying["w_recv"], grad_x,
                         _scatter_plan(1))[1]
    w_in_sum = _join_halves([_add_chips(g_w_in, flying["w_got"], landed, k, c, "add_chips_w_in")])[0]

    sbufs = _split_wait("scatter_rest_wait", flying["bufs"], flying["send"], flying["recv"], g_w_in,
                        _scatter_plan(n_rest))
    mine = [_add_landed(grads[n], ld, k, "add_landed_" + n) for n, ld in zip(REST, sbufs[n_rest:])]
    theirs = _swap_sibling(mine)
    g_sums = dict(zip(REST, zip(mine, theirs)))
    g_sums["w_in"] = (w_in_sum,)

    conv_shape = flat2(conv_w).shape
    small_sum = _allreduce_small(_pack([grads[n] for n in SMALL] + [grads["conv_w"], loss]))
    small_shapes = [wts[n].shape for n in SMALL] + [(N_SHARD,) + conv_shape, (1,)]
    small_g = _unpack(small_sum, small_shapes)
    g_conv = lax.dynamic_index_in_dim(small_g[-2], k, axis=0, keepdims=False)

    outs = {}
    for n in BIG:
        vals = _adamw(flat2(wts[n]), g_sums[n], flat2(mom_m[n]), flat2(mom_v[n]), "adamw_" + n)
        for kind, val in zip(("grad", "delta", "new_m", "new_v"), vals):
            outs[kind, n] = val.reshape(wts[n].shape)
    names = SMALL + ("conv_w",)
    sm = _adamw(_pack([wts[n] for n in names]), (_pack(small_g[:len(SMALL)] + [g_conv]),),
                _pack([mom_m[n] for n in names]), _pack([mom_v[n] for n in names]), "adamw_small")
    sm_shapes = [wts[n].shape for n in names]
    for kind, val in zip(("grad", "delta", "new_m", "new_v"), sm):
        for n, piece in zip(names, _unpack(val, sm_shapes)):
            outs[kind, n] = piece
    return (small_g[-1][0], grad_x, *[outs[kind, n] for kind in ("grad", "delta", "new_m", "new_v") for n in WEIGHTS])
```

```python
import functools

import jax
import jax.numpy as jnp
from jax import lax
from jax.experimental import pallas as pl
from jax.experimental.pallas import tpu as pltpu

F32 = jnp.float32
BF16 = jnp.bfloat16
HIGHEST = lax.Precision.HIGHEST
MESH = pl.DeviceIdType.MESH

EPS = 1e-6
HEAD_DIM = 64
SSM_GROUPS = 4
D_STATE = 128
CONV_WIDTH = 4
CHUNK = 128
N_POOL = 4
LANES = 128
N_SHARD = 4

ADAM_LR = 0.001
ADAM_B1 = 0.9
ADAM_B2 = 0.999
ADAM_EPS = 1e-08
ADAM_WD = 0.01
ADAM_STEP = 10

BIG = ("w_in", "pool_mix_w", "w_branch_a", "w_branch_b", "w_out", "w_ple_gate", "w_ple_up")
REST = BIG[1:]
SMALL = ("norm_g", "conv_b", "dt_bias", "a_log", "d_skip", "gnorm_g", "pool_mix_b", "pool_scale",
         "ple_norm_g", "final_g")
WEIGHTS = ("norm_g", "w_in", "conv_w", "conv_b", "dt_bias", "a_log", "d_skip", "gnorm_g", "pool_mix_w",
           "pool_mix_b", "pool_scale", "w_branch_a", "w_branch_b", "w_out", "ple_norm_g", "w_ple_gate",
           "w_ple_up", "final_g")


def _params(sem=None, vmem_mb=56):
    kw = dict(vmem_limit_bytes=vmem_mb << 20)
    if sem is not None:
        kw["dimension_semantics"] = sem
    return pltpu.CompilerParams(**kw)


def _sigmoid(v):
    return 1.0 / (1.0 + jnp.exp(-v))


def _softplus(v):
    return jnp.maximum(v, 0.0) + jnp.log1p(jnp.exp(-jnp.abs(v)))


def _bdot(a, b):
    return jnp.dot(a.astype(BF16), b.astype(BF16), preferred_element_type=F32)


def _bdot_nt(a, b):
    return lax.dot_general(a.astype(BF16), b.astype(BF16), (((1,), (1,)), ((), ())), preferred_element_type=F32)


def _bdot_tn(a, b):
    return lax.dot_general(a.astype(BF16), b.astype(BF16), (((0,), (0,)), ((), ())), preferred_element_type=F32)


def _col_block(col0, width):
    assert col0 % width == 0, (col0, width)
    return col0 // width


def _tile(n, cap):
    if n <= cap:
        return n
    best = None
    for cand in range(8, cap + 1, 8):
        if n % cand == 0:
            best = cand
    assert best is not None, (n, cap)
    return best


def _shift_down(v, j, row):
    return jnp.where(row >= j, pltpu.roll(v, j, 0), 0.0)


def _shift_up(v, j, row):
    n = v.shape[0]
    return jnp.where(row < n - j, pltpu.roll(v, n - j, 0), 0.0)


def _mm(a, w, name, tm=1024, tn=1024):
    t, k = a.shape
    tm = min(tm, t)
    blocked = w.ndim == 3
    if blocked:
        nblk, _, tn = w.shape
        n = nblk * tn
        w_spec = pl.BlockSpec((1, k, tn), lambda i, j: (j, 0, 0))
    else:
        n = w.shape[1]
        tn = min(tn, n)
        w_spec = pl.BlockSpec((k, tn), lambda i, j: (0, j))

    def body(a_ref, w_ref, o_ref):
        wv = w_ref[0] if blocked else w_ref[...]
        o_ref[...] = _bdot(a_ref[...], wv)

    return pl.pallas_call(
        body, name=name, grid=(t // tm, n // tn),
        in_specs=[pl.BlockSpec((tm, k), lambda i, j: (i, 0)), w_spec],
        out_specs=pl.BlockSpec((tm, tn), lambda i, j: (i, j)),
        out_shape=jax.ShapeDtypeStruct((t, n), F32),
        compiler_params=_params(("parallel", "parallel")))(a, w)


def _mm_nt(a, w, name, tm=1024, tk=1024):
    t, k = a.shape
    n = w.shape[0]
    tm, tk = min(tm, t), min(tk, k)

    def body(a_ref, w_ref, o_ref):
        kk = pl.program_id(1)
        part = _bdot_nt(a_ref[...], w_ref[...])

        @pl.when(kk == 0)
        def _():
            o_ref[...] = part

        @pl.when(kk > 0)
        def _():
            o_ref[...] += part

    return pl.pallas_call(
        body, name=name, grid=(t // tm, k // tk),
        in_specs=[pl.BlockSpec((tm, tk), lambda i, j: (i, j)), pl.BlockSpec((n, tk), lambda i, j: (0, j))],
        out_specs=pl.BlockSpec((tm, n), lambda i, j: (i, 0)),
        out_shape=jax.ShapeDtypeStruct((t, n), F32),
        compiler_params=_params(("parallel", "arbitrary")))(a, w)


def _mm_tn(a, b, name, tn=1024, tk=1024, col_blocks=False):
    t, m = a.shape
    n = b.shape[1]
    tn, tk = min(tn, n), min(tk, t)

    def body(a_ref, b_ref, o_ref):
        kk = pl.program_id(1)
        part = _bdot_tn(a_ref[...], b_ref[...])
        part = part[None] if col_blocks else part

        @pl.when(kk == 0)
        def _():
            o_ref[...] = part

        @pl.when(kk > 0)
        def _():
            o_ref[...] += part

    if col_blocks:
        out_spec = pl.BlockSpec((1, m, tn), lambda j, kk: (j, 0, 0))
        out_shape = jax.ShapeDtypeStruct((n // tn, m, tn), F32)
    else:
        out_spec = pl.BlockSpec((m, tn), lambda j, kk: (0, j))
        out_shape = jax.ShapeDtypeStruct((m, n), F32)
    return pl.pallas_call(
        body, name=name, grid=(n // tn, t // tk),
        in_specs=[pl.BlockSpec((tk, m), lambda j, kk: (kk, 0)), pl.BlockSpec((tk, tn), lambda j, kk: (kk, j))],
        out_specs=out_spec, out_shape=out_shape,
        compiler_params=_params(("parallel", "arbitrary")))(a, b)


def _w_in_pieces(d, di, dc, nh, shard_w):
    pgd = d // N_POOL
    o_dt, o_u = di + dc, di + dc + nh
    o_zp, o_ga, o_gb = o_u + d, o_u + 2 * d, o_u + 3 * d
    c_z, c_uz = 2 * d, 2 * d + di + dc
    runs = [(False, 0, o_ga, d), (False, d, o_gb, d), (False, c_z, 0, di + dc), (True, 0, o_dt, nh)]
    for g in range(N_POOL):
        runs.append((False, c_uz + 2 * g * pgd, o_u + g * pgd, pgd))
        runs.append((False, c_uz + (2 * g + 1) * pgd, o_zp + g * pgd, pgd))
    pieces = []
    for is_dt, dst, src, n in runs:
        while n > 0:
            k, off = divmod(src, shard_w)
            m = min(n, shard_w - off)
            pieces.append((is_dt, dst, k, off, m))
            dst, src, n = dst + m, src + m, n - m
    return pieces


def _regroup_w_in(w_sh, d, di, dc, nh, rb=256):
    _, rows, sw = w_sh.shape
    n_main = 4 * d + di + dc
    pieces = _w_in_pieces(d, di, dc, nh, sw)
    rb = min(rb, rows)

    def body(w_ref, main_ref, dt_ref):
        dt_ref[...] = jnp.zeros_like(dt_ref)
        for is_dt, dst, k, off, m in pieces:
            out = dt_ref if is_dt else main_ref
            out[:, dst:dst + m] = w_ref[k, :, off:off + m]

    return pl.pallas_call(
        body, name="regroup_w_in", grid=(rows // rb,),
        in_specs=[pl.BlockSpec((N_SHARD, rb, sw), lambda i: (0, i, 0))],
        out_specs=[pl.BlockSpec((rb, n_main), lambda i: (i, 0)), pl.BlockSpec((rb, LANES), lambda i: (i, 0))],
        out_shape=[jax.ShapeDtypeStruct((rows, n_main), w_sh.dtype), jax.ShapeDtypeStruct((rows, LANES), w_sh.dtype)],
        compiler_params=_params(("parallel",)))(w_sh)


def _ungroup_w_in(d_main, d_dt, d, di, dc, nh, rb=128):
    rows, n_main = d_main.shape
    sw = (n_main + nh) // N_SHARD
    pieces = _w_in_pieces(d, di, dc, nh, sw)
    rb = min(rb, rows)

    def body(main_ref, dt_ref, o_ref):
        for is_dt, dst, k, off, m in pieces:
            src = dt_ref if is_dt else main_ref
            o_ref[k, :, off:off + m] = src[:, dst:dst + m]

    return pl.pallas_call(
        body, name="ungroup_w_in", grid=(rows // rb,),
        in_specs=[pl.BlockSpec((rb, n_main), lambda i: (i, 0)), pl.BlockSpec((rb, LANES), lambda i: (i, 0))],
        out_specs=pl.BlockSpec((N_SHARD, rb, sw), lambda i: (0, i, 0)),
        out_shape=jax.ShapeDtypeStruct((N_SHARD, rows, sw), F32),
        compiler_params=_params(("parallel",)))(d_main, d_dt)


def _inproj(x2, norm_g, w_main, w_dt, tm=1024, tn=1024):
    t, d = x2.shape
    n = w_main.shape[1]
    tm, tn = min(tm, t), min(tn, n)

    def body(x_ref, g_ref, w_ref, wdt_ref, proj_ref, dt_ref, h_ref):
        @pl.when(pl.program_id(1) == 0)
        def _():
            xv = x_ref[...]
            r = lax.rsqrt(jnp.mean(xv * xv, axis=-1, keepdims=True) + EPS)
            h = (xv * r * g_ref[...]).astype(BF16)
            h_ref[...] = h
            dt_ref[...] = jnp.dot(h, wdt_ref[...].astype(BF16), preferred_element_type=F32)

        proj_ref[...] = jnp.dot(h_ref[...], w_ref[...].astype(BF16), preferred_element_type=F32)

    return pl.pallas_call(
        body, name="inproj", grid=(t // tm, n // tn),
        in_specs=[pl.BlockSpec((tm, d), lambda i, j: (i, 0)), pl.BlockSpec((1, d), lambda i, j: (0, 0)),
                  pl.BlockSpec((d, tn), lambda i, j: (0, j)), pl.BlockSpec((d, LANES), lambda i, j: (0, 0))],
        out_specs=[pl.BlockSpec((tm, tn), lambda i, j: (i, j)), pl.BlockSpec((tm, LANES), lambda i, j: (i, 0)),
                   pl.BlockSpec((tm, d), lambda i, j: (i, 0))],
        out_shape=[jax.ShapeDtypeStruct((t, n), F32), jax.ShapeDtypeStruct((t, LANES), F32),
                   jax.ShapeDtypeStruct((t, d), BF16)],
        compiler_params=_params(("parallel", "arbitrary")))(x2, norm_g, w_main, w_dt)


def _conv_w_spec(conv_w, cb, j_axis):
    sw = conv_w.shape[2]
    assert sw % cb == 0, (sw, cb)
    per = sw // cb
    return N_SHARD * per, pl.BlockSpec((1, CONV_WIDTH, cb), lambda *ij: (ij[j_axis] // per, 0, ij[j_axis] % per))


CONV_ROWS = 64
CONV_HALO = 8


def _conv_taps(x_ref, t0, rc):
    if t0 == 0:
        cur = x_ref[0:rc, :]
        row = lax.broadcasted_iota(jnp.int32, cur.shape, 0)
        return [cur] + [_shift_down(cur, j, row) for j in range(1, CONV_WIDTH)]
    ext = x_ref[t0 - CONV_HALO:t0 + rc, :]
    return [ext[CONV_HALO:]] + [pltpu.roll(ext, j, 0)[CONV_HALO:] for j in range(1, CONV_WIDTH)]


def _conv_weights(w_ref):
    return [w_ref[0, CONV_WIDTH - 1 - j:CONV_WIDTH - j, :] for j in range(CONV_WIDTH)]


def _conv_pre(taps, wts, bias):
    acc = bias + taps[0] * wts[0]
    for j in range(1, CONV_WIDTH):
        acc = acc + taps[j] * wts[j]
    return acc


def _conv_fwd(proj, conv_w, conv_b, nb, s, col0, cb=256):
    n_blk, w_spec = _conv_w_spec(conv_w, cb, 1)
    blk0 = _col_block(col0, cb)
    rc = min(CONV_ROWS, s)

    def body(x_ref, w_ref, b_ref, o_ref):
        wts, bias = _conv_weights(w_ref), b_ref[...]
        for t0 in range(0, s, rc):
            acc = _conv_pre(_conv_taps(x_ref, t0, rc), wts, bias)
            o_ref[t0:t0 + rc, :] = acc * _sigmoid(acc)

    return pl.pallas_call(
        body, name="conv_fwd", grid=(nb, n_blk),
        in_specs=[pl.BlockSpec((s, cb), lambda b, j: (b, blk0 + j)), w_spec, pl.BlockSpec((1, cb), lambda b, j: (0, j))],
        out_specs=pl.BlockSpec((s, cb), lambda b, j: (b, j)),
        out_shape=jax.ShapeDtypeStruct((nb * s, n_blk * cb), F32),
        compiler_params=_params(("parallel", "parallel")))(proj, conv_w, conv_b)


def _ssd_consts(di):
    r = lax.broadcasted_iota(jnp.int32, (CHUNK, CHUNK), 0)
    c = lax.broadcasted_iota(jnp.int32, (CHUNK, CHUNK), 1)
    tril = (r >= c).astype(F32)
    head = lax.broadcasted_iota(jnp.int32, (LANES, di), 0)
    chan = lax.broadcasted_iota(jnp.int32, (LANES, di), 1) // HEAD_DIM
    expand = (head == chan).astype(BF16)
    return tril, expand, expand.T


def _expand(v, e, terms=3):
    acc = None
    for _ in range(terms):
        vb = v.astype(BF16)
        part = jnp.dot(vb, e, preferred_element_type=F32)
        acc = part if acc is None else acc + part
        v = v - vb.astype(F32)
    return acc


def _head_sum(t, et, terms=2):
    acc = None
    for _ in range(terms):
        tb = t.astype(BF16)
        part = jnp.dot(tb, et, preferred_element_type=F32)
        acc = part if acc is None else acc + part
        t = t - tb.astype(F32)
    return acc


def _ssd_scalars(dtr_ref, dtb_ref, alog_ref, tri):
    dtpre = dtr_ref[...] + dtb_ref[...]
    dt = _softplus(dtpre)
    a_neg = -jnp.exp(alog_ref[...])
    a_dt = dt * a_neg
    a_cs = jnp.dot(tri, a_dt, precision=HIGHEST, preferred_element_type=F32)
    a_cst = lax.dot_general(a_dt, tri, (((0,), (1,)), ((), ())), precision=HIGHEST, preferred_element_type=F32)
    return dtpre, dt, a_neg, a_cs, a_cst


def _ssd_fwd(proj, xbc, dtraw, dtb, alog, dskx, gn, nb, s, di, z_col0):
    t = nb * s
    nc = s // CHUNK
    hpg = di // HEAD_DIM // SSM_GROUPS
    gw = di // SSM_GROUPS
    gn_w = SSM_GROUPS * D_STATE
    b_blk = _col_block(di, gn_w)
    z_blk = _col_block(z_col0, di)
    L, P, N = CHUNK, HEAD_DIM, D_STATE
    tril, expand, _ = _ssd_consts(di)

    def body(z_ref, x_ref, b_ref, c_ref, dtr_ref, dtb_ref, alog_ref, dskx_ref, gn_ref, tril_ref, e_ref,
             ypre_ref, yan_ref, hp_ref, st_ref, yd_ref, xdt_ref):
        @pl.when(pl.program_id(1) == 0)
        def _():
            st_ref[...] = jnp.zeros_like(st_ref)

        hp_ref[0] = st_ref[...]
        tri = tril_ref[...]
        _, dt, _, a_cs, a_cst = _ssd_scalars(dtr_ref, dtb_ref, alog_ref, tri)
        ev = e_ref[...]
        a_exp = _expand(a_cs, ev)
        xv = x_ref[...]
        xdt = xv * _expand(dt, ev, terms=2)
        xdt_ref[...] = xdt
        a_last = a_exp[L - 1:L, :]
        xe = xdt * jnp.exp(a_last - a_exp)
        ea = jnp.exp(a_exp)
        e_last = jnp.exp(a_last)
        lower = tri > 0.5
        for g in range(SSM_GROUPS):
            gs = slice(g * gw, (g + 1) * gw)
            bg = b_ref[:, g * N:(g + 1) * N].astype(BF16)
            cg = c_ref[:, g * N:(g + 1) * N].astype(BF16)
            gm = _bdot_nt(cg, bg)
            ht = st_ref[:, gs]
            ch = _bdot(cg, ht)
            for e in range(hpg):
                h = g * hpg + e
                hs = slice(h * P, (h + 1) * P)
                decay = jnp.where(lower, jnp.exp(a_cs[:, h:h + 1] - a_cst[h:h + 1, :]), 0.0)
                yd_ref[:, hs] = _bdot(gm * decay, xdt_ref[:, hs])
            st_ref[:, gs] = ht * e_last[:, gs] + _bdot_tn(bg, xe[:, gs])
            ypre = yd_ref[:, gs] + ea[:, gs] * ch + xv[:, gs] * dskx_ref[:, gs]
            ypre_ref[:, gs] = ypre
            zv = z_ref[:, gs]
            v = ypre * zv * _sigmoid(zv)
            r = lax.rsqrt(jnp.mean(v * v, axis=-1, keepdims=True) + EPS)
            yan_ref[:, gs] = (v * r * gn_ref[:, gs]).astype(BF16)

    row = lambda b, c: b * nc + c
    vec = lambda w: pl.BlockSpec((1, w), lambda b, c: (0, 0))
    return pl.pallas_call(
        body, name="ssd_fwd", grid=(nb, nc),
        in_specs=[pl.BlockSpec((L, di), lambda b, c: (row(b, c), z_blk)),
                  pl.BlockSpec((L, di), lambda b, c: (row(b, c), 0)),
                  pl.BlockSpec((L, gn_w), lambda b, c: (row(b, c), b_blk)),
                  pl.BlockSpec((L, gn_w), lambda b, c: (row(b, c), b_blk + 1)),
                  pl.BlockSpec((L, LANES), lambda b, c: (row(b, c), 0)),
                  vec(LANES), vec(LANES), vec(di), vec(di),
                  pl.BlockSpec((L, L), lambda b, c: (0, 0)),
                  pl.BlockSpec((LANES, di), lambda b, c: (0, 0))],
        out_specs=[pl.BlockSpec((L, di), lambda b, c: (row(b, c), 0)),
                   pl.BlockSpec((L, di), lambda b, c: (row(b, c), 0)),
                   pl.BlockSpec((1, N, di), lambda b, c: (row(b, c), 0, 0))],
        out_shape=[jax.ShapeDtypeStruct((t, di), F32), jax.ShapeDtypeStruct((t, di), BF16),
                   jax.ShapeDtypeStruct((nb * nc, N, di), F32)],
        scratch_shapes=[pltpu.VMEM((N, di), F32), pltpu.VMEM((L, di), F32), pltpu.VMEM((L, di), F32)],
        compiler_params=_params(("parallel", "arbitrary")))(
            proj, xbc, xbc, xbc, dtraw, dtb, alog, dskx, gn, tril, expand)


def _pool_sum(v, g, row, shift):
    s2 = v + shift(v, 1, row)
    s4 = s2 + shift(s2, 2, row)
    s8 = s4 + shift(s4, 4, row)
    s16 = s8 + shift(s8, 8, row)
    return jnp.where(g == 0, s2, jnp.where(g == 1, s4, jnp.where(g == 2, s8, s16)))


def _pool_count(g, row):
    return jnp.minimum(row + 1, jnp.left_shift(2, g)).astype(F32)


def _pool_fwd(proj, mix_w, mix_b, scale, nb, s, col0):
    pgd = mix_w.shape[-1]
    blk0 = _col_block(col0, 2 * pgd)

    def body(uz_ref, w_ref, b_ref, sc_ref, o_ref):
        g = pl.program_id(1)
        u = uz_ref[:, :pgd]
        zp = uz_ref[:, pgd:]
        row = lax.broadcasted_iota(jnp.int32, u.shape, 0)
        pooled = _pool_sum(u, g, row, _shift_down) / _pool_count(g, row) - u
        mixed = _bdot(pooled, w_ref[:, 0].reshape(pgd, pgd)) + b_ref[...]
        o_ref[...] = (mixed * sc_ref[...] * zp * _sigmoid(zp)).astype(BF16)

    return pl.pallas_call(
        body, name="pool_fwd", grid=(nb, N_POOL),
        in_specs=[pl.BlockSpec((s, 2 * pgd), lambda b, g: (b, blk0 + g)),
                  pl.BlockSpec((N_SHARD, 1, pgd // N_SHARD, pgd), lambda b, g: (0, g, 0, 0)),
                  pl.BlockSpec((1, pgd), lambda b, g: (0, g)), pl.BlockSpec((1, pgd), lambda b, g: (0, g))],
        out_specs=pl.BlockSpec((s, pgd), lambda b, g: (b, g)),
        out_shape=jax.ShapeDtypeStruct((nb * s, N_POOL * pgd), BF16),
        compiler_params=_params(("parallel", "parallel")))(proj, mix_w, mix_b, scale)


def _mid_fwd(ya, yb, proj, col0, x2, p2, tgt, wo, wpg, wup, ple_g, final_g, tm=256):
    t, d = ya.shape
    tm = min(tm, t)
    blk = _col_block(col0, 2 * d)
    n_up, pdim, up_w = wup.shape

    def body(ya_ref, yb_ref, g_ref, x_ref, p_ref, tgt_ref, wo_ref, wpg_ref, wup_ref, pg_ref, g_fin_ref,
             merged_ref, hn_ref, dpre_ref, dpu_ref, x1_ref, dx2_ref, loss_ref, dg_ref):
        @pl.when(pl.program_id(0) == 0)
        def _():
            loss_ref[...] = jnp.zeros_like(loss_ref)
            dg_ref[...] = jnp.zeros_like(dg_ref)

        merged = (_sigmoid(g_ref[:, :d]) * ya_ref[...] + _sigmoid(g_ref[:, d:]) * yb_ref[...]).astype(BF16)
        merged_ref[...] = merged
        x1 = x_ref[...] + jnp.dot(merged, wo_ref[...], preferred_element_type=F32)
        x1_ref[...] = x1
        r1 = lax.rsqrt(jnp.mean(x1 * x1, axis=-1, keepdims=True) + EPS)
        hn = (x1 * r1 * pg_ref[...]).astype(BF16)
        hn_ref[...] = hn
        gate = _sigmoid(jnp.dot(hn, wpg_ref[...], preferred_element_type=F32))
        pb = p_ref[...].astype(BF16)
        pu = jnp.concatenate([jnp.dot(pb, wup_ref[j], preferred_element_type=F32) for j in range(n_up)], axis=1)
        x2 = x1 + gate * pu
        r = lax.rsqrt(jnp.mean(x2 * x2, axis=-1, keepdims=True) + EPS)
        xn = x2 * r
        fg = g_fin_ref[...]
        err = xn * fg - tgt_ref[...]
        loss_ref[...] += 0.5 * jnp.sum(jnp.mean(err * err, axis=-1, keepdims=True))
        dy = err * (1.0 / d)
        dg_ref[...] += jnp.sum(dy * xn, axis=0, keepdims=True)
        dxn = dy * fg
        dx2 = r * (dxn - xn * jnp.mean(dxn * xn, axis=-1, keepdims=True))
        dx2_ref[...] = dx2
        dpre_ref[...] = (dx2 * pu * gate * (1.0 - gate)).astype(BF16)
        dpu_ref[...] = (dx2 * gate).astype(BF16)

    row = pl.BlockSpec((tm, d), lambda i: (i, 0))
    vec = pl.BlockSpec((1, d), lambda i: (0, 0))
    whole = lambda a: pl.BlockSpec(a.shape, lambda i: (0,) * a.ndim)
    return pl.pallas_call(
        body, name="mid_fwd", grid=(t // tm,),
        in_specs=[row, row, pl.BlockSpec((tm, 2 * d), lambda i: (i, blk)), row,
                  pl.BlockSpec((tm, pdim), lambda i: (i, 0)), row, whole(wo), whole(wpg), whole(wup), vec, vec],
        out_specs=[row] * 6 + [pl.BlockSpec((1, LANES), lambda i: (0, 0)), vec],
        out_shape=[jax.ShapeDtypeStruct((t, d), BF16)] * 4 + [jax.ShapeDtypeStruct((t, d), F32)] * 2 + [
            jax.ShapeDtypeStruct((1, LANES), F32), jax.ShapeDtypeStruct((1, d), F32)],
        compiler_params=_params(("arbitrary",)))(ya, yb, proj, x2, p2, tgt, wo, wpg, wup, ple_g, final_g)


def _rms_grad(xv, dh, g):
    r = lax.rsqrt(jnp.mean(xv * xv, axis=-1, keepdims=True) + EPS)
    xn = xv * r
    dd = dh * g
    return r * (dd - xn * jnp.mean(dd * xn, axis=-1, keepdims=True)), jnp.sum(dh * xn, axis=0, keepdims=True)


def _mid_bwd(dpre, dx2, x1, ya, yb, proj, col0, wpg, wo, ple_g, n_cols, tm=256):
    t, d = ya.shape
    tm = min(tm, t)
    blk = _col_block(col0, 2 * d)

    def body(dpre_ref, dx2_ref, x1_ref, ya_ref, yb_ref, g_ref, wpg_ref, wo_ref, pg_ref,
             dx1_ref, dya_ref, dyb_ref, dg_ref, dpg_ref):
        @pl.when(pl.program_id(0) == 0)
        def _():
            dpg_ref[...] = jnp.zeros_like(dpg_ref)

        dhn = _bdot_nt(dpre_ref[...], wpg_ref[...])
        dx, dpg = _rms_grad(x1_ref[...], dhn, pg_ref[...])
        dpg_ref[...] += dpg
        dx1 = dx2_ref[...] + dx
        dx1_ref[...] = dx1
        dm_v = _bdot_nt(dx1, wo_ref[...])
        sa = _sigmoid(g_ref[:, :d])
        sb = _sigmoid(g_ref[:, d:])
        dya_ref[...] = (dm_v * sa).astype(BF16)
        dyb_ref[...] = (dm_v * sb).astype(BF16)
        dg_ref[:, :d] = (dm_v * ya_ref[...] * sa * (1.0 - sa)).astype(BF16)
        dg_ref[:, d:] = (dm_v * yb_ref[...] * sb * (1.0 - sb)).astype(BF16)

    row = pl.BlockSpec((tm, d), lambda i: (i, 0))
    vec = pl.BlockSpec((1, d), lambda i: (0, 0))
    gspec = pl.BlockSpec((tm, 2 * d), lambda i: (i, blk))
    whole = lambda a: pl.BlockSpec(a.shape, lambda i: (0,) * a.ndim)
    return pl.pallas_call(
        body, name="mid_bwd", grid=(t // tm,),
        in_specs=[row, row, row, row, row, gspec, whole(wpg), whole(wo), vec],
        out_specs=[row, row, row, gspec, vec],
        out_shape=[jax.ShapeDtypeStruct((t, d), F32), jax.ShapeDtypeStruct((t, d), BF16),
                   jax.ShapeDtypeStruct((t, d), BF16), jax.ShapeDtypeStruct((t, n_cols), BF16),
                   jax.ShapeDtypeStruct((1, d), F32)],
        compiler_params=_params(("arbitrary",)))(dpre, dx2, x1, ya, yb, proj, wpg, wo, ple_g)


def _in_bwd(dproj, w_main, ddt, w_dt, x2, dx1, norm_g, after, tm=1024, tk=1024):
    t, k = dproj.shape
    d = x2.shape[1]
    tm, tk = min(tm, t), min(tk, k)
    nk = k // tk

    def body(a_ref, w_ref, ddt_ref, wdt_ref, x_ref, dres_ref, g_ref, _, gx_ref, dg_ref, acc_ref):
        kk = pl.program_id(1)

        @pl.when((pl.program_id(0) == 0) & (kk == 0))
        def _():
            dg_ref[...] = jnp.zeros_like(dg_ref)

        part = _bdot_nt(a_ref[...], w_ref[...])

        @pl.when(kk == 0)
        def _():
            acc_ref[...] = part

        @pl.when(kk > 0)
        def _():
            acc_ref[...] += part

        @pl.when(kk == nk - 1)
        def _():
            dh = acc_ref[...] + _bdot_nt(ddt_ref[...], wdt_ref[...])
            dx, dg = _rms_grad(x_ref[...], dh, g_ref[...])
            dg_ref[...] += dg
            gx_ref[...] = dres_ref[...] + dx

    row = pl.BlockSpec((tm, d), lambda i, j: (i, 0))
    vec = pl.BlockSpec((1, d), lambda i, j: (0, 0))
    return pl.pallas_call(
        body, name="in_bwd", grid=(t // tm, nk),
        in_specs=[pl.BlockSpec((tm, tk), lambda i, j: (i, j)), pl.BlockSpec((d, tk), lambda i, j: (0, j)),
                  pl.BlockSpec((tm, LANES), lambda i, j: (i, 0)), pl.BlockSpec((d, LANES), lambda i, j: (0, 0)),
                  row, row, vec, pl.BlockSpec((8, LANES), lambda i, j: (0, 0))],
        out_specs=[row, vec],
        out_shape=[jax.ShapeDtypeStruct((t, d), F32), jax.ShapeDtypeStruct((1, d), F32)],
        scratch_shapes=[pltpu.VMEM((tm, d), F32)],
        compiler_params=_params(("arbitrary", "arbitrary")))(dproj, w_main, ddt, w_dt, x2, dx1, norm_g, after)


def _pool_bwd(proj, dyb, dproj, mix_w, mix_b, scale, nb, s, col0):
    pgd = mix_w.shape[-1]
    blk0 = _col_block(col0, 2 * pgd)

    def body(uz_ref, dy_ref, _, w_ref, b_ref, sc_ref, duz_ref, dw_ref, db_ref, dsc_ref):
        g = pl.program_id(0)

        @pl.when(pl.program_id(1) == 0)
        def _():
            dw_ref[...] = jnp.zeros_like(dw_ref)
            db_ref[...] = jnp.zeros_like(db_ref)
            dsc_ref[...] = jnp.zeros_like(dsc_ref)

        u = uz_ref[:, :pgd]
        zp = uz_ref[:, pgd:]
        row = lax.broadcasted_iota(jnp.int32, u.shape, 0)
        cnt = _pool_count(g, row)
        pooled = _pool_sum(u, g, row, _shift_down) / cnt - u
        wv = w_ref[:, 0].reshape(pgd, pgd)
        mixed = _bdot(pooled, wv) + b_ref[...]
        sg = _sigmoid(zp)
        sz = zp * sg
        dy = dy_ref[...]
        sc = sc_ref[...]
        dsc_ref[...] += jnp.sum(dy * mixed * sz, axis=0, keepdims=True)
        dmixed = dy * sc * sz
        db_ref[...] += jnp.sum(dmixed, axis=0, keepdims=True)
        dw_ref[:, 0] += _bdot_tn(pooled, dmixed).reshape(N_SHARD, pgd // N_SHARD, pgd)
        dpooled = _bdot_nt(dmixed, wv)
        duz_ref[:, :pgd] = (_pool_sum(dpooled / cnt, g, row, _shift_up) - dpooled).astype(BF16)
        duz_ref[:, pgd:] = (dy * mixed * sc * sg * (1.0 + zp * (1.0 - sg))).astype(BF16)

    uz = pl.BlockSpec((s, 2 * pgd), lambda g, b: (b, blk0 + g))
    vec = pl.BlockSpec((1, pgd), lambda g, b: (0, g))
    wspec = pl.BlockSpec((N_SHARD, 1, pgd // N_SHARD, pgd), lambda g, b: (0, g, 0, 0))
    return pl.pallas_call(
        body, name="pool_bwd", grid=(N_POOL, nb),
        in_specs=[uz, pl.BlockSpec((s, pgd), lambda g, b: (b, g)), pl.BlockSpec(memory_space=pl.ANY), wspec, vec, vec],
        out_specs=[uz, wspec, vec, vec],
        out_shape=[jax.ShapeDtypeStruct(dproj.shape, dproj.dtype), jax.ShapeDtypeStruct(mix_w.shape, F32),
                   jax.ShapeDtypeStruct(mix_b.shape, F32), jax.ShapeDtypeStruct(scale.shape, F32)],
        input_output_aliases={2: 0},
        compiler_params=_params(("parallel", "arbitrary")))(proj, dyb, dproj, mix_w, mix_b, scale)


def _ssd_bwd(dyan, ypre, proj, xbc, dtraw, hp, dproj, dtb, alog, dskx, gn, nb, s, di, z_col0):
    t = nb * s
    nc = s // CHUNK
    hpg = di // HEAD_DIM // SSM_GROUPS
    gw = di // SSM_GROUPS
    gn_w = SSM_GROUPS * D_STATE
    dc = di + 2 * gn_w
    b_blk = _col_block(di, gn_w)
    z_blk = _col_block(z_col0, di)
    L, P, N = CHUNK, HEAD_DIM, D_STATE
    tril, expand, expand_t = _ssd_consts(di)

    def body(dy_ref, ypre_ref, z_ref, x_ref, b_ref, c_ref, dtr_ref, hp_ref, _, dtb_ref, alog_ref, dskx_ref, gn_ref,
             tril_ref, e_ref, et_ref, dz_ref, ddt_ref, dxbc_ref, dgn_ref, ddsk_ref, dalog_ref, ddtb_ref,
             dst_ref, dyp_ref, xdt_ref, dxm_ref, t1_ref, t3_ref, aux_ref):
        @pl.when((pl.program_id(0) == 0) & (pl.program_id(1) == 0))
        def _():
            dgn_ref[...] = jnp.zeros_like(dgn_ref)
            ddsk_ref[...] = jnp.zeros_like(ddsk_ref)
            dalog_ref[...] = jnp.zeros_like(dalog_ref)
            ddtb_ref[...] = jnp.zeros_like(ddtb_ref)

        @pl.when(pl.program_id(1) == 0)
        def _():
            dst_ref[...] = jnp.zeros_like(dst_ref)

        tri = tril_ref[...]
        dtpre, dt, a_neg, a_cs, a_cst = _ssd_scalars(dtr_ref, dtb_ref, alog_ref, tri)
        ev = e_ref[...]
        a_exp = _expand(a_cs, ev)
        dt_exp = _expand(dt, ev, terms=2)
        xv = x_ref[...]
        xdt = xv * dt_exp
        xdt_ref[...] = xdt
        a_last = a_exp[L - 1:L, :]
        dte = jnp.exp(a_last - a_exp)
        xe = xdt * dte
        ea = jnp.exp(a_exp)
        e_last = jnp.exp(a_last)
        lower = tri > 0.5
        aux_ref[...] = jnp.zeros_like(aux_ref)
        for g in range(SSM_GROUPS):
            gs = slice(g * gw, (g + 1) * gw)
            zv = z_ref[:, gs]
            yp = ypre_ref[:, gs]
            sg = _sigmoid(zv)
            sz = zv * sg
            vg = yp * sz
            r = lax.rsqrt(jnp.mean(vg * vg, axis=-1, keepdims=True) + EPS)
            vn = vg * r
            dyg = dy_ref[:, gs]
            dgn_ref[:, gs] += jnp.sum(dyg * vn, axis=0, keepdims=True)
            dvn = dyg * gn_ref[:, gs]
            dv = r * (dvn - vn * jnp.mean(dvn * vn, axis=-1, keepdims=True))
            dy = dv * sz
            dyp_ref[:, gs] = dy
            dz_ref[:, gs] = (dv * yp * sg * (1.0 + zv * (1.0 - sg))).astype(BF16)
            bg = b_ref[:, g * N:(g + 1) * N].astype(BF16)
            cg = c_ref[:, g * N:(g + 1) * N].astype(BF16)
            gm = _bdot_nt(cg, bg)
            ht = hp_ref[0, :, gs]
            dht = dst_ref[:, gs]
            bds = _bdot(bg, dht)
            dye = dy * ea[:, gs]
            xe_g = xe[:, gs]
            dcg = _bdot_nt(dye, ht)
            dbg = _bdot_nt(xe_g, dht)
            dst_ref[:, gs] = e_last[:, gs] * dht + _bdot_tn(cg, dye)
            dgm = jnp.zeros((L, L), F32)
            for e in range(hpg):
                h = g * hpg + e
                hs = slice(h * P, (h + 1) * P)
                decay = jnp.where(lower, jnp.exp(a_cs[:, h:h + 1] - a_cst[h:h + 1, :]), 0.0)
                dy_h = dyp_ref[:, hs]
                dgm = dgm + _bdot_nt(dy_h, xdt_ref[:, hs]) * decay
                dxm_ref[:, hs] = _bdot_tn(gm * decay, dy_h)
            dxbc_ref[:, di + g * N:di + (g + 1) * N] = dbg + _bdot_tn(dgm, cg)
            dxbc_ref[:, di + gn_w + g * N:di + gn_w + (g + 1) * N] = dcg + _bdot(dgm, bg)
            dxm = dxm_ref[:, gs]
            x_g = xv[:, gs]
            dskx = dskx_ref[:, gs]
            xeb = xe_g * bds
            dxdt = dxm + dte[:, gs] * bds
            dxbc_ref[:, gs] = dxdt * dt_exp[:, gs] + dy * dskx
            each = ea[:, gs] * _bdot(cg, ht)
            y_diag = yp - x_g * dskx - each
            rnd = lambda v: v.astype(BF16).astype(F32)
            t1_ref[:, gs] = rnd(dy) * y_diag + dy * each - rnd(xdt[:, gs]) * dxm - xeb
            t3_ref[:, gs] = dxdt * x_g
            aux_ref[0:1, gs] = jnp.sum(dht * ht, axis=0, keepdims=True)
            aux_ref[1:2, gs] = jnp.sum(dy * x_g, axis=0, keepdims=True)
            aux_ref[2:3, gs] = jnp.sum(xeb, axis=0, keepdims=True)
        etv = et_ref[...]
        aux = _head_sum(aux_ref[...], etv)
        rowi = lax.broadcasted_iota(jnp.int32, (L, LANES), 0)
        end = aux[2:3, :] + aux[0:1, :] * jnp.exp(a_cs[L - 1:L, :])
        da = _head_sum(t1_ref[...], etv, terms=3) + jnp.where(rowi == L - 1, end, 0.0)
        rc = lax.dot_general(tri, da, (((0,), (0,)), ((), ())), precision=HIGHEST, preferred_element_type=F32)
        ddt = a_neg * rc + _head_sum(t3_ref[...], etv, terms=1)
        ddtraw = ddt * _sigmoid(dtpre)
        ddt_ref[...] = ddtraw.astype(BF16)
        ddtb_ref[...] += jnp.sum(ddtraw, axis=0, keepdims=True)
        dalog_ref[...] += jnp.sum(dt * rc, axis=0, keepdims=True) * a_neg
        ddsk_ref[...] += aux[1:2, :]

    row = lambda b, c: b * nc + (nc - 1 - c)
    full = lambda w: pl.BlockSpec((L, w), lambda b, c: (row(b, c), 0))
    zspec = pl.BlockSpec((L, di), lambda b, c: (row(b, c), z_blk))
    vec = lambda w: pl.BlockSpec((1, w), lambda b, c: (0, 0))
    slab = lambda shape: pltpu.VMEM(shape, F32)
    return pl.pallas_call(
        body, name="ssd_bwd", grid=(nb, nc),
        in_specs=[full(di), full(di), zspec, full(di),
                  pl.BlockSpec((L, gn_w), lambda b, c: (row(b, c), b_blk)),
                  pl.BlockSpec((L, gn_w), lambda b, c: (row(b, c), b_blk + 1)),
                  full(LANES),
                  pl.BlockSpec((1, N, di), lambda b, c: (row(b, c), 0, 0)),
                  pl.BlockSpec(memory_space=pl.ANY),
                  vec(LANES), vec(LANES), vec(di), vec(di),
                  pl.BlockSpec((L, L), lambda b, c: (0, 0)),
                  pl.BlockSpec((LANES, di), lambda b, c: (0, 0)),
                  pl.BlockSpec((di, LANES), lambda b, c: (0, 0))],
        out_specs=[zspec, full(LANES), full(dc), vec(di), vec(LANES), vec(LANES), vec(LANES)],
        out_shape=[jax.ShapeDtypeStruct(dproj.shape, dproj.dtype), jax.ShapeDtypeStruct((t, LANES), BF16),
                   jax.ShapeDtypeStruct((t, dc), F32), jax.ShapeDtypeStruct((1, di), F32),
                   jax.ShapeDtypeStruct((1, LANES), F32), jax.ShapeDtypeStruct((1, LANES), F32),
                   jax.ShapeDtypeStruct((1, LANES), F32)],
        scratch_shapes=[slab((N, di)), slab((L, di)), slab((L, di)), slab((L, di)), slab((L, di)), slab((L, di)),
                        slab((8, di))],
        input_output_aliases={8: 0},
        compiler_params=_params(("arbitrary", "arbitrary")))(
            dyan, ypre, proj, xbc, xbc, xbc, dtraw, hp, dproj, dtb, alog, dskx, gn, tril, expand, expand_t)


def _conv_bwd(proj, dxbc, dproj, conv_w, conv_b, nb, s, col0, cb=256):
    n_blk, w_spec = _conv_w_spec(conv_w, cb, 0)
    blk0 = _col_block(col0, cb)
    rc = min(CONV_ROWS, s)

    def body(x_ref, dy_ref, _, w_ref, b_ref, dx_ref, dw_ref, db_ref, dacc_ref):
        @pl.when(pl.program_id(1) == 0)
        def _():
            dw_ref[...] = jnp.zeros_like(dw_ref)
            db_ref[...] = jnp.zeros_like(db_ref)

        wts, bias = _conv_weights(w_ref), b_ref[...]
        fold = lambda v: v.reshape(rc // 8, 8, cb).sum(axis=0)
        db8 = jnp.zeros((8, cb), F32)
        dw8 = [jnp.zeros((8, cb), F32) for _ in range(CONV_WIDTH)]
        for t0 in range(0, s, rc):
            taps = _conv_taps(x_ref, t0, rc)
            acc = _conv_pre(taps, wts, bias)
            sg = _sigmoid(acc)
            dacc = dy_ref[t0:t0 + rc, :] * sg * (1.0 + acc * (1.0 - sg))
            dacc_ref[t0:t0 + rc, :] = dacc
            db8 = db8 + fold(dacc)
            dw8 = [dw8[j] + fold(dacc * taps[j]) for j in range(CONV_WIDTH)]
        db_ref[...] += jnp.sum(db8, axis=0, keepdims=True)
        for j in range(CONV_WIDTH):
            dw_ref[0, CONV_WIDTH - 1 - j:CONV_WIDTH - j, :] += jnp.sum(dw8[j], axis=0, keepdims=True)
        for t0 in range(0, s, rc):
            if t0 + rc < s:
                n = rc + CONV_HALO
                win = dacc_ref[t0:t0 + n, :]
                ups = [win[:rc]] + [pltpu.roll(win, n - j, 0)[:rc] for j in range(1, CONV_WIDTH)]
            else:
                cur = dacc_ref[t0:t0 + rc, :]
                row = lax.broadcasted_iota(jnp.int32, cur.shape, 0)
                ups = [cur] + [_shift_up(cur, j, row) for j in range(1, CONV_WIDTH)]
            dx = ups[0] * wts[0]
            for j in range(1, CONV_WIDTH):
                dx = dx + ups[j] * wts[j]
            dx_ref[t0:t0 + rc, :] = dx.astype(BF16)

    return pl.pallas_call(
        body, name="conv_bwd", grid=(n_blk, nb),
        in_specs=[pl.BlockSpec((s, cb), lambda j, b: (b, blk0 + j)), pl.BlockSpec((s, cb), lambda j, b: (b, j)),
                  pl.BlockSpec(memory_space=pl.ANY), w_spec, pl.BlockSpec((1, cb), lambda j, b: (0, j))],
        out_specs=[pl.BlockSpec((s, cb), lambda j, b: (b, blk0 + j)), w_spec, pl.BlockSpec((1, cb), lambda j, b: (0, j))],
        out_shape=[jax.ShapeDtypeStruct(dproj.shape, dproj.dtype), jax.ShapeDtypeStruct(conv_w.shape, F32),
                   jax.ShapeDtypeStruct(conv_b.shape, F32)],
        scratch_shapes=[pltpu.VMEM((s, cb), F32)],
        input_output_aliases={2: 0},
        compiler_params=_params(("parallel", "arbitrary")))(proj, dxbc, dproj, conv_w, conv_b)


def _local_step(x, p, tgt, wg, small, rest_weights, early_grads, w_in_grad):
    nb, s, d = x.shape
    t = nb * s
    gn_w = SSM_GROUPS * D_STATE
    dc = N_SHARD * wg["conv_w"].shape[2]
    di = dc - 2 * gn_w
    nh = di // HEAD_DIM
    pgd = d // N_POOL
    x2 = x.reshape(t, d)
    p2 = p.reshape(t, p.shape[-1])
    tgt2 = tgt.reshape(t, d)

    w_main, w_dt = _regroup_w_in(wg["w_in"], d, di, dc, nh)
    c_g, c_z, c_xbc, c_uz = 0, 2 * d, 2 * d + di, 2 * d + di + dc
    n_main = w_main.shape[1]

    pad_h = lambda v: jnp.pad(v.reshape(1, nh).astype(F32), ((0, 0), (0, LANES - nh)))
    dtb, alog = pad_h(small["dt_bias"]), pad_h(small["a_log"])
    dskx = jnp.repeat(small["d_skip"].reshape(1, nh).astype(F32), HEAD_DIM, axis=1)
    vec = lambda v: v.reshape(1, -1).astype(F32)
    norm_g, gn, conv_b = vec(small["norm_g"]), vec(small["gnorm_g"]), vec(small["conv_b"])
    mix_b, scale = vec(small["pool_mix_b"]), vec(small["pool_scale"])
    ple_g, final_g = vec(small["ple_norm_g"]), vec(small["final_g"])
    conv_w = wg["conv_w"]

    proj, dtraw, h = _inproj(x2, norm_g, w_main, w_dt)
    xbc = _conv_fwd(proj, conv_w, conv_b, nb, s, c_xbc)
    ypre, yan, hp = _ssd_fwd(proj, xbc, dtraw, dtb, alog, dskx, gn, nb, s, di, c_z)
    wr = rest_weights(yan)
    mix_w = wr["pool_mix_w"].reshape(N_SHARD, N_POOL, pgd // N_SHARD, pgd)
    rows = lambda v: v.reshape(-1, v.shape[-1])
    wa, wb, wo, wpg = rows(wr["w_branch_a"]), rows(wr["w_branch_b"]), rows(wr["w_out"]), rows(wr["w_ple_gate"])
    wup = wr["w_ple_up"]
    ybp = _pool_fwd(proj, mix_w, mix_b, scale, nb, s, c_uz)
    ya = _mm(yan, wa, "branch_a")
    yb = _mm(ybp, wb, "branch_b")
    merged, hn, dpre, dpu, x1, dx2, loss, d_final_g = _mid_fwd(
        ya, yb, proj, c_g, x2, p2, tgt2, wo, wpg, wup, ple_g, final_g)

    d_wpg = _mm_tn(hn, dpre, "d_w_ple_gate")
    d_wup = _mm_tn(p2, dpu, "d_w_ple_up", tn=wup.shape[-1], col_blocks=True)
    dx1, dya, dyb, dproj, d_ple_g = _mid_bwd(dpre, dx2, x1, ya, yb, proj, c_g, wpg, wo, ple_g, n_main)
    d_wo = _mm_tn(merged, dx1, "d_w_out")
    d_wa = _mm_tn(yan, dya, "d_w_branch_a")
    d_wb = _mm_tn(ybp, dyb, "d_w_branch_b")
    dyan = _mm_nt(dya, wa, "d_y_a")
    dybp = _mm_nt(dyb, wb, "d_y_b")
    dproj, d_mix_w, d_mix_b, d_scale = _pool_bwd(proj, dybp, dproj, mix_w, mix_b, scale, nb, s, c_uz)
    shard_major = lambda v: v.reshape(N_SHARD, v.shape[0] // N_SHARD, v.shape[1])
    early = dict(pool_mix_w=d_mix_w.reshape(N_SHARD, pgd, pgd), w_branch_a=shard_major(d_wa),
                 w_branch_b=shard_major(d_wb), w_out=shard_major(d_wo), w_ple_gate=shard_major(d_wpg),
                 w_ple_up=d_wup)
    token = early_grads(early)
    dproj, ddt, dxbc, d_gn, d_dsk, d_alog, d_dtb = _ssd_bwd(
        dyan, ypre, proj, xbc, dtraw, hp, dproj, dtb + token[0:1, 0:1], alog, dskx, gn, nb, s, di, c_z)
    dproj, d_conv_w, d_conv_b = _conv_bwd(proj, dxbc, dproj, conv_w, conv_b, nb, s, c_xbc)
    d_wmain = _mm_tn(h, dproj, "d_w_in")
    d_wdt = _mm_tn(h, ddt, "d_w_dt")
    d_w_in = _ungroup_w_in(d_wmain, d_wdt, d, di, dc, nh)
    token = w_in_grad(d_w_in)
    gx, d_norm_g = _in_bwd(dproj, w_main, ddt, w_dt, x2, dx1, norm_g, token)

    grads = dict(norm_g=d_norm_g, w_in=d_w_in, conv_w=d_conv_w, conv_b=d_conv_b, dt_bias=d_dtb[:, :nh],
                 a_log=d_alog[:, :nh], d_skip=d_dsk[:, :nh], gnorm_g=d_gn, pool_mix_b=d_mix_b, pool_scale=d_scale,
                 ple_norm_g=d_ple_g, final_g=d_final_g, **early)
    return loss[0, 0], gx.reshape(nb, s, d), grads


def _place():
    return lax.axis_index("x"), lax.axis_index("y"), lax.axis_index("c")


def _other_chips(x, y):
    return [(1 - x, y), (x, 1 - y), (1 - x, 1 - y)]


def _halves(c, rows, align):
    rh = rows // 2
    assert rows % 2 == 0 and rh % align == 0, (rows, align)
    return (pl.ds(pl.multiple_of(c * rh, align), rh), pl.ds(pl.multiple_of((1 - c) * rh, align), rh))


HBM = pl.BlockSpec(memory_space=pl.ANY)


def _into_slot(w2, k, dtype, name):
    rows, cols = w2.shape
    rb = _tile(rows, 256)

    def body(k_ref, w_ref, o_ref):
        o_ref[0] = w_ref[...].astype(dtype)

    return pl.pallas_call(
        body, name=name,
        grid_spec=pltpu.PrefetchScalarGridSpec(
            num_scalar_prefetch=1, grid=(rows // rb,),
            in_specs=[pl.BlockSpec((rb, cols), lambda i, k_ref: (i, 0))],
            out_specs=pl.BlockSpec((1, rb, cols), lambda i, k_ref: (k_ref[0], i, 0))),
        out_shape=jax.ShapeDtypeStruct((N_SHARD, rows, cols), dtype),
        compiler_params=_params(("parallel",)))(k.reshape(1), w2)


def _gather_weights(split, whole):
    n_split, n_all = len(split), len(split) + len(whole)

    def body(*refs):
        bufs = refs[n_all:2 * n_all]
        send_sems, recv_sems = refs[2 * n_all:]
        x, y, c = _place()
        k = 2 * x + y
        chips = _other_chips(x, y)

        def copy(idx, block, to):
            return pltpu.make_async_remote_copy(src_ref=block, dst_ref=block, send_sem=send_sems.at[idx],
                                                recv_sem=recv_sems.at[idx], device_id=to, device_id_type=MESH)

        def block(i, shard, rows):
            return bufs[i].at[shard, rows] if i < n_split else bufs[i].at[shard]

        def sem(i, j):
            return 6 * i + j if i < n_split else 6 * n_split + 3 * (i - n_split) + j

        started = []
        for i in range(n_all):
            mine, _ = _halves(c, bufs[i].shape[1], 16) if i < n_split else (None, None)
            for j, (px, py) in enumerate(chips):
                started.append(copy(sem(i, j), block(i, k, mine), (px, py, c)))
                started[-1].start()
        for i in range(n_all):
            mine, _ = _halves(c, bufs[i].shape[1], 16) if i < n_split else (None, None)
            for j, (px, py) in enumerate(chips):
                landed = block(i, 2 * px + py, mine)
                copy(sem(i, j), landed, (px, py, c)).wait_recv()
                if i < n_split:
                    started.append(copy(sem(i, 3 + j), landed, (x, y, 1 - c)))
                    started[-1].start()
        for i in range(n_split):
            _, theirs = _halves(c, bufs[i].shape[1], 16)
            for j, (px, py) in enumerate(chips):
                copy(sem(i, 3 + j), block(i, 2 * px + py, theirs), (x, y, 1 - c)).wait_recv()
        for cp in started:
            cp.wait_send()

    arrays = list(split) + list(whole)
    n_sem = 6 * n_split + 3 * len(whole)
    return pl.pallas_call(
        body, name="gather_weights",
        in_specs=[HBM] * n_all, out_specs=[HBM] * n_all,
        out_shape=[jax.ShapeDtypeStruct(a.shape, a.dtype) for a in arrays],
        input_output_aliases={i: i for i in range(n_all)},
        scratch_shapes=[pltpu.SemaphoreType.DMA((n_sem,)), pltpu.SemaphoreType.DMA((n_sem,))],
    )(*arrays)


def _swap_halves(gs):
    n = len(gs)

    def body(*refs):
        ins, outs, send_sems, recv_sems = refs[:n], refs[n:2 * n], refs[2 * n], refs[2 * n + 1]
        x, y, c = _place()
        copies = []
        for i in range(n):
            _, theirs = _halves(c, gs[i].shape[1], 8)
            cp = pltpu.make_async_remote_copy(src_ref=ins[i].at[:, theirs], dst_ref=outs[i], send_sem=send_sems.at[i],
                                              recv_sem=recv_sems.at[i], device_id=(x, y, 1 - c), device_id_type=MESH)
            cp.start()
            copies.append(cp)
        for cp in copies:
            cp.wait()

    return pl.pallas_call(
        body, name="swap_halves", in_specs=[HBM] * n, out_specs=[HBM] * n,
        out_shape=[jax.ShapeDtypeStruct((g.shape[0], g.shape[1] // 2, g.shape[2]), g.dtype) for g in gs],
        scratch_shapes=[pltpu.SemaphoreType.DMA((n,)), pltpu.SemaphoreType.DMA((n,))],
    )(*gs)


def _join_halves(vs):
    n = len(vs)

    def body(*refs):
        bufs, send_sems, recv_sems = refs[n:2 * n], refs[2 * n], refs[2 * n + 1]
        x, y, c = _place()

        def copy(i, rows):
            return pltpu.make_async_remote_copy(src_ref=bufs[i].at[rows], dst_ref=bufs[i].at[rows],
                                                send_sem=send_sems.at[i], recv_sem=recv_sems.at[i],
                                                device_id=(x, y, 1 - c), device_id_type=MESH)

        halves = [_halves(c, bufs[i].shape[0], 8) for i in range(n)]
        sends = [copy(i, halves[i][0]) for i in range(n)]
        for cp in sends:
            cp.start()
        for i in range(n):
            copy(i, halves[i][1]).wait_recv()
        for cp in sends:
            cp.wait_send()

    return pl.pallas_call(
        body, name="join_halves", in_specs=[HBM] * n, out_specs=[HBM] * n,
        out_shape=[jax.ShapeDtypeStruct(v.shape, v.dtype) for v in vs],
        input_output_aliases={i: i for i in range(n)},
        scratch_shapes=[pltpu.SemaphoreType.DMA((n,)), pltpu.SemaphoreType.DMA((n,))],
    )(*vs)


SEM = pl.BlockSpec(memory_space=pltpu.SEMAPHORE)
IN_HBM = pl.BlockSpec(memory_space=pltpu.HBM)
SPLIT_EFFECT = pltpu.SideEffectType.DATAFLOW_SIDE_EFFECTING


def _split_copies(plan, refs, send_sems, recv_sems):
    pairs = []
    for idx, (src, dst, landing, to) in enumerate(plan(refs)):
        mk = lambda d: pltpu.make_async_remote_copy(src_ref=src, dst_ref=d, send_sem=send_sems.at[idx],
                                                    recv_sem=recv_sems.at[idx], device_id=to, device_id_type=MESH)
        pairs.append((mk(dst), mk(landing)))
    return pairs


def _split_start(name, bufs, after, plan, n_copies):
    n = len(bufs)

    def body(*refs):
        send_sems, recv_sems, token = refs[n + 1], refs[n + 2], refs[-1]
        for send, _ in _split_copies(plan, refs[:n], send_sems, recv_sems):
            send.start()
        token[...] = jnp.zeros_like(token)

    sems = pltpu.SemaphoreType.DMA((n_copies,))
    out = pl.pallas_call(
        body, name=name,
        in_specs=[IN_HBM] * n + [HBM],
        out_specs=[SEM, SEM] + [IN_HBM] * n + [pl.BlockSpec(memory_space=pltpu.VMEM)],
        out_shape=[sems, sems] + [pltpu.HBM(b.shape, b.dtype) for b in bufs] + [jax.ShapeDtypeStruct((8, LANES), F32)],
        input_output_aliases={i: 2 + i for i in range(n)},
        compiler_params=pltpu.CompilerParams(has_side_effects=SPLIT_EFFECT),
    )(*[pltpu.with_memory_space_constraint(b, pltpu.HBM) for b in bufs], after)
    return out[0], out[1], out[2:2 + n], out[-1]


def _split_wait(name, bufs, send_sems, recv_sems, after, plan):
    n = len(bufs)

    def body(*refs):
        for send, recv in _split_copies(plan, refs[:n], refs[n], refs[n + 1]):
            send.wait_send()
            recv.wait_recv()

    return pl.pallas_call(
        body, name=name,
        in_specs=[IN_HBM] * n + [SEM, SEM, HBM],
        out_specs=[IN_HBM] * n,
        out_shape=[pltpu.HBM(b.shape, b.dtype) for b in bufs],
        input_output_aliases={i: i for i in range(n)},
        compiler_params=pltpu.CompilerParams(has_side_effects=SPLIT_EFFECT),
    )(*bufs, send_sems, recv_sems, after)


def _gather_plan(n):
    def plan(refs):
        x, y, c = _place()
        k = 2 * x + y
        return [(refs[i].at[k], refs[i].at[k], refs[i].at[2 * px + py], (px, py, c))
                for i in range(n) for px, py in _other_chips(x, y)]
    return plan


def _scatter_plan(n):
    def plan(refs):
        x, y, c = _place()
        return [(refs[i].at[2 * px + py], refs[n + i].at[j], refs[n + i].at[j], (px, py, c))
                for i in range(n) for j, (px, py) in enumerate(_other_chips(x, y))]
    return plan


def _swap_sibling(vs):
    n = len(vs)

    def body(*refs):
        ins, outs, send_sems, recv_sems = refs[:n], refs[n:2 * n], refs[2 * n], refs[2 * n + 1]
        x, y, c = _place()
        copies = [pltpu.make_async_remote_copy(src_ref=ins[i], dst_ref=outs[i], send_sem=send_sems.at[i],
                                               recv_sem=recv_sems.at[i], device_id=(x, y, 1 - c), device_id_type=MESH)
                  for i in range(n)]
        for cp in copies:
            cp.start()
        for cp in copies:
            cp.wait()

    return pl.pallas_call(
        body, name="swap_sibling", in_specs=[HBM] * n, out_specs=[HBM] * n,
        out_shape=[jax.ShapeDtypeStruct(v.shape, v.dtype) for v in vs],
        scratch_shapes=[pltpu.SemaphoreType.DMA((n,)), pltpu.SemaphoreType.DMA((n,))],
    )(*vs)


def _to_bf16(g, name):
    _, rows, cols = g.shape
    rb = _tile(rows, 256)

    def body(g_ref, o_ref):
        o_ref[...] = g_ref[...].astype(BF16)

    spec = pl.BlockSpec((1, rb, cols), lambda j, i: (j, i, 0))
    return pl.pallas_call(
        body, name=name, grid=(N_SHARD, rows // rb), in_specs=[spec], out_specs=spec,
        out_shape=jax.ShapeDtypeStruct(g.shape, BF16),
        compiler_params=_params(("parallel", "parallel")))(g)


def _add_landed(g, landed, k, name):
    _, rows, cols = g.shape
    rb = _tile(rows, 256)

    def body(k_ref, g_ref, l_ref, o_ref):
        o_ref[...] = g_ref[0] + l_ref[0].astype(F32) + l_ref[1].astype(F32) + l_ref[2].astype(F32)

    return pl.pallas_call(
        body, name=name,
        grid_spec=pltpu.PrefetchScalarGridSpec(
            num_scalar_prefetch=1, grid=(rows // rb,),
            in_specs=[pl.BlockSpec((1, rb, cols), lambda i, k_ref: (k_ref[0], i, 0)),
                      pl.BlockSpec((N_SHARD - 1, rb, cols), lambda i, k_ref: (0, i, 0))],
            out_specs=pl.BlockSpec((rb, cols), lambda i, k_ref: (i, 0))),
        out_shape=jax.ShapeDtypeStruct((rows, cols), F32),
        compiler_params=_params(("parallel",)))(k.reshape(1), g, landed)


def _allreduce_small(v):
    rows = v.shape[0]

    def body(v_ref, o_ref, buf_ref, send_sems, recv_sems):
        x, y, c = _place()
        me = 4 * x + 2 * y + c
        buf_ref[me] = v_ref[...]
        copies = []
        for rel in range(1, 8):
            peer = (x ^ (rel >> 2), y ^ ((rel >> 1) & 1), c ^ (rel & 1))
            cp = pltpu.make_async_remote_copy(src_ref=v_ref, dst_ref=buf_ref.at[me], send_sem=send_sems.at[rel - 1],
                                              recv_sem=recv_sems.at[rel - 1], device_id=peer, device_id_type=MESH)
            cp.start()
            copies.append(cp)
        for rel in range(1, 8):
            peer_id = me ^ rel
            pltpu.make_async_remote_copy(src_ref=v_ref, dst_ref=buf_ref.at[peer_id], send_sem=send_sems.at[rel - 1],
                                         recv_sem=recv_sems.at[rel - 1], device_id=(x, y, c),
                                         device_id_type=MESH).wait_recv()
        for cp in copies:
            cp.wait_send()
        acc = buf_ref[0]
        for i in range(1, 8):
            acc = acc + buf_ref[i]
        o_ref[...] = acc

    return pl.pallas_call(
        body, name="allreduce_small",
        in_specs=[pl.BlockSpec(memory_space=pltpu.VMEM)], out_specs=pl.BlockSpec(memory_space=pltpu.VMEM),
        out_shape=jax.ShapeDtypeStruct(v.shape, F32),
        scratch_shapes=[pltpu.VMEM((8, rows, LANES), F32), pltpu.SemaphoreType.DMA((7,)), pltpu.SemaphoreType.DMA((7,))],
    )(v)


def _add_pair(g, got, c, name):
    _, rh, cols = got.shape
    rb = _tile(rh, 256)
    nrb = rh // rb

    def body(c_ref, g_ref, got_ref, o_ref):
        o_ref[...] = (g_ref[...] + got_ref[...]).astype(BF16)

    spec = pl.BlockSpec((1, rb, cols), lambda j, i, c_ref: (j, i, 0))
    return pl.pallas_call(
        body, name=name,
        grid_spec=pltpu.PrefetchScalarGridSpec(
            num_scalar_prefetch=1, grid=(N_SHARD, nrb),
            in_specs=[pl.BlockSpec((1, rb, cols), lambda j, i, c_ref: (j, c_ref[0] * nrb + i, 0)), spec],
            out_specs=spec),
        out_shape=jax.ShapeDtypeStruct(got.shape, BF16),
        compiler_params=_params(("parallel", "parallel")))(c.reshape(1), g, got)


def _add_chips(g, got, landed, k, c, name):
    _, rh, cols = got.shape
    rb = _tile(rh, 256)
    nrb = rh // rb

    def body(kc_ref, g_ref, got_ref, l_ref, o_ref):
        own = g_ref[0] + got_ref[0]
        o_ref[...] = own + l_ref[0].astype(F32) + l_ref[1].astype(F32) + l_ref[2].astype(F32)

    half_c = lambda i, kc: (kc[1] * nrb + i, 0)
    return pl.pallas_call(
        body, name=name,
        grid_spec=pltpu.PrefetchScalarGridSpec(
            num_scalar_prefetch=1, grid=(nrb,),
            in_specs=[pl.BlockSpec((1, rb, cols), lambda i, kc: (kc[0],) + half_c(i, kc)),
                      pl.BlockSpec((1, rb, cols), lambda i, kc: (kc[0], i, 0)),
                      pl.BlockSpec((N_SHARD - 1, rb, cols), lambda i, kc: (0, i, 0))],
            out_specs=pl.BlockSpec((rb, cols), half_c)),
        out_shape=jax.ShapeDtypeStruct((2 * rh, cols), F32),
        compiler_params=_params(("parallel",)))(jnp.stack([k, c]), g, got, landed)


def _adamw(wv, gs, m, v, name):
    rows, cols = wv.shape
    rb = _tile(rows, 256)
    c1 = 1.0 - ADAM_B1 ** ADAM_STEP
    c2 = 1.0 - ADAM_B2 ** ADAM_STEP
    n_g = len(gs)

    def body(*refs):
        w_ref, g_refs, (m_ref, v_ref, go_ref, d_ref, nm_ref, nv_ref) = refs[0], refs[1:1 + n_g], refs[1 + n_g:]
        gv = g_refs[0][...]
        for ref in g_refs[1:]:
            gv = gv + ref[...]
        go_ref[...] = gv
        nm = ADAM_B1 * m_ref[...] + (1.0 - ADAM_B1) * gv
        nv = ADAM_B2 * v_ref[...] + (1.0 - ADAM_B2) * (gv * gv)
        nm_ref[...] = nm
        nv_ref[...] = nv
        d_ref[...] = -ADAM_LR * ((nm / c1) / (jnp.sqrt(nv / c2) + ADAM_EPS) + ADAM_WD * w_ref[...])

    spec = pl.BlockSpec((rb, cols), lambda i: (i, 0))
    return pl.pallas_call(
        body, name=name, grid=(rows // rb,), in_specs=[spec] * (3 + n_g), out_specs=[spec] * 4,
        out_shape=[jax.ShapeDtypeStruct((rows, cols), F32)] * 4,
        compiler_params=_params(("parallel",)))(wv, *gs, m, v)


def _pack(flats):
    cat = jnp.concatenate([f.reshape(-1) for f in flats])
    n = cat.shape[0]
    rows = -(-n // (8 * LANES)) * 8
    return jnp.pad(cat, (0, rows * LANES - n)).reshape(rows, LANES)


def _unpack(packed, shapes):
    flat = packed.reshape(-1)
    out, off = [], 0
    for shp in shapes:
        n = 1
        for dim in shp:
            n *= dim
        out.append(flat[off:off + n].reshape(shp))
        off += n
    return out


def kernel(x, p, norm_g, w_in, conv_w, conv_b, dt_bias, a_log, d_skip, gnorm_g, pool_mix_w, pool_mix_b, pool_scale, w_branch_a, w_branch_b, w_out, ple_norm_g, w_ple_gate, w_ple_up, final_g, loss_target, m_norm_g, m_w_in, m_conv_w, m_conv_b, m_dt_bias, m_a_log, m_d_skip, m_gnorm_g, m_pool_mix_w, m_pool_mix_b, m_pool_scale, m_w_branch_a, m_w_branch_b, m_w_out, m_ple_norm_g, m_w_ple_gate, m_w_ple_up, m_final_g, v_norm_g, v_w_in, v_conv_w, v_conv_b, v_dt_bias, v_a_log, v_d_skip, v_gnorm_g, v_pool_mix_w, v_pool_mix_b, v_pool_scale, v_w_branch_a, v_w_branch_b, v_w_out, v_ple_norm_g, v_w_ple_gate, v_w_ple_up, v_final_g):
    wts = dict(norm_g=norm_g, w_in=w_in, conv_w=conv_w, conv_b=conv_b, dt_bias=dt_bias, a_log=a_log, d_skip=d_skip,
               gnorm_g=gnorm_g, pool_mix_w=pool_mix_w, pool_mix_b=pool_mix_b, pool_scale=pool_scale,
               w_branch_a=w_branch_a, w_branch_b=w_branch_b, w_out=w_out, ple_norm_g=ple_norm_g,
               w_ple_gate=w_ple_gate, w_ple_up=w_ple_up, final_g=final_g)
    mom_m = dict(norm_g=m_norm_g, w_in=m_w_in, conv_w=m_conv_w, conv_b=m_conv_b, dt_bias=m_dt_bias, a_log=m_a_log,
                 d_skip=m_d_skip, gnorm_g=m_gnorm_g, pool_mix_w=m_pool_mix_w, pool_mix_b=m_pool_mix_b,
                 pool_scale=m_pool_scale, w_branch_a=m_w_branch_a, w_branch_b=m_w_branch_b, w_out=m_w_out,
                 ple_norm_g=m_ple_norm_g, w_ple_gate=m_w_ple_gate, w_ple_up=m_w_ple_up, final_g=m_final_g)
    mom_v = dict(norm_g=v_norm_g, w_in=v_w_in, conv_w=v_conv_w, conv_b=v_conv_b, dt_bias=v_dt_bias, a_log=v_a_log,
                 d_skip=v_d_skip, gnorm_g=v_gnorm_g, pool_mix_w=v_pool_mix_w, pool_mix_b=v_pool_mix_b,
                 pool_scale=v_pool_scale, w_branch_a=v_w_branch_a, w_branch_b=v_w_branch_b, w_out=v_w_out,
                 ple_norm_g=v_ple_norm_g, w_ple_gate=v_w_ple_gate, w_ple_up=v_w_ple_up, final_g=v_final_g)
    c = lax.axis_index("c")
    k = 2 * lax.axis_index("x") + lax.axis_index("y")
    flat2 = lambda a: a.reshape(-1, a.shape[-1])

    slots = {n: _into_slot(flat2(wts[n]), k, BF16, "slot_" + n) for n in BIG}
    w_in_g, conv_g = _gather_weights([slots["w_in"]], [_into_slot(flat2(conv_w), k, F32, "slot_conv_w")])
    n_rest = len(REST)
    gsend, grecv, gbufs, gtoken = _split_start("gather_rest_start", [slots[n] for n in REST], conv_g,
                                               _gather_plan(n_rest), 3 * n_rest)

    def rest_weights(after):
        return dict(zip(REST, _split_wait("gather_rest_wait", gbufs, gsend, grecv, after, _gather_plan(n_rest))))

    flying = {}

    def early_grads(early):
        sends = [_to_bf16(early[n], "bf16_" + n) for n in REST]
        lands = [pltpu.with_memory_space_constraint(lax.empty((N_SHARD - 1,) + v.shape[1:], BF16), pltpu.HBM)
                 for v in sends]
        ssend, srecv, sbufs, stoken = _split_start("scatter_rest_start", sends + lands, early[REST[0]],
                                                   _scatter_plan(n_rest), 3 * n_rest)
        flying.update(send=ssend, recv=srecv, bufs=sbufs)
        return stoken

    def w_in_grad(g_w_in):
        got = _swap_halves([g_w_in])[0]
        pair = _add_pair(g_w_in, got, c, "add_pair_w_in")
        land = pltpu.with_memory_space_constraint(lax.empty((N_SHARD - 1,) + pair.shape[1:], BF16), pltpu.HBM)
        wsend, wrecv, wbufs, wtoken = _split_start("scatter_w_in_start", [pair, land], got, _scatter_plan(1), 3)
        flying.update(w_send=wsend, w_recv=wrecv, w_bufs=wbufs, w_got=got)
        return wtoken

    small = {n: wts[n] for n in SMALL}
    small["norm_g"] = norm_g + gtoken[0, 0]
    loss, grad_x, grads = _local_step(x, p[0], loss_target, dict(w_in=w_in_g, conv_w=conv_g), small,
                                      rest_weights, early_grads, w_in_grad)
    g_w_in = grads["w_in"]
    landed = _split_wait("scatter_w_in_wait", flying["w_bufs"], flying["w_send"], flying["w_recv"], grad_x,
                         _scatter_plan(1))[1]
    w_in_sum = _join_halves([_add_chips(g_w_in, flying["w_got"], landed, k, c, "add_chips_w_in")])[0]

    sbufs = _split_wait("scatter_rest_wait", flying["bufs"], flying["send"], flying["recv"], g_w_in,
                        _scatter_plan(n_rest))
    mine = [_add_landed(grads[n], ld, k, "add_landed_" + n) for n, ld in zip(REST, sbufs[n_rest:])]
    theirs = _swap_sibling(mine)
    g_sums = dict(zip(REST, zip(mine, theirs)))
    g_sums["w_in"] = (w_in_sum,)

    conv_shape = flat2(conv_w).shape
    small_sum = _allreduce_small(_pack([grads[n] for n in SMALL] + [grads["conv_w"], loss]))
    small_shapes = [wts[n].shape for n in SMALL] + [(N_SHARD,) + conv_shape, (1,)]
    small_g = _unpack(small_sum, small_shapes)
    g_conv = lax.dynamic_index_in_dim(small_g[-2], k, axis=0, keepdims=False)

    outs = {}
    for n in BIG:
        vals = _adamw(flat2(wts[n]), g_sums[n], flat2(mom_m[n]), flat2(mom_v[n]), "adamw_" + n)
        for kind, val in zip(("grad", "delta", "new_m", "new_v"), vals):
            outs[kind, n] = val.reshape(wts[n].shape)
    names = SMALL + ("conv_w",)
    sm = _adamw(_pack([wts[n] for n in names]), (_pack(small_g[:len(SMALL)] + [g_conv]),),
                _pack([mom_m[n] for n in names]), _pack([mom_v[n] for n in names]), "adamw_small")
    sm_shapes = [wts[n].shape for n in names]
    for kind, val in zip(("grad", "delta", "new_m", "new_v"), sm):
        for n, piece in zip(names, _unpack(val, sm_shapes)):
            outs[kind, n] = piece
    return (small_g[-1][0], grad_x, *[outs[kind, n] for kind in ("grad", "delta", "new_m", "new_v") for n in WEIGHTS])
```

```python
import functools

import jax
import jax.numpy as jnp
from jax import lax
from jax.experimental import pallas as pl
from jax.experimental.pallas import tpu as pltpu

F32 = jnp.float32
BF16 = jnp.bfloat16
HIGHEST = lax.Precision.HIGHEST
MESH = pl.DeviceIdType.MESH

EPS = 1e-6
HEAD_DIM = 64
SSM_GROUPS = 4
D_STATE = 128
CONV_WIDTH = 4
CHUNK = 128
N_POOL = 4
LANES = 128
N_SHARD = 4

ADAM_LR = 0.001
ADAM_B1 = 0.9
ADAM_B2 = 0.999
ADAM_EPS = 1e-08
ADAM_WD = 0.01
ADAM_STEP = 10

BIG = ("w_in", "pool_mix_w", "w_branch_a", "w_branch_b", "w_out", "w_ple_gate", "w_ple_up")
REST = BIG[1:]
SMALL = ("norm_g", "conv_b", "dt_bias", "a_log", "d_skip", "gnorm_g", "pool_mix_b", "pool_scale",
         "ple_norm_g", "final_g")
WEIGHTS = ("norm_g", "w_in", "conv_w", "conv_b", "dt_bias", "a_log", "d_skip", "gnorm_g", "pool_mix_w",
           "pool_mix_b", "pool_scale", "w_branch_a", "w_branch_b", "w_out", "ple_norm_g", "w_ple_gate",
           "w_ple_up", "final_g")


def _params(sem=None, vmem_mb=56):
    kw = dict(vmem_limit_bytes=vmem_mb << 20)
    if sem is not None:
        kw["dimension_semantics"] = sem
    return pltpu.CompilerParams(**kw)


def _sigmoid(v):
    return 0.5 * jnp.tanh(0.5 * v) + 0.5


def _sigmoid_tail(v):
    return 1.0 / (1.0 + jnp.exp(-v))


def _softplus(v):
    return jnp.maximum(v, 0.0) + jnp.log1p(jnp.exp(-jnp.abs(v)))


def _bdot(a, b):
    return jnp.dot(a.astype(BF16), b.astype(BF16), preferred_element_type=F32)


def _bdot_nt(a, b):
    return lax.dot_general(a.astype(BF16), b.astype(BF16), (((1,), (1,)), ((), ())), preferred_element_type=F32)


def _bdot_tn(a, b):
    return lax.dot_general(a.astype(BF16), b.astype(BF16), (((0,), (0,)), ((), ())), preferred_element_type=F32)


def _col_block(col0, width):
    assert col0 % width == 0, (col0, width)
    return col0 // width


def _tile(n, cap, unit=8):
    if n <= cap:
        return n
    best = None
    for cand in range(unit, cap + 1, unit):
        if n % cand == 0:
            best = cand
    assert best is not None, (n, cap)
    return best


def _shift_down(v, j, row):
    return jnp.where(row >= j, pltpu.roll(v, j, 0), 0.0)


def _shift_up(v, j, row):
    n = v.shape[0]
    return jnp.where(row < n - j, pltpu.roll(v, n - j, 0), 0.0)


def _mm(a, w, name, tm=1024, tn=1024):
    t, k = a.shape
    tm = min(tm, t)
    blocked = w.ndim == 3
    if blocked:
        nblk, _, tn = w.shape
        n = nblk * tn
        w_spec = pl.BlockSpec((1, k, tn), lambda i, j: (j, 0, 0))
    else:
        n = w.shape[1]
        tn = min(tn, n)
        w_spec = pl.BlockSpec((k, tn), lambda i, j: (0, j))

    def body(a_ref, w_ref, o_ref):
        wv = w_ref[0] if blocked else w_ref[...]
        o_ref[...] = _bdot(a_ref[...], wv)

    return pl.pallas_call(
        body, name=name, grid=(t // tm, n // tn),
        in_specs=[pl.BlockSpec((tm, k), lambda i, j: (i, 0)), w_spec],
        out_specs=pl.BlockSpec((tm, tn), lambda i, j: (i, j)),
        out_shape=jax.ShapeDtypeStruct((t, n), F32),
        compiler_params=_params(("parallel", "parallel")))(a, w)


def _mm_nt(a, w, name, tm=1024, tk=1024):
    t, k = a.shape
    n = w.shape[0]
    tm, tk = min(tm, t), min(tk, k)

    def body(a_ref, w_ref, o_ref):
        kk = pl.program_id(1)
        part = _bdot_nt(a_ref[...], w_ref[...])

        @pl.when(kk == 0)
        def _():
            o_ref[...] = part

        @pl.when(kk > 0)
        def _():
            o_ref[...] += part

    return pl.pallas_call(
        body, name=name, grid=(t // tm, k // tk),
        in_specs=[pl.BlockSpec((tm, tk), lambda i, j: (i, j)), pl.BlockSpec((n, tk), lambda i, j: (0, j))],
        out_specs=pl.BlockSpec((tm, n), lambda i, j: (i, 0)),
        out_shape=jax.ShapeDtypeStruct((t, n), F32),
        compiler_params=_params(("parallel", "arbitrary")))(a, w)


def _mm_tn(a, b, name, tn=1024, tk=1024, col_blocks=False):
    t, m = a.shape
    n = b.shape[1]
    tn, tk = min(tn, n), min(tk, t)

    def body(a_ref, b_ref, o_ref):
        kk = pl.program_id(1)
        part = _bdot_tn(a_ref[...], b_ref[...])
        part = part[None] if col_blocks else part

        @pl.when(kk == 0)
        def _():
            o_ref[...] = part

        @pl.when(kk > 0)
        def _():
            o_ref[...] += part

    if col_blocks:
        out_spec = pl.BlockSpec((1, m, tn), lambda j, kk: (j, 0, 0))
        out_shape = jax.ShapeDtypeStruct((n // tn, m, tn), F32)
    else:
        out_spec = pl.BlockSpec((m, tn), lambda j, kk: (0, j))
        out_shape = jax.ShapeDtypeStruct((m, n), F32)
    return pl.pallas_call(
        body, name=name, grid=(n // tn, t // tk),
        in_specs=[pl.BlockSpec((tk, m), lambda j, kk: (kk, 0)), pl.BlockSpec((tk, tn), lambda j, kk: (kk, j))],
        out_specs=out_spec, out_shape=out_shape,
        compiler_params=_params(("parallel", "arbitrary")))(a, b)


def _w_in_pieces(d, di, dc, nh, shard_w):
    pgd = d // N_POOL
    o_dt, o_u = di + dc, di + dc + nh
    o_zp, o_ga, o_gb = o_u + d, o_u + 2 * d, o_u + 3 * d
    c_z, c_uz = 2 * d, 2 * d + di + dc
    runs = [(False, 0, o_ga, d), (False, d, o_gb, d), (False, c_z, 0, di + dc), (True, 0, o_dt, nh)]
    for g in range(N_POOL):
        runs.append((False, c_uz + 2 * g * pgd, o_u + g * pgd, pgd))
        runs.append((False, c_uz + (2 * g + 1) * pgd, o_zp + g * pgd, pgd))
    pieces = []
    for is_dt, dst, src, n in runs:
        while n > 0:
            k, off = divmod(src, shard_w)
            m = min(n, shard_w - off)
            pieces.append((is_dt, dst, k, off, m))
            dst, src, n = dst + m, src + m, n - m
    return pieces


def _regroup_w_in(w_sh, d, di, dc, nh, rb=256):
    _, rows, sw = w_sh.shape
    n_main = 4 * d + di + dc
    pieces = _w_in_pieces(d, di, dc, nh, sw)
    rb = min(rb, rows)

    def body(w_ref, main_ref, dt_ref):
        dt_ref[...] = jnp.zeros_like(dt_ref)
        for is_dt, dst, k, off, m in pieces:
            out = dt_ref if is_dt else main_ref
            out[:, dst:dst + m] = w_ref[k, :, off:off + m]

    return pl.pallas_call(
        body, name="regroup_w_in", grid=(rows // rb,),
        in_specs=[pl.BlockSpec((N_SHARD, rb, sw), lambda i: (0, i, 0))],
        out_specs=[pl.BlockSpec((rb, n_main), lambda i: (i, 0)), pl.BlockSpec((rb, LANES), lambda i: (i, 0))],
        out_shape=[jax.ShapeDtypeStruct((rows, n_main), w_sh.dtype), jax.ShapeDtypeStruct((rows, LANES), w_sh.dtype)],
        compiler_params=_params(("parallel",)))(w_sh)


def _ungroup_w_in(d_main, d_dt, d, di, dc, nh, rb=128):
    rows, n_main = d_main.shape
    sw = (n_main + nh) // N_SHARD
    pieces = _w_in_pieces(d, di, dc, nh, sw)
    rb = min(rb, rows)

    def body(main_ref, dt_ref, o_ref):
        for is_dt, dst, k, off, m in pieces:
            src = dt_ref if is_dt else main_ref
            o_ref[k, :, off:off + m] = src[:, dst:dst + m]

    return pl.pallas_call(
        body, name="ungroup_w_in", grid=(rows // rb,),
        in_specs=[pl.BlockSpec((rb, n_main), lambda i: (i, 0)), pl.BlockSpec((rb, LANES), lambda i: (i, 0))],
        out_specs=pl.BlockSpec((N_SHARD, rb, sw), lambda i: (0, i, 0)),
        out_shape=jax.ShapeDtypeStruct((N_SHARD, rows, sw), F32),
        compiler_params=_params(("parallel",)))(d_main, d_dt)


def _inproj(x2, norm_g, w_main, w_dt, tm=1024, tn=1024):
    t, d = x2.shape
    n = w_main.shape[1]
    tm, tn = min(tm, t), min(tn, n)

    def body(x_ref, g_ref, w_ref, wdt_ref, proj_ref, dt_ref, h_ref):
        @pl.when(pl.program_id(1) == 0)
        def _():
            xv = x_ref[...]
            r = lax.rsqrt(jnp.mean(xv * xv, axis=-1, keepdims=True) + EPS)
            h = (xv * r * g_ref[...]).astype(BF16)
            h_ref[...] = h
            dt_ref[...] = jnp.dot(h, wdt_ref[...].astype(BF16), preferred_element_type=F32)

        proj_ref[...] = jnp.dot(h_ref[...], w_ref[...].astype(BF16), preferred_element_type=F32)

    return pl.pallas_call(
        body, name="inproj", grid=(t // tm, n // tn),
        in_specs=[pl.BlockSpec((tm, d), lambda i, j: (i, 0)), pl.BlockSpec((1, d), lambda i, j: (0, 0)),
                  pl.BlockSpec((d, tn), lambda i, j: (0, j)), pl.BlockSpec((d, LANES), lambda i, j: (0, 0))],
        out_specs=[pl.BlockSpec((tm, tn), lambda i, j: (i, j)), pl.BlockSpec((tm, LANES), lambda i, j: (i, 0)),
                   pl.BlockSpec((tm, d), lambda i, j: (i, 0))],
        out_shape=[jax.ShapeDtypeStruct((t, n), F32), jax.ShapeDtypeStruct((t, LANES), F32),
                   jax.ShapeDtypeStruct((t, d), BF16)],
        compiler_params=_params(("parallel", "arbitrary")))(x2, norm_g, w_main, w_dt)


def _conv_w_spec(conv_w, cb, j_axis):
    sw = conv_w.shape[2]
    assert sw % cb == 0, (sw, cb)
    per = sw // cb
    return N_SHARD * per, pl.BlockSpec((1, CONV_WIDTH, cb), lambda *ij: (ij[j_axis] // per, 0, ij[j_axis] % per))


CONV_ROWS = 64
CONV_HALO = 8


def _conv_taps(x_ref, t0, rc):
    if t0 == 0:
        cur = x_ref[0:rc, :]
        row = lax.broadcasted_iota(jnp.int32, cur.shape, 0)
        return [cur] + [_shift_down(cur, j, row) for j in range(1, CONV_WIDTH)]
    ext = x_ref[t0 - CONV_HALO:t0 + rc, :]
    return [ext[CONV_HALO:]] + [pltpu.roll(ext, j, 0)[CONV_HALO:] for j in range(1, CONV_WIDTH)]


def _conv_weights(w_ref):
    return [w_ref[0, CONV_WIDTH - 1 - j:CONV_WIDTH - j, :] for j in range(CONV_WIDTH)]


def _conv_pre(taps, wts, bias):
    acc = bias + taps[0] * wts[0]
    for j in range(1, CONV_WIDTH):
        acc = acc + taps[j] * wts[j]
    return acc


def _conv_fwd(proj, conv_w, conv_b, nb, s, col0, cb=256):
    n_blk, w_spec = _conv_w_spec(conv_w, cb, 1)
    blk0 = _col_block(col0, cb)
    rc = min(CONV_ROWS, s)

    def body(x_ref, w_ref, b_ref, o_ref):
        wts, bias = _conv_weights(w_ref), b_ref[...]
        for t0 in range(0, s, rc):
            acc = _conv_pre(_conv_taps(x_ref, t0, rc), wts, bias)
            o_ref[t0:t0 + rc, :] = acc * _sigmoid(acc)

    return pl.pallas_call(
        body, name="conv_fwd", grid=(nb, n_blk),
        in_specs=[pl.BlockSpec((s, cb), lambda b, j: (b, blk0 + j)), w_spec, pl.BlockSpec((1, cb), lambda b, j: (0, j))],
        out_specs=pl.BlockSpec((s, cb), lambda b, j: (b, j)),
        out_shape=jax.ShapeDtypeStruct((nb * s, n_blk * cb), F32),
        compiler_params=_params(("parallel", "parallel")))(proj, conv_w, conv_b)


def _ssd_consts(di):
    r = lax.broadcasted_iota(jnp.int32, (CHUNK, CHUNK), 0)
    c = lax.broadcasted_iota(jnp.int32, (CHUNK, CHUNK), 1)
    tril = (r >= c).astype(F32)
    head = lax.broadcasted_iota(jnp.int32, (LANES, di), 0)
    chan = lax.broadcasted_iota(jnp.int32, (LANES, di), 1) // HEAD_DIM
    expand = (head == chan).astype(BF16)
    return tril, expand, expand.T


def _expand(v, e, terms=3):
    acc = None
    for _ in range(terms):
        vb = v.astype(BF16)
        part = jnp.dot(vb, e, preferred_element_type=F32)
        acc = part if acc is None else acc + part
        v = v - vb.astype(F32)
    return acc


def _head_sum(t, et, terms=2):
    acc = None
    for _ in range(terms):
        tb = t.astype(BF16)
        part = jnp.dot(tb, et, preferred_element_type=F32)
        acc = part if acc is None else acc + part
        t = t - tb.astype(F32)
    return acc


def _ssd_scalars(dtr_ref, dtb_ref, alog_ref, tri):
    dtpre = dtr_ref[...] + dtb_ref[...]
    dt = _softplus(dtpre)
    a_neg = -jnp.exp(alog_ref[...])
    a_dt = dt * a_neg
    a_cs = jnp.dot(tri, a_dt, precision=HIGHEST, preferred_element_type=F32)
    a_cst = lax.dot_general(a_dt, tri, (((0,), (1,)), ((), ())), precision=HIGHEST, preferred_element_type=F32)
    return dtpre, dt, a_neg, a_cs, a_cst


def _ssd_fwd(proj, xbc, dtraw, dtb, alog, dskx, gn, nb, s, di, z_col0):
    t = nb * s
    nc = s // CHUNK
    hpg = di // HEAD_DIM // SSM_GROUPS
    gw = di // SSM_GROUPS
    gn_w = SSM_GROUPS * D_STATE
    b_blk = _col_block(di, gn_w)
    z_blk = _col_block(z_col0, di)
    L, P, N = CHUNK, HEAD_DIM, D_STATE
    tril, expand, _ = _ssd_consts(di)

    def body(z_ref, x_ref, b_ref, c_ref, dtr_ref, dtb_ref, alog_ref, dskx_ref, gn_ref, tril_ref, e_ref,
             ypre_ref, yan_ref, hp_ref, st_ref, yd_ref, xdt_ref):
        @pl.when(pl.program_id(1) == 0)
        def _():
            st_ref[...] = jnp.zeros_like(st_ref)

        hp_ref[0] = st_ref[...]
        tri = tril_ref[...]
        _, dt, _, a_cs, a_cst = _ssd_scalars(dtr_ref, dtb_ref, alog_ref, tri)
        ev = e_ref[...]
        a_exp = _expand(a_cs, ev)
        xv = x_ref[...]
        xdt = xv * _expand(dt, ev, terms=2)
        xdt_ref[...] = xdt
        a_last = a_exp[L - 1:L, :]
        xe = xdt * jnp.exp(a_last - a_exp)
        ea = jnp.exp(a_exp)
        e_last = jnp.exp(a_last)
        lower = tri > 0.5
        for g in range(SSM_GROUPS):
            gs = slice(g * gw, (g + 1) * gw)
            bg = b_ref[:, g * N:(g + 1) * N].astype(BF16)
            cg = c_ref[:, g * N:(g + 1) * N].astype(BF16)
            gm = _bdot_nt(cg, bg)
            ht = st_ref[:, gs]
            ch = _bdot(cg, ht)
            for e in range(hpg):
                h = g * hpg + e
                hs = slice(h * P, (h + 1) * P)
                decay = jnp.where(lower, jnp.exp(a_cs[:, h:h + 1] - a_cst[h:h + 1, :]), 0.0)
                yd_ref[:, hs] = _bdot(gm * decay, xdt_ref[:, hs])
            st_ref[:, gs] = ht * e_last[:, gs] + _bdot_tn(bg, xe[:, gs])
            ypre = yd_ref[:, gs] + ea[:, gs] * ch + xv[:, gs] * dskx_ref[:, gs]
            ypre_ref[:, gs] = ypre
            zv = z_ref[:, gs]
            v = ypre * zv * _sigmoid(zv)
            r = lax.rsqrt(jnp.mean(v * v, axis=-1, keepdims=True) + EPS)
            yan_ref[:, gs] = (v * r * gn_ref[:, gs]).astype(BF16)

    row = lambda b, c: b * nc + c
    vec = lambda w: pl.BlockSpec((1, w), lambda b, c: (0, 0))
    return pl.pallas_call(
        body, name="ssd_fwd", grid=(nb, nc),
        in_specs=[pl.BlockSpec((L, di), lambda b, c: (row(b, c), z_blk)),
                  pl.BlockSpec((L, di), lambda b, c: (row(b, c), 0)),
                  pl.BlockSpec((L, gn_w), lambda b, c: (row(b, c), b_blk)),
                  pl.BlockSpec((L, gn_w), lambda b, c: (row(b, c), b_blk + 1)),
                  pl.BlockSpec((L, LANES), lambda b, c: (row(b, c), 0)),
                  vec(LANES), vec(LANES), vec(di), vec(di),
                  pl.BlockSpec((L, L), lambda b, c: (0, 0)),
                  pl.BlockSpec((LANES, di), lambda b, c: (0, 0))],
        out_specs=[pl.BlockSpec((L, di), lambda b, c: (row(b, c), 0)),
                   pl.BlockSpec((L, di), lambda b, c: (row(b, c), 0)),
                   pl.BlockSpec((1, N, di), lambda b, c: (row(b, c), 0, 0))],
        out_shape=[jax.ShapeDtypeStruct((t, di), F32), jax.ShapeDtypeStruct((t, di), BF16),
                   jax.ShapeDtypeStruct((nb * nc, N, di), F32)],
        scratch_shapes=[pltpu.VMEM((N, di), F32), pltpu.VMEM((L, di), F32), pltpu.VMEM((L, di), F32)],
        compiler_params=_params(("parallel", "arbitrary")))(
            proj, xbc, xbc, xbc, dtraw, dtb, alog, dskx, gn, tril, expand)


def _pool_sum(v, g, row, shift):
    s2 = v + shift(v, 1, row)
    s4 = s2 + shift(s2, 2, row)
    s8 = s4 + shift(s4, 4, row)
    s16 = s8 + shift(s8, 8, row)
    return jnp.where(g == 0, s2, jnp.where(g == 1, s4, jnp.where(g == 2, s8, s16)))


def _pool_count(g, row):
    return jnp.minimum(row + 1, jnp.left_shift(2, g)).astype(F32)


def _pool_fwd(proj, mix_w, mix_b, scale, nb, s, col0):
    pgd = mix_w.shape[-1]
    blk0 = _col_block(col0, 2 * pgd)

    def body(uz_ref, w_ref, b_ref, sc_ref, o_ref):
        g = pl.program_id(1)
        u = uz_ref[:, :pgd]
        zp = uz_ref[:, pgd:]
        row = lax.broadcasted_iota(jnp.int32, u.shape, 0)
        pooled = _pool_sum(u, g, row, _shift_down) / _pool_count(g, row) - u
        mixed = _bdot(pooled, w_ref[:, 0].reshape(pgd, pgd)) + b_ref[...]
        o_ref[...] = (mixed * sc_ref[...] * zp * _sigmoid(zp)).astype(BF16)

    return pl.pallas_call(
        body, name="pool_fwd", grid=(nb, N_POOL),
        in_specs=[pl.BlockSpec((s, 2 * pgd), lambda b, g: (b, blk0 + g)),
                  pl.BlockSpec((N_SHARD, 1, pgd // N_SHARD, pgd), lambda b, g: (0, g, 0, 0)),
                  pl.BlockSpec((1, pgd), lambda b, g: (0, g)), pl.BlockSpec((1, pgd), lambda b, g: (0, g))],
        out_specs=pl.BlockSpec((s, pgd), lambda b, g: (b, g)),
        out_shape=jax.ShapeDtypeStruct((nb * s, N_POOL * pgd), BF16),
        compiler_params=_params(("parallel", "parallel")))(proj, mix_w, mix_b, scale)


def _mid_fwd(ya, yb, proj, col0, x2, p2, tgt, wo, wpg, wup, ple_g, final_g, tm=256):
    t, d = ya.shape
    tm = min(tm, t)
    blk = _col_block(col0, 2 * d)
    n_up, pdim, up_w = wup.shape

    def body(ya_ref, yb_ref, g_ref, x_ref, p_ref, tgt_ref, wo_ref, wpg_ref, wup_ref, pg_ref, g_fin_ref,
             merged_ref, hn_ref, dpre_ref, dpu_ref, x1_ref, dx2_ref, loss_ref, dg_ref):
        @pl.when(pl.program_id(0) == 0)
        def _():
            loss_ref[...] = jnp.zeros_like(loss_ref)
            dg_ref[...] = jnp.zeros_like(dg_ref)

        merged = (_sigmoid(g_ref[:, :d]) * ya_ref[...] + _sigmoid(g_ref[:, d:]) * yb_ref[...]).astype(BF16)
        merged_ref[...] = merged
        x1 = x_ref[...] + jnp.dot(merged, wo_ref[...], preferred_element_type=F32)
        x1_ref[...] = x1
        r1 = lax.rsqrt(jnp.mean(x1 * x1, axis=-1, keepdims=True) + EPS)
        hn = (x1 * r1 * pg_ref[...]).astype(BF16)
        hn_ref[...] = hn
        gate = _sigmoid(jnp.dot(hn, wpg_ref[...], preferred_element_type=F32))
        pb = p_ref[...].astype(BF16)
        pu = jnp.concatenate([jnp.dot(pb, wup_ref[j], preferred_element_type=F32) for j in range(n_up)], axis=1)
        x2 = x1 + gate * pu
        r = lax.rsqrt(jnp.mean(x2 * x2, axis=-1, keepdims=True) + EPS)
        xn = x2 * r
        fg = g_fin_ref[...]
        err = xn * fg - tgt_ref[...]
        loss_ref[...] += 0.5 * jnp.sum(jnp.mean(err * err, axis=-1, keepdims=True))
        dy = err * (1.0 / d)
        dg_ref[...] += jnp.sum(dy * xn, axis=0, keepdims=True)
        dxn = dy * fg
        dx2 = r * (dxn - xn * jnp.mean(dxn * xn, axis=-1, keepdims=True))
        dx2_ref[...] = dx2
        dpre_ref[...] = (dx2 * pu * gate * (1.0 - gate)).astype(BF16)
        dpu_ref[...] = (dx2 * gate).astype(BF16)

    row = pl.BlockSpec((tm, d), lambda i: (i, 0))
    vec = pl.BlockSpec((1, d), lambda i: (0, 0))
    whole = lambda a: pl.BlockSpec(a.shape, lambda i: (0,) * a.ndim)
    return pl.pallas_call(
        body, name="mid_fwd", grid=(t // tm,),
        in_specs=[row, row, pl.BlockSpec((tm, 2 * d), lambda i: (i, blk)), row,
                  pl.BlockSpec((tm, pdim), lambda i: (i, 0)), row, whole(wo), whole(wpg), whole(wup), vec, vec],
        out_specs=[row] * 6 + [pl.BlockSpec((1, LANES), lambda i: (0, 0)), vec],
        out_shape=[jax.ShapeDtypeStruct((t, d), BF16)] * 4 + [jax.ShapeDtypeStruct((t, d), F32)] * 2 + [
            jax.ShapeDtypeStruct((1, LANES), F32), jax.ShapeDtypeStruct((1, d), F32)],
        compiler_params=_params(("arbitrary",)))(ya, yb, proj, x2, p2, tgt, wo, wpg, wup, ple_g, final_g)


def _rms_grad(xv, dh, g):
    r = lax.rsqrt(jnp.mean(xv * xv, axis=-1, keepdims=True) + EPS)
    xn = xv * r
    dd = dh * g
    return r * (dd - xn * jnp.mean(dd * xn, axis=-1, keepdims=True)), jnp.sum(dh * xn, axis=0, keepdims=True)


def _mid_bwd(dpre, dx2, x1, ya, yb, proj, col0, wpg, wo, ple_g, n_cols, tm=256):
    t, d = ya.shape
    tm = min(tm, t)
    blk = _col_block(col0, 2 * d)

    def body(dpre_ref, dx2_ref, x1_ref, ya_ref, yb_ref, g_ref, wpg_ref, wo_ref, pg_ref,
             dx1_ref, dya_ref, dyb_ref, dg_ref, dpg_ref):
        @pl.when(pl.program_id(0) == 0)
        def _():
            dpg_ref[...] = jnp.zeros_like(dpg_ref)

        dhn = _bdot_nt(dpre_ref[...], wpg_ref[...])
        dx, dpg = _rms_grad(x1_ref[...], dhn, pg_ref[...])
        dpg_ref[...] += dpg
        dx1 = dx2_ref[...] + dx
        dx1_ref[...] = dx1
        dm_v = _bdot_nt(dx1, wo_ref[...])
        sa = _sigmoid(g_ref[:, :d])
        sb = _sigmoid(g_ref[:, d:])
        dya_ref[...] = (dm_v * sa).astype(BF16)
        dyb_ref[...] = (dm_v * sb).astype(BF16)
        dg_ref[:, :d] = (dm_v * ya_ref[...] * sa * (1.0 - sa)).astype(BF16)
        dg_ref[:, d:] = (dm_v * yb_ref[...] * sb * (1.0 - sb)).astype(BF16)

    row = pl.BlockSpec((tm, d), lambda i: (i, 0))
    vec = pl.BlockSpec((1, d), lambda i: (0, 0))
    gspec = pl.BlockSpec((tm, 2 * d), lambda i: (i, blk))
    whole = lambda a: pl.BlockSpec(a.shape, lambda i: (0,) * a.ndim)
    return pl.pallas_call(
        body, name="mid_bwd", grid=(t // tm,),
        in_specs=[row, row, row, row, row, gspec, whole(wpg), whole(wo), vec],
        out_specs=[row, row, row, gspec, vec],
        out_shape=[jax.ShapeDtypeStruct((t, d), F32), jax.ShapeDtypeStruct((t, d), BF16),
                   jax.ShapeDtypeStruct((t, d), BF16), jax.ShapeDtypeStruct((t, n_cols), BF16),
                   jax.ShapeDtypeStruct((1, d), F32)],
        compiler_params=_params(("arbitrary",)))(dpre, dx2, x1, ya, yb, proj, wpg, wo, ple_g)


def _in_bwd(dproj, w_main, ddt, w_dt, x2, dx1, norm_g, after, tm=1024, tk=1024):
    t, k = dproj.shape
    d = x2.shape[1]
    tm, tk = min(tm, t), min(tk, k)
    nk = k // tk

    def body(a_ref, w_ref, ddt_ref, wdt_ref, x_ref, dres_ref, g_ref, _, gx_ref, dg_ref, acc_ref):
        kk = pl.program_id(1)

        @pl.when((pl.program_id(0) == 0) & (kk == 0))
        def _():
            dg_ref[...] = jnp.zeros_like(dg_ref)

        part = _bdot_nt(a_ref[...], w_ref[...])

        @pl.when(kk == 0)
        def _():
            acc_ref[...] = part

        @pl.when(kk > 0)
        def _():
            acc_ref[...] += part

        @pl.when(kk == nk - 1)
        def _():
            dh = acc_ref[...] + _bdot_nt(ddt_ref[...], wdt_ref[...])
            dx, dg = _rms_grad(x_ref[...], dh, g_ref[...])
            dg_ref[...] += dg
            gx_ref[...] = dres_ref[...] + dx

    row = pl.BlockSpec((tm, d), lambda i, j: (i, 0))
    vec = pl.BlockSpec((1, d), lambda i, j: (0, 0))
    return pl.pallas_call(
        body, name="in_bwd", grid=(t // tm, nk),
        in_specs=[pl.BlockSpec((tm, tk), lambda i, j: (i, j)), pl.BlockSpec((d, tk), lambda i, j: (0, j)),
                  pl.BlockSpec((tm, LANES), lambda i, j: (i, 0)), pl.BlockSpec((d, LANES), lambda i, j: (0, 0)),
                  row, row, vec, pl.BlockSpec((8, LANES), lambda i, j: (0, 0))],
        out_specs=[row, vec],
        out_shape=[jax.ShapeDtypeStruct((t, d), F32), jax.ShapeDtypeStruct((1, d), F32)],
        scratch_shapes=[pltpu.VMEM((tm, d), F32)],
        compiler_params=_params(("arbitrary", "arbitrary")))(dproj, w_main, ddt, w_dt, x2, dx1, norm_g, after)


def _pool_bwd(proj, dyb, dproj, mix_w, mix_b, scale, nb, s, col0):
    pgd = mix_w.shape[-1]
    blk0 = _col_block(col0, 2 * pgd)

    def body(uz_ref, dy_ref, _, w_ref, b_ref, sc_ref, duz_ref, dw_ref, db_ref, dsc_ref):
        g = pl.program_id(0)

        @pl.when(pl.program_id(1) == 0)
        def _():
            dw_ref[...] = jnp.zeros_like(dw_ref)
            db_ref[...] = jnp.zeros_like(db_ref)
            dsc_ref[...] = jnp.zeros_like(dsc_ref)

        u = uz_ref[:, :pgd]
        zp = uz_ref[:, pgd:]
        row = lax.broadcasted_iota(jnp.int32, u.shape, 0)
        cnt = _pool_count(g, row)
        pooled = _pool_sum(u, g, row, _shift_down) / cnt - u
        wv = w_ref[:, 0].reshape(pgd, pgd)
        mixed = _bdot(pooled, wv) + b_ref[...]
        sg = _sigmoid(zp)
        sz = zp * sg
        dy = dy_ref[...]
        sc = sc_ref[...]
        dsc_ref[...] += jnp.sum(dy * mixed * sz, axis=0, keepdims=True)
        dmixed = dy * sc * sz
        db_ref[...] += jnp.sum(dmixed, axis=0, keepdims=True)
        dw_ref[:, 0] += _bdot_tn(pooled, dmixed).reshape(N_SHARD, pgd // N_SHARD, pgd)
        dpooled = _bdot_nt(dmixed, wv)
        duz_ref[:, :pgd] = (_pool_sum(dpooled / cnt, g, row, _shift_up) - dpooled).astype(BF16)
        duz_ref[:, pgd:] = (dy * mixed * sc * sg * (1.0 + zp * (1.0 - sg))).astype(BF16)

    uz = pl.BlockSpec((s, 2 * pgd), lambda g, b: (b, blk0 + g))
    vec = pl.BlockSpec((1, pgd), lambda g, b: (0, g))
    wspec = pl.BlockSpec((N_SHARD, 1, pgd // N_SHARD, pgd), lambda g, b: (0, g, 0, 0))
    return pl.pallas_call(
        body, name="pool_bwd", grid=(N_POOL, nb),
        in_specs=[uz, pl.BlockSpec((s, pgd), lambda g, b: (b, g)), pl.BlockSpec(memory_space=pl.ANY), wspec, vec, vec],
        out_specs=[uz, wspec, vec, vec],
        out_shape=[jax.ShapeDtypeStruct(dproj.shape, dproj.dtype), jax.ShapeDtypeStruct(mix_w.shape, F32),
                   jax.ShapeDtypeStruct(mix_b.shape, F32), jax.ShapeDtypeStruct(scale.shape, F32)],
        input_output_aliases={2: 0},
        compiler_params=_params(("parallel", "arbitrary")))(proj, dyb, dproj, mix_w, mix_b, scale)


def _ssd_bwd(dyan, ypre, proj, xbc, dtraw, hp, dproj, dtb, alog, dskx, gn, nb, s, di, z_col0):
    t = nb * s
    nc = s // CHUNK
    hpg = di // HEAD_DIM // SSM_GROUPS
    gw = di // SSM_GROUPS
    gn_w = SSM_GROUPS * D_STATE
    dc = di + 2 * gn_w
    b_blk = _col_block(di, gn_w)
    z_blk = _col_block(z_col0, di)
    L, P, N = CHUNK, HEAD_DIM, D_STATE
    tril, expand, expand_t = _ssd_consts(di)

    def body(dy_ref, ypre_ref, z_ref, x_ref, b_ref, c_ref, dtr_ref, hp_ref, _, dtb_ref, alog_ref, dskx_ref, gn_ref,
             tril_ref, e_ref, et_ref, dz_ref, ddt_ref, dxbc_ref, dgn_ref, ddsk_ref, dalog_ref, ddtb_ref,
             dst_ref, dyp_ref, xdt_ref, dxm_ref, t1_ref, t3_ref, aux_ref):
        @pl.when((pl.program_id(0) == 0) & (pl.program_id(1) == 0))
        def _():
            dgn_ref[...] = jnp.zeros_like(dgn_ref)
            ddsk_ref[...] = jnp.zeros_like(ddsk_ref)
            dalog_ref[...] = jnp.zeros_like(dalog_ref)
            ddtb_ref[...] = jnp.zeros_like(ddtb_ref)

        @pl.when(pl.program_id(1) == 0)
        def _():
            dst_ref[...] = jnp.zeros_like(dst_ref)

        tri = tril_ref[...]
        dtpre, dt, a_neg, a_cs, a_cst = _ssd_scalars(dtr_ref, dtb_ref, alog_ref, tri)
        ev = e_ref[...]
        a_exp = _expand(a_cs, ev)
        dt_exp = _expand(dt, ev, terms=2)
        xv = x_ref[...]
        xdt = xv * dt_exp
        xdt_ref[...] = xdt
        a_last = a_exp[L - 1:L, :]
        dte = jnp.exp(a_last - a_exp)
        xe = xdt * dte
        ea = jnp.exp(a_exp)
        e_last = jnp.exp(a_last)
        lower = tri > 0.5
        aux_ref[...] = jnp.zeros_like(aux_ref)
        for g in range(SSM_GROUPS):
            gs = slice(g * gw, (g + 1) * gw)
            zv = z_ref[:, gs]
            yp = ypre_ref[:, gs]
            sg = _sigmoid(zv)
            sz = zv * sg
            vg = yp * sz
            r = lax.rsqrt(jnp.mean(vg * vg, axis=-1, keepdims=True) + EPS)
            vn = vg * r
            dyg = dy_ref[:, gs]
            dgn_ref[:, gs] += jnp.sum(dyg * vn, axis=0, keepdims=True)
            dvn = dyg * gn_ref[:, gs]
            dv = r * (dvn - vn * jnp.mean(dvn * vn, axis=-1, keepdims=True))
            dy = dv * sz
            dyp_ref[:, gs] = dy
            dz_ref[:, gs] = (dv * yp * sg * (1.0 + zv * (1.0 - sg))).astype(BF16)
            bg = b_ref[:, g * N:(g + 1) * N].astype(BF16)
            cg = c_ref[:, g * N:(g + 1) * N].astype(BF16)
            gm = _bdot_nt(cg, bg)
            ht = hp_ref[0, :, gs]
            dht = dst_ref[:, gs]
            bds = _bdot(bg, dht)
            dye = dy * ea[:, gs]
            xe_g = xe[:, gs]
            dcg = _bdot_nt(dye, ht)
            dbg = _bdot_nt(xe_g, dht)
            dst_ref[:, gs] = e_last[:, gs] * dht + _bdot_tn(cg, dye)
            dgm = jnp.zeros((L, L), F32)
            for e in range(hpg):
                h = g * hpg + e
                hs = slice(h * P, (h + 1) * P)
                decay = jnp.where(lower, jnp.exp(a_cs[:, h:h + 1] - a_cst[h:h + 1, :]), 0.0)
                dy_h = dyp_ref[:, hs]
                dgm = dgm + _bdot_nt(dy_h, xdt_ref[:, hs]) * decay
                dxm_ref[:, hs] = _bdot_tn(gm * decay, dy_h)
            dxbc_ref[:, di + g * N:di + (g + 1) * N] = dbg + _bdot_tn(dgm, cg)
            dxbc_ref[:, di + gn_w + g * N:di + gn_w + (g + 1) * N] = dcg + _bdot(dgm, bg)
            dxm = dxm_ref[:, gs]
            x_g = xv[:, gs]
            dskx = dskx_ref[:, gs]
            xeb = xe_g * bds
            dxdt = dxm + dte[:, gs] * bds
            dxbc_ref[:, gs] = dxdt * dt_exp[:, gs] + dy * dskx
            each = ea[:, gs] * _bdot(cg, ht)
            y_diag = yp - x_g * dskx - each
            rnd = lambda v: v.astype(BF16).astype(F32)
            t1_ref[:, gs] = rnd(dy) * y_diag + dy * each - rnd(xdt[:, gs]) * dxm - xeb
            t3_ref[:, gs] = dxdt * x_g
            aux_ref[0:1, gs] = jnp.sum(dht * ht, axis=0, keepdims=True)
            aux_ref[1:2, gs] = jnp.sum(dy * x_g, axis=0, keepdims=True)
            aux_ref[2:3, gs] = jnp.sum(xeb, axis=0, keepdims=True)
        etv = et_ref[...]
        aux = _head_sum(aux_ref[...], etv)
        rowi = lax.broadcasted_iota(jnp.int32, (L, LANES), 0)
        end = aux[2:3, :] + aux[0:1, :] * jnp.exp(a_cs[L - 1:L, :])
        da = _head_sum(t1_ref[...], etv, terms=3) + jnp.where(rowi == L - 1, end, 0.0)
        rc = lax.dot_general(tri, da, (((0,), (0,)), ((), ())), precision=HIGHEST, preferred_element_type=F32)
        ddt = a_neg * rc + _head_sum(t3_ref[...], etv, terms=1)
        ddtraw = ddt * _sigmoid_tail(dtpre)
        ddt_ref[...] = ddtraw.astype(BF16)
        ddtb_ref[...] += jnp.sum(ddtraw, axis=0, keepdims=True)
        dalog_ref[...] += jnp.sum(dt * rc, axis=0, keepdims=True) * a_neg
        ddsk_ref[...] += aux[1:2, :]

    row = lambda b, c: b * nc + (nc - 1 - c)
    full = lambda w: pl.BlockSpec((L, w), lambda b, c: (row(b, c), 0))
    zspec = pl.BlockSpec((L, di), lambda b, c: (row(b, c), z_blk))
    vec = lambda w: pl.BlockSpec((1, w), lambda b, c: (0, 0))
    slab = lambda shape: pltpu.VMEM(shape, F32)
    return pl.pallas_call(
        body, name="ssd_bwd", grid=(nb, nc),
        in_specs=[full(di), full(di), zspec, full(di),
                  pl.BlockSpec((L, gn_w), lambda b, c: (row(b, c), b_blk)),
                  pl.BlockSpec((L, gn_w), lambda b, c: (row(b, c), b_blk + 1)),
                  full(LANES),
                  pl.BlockSpec((1, N, di), lambda b, c: (row(b, c), 0, 0)),
                  pl.BlockSpec(memory_space=pl.ANY),
                  vec(LANES), vec(LANES), vec(di), vec(di),
                  pl.BlockSpec((L, L), lambda b, c: (0, 0)),
                  pl.BlockSpec((LANES, di), lambda b, c: (0, 0)),
                  pl.BlockSpec((di, LANES), lambda b, c: (0, 0))],
        out_specs=[zspec, full(LANES), full(dc), vec(di), vec(LANES), vec(LANES), vec(LANES)],
        out_shape=[jax.ShapeDtypeStruct(dproj.shape, dproj.dtype), jax.ShapeDtypeStruct((t, LANES), BF16),
                   jax.ShapeDtypeStruct((t, dc), F32), jax.ShapeDtypeStruct((1, di), F32),
                   jax.ShapeDtypeStruct((1, LANES), F32), jax.ShapeDtypeStruct((1, LANES), F32),
                   jax.ShapeDtypeStruct((1, LANES), F32)],
        scratch_shapes=[slab((N, di)), slab((L, di)), slab((L, di)), slab((L, di)), slab((L, di)), slab((L, di)),
                        slab((8, di))],
        input_output_aliases={8: 0},
        compiler_params=_params(("arbitrary", "arbitrary")))(
            dyan, ypre, proj, xbc, xbc, xbc, dtraw, hp, dproj, dtb, alog, dskx, gn, tril, expand, expand_t)


def _conv_bwd(proj, dxbc, dproj, conv_w, conv_b, nb, s, col0, cb=256):
    n_blk, w_spec = _conv_w_spec(conv_w, cb, 0)
    blk0 = _col_block(col0, cb)
    rc = min(CONV_ROWS, s)

    def body(x_ref, dy_ref, _, w_ref, b_ref, dx_ref, dw_ref, db_ref, dacc_ref):
        @pl.when(pl.program_id(1) == 0)
        def _():
            dw_ref[...] = jnp.zeros_like(dw_ref)
            db_ref[...] = jnp.zeros_like(db_ref)

        wts, bias = _conv_weights(w_ref), b_ref[...]
        fold = lambda v: v.reshape(rc // 8, 8, cb).sum(axis=0)
        db8 = jnp.zeros((8, cb), F32)
        dw8 = [jnp.zeros((8, cb), F32) for _ in range(CONV_WIDTH)]
        for t0 in range(0, s, rc):
            taps = _conv_taps(x_ref, t0, rc)
            acc = _conv_pre(taps, wts, bias)
            sg = _sigmoid(acc)
            dacc = dy_ref[t0:t0 + rc, :] * sg * (1.0 + acc * (1.0 - sg))
            dacc_ref[t0:t0 + rc, :] = dacc
            db8 = db8 + fold(dacc)
            dw8 = [dw8[j] + fold(dacc * taps[j]) for j in range(CONV_WIDTH)]
        db_ref[...] += jnp.sum(db8, axis=0, keepdims=True)
        for j in range(CONV_WIDTH):
            dw_ref[0, CONV_WIDTH - 1 - j:CONV_WIDTH - j, :] += jnp.sum(dw8[j], axis=0, keepdims=True)
        for t0 in range(0, s, rc):
            if t0 + rc < s:
                n = rc + CONV_HALO
                win = dacc_ref[t0:t0 + n, :]
                ups = [win[:rc]] + [pltpu.roll(win, n - j, 0)[:rc] for j in range(1, CONV_WIDTH)]
            else:
                cur = dacc_ref[t0:t0 + rc, :]
                row = lax.broadcasted_iota(jnp.int32, cur.shape, 0)
                ups = [cur] + [_shift_up(cur, j, row) for j in range(1, CONV_WIDTH)]
            dx = ups[0] * wts[0]
            for j in range(1, CONV_WIDTH):
                dx = dx + ups[j] * wts[j]
            dx_ref[t0:t0 + rc, :] = dx.astype(BF16)

    return pl.pallas_call(
        body, name="conv_bwd", grid=(n_blk, nb),
        in_specs=[pl.BlockSpec((s, cb), lambda j, b: (b, blk0 + j)), pl.BlockSpec((s, cb), lambda j, b: (b, j)),
                  pl.BlockSpec(memory_space=pl.ANY), w_spec, pl.BlockSpec((1, cb), lambda j, b: (0, j))],
        out_specs=[pl.BlockSpec((s, cb), lambda j, b: (b, blk0 + j)), w_spec, pl.BlockSpec((1, cb), lambda j, b: (0, j))],
        out_shape=[jax.ShapeDtypeStruct(dproj.shape, dproj.dtype), jax.ShapeDtypeStruct(conv_w.shape, F32),
                   jax.ShapeDtypeStruct(conv_b.shape, F32)],
        scratch_shapes=[pltpu.VMEM((s, cb), F32)],
        input_output_aliases={2: 0},
        compiler_params=_params(("parallel", "arbitrary")))(proj, dxbc, dproj, conv_w, conv_b)


def _local_step(x, p, tgt, wg, small, rest_weights, early_grads, w_in_grad):
    nb, s, d = x.shape
    t = nb * s
    gn_w = SSM_GROUPS * D_STATE
    dc = N_SHARD * wg["conv_w"].shape[2]
    di = dc - 2 * gn_w
    nh = di // HEAD_DIM
    pgd = d // N_POOL
    x2 = x.reshape(t, d)
    p2 = p.reshape(t, p.shape[-1])
    tgt2 = tgt.reshape(t, d)

    w_main, w_dt = _regroup_w_in(wg["w_in"], d, di, dc, nh)
    c_g, c_z, c_xbc, c_uz = 0, 2 * d, 2 * d + di, 2 * d + di + dc
    n_main = w_main.shape[1]

    pad_h = lambda v: jnp.pad(v.reshape(1, nh).astype(F32), ((0, 0), (0, LANES - nh)))
    dtb, alog = pad_h(small["dt_bias"]), pad_h(small["a_log"])
    dskx = jnp.repeat(small["d_skip"].reshape(1, nh).astype(F32), HEAD_DIM, axis=1)
    vec = lambda v: v.reshape(1, -1).astype(F32)
    norm_g, gn, conv_b = vec(small["norm_g"]), vec(small["gnorm_g"]), vec(small["conv_b"])
    mix_b, scale = vec(small["pool_mix_b"]), vec(small["pool_scale"])
    ple_g, final_g = vec(small["ple_norm_g"]), vec(small["final_g"])
    conv_w = wg["conv_w"]

    wide = _tile(n_main, 2304, LANES)
    proj, dtraw, h = _inproj(x2, norm_g, w_main, w_dt, tn=wide)
    xbc = _conv_fwd(proj, conv_w, conv_b, nb, s, c_xbc)
    ypre, yan, hp = _ssd_fwd(proj, xbc, dtraw, dtb, alog, dskx, gn, nb, s, di, c_z)
    wr = rest_weights(yan)
    mix_w = wr["pool_mix_w"].reshape(N_SHARD, N_POOL, pgd // N_SHARD, pgd)
    rows = lambda v: v.reshape(-1, v.shape[-1])
    wa, wb, wo, wpg = rows(wr["w_branch_a"]), rows(wr["w_branch_b"]), rows(wr["w_out"]), rows(wr["w_ple_gate"])
    wup = wr["w_ple_up"]
    ybp = _pool_fwd(proj, mix_w, mix_b, scale, nb, s, c_uz)
    ya = _mm(yan, wa, "branch_a")
    yb = _mm(ybp, wb, "branch_b")
    merged, hn, dpre, dpu, x1, dx2, loss, d_final_g = _mid_fwd(
        ya, yb, proj, c_g, x2, p2, tgt2, wo, wpg, wup, ple_g, final_g)

    d_wpg = _mm_tn(hn, dpre, "d_w_ple_gate")
    d_wup = _mm_tn(p2, dpu, "d_w_ple_up", tn=wup.shape[-1], col_blocks=True)
    dx1, dya, dyb, dproj, d_ple_g = _mid_bwd(dpre, dx2, x1, ya, yb, proj, c_g, wpg, wo, ple_g, n_main)
    d_wo = _mm_tn(merged, dx1, "d_w_out")
    d_wa = _mm_tn(yan, dya, "d_w_branch_a")
    d_wb = _mm_tn(ybp, dyb, "d_w_branch_b")
    dyan = _mm_nt(dya, wa, "d_y_a")
    dybp = _mm_nt(dyb, wb, "d_y_b")
    dproj, d_mix_w, d_mix_b, d_scale = _pool_bwd(proj, dybp, dproj, mix_w, mix_b, scale, nb, s, c_uz)
    shard_major = lambda v: v.reshape(N_SHARD, v.shape[0] // N_SHARD, v.shape[1])
    early = dict(pool_mix_w=d_mix_w.reshape(N_SHARD, pgd, pgd), w_branch_a=shard_major(d_wa),
                 w_branch_b=shard_major(d_wb), w_out=shard_major(d_wo), w_ple_gate=shard_major(d_wpg),
                 w_ple_up=d_wup)
    token = early_grads(early)
    dproj, ddt, dxbc, d_gn, d_dsk, d_alog, d_dtb = _ssd_bwd(
        dyan, ypre, proj, xbc, dtraw, hp, dproj, dtb + token[0:1, 0:1], alog, dskx, gn, nb, s, di, c_z)
    dproj, d_conv_w, d_conv_b = _conv_bwd(proj, dxbc, dproj, conv_w, conv_b, nb, s, c_xbc)
    d_wmain = _mm_tn(h, dproj, "d_w_in", tk=2048)
    d_wdt = _mm_tn(h, ddt, "d_w_dt")
    d_w_in = _ungroup_w_in(d_wmain, d_wdt, d, di, dc, nh)
    token = w_in_grad(d_w_in)
    gx, d_norm_g = _in_bwd(dproj, w_main, ddt, w_dt, x2, dx1, norm_g, token, tk=_tile(n_main, 1536, LANES))

    grads = dict(norm_g=d_norm_g, w_in=d_w_in, conv_w=d_conv_w, conv_b=d_conv_b, dt_bias=d_dtb[:, :nh],
                 a_log=d_alog[:, :nh], d_skip=d_dsk[:, :nh], gnorm_g=d_gn, pool_mix_b=d_mix_b, pool_scale=d_scale,
                 ple_norm_g=d_ple_g, final_g=d_final_g, **early)
    return loss[0, 0], gx.reshape(nb, s, d), grads


def _place():
    return lax.axis_index("x"), lax.axis_index("y"), lax.axis_index("c")


def _other_chips(x, y):
    return [(1 - x, y), (x, 1 - y), (1 - x, 1 - y)]


def _halves(c, rows, align):
    rh = rows // 2
    assert rows % 2 == 0 and rh % align == 0, (rows, align)
    return (pl.ds(pl.multiple_of(c * rh, align), rh), pl.ds(pl.multiple_of((1 - c) * rh, align), rh))


HBM = pl.BlockSpec(memory_space=pl.ANY)


def _into_slot(w2, k, dtype, name):
    rows, cols = w2.shape
    rb = _tile(rows, 256)

    def body(k_ref, w_ref, o_ref):
        o_ref[0] = w_ref[...].astype(dtype)

    return pl.pallas_call(
        body, name=name,
        grid_spec=pltpu.PrefetchScalarGridSpec(
            num_scalar_prefetch=1, grid=(rows // rb,),
            in_specs=[pl.BlockSpec((rb, cols), lambda i, k_ref: (i, 0))],
            out_specs=pl.BlockSpec((1, rb, cols), lambda i, k_ref: (k_ref[0], i, 0))),
        out_shape=jax.ShapeDtypeStruct((N_SHARD, rows, cols), dtype),
        compiler_params=_params(("parallel",)))(k.reshape(1), w2)


def _gather_weights(split, whole):
    n_split, n_all = len(split), len(split) + len(whole)

    def body(*refs):
        bufs = refs[n_all:2 * n_all]
        send_sems, recv_sems = refs[2 * n_all:]
        x, y, c = _place()
        k = 2 * x + y
        chips = _other_chips(x, y)

        def copy(idx, block, to):
            return pltpu.make_async_remote_copy(src_ref=block, dst_ref=block, send_sem=send_sems.at[idx],
                                                recv_sem=recv_sems.at[idx], device_id=to, device_id_type=MESH)

        def block(i, shard, rows):
            return bufs[i].at[shard, rows] if i < n_split else bufs[i].at[shard]

        def sem(i, j):
            return 6 * i + j if i < n_split else 6 * n_split + 3 * (i - n_split) + j

        started = []
        for i in range(n_all):
            mine, _ = _halves(c, bufs[i].shape[1], 16) if i < n_split else (None, None)
            for j, (px, py) in enumerate(chips):
                started.append(copy(sem(i, j), block(i, k, mine), (px, py, c)))
                started[-1].start()
        for i in range(n_all):
            mine, _ = _halves(c, bufs[i].shape[1], 16) if i < n_split else (None, None)
            for j, (px, py) in enumerate(chips):
                landed = block(i, 2 * px + py, mine)
                copy(sem(i, j), landed, (px, py, c)).wait_recv()
                if i < n_split:
                    started.append(copy(sem(i, 3 + j), landed, (x, y, 1 - c)))
                    started[-1].start()
        for i in range(n_split):
            _, theirs = _halves(c, bufs[i].shape[1], 16)
            for j, (px, py) in enumerate(chips):
                copy(sem(i, 3 + j), block(i, 2 * px + py, theirs), (x, y, 1 - c)).wait_recv()
        for cp in started:
            cp.wait_send()

    arrays = list(split) + list(whole)
    n_sem = 6 * n_split + 3 * len(whole)
    return pl.pallas_call(
        body, name="gather_weights",
        in_specs=[HBM] * n_all, out_specs=[HBM] * n_all,
        out_shape=[jax.ShapeDtypeStruct(a.shape, a.dtype) for a in arrays],
        input_output_aliases={i: i for i in range(n_all)},
        scratch_shapes=[pltpu.SemaphoreType.DMA((n_sem,)), pltpu.SemaphoreType.DMA((n_sem,))],
    )(*arrays)


def _swap_halves(gs):
    n = len(gs)

    def body(*refs):
        ins, outs, send_sems, recv_sems = refs[:n], refs[n:2 * n], refs[2 * n], refs[2 * n + 1]
        x, y, c = _place()
        copies = []
        for i in range(n):
            _, theirs = _halves(c, gs[i].shape[1], 8)
            cp = pltpu.make_async_remote_copy(src_ref=ins[i].at[:, theirs], dst_ref=outs[i], send_sem=send_sems.at[i],
                                              recv_sem=recv_sems.at[i], device_id=(x, y, 1 - c), device_id_type=MESH)
            cp.start()
            copies.append(cp)
        for cp in copies:
            cp.wait()

    return pl.pallas_call(
        body, name="swap_halves", in_specs=[HBM] * n, out_specs=[HBM] * n,
        out_shape=[jax.ShapeDtypeStruct((g.shape[0], g.shape[1] // 2, g.shape[2]), g.dtype) for g in gs],
        scratch_shapes=[pltpu.SemaphoreType.DMA((n,)), pltpu.SemaphoreType.DMA((n,))],
    )(*gs)


def _join_halves(vs):
    n = len(vs)

    def body(*refs):
        bufs, send_sems, recv_sems = refs[n:2 * n], refs[2 * n], refs[2 * n + 1]
        x, y, c = _place()

        def copy(i, rows):
            return pltpu.make_async_remote_copy(src_ref=bufs[i].at[rows], dst_ref=bufs[i].at[rows],
                                                send_sem=send_sems.at[i], recv_sem=recv_sems.at[i],
                                                device_id=(x, y, 1 - c), device_id_type=MESH)

        halves = [_halves(c, bufs[i].shape[0], 8) for i in range(n)]
        sends = [copy(i, halves[i][0]) for i in range(n)]
        for cp in sends:
            cp.start()
        for i in range(n):
            copy(i, halves[i][1]).wait_recv()
        for cp in sends:
            cp.wait_send()

    return pl.pallas_call(
        body, name="join_halves", in_specs=[HBM] * n, out_specs=[HBM] * n,
        out_shape=[jax.ShapeDtypeStruct(v.shape, v.dtype) for v in vs],
        input_output_aliases={i: i for i in range(n)},
        scratch_shapes=[pltpu.SemaphoreType.DMA((n,)), pltpu.SemaphoreType.DMA((n,))],
    )(*vs)


SEM = pl.BlockSpec(memory_space=pltpu.SEMAPHORE)
IN_HBM = pl.BlockSpec(memory_space=pltpu.HBM)
SPLIT_EFFECT = pltpu.SideEffectType.DATAFLOW_SIDE_EFFECTING


def _split_copies(plan, refs, send_sems, recv_sems):
    pairs = []
    for idx, (src, dst, landing, to) in enumerate(plan(refs)):
        mk = lambda d: pltpu.make_async_remote_copy(src_ref=src, dst_ref=d, send_sem=send_sems.at[idx],
                                                    recv_sem=recv_sems.at[idx], device_id=to, device_id_type=MESH)
        pairs.append((mk(dst), mk(landing)))
    return pairs


def _split_start(name, bufs, after, plan, n_copies):
    n = len(bufs)

    def body(*refs):
        send_sems, recv_sems, token = refs[n + 1], refs[n + 2], refs[-1]
        for send, _ in _split_copies(plan, refs[:n], send_sems, recv_sems):
            send.start()
        token[...] = jnp.zeros_like(token)

    sems = pltpu.SemaphoreType.DMA((n_copies,))
    out = pl.pallas_call(
        body, name=name,
        in_specs=[IN_HBM] * n + [HBM],
        out_specs=[SEM, SEM] + [IN_HBM] * n + [pl.BlockSpec(memory_space=pltpu.VMEM)],
        out_shape=[sems, sems] + [pltpu.HBM(b.shape, b.dtype) for b in bufs] + [jax.ShapeDtypeStruct((8, LANES), F32)],
        input_output_aliases={i: 2 + i for i in range(n)},
        compiler_params=pltpu.CompilerParams(has_side_effects=SPLIT_EFFECT),
    )(*[pltpu.with_memory_space_constraint(b, pltpu.HBM) for b in bufs], after)
    return out[0], out[1], out[2:2 + n], out[-1]


def _split_wait(name, bufs, send_sems, recv_sems, after, plan):
    n = len(bufs)

    def body(*refs):
        for send, recv in _split_copies(plan, refs[:n], refs[n], refs[n + 1]):
            send.wait_send()
            recv.wait_recv()

    return pl.pallas_call(
        body, name=name,
        in_specs=[IN_HBM] * n + [SEM, SEM, HBM],
        out_specs=[IN_HBM] * n,
        out_shape=[pltpu.HBM(b.shape, b.dtype) for b in bufs],
        input_output_aliases={i: i for i in range(n)},
        compiler_params=pltpu.CompilerParams(has_side_effects=SPLIT_EFFECT),
    )(*bufs, send_sems, recv_sems, after)


def _gather_plan(n):
    def plan(refs):
        x, y, c = _place()
        k = 2 * x + y
        return [(refs[i].at[k], refs[i].at[k], refs[i].at[2 * px + py], (px, py, c))
                for i in range(n) for px, py in _other_chips(x, y)]
    return plan


def _scatter_plan(n):
    def plan(refs):
        x, y, c = _place()
        return [(refs[i].at[2 * px + py], refs[n + i].at[j], refs[n + i].at[j], (px, py, c))
                for i in range(n) for j, (px, py) in enumerate(_other_chips(x, y))]
    return plan


def _swap_sibling(vs):
    n = len(vs)

    def body(*refs):
        ins, outs, send_sems, recv_sems = refs[:n], refs[n:2 * n], refs[2 * n], refs[2 * n + 1]
        x, y, c = _place()
        copies = [pltpu.make_async_remote_copy(src_ref=ins[i], dst_ref=outs[i], send_sem=send_sems.at[i],
                                               recv_sem=recv_sems.at[i], device_id=(x, y, 1 - c), device_id_type=MESH)
                  for i in range(n)]
        for cp in copies:
            cp.start()
        for cp in copies:
            cp.wait()

    return pl.pallas_call(
        body, name="swap_sibling", in_specs=[HBM] * n, out_specs=[HBM] * n,
        out_shape=[jax.ShapeDtypeStruct(v.shape, v.dtype) for v in vs],
        scratch_shapes=[pltpu.SemaphoreType.DMA((n,)), pltpu.SemaphoreType.DMA((n,))],
    )(*vs)


def _to_bf16(g, name):
    _, rows, cols = g.shape
    rb = _tile(rows, 256)

    def body(g_ref, o_ref):
        o_ref[...] = g_ref[...].astype(BF16)

    spec = pl.BlockSpec((1, rb, cols), lambda j, i: (j, i, 0))
    return pl.pallas_call(
        body, name=name, grid=(N_SHARD, rows // rb), in_specs=[spec], out_specs=spec,
        out_shape=jax.ShapeDtypeStruct(g.shape, BF16),
        compiler_params=_params(("parallel", "parallel")))(g)


def _add_landed(g, landed, k, name):
    _, rows, cols = g.shape
    rb = _tile(rows, 256)

    def body(k_ref, g_ref, l_ref, o_ref):
        o_ref[...] = g_ref[0] + l_ref[0].astype(F32) + l_ref[1].astype(F32) + l_ref[2].astype(F32)

    return pl.pallas_call(
        body, name=name,
        grid_spec=pltpu.PrefetchScalarGridSpec(
            num_scalar_prefetch=1, grid=(rows // rb,),
            in_specs=[pl.BlockSpec((1, rb, cols), lambda i, k_ref: (k_ref[0], i, 0)),
                      pl.BlockSpec((N_SHARD - 1, rb, cols), lambda i, k_ref: (0, i, 0))],
            out_specs=pl.BlockSpec((rb, cols), lambda i, k_ref: (i, 0))),
        out_shape=jax.ShapeDtypeStruct((rows, cols), F32),
        compiler_params=_params(("parallel",)))(k.reshape(1), g, landed)


def _allreduce_small(v):
    rows = v.shape[0]

    def body(v_ref, o_ref, buf_ref, send_sems, recv_sems):
        x, y, c = _place()
        me = 4 * x + 2 * y + c
        buf_ref[me] = v_ref[...]
        copies = []
        for rel in range(1, 8):
            peer = (x ^ (rel >> 2), y ^ ((rel >> 1) & 1), c ^ (rel & 1))
            cp = pltpu.make_async_remote_copy(src_ref=v_ref, dst_ref=buf_ref.at[me], send_sem=send_sems.at[rel - 1],
                                              recv_sem=recv_sems.at[rel - 1], device_id=peer, device_id_type=MESH)
            cp.start()
            copies.append(cp)
        for rel in range(1, 8):
            peer_id = me ^ rel
            pltpu.make_async_remote_copy(src_ref=v_ref, dst_ref=buf_ref.at[peer_id], send_sem=send_sems.at[rel - 1],
                                         recv_sem=recv_sems.at[rel - 1], device_id=(x, y, c),
                                         device_id_type=MESH).wait_recv()
        for cp in copies:
            cp.wait_send()
        acc = buf_ref[0]
        for i in range(1, 8):
            acc = acc + buf_ref[i]
        o_ref[...] = acc

    return pl.pallas_call(
        body, name="allreduce_small",
        in_specs=[pl.BlockSpec(memory_space=pltpu.VMEM)], out_specs=pl.BlockSpec(memory_space=pltpu.VMEM),
        out_shape=jax.ShapeDtypeStruct(v.shape, F32),
        scratch_shapes=[pltpu.VMEM((8, rows, LANES), F32), pltpu.SemaphoreType.DMA((7,)), pltpu.SemaphoreType.DMA((7,))],
    )(v)


def _add_pair(g, got, c, name):
    _, rh, cols = got.shape
    rb = _tile(rh, 256)
    nrb = rh // rb

    def body(c_ref, g_ref, got_ref, o_ref):
        o_ref[...] = (g_ref[...] + got_ref[...]).astype(BF16)

    spec = pl.BlockSpec((1, rb, cols), lambda j, i, c_ref: (j, i, 0))
    return pl.pallas_call(
        body, name=name,
        grid_spec=pltpu.PrefetchScalarGridSpec(
            num_scalar_prefetch=1, grid=(N_SHARD, nrb),
            in_specs=[pl.BlockSpec((1, rb, cols), lambda j, i, c_ref: (j, c_ref[0] * nrb + i, 0)), spec],
            out_specs=spec),
        out_shape=jax.ShapeDtypeStruct(got.shape, BF16),
        compiler_params=_params(("parallel", "parallel")))(c.reshape(1), g, got)


def _add_chips(g, got, landed, k, c, name):
    _, rh, cols = got.shape
    rb = _tile(rh, 256)
    nrb = rh // rb

    def body(kc_ref, g_ref, got_ref, l_ref, o_ref):
        own = g_ref[0] + got_ref[0]
        o_ref[...] = own + l_ref[0].astype(F32) + l_ref[1].astype(F32) + l_ref[2].astype(F32)

    half_c = lambda i, kc: (kc[1] * nrb + i, 0)
    return pl.pallas_call(
        body, name=name,
        grid_spec=pltpu.PrefetchScalarGridSpec(
            num_scalar_prefetch=1, grid=(nrb,),
            in_specs=[pl.BlockSpec((1, rb, cols), lambda i, kc: (kc[0],) + half_c(i, kc)),
                      pl.BlockSpec((1, rb, cols), lambda i, kc: (kc[0], i, 0)),
                      pl.BlockSpec((N_SHARD - 1, rb, cols), lambda i, kc: (0, i, 0))],
            out_specs=pl.BlockSpec((rb, cols), half_c)),
        out_shape=jax.ShapeDtypeStruct((2 * rh, cols), F32),
        compiler_params=_params(("parallel",)))(jnp.stack([k, c]), g, got, landed)


def _adamw(wv, gs, m, v, name):
    rows, cols = wv.shape
    rb = _tile(rows, 256)
    c1 = 1.0 - ADAM_B1 ** ADAM_STEP
    c2 = 1.0 - ADAM_B2 ** ADAM_STEP
    n_g = len(gs)

    def body(*refs):
        w_ref, g_refs, (m_ref, v_ref, go_ref, d_ref, nm_ref, nv_ref) = refs[0], refs[1:1 + n_g], refs[1 + n_g:]
        gv = g_refs[0][...]
        for ref in g_refs[1:]:
            gv = gv + ref[...]
        go_ref[...] = gv
        nm = ADAM_B1 * m_ref[...] + (1.0 - ADAM_B1) * gv
        nv = ADAM_B2 * v_ref[...] + (1.0 - ADAM_B2) * (gv * gv)
        nm_ref[...] = nm
        nv_ref[...] = nv
        d_ref[...] = -ADAM_LR * ((nm / c1) / (jnp.sqrt(nv / c2) + ADAM_EPS) + ADAM_WD * w_ref[...])

    spec = pl.BlockSpec((rb, cols), lambda i: (i, 0))
    return pl.pallas_call(
        body, name=name, grid=(rows // rb,), in_specs=[spec] * (3 + n_g), out_specs=[spec] * 4,
        out_shape=[jax.ShapeDtypeStruct((rows, cols), F32)] * 4,
        compiler_params=_params(("parallel",)))(wv, *gs, m, v)


def _pack(flats):
    cat = jnp.concatenate([f.reshape(-1) for f in flats])
    n = cat.shape[0]
    rows = -(-n // (8 * LANES)) * 8
    return jnp.pad(cat, (0, rows * LANES - n)).reshape(rows, LANES)


def _unpack(packed, shapes):
    flat = packed.reshape(-1)
    out, off = [], 0
    for shp in shapes:
        n = 1
        for dim in shp:
            n *= dim
        out.append(flat[off:off + n].reshape(shp))
        off += n
    return out


def kernel(x, p, norm_g, w_in, conv_w, conv_b, dt_bias, a_log, d_skip, gnorm_g, pool_mix_w, pool_mix_b, pool_scale, w_branch_a, w_branch_b, w_out, ple_norm_g, w_ple_gate, w_ple_up, final_g, loss_target, m_norm_g, m_w_in, m_conv_w, m_conv_b, m_dt_bias, m_a_log, m_d_skip, m_gnorm_g, m_pool_mix_w, m_pool_mix_b, m_pool_scale, m_w_branch_a, m_w_branch_b, m_w_out, m_ple_norm_g, m_w_ple_gate, m_w_ple_up, m_final_g, v_norm_g, v_w_in, v_conv_w, v_conv_b, v_dt_bias, v_a_log, v_d_skip, v_gnorm_g, v_pool_mix_w, v_pool_mix_b, v_pool_scale, v_w_branch_a, v_w_branch_b, v_w_out, v_ple_norm_g, v_w_ple_gate, v_w_ple_up, v_final_g):
    wts = dict(norm_g=norm_g, w_in=w_in, conv_w=conv_w, conv_b=conv_b, dt_bias=dt_bias, a_log=a_log, d_skip=d_skip,
               gnorm_g=gnorm_g, pool_mix_w=pool_mix_w, pool_mix_b=pool_mix_b, pool_scale=pool_scale,
               w_branch_a=w_branch_a, w_branch_b=w_branch_b, w_out=w_out, ple_norm_g=ple_norm_g,
               w_ple_gate=w_ple_gate, w_ple_up=w_ple_up, final_g=final_g)
    mom_m = dict(norm_g=m_norm_g, w_in=m_w_in, conv_w=m_conv_w, conv_b=m_conv_b, dt_bias=m_dt_bias, a_log=m_a_log,
                 d_skip=m_d_skip, gnorm_g=m_gnorm_g, pool_mix_w=m_pool_mix_w, pool_mix_b=m_pool_mix_b,
                 pool_scale=m_pool_scale, w_branch_a=m_w_branch_a, w_branch_b=m_w_branch_b, w_out=m_w_out,
                 ple_norm_g=m_ple_norm_g, w_ple_gate=m_w_ple_gate, w_ple_up=m_w_ple_up, final_g=m_final_g)
    mom_v = dict(norm_g=v_norm_g, w_in=v_w_in, conv_w=v_conv_w, conv_b=v_conv_b, dt_bias=v_dt_bias, a_log=v_a_log,
                 d_skip=v_d_skip, gnorm_g=v_gnorm_g, pool_mix_w=v_pool_mix_w, pool_mix_b=v_pool_mix_b,
                 pool_scale=v_pool_scale, w_branch_a=v_w_branch_a, w_branch_b=v_w_branch_b, w_out=v_w_out,
                 ple_norm_g=v_ple_norm_g, w_ple_gate=v_w_ple_gate, w_ple_up=v_w_ple_up, final_g=v_final_g)
    c = lax.axis_index("c")
    k = 2 * lax.axis_index("x") + lax.axis_index("y")
    flat2 = lambda a: a.reshape(-1, a.shape[-1])

    slots = {n: _into_slot(flat2(wts[n]), k, BF16, "slot_" + n) for n in BIG}
    w_in_g, conv_g = _gather_weights([slots["w_in"]], [_into_slot(flat2(conv_w), k, F32, "slot_conv_w")])
    n_rest = len(REST)
    gsend, grecv, gbufs, gtoken = _split_start("gather_rest_start", [slots[n] for n in REST], conv_g,
                                               _gather_plan(n_rest), 3 * n_rest)

    def rest_weights(after):
        return dict(zip(REST, _split_wait("gather_rest_wait", gbufs, gsend, grecv, after, _gather_plan(n_rest))))

    flying = {}

    def early_grads(early):
        sends = [_to_bf16(early[n], "bf16_" + n) for n in REST]
        lands = [pltpu.with_memory_space_constraint(lax.empty((N_SHARD - 1,) + v.shape[1:], BF16), pltpu.HBM)
                 for v in sends]
        ssend, srecv, sbufs, stoken = _split_start("scatter_rest_start", sends + lands, early[REST[0]],
                                                   _scatter_plan(n_rest), 3 * n_rest)
        flying.update(send=ssend, recv=srecv, bufs=sbufs)
        return stoken

    def w_in_grad(g_w_in):
        got = _swap_halves([g_w_in])[0]
        pair = _add_pair(g_w_in, got, c, "add_pair_w_in")
        land = pltpu.with_memory_space_constraint(lax.empty((N_SHARD - 1,) + pair.shape[1:], BF16), pltpu.HBM)
        wsend, wrecv, wbufs, wtoken = _split_start("scatter_w_in_start", [pair, land], got, _scatter_plan(1), 3)
        flying.update(w_send=wsend, w_recv=wrecv, w_bufs=wbufs, w_got=got)
        return wtoken

    small = {n: wts[n] for n in SMALL}
    small["norm_g"] = norm_g + gtoken[0, 0]
    loss, grad_x, grads = _local_step(x, p[0], loss_target, dict(w_in=w_in_g, conv_w=conv_g), small,
                                      rest_weights, early_grads, w_in_grad)
    g_w_in = grads["w_in"]
    landed = _split_wait("scatter_w_in_wait", flying["w_bufs"], flying["w_send"], flying["w_recv"], grad_x,
                         _scatter_plan(1))[1]
    w_in_sum = _join_halves([_add_chips(g_w_in, flying["w_got"], landed, k, c, "add_chips_w_in")])[0]

    sbufs = _split_wait("scatter_rest_wait", flying["bufs"], flying["send"], flying["recv"], g_w_in,
                        _scatter_plan(n_rest))
    mine = [_add_landed(grads[n], ld, k, "add_landed_" + n) for n, ld in zip(REST, sbufs[n_rest:])]
    theirs = _swap_sibling(mine)
    g_sums = dict(zip(REST, zip(mine, theirs)))
    g_sums["w_in"] = (w_in_sum,)

    conv_shape = flat2(conv_w).shape
    small_sum = _allreduce_small(_pack([grads[n] for n in SMALL] + [grads["conv_w"], loss]))
    small_shapes = [wts[n].shape for n in SMALL] + [(N_SHARD,) + conv_shape, (1,)]
    small_g = _unpack(small_sum, small_shapes)
    g_conv = lax.dynamic_index_in_dim(small_g[-2], k, axis=0, keepdims=False)

    outs = {}
    for n in BIG:
        vals = _adamw(flat2(wts[n]), g_sums[n], flat2(mom_m[n]), flat2(mom_v[n]), "adamw_" + n)
        for kind, val in zip(("grad", "delta", "new_m", "new_v"), vals):
            outs[kind, n] = val.reshape(wts[n].shape)
    names = SMALL + ("conv_w",)
    sm = _adamw(_pack([wts[n] for n in names]), (_pack(small_g[:len(SMALL)] + [g_conv]),),
                _pack([mom_m[n] for n in names]), _pack([mom_v[n] for n in names]), "adamw_small")
    sm_shapes = [wts[n].shape for n in names]
    for kind, val in zip(("grad", "delta", "new_m", "new_v"), sm):
        for n, piece in zip(names, _unpack(val, sm_shapes)):
            outs[kind, n] = piece
    return (small_g[-1][0], grad_x, *[outs[kind, n] for kind in ("grad", "delta", "new_m", "new_v") for n in WEIGHTS])
```

```python
import functools

import jax
import jax.numpy as jnp
from jax import lax
from jax.experimental import pallas as pl
from jax.experimental.pallas import tpu as pltpu

F32 = jnp.float32
BF16 = jnp.bfloat16
HIGHEST = lax.Precision.HIGHEST
MESH = pl.DeviceIdType.MESH

EPS = 1e-6
HEAD_DIM = 64
SSM_GROUPS = 4
D_STATE = 128
CONV_WIDTH = 4
CHUNK = 128
N_POOL = 4
LANES = 128
N_SHARD = 4

ADAM_LR = 0.001
ADAM_B1 = 0.9
ADAM_B2 = 0.999
ADAM_EPS = 1e-08
ADAM_WD = 0.01
ADAM_STEP = 10

BIG = ("w_in", "pool_mix_w", "w_branch_a", "w_branch_b", "w_out", "w_ple_gate", "w_ple_up")
REST = BIG[1:]
SMALL = ("norm_g", "conv_b", "dt_bias", "a_log", "d_skip", "gnorm_g", "pool_mix_b", "pool_scale",
         "ple_norm_g", "final_g")
WEIGHTS = ("norm_g", "w_in", "conv_w", "conv_b", "dt_bias", "a_log", "d_skip", "gnorm_g", "pool_mix_w",
           "pool_mix_b", "pool_scale", "w_branch_a", "w_branch_b", "w_out", "ple_norm_g", "w_ple_gate",
           "w_ple_up", "final_g")


def _params(sem=None, vmem_mb=56):
    kw = dict(vmem_limit_bytes=vmem_mb << 20)
    if sem is not None:
        kw["dimension_semantics"] = sem
    return pltpu.CompilerParams(**kw)


def _sigmoid(v):
    return 0.5 * jnp.tanh(0.5 * v) + 0.5


def _sigmoid_tail(v):
    return 1.0 / (1.0 + jnp.exp(-v))


def _softplus(v):
    return jnp.maximum(v, 0.0) + jnp.log1p(jnp.exp(-jnp.abs(v)))


def _bdot(a, b):
    return jnp.dot(a.astype(BF16), b.astype(BF16), preferred_element_type=F32)


def _bdot_nt(a, b):
    return lax.dot_general(a.astype(BF16), b.astype(BF16), (((1,), (1,)), ((), ())), preferred_element_type=F32)


def _bdot_tn(a, b):
    return lax.dot_general(a.astype(BF16), b.astype(BF16), (((0,), (0,)), ((), ())), preferred_element_type=F32)


def _col_block(col0, width):
    assert col0 % width == 0, (col0, width)
    return col0 // width


def _tile(n, cap, unit=8):
    if n <= cap:
        return n
    best = None
    for cand in range(unit, cap + 1, unit):
        if n % cand == 0:
            best = cand
    assert best is not None, (n, cap)
    return best


def _shift_down(v, j, row):
    return jnp.where(row >= j, pltpu.roll(v, j, 0), 0.0)


def _shift_up(v, j, row):
    n = v.shape[0]
    return jnp.where(row < n - j, pltpu.roll(v, n - j, 0), 0.0)


def _mm(a, w, name, tm=1024, tn=1024):
    t, k = a.shape
    n = w.shape[1]
    tm, tn = min(tm, t), min(tn, n)

    def body(a_ref, w_ref, o_ref):
        o_ref[...] = _bdot(a_ref[...], w_ref[...]).astype(BF16)

    return pl.pallas_call(
        body, name=name, grid=(t // tm, n // tn),
        in_specs=[pl.BlockSpec((tm, k), lambda i, j: (i, 0)), pl.BlockSpec((k, tn), lambda i, j: (0, j))],
        out_specs=pl.BlockSpec((tm, tn), lambda i, j: (i, j)),
        out_shape=jax.ShapeDtypeStruct((t, n), BF16),
        compiler_params=_params(("parallel", "parallel")))(a, w)


def _mm_nt(a, w, name, tm=1024, tk=1024):
    t, k = a.shape
    n = w.shape[0]
    tm, tk = min(tm, t), min(tk, k)

    def body(a_ref, w_ref, o_ref):
        kk = pl.program_id(1)
        part = _bdot_nt(a_ref[...], w_ref[...])

        @pl.when(kk == 0)
        def _():
            o_ref[...] = part

        @pl.when(kk > 0)
        def _():
            o_ref[...] += part

    return pl.pallas_call(
        body, name=name, grid=(t // tm, k // tk),
        in_specs=[pl.BlockSpec((tm, tk), lambda i, j: (i, j)), pl.BlockSpec((n, tk), lambda i, j: (0, j))],
        out_specs=pl.BlockSpec((tm, n), lambda i, j: (i, 0)),
        out_shape=jax.ShapeDtypeStruct((t, n), F32),
        compiler_params=_params(("parallel", "arbitrary")))(a, w)


def _mm_tn(a, b, name, tn=1024, tk=2048, col_blocks=False):
    t, m = a.shape
    n = b.shape[1]
    tn, tk = min(tn, n), min(tk, t)

    def body(a_ref, b_ref, o_ref):
        kk = pl.program_id(1)
        part = _bdot_tn(a_ref[...], b_ref[...])
        part = part[None] if col_blocks else part

        @pl.when(kk == 0)
        def _():
            o_ref[...] = part

        @pl.when(kk > 0)
        def _():
            o_ref[...] += part

    if col_blocks:
        out_spec = pl.BlockSpec((1, m, tn), lambda j, kk: (j, 0, 0))
        out_shape = jax.ShapeDtypeStruct((n // tn, m, tn), F32)
    else:
        out_spec = pl.BlockSpec((m, tn), lambda j, kk: (0, j))
        out_shape = jax.ShapeDtypeStruct((m, n), F32)
    return pl.pallas_call(
        body, name=name, grid=(n // tn, t // tk),
        in_specs=[pl.BlockSpec((tk, m), lambda j, kk: (kk, 0)), pl.BlockSpec((tk, tn), lambda j, kk: (kk, j))],
        out_specs=out_spec, out_shape=out_shape,
        compiler_params=_params(("parallel", "arbitrary")))(a, b)


def _w_in_pieces(d, di, dc, nh, shard_w):
    pgd = d // N_POOL
    o_dt, o_u = di + dc, di + dc + nh
    o_zp, o_ga, o_gb = o_u + d, o_u + 2 * d, o_u + 3 * d
    c_z, c_uz = 2 * d, 2 * d + di + dc
    runs = [(False, 0, o_ga, d), (False, d, o_gb, d), (False, c_z, 0, di + dc), (True, 0, o_dt, nh)]
    for g in range(N_POOL):
        runs.append((False, c_uz + 2 * g * pgd, o_u + g * pgd, pgd))
        runs.append((False, c_uz + (2 * g + 1) * pgd, o_zp + g * pgd, pgd))
    pieces = []
    for is_dt, dst, src, n in runs:
        while n > 0:
            k, off = divmod(src, shard_w)
            m = min(n, shard_w - off)
            pieces.append((is_dt, dst, k, off, m))
            dst, src, n = dst + m, src + m, n - m
    return pieces


def _regroup_w_in(w_sh, d, di, dc, nh, rb=256):
    _, rows, sw = w_sh.shape
    n_main = 4 * d + di + dc
    pieces = _w_in_pieces(d, di, dc, nh, sw)
    rb = min(rb, rows)

    def body(w_ref, main_ref, dt_ref):
        dt_ref[...] = jnp.zeros_like(dt_ref)
        for is_dt, dst, k, off, m in pieces:
            out = dt_ref if is_dt else main_ref
            out[:, dst:dst + m] = w_ref[k, :, off:off + m]

    return pl.pallas_call(
        body, name="regroup_w_in", grid=(rows // rb,),
        in_specs=[pl.BlockSpec((N_SHARD, rb, sw), lambda i: (0, i, 0))],
        out_specs=[pl.BlockSpec((rb, n_main), lambda i: (i, 0)), pl.BlockSpec((rb, LANES), lambda i: (i, 0))],
        out_shape=[jax.ShapeDtypeStruct((rows, n_main), w_sh.dtype), jax.ShapeDtypeStruct((rows, LANES), w_sh.dtype)],
        compiler_params=_params(("parallel",)))(w_sh)


def _ungroup_w_in(d_main, d_dt, d, di, dc, nh, rb=128):
    rows, n_main = d_main.shape
    sw = (n_main + nh) // N_SHARD
    pieces = _w_in_pieces(d, di, dc, nh, sw)
    rb = min(rb, rows)

    def body(main_ref, dt_ref, o_ref):
        for is_dt, dst, k, off, m in pieces:
            src = dt_ref if is_dt else main_ref
            o_ref[k, :, off:off + m] = src[:, dst:dst + m]

    return pl.pallas_call(
        body, name="ungroup_w_in", grid=(rows // rb,),
        in_specs=[pl.BlockSpec((rb, n_main), lambda i: (i, 0)), pl.BlockSpec((rb, LANES), lambda i: (i, 0))],
        out_specs=pl.BlockSpec((N_SHARD, rb, sw), lambda i: (0, i, 0)),
        out_shape=jax.ShapeDtypeStruct((N_SHARD, rows, sw), F32),
        compiler_params=_params(("parallel",)))(d_main, d_dt)


def _inproj(x2, norm_g, w_main, w_dt, tm=1024, tn=1024):
    t, d = x2.shape
    n = w_main.shape[1]
    tm, tn = min(tm, t), min(tn, n)

    def body(x_ref, g_ref, w_ref, wdt_ref, proj_ref, dt_ref, h_ref):
        @pl.when(pl.program_id(1) == 0)
        def _():
            xv = x_ref[...]
            r = lax.rsqrt(jnp.mean(xv * xv, axis=-1, keepdims=True) + EPS)
            h = (xv * r * g_ref[...]).astype(BF16)
            h_ref[...] = h
            dt_ref[...] = jnp.dot(h, wdt_ref[...].astype(BF16), preferred_element_type=F32)

        proj_ref[...] = jnp.dot(h_ref[...], w_ref[...].astype(BF16), preferred_element_type=F32)

    return pl.pallas_call(
        body, name="inproj", grid=(t // tm, n // tn),
        in_specs=[pl.BlockSpec((tm, d), lambda i, j: (i, 0)), pl.BlockSpec((1, d), lambda i, j: (0, 0)),
                  pl.BlockSpec((d, tn), lambda i, j: (0, j)), pl.BlockSpec((d, LANES), lambda i, j: (0, 0))],
        out_specs=[pl.BlockSpec((tm, tn), lambda i, j: (i, j)), pl.BlockSpec((tm, LANES), lambda i, j: (i, 0)),
                   pl.BlockSpec((tm, d), lambda i, j: (i, 0))],
        out_shape=[jax.ShapeDtypeStruct((t, n), F32), jax.ShapeDtypeStruct((t, LANES), F32),
                   jax.ShapeDtypeStruct((t, d), BF16)],
        compiler_params=_params(("parallel", "arbitrary")))(x2, norm_g, w_main, w_dt)


def _conv_w_spec(conv_w, cb, j_axis):
    sw = conv_w.shape[2]
    assert sw % cb == 0, (sw, cb)
    per = sw // cb
    return N_SHARD * per, pl.BlockSpec((1, CONV_WIDTH, cb), lambda *ij: (ij[j_axis] // per, 0, ij[j_axis] % per))


CONV_ROWS = 64
CONV_HALO = 8


def _conv_taps(x_ref, t0, rc):
    if t0 == 0:
        cur = x_ref[0:rc, :]
        row = lax.broadcasted_iota(jnp.int32, cur.shape, 0)
        return [cur] + [_shift_down(cur, j, row) for j in range(1, CONV_WIDTH)]
    ext = x_ref[t0 - CONV_HALO:t0 + rc, :]
    return [ext[CONV_HALO:]] + [pltpu.roll(ext, j, 0)[CONV_HALO:] for j in range(1, CONV_WIDTH)]


def _conv_weights(w_ref):
    return [w_ref[0, CONV_WIDTH - 1 - j:CONV_WIDTH - j, :] for j in range(CONV_WIDTH)]


def _conv_pre(taps, wts, bias):
    acc = bias + taps[0] * wts[0]
    for j in range(1, CONV_WIDTH):
        acc = acc + taps[j] * wts[j]
    return acc


def _conv_fwd(proj, conv_w, conv_b, nb, s, col0, cb=256):
    n_blk, w_spec = _conv_w_spec(conv_w, cb, 1)
    blk0 = _col_block(col0, cb)
    rc = min(CONV_ROWS, s)

    def body(x_ref, w_ref, b_ref, o_ref):
        wts, bias = _conv_weights(w_ref), b_ref[...]
        for t0 in range(0, s, rc):
            acc = _conv_pre(_conv_taps(x_ref, t0, rc), wts, bias)
            o_ref[t0:t0 + rc, :] = acc * _sigmoid(acc)

    return pl.pallas_call(
        body, name="conv_fwd", grid=(nb, n_blk),
        in_specs=[pl.BlockSpec((s, cb), lambda b, j: (b, blk0 + j)), w_spec, pl.BlockSpec((1, cb), lambda b, j: (0, j))],
        out_specs=pl.BlockSpec((s, cb), lambda b, j: (b, j)),
        out_shape=jax.ShapeDtypeStruct((nb * s, n_blk * cb), F32),
        compiler_params=_params(("parallel", "parallel")))(proj, conv_w, conv_b)


def _ssd_consts(di):
    r = lax.broadcasted_iota(jnp.int32, (CHUNK, CHUNK), 0)
    c = lax.broadcasted_iota(jnp.int32, (CHUNK, CHUNK), 1)
    tril = (r >= c).astype(F32)
    head = lax.broadcasted_iota(jnp.int32, (LANES, di), 0)
    chan = lax.broadcasted_iota(jnp.int32, (LANES, di), 1) // HEAD_DIM
    expand = (head == chan).astype(BF16)
    return tril, expand, expand.T


def _expand(v, e, terms=3):
    acc = None
    for _ in range(terms):
        vb = v.astype(BF16)
        part = jnp.dot(vb, e, preferred_element_type=F32)
        acc = part if acc is None else acc + part
        v = v - vb.astype(F32)
    return acc


def _head_sum(t, et, terms=2):
    acc = None
    for _ in range(terms):
        tb = t.astype(BF16)
        part = jnp.dot(tb, et, preferred_element_type=F32)
        acc = part if acc is None else acc + part
        t = t - tb.astype(F32)
    return acc


def _ssd_scalars(dtr_ref, dtb_ref, alog_ref, tri):
    dtpre = dtr_ref[...] + dtb_ref[...]
    dt = _softplus(dtpre)
    a_neg = -jnp.exp(alog_ref[...])
    a_dt = dt * a_neg
    a_cs = jnp.dot(tri, a_dt, precision=HIGHEST, preferred_element_type=F32)
    a_cst = lax.dot_general(a_dt, tri, (((0,), (1,)), ((), ())), precision=HIGHEST, preferred_element_type=F32)
    return dtpre, dt, a_neg, a_cs, a_cst


def _ssd_fwd(proj, xbc, dtraw, dtb, alog, dskx, gn, nb, s, di, z_col0):
    t = nb * s
    nc = s // CHUNK
    hpg = di // HEAD_DIM // SSM_GROUPS
    gw = di // SSM_GROUPS
    gn_w = SSM_GROUPS * D_STATE
    b_blk = _col_block(di, gn_w)
    z_blk = _col_block(z_col0, di)
    L, P, N = CHUNK, HEAD_DIM, D_STATE
    tril, expand, _ = _ssd_consts(di)

    def body(z_ref, x_ref, b_ref, c_ref, dtr_ref, dtb_ref, alog_ref, dskx_ref, gn_ref, tril_ref, e_ref,
             ypre_ref, yan_ref, hp_ref, st_ref, yd_ref, xdt_ref):
        @pl.when(pl.program_id(1) == 0)
        def _():
            st_ref[...] = jnp.zeros_like(st_ref)

        hp_ref[0] = st_ref[...]
        tri = tril_ref[...]
        _, dt, _, a_cs, a_cst = _ssd_scalars(dtr_ref, dtb_ref, alog_ref, tri)
        ev = e_ref[...]
        a_exp = _expand(a_cs, ev)
        xv = x_ref[...]
        xdt = xv * _expand(dt, ev, terms=2)
        xdt_ref[...] = xdt
        a_last = a_exp[L - 1:L, :]
        xe = xdt * jnp.exp(a_last - a_exp)
        ea = jnp.exp(a_exp)
        e_last = jnp.exp(a_last)
        lower = tri > 0.5
        for g in range(SSM_GROUPS):
            gs = slice(g * gw, (g + 1) * gw)
            bg = b_ref[:, g * N:(g + 1) * N].astype(BF16)
            cg = c_ref[:, g * N:(g + 1) * N].astype(BF16)
            gm = _bdot_nt(cg, bg)
            ht = st_ref[:, gs]
            ch = _bdot(cg, ht)
            for e in range(hpg):
                h = g * hpg + e
                hs = slice(h * P, (h + 1) * P)
                decay = jnp.where(lower, jnp.exp(a_cs[:, h:h + 1] - a_cst[h:h + 1, :]), 0.0)
                yd_ref[:, hs] = _bdot(gm * decay, xdt_ref[:, hs])
            st_ref[:, gs] = ht * e_last[:, gs] + _bdot_tn(bg, xe[:, gs])
            ypre = yd_ref[:, gs] + ea[:, gs] * ch + xv[:, gs] * dskx_ref[:, gs]
            ypre_ref[:, gs] = ypre
            zv = z_ref[:, gs]
            v = ypre * zv * _sigmoid(zv)
            r = lax.rsqrt(jnp.mean(v * v, axis=-1, keepdims=True) + EPS)
            yan_ref[:, gs] = (v * r * gn_ref[:, gs]).astype(BF16)

    row = lambda b, c: b * nc + c
    vec = lambda w: pl.BlockSpec((1, w), lambda b, c: (0, 0))
    return pl.pallas_call(
        body, name="ssd_fwd", grid=(nb, nc),
        in_specs=[pl.BlockSpec((L, di), lambda b, c: (row(b, c), z_blk)),
                  pl.BlockSpec((L, di), lambda b, c: (row(b, c), 0)),
                  pl.BlockSpec((L, gn_w), lambda b, c: (row(b, c), b_blk)),
                  pl.BlockSpec((L, gn_w), lambda b, c: (row(b, c), b_blk + 1)),
                  pl.BlockSpec((L, LANES), lambda b, c: (row(b, c), 0)),
                  vec(LANES), vec(LANES), vec(di), vec(di),
                  pl.BlockSpec((L, L), lambda b, c: (0, 0)),
                  pl.BlockSpec((LANES, di), lambda b, c: (0, 0))],
        out_specs=[pl.BlockSpec((L, di), lambda b, c: (row(b, c), 0)),
                   pl.BlockSpec((L, di), lambda b, c: (row(b, c), 0)),
                   pl.BlockSpec((1, N, di), lambda b, c: (row(b, c), 0, 0))],
        out_shape=[jax.ShapeDtypeStruct((t, di), F32), jax.ShapeDtypeStruct((t, di), BF16),
                   jax.ShapeDtypeStruct((nb * nc, N, di), F32)],
        scratch_shapes=[pltpu.VMEM((N, di), F32), pltpu.VMEM((L, di), F32), pltpu.VMEM((L, di), F32)],
        compiler_params=_params(("parallel", "arbitrary")))(
            proj, xbc, xbc, xbc, dtraw, dtb, alog, dskx, gn, tril, expand)


def _pool_sum(v, g, row, shift):
    s2 = v + shift(v, 1, row)
    s4 = s2 + shift(s2, 2, row)
    s8 = s4 + shift(s4, 4, row)
    s16 = s8 + shift(s8, 8, row)
    return jnp.where(g == 0, s2, jnp.where(g == 1, s4, jnp.where(g == 2, s8, s16)))


def _pool_count(g, row):
    return jnp.minimum(row + 1, jnp.left_shift(2, g)).astype(F32)


def _pool_fwd(proj, mix_w, mix_b, scale, nb, s, col0):
    pgd = mix_w.shape[-1]
    blk0 = _col_block(col0, 2 * pgd)

    def body(uz_ref, w_ref, b_ref, sc_ref, o_ref):
        g = pl.program_id(1)
        u = uz_ref[:, :pgd]
        zp = uz_ref[:, pgd:]
        row = lax.broadcasted_iota(jnp.int32, u.shape, 0)
        pooled = _pool_sum(u, g, row, _shift_down) / _pool_count(g, row) - u
        mixed = _bdot(pooled, w_ref[:, 0].reshape(pgd, pgd)) + b_ref[...]
        o_ref[...] = (mixed * sc_ref[...] * zp * _sigmoid(zp)).astype(BF16)

    return pl.pallas_call(
        body, name="pool_fwd", grid=(nb, N_POOL),
        in_specs=[pl.BlockSpec((s, 2 * pgd), lambda b, g: (b, blk0 + g)),
                  pl.BlockSpec((N_SHARD, 1, pgd // N_SHARD, pgd), lambda b, g: (0, g, 0, 0)),
                  pl.BlockSpec((1, pgd), lambda b, g: (0, g)), pl.BlockSpec((1, pgd), lambda b, g: (0, g))],
        out_specs=pl.BlockSpec((s, pgd), lambda b, g: (b, g)),
        out_shape=jax.ShapeDtypeStruct((nb * s, N_POOL * pgd), BF16),
        compiler_params=_params(("parallel", "parallel")))(proj, mix_w, mix_b, scale)


def _mid_fwd(ya, yb, proj, col0, x2, p2, tgt, wo, wpg, wup, ple_g, final_g, tm=256):
    t, d = ya.shape
    tm = min(tm, t)
    blk = _col_block(col0, 2 * d)
    n_up, pdim, up_w = wup.shape

    def body(ya_ref, yb_ref, g_ref, x_ref, p_ref, tgt_ref, wo_ref, wpg_ref, wup_ref, pg_ref, g_fin_ref,
             merged_ref, hn_ref, dpre_ref, dpu_ref, x1_ref, dx2_ref, loss_ref, dg_ref):
        @pl.when(pl.program_id(0) == 0)
        def _():
            loss_ref[...] = jnp.zeros_like(loss_ref)
            dg_ref[...] = jnp.zeros_like(dg_ref)

        merged = (_sigmoid(g_ref[:, :d]) * ya_ref[...] + _sigmoid(g_ref[:, d:]) * yb_ref[...]).astype(BF16)
        merged_ref[...] = merged
        x1 = x_ref[...] + jnp.dot(merged, wo_ref[...], preferred_element_type=F32)
        x1_ref[...] = x1
        r1 = lax.rsqrt(jnp.mean(x1 * x1, axis=-1, keepdims=True) + EPS)
        hn = (x1 * r1 * pg_ref[...]).astype(BF16)
        hn_ref[...] = hn
        gate = _sigmoid(jnp.dot(hn, wpg_ref[...], preferred_element_type=F32))
        pb = p_ref[...].astype(BF16)
        pu = jnp.concatenate([jnp.dot(pb, wup_ref[j], preferred_element_type=F32) for j in range(n_up)], axis=1)
        x2 = x1 + gate * pu
        r = lax.rsqrt(jnp.mean(x2 * x2, axis=-1, keepdims=True) + EPS)
        xn = x2 * r
        fg = g_fin_ref[...]
        err = xn * fg - tgt_ref[...]
        loss_ref[...] += 0.5 * jnp.sum(jnp.mean(err * err, axis=-1, keepdims=True))
        dy = err * (1.0 / d)
        dg_ref[...] += jnp.sum(dy * xn, axis=0, keepdims=True)
        dxn = dy * fg
        dx2 = r * (dxn - xn * jnp.mean(dxn * xn, axis=-1, keepdims=True))
        dx2_ref[...] = dx2
        dpre_ref[...] = (dx2 * pu * gate * (1.0 - gate)).astype(BF16)
        dpu_ref[...] = (dx2 * gate).astype(BF16)

    row = pl.BlockSpec((tm, d), lambda i: (i, 0))
    vec = pl.BlockSpec((1, d), lambda i: (0, 0))
    whole = lambda a: pl.BlockSpec(a.shape, lambda i: (0,) * a.ndim)
    return pl.pallas_call(
        body, name="mid_fwd", grid=(t // tm,),
        in_specs=[row, row, pl.BlockSpec((tm, 2 * d), lambda i: (i, blk)), row,
                  pl.BlockSpec((tm, pdim), lambda i: (i, 0)), row, whole(wo), whole(wpg), whole(wup), vec, vec],
        out_specs=[row] * 6 + [pl.BlockSpec((1, LANES), lambda i: (0, 0)), vec],
        out_shape=[jax.ShapeDtypeStruct((t, d), BF16)] * 4 + [jax.ShapeDtypeStruct((t, d), F32)] * 2 + [
            jax.ShapeDtypeStruct((1, LANES), F32), jax.ShapeDtypeStruct((1, d), F32)],
        compiler_params=_params(("arbitrary",)))(ya, yb, proj, x2, p2, tgt, wo, wpg, wup, ple_g, final_g)


def _rms_grad(xv, dh, g):
    r = lax.rsqrt(jnp.mean(xv * xv, axis=-1, keepdims=True) + EPS)
    xn = xv * r
    dd = dh * g
    return r * (dd - xn * jnp.mean(dd * xn, axis=-1, keepdims=True)), jnp.sum(dh * xn, axis=0, keepdims=True)


def _mid_bwd(dpre, dx2, x1, ya, yb, proj, col0, wpg, wo, ple_g, n_cols, tm=256):
    t, d = ya.shape
    tm = min(tm, t)
    blk = _col_block(col0, 2 * d)

    def body(dpre_ref, dx2_ref, x1_ref, ya_ref, yb_ref, g_ref, wpg_ref, wo_ref, pg_ref,
             dx1_ref, dya_ref, dyb_ref, dg_ref, dpg_ref):
        @pl.when(pl.program_id(0) == 0)
        def _():
            dpg_ref[...] = jnp.zeros_like(dpg_ref)

        dhn = _bdot_nt(dpre_ref[...], wpg_ref[...])
        dx, dpg = _rms_grad(x1_ref[...], dhn, pg_ref[...])
        dpg_ref[...] += dpg
        dx1 = dx2_ref[...] + dx
        dx1_ref[...] = dx1
        dm_v = _bdot_nt(dx1, wo_ref[...])
        sa = _sigmoid(g_ref[:, :d])
        sb = _sigmoid(g_ref[:, d:])
        dya_ref[...] = (dm_v * sa).astype(BF16)
        dyb_ref[...] = (dm_v * sb).astype(BF16)
        dg_ref[:, :d] = (dm_v * ya_ref[...] * sa * (1.0 - sa)).astype(BF16)
        dg_ref[:, d:] = (dm_v * yb_ref[...] * sb * (1.0 - sb)).astype(BF16)

    row = pl.BlockSpec((tm, d), lambda i: (i, 0))
    vec = pl.BlockSpec((1, d), lambda i: (0, 0))
    gspec = pl.BlockSpec((tm, 2 * d), lambda i: (i, blk))
    whole = lambda a: pl.BlockSpec(a.shape, lambda i: (0,) * a.ndim)
    return pl.pallas_call(
        body, name="mid_bwd", grid=(t // tm,),
        in_specs=[row, row, row, row, row, gspec, whole(wpg), whole(wo), vec],
        out_specs=[row, row, row, gspec, vec],
        out_shape=[jax.ShapeDtypeStruct((t, d), F32), jax.ShapeDtypeStruct((t, d), BF16),
                   jax.ShapeDtypeStruct((t, d), BF16), jax.ShapeDtypeStruct((t, n_cols), BF16),
                   jax.ShapeDtypeStruct((1, d), F32)],
        compiler_params=_params(("arbitrary",)))(dpre, dx2, x1, ya, yb, proj, wpg, wo, ple_g)


def _in_bwd(dproj, w_main, ddt, w_dt, x2, dx1, norm_g, after, tm=1024, tk=1024):
    t, k = dproj.shape
    d = x2.shape[1]
    tm, tk = min(tm, t), min(tk, k)
    nk = k // tk

    def body(a_ref, w_ref, ddt_ref, wdt_ref, x_ref, dres_ref, g_ref, _, gx_ref, dg_ref, acc_ref):
        kk = pl.program_id(1)

        @pl.when((pl.program_id(0) == 0) & (kk == 0))
        def _():
            dg_ref[...] = jnp.zeros_like(dg_ref)

        part = _bdot_nt(a_ref[...], w_ref[...])

        @pl.when(kk == 0)
        def _():
            acc_ref[...] = part

        @pl.when(kk > 0)
        def _():
            acc_ref[...] += part

        @pl.when(kk == nk - 1)
        def _():
            dh = acc_ref[...] + _bdot_nt(ddt_ref[...], wdt_ref[...])
            dx, dg = _rms_grad(x_ref[...], dh, g_ref[...])
            dg_ref[...] += dg
            gx_ref[...] = dres_ref[...] + dx

    row = pl.BlockSpec((tm, d), lambda i, j: (i, 0))
    vec = pl.BlockSpec((1, d), lambda i, j: (0, 0))
    return pl.pallas_call(
        body, name="in_bwd", grid=(t // tm, nk),
        in_specs=[pl.BlockSpec((tm, tk), lambda i, j: (i, j)), pl.BlockSpec((d, tk), lambda i, j: (0, j)),
                  pl.BlockSpec((tm, LANES), lambda i, j: (i, 0)), pl.BlockSpec((d, LANES), lambda i, j: (0, 0)),
                  row, row, vec, pl.BlockSpec((8, LANES), lambda i, j: (0, 0))],
        out_specs=[row, vec],
        out_shape=[jax.ShapeDtypeStruct((t, d), F32), jax.ShapeDtypeStruct((1, d), F32)],
        scratch_shapes=[pltpu.VMEM((tm, d), F32)],
        compiler_params=_params(("arbitrary", "arbitrary")))(dproj, w_main, ddt, w_dt, x2, dx1, norm_g, after)


def _pool_bwd(proj, dyb, dproj, mix_w, mix_b, scale, nb, s, col0):
    pgd = mix_w.shape[-1]
    blk0 = _col_block(col0, 2 * pgd)

    def body(uz_ref, dy_ref, _, w_ref, b_ref, sc_ref, duz_ref, dw_ref, db_ref, dsc_ref):
        g = pl.program_id(0)

        @pl.when(pl.program_id(1) == 0)
        def _():
            dw_ref[...] = jnp.zeros_like(dw_ref)
            db_ref[...] = jnp.zeros_like(db_ref)
            dsc_ref[...] = jnp.zeros_like(dsc_ref)

        u = uz_ref[:, :pgd]
        zp = uz_ref[:, pgd:]
        row = lax.broadcasted_iota(jnp.int32, u.shape, 0)
        cnt = _pool_count(g, row)
        pooled = _pool_sum(u, g, row, _shift_down) / cnt - u
        wv = w_ref[:, 0].reshape(pgd, pgd)
        mixed = _bdot(pooled, wv) + b_ref[...]
        sg = _sigmoid(zp)
        sz = zp * sg
        dy = dy_ref[...]
        sc = sc_ref[...]
        dsc_ref[...] += jnp.sum(dy * mixed * sz, axis=0, keepdims=True)
        dmixed = dy * sc * sz
        db_ref[...] += jnp.sum(dmixed, axis=0, keepdims=True)
        dw_ref[:, 0] += _bdot_tn(pooled, dmixed).reshape(N_SHARD, pgd // N_SHARD, pgd)
        dpooled = _bdot_nt(dmixed, wv)
        duz_ref[:, :pgd] = (_pool_sum(dpooled / cnt, g, row, _shift_up) - dpooled).astype(BF16)
        duz_ref[:, pgd:] = (dy * mixed * sc * sg * (1.0 + zp * (1.0 - sg))).astype(BF16)

    uz = pl.BlockSpec((s, 2 * pgd), lambda g, b: (b, blk0 + g))
    vec = pl.BlockSpec((1, pgd), lambda g, b: (0, g))
    wspec = pl.BlockSpec((N_SHARD, 1, pgd // N_SHARD, pgd), lambda g, b: (0, g, 0, 0))
    return pl.pallas_call(
        body, name="pool_bwd", grid=(N_POOL, nb),
        in_specs=[uz, pl.BlockSpec((s, pgd), lambda g, b: (b, g)), pl.BlockSpec(memory_space=pl.ANY), wspec, vec, vec],
        out_specs=[uz, wspec, vec, vec],
        out_shape=[jax.ShapeDtypeStruct(dproj.shape, dproj.dtype), jax.ShapeDtypeStruct(mix_w.shape, F32),
                   jax.ShapeDtypeStruct(mix_b.shape, F32), jax.ShapeDtypeStruct(scale.shape, F32)],
        input_output_aliases={2: 0},
        compiler_params=_params(("parallel", "arbitrary")))(proj, dyb, dproj, mix_w, mix_b, scale)


def _ssd_bwd(dyan, ypre, proj, xbc, dtraw, hp, dproj, dtb, alog, dskx, gn, nb, s, di, z_col0):
    t = nb * s
    nc = s // CHUNK
    hpg = di // HEAD_DIM // SSM_GROUPS
    gw = di // SSM_GROUPS
    gn_w = SSM_GROUPS * D_STATE
    dc = di + 2 * gn_w
    b_blk = _col_block(di, gn_w)
    z_blk = _col_block(z_col0, di)
    L, P, N = CHUNK, HEAD_DIM, D_STATE
    tril, expand, expand_t = _ssd_consts(di)

    def body(dy_ref, ypre_ref, z_ref, x_ref, b_ref, c_ref, dtr_ref, hp_ref, _, dtb_ref, alog_ref, dskx_ref, gn_ref,
             tril_ref, e_ref, et_ref, dz_ref, ddt_ref, dxbc_ref, dgn_ref, ddsk_ref, dalog_ref, ddtb_ref,
             dst_ref, dyp_ref, xdt_ref, dxm_ref, t1_ref, t3_ref, aux_ref):
        @pl.when((pl.program_id(0) == 0) & (pl.program_id(1) == 0))
        def _():
            dgn_ref[...] = jnp.zeros_like(dgn_ref)
            ddsk_ref[...] = jnp.zeros_like(ddsk_ref)
            dalog_ref[...] = jnp.zeros_like(dalog_ref)
            ddtb_ref[...] = jnp.zeros_like(ddtb_ref)

        @pl.when(pl.program_id(1) == 0)
        def _():
            dst_ref[...] = jnp.zeros_like(dst_ref)

        tri = tril_ref[...]
        dtpre, dt, a_neg, a_cs, a_cst = _ssd_scalars(dtr_ref, dtb_ref, alog_ref, tri)
        ev = e_ref[...]
        a_exp = _expand(a_cs, ev)
        dt_exp = _expand(dt, ev, terms=2)
        xv = x_ref[...]
        xdt = xv * dt_exp
        xdt_ref[...] = xdt
        a_last = a_exp[L - 1:L, :]
        dte = jnp.exp(a_last - a_exp)
        xe = xdt * dte
        ea = jnp.exp(a_exp)
        e_last = jnp.exp(a_last)
        lower = tri > 0.5
        aux_ref[...] = jnp.zeros_like(aux_ref)
        for g in range(SSM_GROUPS):
            gs = slice(g * gw, (g + 1) * gw)
            zv = z_ref[:, gs]
            yp = ypre_ref[:, gs]
            sg = _sigmoid(zv)
            sz = zv * sg
            vg = yp * sz
            r = lax.rsqrt(jnp.mean(vg * vg, axis=-1, keepdims=True) + EPS)
            vn = vg * r
            dyg = dy_ref[:, gs]
            dgn_ref[:, gs] += jnp.sum(dyg * vn, axis=0, keepdims=True)
            dvn = dyg * gn_ref[:, gs]
            dv = r * (dvn - vn * jnp.mean(dvn * vn, axis=-1, keepdims=True))
            dy = dv * sz
            dyp_ref[:, gs] = dy
            dz_ref[:, gs] = (dv * yp * sg * (1.0 + zv * (1.0 - sg))).astype(BF16)
            bg = b_ref[:, g * N:(g + 1) * N].astype(BF16)
            cg = c_ref[:, g * N:(g + 1) * N].astype(BF16)
            gm = _bdot_nt(cg, bg)
            ht = hp_ref[0, :, gs]
            dht = dst_ref[:, gs]
            bds = _bdot(bg, dht)
            dye = dy * ea[:, gs]
            xe_g = xe[:, gs]
            dcg = _bdot_nt(dye, ht)
            dbg = _bdot_nt(xe_g, dht)
            dst_ref[:, gs] = e_last[:, gs] * dht + _bdot_tn(cg, dye)
            dgm = jnp.zeros((L, L), F32)
            for e in range(hpg):
                h = g * hpg + e
                hs = slice(h * P, (h + 1) * P)
                decay = jnp.where(lower, jnp.exp(a_cs[:, h:h + 1] - a_cst[h:h + 1, :]), 0.0)
                dy_h = dyp_ref[:, hs]
                dgm = dgm + _bdot_nt(dy_h, xdt_ref[:, hs]) * decay
                dxm_ref[:, hs] = _bdot_tn(gm * decay, dy_h)
            dxbc_ref[:, di + g * N:di + (g + 1) * N] = dbg + _bdot_tn(dgm, cg)
            dxbc_ref[:, di + gn_w + g * N:di + gn_w + (g + 1) * N] = dcg + _bdot(dgm, bg)
            dxm = dxm_ref[:, gs]
            x_g = xv[:, gs]
            dskx = dskx_ref[:, gs]
            xeb = xe_g * bds
            dxdt = dxm + dte[:, gs] * bds
            dxbc_ref[:, gs] = dxdt * dt_exp[:, gs] + dy * dskx
            each = ea[:, gs] * _bdot(cg, ht)
            y_diag = yp - x_g * dskx - each
            rnd = lambda v: v.astype(BF16).astype(F32)
            t1_ref[:, gs] = rnd(dy) * y_diag + dy * each - rnd(xdt[:, gs]) * dxm - xeb
            t3_ref[:, gs] = dxdt * x_g
            aux_ref[0:1, gs] = jnp.sum(dht * ht, axis=0, keepdims=True)
            aux_ref[1:2, gs] = jnp.sum(dy * x_g, axis=0, keepdims=True)
            aux_ref[2:3, gs] = jnp.sum(xeb, axis=0, keepdims=True)
        etv = et_ref[...]
        aux = _head_sum(aux_ref[...], etv)
        rowi = lax.broadcasted_iota(jnp.int32, (L, LANES), 0)
        end = aux[2:3, :] + aux[0:1, :] * jnp.exp(a_cs[L - 1:L, :])
        da = _head_sum(t1_ref[...], etv, terms=3) + jnp.where(rowi == L - 1, end, 0.0)
        rc = lax.dot_general(tri, da, (((0,), (0,)), ((), ())), precision=HIGHEST, preferred_element_type=F32)
        ddt = a_neg * rc + _head_sum(t3_ref[...], etv, terms=1)
        ddtraw = ddt * _sigmoid_tail(dtpre)
        ddt_ref[...] = ddtraw.astype(BF16)
        ddtb_ref[...] += jnp.sum(ddtraw, axis=0, keepdims=True)
        dalog_ref[...] += jnp.sum(dt * rc, axis=0, keepdims=True) * a_neg
        ddsk_ref[...] += aux[1:2, :]

    row = lambda b, c: b * nc + (nc - 1 - c)
    full = lambda w: pl.BlockSpec((L, w), lambda b, c: (row(b, c), 0))
    zspec = pl.BlockSpec((L, di), lambda b, c: (row(b, c), z_blk))
    vec = lambda w: pl.BlockSpec((1, w), lambda b, c: (0, 0))
    slab = lambda shape: pltpu.VMEM(shape, F32)
    return pl.pallas_call(
        body, name="ssd_bwd", grid=(nb, nc),
        in_specs=[full(di), full(di), zspec, full(di),
                  pl.BlockSpec((L, gn_w), lambda b, c: (row(b, c), b_blk)),
                  pl.BlockSpec((L, gn_w), lambda b, c: (row(b, c), b_blk + 1)),
                  full(LANES),
                  pl.BlockSpec((1, N, di), lambda b, c: (row(b, c), 0, 0)),
                  pl.BlockSpec(memory_space=pl.ANY),
                  vec(LANES), vec(LANES), vec(di), vec(di),
                  pl.BlockSpec((L, L), lambda b, c: (0, 0)),
                  pl.BlockSpec((LANES, di), lambda b, c: (0, 0)),
                  pl.BlockSpec((di, LANES), lambda b, c: (0, 0))],
        out_specs=[zspec, full(LANES), full(dc), vec(di), vec(LANES), vec(LANES), vec(LANES)],
        out_shape=[jax.ShapeDtypeStruct(dproj.shape, dproj.dtype), jax.ShapeDtypeStruct((t, LANES), BF16),
                   jax.ShapeDtypeStruct((t, dc), F32), jax.ShapeDtypeStruct((1, di), F32),
                   jax.ShapeDtypeStruct((1, LANES), F32), jax.ShapeDtypeStruct((1, LANES), F32),
                   jax.ShapeDtypeStruct((1, LANES), F32)],
        scratch_shapes=[slab((N, di)), slab((L, di)), slab((L, di)), slab((L, di)), slab((L, di)), slab((L, di)),
                        slab((8, di))],
        input_output_aliases={8: 0},
        compiler_params=_params(("arbitrary", "arbitrary")))(
            dyan, ypre, proj, xbc, xbc, xbc, dtraw, hp, dproj, dtb, alog, dskx, gn, tril, expand, expand_t)


def _conv_bwd(proj, dxbc, dproj, conv_w, conv_b, nb, s, col0, cb=256):
    n_blk, w_spec = _conv_w_spec(conv_w, cb, 0)
    blk0 = _col_block(col0, cb)
    rc = min(CONV_ROWS, s)

    def body(x_ref, dy_ref, _, w_ref, b_ref, dx_ref, dw_ref, db_ref, dacc_ref):
        @pl.when(pl.program_id(1) == 0)
        def _():
            dw_ref[...] = jnp.zeros_like(dw_ref)
            db_ref[...] = jnp.zeros_like(db_ref)

        wts, bias = _conv_weights(w_ref), b_ref[...]
        fold = lambda v: v.reshape(rc // 8, 8, cb).sum(axis=0)
        db8 = jnp.zeros((8, cb), F32)
        dw8 = [jnp.zeros((8, cb), F32) for _ in range(CONV_WIDTH)]
        for t0 in range(0, s, rc):
            taps = _conv_taps(x_ref, t0, rc)
            acc = _conv_pre(taps, wts, bias)
            sg = _sigmoid(acc)
            dacc = dy_ref[t0:t0 + rc, :] * sg * (1.0 + acc * (1.0 - sg))
            dacc_ref[t0:t0 + rc, :] = dacc
            db8 = db8 + fold(dacc)
            dw8 = [dw8[j] + fold(dacc * taps[j]) for j in range(CONV_WIDTH)]
        db_ref[...] += jnp.sum(db8, axis=0, keepdims=True)
        for j in range(CONV_WIDTH):
            dw_ref[0, CONV_WIDTH - 1 - j:CONV_WIDTH - j, :] += jnp.sum(dw8[j], axis=0, keepdims=True)
        for t0 in range(0, s, rc):
            if t0 + rc < s:
                n = rc + CONV_HALO
                win = dacc_ref[t0:t0 + n, :]
                ups = [win[:rc]] + [pltpu.roll(win, n - j, 0)[:rc] for j in range(1, CONV_WIDTH)]
            else:
                cur = dacc_ref[t0:t0 + rc, :]
                row = lax.broadcasted_iota(jnp.int32, cur.shape, 0)
                ups = [cur] + [_shift_up(cur, j, row) for j in range(1, CONV_WIDTH)]
            dx = ups[0] * wts[0]
            for j in range(1, CONV_WIDTH):
                dx = dx + ups[j] * wts[j]
            dx_ref[t0:t0 + rc, :] = dx.astype(BF16)

    return pl.pallas_call(
        body, name="conv_bwd", grid=(n_blk, nb),
        in_specs=[pl.BlockSpec((s, cb), lambda j, b: (b, blk0 + j)), pl.BlockSpec((s, cb), lambda j, b: (b, j)),
                  pl.BlockSpec(memory_space=pl.ANY), w_spec, pl.BlockSpec((1, cb), lambda j, b: (0, j))],
        out_specs=[pl.BlockSpec((s, cb), lambda j, b: (b, blk0 + j)), w_spec, pl.BlockSpec((1, cb), lambda j, b: (0, j))],
        out_shape=[jax.ShapeDtypeStruct(dproj.shape, dproj.dtype), jax.ShapeDtypeStruct(conv_w.shape, F32),
                   jax.ShapeDtypeStruct(conv_b.shape, F32)],
        scratch_shapes=[pltpu.VMEM((s, cb), F32)],
        input_output_aliases={2: 0},
        compiler_params=_params(("parallel", "arbitrary")))(proj, dxbc, dproj, conv_w, conv_b)


def _local_step(x, p, tgt, wg, small, rest_weights, early_grads, w_in_grad):
    nb, s, d = x.shape
    t = nb * s
    gn_w = SSM_GROUPS * D_STATE
    dc = N_SHARD * wg["conv_w"].shape[2]
    di = dc - 2 * gn_w
    nh = di // HEAD_DIM
    pgd = d // N_POOL
    x2 = x.reshape(t, d)
    p2 = p.reshape(t, p.shape[-1])
    tgt2 = tgt.reshape(t, d)

    w_main, w_dt = _regroup_w_in(wg["w_in"], d, di, dc, nh)
    c_g, c_z, c_xbc, c_uz = 0, 2 * d, 2 * d + di, 2 * d + di + dc
    n_main = w_main.shape[1]

    pad_h = lambda v: jnp.pad(v.reshape(1, nh).astype(F32), ((0, 0), (0, LANES - nh)))
    dtb, alog = pad_h(small["dt_bias"]), pad_h(small["a_log"])
    dskx = jnp.repeat(small["d_skip"].reshape(1, nh).astype(F32), HEAD_DIM, axis=1)
    vec = lambda v: v.reshape(1, -1).astype(F32)
    norm_g, gn, conv_b = vec(small["norm_g"]), vec(small["gnorm_g"]), vec(small["conv_b"])
    mix_b, scale = vec(small["pool_mix_b"]), vec(small["pool_scale"])
    ple_g, final_g = vec(small["ple_norm_g"]), vec(small["final_g"])
    conv_w = wg["conv_w"]

    wide = _tile(n_main, 2304, LANES)
    proj, dtraw, h = _inproj(x2, norm_g, w_main, w_dt, tn=wide)
    xbc = _conv_fwd(proj, conv_w, conv_b, nb, s, c_xbc)
    ypre, yan, hp = _ssd_fwd(proj, xbc, dtraw, dtb, alog, dskx, gn, nb, s, di, c_z)
    wr = rest_weights(yan)
    mix_w = wr["pool_mix_w"].reshape(N_SHARD, N_POOL, pgd // N_SHARD, pgd)
    rows = lambda v: v.reshape(-1, v.shape[-1])
    wa, wb, wo, wpg = rows(wr["w_branch_a"]), rows(wr["w_branch_b"]), rows(wr["w_out"]), rows(wr["w_ple_gate"])
    wup = wr["w_ple_up"]
    ybp = _pool_fwd(proj, mix_w, mix_b, scale, nb, s, c_uz)
    ya = _mm(yan, wa, "branch_a")
    yb = _mm(ybp, wb, "branch_b")
    merged, hn, dpre, dpu, x1, dx2, loss, d_final_g = _mid_fwd(
        ya, yb, proj, c_g, x2, p2, tgt2, wo, wpg, wup, ple_g, final_g)

    d_wpg = _mm_tn(hn, dpre, "d_w_ple_gate")
    d_wup = _mm_tn(p2, dpu, "d_w_ple_up", tn=wup.shape[-1], col_blocks=True)
    dx1, dya, dyb, dproj, d_ple_g = _mid_bwd(dpre, dx2, x1, ya, yb, proj, c_g, wpg, wo, ple_g, n_main)
    d_wo = _mm_tn(merged, dx1, "d_w_out")
    d_wa = _mm_tn(yan, dya, "d_w_branch_a", tk=1024)
    d_wb = _mm_tn(ybp, dyb, "d_w_branch_b")
    dyan = _mm_nt(dya, wa, "d_y_a")
    dybp = _mm_nt(dyb, wb, "d_y_b")
    dproj, d_mix_w, d_mix_b, d_scale = _pool_bwd(proj, dybp, dproj, mix_w, mix_b, scale, nb, s, c_uz)
    shard_major = lambda v: v.reshape(N_SHARD, v.shape[0] // N_SHARD, v.shape[1])
    early = dict(pool_mix_w=d_mix_w.reshape(N_SHARD, pgd, pgd), w_branch_a=shard_major(d_wa),
                 w_branch_b=shard_major(d_wb), w_out=shard_major(d_wo), w_ple_gate=shard_major(d_wpg),
                 w_ple_up=d_wup)
    token = early_grads(early)
    dproj, ddt, dxbc, d_gn, d_dsk, d_alog, d_dtb = _ssd_bwd(
        dyan, ypre, proj, xbc, dtraw, hp, dproj, dtb + token[0:1, 0:1], alog, dskx, gn, nb, s, di, c_z)
    dproj, d_conv_w, d_conv_b = _conv_bwd(proj, dxbc, dproj, conv_w, conv_b, nb, s, c_xbc)
    d_wmain = _mm_tn(h, dproj, "d_w_in", tk=2048)
    d_wdt = _mm_tn(h, ddt, "d_w_dt")
    d_w_in = _ungroup_w_in(d_wmain, d_wdt, d, di, dc, nh)
    token = w_in_grad(d_w_in)
    gx, d_norm_g = _in_bwd(dproj, w_main, ddt, w_dt, x2, dx1, norm_g, token, tk=_tile(n_main, 1536, LANES))

    grads = dict(norm_g=d_norm_g, w_in=d_w_in, conv_w=d_conv_w, conv_b=d_conv_b, dt_bias=d_dtb[:, :nh],
                 a_log=d_alog[:, :nh], d_skip=d_dsk[:, :nh], gnorm_g=d_gn, pool_mix_b=d_mix_b, pool_scale=d_scale,
                 ple_norm_g=d_ple_g, final_g=d_final_g, **early)
    return loss[0, 0], gx.reshape(nb, s, d), grads


def _place():
    return lax.axis_index("x"), lax.axis_index("y"), lax.axis_index("c")


def _other_chips(x, y):
    return [(1 - x, y), (x, 1 - y), (1 - x, 1 - y)]


def _halves(c, rows, align):
    rh = rows // 2
    assert rows % 2 == 0 and rh % align == 0, (rows, align)
    return (pl.ds(pl.multiple_of(c * rh, align), rh), pl.ds(pl.multiple_of((1 - c) * rh, align), rh))


HBM = pl.BlockSpec(memory_space=pl.ANY)


def _into_slot(w2, k, dtype, name):
    rows, cols = w2.shape
    rb = _tile(rows, 256)

    def body(k_ref, w_ref, o_ref):
        o_ref[0] = w_ref[...].astype(dtype)

    return pl.pallas_call(
        body, name=name,
        grid_spec=pltpu.PrefetchScalarGridSpec(
            num_scalar_prefetch=1, grid=(rows // rb,),
            in_specs=[pl.BlockSpec((rb, cols), lambda i, k_ref: (i, 0))],
            out_specs=pl.BlockSpec((1, rb, cols), lambda i, k_ref: (k_ref[0], i, 0))),
        out_shape=jax.ShapeDtypeStruct((N_SHARD, rows, cols), dtype),
        compiler_params=_params(("parallel",)))(k.reshape(1), w2)


def _gather_weights(split, whole):
    n_split, n_all = len(split), len(split) + len(whole)

    def body(*refs):
        bufs = refs[n_all:2 * n_all]
        send_sems, recv_sems = refs[2 * n_all:]
        x, y, c = _place()
        k, k_x, k_y, k_d = 2 * x + y, 2 * (1 - x) + y, 2 * x + (1 - y), 2 * (1 - x) + (1 - y)
        x_nb, y_nb, sib = (1 - x, y, c), (x, 1 - y, c), (x, y, 1 - c)

        def copy(idx, block, to):
            return pltpu.make_async_remote_copy(src_ref=block, dst_ref=block, send_sem=send_sems.at[idx],
                                                recv_sem=recv_sems.at[idx], device_id=to, device_id_type=MESH)

        started = []

        def start(idx, block, to):
            started.append(copy(idx, block, to))
            started[-1].start()

        for i in range(n_split):
            buf, s0 = bufs[i], 8 * i
            rh = buf.shape[1] // 2
            rq = rh // 2
            assert buf.shape[1] == 4 * rq and rq % 16 == 0, buf.shape

            def rows(core, part):
                lo = core * rh + (rq if part == "bottom" else 0)
                return pl.ds(pl.multiple_of(lo, 16), rh if part == "all" else rq)

            start(s0 + 0, buf.at[k, rows(c, "all")], x_nb)
            start(s0 + 1, buf.at[k, rows(c, "all")], y_nb)
            copy(s0 + 0, buf.at[k_x, rows(c, "all")], x_nb).wait_recv()
            start(s0 + 2, buf.at[k_x, rows(c, "top")], y_nb)
            start(s0 + 4, buf.at[k_x, rows(c, "all")], sib)
            copy(s0 + 1, buf.at[k_y, rows(c, "all")], y_nb).wait_recv()
            start(s0 + 3, buf.at[k_y, rows(c, "bottom")], x_nb)
            start(s0 + 5, buf.at[k_y, rows(c, "all")], sib)
            copy(s0 + 2, buf.at[k_d, rows(c, "top")], y_nb).wait_recv()
            start(s0 + 6, buf.at[k_d, rows(c, "top")], sib)
            copy(s0 + 3, buf.at[k_d, rows(c, "bottom")], x_nb).wait_recv()
            start(s0 + 7, buf.at[k_d, rows(c, "bottom")], sib)
            copy(s0 + 4, buf.at[k_x, rows(1 - c, "all")], sib).wait_recv()
            copy(s0 + 5, buf.at[k_y, rows(1 - c, "all")], sib).wait_recv()
            copy(s0 + 6, buf.at[k_d, rows(1 - c, "top")], sib).wait_recv()
            copy(s0 + 7, buf.at[k_d, rows(1 - c, "bottom")], sib).wait_recv()
        for i in range(n_split, n_all):
            s0 = 8 * n_split + 3 * (i - n_split)
            for j, (px, py) in enumerate(_other_chips(x, y)):
                start(s0 + j, bufs[i].at[k], (px, py, c))
            for j, (px, py) in enumerate(_other_chips(x, y)):
                copy(s0 + j, bufs[i].at[2 * px + py], (px, py, c)).wait_recv()
        for cp in started:
            cp.wait_send()

    arrays = list(split) + list(whole)
    n_sem = 8 * n_split + 3 * len(whole)
    return pl.pallas_call(
        body, name="gather_weights",
        in_specs=[HBM] * n_all, out_specs=[HBM] * n_all,
        out_shape=[jax.ShapeDtypeStruct(a.shape, a.dtype) for a in arrays],
        input_output_aliases={i: i for i in range(n_all)},
        scratch_shapes=[pltpu.SemaphoreType.DMA((n_sem,)), pltpu.SemaphoreType.DMA((n_sem,))],
    )(*arrays)


def _swap_halves(gs):
    n = len(gs)

    def body(*refs):
        ins, outs, send_sems, recv_sems = refs[:n], refs[n:2 * n], refs[2 * n], refs[2 * n + 1]
        x, y, c = _place()
        copies = []
        for i in range(n):
            _, theirs = _halves(c, gs[i].shape[1], 8)
            cp = pltpu.make_async_remote_copy(src_ref=ins[i].at[:, theirs], dst_ref=outs[i], send_sem=send_sems.at[i],
                                              recv_sem=recv_sems.at[i], device_id=(x, y, 1 - c), device_id_type=MESH)
            cp.start()
            copies.append(cp)
        for cp in copies:
            cp.wait()

    return pl.pallas_call(
        body, name="swap_halves", in_specs=[HBM] * n, out_specs=[HBM] * n,
        out_shape=[jax.ShapeDtypeStruct((g.shape[0], g.shape[1] // 2, g.shape[2]), g.dtype) for g in gs],
        scratch_shapes=[pltpu.SemaphoreType.DMA((n,)), pltpu.SemaphoreType.DMA((n,))],
    )(*gs)


def _join_halves(vs):
    n = len(vs)

    def body(*refs):
        bufs, send_sems, recv_sems = refs[n:2 * n], refs[2 * n], refs[2 * n + 1]
        x, y, c = _place()

        def copy(i, rows):
            return pltpu.make_async_remote_copy(src_ref=bufs[i].at[rows], dst_ref=bufs[i].at[rows],
                                                send_sem=send_sems.at[i], recv_sem=recv_sems.at[i],
                                                device_id=(x, y, 1 - c), device_id_type=MESH)

        halves = [_halves(c, bufs[i].shape[0], 8) for i in range(n)]
        sends = [copy(i, halves[i][0]) for i in range(n)]
        for cp in sends:
            cp.start()
        for i in range(n):
            copy(i, halves[i][1]).wait_recv()
        for cp in sends:
            cp.wait_send()

    return pl.pallas_call(
        body, name="join_halves", in_specs=[HBM] * n, out_specs=[HBM] * n,
        out_shape=[jax.ShapeDtypeStruct(v.shape, v.dtype) for v in vs],
        input_output_aliases={i: i for i in range(n)},
        scratch_shapes=[pltpu.SemaphoreType.DMA((n,)), pltpu.SemaphoreType.DMA((n,))],
    )(*vs)


SEM = pl.BlockSpec(memory_space=pltpu.SEMAPHORE)
IN_HBM = pl.BlockSpec(memory_space=pltpu.HBM)
SPLIT_EFFECT = pltpu.SideEffectType.DATAFLOW_SIDE_EFFECTING


def _split_copies(plan, refs, send_sems, recv_sems):
    pairs = []
    for idx, (src, dst, landing, to) in enumerate(plan(refs)):
        mk = lambda d: pltpu.make_async_remote_copy(src_ref=src, dst_ref=d, send_sem=send_sems.at[idx],
                                                    recv_sem=recv_sems.at[idx], device_id=to, device_id_type=MESH)
        pairs.append((mk(dst), mk(landing)))
    return pairs


def _split_start(name, bufs, after, plan, n_copies):
    n = len(bufs)

    def body(*refs):
        send_sems, recv_sems, token = refs[n + 1], refs[n + 2], refs[-1]
        for send, _ in _split_copies(plan, refs[:n], send_sems, recv_sems):
            send.start()
        token[...] = jnp.zeros_like(token)

    sems = pltpu.SemaphoreType.DMA((n_copies,))
    out = pl.pallas_call(
        body, name=name,
        in_specs=[IN_HBM] * n + [HBM],
        out_specs=[SEM, SEM] + [IN_HBM] * n + [pl.BlockSpec(memory_space=pltpu.VMEM)],
        out_shape=[sems, sems] + [pltpu.HBM(b.shape, b.dtype) for b in bufs] + [jax.ShapeDtypeStruct((8, LANES), F32)],
        input_output_aliases={i: 2 + i for i in range(n)},
        compiler_params=pltpu.CompilerParams(has_side_effects=SPLIT_EFFECT),
    )(*[pltpu.with_memory_space_constraint(b, pltpu.HBM) for b in bufs], after)
    return out[0], out[1], out[2:2 + n], out[-1]


def _split_wait(name, bufs, send_sems, recv_sems, after, plan):
    n = len(bufs)

    def body(*refs):
        for send, recv in _split_copies(plan, refs[:n], refs[n], refs[n + 1]):
            send.wait_send()
            recv.wait_recv()

    return pl.pallas_call(
        body, name=name,
        in_specs=[IN_HBM] * n + [SEM, SEM, HBM],
        out_specs=[IN_HBM] * n,
        out_shape=[pltpu.HBM(b.shape, b.dtype) for b in bufs],
        input_output_aliases={i: i for i in range(n)},
        compiler_params=pltpu.CompilerParams(has_side_effects=SPLIT_EFFECT),
    )(*bufs, send_sems, recv_sems, after)


def _gather_plan(n):
    def plan(refs):
        x, y, c = _place()
        k = 2 * x + y
        return [(refs[i].at[k], refs[i].at[k], refs[i].at[2 * px + py], (px, py, c))
                for i in range(n) for px, py in _other_chips(x, y)]
    return plan


def _scatter_plan(n):
    def plan(refs):
        x, y, c = _place()
        return [(refs[i].at[2 * px + py], refs[n + i].at[j], refs[n + i].at[j], (px, py, c))
                for i in range(n) for j, (px, py) in enumerate(_other_chips(x, y))]
    return plan


def _swap_sibling(vs):
    n = len(vs)

    def body(*refs):
        ins, outs, send_sems, recv_sems = refs[:n], refs[n:2 * n], refs[2 * n], refs[2 * n + 1]
        x, y, c = _place()
        copies = [pltpu.make_async_remote_copy(src_ref=ins[i], dst_ref=outs[i], send_sem=send_sems.at[i],
                                               recv_sem=recv_sems.at[i], device_id=(x, y, 1 - c), device_id_type=MESH)
                  for i in range(n)]
        for cp in copies:
            cp.start()
        for cp in copies:
            cp.wait()

    return pl.pallas_call(
        body, name="swap_sibling", in_specs=[HBM] * n, out_specs=[HBM] * n,
        out_shape=[jax.ShapeDtypeStruct(v.shape, v.dtype) for v in vs],
        scratch_shapes=[pltpu.SemaphoreType.DMA((n,)), pltpu.SemaphoreType.DMA((n,))],
    )(*vs)


def _to_bf16(g, name):
    _, rows, cols = g.shape
    rb = _tile(rows, 256)

    def body(g_ref, o_ref):
        o_ref[...] = g_ref[...].astype(BF16)

    spec = pl.BlockSpec((1, rb, cols), lambda j, i: (j, i, 0))
    return pl.pallas_call(
        body, name=name, grid=(N_SHARD, rows // rb), in_specs=[spec], out_specs=spec,
        out_shape=jax.ShapeDtypeStruct(g.shape, BF16),
        compiler_params=_params(("parallel", "parallel")))(g)


def _add_landed(g, landed, k, name):
    _, rows, cols = g.shape
    rb = _tile(rows, 256)

    def body(k_ref, g_ref, l_ref, o_ref):
        o_ref[...] = g_ref[0] + l_ref[0].astype(F32) + l_ref[1].astype(F32) + l_ref[2].astype(F32)

    return pl.pallas_call(
        body, name=name,
        grid_spec=pltpu.PrefetchScalarGridSpec(
            num_scalar_prefetch=1, grid=(rows // rb,),
            in_specs=[pl.BlockSpec((1, rb, cols), lambda i, k_ref: (k_ref[0], i, 0)),
                      pl.BlockSpec((N_SHARD - 1, rb, cols), lambda i, k_ref: (0, i, 0))],
            out_specs=pl.BlockSpec((rb, cols), lambda i, k_ref: (i, 0))),
        out_shape=jax.ShapeDtypeStruct((rows, cols), F32),
        compiler_params=_params(("parallel",)))(k.reshape(1), g, landed)


def _allreduce_small(v):
    rows = v.shape[0]

    def body(v_ref, o_ref, buf_ref, send_sems, recv_sems):
        x, y, c = _place()
        me = 4 * x + 2 * y + c
        buf_ref[me] = v_ref[...]
        copies = []
        for rel in range(1, 8):
            peer = (x ^ (rel >> 2), y ^ ((rel >> 1) & 1), c ^ (rel & 1))
            cp = pltpu.make_async_remote_copy(src_ref=v_ref, dst_ref=buf_ref.at[me], send_sem=send_sems.at[rel - 1],
                                              recv_sem=recv_sems.at[rel - 1], device_id=peer, device_id_type=MESH)
            cp.start()
            copies.append(cp)
        for rel in range(1, 8):
            peer_id = me ^ rel
            pltpu.make_async_remote_copy(src_ref=v_ref, dst_ref=buf_ref.at[peer_id], send_sem=send_sems.at[rel - 1],
                                         recv_sem=recv_sems.at[rel - 1], device_id=(x, y, c),
                                         device_id_type=MESH).wait_recv()
        for cp in copies:
            cp.wait_send()
        acc = buf_ref[0]
        for i in range(1, 8):
            acc = acc + buf_ref[i]
        o_ref[...] = acc

    return pl.pallas_call(
        body, name="allreduce_small",
        in_specs=[pl.BlockSpec(memory_space=pltpu.VMEM)], out_specs=pl.BlockSpec(memory_space=pltpu.VMEM),
        out_shape=jax.ShapeDtypeStruct(v.shape, F32),
        scratch_shapes=[pltpu.VMEM((8, rows, LANES), F32), pltpu.SemaphoreType.DMA((7,)), pltpu.SemaphoreType.DMA((7,))],
    )(v)


def _add_pair(g, got, c, name):
    _, rh, cols = got.shape
    rb = _tile(rh, 256)
    nrb = rh // rb

    def body(c_ref, g_ref, got_ref, o_ref):
        o_ref[...] = (g_ref[...] + got_ref[...]).astype(BF16)

    spec = pl.BlockSpec((1, rb, cols), lambda j, i, c_ref: (j, i, 0))
    return pl.pallas_call(
        body, name=name,
        grid_spec=pltpu.PrefetchScalarGridSpec(
            num_scalar_prefetch=1, grid=(N_SHARD, nrb),
            in_specs=[pl.BlockSpec((1, rb, cols), lambda j, i, c_ref: (j, c_ref[0] * nrb + i, 0)), spec],
            out_specs=spec),
        out_shape=jax.ShapeDtypeStruct(got.shape, BF16),
        compiler_params=_params(("parallel", "parallel")))(c.reshape(1), g, got)


def _add_chips(g, got, landed, k, c, name):
    _, rh, cols = got.shape
    rb = _tile(rh, 256)
    nrb = rh // rb

    def body(kc_ref, g_ref, got_ref, l_ref, o_ref):
        own = g_ref[0] + got_ref[0]
        o_ref[...] = own + l_ref[0].astype(F32) + l_ref[1].astype(F32) + l_ref[2].astype(F32)

    half_c = lambda i, kc: (kc[1] * nrb + i, 0)
    return pl.pallas_call(
        body, name=name,
        grid_spec=pltpu.PrefetchScalarGridSpec(
            num_scalar_prefetch=1, grid=(nrb,),
            in_specs=[pl.BlockSpec((1, rb, cols), lambda i, kc: (kc[0],) + half_c(i, kc)),
                      pl.BlockSpec((1, rb, cols), lambda i, kc: (kc[0], i, 0)),
                      pl.BlockSpec((N_SHARD - 1, rb, cols), lambda i, kc: (0, i, 0))],
            out_specs=pl.BlockSpec((rb, cols), half_c)),
        out_shape=jax.ShapeDtypeStruct((2 * rh, cols), F32),
        compiler_params=_params(("parallel",)))(jnp.stack([k, c]), g, got, landed)


def _adamw(wv, gs, m, v, name):
    rows, cols = wv.shape
    rb = _tile(rows, 256)
    c1 = 1.0 - ADAM_B1 ** ADAM_STEP
    c2 = 1.0 - ADAM_B2 ** ADAM_STEP
    n_g = len(gs)

    def body(*refs):
        w_ref, g_refs, (m_ref, v_ref, go_ref, d_ref, nm_ref, nv_ref) = refs[0], refs[1:1 + n_g], refs[1 + n_g:]
        gv = g_refs[0][...]
        for ref in g_refs[1:]:
            gv = gv + ref[...]
        go_ref[...] = gv
        nm = ADAM_B1 * m_ref[...] + (1.0 - ADAM_B1) * gv
        nv = ADAM_B2 * v_ref[...] + (1.0 - ADAM_B2) * (gv * gv)
        nm_ref[...] = nm
        nv_ref[...] = nv
        d_ref[...] = -ADAM_LR * ((nm / c1) / (jnp.sqrt(nv / c2) + ADAM_EPS) + ADAM_WD * w_ref[...])

    spec = pl.BlockSpec((rb, cols), lambda i: (i, 0))
    return pl.pallas_call(
        body, name=name, grid=(rows // rb,), in_specs=[spec] * (3 + n_g), out_specs=[spec] * 4,
        out_shape=[jax.ShapeDtypeStruct((rows, cols), F32)] * 4,
        compiler_params=_params(("parallel",)))(wv, *gs, m, v)


def _pack(flats):
    cat = jnp.concatenate([f.reshape(-1) for f in flats])
    n = cat.shape[0]
    rows = -(-n // (8 * LANES)) * 8
    return jnp.pad(cat, (0, rows * LANES - n)).reshape(rows, LANES)


def _unpack(packed, shapes):
    flat = packed.reshape(-1)
    out, off = [], 0
    for shp in shapes:
        n = 1
        for dim in shp:
            n *= dim
        out.append(flat[off:off + n].reshape(shp))
        off += n
    return out


def kernel(x, p, norm_g, w_in, conv_w, conv_b, dt_bias, a_log, d_skip, gnorm_g, pool_mix_w, pool_mix_b, pool_scale, w_branch_a, w_branch_b, w_out, ple_norm_g, w_ple_gate, w_ple_up, final_g, loss_target, m_norm_g, m_w_in, m_conv_w, m_conv_b, m_dt_bias, m_a_log, m_d_skip, m_gnorm_g, m_pool_mix_w, m_pool_mix_b, m_pool_scale, m_w_branch_a, m_w_branch_b, m_w_out, m_ple_norm_g, m_w_ple_gate, m_w_ple_up, m_final_g, v_norm_g, v_w_in, v_conv_w, v_conv_b, v_dt_bias, v_a_log, v_d_skip, v_gnorm_g, v_pool_mix_w, v_pool_mix_b, v_pool_scale, v_w_branch_a, v_w_branch_b, v_w_out, v_ple_norm_g, v_w_ple_gate, v_w_ple_up, v_final_g):
    wts = dict(norm_g=norm_g, w_in=w_in, conv_w=conv_w, conv_b=conv_b, dt_bias=dt_bias, a_log=a_log, d_skip=d_skip,
               gnorm_g=gnorm_g, pool_mix_w=pool_mix_w, pool_mix_b=pool_mix_b, pool_scale=pool_scale,
               w_branch_a=w_branch_a, w_branch_b=w_branch_b, w_out=w_out, ple_norm_g=ple_norm_g,
               w_ple_gate=w_ple_gate, w_ple_up=w_ple_up, final_g=final_g)
    mom_m = dict(norm_g=m_norm_g, w_in=m_w_in, conv_w=m_conv_w, conv_b=m_conv_b, dt_bias=m_dt_bias, a_log=m_a_log,
                 d_skip=m_d_skip, gnorm_g=m_gnorm_g, pool_mix_w=m_pool_mix_w, pool_mix_b=m_pool_mix_b,
                 pool_scale=m_pool_scale, w_branch_a=m_w_branch_a, w_branch_b=m_w_branch_b, w_out=m_w_out,
                 ple_norm_g=m_ple_norm_g, w_ple_gate=m_w_ple_gate, w_ple_up=m_w_ple_up, final_g=m_final_g)
    mom_v = dict(norm_g=v_norm_g, w_in=v_w_in, conv_w=v_conv_w, conv_b=v_conv_b, dt_bias=v_dt_bias, a_log=v_a_log,
                 d_skip=v_d_skip, gnorm_g=v_gnorm_g, pool_mix_w=v_pool_mix_w, pool_mix_b=v_pool_mix_b,
                 pool_scale=v_pool_scale, w_branch_a=v_w_branch_a, w_branch_b=v_w_branch_b, w_out=v_w_out,
                 ple_norm_g=v_ple_norm_g, w_ple_gate=v_w_ple_gate, w_ple_up=v_w_ple_up, final_g=v_final_g)
    c = lax.axis_index("c")
    k = 2 * lax.axis_index("x") + lax.axis_index("y")
    flat2 = lambda a: a.reshape(-1, a.shape[-1])

    slots = {n: _into_slot(flat2(wts[n]), k, BF16, "slot_" + n) for n in BIG}
    w_in_g, conv_g = _gather_weights([slots["w_in"]], [_into_slot(flat2(conv_w), k, F32, "slot_conv_w")])
    n_rest = len(REST)
    gsend, grecv, gbufs, gtoken = _split_start("gather_rest_start", [slots[n] for n in REST], conv_g,
                                               _gather_plan(n_rest), 3 * n_rest)

    def rest_weights(after):
        return dict(zip(REST, _split_wait("gather_rest_wait", gbufs, gsend, grecv, after, _gather_plan(n_rest))))

    flying = {}

    def early_grads(early):
        sends = [_to_bf16(early[n], "bf16_" + n) for n in REST]
        lands = [pltpu.with_memory_space_constraint(lax.empty((N_SHARD - 1,) + v.shape[1:], BF16), pltpu.HBM)
                 for v in sends]
        ssend, srecv, sbufs, stoken = _split_start("scatter_rest_start", sends + lands, early[REST[0]],
                                                   _scatter_plan(n_rest), 3 * n_rest)
        flying.update(send=ssend, recv=srecv, bufs=sbufs)
        return stoken

    def w_in_grad(g_w_in):
        got = _swap_halves([g_w_in])[0]
        pair = _add_pair(g_w_in, got, c, "add_pair_w_in")
        land = pltpu.with_memory_space_constraint(lax.empty((N_SHARD - 1,) + pair.shape[1:], BF16), pltpu.HBM)
        wsend, wrecv, wbufs, wtoken = _split_start("scatter_w_in_start", [pair, land], got, _scatter_plan(1), 3)
        flying.update(w_send=wsend, w_recv=wrecv, w_bufs=wbufs, w_got=got)
        return wtoken

    small = {n: wts[n] for n in SMALL}
    small["norm_g"] = norm_g + gtoken[0, 0]
    loss, grad_x, grads = _local_step(x, p[0], loss_target, dict(w_in=w_in_g, conv_w=conv_g), small,
                                      rest_weights, early_grads, w_in_grad)
    g_w_in = grads["w_in"]
    landed = _split_wait("scatter_w_in_wait", flying["w_bufs"], flying["w_send"], flying["w_recv"], grad_x,
                         _scatter_plan(1))[1]
    w_in_sum = _join_halves([_add_chips(g_w_in, flying["w_got"], landed, k, c, "add_chips_w_in")])[0]

    sbufs = _split_wait("scatter_rest_wait", flying["bufs"], flying["send"], flying["recv"], g_w_in,
                        _scatter_plan(n_rest))
    mine = [_add_landed(grads[n], ld, k, "add_landed_" + n) for n, ld in zip(REST, sbufs[n_rest:])]
    theirs = _swap_sibling(mine)
    g_sums = dict(zip(REST, zip(mine, theirs)))
    g_sums["w_in"] = (w_in_sum,)

    conv_shape = flat2(conv_w).shape
    small_sum = _allreduce_small(_pack([grads[n] for n in SMALL] + [grads["conv_w"], loss]))
    small_shapes = [wts[n].shape for n in SMALL] + [(N_SHARD,) + conv_shape, (1,)]
    small_g = _unpack(small_sum, small_shapes)
    g_conv = lax.dynamic_index_in_dim(small_g[-2], k, axis=0, keepdims=False)

    outs = {}
    for n in BIG:
        vals = _adamw(flat2(wts[n]), g_sums[n], flat2(mom_m[n]), flat2(mom_v[n]), "adamw_" + n)
        for kind, val in zip(("grad", "delta", "new_m", "new_v"), vals):
            outs[kind, n] = val.reshape(wts[n].shape)
    names = SMALL + ("conv_w",)
    sm = _adamw(_pack([wts[n] for n in names]), (_pack(small_g[:len(SMALL)] + [g_conv]),),
                _pack([mom_m[n] for n in names]), _pack([mom_v[n] for n in names]), "adamw_small")
    sm_shapes = [wts[n].shape for n in names]
    for kind, val in zip(("grad", "delta", "new_m", "new_v"), sm):
        for n, piece in zip(names, _unpack(val, sm_shapes)):
            outs[kind, n] = piece
    return (small_g[-1][0], grad_x, *[outs[kind, n] for kind in ("grad", "delta", "new_m", "new_v") for n in WEIGHTS])
```

```python
import functools

import jax
import jax.numpy as jnp
from jax import lax
from jax.experimental import pallas as pl
from jax.experimental.pallas import tpu as pltpu

F32 = jnp.float32
BF16 = jnp.bfloat16
HIGHEST = lax.Precision.HIGHEST
MESH = pl.DeviceIdType.MESH

EPS = 1e-6
HEAD_DIM = 64
SSM_GROUPS = 4
D_STATE = 128
CONV_WIDTH = 4
CHUNK = 128
N_POOL = 4
LANES = 128
N_SHARD = 4

ADAM_LR = 0.001
ADAM_B1 = 0.9
ADAM_B2 = 0.999
ADAM_EPS = 1e-08
ADAM_WD = 0.01
ADAM_STEP = 10

BIG = ("w_in", "pool_mix_w", "w_branch_a", "w_branch_b", "w_out", "w_ple_gate", "w_ple_up")
REST = BIG[1:]
SMALL = ("norm_g", "conv_b", "dt_bias", "a_log", "d_skip", "gnorm_g", "pool_mix_b", "pool_scale",
         "ple_norm_g", "final_g")
WEIGHTS = ("norm_g", "w_in", "conv_w", "conv_b", "dt_bias", "a_log", "d_skip", "gnorm_g", "pool_mix_w",
           "pool_mix_b", "pool_scale", "w_branch_a", "w_branch_b", "w_out", "ple_norm_g", "w_ple_gate",
           "w_ple_up", "final_g")


def _params(sem=None, vmem_mb=56):
    kw = dict(vmem_limit_bytes=vmem_mb << 20)
    if sem is not None:
        kw["dimension_semantics"] = sem
    return pltpu.CompilerParams(**kw)


def _sigmoid(v):
    return 0.5 * jnp.tanh(0.5 * v) + 0.5


def _sigmoid_tail(v):
    return 1.0 / (1.0 + jnp.exp(-v))


def _softplus(v):
    return jnp.maximum(v, 0.0) + jnp.log1p(jnp.exp(-jnp.abs(v)))


def _bdot(a, b):
    return jnp.dot(a.astype(BF16), b.astype(BF16), preferred_element_type=F32)


def _bdot_nt(a, b):
    return lax.dot_general(a.astype(BF16), b.astype(BF16), (((1,), (1,)), ((), ())), preferred_element_type=F32)


def _bdot_tn(a, b):
    return lax.dot_general(a.astype(BF16), b.astype(BF16), (((0,), (0,)), ((), ())), preferred_element_type=F32)


def _col_block(col0, width):
    assert col0 % width == 0, (col0, width)
    return col0 // width


def _tile(n, cap, unit=8):
    if n <= cap:
        return n
    best = None
    for cand in range(unit, cap + 1, unit):
        if n % cand == 0:
            best = cand
    assert best is not None, (n, cap)
    return best


def _shift_down(v, j, row):
    return jnp.where(row >= j, pltpu.roll(v, j, 0), 0.0)


def _shift_up(v, j, row):
    n = v.shape[0]
    return jnp.where(row < n - j, pltpu.roll(v, n - j, 0), 0.0)


def _mm(a, w, name, tm=1024, tn=1024):
    t, k = a.shape
    n = w.shape[1]
    tm, tn = min(tm, t), min(tn, n)

    def body(a_ref, w_ref, o_ref):
        o_ref[...] = _bdot(a_ref[...], w_ref[...]).astype(BF16)

    return pl.pallas_call(
        body, name=name, grid=(t // tm, n // tn),
        in_specs=[pl.BlockSpec((tm, k), lambda i, j: (i, 0)), pl.BlockSpec((k, tn), lambda i, j: (0, j))],
        out_specs=pl.BlockSpec((tm, tn), lambda i, j: (i, j)),
        out_shape=jax.ShapeDtypeStruct((t, n), BF16),
        compiler_params=_params(("parallel", "parallel")))(a, w)


def _mm_nt(a, w, name, tm=1024, tk=1024):
    t, k = a.shape
    n = w.shape[0]
    tm, tk = min(tm, t), min(tk, k)

    def body(a_ref, w_ref, o_ref):
        kk = pl.program_id(1)
        part = _bdot_nt(a_ref[...], w_ref[...])

        @pl.when(kk == 0)
        def _():
            o_ref[...] = part

        @pl.when(kk > 0)
        def _():
            o_ref[...] += part

    return pl.pallas_call(
        body, name=name, grid=(t // tm, k // tk),
        in_specs=[pl.BlockSpec((tm, tk), lambda i, j: (i, j)), pl.BlockSpec((n, tk), lambda i, j: (0, j))],
        out_specs=pl.BlockSpec((tm, n), lambda i, j: (i, 0)),
        out_shape=jax.ShapeDtypeStruct((t, n), F32),
        compiler_params=_params(("parallel", "arbitrary")))(a, w)


def _mm_tn(a, b, name, tn=1024, tk=2048, col_blocks=False):
    t, m = a.shape
    n = b.shape[1]
    tn, tk = min(tn, n), min(tk, t)

    def body(a_ref, b_ref, o_ref):
        kk = pl.program_id(1)
        part = _bdot_tn(a_ref[...], b_ref[...])
        part = part[None] if col_blocks else part

        @pl.when(kk == 0)
        def _():
            o_ref[...] = part

        @pl.when(kk > 0)
        def _():
            o_ref[...] += part

    if col_blocks:
        out_spec = pl.BlockSpec((1, m, tn), lambda j, kk: (j, 0, 0))
        out_shape = jax.ShapeDtypeStruct((n // tn, m, tn), F32)
    else:
        out_spec = pl.BlockSpec((m, tn), lambda j, kk: (0, j))
        out_shape = jax.ShapeDtypeStruct((m, n), F32)
    return pl.pallas_call(
        body, name=name, grid=(n // tn, t // tk),
        in_specs=[pl.BlockSpec((tk, m), lambda j, kk: (kk, 0)), pl.BlockSpec((tk, tn), lambda j, kk: (kk, j))],
        out_specs=out_spec, out_shape=out_shape,
        compiler_params=_params(("parallel", "arbitrary")))(a, b)


def _w_in_pieces(d, di, dc, nh, shard_w):
    pgd = d // N_POOL
    o_dt, o_u = di + dc, di + dc + nh
    o_zp, o_ga, o_gb = o_u + d, o_u + 2 * d, o_u + 3 * d
    c_z, c_uz = 2 * d, 2 * d + di + dc
    runs = [(False, 0, o_ga, d), (False, d, o_gb, d), (False, c_z, 0, di + dc), (True, 0, o_dt, nh)]
    for g in range(N_POOL):
        runs.append((False, c_uz + 2 * g * pgd, o_u + g * pgd, pgd))
        runs.append((False, c_uz + (2 * g + 1) * pgd, o_zp + g * pgd, pgd))
    pieces = []
    for is_dt, dst, src, n in runs:
        while n > 0:
            k, off = divmod(src, shard_w)
            m = min(n, shard_w - off)
            pieces.append((is_dt, dst, k, off, m))
            dst, src, n = dst + m, src + m, n - m
    return pieces


def _regroup_w_in(w_sh, d, di, dc, nh, rb=256):
    _, rows, sw = w_sh.shape
    n_main = 4 * d + di + dc
    pieces = _w_in_pieces(d, di, dc, nh, sw)
    rb = min(rb, rows)

    def body(w_ref, main_ref, dt_ref):
        dt_ref[...] = jnp.zeros_like(dt_ref)
        for is_dt, dst, k, off, m in pieces:
            out = dt_ref if is_dt else main_ref
            out[:, dst:dst + m] = w_ref[k, :, off:off + m]

    return pl.pallas_call(
        body, name="regroup_w_in", grid=(rows // rb,),
        in_specs=[pl.BlockSpec((N_SHARD, rb, sw), lambda i: (0, i, 0))],
        out_specs=[pl.BlockSpec((rb, n_main), lambda i: (i, 0)), pl.BlockSpec((rb, LANES), lambda i: (i, 0))],
        out_shape=[jax.ShapeDtypeStruct((rows, n_main), w_sh.dtype), jax.ShapeDtypeStruct((rows, LANES), w_sh.dtype)],
        compiler_params=_params(("parallel",)))(w_sh)


def _ungroup_w_in(d_main, d_dt, d, di, dc, nh, rb=128):
    rows, n_main = d_main.shape
    sw = (n_main + nh) // N_SHARD
    pieces = _w_in_pieces(d, di, dc, nh, sw)
    rb = min(rb, rows)

    def body(main_ref, dt_ref, o_ref):
        for is_dt, dst, k, off, m in pieces:
            src = dt_ref if is_dt else main_ref
            o_ref[k, :, off:off + m] = src[:, dst:dst + m]

    return pl.pallas_call(
        body, name="ungroup_w_in", grid=(rows // rb,),
        in_specs=[pl.BlockSpec((rb, n_main), lambda i: (i, 0)), pl.BlockSpec((rb, LANES), lambda i: (i, 0))],
        out_specs=pl.BlockSpec((N_SHARD, rb, sw), lambda i: (0, i, 0)),
        out_shape=jax.ShapeDtypeStruct((N_SHARD, rows, sw), F32),
        compiler_params=_params(("parallel",)))(d_main, d_dt)


def _inproj(x2, norm_g, w_main, w_dt, tm=1024, tn=1024):
    t, d = x2.shape
    n = w_main.shape[1]
    tm, tn = min(tm, t), min(tn, n)

    def body(x_ref, g_ref, w_ref, wdt_ref, proj_ref, dt_ref, h_ref):
        @pl.when(pl.program_id(1) == 0)
        def _():
            xv = x_ref[...]
            r = lax.rsqrt(jnp.mean(xv * xv, axis=-1, keepdims=True) + EPS)
            h = (xv * r * g_ref[...]).astype(BF16)
            h_ref[...] = h
            dt_ref[...] = jnp.dot(h, wdt_ref[...].astype(BF16), preferred_element_type=F32)

        proj_ref[...] = jnp.dot(h_ref[...], w_ref[...].astype(BF16), preferred_element_type=F32)

    return pl.pallas_call(
        body, name="inproj", grid=(t // tm, n // tn),
        in_specs=[pl.BlockSpec((tm, d), lambda i, j: (i, 0)), pl.BlockSpec((1, d), lambda i, j: (0, 0)),
                  pl.BlockSpec((d, tn), lambda i, j: (0, j)), pl.BlockSpec((d, LANES), lambda i, j: (0, 0))],
        out_specs=[pl.BlockSpec((tm, tn), lambda i, j: (i, j)), pl.BlockSpec((tm, LANES), lambda i, j: (i, 0)),
                   pl.BlockSpec((tm, d), lambda i, j: (i, 0))],
        out_shape=[jax.ShapeDtypeStruct((t, n), F32), jax.ShapeDtypeStruct((t, LANES), F32),
                   jax.ShapeDtypeStruct((t, d), BF16)],
        compiler_params=_params(("parallel", "arbitrary")))(x2, norm_g, w_main, w_dt)


def _conv_w_spec(conv_w, cb, j_axis):
    sw = conv_w.shape[2]
    assert sw % cb == 0, (sw, cb)
    per = sw // cb
    return N_SHARD * per, pl.BlockSpec((1, CONV_WIDTH, cb), lambda *ij: (ij[j_axis] // per, 0, ij[j_axis] % per))


CONV_ROWS = 64
CONV_HALO = 8


def _conv_taps(x_ref, t0, rc):
    if t0 == 0:
        cur = x_ref[0:rc, :]
        row = lax.broadcasted_iota(jnp.int32, cur.shape, 0)
        return [cur] + [_shift_down(cur, j, row) for j in range(1, CONV_WIDTH)]
    ext = x_ref[t0 - CONV_HALO:t0 + rc, :]
    return [ext[CONV_HALO:]] + [pltpu.roll(ext, j, 0)[CONV_HALO:] for j in range(1, CONV_WIDTH)]


def _conv_weights(w_ref):
    return [w_ref[0, CONV_WIDTH - 1 - j:CONV_WIDTH - j, :] for j in range(CONV_WIDTH)]


def _conv_pre(taps, wts, bias):
    acc = bias + taps[0] * wts[0]
    for j in range(1, CONV_WIDTH):
        acc = acc + taps[j] * wts[j]
    return acc


def _conv_fwd(proj, conv_w, conv_b, nb, s, col0, cb=256):
    n_blk, w_spec = _conv_w_spec(conv_w, cb, 1)
    blk0 = _col_block(col0, cb)
    rc = min(CONV_ROWS, s)

    def body(x_ref, w_ref, b_ref, o_ref):
        wts, bias = _conv_weights(w_ref), b_ref[...]
        for t0 in range(0, s, rc):
            acc = _conv_pre(_conv_taps(x_ref, t0, rc), wts, bias)
            o_ref[t0:t0 + rc, :] = acc * _sigmoid(acc)

    return pl.pallas_call(
        body, name="conv_fwd", grid=(nb, n_blk),
        in_specs=[pl.BlockSpec((s, cb), lambda b, j: (b, blk0 + j)), w_spec, pl.BlockSpec((1, cb), lambda b, j: (0, j))],
        out_specs=pl.BlockSpec((s, cb), lambda b, j: (b, j)),
        out_shape=jax.ShapeDtypeStruct((nb * s, n_blk * cb), F32),
        compiler_params=_params(("parallel", "parallel")))(proj, conv_w, conv_b)


def _ssd_consts(di):
    r = lax.broadcasted_iota(jnp.int32, (CHUNK, CHUNK), 0)
    c = lax.broadcasted_iota(jnp.int32, (CHUNK, CHUNK), 1)
    tril = (r >= c).astype(F32)
    head = lax.broadcasted_iota(jnp.int32, (LANES, di), 0)
    chan = lax.broadcasted_iota(jnp.int32, (LANES, di), 1) // HEAD_DIM
    expand = (head == chan).astype(BF16)
    return tril, expand, expand.T


def _expand(v, e, terms=3):
    acc = None
    for _ in range(terms):
        vb = v.astype(BF16)
        part = jnp.dot(vb, e, preferred_element_type=F32)
        acc = part if acc is None else acc + part
        v = v - vb.astype(F32)
    return acc


def _head_sum(t, et, terms=2):
    acc = None
    for _ in range(terms):
        tb = t.astype(BF16)
        part = jnp.dot(tb, et, preferred_element_type=F32)
        acc = part if acc is None else acc + part
        t = t - tb.astype(F32)
    return acc


def _ssd_scalars(dtr_ref, dtb_ref, alog_ref, tri):
    dtpre = dtr_ref[...] + dtb_ref[...]
    dt = _softplus(dtpre)
    a_neg = -jnp.exp(alog_ref[...])
    a_dt = dt * a_neg
    a_cs = jnp.dot(tri, a_dt, precision=HIGHEST, preferred_element_type=F32)
    a_cst = lax.dot_general(a_dt, tri, (((0,), (1,)), ((), ())), precision=HIGHEST, preferred_element_type=F32)
    return dtpre, dt, a_neg, a_cs, a_cst


def _ssd_fwd(proj, xbc, dtraw, dtb, alog, dskx, gn, nb, s, di, z_col0):
    t = nb * s
    nc = s // CHUNK
    hpg = di // HEAD_DIM // SSM_GROUPS
    gw = di // SSM_GROUPS
    gn_w = SSM_GROUPS * D_STATE
    b_blk = _col_block(di, gn_w)
    z_blk = _col_block(z_col0, di)
    L, P, N = CHUNK, HEAD_DIM, D_STATE
    tril, expand, _ = _ssd_consts(di)

    def body(z_ref, x_ref, b_ref, c_ref, dtr_ref, dtb_ref, alog_ref, dskx_ref, gn_ref, tril_ref, e_ref,
             ypre_ref, yan_ref, hp_ref, st_ref, yd_ref, xdt_ref):
        @pl.when(pl.program_id(1) == 0)
        def _():
            st_ref[...] = jnp.zeros_like(st_ref)

        hp_ref[0] = st_ref[...]
        tri = tril_ref[...]
        _, dt, _, a_cs, a_cst = _ssd_scalars(dtr_ref, dtb_ref, alog_ref, tri)
        ev = e_ref[...]
        a_exp = _expand(a_cs, ev)
        xv = x_ref[...]
        xdt = xv * _expand(dt, ev, terms=2)
        xdt_ref[...] = xdt
        a_last = a_exp[L - 1:L, :]
        xe = xdt * jnp.exp(a_last - a_exp)
        ea = jnp.exp(a_exp)
        e_last = jnp.exp(a_last)
        lower = tri > 0.5
        for g in range(SSM_GROUPS):
            gs = slice(g * gw, (g + 1) * gw)
            bg = b_ref[:, g * N:(g + 1) * N].astype(BF16)
            cg = c_ref[:, g * N:(g + 1) * N].astype(BF16)
            gm = _bdot_nt(cg, bg)
            ht = st_ref[:, gs]
            ch = _bdot(cg, ht)
            for e in range(hpg):
                h = g * hpg + e
                hs = slice(h * P, (h + 1) * P)
                decay = jnp.where(lower, jnp.exp(a_cs[:, h:h + 1] - a_cst[h:h + 1, :]), 0.0)
                yd_ref[:, hs] = _bdot(gm * decay, xdt_ref[:, hs])
            st_ref[:, gs] = ht * e_last[:, gs] + _bdot_tn(bg, xe[:, gs])
            ypre = yd_ref[:, gs] + ea[:, gs] * ch + xv[:, gs] * dskx_ref[:, gs]
            ypre_ref[:, gs] = ypre
            zv = z_ref[:, gs]
            v = ypre * zv * _sigmoid(zv)
            r = lax.rsqrt(jnp.mean(v * v, axis=-1, keepdims=True) + EPS)
            yan_ref[:, gs] = (v * r * gn_ref[:, gs]).astype(BF16)

    row = lambda b, c: b * nc + c
    vec = lambda w: pl.BlockSpec((1, w), lambda b, c: (0, 0))
    return pl.pallas_call(
        body, name="ssd_fwd", grid=(nb, nc),
        in_specs=[pl.BlockSpec((L, di), lambda b, c: (row(b, c), z_blk)),
                  pl.BlockSpec((L, di), lambda b, c: (row(b, c), 0)),
                  pl.BlockSpec((L, gn_w), lambda b, c: (row(b, c), b_blk)),
                  pl.BlockSpec((L, gn_w), lambda b, c: (row(b, c), b_blk + 1)),
                  pl.BlockSpec((L, LANES), lambda b, c: (row(b, c), 0)),
                  vec(LANES), vec(LANES), vec(di), vec(di),
                  pl.BlockSpec((L, L), lambda b, c: (0, 0)),
                  pl.BlockSpec((LANES, di), lambda b, c: (0, 0))],
        out_specs=[pl.BlockSpec((L, di), lambda b, c: (row(b, c), 0)),
                   pl.BlockSpec((L, di), lambda b, c: (row(b, c), 0)),
                   pl.BlockSpec((1, N, di), lambda b, c: (row(b, c), 0, 0))],
        out_shape=[jax.ShapeDtypeStruct((t, di), F32), jax.ShapeDtypeStruct((t, di), BF16),
                   jax.ShapeDtypeStruct((nb * nc, N, di), F32)],
        scratch_shapes=[pltpu.VMEM((N, di), F32), pltpu.VMEM((L, di), F32), pltpu.VMEM((L, di), F32)],
        compiler_params=_params(("parallel", "arbitrary")))(
            proj, xbc, xbc, xbc, dtraw, dtb, alog, dskx, gn, tril, expand)


def _pool_sum(v, g, row, shift):
    s2 = v + shift(v, 1, row)
    s4 = s2 + shift(s2, 2, row)
    s8 = s4 + shift(s4, 4, row)
    s16 = s8 + shift(s8, 8, row)
    return jnp.where(g == 0, s2, jnp.where(g == 1, s4, jnp.where(g == 2, s8, s16)))


def _pool_count(g, row):
    return jnp.minimum(row + 1, jnp.left_shift(2, g)).astype(F32)


POOL_ROWS = 128
POOL_HALO = 16


def _roll_sum(v, g, step):
    n = v.shape[0]
    s2 = v + pltpu.roll(v, step % n, 0)
    s4 = s2 + pltpu.roll(s2, (2 * step) % n, 0)
    s8 = s4 + pltpu.roll(s4, (4 * step) % n, 0)
    s16 = s8 + pltpu.roll(s8, (8 * step) % n, 0)
    return jnp.where(g == 0, s2, jnp.where(g == 1, s4, jnp.where(g == 2, s8, s16)))


def _pool_trailing(u_ref, pgd, t0, rc, g):
    if t0 == 0:
        cur = u_ref[0:rc, :pgd]
        row = lax.broadcasted_iota(jnp.int32, cur.shape, 0)
        return cur, _pool_sum(cur, g, row, _shift_down), _pool_count(g, row)
    ext = u_ref[t0 - POOL_HALO:t0 + rc, :pgd]
    return ext[POOL_HALO:], _roll_sum(ext, g, 1)[POOL_HALO:], jnp.left_shift(2, g).astype(F32)


def _pool_fwd(proj, mix_w, mix_b, scale, nb, s, col0):
    pgd = mix_w.shape[-1]
    blk0 = _col_block(col0, 2 * pgd)
    rc = min(POOL_ROWS, s)

    def body(uz_ref, w_ref, b_ref, sc_ref, o_ref):
        g = pl.program_id(1)
        wv = w_ref[:, 0].reshape(pgd, pgd)
        for t0 in range(0, s, rc):
            u, win, cnt = _pool_trailing(uz_ref, pgd, t0, rc, g)
            zp = uz_ref[t0:t0 + rc, pgd:]
            mixed = _bdot(win / cnt - u, wv) + b_ref[...]
            o_ref[t0:t0 + rc, :] = (mixed * sc_ref[...] * zp * _sigmoid(zp)).astype(BF16)

    return pl.pallas_call(
        body, name="pool_fwd", grid=(nb, N_POOL),
        in_specs=[pl.BlockSpec((s, 2 * pgd), lambda b, g: (b, blk0 + g)),
                  pl.BlockSpec((N_SHARD, 1, pgd // N_SHARD, pgd), lambda b, g: (0, g, 0, 0)),
                  pl.BlockSpec((1, pgd), lambda b, g: (0, g)), pl.BlockSpec((1, pgd), lambda b, g: (0, g))],
        out_specs=pl.BlockSpec((s, pgd), lambda b, g: (b, g)),
        out_shape=jax.ShapeDtypeStruct((nb * s, N_POOL * pgd), BF16),
        compiler_params=_params(("parallel", "parallel")))(proj, mix_w, mix_b, scale)


def _mid_fwd(ya, yb, proj, col0, x2, p2, tgt, wo, wpg, wup, ple_g, final_g, tm=256):
    t, d = ya.shape
    tm = min(tm, t)
    blk = _col_block(col0, 2 * d)
    n_up, pdim, up_w = wup.shape

    def body(ya_ref, yb_ref, g_ref, x_ref, p_ref, tgt_ref, wo_ref, wpg_ref, wup_ref, pg_ref, g_fin_ref,
             merged_ref, hn_ref, dpre_ref, dpu_ref, x1_ref, dx2_ref, loss_ref, dg_ref):
        @pl.when(pl.program_id(0) == 0)
        def _():
            loss_ref[...] = jnp.zeros_like(loss_ref)
            dg_ref[...] = jnp.zeros_like(dg_ref)

        merged = (_sigmoid(g_ref[:, :d]) * ya_ref[...] + _sigmoid(g_ref[:, d:]) * yb_ref[...]).astype(BF16)
        merged_ref[...] = merged
        x1 = x_ref[...] + jnp.dot(merged, wo_ref[...], preferred_element_type=F32)
        x1_ref[...] = x1
        r1 = lax.rsqrt(jnp.mean(x1 * x1, axis=-1, keepdims=True) + EPS)
        hn = (x1 * r1 * pg_ref[...]).astype(BF16)
        hn_ref[...] = hn
        gate = _sigmoid(jnp.dot(hn, wpg_ref[...], preferred_element_type=F32))
        pb = p_ref[...].astype(BF16)
        pu = jnp.concatenate([jnp.dot(pb, wup_ref[j], preferred_element_type=F32) for j in range(n_up)], axis=1)
        x2 = x1 + gate * pu
        r = lax.rsqrt(jnp.mean(x2 * x2, axis=-1, keepdims=True) + EPS)
        xn = x2 * r
        fg = g_fin_ref[...]
        err = xn * fg - tgt_ref[...]
        loss_ref[...] += 0.5 * jnp.sum(jnp.mean(err * err, axis=-1, keepdims=True))
        dy = err * (1.0 / d)
        dg_ref[...] += jnp.sum(dy * xn, axis=0, keepdims=True)
        dxn = dy * fg
        dx2 = r * (dxn - xn * jnp.mean(dxn * xn, axis=-1, keepdims=True))
        dx2_ref[...] = dx2
        dpre_ref[...] = (dx2 * pu * gate * (1.0 - gate)).astype(BF16)
        dpu_ref[...] = (dx2 * gate).astype(BF16)

    row = pl.BlockSpec((tm, d), lambda i: (i, 0))
    vec = pl.BlockSpec((1, d), lambda i: (0, 0))
    whole = lambda a: pl.BlockSpec(a.shape, lambda i: (0,) * a.ndim)
    return pl.pallas_call(
        body, name="mid_fwd", grid=(t // tm,),
        in_specs=[row, row, pl.BlockSpec((tm, 2 * d), lambda i: (i, blk)), row,
                  pl.BlockSpec((tm, pdim), lambda i: (i, 0)), row, whole(wo), whole(wpg), whole(wup), vec, vec],
        out_specs=[row] * 6 + [pl.BlockSpec((1, LANES), lambda i: (0, 0)), vec],
        out_shape=[jax.ShapeDtypeStruct((t, d), BF16)] * 4 + [jax.ShapeDtypeStruct((t, d), F32)] * 2 + [
            jax.ShapeDtypeStruct((1, LANES), F32), jax.ShapeDtypeStruct((1, d), F32)],
        compiler_params=_params(("arbitrary",)))(ya, yb, proj, x2, p2, tgt, wo, wpg, wup, ple_g, final_g)


def _rms_grad(xv, dh, g):
    r = lax.rsqrt(jnp.mean(xv * xv, axis=-1, keepdims=True) + EPS)
    xn = xv * r
    dd = dh * g
    return r * (dd - xn * jnp.mean(dd * xn, axis=-1, keepdims=True)), jnp.sum(dh * xn, axis=0, keepdims=True)


def _mid_bwd(dpre, dx2, x1, ya, yb, proj, col0, wpg, wo, ple_g, n_cols, tm=256):
    t, d = ya.shape
    tm = min(tm, t)
    blk = _col_block(col0, 2 * d)

    def body(dpre_ref, dx2_ref, x1_ref, ya_ref, yb_ref, g_ref, wpg_ref, wo_ref, pg_ref,
             dx1_ref, dya_ref, dyb_ref, dg_ref, dpg_ref):
        @pl.when(pl.program_id(0) == 0)
        def _():
            dpg_ref[...] = jnp.zeros_like(dpg_ref)

        dhn = _bdot_nt(dpre_ref[...], wpg_ref[...])
        dx, dpg = _rms_grad(x1_ref[...], dhn, pg_ref[...])
        dpg_ref[...] += dpg
        dx1 = dx2_ref[...] + dx
        dx1_ref[...] = dx1
        dm_v = _bdot_nt(dx1, wo_ref[...])
        sa = _sigmoid(g_ref[:, :d])
        sb = _sigmoid(g_ref[:, d:])
        dya_ref[...] = (dm_v * sa).astype(BF16)
        dyb_ref[...] = (dm_v * sb).astype(BF16)
        dg_ref[:, :d] = (dm_v * ya_ref[...] * sa * (1.0 - sa)).astype(BF16)
        dg_ref[:, d:] = (dm_v * yb_ref[...] * sb * (1.0 - sb)).astype(BF16)

    row = pl.BlockSpec((tm, d), lambda i: (i, 0))
    vec = pl.BlockSpec((1, d), lambda i: (0, 0))
    gspec = pl.BlockSpec((tm, 2 * d), lambda i: (i, blk))
    whole = lambda a: pl.BlockSpec(a.shape, lambda i: (0,) * a.ndim)
    return pl.pallas_call(
        body, name="mid_bwd", grid=(t // tm,),
        in_specs=[row, row, row, row, row, gspec, whole(wpg), whole(wo), vec],
        out_specs=[row, row, row, gspec, vec],
        out_shape=[jax.ShapeDtypeStruct((t, d), F32), jax.ShapeDtypeStruct((t, d), BF16),
                   jax.ShapeDtypeStruct((t, d), BF16), jax.ShapeDtypeStruct((t, n_cols), BF16),
                   jax.ShapeDtypeStruct((1, d), F32)],
        compiler_params=_params(("arbitrary",)))(dpre, dx2, x1, ya, yb, proj, wpg, wo, ple_g)


def _in_bwd(dproj, w_main, ddt, w_dt, x2, dx1, norm_g, after, tm=1024, tk=1024):
    t, k = dproj.shape
    d = x2.shape[1]
    tm, tk = min(tm, t), min(tk, k)
    nk = k // tk

    def body(a_ref, w_ref, ddt_ref, wdt_ref, x_ref, dres_ref, g_ref, _, gx_ref, dg_ref, acc_ref):
        kk = pl.program_id(1)

        @pl.when((pl.program_id(0) == 0) & (kk == 0))
        def _():
            dg_ref[...] = jnp.zeros_like(dg_ref)

        part = _bdot_nt(a_ref[...], w_ref[...])

        @pl.when(kk == 0)
        def _():
            acc_ref[...] = part

        @pl.when(kk > 0)
        def _():
            acc_ref[...] += part

        @pl.when(kk == nk - 1)
        def _():
            dh = acc_ref[...] + _bdot_nt(ddt_ref[...], wdt_ref[...])
            dx, dg = _rms_grad(x_ref[...], dh, g_ref[...])
            dg_ref[...] += dg
            gx_ref[...] = dres_ref[...] + dx

    row = pl.BlockSpec((tm, d), lambda i, j: (i, 0))
    vec = pl.BlockSpec((1, d), lambda i, j: (0, 0))
    return pl.pallas_call(
        body, name="in_bwd", grid=(t // tm, nk),
        in_specs=[pl.BlockSpec((tm, tk), lambda i, j: (i, j)), pl.BlockSpec((d, tk), lambda i, j: (0, j)),
                  pl.BlockSpec((tm, LANES), lambda i, j: (i, 0)), pl.BlockSpec((d, LANES), lambda i, j: (0, 0)),
                  row, row, vec, pl.BlockSpec((8, LANES), lambda i, j: (0, 0))],
        out_specs=[row, vec],
        out_shape=[jax.ShapeDtypeStruct((t, d), F32), jax.ShapeDtypeStruct((1, d), F32)],
        scratch_shapes=[pltpu.VMEM((tm, d), F32)],
        compiler_params=_params(("arbitrary", "arbitrary")))(dproj, w_main, ddt, w_dt, x2, dx1, norm_g, after)


def _pool_bwd(proj, dyb, dproj, mix_w, mix_b, scale, nb, s, col0):
    pgd = mix_w.shape[-1]
    blk0 = _col_block(col0, 2 * pgd)
    rc = min(POOL_ROWS, s)

    def body(uz_ref, dy_ref, _, w_ref, b_ref, sc_ref, duz_ref, dw_ref, db_ref, dsc_ref, dpn_ref, dwacc_ref):
        g = pl.program_id(0)

        @pl.when(pl.program_id(1) == 0)
        def _():
            dw_ref[...] = jnp.zeros_like(dw_ref)
            db_ref[...] = jnp.zeros_like(db_ref)
            dsc_ref[...] = jnp.zeros_like(dsc_ref)

        wv = w_ref[:, 0].reshape(pgd, pgd)
        sc = sc_ref[...]
        fold = lambda v: v.reshape(rc // 8, 8, pgd).sum(axis=0)
        db8 = jnp.zeros((8, pgd), F32)
        dsc8 = jnp.zeros((8, pgd), F32)
        dwacc_ref[...] = jnp.zeros_like(dwacc_ref)
        for t0 in range(0, s, rc):
            u, win, cnt = _pool_trailing(uz_ref, pgd, t0, rc, g)
            pooled = win / cnt - u
            zp = uz_ref[t0:t0 + rc, pgd:]
            mixed = _bdot(pooled, wv) + b_ref[...]
            sg = _sigmoid(zp)
            sz = zp * sg
            dy = dy_ref[t0:t0 + rc, :]
            dsc8 = dsc8 + fold(dy * mixed * sz)
            dmixed = dy * sc * sz
            db8 = db8 + fold(dmixed)
            dwacc_ref[...] += _bdot_tn(pooled, dmixed)
            dpn_ref[t0:t0 + rc, :] = _bdot_nt(dmixed, wv) / cnt
            duz_ref[t0:t0 + rc, pgd:] = (dy * mixed * sc * sg * (1.0 + zp * (1.0 - sg))).astype(BF16)
        dsc_ref[...] += jnp.sum(dsc8, axis=0, keepdims=True)
        db_ref[...] += jnp.sum(db8, axis=0, keepdims=True)
        dw_ref[:, 0] += dwacc_ref[...].reshape(N_SHARD, pgd // N_SHARD, pgd)
        for t0 in range(0, s, rc):
            if t0 + rc < s:
                n = rc + POOL_HALO
                win = dpn_ref[t0:t0 + n, :]
                cur, lead = win[:rc], _roll_sum(win, g, n - 1)[:rc]
            else:
                cur = dpn_ref[t0:t0 + rc, :]
                row = lax.broadcasted_iota(jnp.int32, cur.shape, 0)
                lead = _pool_sum(cur, g, row, _shift_up)
            if t0 == 0:
                cnt = _pool_count(g, lax.broadcasted_iota(jnp.int32, cur.shape, 0))
            else:
                cnt = jnp.left_shift(2, g).astype(F32)
            duz_ref[t0:t0 + rc, :pgd] = (lead - cur * cnt).astype(BF16)

    uz = pl.BlockSpec((s, 2 * pgd), lambda g, b: (b, blk0 + g))
    vec = pl.BlockSpec((1, pgd), lambda g, b: (0, g))
    wspec = pl.BlockSpec((N_SHARD, 1, pgd // N_SHARD, pgd), lambda g, b: (0, g, 0, 0))
    return pl.pallas_call(
        body, name="pool_bwd", grid=(N_POOL, nb),
        in_specs=[uz, pl.BlockSpec((s, pgd), lambda g, b: (b, g)), pl.BlockSpec(memory_space=pl.ANY), wspec, vec, vec],
        out_specs=[uz, wspec, vec, vec],
        out_shape=[jax.ShapeDtypeStruct(dproj.shape, dproj.dtype), jax.ShapeDtypeStruct(mix_w.shape, F32),
                   jax.ShapeDtypeStruct(mix_b.shape, F32), jax.ShapeDtypeStruct(scale.shape, F32)],
        scratch_shapes=[pltpu.VMEM((s, pgd), F32), pltpu.VMEM((pgd, pgd), F32)],
        input_output_aliases={2: 0},
        compiler_params=_params(("parallel", "arbitrary")))(proj, dyb, dproj, mix_w, mix_b, scale)


def _ssd_bwd(dyan, ypre, proj, xbc, dtraw, hp, dproj, dtb, alog, dskx, gn, nb, s, di, z_col0):
    t = nb * s
    nc = s // CHUNK
    hpg = di // HEAD_DIM // SSM_GROUPS
    gw = di // SSM_GROUPS
    gn_w = SSM_GROUPS * D_STATE
    dc = di + 2 * gn_w
    b_blk = _col_block(di, gn_w)
    z_blk = _col_block(z_col0, di)
    L, P, N = CHUNK, HEAD_DIM, D_STATE
    tril, expand, expand_t = _ssd_consts(di)

    def body(dy_ref, ypre_ref, z_ref, x_ref, b_ref, c_ref, dtr_ref, hp_ref, _, dtb_ref, alog_ref, dskx_ref, gn_ref,
             tril_ref, e_ref, et_ref, dz_ref, ddt_ref, dxbc_ref, dgn_ref, ddsk_ref, dalog_ref, ddtb_ref,
             dst_ref, dyp_ref, xdt_ref, dxm_ref, t1_ref, t3_ref, aux_ref):
        @pl.when((pl.program_id(0) == 0) & (pl.program_id(1) == 0))
        def _():
            dgn_ref[...] = jnp.zeros_like(dgn_ref)
            ddsk_ref[...] = jnp.zeros_like(ddsk_ref)
            dalog_ref[...] = jnp.zeros_like(dalog_ref)
            ddtb_ref[...] = jnp.zeros_like(ddtb_ref)

        @pl.when(pl.program_id(1) == 0)
        def _():
            dst_ref[...] = jnp.zeros_like(dst_ref)

        tri = tril_ref[...]
        dtpre, dt, a_neg, a_cs, a_cst = _ssd_scalars(dtr_ref, dtb_ref, alog_ref, tri)
        ev = e_ref[...]
        a_exp = _expand(a_cs, ev)
        dt_exp = _expand(dt, ev, terms=2)
        xv = x_ref[...]
        xdt = xv * dt_exp
        xdt_ref[...] = xdt
        a_last = a_exp[L - 1:L, :]
        dte = jnp.exp(a_last - a_exp)
        xe = xdt * dte
        ea = jnp.exp(a_exp)
        e_last = jnp.exp(a_last)
        lower = tri > 0.5
        aux_ref[...] = jnp.zeros_like(aux_ref)
        for g in range(SSM_GROUPS):
            gs = slice(g * gw, (g + 1) * gw)
            zv = z_ref[:, gs]
            yp = ypre_ref[:, gs]
            sg = _sigmoid(zv)
            sz = zv * sg
            vg = yp * sz
            r = lax.rsqrt(jnp.mean(vg * vg, axis=-1, keepdims=True) + EPS)
            vn = vg * r
            dyg = dy_ref[:, gs]
            dgn_ref[:, gs] += jnp.sum(dyg * vn, axis=0, keepdims=True)
            dvn = dyg * gn_ref[:, gs]
            dv = r * (dvn - vn * jnp.mean(dvn * vn, axis=-1, keepdims=True))
            dy = dv * sz
            dyp_ref[:, gs] = dy
            dz_ref[:, gs] = (dv * yp * sg * (1.0 + zv * (1.0 - sg))).astype(BF16)
            bg = b_ref[:, g * N:(g + 1) * N].astype(BF16)
            cg = c_ref[:, g * N:(g + 1) * N].astype(BF16)
            gm = _bdot_nt(cg, bg)
            ht = hp_ref[0, :, gs]
            dht = dst_ref[:, gs]
            bds = _bdot(bg, dht)
            dye = dy * ea[:, gs]
            xe_g = xe[:, gs]
            dcg = _bdot_nt(dye, ht)
            dbg = _bdot_nt(xe_g, dht)
            dst_ref[:, gs] = e_last[:, gs] * dht + _bdot_tn(cg, dye)
            dgm = jnp.zeros((L, L), F32)
            for e in range(hpg):
                h = g * hpg + e
                hs = slice(h * P, (h + 1) * P)
                decay = jnp.where(lower, jnp.exp(a_cs[:, h:h + 1] - a_cst[h:h + 1, :]), 0.0)
                dy_h = dyp_ref[:, hs]
                dgm = dgm + _bdot_nt(dy_h, xdt_ref[:, hs]) * decay
                dxm_ref[:, hs] = _bdot_tn(gm * decay, dy_h)
            dxbc_ref[:, di + g * N:di + (g + 1) * N] = dbg + _bdot_tn(dgm, cg)
            dxbc_ref[:, di + gn_w + g * N:di + gn_w + (g + 1) * N] = dcg + _bdot(dgm, bg)
            dxm = dxm_ref[:, gs]
            x_g = xv[:, gs]
            dskx = dskx_ref[:, gs]
            xeb = xe_g * bds
            dxdt = dxm + dte[:, gs] * bds
            dxbc_ref[:, gs] = dxdt * dt_exp[:, gs] + dy * dskx
            each = ea[:, gs] * _bdot(cg, ht)
            y_diag = yp - x_g * dskx - each
            rnd = lambda v: v.astype(BF16).astype(F32)
            t1_ref[:, gs] = rnd(dy) * y_diag + dy * each - rnd(xdt[:, gs]) * dxm - xeb
            t3_ref[:, gs] = dxdt * x_g
            aux_ref[0:1, gs] = jnp.sum(dht * ht, axis=0, keepdims=True)
            aux_ref[1:2, gs] = jnp.sum(dy * x_g, axis=0, keepdims=True)
            aux_ref[2:3, gs] = jnp.sum(xeb, axis=0, keepdims=True)
        etv = et_ref[...]
        aux = _head_sum(aux_ref[...], etv)
        rowi = lax.broadcasted_iota(jnp.int32, (L, LANES), 0)
        end = aux[2:3, :] + aux[0:1, :] * jnp.exp(a_cs[L - 1:L, :])
        da = _head_sum(t1_ref[...], etv, terms=3) + jnp.where(rowi == L - 1, end, 0.0)
        rc = lax.dot_general(tri, da, (((0,), (0,)), ((), ())), precision=HIGHEST, preferred_element_type=F32)
        ddt = a_neg * rc + _head_sum(t3_ref[...], etv, terms=1)
        ddtraw = ddt * _sigmoid_tail(dtpre)
        ddt_ref[...] = ddtraw.astype(BF16)
        ddtb_ref[...] += jnp.sum(ddtraw, axis=0, keepdims=True)
        dalog_ref[...] += jnp.sum(dt * rc, axis=0, keepdims=True) * a_neg
        ddsk_ref[...] += aux[1:2, :]

    row = lambda b, c: b * nc + (nc - 1 - c)
    full = lambda w: pl.BlockSpec((L, w), lambda b, c: (row(b, c), 0))
    zspec = pl.BlockSpec((L, di), lambda b, c: (row(b, c), z_blk))
    vec = lambda w: pl.BlockSpec((1, w), lambda b, c: (0, 0))
    slab = lambda shape: pltpu.VMEM(shape, F32)
    return pl.pallas_call(
        body, name="ssd_bwd", grid=(nb, nc),
        in_specs=[full(di), full(di), zspec, full(di),
                  pl.BlockSpec((L, gn_w), lambda b, c: (row(b, c), b_blk)),
                  pl.BlockSpec((L, gn_w), lambda b, c: (row(b, c), b_blk + 1)),
                  full(LANES),
                  pl.BlockSpec((1, N, di), lambda b, c: (row(b, c), 0, 0)),
                  pl.BlockSpec(memory_space=pl.ANY),
                  vec(LANES), vec(LANES), vec(di), vec(di),
                  pl.BlockSpec((L, L), lambda b, c: (0, 0)),
                  pl.BlockSpec((LANES, di), lambda b, c: (0, 0)),
                  pl.BlockSpec((di, LANES), lambda b, c: (0, 0))],
        out_specs=[zspec, full(LANES), full(dc), vec(di), vec(LANES), vec(LANES), vec(LANES)],
        out_shape=[jax.ShapeDtypeStruct(dproj.shape, dproj.dtype), jax.ShapeDtypeStruct((t, LANES), BF16),
                   jax.ShapeDtypeStruct((t, dc), F32), jax.ShapeDtypeStruct((1, di), F32),
                   jax.ShapeDtypeStruct((1, LANES), F32), jax.ShapeDtypeStruct((1, LANES), F32),
                   jax.ShapeDtypeStruct((1, LANES), F32)],
        scratch_shapes=[slab((N, di)), slab((L, di)), slab((L, di)), slab((L, di)), slab((L, di)), slab((L, di)),
                        slab((8, di))],
        input_output_aliases={8: 0},
        compiler_params=_params(("arbitrary", "arbitrary")))(
            dyan, ypre, proj, xbc, xbc, xbc, dtraw, hp, dproj, dtb, alog, dskx, gn, tril, expand, expand_t)


def _conv_bwd(proj, dxbc, dproj, conv_w, conv_b, nb, s, col0, cb=256):
    n_blk, w_spec = _conv_w_spec(conv_w, cb, 0)
    blk0 = _col_block(col0, cb)
    rc = min(CONV_ROWS, s)

    def body(x_ref, dy_ref, _, w_ref, b_ref, dx_ref, dw_ref, db_ref, dacc_ref):
        @pl.when(pl.program_id(1) == 0)
        def _():
            dw_ref[...] = jnp.zeros_like(dw_ref)
            db_ref[...] = jnp.zeros_like(db_ref)

        wts, bias = _conv_weights(w_ref), b_ref[...]
        fold = lambda v: v.reshape(rc // 8, 8, cb).sum(axis=0)
        db8 = jnp.zeros((8, cb), F32)
        dw8 = [jnp.zeros((8, cb), F32) for _ in range(CONV_WIDTH)]
        for t0 in range(0, s, rc):
            taps = _conv_taps(x_ref, t0, rc)
            acc = _conv_pre(taps, wts, bias)
            sg = _sigmoid(acc)
            dacc = dy_ref[t0:t0 + rc, :] * sg * (1.0 + acc * (1.0 - sg))
            dacc_ref[t0:t0 + rc, :] = dacc
            db8 = db8 + fold(dacc)
            dw8 = [dw8[j] + fold(dacc * taps[j]) for j in range(CONV_WIDTH)]
        db_ref[...] += jnp.sum(db8, axis=0, keepdims=True)
        for j in range(CONV_WIDTH):
            dw_ref[0, CONV_WIDTH - 1 - j:CONV_WIDTH - j, :] += jnp.sum(dw8[j], axis=0, keepdims=True)
        for t0 in range(0, s, rc):
            if t0 + rc < s:
                n = rc + CONV_HALO
                win = dacc_ref[t0:t0 + n, :]
                ups = [win[:rc]] + [pltpu.roll(win, n - j, 0)[:rc] for j in range(1, CONV_WIDTH)]
            else:
                cur = dacc_ref[t0:t0 + rc, :]
                row = lax.broadcasted_iota(jnp.int32, cur.shape, 0)
                ups = [cur] + [_shift_up(cur, j, row) for j in range(1, CONV_WIDTH)]
            dx = ups[0] * wts[0]
            for j in range(1, CONV_WIDTH):
                dx = dx + ups[j] * wts[j]
            dx_ref[t0:t0 + rc, :] = dx.astype(BF16)

    return pl.pallas_call(
        body, name="conv_bwd", grid=(n_blk, nb),
        in_specs=[pl.BlockSpec((s, cb), lambda j, b: (b, blk0 + j)), pl.BlockSpec((s, cb), lambda j, b: (b, j)),
                  pl.BlockSpec(memory_space=pl.ANY), w_spec, pl.BlockSpec((1, cb), lambda j, b: (0, j))],
        out_specs=[pl.BlockSpec((s, cb), lambda j, b: (b, blk0 + j)), w_spec, pl.BlockSpec((1, cb), lambda j, b: (0, j))],
        out_shape=[jax.ShapeDtypeStruct(dproj.shape, dproj.dtype), jax.ShapeDtypeStruct(conv_w.shape, F32),
                   jax.ShapeDtypeStruct(conv_b.shape, F32)],
        scratch_shapes=[pltpu.VMEM((s, cb), F32)],
        input_output_aliases={2: 0},
        compiler_params=_params(("parallel", "arbitrary")))(proj, dxbc, dproj, conv_w, conv_b)


def _local_step(x, p, tgt, wg, small, rest_weights, early_grads, w_in_grad):
    nb, s, d = x.shape
    t = nb * s
    gn_w = SSM_GROUPS * D_STATE
    dc = N_SHARD * wg["conv_w"].shape[2]
    di = dc - 2 * gn_w
    nh = di // HEAD_DIM
    pgd = d // N_POOL
    x2 = x.reshape(t, d)
    p2 = p.reshape(t, p.shape[-1])
    tgt2 = tgt.reshape(t, d)

    w_main, w_dt = _regroup_w_in(wg["w_in"], d, di, dc, nh)
    c_g, c_z, c_xbc, c_uz = 0, 2 * d, 2 * d + di, 2 * d + di + dc
    n_main = w_main.shape[1]

    pad_h = lambda v: jnp.pad(v.reshape(1, nh).astype(F32), ((0, 0), (0, LANES - nh)))
    dtb, alog = pad_h(small["dt_bias"]), pad_h(small["a_log"])
    dskx = jnp.repeat(small["d_skip"].reshape(1, nh).astype(F32), HEAD_DIM, axis=1)
    vec = lambda v: v.reshape(1, -1).astype(F32)
    norm_g, gn, conv_b = vec(small["norm_g"]), vec(small["gnorm_g"]), vec(small["conv_b"])
    mix_b, scale = vec(small["pool_mix_b"]), vec(small["pool_scale"])
    ple_g, final_g = vec(small["ple_norm_g"]), vec(small["final_g"])
    conv_w = wg["conv_w"]

    wide = _tile(n_main, 2304, LANES)
    proj, dtraw, h = _inproj(x2, norm_g, w_main, w_dt, tn=wide)
    xbc = _conv_fwd(proj, conv_w, conv_b, nb, s, c_xbc)
    ypre, yan, hp = _ssd_fwd(proj, xbc, dtraw, dtb, alog, dskx, gn, nb, s, di, c_z)
    wr = rest_weights(yan)
    mix_w = wr["pool_mix_w"].reshape(N_SHARD, N_POOL, pgd // N_SHARD, pgd)
    rows = lambda v: v.reshape(-1, v.shape[-1])
    wa, wb, wo, wpg = rows(wr["w_branch_a"]), rows(wr["w_branch_b"]), rows(wr["w_out"]), rows(wr["w_ple_gate"])
    wup = wr["w_ple_up"]
    ybp = _pool_fwd(proj, mix_w, mix_b, scale, nb, s, c_uz)
    ya = _mm(yan, wa, "branch_a")
    yb = _mm(ybp, wb, "branch_b")
    merged, hn, dpre, dpu, x1, dx2, loss, d_final_g = _mid_fwd(
        ya, yb, proj, c_g, x2, p2, tgt2, wo, wpg, wup, ple_g, final_g)

    d_wpg = _mm_tn(hn, dpre, "d_w_ple_gate")
    d_wup = _mm_tn(p2, dpu, "d_w_ple_up", tn=wup.shape[-1], col_blocks=True)
    dx1, dya, dyb, dproj, d_ple_g = _mid_bwd(dpre, dx2, x1, ya, yb, proj, c_g, wpg, wo, ple_g, n_main)
    d_wo = _mm_tn(merged, dx1, "d_w_out")
    d_wa = _mm_tn(yan, dya, "d_w_branch_a", tk=1024)
    d_wb = _mm_tn(ybp, dyb, "d_w_branch_b")
    dyan = _mm_nt(dya, wa, "d_y_a")
    dybp = _mm_nt(dyb, wb, "d_y_b")
    dproj, d_mix_w, d_mix_b, d_scale = _pool_bwd(proj, dybp, dproj, mix_w, mix_b, scale, nb, s, c_uz)
    shard_major = lambda v: v.reshape(N_SHARD, v.shape[0] // N_SHARD, v.shape[1])
    early = dict(pool_mix_w=d_mix_w.reshape(N_SHARD, pgd, pgd), w_branch_a=shard_major(d_wa),
                 w_branch_b=shard_major(d_wb), w_out=shard_major(d_wo), w_ple_gate=shard_major(d_wpg),
                 w_ple_up=d_wup)
    token = early_grads(early)
    dproj, ddt, dxbc, d_gn, d_dsk, d_alog, d_dtb = _ssd_bwd(
        dyan, ypre, proj, xbc, dtraw, hp, dproj, dtb + token[0:1, 0:1], alog, dskx, gn, nb, s, di, c_z)
    dproj, d_conv_w, d_conv_b = _conv_bwd(proj, dxbc, dproj, conv_w, conv_b, nb, s, c_xbc)
    d_wmain = _mm_tn(h, dproj, "d_w_in", tk=2048)
    d_wdt = _mm_tn(h, ddt, "d_w_dt")
    d_w_in = _ungroup_w_in(d_wmain, d_wdt, d, di, dc, nh)
    token = w_in_grad(d_w_in)
    gx, d_norm_g = _in_bwd(dproj, w_main, ddt, w_dt, x2, dx1, norm_g, token, tk=_tile(n_main, 1536, LANES))

    grads = dict(norm_g=d_norm_g, w_in=d_w_in, conv_w=d_conv_w, conv_b=d_conv_b, dt_bias=d_dtb[:, :nh],
                 a_log=d_alog[:, :nh], d_skip=d_dsk[:, :nh], gnorm_g=d_gn, pool_mix_b=d_mix_b, pool_scale=d_scale,
                 ple_norm_g=d_ple_g, final_g=d_final_g, **early)
    return loss[0, 0], gx.reshape(nb, s, d), grads


def _place():
    return lax.axis_index("x"), lax.axis_index("y"), lax.axis_index("c")


def _other_chips(x, y):
    return [(1 - x, y), (x, 1 - y), (1 - x, 1 - y)]


def _halves(c, rows, align):
    rh = rows // 2
    assert rows % 2 == 0 and rh % align == 0, (rows, align)
    return (pl.ds(pl.multiple_of(c * rh, align), rh), pl.ds(pl.multiple_of((1 - c) * rh, align), rh))


HBM = pl.BlockSpec(memory_space=pl.ANY)


def _into_slot(w2, k, dtype, name):
    rows, cols = w2.shape
    rb = _tile(rows, 256)

    def body(k_ref, w_ref, o_ref):
        o_ref[0] = w_ref[...].astype(dtype)

    return pl.pallas_call(
        body, name=name,
        grid_spec=pltpu.PrefetchScalarGridSpec(
            num_scalar_prefetch=1, grid=(rows // rb,),
            in_specs=[pl.BlockSpec((rb, cols), lambda i, k_ref: (i, 0))],
            out_specs=pl.BlockSpec((1, rb, cols), lambda i, k_ref: (k_ref[0], i, 0))),
        out_shape=jax.ShapeDtypeStruct((N_SHARD, rows, cols), dtype),
        compiler_params=_params(("parallel",)))(k.reshape(1), w2)


def _gather_weights(split, whole):
    n_split, n_all = len(split), len(split) + len(whole)

    def body(*refs):
        bufs = refs[n_all:2 * n_all]
        send_sems, recv_sems = refs[2 * n_all:]
        x, y, c = _place()
        k, k_x, k_y, k_d = 2 * x + y, 2 * (1 - x) + y, 2 * x + (1 - y), 2 * (1 - x) + (1 - y)
        x_nb, y_nb, sib = (1 - x, y, c), (x, 1 - y, c), (x, y, 1 - c)

        def copy(idx, block, to):
            return pltpu.make_async_remote_copy(src_ref=block, dst_ref=block, send_sem=send_sems.at[idx],
                                                recv_sem=recv_sems.at[idx], device_id=to, device_id_type=MESH)

        started = []

        def start(idx, block, to):
            started.append(copy(idx, block, to))
            started[-1].start()

        for i in range(n_split):
            buf, s0 = bufs[i], 8 * i
            rh = buf.shape[1] // 2
            rq = rh // 2
            assert buf.shape[1] == 4 * rq and rq % 16 == 0, buf.shape

            def rows(core, part):
                lo = core * rh + (rq if part == "bottom" else 0)
                return pl.ds(pl.multiple_of(lo, 16), rh if part == "all" else rq)

            start(s0 + 0, buf.at[k, rows(c, "all")], x_nb)
            start(s0 + 1, buf.at[k, rows(c, "all")], y_nb)
            copy(s0 + 0, buf.at[k_x, rows(c, "all")], x_nb).wait_recv()
            start(s0 + 2, buf.at[k_x, rows(c, "top")], y_nb)
            start(s0 + 4, buf.at[k_x, rows(c, "all")], sib)
            copy(s0 + 1, buf.at[k_y, rows(c, "all")], y_nb).wait_recv()
            start(s0 + 3, buf.at[k_y, rows(c, "bottom")], x_nb)
            start(s0 + 5, buf.at[k_y, rows(c, "all")], sib)
            copy(s0 + 2, buf.at[k_d, rows(c, "top")], y_nb).wait_recv()
            start(s0 + 6, buf.at[k_d, rows(c, "top")], sib)
            copy(s0 + 3, buf.at[k_d, rows(c, "bottom")], x_nb).wait_recv()
            start(s0 + 7, buf.at[k_d, rows(c, "bottom")], sib)
            copy(s0 + 4, buf.at[k_x, rows(1 - c, "all")], sib).wait_recv()
            copy(s0 + 5, buf.at[k_y, rows(1 - c, "all")], sib).wait_recv()
            copy(s0 + 6, buf.at[k_d, rows(1 - c, "top")], sib).wait_recv()
            copy(s0 + 7, buf.at[k_d, rows(1 - c, "bottom")], sib).wait_recv()
        for i in range(n_split, n_all):
            s0 = 8 * n_split + 3 * (i - n_split)
            for j, (px, py) in enumerate(_other_chips(x, y)):
                start(s0 + j, bufs[i].at[k], (px, py, c))
            for j, (px, py) in enumerate(_other_chips(x, y)):
                copy(s0 + j, bufs[i].at[2 * px + py], (px, py, c)).wait_recv()
        for cp in started:
            cp.wait_send()

    arrays = list(split) + list(whole)
    n_sem = 8 * n_split + 3 * len(whole)
    return pl.pallas_call(
        body, name="gather_weights",
        in_specs=[HBM] * n_all, out_specs=[HBM] * n_all,
        out_shape=[jax.ShapeDtypeStruct(a.shape, a.dtype) for a in arrays],
        input_output_aliases={i: i for i in range(n_all)},
        scratch_shapes=[pltpu.SemaphoreType.DMA((n_sem,)), pltpu.SemaphoreType.DMA((n_sem,))],
    )(*arrays)


def _swap_halves(gs):
    n = len(gs)

    def body(*refs):
        ins, outs, send_sems, recv_sems = refs[:n], refs[n:2 * n], refs[2 * n], refs[2 * n + 1]
        x, y, c = _place()
        copies = []
        for i in range(n):
            _, theirs = _halves(c, gs[i].shape[1], 8)
            cp = pltpu.make_async_remote_copy(src_ref=ins[i].at[:, theirs], dst_ref=outs[i], send_sem=send_sems.at[i],
                                              recv_sem=recv_sems.at[i], device_id=(x, y, 1 - c), device_id_type=MESH)
            cp.start()
            copies.append(cp)
        for cp in copies:
            cp.wait()

    return pl.pallas_call(
        body, name="swap_halves", in_specs=[HBM] * n, out_specs=[HBM] * n,
        out_shape=[jax.ShapeDtypeStruct((g.shape[0], g.shape[1] // 2, g.shape[2]), g.dtype) for g in gs],
        scratch_shapes=[pltpu.SemaphoreType.DMA((n,)), pltpu.SemaphoreType.DMA((n,))],
    )(*gs)


def _join_halves(vs):
    n = len(vs)

    def body(*refs):
        bufs, send_sems, recv_sems = refs[n:2 * n], refs[2 * n], refs[2 * n + 1]
        x, y, c = _place()

        def copy(i, rows):
            return pltpu.make_async_remote_copy(src_ref=bufs[i].at[rows], dst_ref=bufs[i].at[rows],
                                                send_sem=send_sems.at[i], recv_sem=recv_sems.at[i],
                                                device_id=(x, y, 1 - c), device_id_type=MESH)

        halves = [_halves(c, bufs[i].shape[0], 8) for i in range(n)]
        sends = [copy(i, halves[i][0]) for i in range(n)]
        for cp in sends:
            cp.start()
        for i in range(n):
            copy(i, halves[i][1]).wait_recv()
        for cp in sends:
            cp.wait_send()

    return pl.pallas_call(
        body, name="join_halves", in_specs=[HBM] * n, out_specs=[HBM] * n,
        out_shape=[jax.ShapeDtypeStruct(v.shape, v.dtype) for v in vs],
        input_output_aliases={i: i for i in range(n)},
        scratch_shapes=[pltpu.SemaphoreType.DMA((n,)), pltpu.SemaphoreType.DMA((n,))],
    )(*vs)


SEM = pl.BlockSpec(memory_space=pltpu.SEMAPHORE)
IN_HBM = pl.BlockSpec(memory_space=pltpu.HBM)
SPLIT_EFFECT = pltpu.SideEffectType.DATAFLOW_SIDE_EFFECTING


def _split_copies(plan, refs, send_sems, recv_sems):
    pairs = []
    for idx, (src, dst, landing, to) in enumerate(plan(refs)):
        mk = lambda d: pltpu.make_async_remote_copy(src_ref=src, dst_ref=d, send_sem=send_sems.at[idx],
                                                    recv_sem=recv_sems.at[idx], device_id=to, device_id_type=MESH)
        pairs.append((mk(dst), mk(landing)))
    return pairs


def _split_start(name, bufs, after, plan, n_copies):
    n = len(bufs)

    def body(*refs):
        send_sems, recv_sems, token = refs[n + 1], refs[n + 2], refs[-1]
        for send, _ in _split_copies(plan, refs[:n], send_sems, recv_sems):
            send.start()
        token[...] = jnp.zeros_like(token)

    sems = pltpu.SemaphoreType.DMA((n_copies,))
    out = pl.pallas_call(
        body, name=name,
        in_specs=[IN_HBM] * n + [HBM],
        out_specs=[SEM, SEM] + [IN_HBM] * n + [pl.BlockSpec(memory_space=pltpu.VMEM)],
        out_shape=[sems, sems] + [pltpu.HBM(b.shape, b.dtype) for b in bufs] + [jax.ShapeDtypeStruct((8, LANES), F32)],
        input_output_aliases={i: 2 + i for i in range(n)},
        compiler_params=pltpu.CompilerParams(has_side_effects=SPLIT_EFFECT),
    )(*[pltpu.with_memory_space_constraint(b, pltpu.HBM) for b in bufs], after)
    return out[0], out[1], out[2:2 + n], out[-1]


def _split_wait(name, bufs, send_sems, recv_sems, after, plan):
    n = len(bufs)

    def body(*refs):
        for send, recv in _split_copies(plan, refs[:n], refs[n], refs[n + 1]):
            send.wait_send()
            recv.wait_recv()

    return pl.pallas_call(
        body, name=name,
        in_specs=[IN_HBM] * n + [SEM, SEM, HBM],
        out_specs=[IN_HBM] * n,
        out_shape=[pltpu.HBM(b.shape, b.dtype) for b in bufs],
        input_output_aliases={i: i for i in range(n)},
        compiler_params=pltpu.CompilerParams(has_side_effects=SPLIT_EFFECT),
    )(*bufs, send_sems, recv_sems, after)


def _gather_plan(n):
    def plan(refs):
        x, y, c = _place()
        k = 2 * x + y
        return [(refs[i].at[k], refs[i].at[k], refs[i].at[2 * px + py], (px, py, c))
                for i in range(n) for px, py in _other_chips(x, y)]
    return plan


def _scatter_plan(n):
    def plan(refs):
        x, y, c = _place()
        return [(refs[i].at[2 * px + py], refs[n + i].at[j], refs[n + i].at[j], (px, py, c))
                for i in range(n) for j, (px, py) in enumerate(_other_chips(x, y))]
    return plan


def _swap_sibling(vs):
    n = len(vs)

    def body(*refs):
        ins, outs, send_sems, recv_sems = refs[:n], refs[n:2 * n], refs[2 * n], refs[2 * n + 1]
        x, y, c = _place()
        copies = [pltpu.make_async_remote_copy(src_ref=ins[i], dst_ref=outs[i], send_sem=send_sems.at[i],
                                               recv_sem=recv_sems.at[i], device_id=(x, y, 1 - c), device_id_type=MESH)
                  for i in range(n)]
        for cp in copies:
            cp.start()
        for cp in copies:
            cp.wait()

    return pl.pallas_call(
        body, name="swap_sibling", in_specs=[HBM] * n, out_specs=[HBM] * n,
        out_shape=[jax.ShapeDtypeStruct(v.shape, v.dtype) for v in vs],
        scratch_shapes=[pltpu.SemaphoreType.DMA((n,)), pltpu.SemaphoreType.DMA((n,))],
    )(*vs)


def _to_bf16(g, name):
    _, rows, cols = g.shape
    rb = _tile(rows, 256)

    def body(g_ref, o_ref):
        o_ref[...] = g_ref[...].astype(BF16)

    spec = pl.BlockSpec((1, rb, cols), lambda j, i: (j, i, 0))
    return pl.pallas_call(
        body, name=name, grid=(N_SHARD, rows // rb), in_specs=[spec], out_specs=spec,
        out_shape=jax.ShapeDtypeStruct(g.shape, BF16),
        compiler_params=_params(("parallel", "parallel")))(g)


def _add_landed(g, landed, k, name):
    _, rows, cols = g.shape
    rb = _tile(rows, 256)

    def body(k_ref, g_ref, l_ref, o_ref):
        o_ref[...] = g_ref[0] + l_ref[0].astype(F32) + l_ref[1].astype(F32) + l_ref[2].astype(F32)

    return pl.pallas_call(
        body, name=name,
        grid_spec=pltpu.PrefetchScalarGridSpec(
            num_scalar_prefetch=1, grid=(rows // rb,),
            in_specs=[pl.BlockSpec((1, rb, cols), lambda i, k_ref: (k_ref[0], i, 0)),
                      pl.BlockSpec((N_SHARD - 1, rb, cols), lambda i, k_ref: (0, i, 0))],
            out_specs=pl.BlockSpec((rb, cols), lambda i, k_ref: (i, 0))),
        out_shape=jax.ShapeDtypeStruct((rows, cols), F32),
        compiler_params=_params(("parallel",)))(k.reshape(1), g, landed)


def _allreduce_small(v):
    rows = v.shape[0]

    def body(v_ref, o_ref, buf_ref, send_sems, recv_sems):
        x, y, c = _place()
        me = 4 * x + 2 * y + c
        buf_ref[me] = v_ref[...]
        copies = []
        for rel in range(1, 8):
            peer = (x ^ (rel >> 2), y ^ ((rel >> 1) & 1), c ^ (rel & 1))
            cp = pltpu.make_async_remote_copy(src_ref=v_ref, dst_ref=buf_ref.at[me], send_sem=send_sems.at[rel - 1],
                                              recv_sem=recv_sems.at[rel - 1], device_id=peer, device_id_type=MESH)
            cp.start()
            copies.append(cp)
        for rel in range(1, 8):
            peer_id = me ^ rel
            pltpu.make_async_remote_copy(src_ref=v_ref, dst_ref=buf_ref.at[peer_id], send_sem=send_sems.at[rel - 1],
                                         recv_sem=recv_sems.at[rel - 1], device_id=(x, y, c),
                                         device_id_type=MESH).wait_recv()
        for cp in copies:
            cp.wait_send()
        acc = buf_ref[0]
        for i in range(1, 8):
            acc = acc + buf_ref[i]
        o_ref[...] = acc

    return pl.pallas_call(
        body, name="allreduce_small",
        in_specs=[pl.BlockSpec(memory_space=pltpu.VMEM)], out_specs=pl.BlockSpec(memory_space=pltpu.VMEM),
        out_shape=jax.ShapeDtypeStruct(v.shape, F32),
        scratch_shapes=[pltpu.VMEM((8, rows, LANES), F32), pltpu.SemaphoreType.DMA((7,)), pltpu.SemaphoreType.DMA((7,))],
    )(v)


def _add_pair(g, got, c, name):
    _, rh, cols = got.shape
    rb = _tile(rh, 256)
    nrb = rh // rb

    def body(c_ref, g_ref, got_ref, o_ref):
        o_ref[...] = (g_ref[...] + got_ref[...]).astype(BF16)

    spec = pl.BlockSpec((1, rb, cols), lambda j, i, c_ref: (j, i, 0))
    return pl.pallas_call(
        body, name=name,
        grid_spec=pltpu.PrefetchScalarGridSpec(
            num_scalar_prefetch=1, grid=(N_SHARD, nrb),
            in_specs=[pl.BlockSpec((1, rb, cols), lambda j, i, c_ref: (j, c_ref[0] * nrb + i, 0)), spec],
            out_specs=spec),
        out_shape=jax.ShapeDtypeStruct(got.shape, BF16),
        compiler_params=_params(("parallel", "parallel")))(c.reshape(1), g, got)


def _add_chips(g, got, landed, k, c, name):
    _, rh, cols = got.shape
    rb = _tile(rh, 256)
    nrb = rh // rb

    def body(kc_ref, g_ref, got_ref, l_ref, o_ref):
        own = g_ref[0] + got_ref[0]
        o_ref[...] = own + l_ref[0].astype(F32) + l_ref[1].astype(F32) + l_ref[2].astype(F32)

    half_c = lambda i, kc: (kc[1] * nrb + i, 0)
    return pl.pallas_call(
        body, name=name,
        grid_spec=pltpu.PrefetchScalarGridSpec(
            num_scalar_prefetch=1, grid=(nrb,),
            in_specs=[pl.BlockSpec((1, rb, cols), lambda i, kc: (kc[0],) + half_c(i, kc)),
                      pl.BlockSpec((1, rb, cols), lambda i, kc: (kc[0], i, 0)),
                      pl.BlockSpec((N_SHARD - 1, rb, cols), lambda i, kc: (0, i, 0))],
            out_specs=pl.BlockSpec((rb, cols), half_c)),
        out_shape=jax.ShapeDtypeStruct((2 * rh, cols), F32),
        compiler_params=_params(("parallel",)))(jnp.stack([k, c]), g, got, landed)


def _adamw(wv, gs, m, v, name):
    rows, cols = wv.shape
    rb = _tile(rows, 256)
    c1 = 1.0 - ADAM_B1 ** ADAM_STEP
    c2 = 1.0 - ADAM_B2 ** ADAM_STEP
    n_g = len(gs)

    def body(*refs):
        w_ref, g_refs, (m_ref, v_ref, go_ref, d_ref, nm_ref, nv_ref) = refs[0], refs[1:1 + n_g], refs[1 + n_g:]
        gv = g_refs[0][...]
        for ref in g_refs[1:]:
            gv = gv + ref[...]
        go_ref[...] = gv
        nm = ADAM_B1 * m_ref[...] + (1.0 - ADAM_B1) * gv
        nv = ADAM_B2 * v_ref[...] + (1.0 - ADAM_B2) * (gv * gv)
        nm_ref[...] = nm
        nv_ref[...] = nv
        d_ref[...] = -ADAM_LR * ((nm / c1) / (jnp.sqrt(nv / c2) + ADAM_EPS) + ADAM_WD * w_ref[...])

    spec = pl.BlockSpec((rb, cols), lambda i: (i, 0))
    return pl.pallas_call(
        body, name=name, grid=(rows // rb,), in_specs=[spec] * (3 + n_g), out_specs=[spec] * 4,
        out_shape=[jax.ShapeDtypeStruct((rows, cols), F32)] * 4,
        compiler_params=_params(("parallel",)))(wv, *gs, m, v)


def _pack(flats):
    cat = jnp.concatenate([f.reshape(-1) for f in flats])
    n = cat.shape[0]
    rows = -(-n // (8 * LANES)) * 8
    return jnp.pad(cat, (0, rows * LANES - n)).reshape(rows, LANES)


def _unpack(packed, shapes):
    flat = packed.reshape(-1)
    out, off = [], 0
    for shp in shapes:
        n = 1
        for dim in shp:
            n *= dim
        out.append(flat[off:off + n].reshape(shp))
        off += n
    return out


def kernel(x, p, norm_g, w_in, conv_w, conv_b, dt_bias, a_log, d_skip, gnorm_g, pool_mix_w, pool_mix_b, pool_scale, w_branch_a, w_branch_b, w_out, ple_norm_g, w_ple_gate, w_ple_up, final_g, loss_target, m_norm_g, m_w_in, m_conv_w, m_conv_b, m_dt_bias, m_a_log, m_d_skip, m_gnorm_g, m_pool_mix_w, m_pool_mix_b, m_pool_scale, m_w_branch_a, m_w_branch_b, m_w_out, m_ple_norm_g, m_w_ple_gate, m_w_ple_up, m_final_g, v_norm_g, v_w_in, v_conv_w, v_conv_b, v_dt_bias, v_a_log, v_d_skip, v_gnorm_g, v_pool_mix_w, v_pool_mix_b, v_pool_scale, v_w_branch_a, v_w_branch_b, v_w_out, v_ple_norm_g, v_w_ple_gate, v_w_ple_up, v_final_g):
    wts = dict(norm_g=norm_g, w_in=w_in, conv_w=conv_w, conv_b=conv_b, dt_bias=dt_bias, a_log=a_log, d_skip=d_skip,
               gnorm_g=gnorm_g, pool_mix_w=pool_mix_w, pool_mix_b=pool_mix_b, pool_scale=pool_scale,
               w_branch_a=w_branch_a, w_branch_b=w_branch_b, w_out=w_out, ple_norm_g=ple_norm_g,
               w_ple_gate=w_ple_gate, w_ple_up=w_ple_up, final_g=final_g)
    mom_m = dict(norm_g=m_norm_g, w_in=m_w_in, conv_w=m_conv_w, conv_b=m_conv_b, dt_bias=m_dt_bias, a_log=m_a_log,
                 d_skip=m_d_skip, gnorm_g=m_gnorm_g, pool_mix_w=m_pool_mix_w, pool_mix_b=m_pool_mix_b,
                 pool_scale=m_pool_scale, w_branch_a=m_w_branch_a, w_branch_b=m_w_branch_b, w_out=m_w_out,
                 ple_norm_g=m_ple_norm_g, w_ple_gate=m_w_ple_gate, w_ple_up=m_w_ple_up, final_g=m_final_g)
    mom_v = dict(norm_g=v_norm_g, w_in=v_w_in, conv_w=v_conv_w, conv_b=v_conv_b, dt_bias=v_dt_bias, a_log=v_a_log,
                 d_skip=v_d_skip, gnorm_g=v_gnorm_g, pool_mix_w=v_pool_mix_w, pool_mix_b=v_pool_mix_b,
                 pool_scale=v_pool_scale, w_branch_a=v_w_branch_a, w_branch_b=v_w_branch_b, w_out=v_w_out,
                 ple_norm_g=v_ple_norm_g, w_ple_gate=v_w_ple_gate, w_ple_up=v_w_ple_up, final_g=v_final_g)
    c = lax.axis_index("c")
    k = 2 * lax.axis_index("x") + lax.axis_index("y")
    flat2 = lambda a: a.reshape(-1, a.shape[-1])

    slots = {n: _into_slot(flat2(wts[n]), k, BF16, "slot_" + n) for n in BIG}
    w_in_g, conv_g = _gather_weights([slots["w_in"]], [_into_slot(flat2(conv_w), k, F32, "slot_conv_w")])
    n_rest = len(REST)
    gsend, grecv, gbufs, gtoken = _split_start("gather_rest_start", [slots[n] for n in REST], conv_g,
                                               _gather_plan(n_rest), 3 * n_rest)

    def rest_weights(after):
        return dict(zip(REST, _split_wait("gather_rest_wait", gbufs, gsend, grecv, after, _gather_plan(n_rest))))

    flying = {}

    def early_grads(early):
        sends = [_to_bf16(early[n], "bf16_" + n) for n in REST]
        lands = [pltpu.with_memory_space_constraint(lax.empty((N_SHARD - 1,) + v.shape[1:], BF16), pltpu.HBM)
                 for v in sends]
        ssend, srecv, sbufs, stoken = _split_start("scatter_rest_start", sends + lands, early[REST[0]],
                                                   _scatter_plan(n_rest), 3 * n_rest)
        flying.update(send=ssend, recv=srecv, bufs=sbufs)
        return stoken

    def w_in_grad(g_w_in):
        got = _swap_halves([g_w_in])[0]
        pair = _add_pair(g_w_in, got, c, "add_pair_w_in")
        land = pltpu.with_memory_space_constraint(lax.empty((N_SHARD - 1,) + pair.shape[1:], BF16), pltpu.HBM)
        wsend, wrecv, wbufs, wtoken = _split_start("scatter_w_in_start", [pair, land], got, _scatter_plan(1), 3)
        flying.update(w_send=wsend, w_recv=wrecv, w_bufs=wbufs, w_got=got)
        return wtoken

    small = {n: wts[n] for n in SMALL}
    small["norm_g"] = norm_g + gtoken[0, 0]
    loss, grad_x, grads = _local_step(x, p[0], loss_target, dict(w_in=w_in_g, conv_w=conv_g), small,
                                      rest_weights, early_grads, w_in_grad)
    g_w_in = grads["w_in"]
    landed = _split_wait("scatter_w_in_wait", flying["w_bufs"], flying["w_send"], flying["w_recv"], grad_x,
                         _scatter_plan(1))[1]
    w_in_sum = _join_halves([_add_chips(g_w_in, flying["w_got"], landed, k, c, "add_chips_w_in")])[0]

    sbufs = _split_wait("scatter_rest_wait", flying["bufs"], flying["send"], flying["recv"], g_w_in,
                        _scatter_plan(n_rest))
    mine = [_add_landed(grads[n], ld, k, "add_landed_" + n) for n, ld in zip(REST, sbufs[n_rest:])]
    theirs = _swap_sibling(mine)
    g_sums = dict(zip(REST, zip(mine, theirs)))
    g_sums["w_in"] = (w_in_sum,)

    conv_shape = flat2(conv_w).shape
    small_sum = _allreduce_small(_pack([grads[n] for n in SMALL] + [grads["conv_w"], loss]))
    small_shapes = [wts[n].shape for n in SMALL] + [(N_SHARD,) + conv_shape, (1,)]
    small_g = _unpack(small_sum, small_shapes)
    g_conv = lax.dynamic_index_in_dim(small_g[-2], k, axis=0, keepdims=False)

    outs = {}
    for n in BIG:
        vals = _adamw(flat2(wts[n]), g_sums[n], flat2(mom_m[n]), flat2(mom_v[n]), "adamw_" + n)
        for kind, val in zip(("grad", "delta", "new_m", "new_v"), vals):
            outs[kind, n] = val.reshape(wts[n].shape)
    names = SMALL + ("conv_w",)
    sm = _adamw(_pack([wts[n] for n in names]), (_pack(small_g[:len(SMALL)] + [g_conv]),),
                _pack([mom_m[n] for n in names]), _pack([mom_v[n] for n in names]), "adamw_small")
    sm_shapes = [wts[n].shape for n in names]
    for kind, val in zip(("grad", "delta", "new_m", "new_v"), sm):
        for n, piece in zip(names, _unpack(val, sm_shapes)):
            outs[kind, n] = piece
    return (small_g[-1][0], grad_x, *[outs[kind, n] for kind in ("grad", "delta", "new_m", "new_v") for n in WEIGHTS])
```

```python
import functools

import jax
import jax.numpy as jnp
from jax import lax
from jax.experimental import pallas as pl
from jax.experimental.pallas import tpu as pltpu

F32 = jnp.float32
BF16 = jnp.bfloat16
HIGHEST = lax.Precision.HIGHEST
MESH = pl.DeviceIdType.MESH

EPS = 1e-6
HEAD_DIM = 64
SSM_GROUPS = 4
D_STATE = 128
CONV_WIDTH = 4
CHUNK = 128
N_POOL = 4
LANES = 128
N_SHARD = 4

ADAM_LR = 0.001
ADAM_B1 = 0.9
ADAM_B2 = 0.999
ADAM_EPS = 1e-08
ADAM_WD = 0.01
ADAM_STEP = 10

BIG = ("w_in", "pool_mix_w", "w_branch_a", "w_branch_b", "w_out", "w_ple_gate", "w_ple_up")
REST = BIG[1:]
SMALL = ("norm_g", "conv_b", "dt_bias", "a_log", "d_skip", "gnorm_g", "pool_mix_b", "pool_scale",
         "ple_norm_g", "final_g")
WEIGHTS = ("norm_g", "w_in", "conv_w", "conv_b", "dt_bias", "a_log", "d_skip", "gnorm_g", "pool_mix_w",
           "pool_mix_b", "pool_scale", "w_branch_a", "w_branch_b", "w_out", "ple_norm_g", "w_ple_gate",
           "w_ple_up", "final_g")


def _params(sem=None, vmem_mb=56):
    kw = dict(vmem_limit_bytes=vmem_mb << 20)
    if sem is not None:
        kw["dimension_semantics"] = sem
    return pltpu.CompilerParams(**kw)


def _sigmoid(v):
    return 0.5 * jnp.tanh(0.5 * v) + 0.5


def _sigmoid_tail(v):
    return 1.0 / (1.0 + jnp.exp(-v))


def _softplus(v):
    return jnp.maximum(v, 0.0) + jnp.log1p(jnp.exp(-jnp.abs(v)))


def _bdot(a, b):
    return jnp.dot(a.astype(BF16), b.astype(BF16), preferred_element_type=F32)


def _bdot_nt(a, b):
    return lax.dot_general(a.astype(BF16), b.astype(BF16), (((1,), (1,)), ((), ())), preferred_element_type=F32)


def _bdot_tn(a, b):
    return lax.dot_general(a.astype(BF16), b.astype(BF16), (((0,), (0,)), ((), ())), preferred_element_type=F32)


def _col_block(col0, width):
    assert col0 % width == 0, (col0, width)
    return col0 // width


def _tile(n, cap, unit=8):
    if n <= cap:
        return n
    best = None
    for cand in range(unit, cap + 1, unit):
        if n % cand == 0:
            best = cand
    assert best is not None, (n, cap)
    return best


def _shift_down(v, j, row):
    return jnp.where(row >= j, pltpu.roll(v, j, 0), 0.0)


def _shift_up(v, j, row):
    n = v.shape[0]
    return jnp.where(row < n - j, pltpu.roll(v, n - j, 0), 0.0)


def _mm(a, w, name, tm=1024, tn=1024):
    t, k = a.shape
    n = w.shape[1]
    tm, tn = min(tm, t), min(tn, n)

    def body(a_ref, w_ref, o_ref):
        o_ref[...] = _bdot(a_ref[...], w_ref[...]).astype(BF16)

    return pl.pallas_call(
        body, name=name, grid=(t // tm, n // tn),
        in_specs=[pl.BlockSpec((tm, k), lambda i, j: (i, 0)), pl.BlockSpec((k, tn), lambda i, j: (0, j))],
        out_specs=pl.BlockSpec((tm, tn), lambda i, j: (i, j)),
        out_shape=jax.ShapeDtypeStruct((t, n), BF16),
        compiler_params=_params(("parallel", "parallel")))(a, w)


def _mm_nt(a, w, name, tm=1024, tk=1024):
    t, k = a.shape
    n = w.shape[0]
    tm, tk = min(tm, t), min(tk, k)

    def body(a_ref, w_ref, o_ref):
        kk = pl.program_id(1)
        part = _bdot_nt(a_ref[...], w_ref[...])

        @pl.when(kk == 0)
        def _():
            o_ref[...] = part

        @pl.when(kk > 0)
        def _():
            o_ref[...] += part

    return pl.pallas_call(
        body, name=name, grid=(t // tm, k // tk),
        in_specs=[pl.BlockSpec((tm, tk), lambda i, j: (i, j)), pl.BlockSpec((n, tk), lambda i, j: (0, j))],
        out_specs=pl.BlockSpec((tm, n), lambda i, j: (i, 0)),
        out_shape=jax.ShapeDtypeStruct((t, n), F32),
        compiler_params=_params(("parallel", "arbitrary")))(a, w)


def _mm_tn(a, b, name, tn=1024, tk=2048, col_blocks=False):
    t, m = a.shape
    n = b.shape[1]
    tn, tk = min(tn, n), min(tk, t)

    def body(a_ref, b_ref, o_ref):
        kk = pl.program_id(1)
        part = _bdot_tn(a_ref[...], b_ref[...])
        part = part[None] if col_blocks else part

        @pl.when(kk == 0)
        def _():
            o_ref[...] = part

        @pl.when(kk > 0)
        def _():
            o_ref[...] += part

    if col_blocks:
        out_spec = pl.BlockSpec((1, m, tn), lambda j, kk: (j, 0, 0))
        out_shape = jax.ShapeDtypeStruct((n // tn, m, tn), F32)
    else:
        out_spec = pl.BlockSpec((m, tn), lambda j, kk: (0, j))
        out_shape = jax.ShapeDtypeStruct((m, n), F32)
    return pl.pallas_call(
        body, name=name, grid=(n // tn, t // tk),
        in_specs=[pl.BlockSpec((tk, m), lambda j, kk: (kk, 0)), pl.BlockSpec((tk, tn), lambda j, kk: (kk, j))],
        out_specs=out_spec, out_shape=out_shape,
        compiler_params=_params(("parallel", "arbitrary")))(a, b)


def _w_in_pieces(d, di, dc, nh, shard_w):
    pgd = d // N_POOL
    o_dt, o_u = di + dc, di + dc + nh
    o_zp, o_ga, o_gb = o_u + d, o_u + 2 * d, o_u + 3 * d
    c_z, c_uz = 2 * d, 2 * d + di + dc
    runs = [(False, 0, o_ga, d), (False, d, o_gb, d), (False, c_z, 0, di + dc), (True, 0, o_dt, nh)]
    for g in range(N_POOL):
        runs.append((False, c_uz + 2 * g * pgd, o_u + g * pgd, pgd))
        runs.append((False, c_uz + (2 * g + 1) * pgd, o_zp + g * pgd, pgd))
    pieces = []
    for is_dt, dst, src, n in runs:
        while n > 0:
            k, off = divmod(src, shard_w)
            m = min(n, shard_w - off)
            pieces.append((is_dt, dst, k, off, m))
            dst, src, n = dst + m, src + m, n - m
    return pieces


def _regroup_w_in(w_sh, d, di, dc, nh, rb=256):
    _, rows, sw = w_sh.shape
    n_main = 4 * d + di + dc
    pieces = _w_in_pieces(d, di, dc, nh, sw)
    rb = min(rb, rows)

    def body(w_ref, main_ref, dt_ref):
        dt_ref[...] = jnp.zeros_like(dt_ref)
        for is_dt, dst, k, off, m in pieces:
            out = dt_ref if is_dt else main_ref
            out[:, dst:dst + m] = w_ref[k, :, off:off + m]

    return pl.pallas_call(
        body, name="regroup_w_in", grid=(rows // rb,),
        in_specs=[pl.BlockSpec((N_SHARD, rb, sw), lambda i: (0, i, 0))],
        out_specs=[pl.BlockSpec((rb, n_main), lambda i: (i, 0)), pl.BlockSpec((rb, LANES), lambda i: (i, 0))],
        out_shape=[jax.ShapeDtypeStruct((rows, n_main), w_sh.dtype), jax.ShapeDtypeStruct((rows, LANES), w_sh.dtype)],
        compiler_params=_params(("parallel",)))(w_sh)


def _ungroup_w_in(d_main, d_dt, d, di, dc, nh, rb=128):
    rows, n_main = d_main.shape
    sw = (n_main + nh) // N_SHARD
    pieces = _w_in_pieces(d, di, dc, nh, sw)
    rb = min(rb, rows)

    def body(main_ref, dt_ref, o_ref):
        for is_dt, dst, k, off, m in pieces:
            src = dt_ref if is_dt else main_ref
            o_ref[k, :, off:off + m] = src[:, dst:dst + m]

    return pl.pallas_call(
        body, name="ungroup_w_in", grid=(rows // rb,),
        in_specs=[pl.BlockSpec((rb, n_main), lambda i: (i, 0)), pl.BlockSpec((rb, LANES), lambda i: (i, 0))],
        out_specs=pl.BlockSpec((N_SHARD, rb, sw), lambda i: (0, i, 0)),
        out_shape=jax.ShapeDtypeStruct((N_SHARD, rows, sw), F32),
        compiler_params=_params(("parallel",)))(d_main, d_dt)


def _inproj(x2, norm_g, w_main, w_dt, tm=1024, tn=1024):
    t, d = x2.shape
    n = w_main.shape[1]
    tm, tn = min(tm, t), min(tn, n)

    def body(x_ref, g_ref, w_ref, wdt_ref, proj_ref, dt_ref, h_ref):
        @pl.when(pl.program_id(1) == 0)
        def _():
            xv = x_ref[...]
            r = lax.rsqrt(jnp.mean(xv * xv, axis=-1, keepdims=True) + EPS)
            h = (xv * r * g_ref[...]).astype(BF16)
            h_ref[...] = h
            dt_ref[...] = jnp.dot(h, wdt_ref[...].astype(BF16), preferred_element_type=F32)

        proj_ref[...] = jnp.dot(h_ref[...], w_ref[...].astype(BF16), preferred_element_type=F32)

    return pl.pallas_call(
        body, name="inproj", grid=(t // tm, n // tn),
        in_specs=[pl.BlockSpec((tm, d), lambda i, j: (i, 0)), pl.BlockSpec((1, d), lambda i, j: (0, 0)),
                  pl.BlockSpec((d, tn), lambda i, j: (0, j)), pl.BlockSpec((d, LANES), lambda i, j: (0, 0))],
        out_specs=[pl.BlockSpec((tm, tn), lambda i, j: (i, j)), pl.BlockSpec((tm, LANES), lambda i, j: (i, 0)),
                   pl.BlockSpec((tm, d), lambda i, j: (i, 0))],
        out_shape=[jax.ShapeDtypeStruct((t, n), F32), jax.ShapeDtypeStruct((t, LANES), F32),
                   jax.ShapeDtypeStruct((t, d), BF16)],
        compiler_params=_params(("parallel", "arbitrary")))(x2, norm_g, w_main, w_dt)


def _conv_w_spec(conv_w, cb, j_axis):
    sw = conv_w.shape[2]
    assert sw % cb == 0, (sw, cb)
    per = sw // cb
    return N_SHARD * per, pl.BlockSpec((1, CONV_WIDTH, cb), lambda *ij: (ij[j_axis] // per, 0, ij[j_axis] % per))


CONV_ROWS = 64
CONV_HALO = 8


def _conv_taps(x_ref, t0, rc):
    if t0 == 0:
        cur = x_ref[0:rc, :]
        row = lax.broadcasted_iota(jnp.int32, cur.shape, 0)
        return [cur] + [_shift_down(cur, j, row) for j in range(1, CONV_WIDTH)]
    ext = x_ref[t0 - CONV_HALO:t0 + rc, :]
    return [ext[CONV_HALO:]] + [pltpu.roll(ext, j, 0)[CONV_HALO:] for j in range(1, CONV_WIDTH)]


def _conv_weights(w_ref):
    return [w_ref[0, CONV_WIDTH - 1 - j:CONV_WIDTH - j, :] for j in range(CONV_WIDTH)]


def _conv_pre(taps, wts, bias):
    acc = bias + taps[0] * wts[0]
    for j in range(1, CONV_WIDTH):
        acc = acc + taps[j] * wts[j]
    return acc


def _conv_fwd(proj, conv_w, conv_b, nb, s, col0, cb=256):
    n_blk, w_spec = _conv_w_spec(conv_w, cb, 1)
    blk0 = _col_block(col0, cb)
    rc = min(CONV_ROWS, s)

    def body(x_ref, w_ref, b_ref, o_ref):
        wts, bias = _conv_weights(w_ref), b_ref[...]
        for t0 in range(0, s, rc):
            acc = _conv_pre(_conv_taps(x_ref, t0, rc), wts, bias)
            o_ref[t0:t0 + rc, :] = acc * _sigmoid(acc)

    return pl.pallas_call(
        body, name="conv_fwd", grid=(nb, n_blk),
        in_specs=[pl.BlockSpec((s, cb), lambda b, j: (b, blk0 + j)), w_spec, pl.BlockSpec((1, cb), lambda b, j: (0, j))],
        out_specs=pl.BlockSpec((s, cb), lambda b, j: (b, j)),
        out_shape=jax.ShapeDtypeStruct((nb * s, n_blk * cb), F32),
        compiler_params=_params(("parallel", "parallel")))(proj, conv_w, conv_b)


def _ssd_consts(di):
    r = lax.broadcasted_iota(jnp.int32, (CHUNK, CHUNK), 0)
    c = lax.broadcasted_iota(jnp.int32, (CHUNK, CHUNK), 1)
    tril = (r >= c).astype(F32)
    head = lax.broadcasted_iota(jnp.int32, (LANES, di), 0)
    chan = lax.broadcasted_iota(jnp.int32, (LANES, di), 1) // HEAD_DIM
    expand = (head == chan).astype(BF16)
    return tril, expand, expand.T


def _expand(v, e, terms=3):
    acc = None
    for _ in range(terms):
        vb = v.astype(BF16)
        part = jnp.dot(vb, e, preferred_element_type=F32)
        acc = part if acc is None else acc + part
        v = v - vb.astype(F32)
    return acc


def _head_sum(t, et, terms=2):
    acc = None
    for _ in range(terms):
        tb = t.astype(BF16)
        part = jnp.dot(tb, et, preferred_element_type=F32)
        acc = part if acc is None else acc + part
        t = t - tb.astype(F32)
    return acc


def _ssd_scalars(dtr_ref, dtb_ref, alog_ref, tri):
    dtpre = dtr_ref[...] + dtb_ref[...]
    dt = _softplus(dtpre)
    a_neg = -jnp.exp(alog_ref[...])
    a_dt = dt * a_neg
    a_cs = jnp.dot(tri, a_dt, precision=HIGHEST, preferred_element_type=F32)
    a_cst = lax.dot_general(a_dt, tri, (((0,), (1,)), ((), ())), precision=HIGHEST, preferred_element_type=F32)
    return dtpre, dt, a_neg, a_cs, a_cst


def _ssd_fwd(proj, xbc, dtraw, dtb, alog, dskx, gn, nb, s, di, z_col0):
    t = nb * s
    nc = s // CHUNK
    hpg = di // HEAD_DIM // SSM_GROUPS
    gw = di // SSM_GROUPS
    gn_w = SSM_GROUPS * D_STATE
    b_blk = _col_block(di, gn_w)
    z_blk = _col_block(z_col0, di)
    L, P, N = CHUNK, HEAD_DIM, D_STATE
    tril, expand, _ = _ssd_consts(di)

    def body(z_ref, x_ref, b_ref, c_ref, dtr_ref, dtb_ref, alog_ref, dskx_ref, gn_ref, tril_ref, e_ref,
             ypre_ref, yan_ref, hp_ref, st_ref, yd_ref, xdt_ref):
        @pl.when(pl.program_id(1) == 0)
        def _():
            st_ref[...] = jnp.zeros_like(st_ref)

        hp_ref[0] = st_ref[...]
        tri = tril_ref[...]
        _, dt, _, a_cs, a_cst = _ssd_scalars(dtr_ref, dtb_ref, alog_ref, tri)
        ev = e_ref[...]
        a_exp = _expand(a_cs, ev)
        xv = x_ref[...]
        xdt = xv * _expand(dt, ev, terms=2)
        xdt_ref[...] = xdt
        a_last = a_exp[L - 1:L, :]
        xe = xdt * jnp.exp(a_last - a_exp)
        ea = jnp.exp(a_exp)
        e_last = jnp.exp(a_last)
        lower = tri > 0.5
        for g in range(SSM_GROUPS):
            gs = slice(g * gw, (g + 1) * gw)
            bg = b_ref[:, g * N:(g + 1) * N].astype(BF16)
            cg = c_ref[:, g * N:(g + 1) * N].astype(BF16)
            gm = _bdot_nt(cg, bg)
            ht = st_ref[:, gs]
            ch = _bdot(cg, ht)
            for e in range(hpg):
                h = g * hpg + e
                hs = slice(h * P, (h + 1) * P)
                decay = jnp.where(lower, jnp.exp(a_cs[:, h:h + 1] - a_cst[h:h + 1, :]), 0.0)
                yd_ref[:, hs] = _bdot(gm * decay, xdt_ref[:, hs])
            st_ref[:, gs] = ht * e_last[:, gs] + _bdot_tn(bg, xe[:, gs])
            ypre = yd_ref[:, gs] + ea[:, gs] * ch + xv[:, gs] * dskx_ref[:, gs]
            ypre_ref[:, gs] = ypre
            zv = z_ref[:, gs]
            v = ypre * zv * _sigmoid(zv)
            r = lax.rsqrt(jnp.mean(v * v, axis=-1, keepdims=True) + EPS)
            yan_ref[:, gs] = (v * r * gn_ref[:, gs]).astype(BF16)

    row = lambda b, c: b * nc + c
    vec = lambda w: pl.BlockSpec((1, w), lambda b, c: (0, 0))
    return pl.pallas_call(
        body, name="ssd_fwd", grid=(nb, nc),
        in_specs=[pl.BlockSpec((L, di), lambda b, c: (row(b, c), z_blk)),
                  pl.BlockSpec((L, di), lambda b, c: (row(b, c), 0)),
                  pl.BlockSpec((L, gn_w), lambda b, c: (row(b, c), b_blk)),
                  pl.BlockSpec((L, gn_w), lambda b, c: (row(b, c), b_blk + 1)),
                  pl.BlockSpec((L, LANES), lambda b, c: (row(b, c), 0)),
                  vec(LANES), vec(LANES), vec(di), vec(di),
                  pl.BlockSpec((L, L), lambda b, c: (0, 0)),
                  pl.BlockSpec((LANES, di), lambda b, c: (0, 0))],
        out_specs=[pl.BlockSpec((L, di), lambda b, c: (row(b, c), 0)),
                   pl.BlockSpec((L, di), lambda b, c: (row(b, c), 0)),
                   pl.BlockSpec((1, N, di), lambda b, c: (row(b, c), 0, 0))],
        out_shape=[jax.ShapeDtypeStruct((t, di), F32), jax.ShapeDtypeStruct((t, di), BF16),
                   jax.ShapeDtypeStruct((nb * nc, N, di), F32)],
        scratch_shapes=[pltpu.VMEM((N, di), F32), pltpu.VMEM((L, di), F32), pltpu.VMEM((L, di), F32)],
        compiler_params=_params(("parallel", "arbitrary")))(
            proj, xbc, xbc, xbc, dtraw, dtb, alog, dskx, gn, tril, expand)


def _pool_sum(v, g, row, shift):
    s2 = v + shift(v, 1, row)
    s4 = s2 + shift(s2, 2, row)
    s8 = s4 + shift(s4, 4, row)
    s16 = s8 + shift(s8, 8, row)
    return jnp.where(g == 0, s2, jnp.where(g == 1, s4, jnp.where(g == 2, s8, s16)))


def _pool_count(g, row):
    return jnp.minimum(row + 1, jnp.left_shift(2, g)).astype(F32)


POOL_ROWS = 128
POOL_HALO = 16


def _roll_sum(v, g, step):
    n = v.shape[0]
    s2 = v + pltpu.roll(v, step % n, 0)
    s4 = s2 + pltpu.roll(s2, (2 * step) % n, 0)
    s8 = s4 + pltpu.roll(s4, (4 * step) % n, 0)
    s16 = s8 + pltpu.roll(s8, (8 * step) % n, 0)
    return jnp.where(g == 0, s2, jnp.where(g == 1, s4, jnp.where(g == 2, s8, s16)))


def _pool_trailing(u_ref, pgd, t0, rc, g):
    if t0 == 0:
        cur = u_ref[0:rc, :pgd]
        row = lax.broadcasted_iota(jnp.int32, cur.shape, 0)
        return cur, _pool_sum(cur, g, row, _shift_down), _pool_count(g, row)
    ext = u_ref[t0 - POOL_HALO:t0 + rc, :pgd]
    return ext[POOL_HALO:], _roll_sum(ext, g, 1)[POOL_HALO:], jnp.left_shift(2, g).astype(F32)


def _pool_fwd(proj, mix_w, mix_b, scale, nb, s, col0):
    pgd = mix_w.shape[-1]
    blk0 = _col_block(col0, 2 * pgd)
    rc = min(POOL_ROWS, s)

    def body(uz_ref, w_ref, b_ref, sc_ref, o_ref):
        g = pl.program_id(1)
        wv = w_ref[:, 0].reshape(pgd, pgd)
        for t0 in range(0, s, rc):
            u, win, cnt = _pool_trailing(uz_ref, pgd, t0, rc, g)
            zp = uz_ref[t0:t0 + rc, pgd:]
            mixed = _bdot(win / cnt - u, wv) + b_ref[...]
            o_ref[t0:t0 + rc, :] = (mixed * sc_ref[...] * zp * _sigmoid(zp)).astype(BF16)

    return pl.pallas_call(
        body, name="pool_fwd", grid=(nb, N_POOL),
        in_specs=[pl.BlockSpec((s, 2 * pgd), lambda b, g: (b, blk0 + g)),
                  pl.BlockSpec((N_SHARD, 1, pgd // N_SHARD, pgd), lambda b, g: (0, g, 0, 0)),
                  pl.BlockSpec((1, pgd), lambda b, g: (0, g)), pl.BlockSpec((1, pgd), lambda b, g: (0, g))],
        out_specs=pl.BlockSpec((s, pgd), lambda b, g: (b, g)),
        out_shape=jax.ShapeDtypeStruct((nb * s, N_POOL * pgd), BF16),
        compiler_params=_params(("parallel", "parallel")))(proj, mix_w, mix_b, scale)


def _mid_fwd(ya, yb, proj, col0, x2, p2, tgt, wo, wpg, wup, ple_g, final_g, tm=256):
    t, d = ya.shape
    tm = min(tm, t)
    blk = _col_block(col0, 2 * d)
    n_up, pdim, up_w = wup.shape

    def body(ya_ref, yb_ref, g_ref, x_ref, p_ref, tgt_ref, wo_ref, wpg_ref, wup_ref, pg_ref, g_fin_ref,
             merged_ref, hn_ref, dpre_ref, dpu_ref, x1_ref, dx2_ref, loss_ref, dg_ref):
        @pl.when(pl.program_id(0) == 0)
        def _():
            loss_ref[...] = jnp.zeros_like(loss_ref)
            dg_ref[...] = jnp.zeros_like(dg_ref)

        merged = (_sigmoid(g_ref[:, :d]) * ya_ref[...] + _sigmoid(g_ref[:, d:]) * yb_ref[...]).astype(BF16)
        merged_ref[...] = merged
        x1 = x_ref[...] + jnp.dot(merged, wo_ref[...], preferred_element_type=F32)
        x1_ref[...] = x1
        r1 = lax.rsqrt(jnp.mean(x1 * x1, axis=-1, keepdims=True) + EPS)
        hn = (x1 * r1 * pg_ref[...]).astype(BF16)
        hn_ref[...] = hn
        gate = _sigmoid(jnp.dot(hn, wpg_ref[...], preferred_element_type=F32))
        pb = p_ref[...].astype(BF16)
        pu = jnp.concatenate([jnp.dot(pb, wup_ref[j], preferred_element_type=F32) for j in range(n_up)], axis=1)
        x2 = x1 + gate * pu
        r = lax.rsqrt(jnp.mean(x2 * x2, axis=-1, keepdims=True) + EPS)
        xn = x2 * r
        fg = g_fin_ref[...]
        err = xn * fg - tgt_ref[...]
        loss_ref[...] += 0.5 * jnp.sum(jnp.mean(err * err, axis=-1, keepdims=True))
        dy = err * (1.0 / d)
        dg_ref[...] += jnp.sum(dy * xn, axis=0, keepdims=True)
        dxn = dy * fg
        dx2 = r * (dxn - xn * jnp.mean(dxn * xn, axis=-1, keepdims=True))
        dx2_ref[...] = dx2
        dpre_ref[...] = (dx2 * pu * gate * (1.0 - gate)).astype(BF16)
        dpu_ref[...] = (dx2 * gate).astype(BF16)

    row = pl.BlockSpec((tm, d), lambda i: (i, 0))
    vec = pl.BlockSpec((1, d), lambda i: (0, 0))
    whole = lambda a: pl.BlockSpec(a.shape, lambda i: (0,) * a.ndim)
    return pl.pallas_call(
        body, name="mid_fwd", grid=(t // tm,),
        in_specs=[row, row, pl.BlockSpec((tm, 2 * d), lambda i: (i, blk)), row,
                  pl.BlockSpec((tm, pdim), lambda i: (i, 0)), row, whole(wo), whole(wpg), whole(wup), vec, vec],
        out_specs=[row] * 6 + [pl.BlockSpec((1, LANES), lambda i: (0, 0)), vec],
        out_shape=[jax.ShapeDtypeStruct((t, d), BF16)] * 4 + [jax.ShapeDtypeStruct((t, d), F32)] * 2 + [
            jax.ShapeDtypeStruct((1, LANES), F32), jax.ShapeDtypeStruct((1, d), F32)],
        compiler_params=_params(("arbitrary",)))(ya, yb, proj, x2, p2, tgt, wo, wpg, wup, ple_g, final_g)


def _rms_grad(xv, dh, g):
    r = lax.rsqrt(jnp.mean(xv * xv, axis=-1, keepdims=True) + EPS)
    xn = xv * r
    dd = dh * g
    return r * (dd - xn * jnp.mean(dd * xn, axis=-1, keepdims=True)), jnp.sum(dh * xn, axis=0, keepdims=True)


def _mid_bwd(dpre, dx2, x1, ya, yb, proj, col0, wpg, wo, ple_g, n_cols, tm=256):
    t, d = ya.shape
    tm = min(tm, t)
    blk = _col_block(col0, 2 * d)

    def body(dpre_ref, dx2_ref, x1_ref, ya_ref, yb_ref, g_ref, wpg_ref, wo_ref, pg_ref,
             dx1_ref, dya_ref, dyb_ref, dg_ref, dpg_ref):
        @pl.when(pl.program_id(0) == 0)
        def _():
            dpg_ref[...] = jnp.zeros_like(dpg_ref)

        dhn = _bdot_nt(dpre_ref[...], wpg_ref[...])
        dx, dpg = _rms_grad(x1_ref[...], dhn, pg_ref[...])
        dpg_ref[...] += dpg
        dx1 = dx2_ref[...] + dx
        dx1_ref[...] = dx1
        dm_v = _bdot_nt(dx1, wo_ref[...])
        sa = _sigmoid(g_ref[:, :d])
        sb = _sigmoid(g_ref[:, d:])
        dya_ref[...] = (dm_v * sa).astype(BF16)
        dyb_ref[...] = (dm_v * sb).astype(BF16)
        dg_ref[:, :d] = (dm_v * ya_ref[...] * sa * (1.0 - sa)).astype(BF16)
        dg_ref[:, d:] = (dm_v * yb_ref[...] * sb * (1.0 - sb)).astype(BF16)

    row = pl.BlockSpec((tm, d), lambda i: (i, 0))
    vec = pl.BlockSpec((1, d), lambda i: (0, 0))
    gspec = pl.BlockSpec((tm, 2 * d), lambda i: (i, blk))
    whole = lambda a: pl.BlockSpec(a.shape, lambda i: (0,) * a.ndim)
    return pl.pallas_call(
        body, name="mid_bwd", grid=(t // tm,),
        in_specs=[row, row, row, row, row, gspec, whole(wpg), whole(wo), vec],
        out_specs=[row, row, row, gspec, vec],
        out_shape=[jax.ShapeDtypeStruct((t, d), F32), jax.ShapeDtypeStruct((t, d), BF16),
                   jax.ShapeDtypeStruct((t, d), BF16), jax.ShapeDtypeStruct((t, n_cols), BF16),
                   jax.ShapeDtypeStruct((1, d), F32)],
        compiler_params=_params(("arbitrary",)))(dpre, dx2, x1, ya, yb, proj, wpg, wo, ple_g)


def _in_bwd(dproj, w_main, ddt, w_dt, x2, dx1, norm_g, after, tm=1024, tk=1024):
    t, k = dproj.shape
    d = x2.shape[1]
    tm, tk = min(tm, t), min(tk, k)
    nk = k // tk

    def body(a_ref, w_ref, ddt_ref, wdt_ref, x_ref, dres_ref, g_ref, _, gx_ref, dg_ref, acc_ref):
        kk = pl.program_id(1)

        @pl.when((pl.program_id(0) == 0) & (kk == 0))
        def _():
            dg_ref[...] = jnp.zeros_like(dg_ref)

        part = _bdot_nt(a_ref[...], w_ref[...])

        @pl.when(kk == 0)
        def _():
            acc_ref[...] = part

        @pl.when(kk > 0)
        def _():
            acc_ref[...] += part

        @pl.when(kk == nk - 1)
        def _():
            dh = acc_ref[...] + _bdot_nt(ddt_ref[...], wdt_ref[...])
            dx, dg = _rms_grad(x_ref[...], dh, g_ref[...])
            dg_ref[...] += dg
            gx_ref[...] = dres_ref[...] + dx

    row = pl.BlockSpec((tm, d), lambda i, j: (i, 0))
    vec = pl.BlockSpec((1, d), lambda i, j: (0, 0))
    return pl.pallas_call(
        body, name="in_bwd", grid=(t // tm, nk),
        in_specs=[pl.BlockSpec((tm, tk), lambda i, j: (i, j)), pl.BlockSpec((d, tk), lambda i, j: (0, j)),
                  pl.BlockSpec((tm, LANES), lambda i, j: (i, 0)), pl.BlockSpec((d, LANES), lambda i, j: (0, 0)),
                  row, row, vec, pl.BlockSpec((8, LANES), lambda i, j: (0, 0))],
        out_specs=[row, vec],
        out_shape=[jax.ShapeDtypeStruct((t, d), F32), jax.ShapeDtypeStruct((1, d), F32)],
        scratch_shapes=[pltpu.VMEM((tm, d), F32)],
        compiler_params=_params(("arbitrary", "arbitrary")))(dproj, w_main, ddt, w_dt, x2, dx1, norm_g, after)


def _pool_bwd(proj, dyb, dproj, mix_w, mix_b, scale, nb, s, col0):
    pgd = mix_w.shape[-1]
    blk0 = _col_block(col0, 2 * pgd)
    rc = min(POOL_ROWS, s)

    def body(uz_ref, dy_ref, _, w_ref, b_ref, sc_ref, duz_ref, dw_ref, db_ref, dsc_ref, dpn_ref, dwacc_ref):
        g = pl.program_id(0)

        @pl.when(pl.program_id(1) == 0)
        def _():
            dw_ref[...] = jnp.zeros_like(dw_ref)
            db_ref[...] = jnp.zeros_like(db_ref)
            dsc_ref[...] = jnp.zeros_like(dsc_ref)

        wv = w_ref[:, 0].reshape(pgd, pgd)
        sc = sc_ref[...]
        fold = lambda v: v.reshape(rc // 8, 8, pgd).sum(axis=0)
        db8 = jnp.zeros((8, pgd), F32)
        dsc8 = jnp.zeros((8, pgd), F32)
        dwacc_ref[...] = jnp.zeros_like(dwacc_ref)
        for t0 in range(0, s, rc):
            u, win, cnt = _pool_trailing(uz_ref, pgd, t0, rc, g)
            pooled = win / cnt - u
            zp = uz_ref[t0:t0 + rc, pgd:]
            mixed = _bdot(pooled, wv) + b_ref[...]
            sg = _sigmoid(zp)
            sz = zp * sg
            dy = dy_ref[t0:t0 + rc, :]
            dsc8 = dsc8 + fold(dy * mixed * sz)
            dmixed = dy * sc * sz
            db8 = db8 + fold(dmixed)
            dwacc_ref[...] += _bdot_tn(pooled, dmixed)
            dpn_ref[t0:t0 + rc, :] = _bdot_nt(dmixed, wv) / cnt
            duz_ref[t0:t0 + rc, pgd:] = (dy * mixed * sc * sg * (1.0 + zp * (1.0 - sg))).astype(BF16)
        dsc_ref[...] += jnp.sum(dsc8, axis=0, keepdims=True)
        db_ref[...] += jnp.sum(db8, axis=0, keepdims=True)
        dw_ref[:, 0] += dwacc_ref[...].reshape(N_SHARD, pgd // N_SHARD, pgd)
        for t0 in range(0, s, rc):
            if t0 + rc < s:
                n = rc + POOL_HALO
                win = dpn_ref[t0:t0 + n, :]
                cur, lead = win[:rc], _roll_sum(win, g, n - 1)[:rc]
            else:
                cur = dpn_ref[t0:t0 + rc, :]
                row = lax.broadcasted_iota(jnp.int32, cur.shape, 0)
                lead = _pool_sum(cur, g, row, _shift_up)
            if t0 == 0:
                cnt = _pool_count(g, lax.broadcasted_iota(jnp.int32, cur.shape, 0))
            else:
                cnt = jnp.left_shift(2, g).astype(F32)
            duz_ref[t0:t0 + rc, :pgd] = (lead - cur * cnt).astype(BF16)

    uz = pl.BlockSpec((s, 2 * pgd), lambda g, b: (b, blk0 + g))
    vec = pl.BlockSpec((1, pgd), lambda g, b: (0, g))
    wspec = pl.BlockSpec((N_SHARD, 1, pgd // N_SHARD, pgd), lambda g, b: (0, g, 0, 0))
    return pl.pallas_call(
        body, name="pool_bwd", grid=(N_POOL, nb),
        in_specs=[uz, pl.BlockSpec((s, pgd), lambda g, b: (b, g)), pl.BlockSpec(memory_space=pl.ANY), wspec, vec, vec],
        out_specs=[uz, wspec, vec, vec],
        out_shape=[jax.ShapeDtypeStruct(dproj.shape, dproj.dtype), jax.ShapeDtypeStruct(mix_w.shape, F32),
                   jax.ShapeDtypeStruct(mix_b.shape, F32), jax.ShapeDtypeStruct(scale.shape, F32)],
        scratch_shapes=[pltpu.VMEM((s, pgd), F32), pltpu.VMEM((pgd, pgd), F32)],
        input_output_aliases={2: 0},
        compiler_params=_params(("parallel", "arbitrary")))(proj, dyb, dproj, mix_w, mix_b, scale)


def _ssd_bwd(dyan, ypre, proj, xbc, dtraw, hp, dproj, dtb, alog, dskx, gn, nb, s, di, z_col0):
    t = nb * s
    nc = s // CHUNK
    hpg = di // HEAD_DIM // SSM_GROUPS
    gw = di // SSM_GROUPS
    gn_w = SSM_GROUPS * D_STATE
    dc = di + 2 * gn_w
    b_blk = _col_block(di, gn_w)
    z_blk = _col_block(z_col0, di)
    L, P, N = CHUNK, HEAD_DIM, D_STATE
    tril, expand, expand_t = _ssd_consts(di)

    def body(dy_ref, ypre_ref, z_ref, x_ref, b_ref, c_ref, dtr_ref, hp_ref, _, dtb_ref, alog_ref, dskx_ref, gn_ref,
             tril_ref, e_ref, et_ref, dz_ref, ddt_ref, dxbc_ref, dgn_ref, ddsk_ref, dalog_ref, ddtb_ref,
             dst_ref, dyp_ref, xdt_ref, dxm_ref, t1_ref, t3_ref, aux_ref):
        @pl.when((pl.program_id(0) == 0) & (pl.program_id(1) == 0))
        def _():
            dgn_ref[...] = jnp.zeros_like(dgn_ref)
            ddsk_ref[...] = jnp.zeros_like(ddsk_ref)
            dalog_ref[...] = jnp.zeros_like(dalog_ref)
            ddtb_ref[...] = jnp.zeros_like(ddtb_ref)

        @pl.when(pl.program_id(1) == 0)
        def _():
            dst_ref[...] = jnp.zeros_like(dst_ref)

        tri = tril_ref[...]
        dtpre, dt, a_neg, a_cs, a_cst = _ssd_scalars(dtr_ref, dtb_ref, alog_ref, tri)
        ev = e_ref[...]
        a_exp = _expand(a_cs, ev)
        dt_exp = _expand(dt, ev, terms=2)
        xv = x_ref[...]
        xdt = xv * dt_exp
        xdt_ref[...] = xdt
        a_last = a_exp[L - 1:L, :]
        dte = jnp.exp(a_last - a_exp)
        xe = xdt * dte
        ea = jnp.exp(a_exp)
        e_last = jnp.exp(a_last)
        lower = tri > 0.5
        aux_ref[...] = jnp.zeros_like(aux_ref)
        for g in range(SSM_GROUPS):
            gs = slice(g * gw, (g + 1) * gw)
            zv = z_ref[:, gs]
            yp = ypre_ref[:, gs]
            sg = _sigmoid(zv)
            sz = zv * sg
            vg = yp * sz
            r = lax.rsqrt(jnp.mean(vg * vg, axis=-1, keepdims=True) + EPS)
            vn = vg * r
            dyg = dy_ref[:, gs]
            dgn_ref[:, gs] += jnp.sum(dyg * vn, axis=0, keepdims=True)
            dvn = dyg * gn_ref[:, gs]
            dv = r * (dvn - vn * jnp.mean(dvn * vn, axis=-1, keepdims=True))
            dy = dv * sz
            dyp_ref[:, gs] = dy
            dz_ref[:, gs] = (dv * yp * sg * (1.0 + zv * (1.0 - sg))).astype(BF16)
            bg = b_ref[:, g * N:(g + 1) * N].astype(BF16)
            cg = c_ref[:, g * N:(g + 1) * N].astype(BF16)
            gm = _bdot_nt(cg, bg)
            ht = hp_ref[0, :, gs]
            dht = dst_ref[:, gs]
            bds = _bdot(bg, dht)
            dye = dy * ea[:, gs]
            xe_g = xe[:, gs]
            dcg = _bdot_nt(dye, ht)
            dbg = _bdot_nt(xe_g, dht)
            dst_ref[:, gs] = e_last[:, gs] * dht + _bdot_tn(cg, dye)
            dgm = jnp.zeros((L, L), F32)
            for e in range(hpg):
                h = g * hpg + e
                hs = slice(h * P, (h + 1) * P)
                decay = jnp.where(lower, jnp.exp(a_cs[:, h:h + 1] - a_cst[h:h + 1, :]), 0.0)
                dy_h = dyp_ref[:, hs]
                dgm = dgm + _bdot_nt(dy_h, xdt_ref[:, hs]) * decay
                dxm_ref[:, hs] = _bdot_tn(gm * decay, dy_h)
            dxbc_ref[:, di + g * N:di + (g + 1) * N] = dbg + _bdot_tn(dgm, cg)
            dxbc_ref[:, di + gn_w + g * N:di + gn_w + (g + 1) * N] = dcg + _bdot(dgm, bg)
            dxm = dxm_ref[:, gs]
            x_g = xv[:, gs]
            dskx = dskx_ref[:, gs]
            xeb = xe_g * bds
            dxdt = dxm + dte[:, gs] * bds
            dxbc_ref[:, gs] = dxdt * dt_exp[:, gs] + dy * dskx
            each = ea[:, gs] * _bdot(cg, ht)
            y_diag = yp - x_g * dskx - each
            rnd = lambda v: v.astype(BF16).astype(F32)
            t1_ref[:, gs] = rnd(dy) * y_diag + dy * each - rnd(xdt[:, gs]) * dxm - xeb
            t3_ref[:, gs] = dxdt * x_g
            aux_ref[0:1, gs] = jnp.sum(dht * ht, axis=0, keepdims=True)
            aux_ref[1:2, gs] = jnp.sum(dy * x_g, axis=0, keepdims=True)
            aux_ref[2:3, gs] = jnp.sum(xeb, axis=0, keepdims=True)
        etv = et_ref[...]
        aux = _head_sum(aux_ref[...], etv)
        rowi = lax.broadcasted_iota(jnp.int32, (L, LANES), 0)
        end = aux[2:3, :] + aux[0:1, :] * jnp.exp(a_cs[L - 1:L, :])
        da = _head_sum(t1_ref[...], etv, terms=3) + jnp.where(rowi == L - 1, end, 0.0)
        rc = lax.dot_general(tri, da, (((0,), (0,)), ((), ())), precision=HIGHEST, preferred_element_type=F32)
        ddt = a_neg * rc + _head_sum(t3_ref[...], etv, terms=1)
        ddtraw = ddt * _sigmoid_tail(dtpre)
        ddt_ref[...] = ddtraw.astype(BF16)
        ddtb_ref[...] += jnp.sum(ddtraw, axis=0, keepdims=True)
        dalog_ref[...] += jnp.sum(dt * rc, axis=0, keepdims=True) * a_neg
        ddsk_ref[...] += aux[1:2, :]

    row = lambda b, c: b * nc + (nc - 1 - c)
    full = lambda w: pl.BlockSpec((L, w), lambda b, c: (row(b, c), 0))
    zspec = pl.BlockSpec((L, di), lambda b, c: (row(b, c), z_blk))
    vec = lambda w: pl.BlockSpec((1, w), lambda b, c: (0, 0))
    slab = lambda shape: pltpu.VMEM(shape, F32)
    return pl.pallas_call(
        body, name="ssd_bwd", grid=(nb, nc),
        in_specs=[full(di), full(di), zspec, full(di),
                  pl.BlockSpec((L, gn_w), lambda b, c: (row(b, c), b_blk)),
                  pl.BlockSpec((L, gn_w), lambda b, c: (row(b, c), b_blk + 1)),
                  full(LANES),
                  pl.BlockSpec((1, N, di), lambda b, c: (row(b, c), 0, 0)),
                  pl.BlockSpec(memory_space=pl.ANY),
                  vec(LANES), vec(LANES), vec(di), vec(di),
                  pl.BlockSpec((L, L), lambda b, c: (0, 0)),
                  pl.BlockSpec((LANES, di), lambda b, c: (0, 0)),
                  pl.BlockSpec((di, LANES), lambda b, c: (0, 0))],
        out_specs=[zspec, full(LANES), full(dc), vec(di), vec(LANES), vec(LANES), vec(LANES)],
        out_shape=[jax.ShapeDtypeStruct(dproj.shape, dproj.dtype), jax.ShapeDtypeStruct((t, LANES), BF16),
                   jax.ShapeDtypeStruct((t, dc), F32), jax.ShapeDtypeStruct((1, di), F32),
                   jax.ShapeDtypeStruct((1, LANES), F32), jax.ShapeDtypeStruct((1, LANES), F32),
                   jax.ShapeDtypeStruct((1, LANES), F32)],
        scratch_shapes=[slab((N, di)), slab((L, di)), slab((L, di)), slab((L, di)), slab((L, di)), slab((L, di)),
                        slab((8, di))],
        input_output_aliases={8: 0},
        compiler_params=_params(("arbitrary", "arbitrary")))(
            dyan, ypre, proj, xbc, xbc, xbc, dtraw, hp, dproj, dtb, alog, dskx, gn, tril, expand, expand_t)


def _conv_bwd(proj, dxbc, dproj, conv_w, conv_b, nb, s, col0, cb=256):
    n_blk, w_spec = _conv_w_spec(conv_w, cb, 0)
    blk0 = _col_block(col0, cb)
    rc = min(CONV_ROWS, s)

    def body(x_ref, dy_ref, _, w_ref, b_ref, dx_ref, dw_ref, db_ref, dacc_ref):
        @pl.when(pl.program_id(1) == 0)
        def _():
            dw_ref[...] = jnp.zeros_like(dw_ref)
            db_ref[...] = jnp.zeros_like(db_ref)

        wts, bias = _conv_weights(w_ref), b_ref[...]
        fold = lambda v: v.reshape(rc // 8, 8, cb).sum(axis=0)
        db8 = jnp.zeros((8, cb), F32)
        dw8 = [jnp.zeros((8, cb), F32) for _ in range(CONV_WIDTH)]
        for t0 in range(0, s, rc):
            taps = _conv_taps(x_ref, t0, rc)
            acc = _conv_pre(taps, wts, bias)
            sg = _sigmoid(acc)
            dacc = dy_ref[t0:t0 + rc, :] * sg * (1.0 + acc * (1.0 - sg))
            dacc_ref[t0:t0 + rc, :] = dacc
            db8 = db8 + fold(dacc)
            dw8 = [dw8[j] + fold(dacc * taps[j]) for j in range(CONV_WIDTH)]
        db_ref[...] += jnp.sum(db8, axis=0, keepdims=True)
        for j in range(CONV_WIDTH):
            dw_ref[0, CONV_WIDTH - 1 - j:CONV_WIDTH - j, :] += jnp.sum(dw8[j], axis=0, keepdims=True)
        for t0 in range(0, s, rc):
            if t0 + rc < s:
                n = rc + CONV_HALO
                win = dacc_ref[t0:t0 + n, :]
                ups = [win[:rc]] + [pltpu.roll(win, n - j, 0)[:rc] for j in range(1, CONV_WIDTH)]
            else:
                cur = dacc_ref[t0:t0 + rc, :]
                row = lax.broadcasted_iota(jnp.int32, cur.shape, 0)
                ups = [cur] + [_shift_up(cur, j, row) for j in range(1, CONV_WIDTH)]
            dx = ups[0] * wts[0]
            for j in range(1, CONV_WIDTH):
                dx = dx + ups[j] * wts[j]
            dx_ref[t0:t0 + rc, :] = dx.astype(BF16)

    return pl.pallas_call(
        body, name="conv_bwd", grid=(n_blk, nb),
        in_specs=[pl.BlockSpec((s, cb), lambda j, b: (b, blk0 + j)), pl.BlockSpec((s, cb), lambda j, b: (b, j)),
                  pl.BlockSpec(memory_space=pl.ANY), w_spec, pl.BlockSpec((1, cb), lambda j, b: (0, j))],
        out_specs=[pl.BlockSpec((s, cb), lambda j, b: (b, blk0 + j)), w_spec, pl.BlockSpec((1, cb), lambda j, b: (0, j))],
        out_shape=[jax.ShapeDtypeStruct(dproj.shape, dproj.dtype), jax.ShapeDtypeStruct(conv_w.shape, F32),
                   jax.ShapeDtypeStruct(conv_b.shape, F32)],
        scratch_shapes=[pltpu.VMEM((s, cb), F32)],
        input_output_aliases={2: 0},
        compiler_params=_params(("parallel", "arbitrary")))(proj, dxbc, dproj, conv_w, conv_b)


def _local_step(x, p, tgt, wg, small, rest_weights, early_grads, w_in_grad):
    nb, s, d = x.shape
    t = nb * s
    gn_w = SSM_GROUPS * D_STATE
    dc = N_SHARD * wg["conv_w"].shape[2]
    di = dc - 2 * gn_w
    nh = di // HEAD_DIM
    pgd = d // N_POOL
    x2 = x.reshape(t, d)
    p2 = p.reshape(t, p.shape[-1])
    tgt2 = tgt.reshape(t, d)

    w_main, w_dt = _regroup_w_in(wg["w_in"], d, di, dc, nh)
    c_g, c_z, c_xbc, c_uz = 0, 2 * d, 2 * d + di, 2 * d + di + dc
    n_main = w_main.shape[1]

    pad_h = lambda v: jnp.pad(v.reshape(1, nh).astype(F32), ((0, 0), (0, LANES - nh)))
    dtb, alog = pad_h(small["dt_bias"]), pad_h(small["a_log"])
    dskx = jnp.repeat(small["d_skip"].reshape(1, nh).astype(F32), HEAD_DIM, axis=1)
    vec = lambda v: v.reshape(1, -1).astype(F32)
    norm_g, gn, conv_b = vec(small["norm_g"]), vec(small["gnorm_g"]), vec(small["conv_b"])
    mix_b, scale = vec(small["pool_mix_b"]), vec(small["pool_scale"])
    ple_g, final_g = vec(small["ple_norm_g"]), vec(small["final_g"])
    conv_w = wg["conv_w"]

    wide = _tile(n_main, 2304, LANES)
    proj, dtraw, h = _inproj(x2, norm_g, w_main, w_dt, tn=wide)
    xbc = _conv_fwd(proj, conv_w, conv_b, nb, s, c_xbc)
    ypre, yan, hp = _ssd_fwd(proj, xbc, dtraw, dtb, alog, dskx, gn, nb, s, di, c_z)
    wr = rest_weights(yan)
    mix_w = wr["pool_mix_w"].reshape(N_SHARD, N_POOL, pgd // N_SHARD, pgd)
    rows = lambda v: v.reshape(-1, v.shape[-1])
    wa, wb, wo, wpg = rows(wr["w_branch_a"]), rows(wr["w_branch_b"]), rows(wr["w_out"]), rows(wr["w_ple_gate"])
    wup = wr["w_ple_up"]
    ybp = _pool_fwd(proj, mix_w, mix_b, scale, nb, s, c_uz)
    ya = _mm(yan, wa, "branch_a")
    yb = _mm(ybp, wb, "branch_b")
    merged, hn, dpre, dpu, x1, dx2, loss, d_final_g = _mid_fwd(
        ya, yb, proj, c_g, x2, p2, tgt2, wo, wpg, wup, ple_g, final_g)

    d_wpg = _mm_tn(hn, dpre, "d_w_ple_gate")
    d_wup = _mm_tn(p2, dpu, "d_w_ple_up", tn=wup.shape[-1], col_blocks=True)
    dx1, dya, dyb, dproj, d_ple_g = _mid_bwd(dpre, dx2, x1, ya, yb, proj, c_g, wpg, wo, ple_g, n_main)
    d_wo = _mm_tn(merged, dx1, "d_w_out")
    d_wa = _mm_tn(yan, dya, "d_w_branch_a", tk=1024)
    d_wb = _mm_tn(ybp, dyb, "d_w_branch_b")
    dyan = _mm_nt(dya, wa, "d_y_a")
    dybp = _mm_nt(dyb, wb, "d_y_b")
    dproj, d_mix_w, d_mix_b, d_scale = _pool_bwd(proj, dybp, dproj, mix_w, mix_b, scale, nb, s, c_uz)
    shard_major = lambda v: v.reshape(N_SHARD, v.shape[0] // N_SHARD, v.shape[1])
    early = dict(pool_mix_w=d_mix_w.reshape(N_SHARD, pgd, pgd), w_branch_a=shard_major(d_wa),
                 w_branch_b=shard_major(d_wb), w_out=shard_major(d_wo), w_ple_gate=shard_major(d_wpg),
                 w_ple_up=d_wup)
    token = early_grads(early)
    dproj, ddt, dxbc, d_gn, d_dsk, d_alog, d_dtb = _ssd_bwd(
        dyan, ypre, proj, xbc, dtraw, hp, dproj, dtb + token[0:1, 0:1], alog, dskx, gn, nb, s, di, c_z)
    dproj, d_conv_w, d_conv_b = _conv_bwd(proj, dxbc, dproj, conv_w, conv_b, nb, s, c_xbc)
    d_wmain = _mm_tn(h, dproj, "d_w_in", tk=2048)
    d_wdt = _mm_tn(h, ddt, "d_w_dt")
    d_w_in = _ungroup_w_in(d_wmain, d_wdt, d, di, dc, nh)
    token = w_in_grad(d_w_in)
    gx, d_norm_g = _in_bwd(dproj, w_main, ddt, w_dt, x2, dx1, norm_g, token, tk=_tile(n_main, 1536, LANES))

    grads = dict(norm_g=d_norm_g, w_in=d_w_in, conv_w=d_conv_w, conv_b=d_conv_b, dt_bias=d_dtb[:, :nh],
                 a_log=d_alog[:, :nh], d_skip=d_dsk[:, :nh], gnorm_g=d_gn, pool_mix_b=d_mix_b, pool_scale=d_scale,
                 ple_norm_g=d_ple_g, final_g=d_final_g, **early)
    return loss[0, 0], gx.reshape(nb, s, d), grads


def _place():
    return lax.axis_index("x"), lax.axis_index("y"), lax.axis_index("c")


def _other_chips(x, y):
    return [(1 - x, y), (x, 1 - y), (1 - x, 1 - y)]


def _halves(c, rows, align):
    rh = rows // 2
    assert rows % 2 == 0 and rh % align == 0, (rows, align)
    return (pl.ds(pl.multiple_of(c * rh, align), rh), pl.ds(pl.multiple_of((1 - c) * rh, align), rh))


HBM = pl.BlockSpec(memory_space=pl.ANY)


def _into_slot(w2, k, dtype, name):
    rows, cols = w2.shape
    rb = _tile(rows, 256)

    def body(k_ref, w_ref, o_ref):
        o_ref[0] = w_ref[...].astype(dtype)

    return pl.pallas_call(
        body, name=name,
        grid_spec=pltpu.PrefetchScalarGridSpec(
            num_scalar_prefetch=1, grid=(rows // rb,),
            in_specs=[pl.BlockSpec((rb, cols), lambda i, k_ref: (i, 0))],
            out_specs=pl.BlockSpec((1, rb, cols), lambda i, k_ref: (k_ref[0], i, 0))),
        out_shape=jax.ShapeDtypeStruct((N_SHARD, rows, cols), dtype),
        compiler_params=_params(("parallel",)))(k.reshape(1), w2)


def _gather_weights(split, whole):
    n_split, n_all = len(split), len(split) + len(whole)

    def body(*refs):
        bufs = refs[n_all:2 * n_all]
        send_sems, recv_sems = refs[2 * n_all:]
        x, y, c = _place()
        k, k_x, k_y, k_d = 2 * x + y, 2 * (1 - x) + y, 2 * x + (1 - y), 2 * (1 - x) + (1 - y)
        x_nb, y_nb, sib = (1 - x, y, c), (x, 1 - y, c), (x, y, 1 - c)

        def copy(idx, block, to):
            return pltpu.make_async_remote_copy(src_ref=block, dst_ref=block, send_sem=send_sems.at[idx],
                                                recv_sem=recv_sems.at[idx], device_id=to, device_id_type=MESH)

        started = []

        def start(idx, block, to):
            started.append(copy(idx, block, to))
            started[-1].start()

        for i in range(n_split):
            buf, s0 = bufs[i], 8 * i
            rh = buf.shape[1] // 2
            rq = rh // 2
            assert buf.shape[1] == 4 * rq and rq % 16 == 0, buf.shape

            def rows(core, part):
                lo = core * rh + (rq if part == "bottom" else 0)
                return pl.ds(pl.multiple_of(lo, 16), rh if part == "all" else rq)

            start(s0 + 0, buf.at[k, rows(c, "all")], x_nb)
            start(s0 + 1, buf.at[k, rows(c, "all")], y_nb)
            copy(s0 + 0, buf.at[k_x, rows(c, "all")], x_nb).wait_recv()
            start(s0 + 2, buf.at[k_x, rows(c, "top")], y_nb)
            start(s0 + 4, buf.at[k_x, rows(c, "all")], sib)
            copy(s0 + 1, buf.at[k_y, rows(c, "all")], y_nb).wait_recv()
            start(s0 + 3, buf.at[k_y, rows(c, "bottom")], x_nb)
            start(s0 + 5, buf.at[k_y, rows(c, "all")], sib)
            copy(s0 + 2, buf.at[k_d, rows(c, "top")], y_nb).wait_recv()
            start(s0 + 6, buf.at[k_d, rows(c, "top")], sib)
            copy(s0 + 3, buf.at[k_d, rows(c, "bottom")], x_nb).wait_recv()
            start(s0 + 7, buf.at[k_d, rows(c, "bottom")], sib)
            copy(s0 + 4, buf.at[k_x, rows(1 - c, "all")], sib).wait_recv()
            copy(s0 + 5, buf.at[k_y, rows(1 - c, "all")], sib).wait_recv()
            copy(s0 + 6, buf.at[k_d, rows(1 - c, "top")], sib).wait_recv()
            copy(s0 + 7, buf.at[k_d, rows(1 - c, "bottom")], sib).wait_recv()
        for i in range(n_split, n_all):
            s0 = 8 * n_split + 3 * (i - n_split)
            for j, (px, py) in enumerate(_other_chips(x, y)):
                start(s0 + j, bufs[i].at[k], (px, py, c))
            for j, (px, py) in enumerate(_other_chips(x, y)):
                copy(s0 + j, bufs[i].at[2 * px + py], (px, py, c)).wait_recv()
        for cp in started:
            cp.wait_send()

    arrays = list(split) + list(whole)
    n_sem = 8 * n_split + 3 * len(whole)
    return pl.pallas_call(
        body, name="gather_weights",
        in_specs=[HBM] * n_all, out_specs=[HBM] * n_all,
        out_shape=[jax.ShapeDtypeStruct(a.shape, a.dtype) for a in arrays],
        input_output_aliases={i: i for i in range(n_all)},
        scratch_shapes=[pltpu.SemaphoreType.DMA((n_sem,)), pltpu.SemaphoreType.DMA((n_sem,))],
    )(*arrays)


def _swap_halves(gs):
    n = len(gs)

    def body(*refs):
        ins, outs, send_sems, recv_sems = refs[:n], refs[n:2 * n], refs[2 * n], refs[2 * n + 1]
        x, y, c = _place()
        copies = []
        for i in range(n):
            _, theirs = _halves(c, gs[i].shape[1], 8)
            cp = pltpu.make_async_remote_copy(src_ref=ins[i].at[:, theirs], dst_ref=outs[i], send_sem=send_sems.at[i],
                                              recv_sem=recv_sems.at[i], device_id=(x, y, 1 - c), device_id_type=MESH)
            cp.start()
            copies.append(cp)
        for cp in copies:
            cp.wait()

    return pl.pallas_call(
        body, name="swap_halves", in_specs=[HBM] * n, out_specs=[HBM] * n,
        out_shape=[jax.ShapeDtypeStruct((g.shape[0], g.shape[1] // 2, g.shape[2]), g.dtype) for g in gs],
        scratch_shapes=[pltpu.SemaphoreType.DMA((n,)), pltpu.SemaphoreType.DMA((n,))],
    )(*gs)


SEM = pl.BlockSpec(memory_space=pltpu.SEMAPHORE)
IN_HBM = pl.BlockSpec(memory_space=pltpu.HBM)
SPLIT_EFFECT = pltpu.SideEffectType.DATAFLOW_SIDE_EFFECTING


def _split_copies(plan, refs, send_sems, recv_sems):
    pairs = []
    for idx, (src, dst, landing, to) in enumerate(plan(refs)):
        mk = lambda d: pltpu.make_async_remote_copy(src_ref=src, dst_ref=d, send_sem=send_sems.at[idx],
                                                    recv_sem=recv_sems.at[idx], device_id=to, device_id_type=MESH)
        pairs.append((mk(dst), mk(landing)))
    return pairs


def _split_start(name, bufs, after, plan, n_copies):
    n = len(bufs)

    def body(*refs):
        send_sems, recv_sems, token = refs[n + 1], refs[n + 2], refs[-1]
        for send, _ in _split_copies(plan, refs[:n], send_sems, recv_sems):
            send.start()
        token[...] = jnp.zeros_like(token)

    sems = pltpu.SemaphoreType.DMA((n_copies,))
    out = pl.pallas_call(
        body, name=name,
        in_specs=[IN_HBM] * n + [HBM],
        out_specs=[SEM, SEM] + [IN_HBM] * n + [pl.BlockSpec(memory_space=pltpu.VMEM)],
        out_shape=[sems, sems] + [pltpu.HBM(b.shape, b.dtype) for b in bufs] + [jax.ShapeDtypeStruct((8, LANES), F32)],
        input_output_aliases={i: 2 + i for i in range(n)},
        compiler_params=pltpu.CompilerParams(has_side_effects=SPLIT_EFFECT),
    )(*[pltpu.with_memory_space_constraint(b, pltpu.HBM) for b in bufs], after)
    return out[0], out[1], out[2:2 + n], out[-1]


def _split_wait(name, bufs, send_sems, recv_sems, after, plan):
    n = len(bufs)

    def body(*refs):
        for send, recv in _split_copies(plan, refs[:n], refs[n], refs[n + 1]):
            send.wait_send()
            recv.wait_recv()

    return pl.pallas_call(
        body, name=name,
        in_specs=[IN_HBM] * n + [SEM, SEM, HBM],
        out_specs=[IN_HBM] * n,
        out_shape=[pltpu.HBM(b.shape, b.dtype) for b in bufs],
        input_output_aliases={i: i for i in range(n)},
        compiler_params=pltpu.CompilerParams(has_side_effects=SPLIT_EFFECT),
    )(*bufs, send_sems, recv_sems, after)


def _gather_plan(n):
    def plan(refs):
        x, y, c = _place()
        k = 2 * x + y
        return [(refs[i].at[k], refs[i].at[k], refs[i].at[2 * px + py], (px, py, c))
                for i in range(n) for px, py in _other_chips(x, y)]
    return plan


def _scatter_plan(n):
    def plan(refs):
        x, y, c = _place()
        return [(refs[i].at[2 * px + py], refs[n + i].at[j], refs[n + i].at[j], (px, py, c))
                for i in range(n) for j, (px, py) in enumerate(_other_chips(x, y))]
    return plan


IN_VMEM = pl.BlockSpec(memory_space=pltpu.VMEM)


def _to_bf16_all(gs):
    n = len(gs)

    def body(*refs):
        for i in range(n):
            refs[n + i][...] = refs[i][...].astype(BF16)

    return pl.pallas_call(
        body, name="bf16_rest", in_specs=[IN_VMEM] * n, out_specs=[IN_VMEM] * n,
        out_shape=[jax.ShapeDtypeStruct(g.shape, BF16) for g in gs],
        compiler_params=_params())(*gs)


def _add_landed_all(gs, landed, k):
    n = len(gs)

    def body(k_ref, *refs):
        for i in range(n):
            g_ref, l_ref, o_ref = refs[i], refs[n + i], refs[2 * n + i]
            o_ref[...] = g_ref[k_ref[0]] + l_ref[0].astype(F32) + l_ref[1].astype(F32) + l_ref[2].astype(F32)

    return pl.pallas_call(
        body, name="add_landed_rest",
        in_specs=[pl.BlockSpec(memory_space=pltpu.SMEM)] + [IN_VMEM] * (2 * n), out_specs=[IN_VMEM] * n,
        out_shape=[jax.ShapeDtypeStruct(g.shape[1:], F32) for g in gs],
        compiler_params=_params())(k.reshape(1), *gs, *landed)


def _final_exchange(halves, wholes, small):
    nh, nw = len(halves), len(wholes)
    n_in = nh + nw + 1
    rows = small.shape[0]

    def body(*refs):
        w_in, v_ref = refs[nh:nh + nw], refs[nh + nw]
        h_out, w_out, o_ref = refs[n_in:n_in + nh], refs[n_in + nh:n_in + nh + nw], refs[n_in + nh + nw]
        buf_ref, send_sems, recv_sems = refs[2 * n_in:]
        x, y, c = _place()
        sib = (x, y, 1 - c)
        me = 4 * x + 2 * y + c

        def copy(idx, src, dst, to):
            return pltpu.make_async_remote_copy(src_ref=src, dst_ref=dst, send_sem=send_sems.at[idx],
                                                recv_sem=recv_sems.at[idx], device_id=to, device_id_type=MESH)

        started = []
        split = [_halves(c, h_out[i].shape[0], 8) for i in range(nh)]
        for i in range(nh):
            started.append(copy(i, h_out[i].at[split[i][0]], h_out[i].at[split[i][0]], sib))
        for i in range(nw):
            started.append(copy(nh + i, w_in[i], w_out[i], sib))
        buf_ref[me] = v_ref[...]
        for rel in range(1, 8):
            peer = (x ^ (rel >> 2), y ^ ((rel >> 1) & 1), c ^ (rel & 1))
            started.append(copy(nh + nw + rel - 1, v_ref, buf_ref.at[me], peer))
        for cp in started:
            cp.start()
        for i in range(nh):
            copy(i, h_out[i].at[split[i][1]], h_out[i].at[split[i][1]], sib).wait_recv()
        for i in range(nw):
            copy(nh + i, w_in[i], w_out[i], sib).wait_recv()
        for rel in range(1, 8):
            copy(nh + nw + rel - 1, v_ref, buf_ref.at[me ^ rel], (x, y, c)).wait_recv()
        for cp in started:
            cp.wait_send()
        acc = buf_ref[0]
        for i in range(1, 8):
            acc = acc + buf_ref[i]
        o_ref[...] = acc

    n_sem = nh + nw + 7
    out = pl.pallas_call(
        body, name="final_exchange",
        in_specs=[HBM] * (nh + nw) + [IN_VMEM], out_specs=[HBM] * (nh + nw) + [IN_VMEM],
        out_shape=[jax.ShapeDtypeStruct(a.shape, a.dtype) for a in list(halves) + list(wholes)]
        + [jax.ShapeDtypeStruct(small.shape, F32)],
        input_output_aliases={i: i for i in range(nh)},
        scratch_shapes=[pltpu.VMEM((8, rows, LANES), F32), pltpu.SemaphoreType.DMA((n_sem,)),
                        pltpu.SemaphoreType.DMA((n_sem,))],
    )(*halves, *wholes, small)
    return out[:nh], out[nh:nh + nw], out[-1]


def _add_pair(g, got, c, name):
    _, rh, cols = got.shape
    rb = _tile(rh, 256)
    nrb = rh // rb

    def body(c_ref, g_ref, got_ref, o_ref):
        o_ref[...] = (g_ref[...] + got_ref[...]).astype(BF16)

    spec = pl.BlockSpec((1, rb, cols), lambda j, i, c_ref: (j, i, 0))
    return pl.pallas_call(
        body, name=name,
        grid_spec=pltpu.PrefetchScalarGridSpec(
            num_scalar_prefetch=1, grid=(N_SHARD, nrb),
            in_specs=[pl.BlockSpec((1, rb, cols), lambda j, i, c_ref: (j, c_ref[0] * nrb + i, 0)), spec],
            out_specs=spec),
        out_shape=jax.ShapeDtypeStruct(got.shape, BF16),
        compiler_params=_params(("parallel", "parallel")))(c.reshape(1), g, got)


def _add_chips(g, got, landed, k, c, name):
    _, rh, cols = got.shape
    rb = _tile(rh, 256)
    nrb = rh // rb

    def body(kc_ref, g_ref, got_ref, l_ref, o_ref):
        own = g_ref[0] + got_ref[0]
        o_ref[...] = own + l_ref[0].astype(F32) + l_ref[1].astype(F32) + l_ref[2].astype(F32)

    half_c = lambda i, kc: (kc[1] * nrb + i, 0)
    return pl.pallas_call(
        body, name=name,
        grid_spec=pltpu.PrefetchScalarGridSpec(
            num_scalar_prefetch=1, grid=(nrb,),
            in_specs=[pl.BlockSpec((1, rb, cols), lambda i, kc: (kc[0],) + half_c(i, kc)),
                      pl.BlockSpec((1, rb, cols), lambda i, kc: (kc[0], i, 0)),
                      pl.BlockSpec((N_SHARD - 1, rb, cols), lambda i, kc: (0, i, 0))],
            out_specs=pl.BlockSpec((rb, cols), half_c)),
        out_shape=jax.ShapeDtypeStruct((2 * rh, cols), F32),
        compiler_params=_params(("parallel",)))(jnp.stack([k, c]), g, got, landed)


def _adamw(wv, gs, m, v, name):
    rows, cols = wv.shape
    rb = _tile(rows, 256)
    c1 = 1.0 - ADAM_B1 ** ADAM_STEP
    c2 = 1.0 - ADAM_B2 ** ADAM_STEP
    n_g = len(gs)

    def body(*refs):
        w_ref, g_refs, (m_ref, v_ref, go_ref, d_ref, nm_ref, nv_ref) = refs[0], refs[1:1 + n_g], refs[1 + n_g:]
        gv = g_refs[0][...]
        for ref in g_refs[1:]:
            gv = gv + ref[...]
        go_ref[...] = gv
        nm = ADAM_B1 * m_ref[...] + (1.0 - ADAM_B1) * gv
        nv = ADAM_B2 * v_ref[...] + (1.0 - ADAM_B2) * (gv * gv)
        nm_ref[...] = nm
        nv_ref[...] = nv
        d_ref[...] = -ADAM_LR * ((nm / c1) / (jnp.sqrt(nv / c2) + ADAM_EPS) + ADAM_WD * w_ref[...])

    spec = pl.BlockSpec((rb, cols), lambda i: (i, 0))
    return pl.pallas_call(
        body, name=name, grid=(rows // rb,), in_specs=[spec] * (3 + n_g), out_specs=[spec] * 4,
        out_shape=[jax.ShapeDtypeStruct((rows, cols), F32)] * 4,
        compiler_params=_params(("parallel",)))(wv, *gs, m, v)


def _pack(flats):
    cat = jnp.concatenate([f.reshape(-1) for f in flats])
    n = cat.shape[0]
    rows = -(-n // (8 * LANES)) * 8
    return jnp.pad(cat, (0, rows * LANES - n)).reshape(rows, LANES)


def _unpack(packed, shapes):
    flat = packed.reshape(-1)
    out, off = [], 0
    for shp in shapes:
        n = 1
        for dim in shp:
            n *= dim
        out.append(flat[off:off + n].reshape(shp))
        off += n
    return out


def kernel(x, p, norm_g, w_in, conv_w, conv_b, dt_bias, a_log, d_skip, gnorm_g, pool_mix_w, pool_mix_b, pool_scale, w_branch_a, w_branch_b, w_out, ple_norm_g, w_ple_gate, w_ple_up, final_g, loss_target, m_norm_g, m_w_in, m_conv_w, m_conv_b, m_dt_bias, m_a_log, m_d_skip, m_gnorm_g, m_pool_mix_w, m_pool_mix_b, m_pool_scale, m_w_branch_a, m_w_branch_b, m_w_out, m_ple_norm_g, m_w_ple_gate, m_w_ple_up, m_final_g, v_norm_g, v_w_in, v_conv_w, v_conv_b, v_dt_bias, v_a_log, v_d_skip, v_gnorm_g, v_pool_mix_w, v_pool_mix_b, v_pool_scale, v_w_branch_a, v_w_branch_b, v_w_out, v_ple_norm_g, v_w_ple_gate, v_w_ple_up, v_final_g):
    wts = dict(norm_g=norm_g, w_in=w_in, conv_w=conv_w, conv_b=conv_b, dt_bias=dt_bias, a_log=a_log, d_skip=d_skip,
               gnorm_g=gnorm_g, pool_mix_w=pool_mix_w, pool_mix_b=pool_mix_b, pool_scale=pool_scale,
               w_branch_a=w_branch_a, w_branch_b=w_branch_b, w_out=w_out, ple_norm_g=ple_norm_g,
               w_ple_gate=w_ple_gate, w_ple_up=w_ple_up, final_g=final_g)
    mom_m = dict(norm_g=m_norm_g, w_in=m_w_in, conv_w=m_conv_w, conv_b=m_conv_b, dt_bias=m_dt_bias, a_log=m_a_log,
                 d_skip=m_d_skip, gnorm_g=m_gnorm_g, pool_mix_w=m_pool_mix_w, pool_mix_b=m_pool_mix_b,
                 pool_scale=m_pool_scale, w_branch_a=m_w_branch_a, w_branch_b=m_w_branch_b, w_out=m_w_out,
                 ple_norm_g=m_ple_norm_g, w_ple_gate=m_w_ple_gate, w_ple_up=m_w_ple_up, final_g=m_final_g)
    mom_v = dict(norm_g=v_norm_g, w_in=v_w_in, conv_w=v_conv_w, conv_b=v_conv_b, dt_bias=v_dt_bias, a_log=v_a_log,
                 d_skip=v_d_skip, gnorm_g=v_gnorm_g, pool_mix_w=v_pool_mix_w, pool_mix_b=v_pool_mix_b,
                 pool_scale=v_pool_scale, w_branch_a=v_w_branch_a, w_branch_b=v_w_branch_b, w_out=v_w_out,
                 ple_norm_g=v_ple_norm_g, w_ple_gate=v_w_ple_gate, w_ple_up=v_w_ple_up, final_g=v_final_g)
    c = lax.axis_index("c")
    k = 2 * lax.axis_index("x") + lax.axis_index("y")
    flat2 = lambda a: a.reshape(-1, a.shape[-1])

    slots = {n: _into_slot(flat2(wts[n]), k, BF16, "slot_" + n) for n in BIG}
    w_in_g, conv_g = _gather_weights([slots["w_in"]], [_into_slot(flat2(conv_w), k, F32, "slot_conv_w")])
    n_rest = len(REST)
    gsend, grecv, gbufs, gtoken = _split_start("gather_rest_start", [slots[n] for n in REST], conv_g,
                                               _gather_plan(n_rest), 3 * n_rest)

    def rest_weights(after):
        return dict(zip(REST, _split_wait("gather_rest_wait", gbufs, gsend, grecv, after, _gather_plan(n_rest))))

    flying = {}

    def early_grads(early):
        sends = list(_to_bf16_all([early[n] for n in REST]))
        lands = [pltpu.with_memory_space_constraint(lax.empty((N_SHARD - 1,) + v.shape[1:], BF16), pltpu.HBM)
                 for v in sends]
        ssend, srecv, sbufs, stoken = _split_start("scatter_rest_start", sends + lands, early[REST[0]],
                                                   _scatter_plan(n_rest), 3 * n_rest)
        flying.update(send=ssend, recv=srecv, bufs=sbufs)
        return stoken

    def w_in_grad(g_w_in):
        got = _swap_halves([g_w_in])[0]
        pair = _add_pair(g_w_in, got, c, "add_pair_w_in")
        land = pltpu.with_memory_space_constraint(lax.empty((N_SHARD - 1,) + pair.shape[1:], BF16), pltpu.HBM)
        wsend, wrecv, wbufs, wtoken = _split_start("scatter_w_in_start", [pair, land], got, _scatter_plan(1), 3)
        flying.update(w_send=wsend, w_recv=wrecv, w_bufs=wbufs, w_got=got)
        return wtoken

    small = {n: wts[n] for n in SMALL}
    small["norm_g"] = norm_g + gtoken[0, 0]
    loss, grad_x, grads = _local_step(x, p[0], loss_target, dict(w_in=w_in_g, conv_w=conv_g), small,
                                      rest_weights, early_grads, w_in_grad)
    g_w_in = grads["w_in"]
    landed = _split_wait("scatter_w_in_wait", flying["w_bufs"], flying["w_send"], flying["w_recv"], grad_x,
                         _scatter_plan(1))[1]
    w_in_half = _add_chips(g_w_in, flying["w_got"], landed, k, c, "add_chips_w_in")

    sbufs = _split_wait("scatter_rest_wait", flying["bufs"], flying["send"], flying["recv"], g_w_in,
                        _scatter_plan(n_rest))
    mine = _add_landed_all([grads[n] for n in REST], list(sbufs[n_rest:]), k)
    (w_in_sum,), theirs, small_sum = _final_exchange(
        [w_in_half], mine, _pack([grads[n] for n in SMALL] + [grads["conv_w"], loss]))
    g_sums = dict(zip(REST, zip(mine, theirs)))
    g_sums["w_in"] = (w_in_sum,)

    conv_shape = flat2(conv_w).shape
    small_shapes = [wts[n].shape for n in SMALL] + [(N_SHARD,) + conv_shape, (1,)]
    small_g = _unpack(small_sum, small_shapes)
    g_conv = lax.dynamic_index_in_dim(small_g[-2], k, axis=0, keepdims=False)

    outs = {}
    for n in BIG:
        vals = _adamw(flat2(wts[n]), g_sums[n], flat2(mom_m[n]), flat2(mom_v[n]), "adamw_" + n)
        for kind, val in zip(("grad", "delta", "new_m", "new_v"), vals):
            outs[kind, n] = val.reshape(wts[n].shape)
    names = SMALL + ("conv_w",)
    sm = _adamw(_pack([wts[n] for n in names]), (_pack(small_g[:len(SMALL)] + [g_conv]),),
                _pack([mom_m[n] for n in names]), _pack([mom_v[n] for n in names]), "adamw_small")
    sm_shapes = [wts[n].shape for n in names]
    for kind, val in zip(("grad", "delta", "new_m", "new_v"), sm):
        for n, piece in zip(names, _unpack(val, sm_shapes)):
            outs[kind, n] = piece
    return (small_g[-1][0], grad_x, *[outs[kind, n] for kind in ("grad", "delta", "new_m", "new_v") for n in WEIGHTS])
```

```python
import functools

import jax
import jax.numpy as jnp
from jax import lax
from jax.experimental import pallas as pl
from jax.experimental.pallas import tpu as pltpu

F32 = jnp.float32
BF16 = jnp.bfloat16
HIGHEST = lax.Precision.HIGHEST
MESH = pl.DeviceIdType.MESH

EPS = 1e-6
HEAD_DIM = 64
SSM_GROUPS = 4
D_STATE = 128
CONV_WIDTH = 4
CHUNK = 128
N_POOL = 4
LANES = 128
N_SHARD = 4

ADAM_LR = 0.001
ADAM_B1 = 0.9
ADAM_B2 = 0.999
ADAM_EPS = 1e-08
ADAM_WD = 0.01
ADAM_STEP = 10

BIG = ("w_in", "pool_mix_w", "w_branch_a", "w_branch_b", "w_out", "w_ple_gate", "w_ple_up")
REST = BIG[1:]
SMALL = ("norm_g", "conv_b", "dt_bias", "a_log", "d_skip", "gnorm_g", "pool_mix_b", "pool_scale",
         "ple_norm_g", "final_g")
WEIGHTS = ("norm_g", "w_in", "conv_w", "conv_b", "dt_bias", "a_log", "d_skip", "gnorm_g", "pool_mix_w",
           "pool_mix_b", "pool_scale", "w_branch_a", "w_branch_b", "w_out", "ple_norm_g", "w_ple_gate",
           "w_ple_up", "final_g")


def _params(sem=None, vmem_mb=56):
    kw = dict(vmem_limit_bytes=vmem_mb << 20)
    if sem is not None:
        kw["dimension_semantics"] = sem
    return pltpu.CompilerParams(**kw)


def _sigmoid(v):
    return 0.5 * jnp.tanh(0.5 * v) + 0.5


def _sigmoid_tail(v):
    return 1.0 / (1.0 + jnp.exp(-v))


def _softplus(v):
    return jnp.maximum(v, 0.0) + jnp.log1p(jnp.exp(-jnp.abs(v)))


def _bdot(a, b):
    return jnp.dot(a.astype(BF16), b.astype(BF16), preferred_element_type=F32)


def _bdot_nt(a, b):
    return lax.dot_general(a.astype(BF16), b.astype(BF16), (((1,), (1,)), ((), ())), preferred_element_type=F32)


def _bdot_tn(a, b):
    return lax.dot_general(a.astype(BF16), b.astype(BF16), (((0,), (0,)), ((), ())), preferred_element_type=F32)


def _col_block(col0, width):
    assert col0 % width == 0, (col0, width)
    return col0 // width


def _tile(n, cap, unit=8):
    if n <= cap:
        return n
    best = None
    for cand in range(unit, cap + 1, unit):
        if n % cand == 0:
            best = cand
    assert best is not None, (n, cap)
    return best


def _shift_down(v, j, row):
    return jnp.where(row >= j, pltpu.roll(v, j, 0), 0.0)


def _shift_up(v, j, row):
    n = v.shape[0]
    return jnp.where(row < n - j, pltpu.roll(v, n - j, 0), 0.0)


def _mm(a, w, name, tm=1024, tn=1024):
    t, k = a.shape
    n = w.shape[1]
    tm, tn = min(tm, t), min(tn, n)

    def body(a_ref, w_ref, o_ref):
        o_ref[...] = _bdot(a_ref[...], w_ref[...]).astype(BF16)

    return pl.pallas_call(
        body, name=name, grid=(t // tm, n // tn),
        in_specs=[pl.BlockSpec((tm, k), lambda i, j: (i, 0)), pl.BlockSpec((k, tn), lambda i, j: (0, j))],
        out_specs=pl.BlockSpec((tm, tn), lambda i, j: (i, j)),
        out_shape=jax.ShapeDtypeStruct((t, n), BF16),
        compiler_params=_params(("parallel", "parallel")))(a, w)


def _mm_nt(a, w, name, tm=1024, tk=1024):
    t, k = a.shape
    n = w.shape[0]
    tm, tk = min(tm, t), min(tk, k)

    def body(a_ref, w_ref, o_ref):
        kk = pl.program_id(1)
        part = _bdot_nt(a_ref[...], w_ref[...])

        @pl.when(kk == 0)
        def _():
            o_ref[...] = part

        @pl.when(kk > 0)
        def _():
            o_ref[...] += part

    return pl.pallas_call(
        body, name=name, grid=(t // tm, k // tk),
        in_specs=[pl.BlockSpec((tm, tk), lambda i, j: (i, j)), pl.BlockSpec((n, tk), lambda i, j: (0, j))],
        out_specs=pl.BlockSpec((tm, n), lambda i, j: (i, 0)),
        out_shape=jax.ShapeDtypeStruct((t, n), F32),
        compiler_params=_params(("parallel", "arbitrary")))(a, w)


def _mm_tn(a, b, name, tn=1024, tk=2048, col_blocks=False):
    t, m = a.shape
    n = b.shape[1]
    tn, tk = min(tn, n), min(tk, t)

    def body(a_ref, b_ref, o_ref):
        kk = pl.program_id(1)
        part = _bdot_tn(a_ref[...], b_ref[...])
        part = part[None] if col_blocks else part

        @pl.when(kk == 0)
        def _():
            o_ref[...] = part

        @pl.when(kk > 0)
        def _():
            o_ref[...] += part

    if col_blocks:
        out_spec = pl.BlockSpec((1, m, tn), lambda j, kk: (j, 0, 0))
        out_shape = jax.ShapeDtypeStruct((n // tn, m, tn), F32)
    else:
        out_spec = pl.BlockSpec((m, tn), lambda j, kk: (0, j))
        out_shape = jax.ShapeDtypeStruct((m, n), F32)
    return pl.pallas_call(
        body, name=name, grid=(n // tn, t // tk),
        in_specs=[pl.BlockSpec((tk, m), lambda j, kk: (kk, 0)), pl.BlockSpec((tk, tn), lambda j, kk: (kk, j))],
        out_specs=out_spec, out_shape=out_shape,
        compiler_params=_params(("parallel", "arbitrary")))(a, b)


def _w_in_pieces(d, di, dc, nh, shard_w):
    pgd = d // N_POOL
    o_dt, o_u = di + dc, di + dc + nh
    o_zp, o_ga, o_gb = o_u + d, o_u + 2 * d, o_u + 3 * d
    c_z, c_uz = 2 * d, 2 * d + di + dc
    runs = [(False, 0, o_ga, d), (False, d, o_gb, d), (False, c_z, 0, di + dc), (True, 0, o_dt, nh)]
    for g in range(N_POOL):
        runs.append((False, c_uz + 2 * g * pgd, o_u + g * pgd, pgd))
        runs.append((False, c_uz + (2 * g + 1) * pgd, o_zp + g * pgd, pgd))
    pieces = []
    for is_dt, dst, src, n in runs:
        while n > 0:
            k, off = divmod(src, shard_w)
            m = min(n, shard_w - off)
            pieces.append((is_dt, dst, k, off, m))
            dst, src, n = dst + m, src + m, n - m
    return pieces


def _regroup_w_in(w_sh, d, di, dc, nh, rb=256):
    _, rows, sw = w_sh.shape
    n_main = 4 * d + di + dc
    pieces = _w_in_pieces(d, di, dc, nh, sw)
    rb = min(rb, rows)

    def body(w_ref, main_ref, dt_ref):
        dt_ref[...] = jnp.zeros_like(dt_ref)
        for is_dt, dst, k, off, m in pieces:
            out = dt_ref if is_dt else main_ref
            out[:, dst:dst + m] = w_ref[k, :, off:off + m]

    return pl.pallas_call(
        body, name="regroup_w_in", grid=(rows // rb,),
        in_specs=[pl.BlockSpec((N_SHARD, rb, sw), lambda i: (0, i, 0))],
        out_specs=[pl.BlockSpec((rb, n_main), lambda i: (i, 0)), pl.BlockSpec((rb, LANES), lambda i: (i, 0))],
        out_shape=[jax.ShapeDtypeStruct((rows, n_main), w_sh.dtype), jax.ShapeDtypeStruct((rows, LANES), w_sh.dtype)],
        compiler_params=_params(("parallel",)))(w_sh)


def _ungroup_w_in(d_main, d_dt, d, di, dc, nh, rb=128):
    rows, n_main = d_main.shape
    sw = (n_main + nh) // N_SHARD
    pieces = _w_in_pieces(d, di, dc, nh, sw)
    rb = min(rb, rows)

    def body(main_ref, dt_ref, o_ref, ob_ref):
        for is_dt, dst, k, off, m in pieces:
            piece = (dt_ref if is_dt else main_ref)[:, dst:dst + m]
            o_ref[k, :, off:off + m] = piece
            ob_ref[k, :, off:off + m] = piece.astype(BF16)

    spec = pl.BlockSpec((N_SHARD, rb, sw), lambda i: (0, i, 0))
    return pl.pallas_call(
        body, name="ungroup_w_in", grid=(rows // rb,),
        in_specs=[pl.BlockSpec((rb, n_main), lambda i: (i, 0)), pl.BlockSpec((rb, LANES), lambda i: (i, 0))],
        out_specs=[spec, spec],
        out_shape=[jax.ShapeDtypeStruct((N_SHARD, rows, sw), F32), jax.ShapeDtypeStruct((N_SHARD, rows, sw), BF16)],
        compiler_params=_params(("parallel",)))(d_main, d_dt)


def _inproj(x2, norm_g, w_main, w_dt, tm=1024, tn=1024):
    t, d = x2.shape
    n = w_main.shape[1]
    tm, tn = min(tm, t), min(tn, n)

    def body(x_ref, g_ref, w_ref, wdt_ref, proj_ref, dt_ref, h_ref):
        @pl.when(pl.program_id(1) == 0)
        def _():
            xv = x_ref[...]
            r = lax.rsqrt(jnp.mean(xv * xv, axis=-1, keepdims=True) + EPS)
            h = (xv * r * g_ref[...]).astype(BF16)
            h_ref[...] = h
            dt_ref[...] = jnp.dot(h, wdt_ref[...].astype(BF16), preferred_element_type=F32)

        proj_ref[...] = jnp.dot(h_ref[...], w_ref[...].astype(BF16), preferred_element_type=F32)

    return pl.pallas_call(
        body, name="inproj", grid=(t // tm, n // tn),
        in_specs=[pl.BlockSpec((tm, d), lambda i, j: (i, 0)), pl.BlockSpec((1, d), lambda i, j: (0, 0)),
                  pl.BlockSpec((d, tn), lambda i, j: (0, j)), pl.BlockSpec((d, LANES), lambda i, j: (0, 0))],
        out_specs=[pl.BlockSpec((tm, tn), lambda i, j: (i, j)), pl.BlockSpec((tm, LANES), lambda i, j: (i, 0)),
                   pl.BlockSpec((tm, d), lambda i, j: (i, 0))],
        out_shape=[jax.ShapeDtypeStruct((t, n), F32), jax.ShapeDtypeStruct((t, LANES), F32),
                   jax.ShapeDtypeStruct((t, d), BF16)],
        compiler_params=_params(("parallel", "arbitrary")))(x2, norm_g, w_main, w_dt)


def _conv_w_spec(conv_w, cb, j_axis):
    sw = conv_w.shape[2]
    assert sw % cb == 0, (sw, cb)
    per = sw // cb
    return N_SHARD * per, pl.BlockSpec((1, CONV_WIDTH, cb), lambda *ij: (ij[j_axis] // per, 0, ij[j_axis] % per))


CONV_ROWS = 64
CONV_HALO = 8


def _conv_taps(x_ref, t0, rc):
    if t0 == 0:
        cur = x_ref[0:rc, :]
        row = lax.broadcasted_iota(jnp.int32, cur.shape, 0)
        return [cur] + [_shift_down(cur, j, row) for j in range(1, CONV_WIDTH)]
    ext = x_ref[t0 - CONV_HALO:t0 + rc, :]
    return [ext[CONV_HALO:]] + [pltpu.roll(ext, j, 0)[CONV_HALO:] for j in range(1, CONV_WIDTH)]


def _conv_weights(w_ref):
    return [w_ref[0, CONV_WIDTH - 1 - j:CONV_WIDTH - j, :] for j in range(CONV_WIDTH)]


def _conv_pre(taps, wts, bias):
    acc = bias + taps[0] * wts[0]
    for j in range(1, CONV_WIDTH):
        acc = acc + taps[j] * wts[j]
    return acc


def _conv_fwd(proj, conv_w, conv_b, nb, s, col0, cb=256):
    n_blk, w_spec = _conv_w_spec(conv_w, cb, 1)
    blk0 = _col_block(col0, cb)
    rc = min(CONV_ROWS, s)

    def body(x_ref, w_ref, b_ref, o_ref):
        wts, bias = _conv_weights(w_ref), b_ref[...]
        for t0 in range(0, s, rc):
            acc = _conv_pre(_conv_taps(x_ref, t0, rc), wts, bias)
            o_ref[t0:t0 + rc, :] = acc * _sigmoid(acc)

    return pl.pallas_call(
        body, name="conv_fwd", grid=(nb, n_blk),
        in_specs=[pl.BlockSpec((s, cb), lambda b, j: (b, blk0 + j)), w_spec, pl.BlockSpec((1, cb), lambda b, j: (0, j))],
        out_specs=pl.BlockSpec((s, cb), lambda b, j: (b, j)),
        out_shape=jax.ShapeDtypeStruct((nb * s, n_blk * cb), F32),
        compiler_params=_params(("parallel", "parallel")))(proj, conv_w, conv_b)


def _ssd_consts(di):
    r = lax.broadcasted_iota(jnp.int32, (CHUNK, CHUNK), 0)
    c = lax.broadcasted_iota(jnp.int32, (CHUNK, CHUNK), 1)
    tril = (r >= c).astype(F32)
    head = lax.broadcasted_iota(jnp.int32, (LANES, di), 0)
    chan = lax.broadcasted_iota(jnp.int32, (LANES, di), 1) // HEAD_DIM
    expand = (head == chan).astype(BF16)
    return tril, expand, expand.T


def _expand(v, e, terms=3):
    acc = None
    for _ in range(terms):
        vb = v.astype(BF16)
        part = jnp.dot(vb, e, preferred_element_type=F32)
        acc = part if acc is None else acc + part
        v = v - vb.astype(F32)
    return acc


def _head_sum(t, et, terms=2):
    acc = None
    for _ in range(terms):
        tb = t.astype(BF16)
        part = jnp.dot(tb, et, preferred_element_type=F32)
        acc = part if acc is None else acc + part
        t = t - tb.astype(F32)
    return acc


def _ssd_scalars(dtr_ref, dtb_ref, alog_ref, tri):
    dtpre = dtr_ref[...] + dtb_ref[...]
    dt = _softplus(dtpre)
    a_neg = -jnp.exp(alog_ref[...])
    a_dt = dt * a_neg
    a_cs = jnp.dot(tri, a_dt, precision=HIGHEST, preferred_element_type=F32)
    a_cst = lax.dot_general(a_dt, tri, (((0,), (1,)), ((), ())), precision=HIGHEST, preferred_element_type=F32)
    return dtpre, dt, a_neg, a_cs, a_cst


def _ssd_fwd(proj, xbc, dtraw, dtb, alog, dskx, gn, nb, s, di, z_col0):
    t = nb * s
    nc = s // CHUNK
    hpg = di // HEAD_DIM // SSM_GROUPS
    gw = di // SSM_GROUPS
    gn_w = SSM_GROUPS * D_STATE
    b_blk = _col_block(di, gn_w)
    z_blk = _col_block(z_col0, di)
    L, P, N = CHUNK, HEAD_DIM, D_STATE
    tril, expand, _ = _ssd_consts(di)

    def body(z_ref, x_ref, b_ref, c_ref, dtr_ref, dtb_ref, alog_ref, dskx_ref, gn_ref, tril_ref, e_ref,
             ypre_ref, yan_ref, hp_ref, st_ref, yd_ref, xdt_ref):
        @pl.when(pl.program_id(1) == 0)
        def _():
            st_ref[...] = jnp.zeros_like(st_ref)

        hp_ref[0] = st_ref[...]
        tri = tril_ref[...]
        _, dt, _, a_cs, a_cst = _ssd_scalars(dtr_ref, dtb_ref, alog_ref, tri)
        ev = e_ref[...]
        a_exp = _expand(a_cs, ev)
        xv = x_ref[...]
        xdt = xv * _expand(dt, ev, terms=2)
        xdt_ref[...] = xdt
        a_last = a_exp[L - 1:L, :]
        xe = xdt * jnp.exp(a_last - a_exp)
        ea = jnp.exp(a_exp)
        e_last = jnp.exp(a_last)
        lower = tri > 0.5
        for g in range(SSM_GROUPS):
            gs = slice(g * gw, (g + 1) * gw)
            bg = b_ref[:, g * N:(g + 1) * N].astype(BF16)
            cg = c_ref[:, g * N:(g + 1) * N].astype(BF16)
            gm = _bdot_nt(cg, bg)
            ht = st_ref[:, gs]
            ch = _bdot(cg, ht)
            for e in range(hpg):
                h = g * hpg + e
                hs = slice(h * P, (h + 1) * P)
                decay = jnp.where(lower, jnp.exp(a_cs[:, h:h + 1] - a_cst[h:h + 1, :]), 0.0)
                yd_ref[:, hs] = _bdot(gm * decay, xdt_ref[:, hs])
            st_ref[:, gs] = ht * e_last[:, gs] + _bdot_tn(bg, xe[:, gs])
            ypre = yd_ref[:, gs] + ea[:, gs] * ch + xv[:, gs] * dskx_ref[:, gs]
            ypre_ref[:, gs] = ypre
            zv = z_ref[:, gs]
            v = ypre * zv * _sigmoid(zv)
            r = lax.rsqrt(jnp.mean(v * v, axis=-1, keepdims=True) + EPS)
            yan_ref[:, gs] = (v * r * gn_ref[:, gs]).astype(BF16)

    row = lambda b, c: b * nc + c
    vec = lambda w: pl.BlockSpec((1, w), lambda b, c: (0, 0))
    return pl.pallas_call(
        body, name="ssd_fwd", grid=(nb, nc),
        in_specs=[pl.BlockSpec((L, di), lambda b, c: (row(b, c), z_blk)),
                  pl.BlockSpec((L, di), lambda b, c: (row(b, c), 0)),
                  pl.BlockSpec((L, gn_w), lambda b, c: (row(b, c), b_blk)),
                  pl.BlockSpec((L, gn_w), lambda b, c: (row(b, c), b_blk + 1)),
                  pl.BlockSpec((L, LANES), lambda b, c: (row(b, c), 0)),
                  vec(LANES), vec(LANES), vec(di), vec(di),
                  pl.BlockSpec((L, L), lambda b, c: (0, 0)),
                  pl.BlockSpec((LANES, di), lambda b, c: (0, 0))],
        out_specs=[pl.BlockSpec((L, di), lambda b, c: (row(b, c), 0)),
                   pl.BlockSpec((L, di), lambda b, c: (row(b, c), 0)),
                   pl.BlockSpec((1, N, di), lambda b, c: (row(b, c), 0, 0))],
        out_shape=[jax.ShapeDtypeStruct((t, di), F32), jax.ShapeDtypeStruct((t, di), BF16),
                   jax.ShapeDtypeStruct((nb * nc, N, di), F32)],
        scratch_shapes=[pltpu.VMEM((N, di), F32), pltpu.VMEM((L, di), F32), pltpu.VMEM((L, di), F32)],
        compiler_params=_params(("parallel", "arbitrary")))(
            proj, xbc, xbc, xbc, dtraw, dtb, alog, dskx, gn, tril, expand)


def _pool_sum(v, g, row, shift):
    s2 = v + shift(v, 1, row)
    s4 = s2 + shift(s2, 2, row)
    s8 = s4 + shift(s4, 4, row)
    s16 = s8 + shift(s8, 8, row)
    return jnp.where(g == 0, s2, jnp.where(g == 1, s4, jnp.where(g == 2, s8, s16)))


def _pool_count(g, row):
    return jnp.minimum(row + 1, jnp.left_shift(2, g)).astype(F32)


POOL_ROWS = 128
POOL_HALO = 16


def _roll_sum(v, g, step):
    n = v.shape[0]
    s2 = v + pltpu.roll(v, step % n, 0)
    s4 = s2 + pltpu.roll(s2, (2 * step) % n, 0)
    s8 = s4 + pltpu.roll(s4, (4 * step) % n, 0)
    s16 = s8 + pltpu.roll(s8, (8 * step) % n, 0)
    return jnp.where(g == 0, s2, jnp.where(g == 1, s4, jnp.where(g == 2, s8, s16)))


def _pool_trailing(u_ref, pgd, t0, rc, g):
    if t0 == 0:
        cur = u_ref[0:rc, :pgd]
        row = lax.broadcasted_iota(jnp.int32, cur.shape, 0)
        return cur, _pool_sum(cur, g, row, _shift_down), _pool_count(g, row)
    ext = u_ref[t0 - POOL_HALO:t0 + rc, :pgd]
    return ext[POOL_HALO:], _roll_sum(ext, g, 1)[POOL_HALO:], jnp.left_shift(2, g).astype(F32)


def _pool_fwd(proj, mix_w, mix_b, scale, nb, s, col0):
    pgd = mix_w.shape[-1]
    blk0 = _col_block(col0, 2 * pgd)
    rc = min(POOL_ROWS, s)

    def body(uz_ref, w_ref, b_ref, sc_ref, o_ref):
        g = pl.program_id(1)
        wv = w_ref[:, 0].reshape(pgd, pgd)
        for t0 in range(0, s, rc):
            u, win, cnt = _pool_trailing(uz_ref, pgd, t0, rc, g)
            zp = uz_ref[t0:t0 + rc, pgd:]
            mixed = _bdot(win / cnt - u, wv) + b_ref[...]
            o_ref[t0:t0 + rc, :] = (mixed * sc_ref[...] * zp * _sigmoid(zp)).astype(BF16)

    return pl.pallas_call(
        body, name="pool_fwd", grid=(nb, N_POOL),
        in_specs=[pl.BlockSpec((s, 2 * pgd), lambda b, g: (b, blk0 + g)),
                  pl.BlockSpec((N_SHARD, 1, pgd // N_SHARD, pgd), lambda b, g: (0, g, 0, 0)),
                  pl.BlockSpec((1, pgd), lambda b, g: (0, g)), pl.BlockSpec((1, pgd), lambda b, g: (0, g))],
        out_specs=pl.BlockSpec((s, pgd), lambda b, g: (b, g)),
        out_shape=jax.ShapeDtypeStruct((nb * s, N_POOL * pgd), BF16),
        compiler_params=_params(("parallel", "parallel")))(proj, mix_w, mix_b, scale)


def _mid_fwd(ya, yb, proj, col0, x2, p2, tgt, wo, wpg, wup, ple_g, final_g, tm=256):
    t, d = ya.shape
    tm = min(tm, t)
    blk = _col_block(col0, 2 * d)
    n_up, pdim, up_w = wup.shape

    def body(ya_ref, yb_ref, g_ref, x_ref, p_ref, tgt_ref, wo_ref, wpg_ref, wup_ref, pg_ref, g_fin_ref,
             merged_ref, hn_ref, dpre_ref, dpu_ref, x1_ref, dx2_ref, loss_ref, dg_ref):
        @pl.when(pl.program_id(0) == 0)
        def _():
            loss_ref[...] = jnp.zeros_like(loss_ref)
            dg_ref[...] = jnp.zeros_like(dg_ref)

        merged = (_sigmoid(g_ref[:, :d]) * ya_ref[...] + _sigmoid(g_ref[:, d:]) * yb_ref[...]).astype(BF16)
        merged_ref[...] = merged
        x1 = x_ref[...] + jnp.dot(merged, wo_ref[...], preferred_element_type=F32)
        x1_ref[...] = x1
        r1 = lax.rsqrt(jnp.mean(x1 * x1, axis=-1, keepdims=True) + EPS)
        hn = (x1 * r1 * pg_ref[...]).astype(BF16)
        hn_ref[...] = hn
        gate = _sigmoid(jnp.dot(hn, wpg_ref[...], preferred_element_type=F32))
        pb = p_ref[...].astype(BF16)
        pu = jnp.concatenate([jnp.dot(pb, wup_ref[j], preferred_element_type=F32) for j in range(n_up)], axis=1)
        x2 = x1 + gate * pu
        r = lax.rsqrt(jnp.mean(x2 * x2, axis=-1, keepdims=True) + EPS)
        xn = x2 * r
        fg = g_fin_ref[...]
        err = xn * fg - tgt_ref[...]
        loss_ref[...] += 0.5 * jnp.sum(jnp.mean(err * err, axis=-1, keepdims=True))
        dy = err * (1.0 / d)
        dg_ref[...] += jnp.sum(dy * xn, axis=0, keepdims=True)
        dxn = dy * fg
        dx2 = r * (dxn - xn * jnp.mean(dxn * xn, axis=-1, keepdims=True))
        dx2_ref[...] = dx2
        dpre_ref[...] = (dx2 * pu * gate * (1.0 - gate)).astype(BF16)
        dpu_ref[...] = (dx2 * gate).astype(BF16)

    row = pl.BlockSpec((tm, d), lambda i: (i, 0))
    vec = pl.BlockSpec((1, d), lambda i: (0, 0))
    whole = lambda a: pl.BlockSpec(a.shape, lambda i: (0,) * a.ndim)
    return pl.pallas_call(
        body, name="mid_fwd", grid=(t // tm,),
        in_specs=[row, row, pl.BlockSpec((tm, 2 * d), lambda i: (i, blk)), row,
                  pl.BlockSpec((tm, pdim), lambda i: (i, 0)), row, whole(wo), whole(wpg), whole(wup), vec, vec],
        out_specs=[row] * 6 + [pl.BlockSpec((1, LANES), lambda i: (0, 0)), vec],
        out_shape=[jax.ShapeDtypeStruct((t, d), BF16)] * 4 + [jax.ShapeDtypeStruct((t, d), F32)] * 2 + [
            jax.ShapeDtypeStruct((1, LANES), F32), jax.ShapeDtypeStruct((1, d), F32)],
        compiler_params=_params(("arbitrary",)))(ya, yb, proj, x2, p2, tgt, wo, wpg, wup, ple_g, final_g)


def _rms_grad(xv, dh, g):
    r = lax.rsqrt(jnp.mean(xv * xv, axis=-1, keepdims=True) + EPS)
    xn = xv * r
    dd = dh * g
    return r * (dd - xn * jnp.mean(dd * xn, axis=-1, keepdims=True)), jnp.sum(dh * xn, axis=0, keepdims=True)


def _mid_bwd(dpre, dx2, x1, ya, yb, proj, col0, wpg, wo, ple_g, n_cols, tm=256):
    t, d = ya.shape
    tm = min(tm, t)
    blk = _col_block(col0, 2 * d)

    def body(dpre_ref, dx2_ref, x1_ref, ya_ref, yb_ref, g_ref, wpg_ref, wo_ref, pg_ref,
             dx1_ref, dya_ref, dyb_ref, dg_ref, dpg_ref):
        @pl.when(pl.program_id(0) == 0)
        def _():
            dpg_ref[...] = jnp.zeros_like(dpg_ref)

        dhn = _bdot_nt(dpre_ref[...], wpg_ref[...])
        dx, dpg = _rms_grad(x1_ref[...], dhn, pg_ref[...])
        dpg_ref[...] += dpg
        dx1 = dx2_ref[...] + dx
        dx1_ref[...] = dx1
        dm_v = _bdot_nt(dx1, wo_ref[...])
        sa = _sigmoid(g_ref[:, :d])
        sb = _sigmoid(g_ref[:, d:])
        dya_ref[...] = (dm_v * sa).astype(BF16)
        dyb_ref[...] = (dm_v * sb).astype(BF16)
        dg_ref[:, :d] = (dm_v * ya_ref[...] * sa * (1.0 - sa)).astype(BF16)
        dg_ref[:, d:] = (dm_v * yb_ref[...] * sb * (1.0 - sb)).astype(BF16)

    row = pl.BlockSpec((tm, d), lambda i: (i, 0))
    vec = pl.BlockSpec((1, d), lambda i: (0, 0))
    gspec = pl.BlockSpec((tm, 2 * d), lambda i: (i, blk))
    whole = lambda a: pl.BlockSpec(a.shape, lambda i: (0,) * a.ndim)
    return pl.pallas_call(
        body, name="mid_bwd", grid=(t // tm,),
        in_specs=[row, row, row, row, row, gspec, whole(wpg), whole(wo), vec],
        out_specs=[row, row, row, gspec, vec],
        out_shape=[jax.ShapeDtypeStruct((t, d), F32), jax.ShapeDtypeStruct((t, d), BF16),
                   jax.ShapeDtypeStruct((t, d), BF16), jax.ShapeDtypeStruct((t, n_cols), BF16),
                   jax.ShapeDtypeStruct((1, d), F32)],
        compiler_params=_params(("arbitrary",)))(dpre, dx2, x1, ya, yb, proj, wpg, wo, ple_g)


def _in_bwd(dproj, w_main, ddt, w_dt, x2, dx1, norm_g, after, tm=1024, tk=1024):
    t, k = dproj.shape
    d = x2.shape[1]
    tm, tk = min(tm, t), min(tk, k)
    nk = k // tk

    def body(a_ref, w_ref, ddt_ref, wdt_ref, x_ref, dres_ref, g_ref, _, gx_ref, dg_ref, acc_ref):
        kk = pl.program_id(1)

        @pl.when((pl.program_id(0) == 0) & (kk == 0))
        def _():
            dg_ref[...] = jnp.zeros_like(dg_ref)

        part = _bdot_nt(a_ref[...], w_ref[...])

        @pl.when(kk == 0)
        def _():
            acc_ref[...] = part

        @pl.when(kk > 0)
        def _():
            acc_ref[...] += part

        @pl.when(kk == nk - 1)
        def _():
            dh = acc_ref[...] + _bdot_nt(ddt_ref[...], wdt_ref[...])
            dx, dg = _rms_grad(x_ref[...], dh, g_ref[...])
            dg_ref[...] += dg
            gx_ref[...] = dres_ref[...] + dx

    row = pl.BlockSpec((tm, d), lambda i, j: (i, 0))
    vec = pl.BlockSpec((1, d), lambda i, j: (0, 0))
    return pl.pallas_call(
        body, name="in_bwd", grid=(t // tm, nk),
        in_specs=[pl.BlockSpec((tm, tk), lambda i, j: (i, j)), pl.BlockSpec((d, tk), lambda i, j: (0, j)),
                  pl.BlockSpec((tm, LANES), lambda i, j: (i, 0)), pl.BlockSpec((d, LANES), lambda i, j: (0, 0)),
                  row, row, vec, pl.BlockSpec((8, LANES), lambda i, j: (0, 0))],
        out_specs=[row, vec],
        out_shape=[jax.ShapeDtypeStruct((t, d), F32), jax.ShapeDtypeStruct((1, d), F32)],
        scratch_shapes=[pltpu.VMEM((tm, d), F32)],
        compiler_params=_params(("arbitrary", "arbitrary")))(dproj, w_main, ddt, w_dt, x2, dx1, norm_g, after)


def _pool_bwd(proj, dyb, dproj, mix_w, mix_b, scale, nb, s, col0):
    pgd = mix_w.shape[-1]
    blk0 = _col_block(col0, 2 * pgd)
    rc = min(POOL_ROWS, s)

    def body(uz_ref, dy_ref, _, w_ref, b_ref, sc_ref, duz_ref, dw_ref, db_ref, dsc_ref, dpn_ref, dwacc_ref):
        g = pl.program_id(0)

        @pl.when(pl.program_id(1) == 0)
        def _():
            dw_ref[...] = jnp.zeros_like(dw_ref)
            db_ref[...] = jnp.zeros_like(db_ref)
            dsc_ref[...] = jnp.zeros_like(dsc_ref)

        wv = w_ref[:, 0].reshape(pgd, pgd)
        sc = sc_ref[...]
        fold = lambda v: v.reshape(rc // 8, 8, pgd).sum(axis=0)
        db8 = jnp.zeros((8, pgd), F32)
        dsc8 = jnp.zeros((8, pgd), F32)
        dwacc_ref[...] = jnp.zeros_like(dwacc_ref)
        for t0 in range(0, s, rc):
            u, win, cnt = _pool_trailing(uz_ref, pgd, t0, rc, g)
            pooled = win / cnt - u
            zp = uz_ref[t0:t0 + rc, pgd:]
            mixed = _bdot(pooled, wv) + b_ref[...]
            sg = _sigmoid(zp)
            sz = zp * sg
            dy = dy_ref[t0:t0 + rc, :]
            dsc8 = dsc8 + fold(dy * mixed * sz)
            dmixed = dy * sc * sz
            db8 = db8 + fold(dmixed)
            dwacc_ref[...] += _bdot_tn(pooled, dmixed)
            dpn_ref[t0:t0 + rc, :] = _bdot_nt(dmixed, wv) / cnt
            duz_ref[t0:t0 + rc, pgd:] = (dy * mixed * sc * sg * (1.0 + zp * (1.0 - sg))).astype(BF16)
        dsc_ref[...] += jnp.sum(dsc8, axis=0, keepdims=True)
        db_ref[...] += jnp.sum(db8, axis=0, keepdims=True)
        dw_ref[:, 0] += dwacc_ref[...].reshape(N_SHARD, pgd // N_SHARD, pgd)
        for t0 in range(0, s, rc):
            if t0 + rc < s:
                n = rc + POOL_HALO
                win = dpn_ref[t0:t0 + n, :]
                cur, lead = win[:rc], _roll_sum(win, g, n - 1)[:rc]
            else:
                cur = dpn_ref[t0:t0 + rc, :]
                row = lax.broadcasted_iota(jnp.int32, cur.shape, 0)
                lead = _pool_sum(cur, g, row, _shift_up)
            if t0 == 0:
                cnt = _pool_count(g, lax.broadcasted_iota(jnp.int32, cur.shape, 0))
            else:
                cnt = jnp.left_shift(2, g).astype(F32)
            duz_ref[t0:t0 + rc, :pgd] = (lead - cur * cnt).astype(BF16)

    uz = pl.BlockSpec((s, 2 * pgd), lambda g, b: (b, blk0 + g))
    vec = pl.BlockSpec((1, pgd), lambda g, b: (0, g))
    wspec = pl.BlockSpec((N_SHARD, 1, pgd // N_SHARD, pgd), lambda g, b: (0, g, 0, 0))
    return pl.pallas_call(
        body, name="pool_bwd", grid=(N_POOL, nb),
        in_specs=[uz, pl.BlockSpec((s, pgd), lambda g, b: (b, g)), pl.BlockSpec(memory_space=pl.ANY), wspec, vec, vec],
        out_specs=[uz, wspec, vec, vec],
        out_shape=[jax.ShapeDtypeStruct(dproj.shape, dproj.dtype), jax.ShapeDtypeStruct(mix_w.shape, F32),
                   jax.ShapeDtypeStruct(mix_b.shape, F32), jax.ShapeDtypeStruct(scale.shape, F32)],
        scratch_shapes=[pltpu.VMEM((s, pgd), F32), pltpu.VMEM((pgd, pgd), F32)],
        input_output_aliases={2: 0},
        compiler_params=_params(("parallel", "arbitrary")))(proj, dyb, dproj, mix_w, mix_b, scale)


def _ssd_bwd(dyan, ypre, proj, xbc, dtraw, hp, dproj, dtb, alog, dskx, gn, nb, s, di, z_col0):
    t = nb * s
    nc = s // CHUNK
    hpg = di // HEAD_DIM // SSM_GROUPS
    gw = di // SSM_GROUPS
    gn_w = SSM_GROUPS * D_STATE
    dc = di + 2 * gn_w
    b_blk = _col_block(di, gn_w)
    z_blk = _col_block(z_col0, di)
    L, P, N = CHUNK, HEAD_DIM, D_STATE
    tril, expand, expand_t = _ssd_consts(di)

    def body(dy_ref, ypre_ref, z_ref, x_ref, b_ref, c_ref, dtr_ref, hp_ref, _, dtb_ref, alog_ref, dskx_ref, gn_ref,
             tril_ref, e_ref, et_ref, dz_ref, ddt_ref, dxbc_ref, dgn_ref, ddsk_ref, dalog_ref, ddtb_ref,
             dst_ref, dyp_ref, xdt_ref, dxm_ref, t1_ref, t3_ref, aux_ref):
        @pl.when((pl.program_id(0) == 0) & (pl.program_id(1) == 0))
        def _():
            dgn_ref[...] = jnp.zeros_like(dgn_ref)
            ddsk_ref[...] = jnp.zeros_like(ddsk_ref)
            dalog_ref[...] = jnp.zeros_like(dalog_ref)
            ddtb_ref[...] = jnp.zeros_like(ddtb_ref)

        @pl.when(pl.program_id(1) == 0)
        def _():
            dst_ref[...] = jnp.zeros_like(dst_ref)

        tri = tril_ref[...]
        dtpre, dt, a_neg, a_cs, a_cst = _ssd_scalars(dtr_ref, dtb_ref, alog_ref, tri)
        ev = e_ref[...]
        a_exp = _expand(a_cs, ev)
        dt_exp = _expand(dt, ev, terms=2)
        xv = x_ref[...]
        xdt = xv * dt_exp
        xdt_ref[...] = xdt
        a_last = a_exp[L - 1:L, :]
        dte = jnp.exp(a_last - a_exp)
        xe = xdt * dte
        ea = jnp.exp(a_exp)
        e_last = jnp.exp(a_last)
        lower = tri > 0.5
        aux_ref[...] = jnp.zeros_like(aux_ref)
        for g in range(SSM_GROUPS):
            gs = slice(g * gw, (g + 1) * gw)
            zv = z_ref[:, gs]
            yp = ypre_ref[:, gs]
            sg = _sigmoid(zv)
            sz = zv * sg
            vg = yp * sz
            r = lax.rsqrt(jnp.mean(vg * vg, axis=-1, keepdims=True) + EPS)
            vn = vg * r
            dyg = dy_ref[:, gs]
            dgn_ref[:, gs] += jnp.sum(dyg * vn, axis=0, keepdims=True)
            dvn = dyg * gn_ref[:, gs]
            dv = r * (dvn - vn * jnp.mean(dvn * vn, axis=-1, keepdims=True))
            dy = dv * sz
            dyp_ref[:, gs] = dy
            dz_ref[:, gs] = (dv * yp * sg * (1.0 + zv * (1.0 - sg))).astype(BF16)
            bg = b_ref[:, g * N:(g + 1) * N].astype(BF16)
            cg = c_ref[:, g * N:(g + 1) * N].astype(BF16)
            gm = _bdot_nt(cg, bg)
            ht = hp_ref[0, :, gs]
            dht = dst_ref[:, gs]
            bds = _bdot(bg, dht)
            dye = dy * ea[:, gs]
            xe_g = xe[:, gs]
            dcg = _bdot_nt(dye, ht)
            dbg = _bdot_nt(xe_g, dht)
            dst_ref[:, gs] = e_last[:, gs] * dht + _bdot_tn(cg, dye)
            dgm = jnp.zeros((L, L), F32)
            for e in range(hpg):
                h = g * hpg + e
                hs = slice(h * P, (h + 1) * P)
                decay = jnp.where(lower, jnp.exp(a_cs[:, h:h + 1] - a_cst[h:h + 1, :]), 0.0)
                dy_h = dyp_ref[:, hs]
                dgm = dgm + _bdot_nt(dy_h, xdt_ref[:, hs]) * decay
                dxm_ref[:, hs] = _bdot_tn(gm * decay, dy_h)
            dxbc_ref[:, di + g * N:di + (g + 1) * N] = dbg + _bdot_tn(dgm, cg)
            dxbc_ref[:, di + gn_w + g * N:di + gn_w + (g + 1) * N] = dcg + _bdot(dgm, bg)
            dxm = dxm_ref[:, gs]
            x_g = xv[:, gs]
            dskx = dskx_ref[:, gs]
            xeb = xe_g * bds
            dxdt = dxm + dte[:, gs] * bds
            dxbc_ref[:, gs] = dxdt * dt_exp[:, gs] + dy * dskx
            each = ea[:, gs] * _bdot(cg, ht)
            y_diag = yp - x_g * dskx - each
            rnd = lambda v: v.astype(BF16).astype(F32)
            t1_ref[:, gs] = rnd(dy) * y_diag + dy * each - rnd(xdt[:, gs]) * dxm - xeb
            t3_ref[:, gs] = dxdt * x_g
            aux_ref[0:1, gs] = jnp.sum(dht * ht, axis=0, keepdims=True)
            aux_ref[1:2, gs] = jnp.sum(dy * x_g, axis=0, keepdims=True)
            aux_ref[2:3, gs] = jnp.sum(xeb, axis=0, keepdims=True)
        etv = et_ref[...]
        aux = _head_sum(aux_ref[...], etv)
        rowi = lax.broadcasted_iota(jnp.int32, (L, LANES), 0)
        end = aux[2:3, :] + aux[0:1, :] * jnp.exp(a_cs[L - 1:L, :])
        da = _head_sum(t1_ref[...], etv, terms=3) + jnp.where(rowi == L - 1, end, 0.0)
        rc = lax.dot_general(tri, da, (((0,), (0,)), ((), ())), precision=HIGHEST, preferred_element_type=F32)
        ddt = a_neg * rc + _head_sum(t3_ref[...], etv, terms=1)
        ddtraw = ddt * _sigmoid_tail(dtpre)
        ddt_ref[...] = ddtraw.astype(BF16)
        ddtb_ref[...] += jnp.sum(ddtraw, axis=0, keepdims=True)
        dalog_ref[...] += jnp.sum(dt * rc, axis=0, keepdims=True) * a_neg
        ddsk_ref[...] += aux[1:2, :]

    row = lambda b, c: b * nc + (nc - 1 - c)
    full = lambda w: pl.BlockSpec((L, w), lambda b, c: (row(b, c), 0))
    zspec = pl.BlockSpec((L, di), lambda b, c: (row(b, c), z_blk))
    vec = lambda w: pl.BlockSpec((1, w), lambda b, c: (0, 0))
    slab = lambda shape: pltpu.VMEM(shape, F32)
    return pl.pallas_call(
        body, name="ssd_bwd", grid=(nb, nc),
        in_specs=[full(di), full(di), zspec, full(di),
                  pl.BlockSpec((L, gn_w), lambda b, c: (row(b, c), b_blk)),
                  pl.BlockSpec((L, gn_w), lambda b, c: (row(b, c), b_blk + 1)),
                  full(LANES),
                  pl.BlockSpec((1, N, di), lambda b, c: (row(b, c), 0, 0)),
                  pl.BlockSpec(memory_space=pl.ANY),
                  vec(LANES), vec(LANES), vec(di), vec(di),
                  pl.BlockSpec((L, L), lambda b, c: (0, 0)),
                  pl.BlockSpec((LANES, di), lambda b, c: (0, 0)),
                  pl.BlockSpec((di, LANES), lambda b, c: (0, 0))],
        out_specs=[zspec, full(LANES), full(dc), vec(di), vec(LANES), vec(LANES), vec(LANES)],
        out_shape=[jax.ShapeDtypeStruct(dproj.shape, dproj.dtype), jax.ShapeDtypeStruct((t, LANES), BF16),
                   jax.ShapeDtypeStruct((t, dc), F32), jax.ShapeDtypeStruct((1, di), F32),
                   jax.ShapeDtypeStruct((1, LANES), F32), jax.ShapeDtypeStruct((1, LANES), F32),
                   jax.ShapeDtypeStruct((1, LANES), F32)],
        scratch_shapes=[slab((N, di)), slab((L, di)), slab((L, di)), slab((L, di)), slab((L, di)), slab((L, di)),
                        slab((8, di))],
        input_output_aliases={8: 0},
        compiler_params=_params(("arbitrary", "arbitrary")))(
            dyan, ypre, proj, xbc, xbc, xbc, dtraw, hp, dproj, dtb, alog, dskx, gn, tril, expand, expand_t)


def _conv_bwd(proj, dxbc, dproj, conv_w, conv_b, nb, s, col0, cb=256):
    n_blk, w_spec = _conv_w_spec(conv_w, cb, 0)
    blk0 = _col_block(col0, cb)
    rc = min(CONV_ROWS, s)

    def body(x_ref, dy_ref, _, w_ref, b_ref, dx_ref, dw_ref, db_ref, dacc_ref):
        @pl.when(pl.program_id(1) == 0)
        def _():
            dw_ref[...] = jnp.zeros_like(dw_ref)
            db_ref[...] = jnp.zeros_like(db_ref)

        wts, bias = _conv_weights(w_ref), b_ref[...]
        fold = lambda v: v.reshape(rc // 8, 8, cb).sum(axis=0)
        db8 = jnp.zeros((8, cb), F32)
        dw8 = [jnp.zeros((8, cb), F32) for _ in range(CONV_WIDTH)]
        for t0 in range(0, s, rc):
            taps = _conv_taps(x_ref, t0, rc)
            acc = _conv_pre(taps, wts, bias)
            sg = _sigmoid(acc)
            dacc = dy_ref[t0:t0 + rc, :] * sg * (1.0 + acc * (1.0 - sg))
            dacc_ref[t0:t0 + rc, :] = dacc
            db8 = db8 + fold(dacc)
            dw8 = [dw8[j] + fold(dacc * taps[j]) for j in range(CONV_WIDTH)]
        db_ref[...] += jnp.sum(db8, axis=0, keepdims=True)
        for j in range(CONV_WIDTH):
            dw_ref[0, CONV_WIDTH - 1 - j:CONV_WIDTH - j, :] += jnp.sum(dw8[j], axis=0, keepdims=True)
        for t0 in range(0, s, rc):
            if t0 + rc < s:
                n = rc + CONV_HALO
                win = dacc_ref[t0:t0 + n, :]
                ups = [win[:rc]] + [pltpu.roll(win, n - j, 0)[:rc] for j in range(1, CONV_WIDTH)]
            else:
                cur = dacc_ref[t0:t0 + rc, :]
                row = lax.broadcasted_iota(jnp.int32, cur.shape, 0)
                ups = [cur] + [_shift_up(cur, j, row) for j in range(1, CONV_WIDTH)]
            dx = ups[0] * wts[0]
            for j in range(1, CONV_WIDTH):
                dx = dx + ups[j] * wts[j]
            dx_ref[t0:t0 + rc, :] = dx.astype(BF16)

    return pl.pallas_call(
        body, name="conv_bwd", grid=(n_blk, nb),
        in_specs=[pl.BlockSpec((s, cb), lambda j, b: (b, blk0 + j)), pl.BlockSpec((s, cb), lambda j, b: (b, j)),
                  pl.BlockSpec(memory_space=pl.ANY), w_spec, pl.BlockSpec((1, cb), lambda j, b: (0, j))],
        out_specs=[pl.BlockSpec((s, cb), lambda j, b: (b, blk0 + j)), w_spec, pl.BlockSpec((1, cb), lambda j, b: (0, j))],
        out_shape=[jax.ShapeDtypeStruct(dproj.shape, dproj.dtype), jax.ShapeDtypeStruct(conv_w.shape, F32),
                   jax.ShapeDtypeStruct(conv_b.shape, F32)],
        scratch_shapes=[pltpu.VMEM((s, cb), F32)],
        input_output_aliases={2: 0},
        compiler_params=_params(("parallel", "arbitrary")))(proj, dxbc, dproj, conv_w, conv_b)


def _local_step(x, p, tgt, wg, small, rest_weights, early_grads, w_in_grad):
    nb, s, d = x.shape
    t = nb * s
    gn_w = SSM_GROUPS * D_STATE
    dc = N_SHARD * wg["conv_w"].shape[2]
    di = dc - 2 * gn_w
    nh = di // HEAD_DIM
    pgd = d // N_POOL
    x2 = x.reshape(t, d)
    p2 = p.reshape(t, p.shape[-1])
    tgt2 = tgt.reshape(t, d)

    w_main, w_dt = _regroup_w_in(wg["w_in"], d, di, dc, nh)
    c_g, c_z, c_xbc, c_uz = 0, 2 * d, 2 * d + di, 2 * d + di + dc
    n_main = w_main.shape[1]

    pad_h = lambda v: jnp.pad(v.reshape(1, nh).astype(F32), ((0, 0), (0, LANES - nh)))
    dtb, alog = pad_h(small["dt_bias"]), pad_h(small["a_log"])
    dskx = jnp.repeat(small["d_skip"].reshape(1, nh).astype(F32), HEAD_DIM, axis=1)
    vec = lambda v: v.reshape(1, -1).astype(F32)
    norm_g, gn, conv_b = vec(small["norm_g"]), vec(small["gnorm_g"]), vec(small["conv_b"])
    mix_b, scale = vec(small["pool_mix_b"]), vec(small["pool_scale"])
    ple_g, final_g = vec(small["ple_norm_g"]), vec(small["final_g"])
    conv_w = wg["conv_w"]

    wide = _tile(n_main, 2304, LANES)
    proj, dtraw, h = _inproj(x2, norm_g, w_main, w_dt, tn=wide)
    xbc = _conv_fwd(proj, conv_w, conv_b, nb, s, c_xbc)
    ypre, yan, hp = _ssd_fwd(proj, xbc, dtraw, dtb, alog, dskx, gn, nb, s, di, c_z)
    wr = rest_weights(yan)
    mix_w = wr["pool_mix_w"].reshape(N_SHARD, N_POOL, pgd // N_SHARD, pgd)
    rows = lambda v: v.reshape(-1, v.shape[-1])
    wa, wb, wo, wpg = rows(wr["w_branch_a"]), rows(wr["w_branch_b"]), rows(wr["w_out"]), rows(wr["w_ple_gate"])
    wup = wr["w_ple_up"]
    ybp = _pool_fwd(proj, mix_w, mix_b, scale, nb, s, c_uz)
    ya = _mm(yan, wa, "branch_a")
    yb = _mm(ybp, wb, "branch_b")
    merged, hn, dpre, dpu, x1, dx2, loss, d_final_g = _mid_fwd(
        ya, yb, proj, c_g, x2, p2, tgt2, wo, wpg, wup, ple_g, final_g)

    d_wpg = _mm_tn(hn, dpre, "d_w_ple_gate")
    d_wup = _mm_tn(p2, dpu, "d_w_ple_up", tn=wup.shape[-1], col_blocks=True)
    dx1, dya, dyb, dproj, d_ple_g = _mid_bwd(dpre, dx2, x1, ya, yb, proj, c_g, wpg, wo, ple_g, n_main)
    d_wo = _mm_tn(merged, dx1, "d_w_out")
    d_wa = _mm_tn(yan, dya, "d_w_branch_a", tk=1024)
    d_wb = _mm_tn(ybp, dyb, "d_w_branch_b")
    dyan = _mm_nt(dya, wa, "d_y_a")
    dybp = _mm_nt(dyb, wb, "d_y_b")
    dproj, d_mix_w, d_mix_b, d_scale = _pool_bwd(proj, dybp, dproj, mix_w, mix_b, scale, nb, s, c_uz)
    shard_major = lambda v: v.reshape(N_SHARD, v.shape[0] // N_SHARD, v.shape[1])
    early = dict(pool_mix_w=d_mix_w.reshape(N_SHARD, pgd, pgd), w_branch_a=shard_major(d_wa),
                 w_branch_b=shard_major(d_wb), w_out=shard_major(d_wo), w_ple_gate=shard_major(d_wpg),
                 w_ple_up=d_wup)
    token = early_grads(early)
    dproj, ddt, dxbc, d_gn, d_dsk, d_alog, d_dtb = _ssd_bwd(
        dyan, ypre, proj, xbc, dtraw, hp, dproj, dtb + token[0:1, 0:1], alog, dskx, gn, nb, s, di, c_z)
    dproj, d_conv_w, d_conv_b = _conv_bwd(proj, dxbc, dproj, conv_w, conv_b, nb, s, c_xbc)
    d_wmain = _mm_tn(h, dproj, "d_w_in", tk=2048)
    d_wdt = _mm_tn(h, ddt, "d_w_dt")
    d_w_in, d_w_in_bf16 = _ungroup_w_in(d_wmain, d_wdt, d, di, dc, nh)
    token = w_in_grad(d_w_in, d_w_in_bf16)
    gx, d_norm_g = _in_bwd(dproj, w_main, ddt, w_dt, x2, dx1, norm_g, token, tk=_tile(n_main, 1536, LANES))

    grads = dict(norm_g=d_norm_g, w_in=d_w_in, conv_w=d_conv_w, conv_b=d_conv_b, dt_bias=d_dtb[:, :nh],
                 a_log=d_alog[:, :nh], d_skip=d_dsk[:, :nh], gnorm_g=d_gn, pool_mix_b=d_mix_b, pool_scale=d_scale,
                 ple_norm_g=d_ple_g, final_g=d_final_g, **early)
    return loss[0, 0], gx.reshape(nb, s, d), grads


def _place():
    return lax.axis_index("x"), lax.axis_index("y"), lax.axis_index("c")


def _other_chips(x, y):
    return [(1 - x, y), (x, 1 - y), (1 - x, 1 - y)]


def _halves(c, rows, align):
    rh = rows // 2
    assert rows % 2 == 0 and rh % align == 0, (rows, align)
    return (pl.ds(pl.multiple_of(c * rh, align), rh), pl.ds(pl.multiple_of((1 - c) * rh, align), rh))


HBM = pl.BlockSpec(memory_space=pl.ANY)


def _into_slot(w2, k, dtype, name):
    rows, cols = w2.shape
    rb = _tile(rows, 256)

    def body(k_ref, w_ref, o_ref):
        o_ref[0] = w_ref[...].astype(dtype)

    return pl.pallas_call(
        body, name=name,
        grid_spec=pltpu.PrefetchScalarGridSpec(
            num_scalar_prefetch=1, grid=(rows // rb,),
            in_specs=[pl.BlockSpec((rb, cols), lambda i, k_ref: (i, 0))],
            out_specs=pl.BlockSpec((1, rb, cols), lambda i, k_ref: (k_ref[0], i, 0))),
        out_shape=jax.ShapeDtypeStruct((N_SHARD, rows, cols), dtype),
        compiler_params=_params(("parallel",)))(k.reshape(1), w2)


def _gather_weights(split, whole):
    n_split, n_all = len(split), len(split) + len(whole)

    def body(*refs):
        bufs = refs[n_all:2 * n_all]
        send_sems, recv_sems = refs[2 * n_all:]
        x, y, c = _place()
        k, k_x, k_y, k_d = 2 * x + y, 2 * (1 - x) + y, 2 * x + (1 - y), 2 * (1 - x) + (1 - y)
        x_nb, y_nb, sib = (1 - x, y, c), (x, 1 - y, c), (x, y, 1 - c)

        def copy(idx, block, to):
            return pltpu.make_async_remote_copy(src_ref=block, dst_ref=block, send_sem=send_sems.at[idx],
                                                recv_sem=recv_sems.at[idx], device_id=to, device_id_type=MESH)

        started = []

        def start(idx, block, to):
            started.append(copy(idx, block, to))
            started[-1].start()

        for i in range(n_split):
            buf, s0 = bufs[i], 8 * i
            rh = buf.shape[1] // 2
            rq = rh // 2
            assert buf.shape[1] == 4 * rq and rq % 16 == 0, buf.shape

            def rows(core, part):
                lo = core * rh + (rq if part == "bottom" else 0)
                return pl.ds(pl.multiple_of(lo, 16), rh if part == "all" else rq)

            start(s0 + 0, buf.at[k, rows(c, "all")], x_nb)
            start(s0 + 1, buf.at[k, rows(c, "all")], y_nb)
            copy(s0 + 0, buf.at[k_x, rows(c, "all")], x_nb).wait_recv()
            start(s0 + 2, buf.at[k_x, rows(c, "top")], y_nb)
            start(s0 + 4, buf.at[k_x, rows(c, "all")], sib)
            copy(s0 + 1, buf.at[k_y, rows(c, "all")], y_nb).wait_recv()
            start(s0 + 3, buf.at[k_y, rows(c, "bottom")], x_nb)
            start(s0 + 5, buf.at[k_y, rows(c, "all")], sib)
            copy(s0 + 2, buf.at[k_d, rows(c, "top")], y_nb).wait_recv()
            start(s0 + 6, buf.at[k_d, rows(c, "top")], sib)
            copy(s0 + 3, buf.at[k_d, rows(c, "bottom")], x_nb).wait_recv()
            start(s0 + 7, buf.at[k_d, rows(c, "bottom")], sib)
            copy(s0 + 4, buf.at[k_x, rows(1 - c, "all")], sib).wait_recv()
            copy(s0 + 5, buf.at[k_y, rows(1 - c, "all")], sib).wait_recv()
            copy(s0 + 6, buf.at[k_d, rows(1 - c, "top")], sib).wait_recv()
            copy(s0 + 7, buf.at[k_d, rows(1 - c, "bottom")], sib).wait_recv()
        for i in range(n_split, n_all):
            s0 = 8 * n_split + 3 * (i - n_split)
            for j, (px, py) in enumerate(_other_chips(x, y)):
                start(s0 + j, bufs[i].at[k], (px, py, c))
            for j, (px, py) in enumerate(_other_chips(x, y)):
                copy(s0 + j, bufs[i].at[2 * px + py], (px, py, c)).wait_recv()
        for cp in started:
            cp.wait_send()

    arrays = list(split) + list(whole)
    n_sem = 8 * n_split + 3 * len(whole)
    return pl.pallas_call(
        body, name="gather_weights",
        in_specs=[HBM] * n_all, out_specs=[HBM] * n_all,
        out_shape=[jax.ShapeDtypeStruct(a.shape, a.dtype) for a in arrays],
        input_output_aliases={i: i for i in range(n_all)},
        scratch_shapes=[pltpu.SemaphoreType.DMA((n_sem,)), pltpu.SemaphoreType.DMA((n_sem,))],
    )(*arrays)


def _swap_halves(gs):
    n = len(gs)

    def body(*refs):
        ins, outs, send_sems, recv_sems = refs[:n], refs[n:2 * n], refs[2 * n], refs[2 * n + 1]
        x, y, c = _place()
        copies = []
        for i in range(n):
            _, theirs = _halves(c, gs[i].shape[1], 8)
            cp = pltpu.make_async_remote_copy(src_ref=ins[i].at[:, theirs], dst_ref=outs[i], send_sem=send_sems.at[i],
                                              recv_sem=recv_sems.at[i], device_id=(x, y, 1 - c), device_id_type=MESH)
            cp.start()
            copies.append(cp)
        for cp in copies:
            cp.wait()

    return pl.pallas_call(
        body, name="swap_halves", in_specs=[HBM] * n, out_specs=[HBM] * n,
        out_shape=[jax.ShapeDtypeStruct((g.shape[0], g.shape[1] // 2, g.shape[2]), g.dtype) for g in gs],
        scratch_shapes=[pltpu.SemaphoreType.DMA((n,)), pltpu.SemaphoreType.DMA((n,))],
    )(*gs)


SEM = pl.BlockSpec(memory_space=pltpu.SEMAPHORE)
IN_HBM = pl.BlockSpec(memory_space=pltpu.HBM)
SPLIT_EFFECT = pltpu.SideEffectType.DATAFLOW_SIDE_EFFECTING


def _split_copies(plan, refs, send_sems, recv_sems):
    pairs = []
    for idx, (src, dst, landing, to) in enumerate(plan(refs)):
        mk = lambda d: pltpu.make_async_remote_copy(src_ref=src, dst_ref=d, send_sem=send_sems.at[idx],
                                                    recv_sem=recv_sems.at[idx], device_id=to, device_id_type=MESH)
        pairs.append((mk(dst), mk(landing)))
    return pairs


def _split_start(name, bufs, after, plan, n_copies):
    n = len(bufs)

    def body(*refs):
        send_sems, recv_sems, token = refs[n + 1], refs[n + 2], refs[-1]
        for send, _ in _split_copies(plan, refs[:n], send_sems, recv_sems):
            send.start()
        token[...] = jnp.zeros_like(token)

    sems = pltpu.SemaphoreType.DMA((n_copies,))
    out = pl.pallas_call(
        body, name=name,
        in_specs=[IN_HBM] * n + [HBM],
        out_specs=[SEM, SEM] + [IN_HBM] * n + [pl.BlockSpec(memory_space=pltpu.VMEM)],
        out_shape=[sems, sems] + [pltpu.HBM(b.shape, b.dtype) for b in bufs] + [jax.ShapeDtypeStruct((8, LANES), F32)],
        input_output_aliases={i: 2 + i for i in range(n)},
        compiler_params=pltpu.CompilerParams(has_side_effects=SPLIT_EFFECT),
    )(*[pltpu.with_memory_space_constraint(b, pltpu.HBM) for b in bufs], after)
    return out[0], out[1], out[2:2 + n], out[-1]


def _split_wait(name, bufs, send_sems, recv_sems, after, plan):
    n = len(bufs)

    def body(*refs):
        for send, recv in _split_copies(plan, refs[:n], refs[n], refs[n + 1]):
            send.wait_send()
            recv.wait_recv()

    return pl.pallas_call(
        body, name=name,
        in_specs=[IN_HBM] * n + [SEM, SEM, HBM],
        out_specs=[IN_HBM] * n,
        out_shape=[pltpu.HBM(b.shape, b.dtype) for b in bufs],
        input_output_aliases={i: i for i in range(n)},
        compiler_params=pltpu.CompilerParams(has_side_effects=SPLIT_EFFECT),
    )(*bufs, send_sems, recv_sems, after)


def _gather_plan(n):
    def plan(refs):
        x, y, c = _place()
        k = 2 * x + y
        return [(refs[i].at[k], refs[i].at[k], refs[i].at[2 * px + py], (px, py, c))
                for i in range(n) for px, py in _other_chips(x, y)]
    return plan


def _scatter_plan(n):
    def plan(refs):
        x, y, c = _place()
        return [(refs[i].at[2 * px + py], refs[n + i].at[j], refs[n + i].at[j], (px, py, c))
                for i in range(n) for j, (px, py) in enumerate(_other_chips(x, y))]
    return plan


IN_VMEM = pl.BlockSpec(memory_space=pltpu.VMEM)


def _to_bf16_all(gs):
    n = len(gs)

    def body(*refs):
        for i in range(n):
            refs[n + i][...] = refs[i][...].astype(BF16)

    return pl.pallas_call(
        body, name="bf16_rest", in_specs=[IN_VMEM] * n, out_specs=[IN_VMEM] * n,
        out_shape=[jax.ShapeDtypeStruct(g.shape, BF16) for g in gs],
        compiler_params=_params())(*gs)


def _add_landed_all(gs, landed, k):
    n = len(gs)

    def body(k_ref, *refs):
        for i in range(n):
            g_ref, l_ref, o_ref = refs[i], refs[n + i], refs[2 * n + i]
            o_ref[...] = g_ref[k_ref[0]] + l_ref[0].astype(F32) + l_ref[1].astype(F32) + l_ref[2].astype(F32)

    return pl.pallas_call(
        body, name="add_landed_rest",
        in_specs=[pl.BlockSpec(memory_space=pltpu.SMEM)] + [IN_VMEM] * (2 * n), out_specs=[IN_VMEM] * n,
        out_shape=[jax.ShapeDtypeStruct(g.shape[1:], F32) for g in gs],
        compiler_params=_params())(k.reshape(1), *gs, *landed)


def _final_exchange(halves, wholes, small):
    nh, nw = len(halves), len(wholes)
    n_in = nh + nw + 1
    rows = small.shape[0]

    def body(*refs):
        w_in, v_ref = refs[nh:nh + nw], refs[nh + nw]
        h_out, w_out, o_ref = refs[n_in:n_in + nh], refs[n_in + nh:n_in + nh + nw], refs[n_in + nh + nw]
        buf_ref, send_sems, recv_sems = refs[2 * n_in:]
        x, y, c = _place()
        sib = (x, y, 1 - c)
        me = 4 * x + 2 * y + c

        def copy(idx, src, dst, to):
            return pltpu.make_async_remote_copy(src_ref=src, dst_ref=dst, send_sem=send_sems.at[idx],
                                                recv_sem=recv_sems.at[idx], device_id=to, device_id_type=MESH)

        started = []
        split = [_halves(c, h_out[i].shape[0], 8) for i in range(nh)]
        for i in range(nh):
            started.append(copy(i, h_out[i].at[split[i][0]], h_out[i].at[split[i][0]], sib))
        for i in range(nw):
            started.append(copy(nh + i, w_in[i], w_out[i], sib))
        buf_ref[me] = v_ref[...]
        for rel in range(1, 8):
            peer = (x ^ (rel >> 2), y ^ ((rel >> 1) & 1), c ^ (rel & 1))
            started.append(copy(nh + nw + rel - 1, v_ref, buf_ref.at[me], peer))
        for cp in started:
            cp.start()
        for i in range(nh):
            copy(i, h_out[i].at[split[i][1]], h_out[i].at[split[i][1]], sib).wait_recv()
        for i in range(nw):
            copy(nh + i, w_in[i], w_out[i], sib).wait_recv()
        for rel in range(1, 8):
            copy(nh + nw + rel - 1, v_ref, buf_ref.at[me ^ rel], (x, y, c)).wait_recv()
        for cp in started:
            cp.wait_send()
        acc = buf_ref[0]
        for i in range(1, 8):
            acc = acc + buf_ref[i]
        o_ref[...] = acc

    n_sem = nh + nw + 7
    out = pl.pallas_call(
        body, name="final_exchange",
        in_specs=[HBM] * (nh + nw) + [IN_VMEM], out_specs=[HBM] * (nh + nw) + [IN_VMEM],
        out_shape=[jax.ShapeDtypeStruct(a.shape, a.dtype) for a in list(halves) + list(wholes)]
        + [jax.ShapeDtypeStruct(small.shape, F32)],
        input_output_aliases={i: i for i in range(nh)},
        scratch_shapes=[pltpu.VMEM((8, rows, LANES), F32), pltpu.SemaphoreType.DMA((n_sem,)),
                        pltpu.SemaphoreType.DMA((n_sem,))],
    )(*halves, *wholes, small)
    return out[:nh], out[nh:nh + nw], out[-1]


def _add_pair(g, got, c, name):
    _, rh, cols = got.shape
    rb = _tile(rh, 256)
    nrb = rh // rb

    def body(c_ref, g_ref, got_ref, o_ref):
        o_ref[...] = (g_ref[...] + got_ref[...]).astype(BF16)

    spec = pl.BlockSpec((1, rb, cols), lambda j, i, c_ref: (j, i, 0))
    return pl.pallas_call(
        body, name=name,
        grid_spec=pltpu.PrefetchScalarGridSpec(
            num_scalar_prefetch=1, grid=(N_SHARD, nrb),
            in_specs=[pl.BlockSpec((1, rb, cols), lambda j, i, c_ref: (j, c_ref[0] * nrb + i, 0)), spec],
            out_specs=spec),
        out_shape=jax.ShapeDtypeStruct(got.shape, BF16),
        compiler_params=_params(("parallel", "parallel")))(c.reshape(1), g, got)


def _add_chips(g, got, landed, k, c, name):
    _, rh, cols = got.shape
    rb = _tile(rh, 256)
    nrb = rh // rb

    def body(kc_ref, g_ref, got_ref, l_ref, o_ref):
        own = g_ref[0] + got_ref[0]
        o_ref[...] = own + l_ref[0].astype(F32) + l_ref[1].astype(F32) + l_ref[2].astype(F32)

    half_c = lambda i, kc: (kc[1] * nrb + i, 0)
    return pl.pallas_call(
        body, name=name,
        grid_spec=pltpu.PrefetchScalarGridSpec(
            num_scalar_prefetch=1, grid=(nrb,),
            in_specs=[pl.BlockSpec((1, rb, cols), lambda i, kc: (kc[0],) + half_c(i, kc)),
                      pl.BlockSpec((1, rb, cols), lambda i, kc: (kc[0], i, 0)),
                      pl.BlockSpec((N_SHARD - 1, rb, cols), lambda i, kc: (0, i, 0))],
            out_specs=pl.BlockSpec((rb, cols), half_c)),
        out_shape=jax.ShapeDtypeStruct((2 * rh, cols), F32),
        compiler_params=_params(("parallel",)))(jnp.stack([k, c]), g, got, landed)


def _adamw_update(w_ref, g_refs, m_ref, v_ref, go_ref, d_ref, nm_ref, nv_ref):
    gv = g_refs[0][...]
    for ref in g_refs[1:]:
        gv = gv + ref[...]
    go_ref[...] = gv
    nm = ADAM_B1 * m_ref[...] + (1.0 - ADAM_B1) * gv
    nv = ADAM_B2 * v_ref[...] + (1.0 - ADAM_B2) * (gv * gv)
    nm_ref[...] = nm
    nv_ref[...] = nv
    m_hat = nm / (1.0 - ADAM_B1 ** ADAM_STEP)
    v_hat = nv / (1.0 - ADAM_B2 ** ADAM_STEP)
    d_ref[...] = -ADAM_LR * (m_hat / (jnp.sqrt(v_hat) + ADAM_EPS) + ADAM_WD * w_ref[...])


def _adamw_all(ws, gs, ms, vs):
    n, n_g = len(ws), len(gs[0])
    flat_g = [part for parts in gs for part in parts]

    def body(*refs):
        w_refs, g_refs = refs[:n], refs[n:n + n * n_g]
        m_refs, v_refs, outs = refs[n + n * n_g:2 * n + n * n_g], refs[2 * n + n * n_g:3 * n + n * n_g], refs[3 * n + n * n_g:]
        for i in range(n):
            _adamw_update(w_refs[i], g_refs[i * n_g:(i + 1) * n_g], m_refs[i], v_refs[i], *outs[4 * i:4 * i + 4])

    out = pl.pallas_call(
        body, name="adamw_rest", in_specs=[IN_VMEM] * (3 * n + n * n_g), out_specs=[IN_VMEM] * (4 * n),
        out_shape=[jax.ShapeDtypeStruct(w.shape, F32) for w in ws for _ in range(4)],
        compiler_params=_params())(*ws, *flat_g, *ms, *vs)
    return [out[4 * i:4 * i + 4] for i in range(n)]


def _adamw(wv, gs, m, v, name):
    rows, cols = wv.shape
    rb = _tile(rows, 256)
    n_g = len(gs)

    def body(*refs):
        _adamw_update(refs[0], refs[1:1 + n_g], *refs[1 + n_g:])

    spec = pl.BlockSpec((rb, cols), lambda i: (i, 0))
    return pl.pallas_call(
        body, name=name, grid=(rows // rb,), in_specs=[spec] * (3 + n_g), out_specs=[spec] * 4,
        out_shape=[jax.ShapeDtypeStruct((rows, cols), F32)] * 4,
        compiler_params=_params(("parallel",)))(wv, *gs, m, v)


def _pack(flats):
    cat = jnp.concatenate([f.reshape(-1) for f in flats])
    n = cat.shape[0]
    rows = -(-n // (8 * LANES)) * 8
    return jnp.pad(cat, (0, rows * LANES - n)).reshape(rows, LANES)


def _unpack(packed, shapes):
    flat = packed.reshape(-1)
    out, off = [], 0
    for shp in shapes:
        n = 1
        for dim in shp:
            n *= dim
        out.append(flat[off:off + n].reshape(shp))
        off += n
    return out


def kernel(x, p, norm_g, w_in, conv_w, conv_b, dt_bias, a_log, d_skip, gnorm_g, pool_mix_w, pool_mix_b, pool_scale, w_branch_a, w_branch_b, w_out, ple_norm_g, w_ple_gate, w_ple_up, final_g, loss_target, m_norm_g, m_w_in, m_conv_w, m_conv_b, m_dt_bias, m_a_log, m_d_skip, m_gnorm_g, m_pool_mix_w, m_pool_mix_b, m_pool_scale, m_w_branch_a, m_w_branch_b, m_w_out, m_ple_norm_g, m_w_ple_gate, m_w_ple_up, m_final_g, v_norm_g, v_w_in, v_conv_w, v_conv_b, v_dt_bias, v_a_log, v_d_skip, v_gnorm_g, v_pool_mix_w, v_pool_mix_b, v_pool_scale, v_w_branch_a, v_w_branch_b, v_w_out, v_ple_norm_g, v_w_ple_gate, v_w_ple_up, v_final_g):
    wts = dict(norm_g=norm_g, w_in=w_in, conv_w=conv_w, conv_b=conv_b, dt_bias=dt_bias, a_log=a_log, d_skip=d_skip,
               gnorm_g=gnorm_g, pool_mix_w=pool_mix_w, pool_mix_b=pool_mix_b, pool_scale=pool_scale,
               w_branch_a=w_branch_a, w_branch_b=w_branch_b, w_out=w_out, ple_norm_g=ple_norm_g,
               w_ple_gate=w_ple_gate, w_ple_up=w_ple_up, final_g=final_g)
    mom_m = dict(norm_g=m_norm_g, w_in=m_w_in, conv_w=m_conv_w, conv_b=m_conv_b, dt_bias=m_dt_bias, a_log=m_a_log,
                 d_skip=m_d_skip, gnorm_g=m_gnorm_g, pool_mix_w=m_pool_mix_w, pool_mix_b=m_pool_mix_b,
                 pool_scale=m_pool_scale, w_branch_a=m_w_branch_a, w_branch_b=m_w_branch_b, w_out=m_w_out,
                 ple_norm_g=m_ple_norm_g, w_ple_gate=m_w_ple_gate, w_ple_up=m_w_ple_up, final_g=m_final_g)
    mom_v = dict(norm_g=v_norm_g, w_in=v_w_in, conv_w=v_conv_w, conv_b=v_conv_b, dt_bias=v_dt_bias, a_log=v_a_log,
                 d_skip=v_d_skip, gnorm_g=v_gnorm_g, pool_mix_w=v_pool_mix_w, pool_mix_b=v_pool_mix_b,
                 pool_scale=v_pool_scale, w_branch_a=v_w_branch_a, w_branch_b=v_w_branch_b, w_out=v_w_out,
                 ple_norm_g=v_ple_norm_g, w_ple_gate=v_w_ple_gate, w_ple_up=v_w_ple_up, final_g=v_final_g)
    c = lax.axis_index("c")
    k = 2 * lax.axis_index("x") + lax.axis_index("y")
    flat2 = lambda a: a.reshape(-1, a.shape[-1])

    slots = {n: _into_slot(flat2(wts[n]), k, BF16, "slot_" + n) for n in BIG}
    w_in_g, conv_g = _gather_weights([slots["w_in"]], [_into_slot(flat2(conv_w), k, F32, "slot_conv_w")])
    n_rest = len(REST)
    gsend, grecv, gbufs, gtoken = _split_start("gather_rest_start", [slots[n] for n in REST], conv_g,
                                               _gather_plan(n_rest), 3 * n_rest)

    def rest_weights(after):
        return dict(zip(REST, _split_wait("gather_rest_wait", gbufs, gsend, grecv, after, _gather_plan(n_rest))))

    flying = {}

    def early_grads(early):
        sends = list(_to_bf16_all([early[n] for n in REST]))
        lands = [pltpu.with_memory_space_constraint(lax.empty((N_SHARD - 1,) + v.shape[1:], BF16), pltpu.HBM)
                 for v in sends]
        ssend, srecv, sbufs, stoken = _split_start("scatter_rest_start", sends + lands, early[REST[0]],
                                                   _scatter_plan(n_rest), 3 * n_rest)
        flying.update(send=ssend, recv=srecv, bufs=sbufs)
        return stoken

    def w_in_grad(g_w_in, g_w_in_bf16):
        got = _swap_halves([g_w_in_bf16])[0]
        pair = _add_pair(g_w_in, got, c, "add_pair_w_in")
        land = pltpu.with_memory_space_constraint(lax.empty((N_SHARD - 1,) + pair.shape[1:], BF16), pltpu.HBM)
        wsend, wrecv, wbufs, wtoken = _split_start("scatter_w_in_start", [pair, land], got, _scatter_plan(1), 3)
        flying.update(w_send=wsend, w_recv=wrecv, w_bufs=wbufs, w_got=got)
        return wtoken

    small = {n: wts[n] for n in SMALL}
    small["norm_g"] = norm_g + gtoken[0, 0]
    loss, grad_x, grads = _local_step(x, p[0], loss_target, dict(w_in=w_in_g, conv_w=conv_g), small,
                                      rest_weights, early_grads, w_in_grad)
    g_w_in = grads["w_in"]
    landed = _split_wait("scatter_w_in_wait", flying["w_bufs"], flying["w_send"], flying["w_recv"], grad_x,
                         _scatter_plan(1))[1]
    w_in_half = _add_chips(g_w_in, flying["w_got"], landed, k, c, "add_chips_w_in")

    sbufs = _split_wait("scatter_rest_wait", flying["bufs"], flying["send"], flying["recv"], g_w_in,
                        _scatter_plan(n_rest))
    mine = _add_landed_all([grads[n] for n in REST], list(sbufs[n_rest:]), k)
    (w_in_sum,), theirs, small_sum = _final_exchange(
        [w_in_half], mine, _pack([grads[n] for n in SMALL] + [grads["conv_w"], loss]))
    g_sums = dict(zip(REST, zip(mine, theirs)))
    g_sums["w_in"] = (w_in_sum,)

    conv_shape = flat2(conv_w).shape
    small_shapes = [wts[n].shape for n in SMALL] + [(N_SHARD,) + conv_shape, (1,)]
    small_g = _unpack(small_sum, small_shapes)
    g_conv = lax.dynamic_index_in_dim(small_g[-2], k, axis=0, keepdims=False)

    outs = {}
    large = ("w_in", "w_branch_a")
    little = [n for n in BIG if n not in large]
    updates = {n: _adamw(flat2(wts[n]), g_sums[n], flat2(mom_m[n]), flat2(mom_v[n]), "adamw_" + n) for n in large}
    updates.update(zip(little, _adamw_all([flat2(wts[n]) for n in little], [g_sums[n] for n in little],
                                          [flat2(mom_m[n]) for n in little], [flat2(mom_v[n]) for n in little])))
    for n in BIG:
        for kind, val in zip(("grad", "delta", "new_m", "new_v"), updates[n]):
            outs[kind, n] = val.reshape(wts[n].shape)
    names = SMALL + ("conv_w",)
    sm = _adamw(_pack([wts[n] for n in names]), (_pack(small_g[:len(SMALL)] + [g_conv]),),
                _pack([mom_m[n] for n in names]), _pack([mom_v[n] for n in names]), "adamw_small")
    sm_shapes = [wts[n].shape for n in names]
    for kind, val in zip(("grad", "delta", "new_m", "new_v"), sm):
        for n, piece in zip(names, _unpack(val, sm_shapes)):
            outs[kind, n] = piece
    return (small_g[-1][0], grad_x, *[outs[kind, n] for kind in ("grad", "delta", "new_m", "new_v") for n in WEIGHTS])
```

```python
import functools

import jax
import jax.numpy as jnp
from jax import lax
from jax.experimental import pallas as pl
from jax.experimental.pallas import tpu as pltpu

F32 = jnp.float32
BF16 = jnp.bfloat16
HIGHEST = lax.Precision.HIGHEST
MESH = pl.DeviceIdType.MESH

EPS = 1e-6
HEAD_DIM = 64
SSM_GROUPS = 4
D_STATE = 128
CONV_WIDTH = 4
CHUNK = 128
N_POOL = 4
LANES = 128
N_SHARD = 4

ADAM_LR = 0.001
ADAM_B1 = 0.9
ADAM_B2 = 0.999
ADAM_EPS = 1e-08
ADAM_WD = 0.01
ADAM_STEP = 10

BIG = ("w_in", "pool_mix_w", "w_branch_a", "w_branch_b", "w_out", "w_ple_gate", "w_ple_up")
REST = BIG[1:]
SMALL = ("norm_g", "conv_b", "dt_bias", "a_log", "d_skip", "gnorm_g", "pool_mix_b", "pool_scale",
         "ple_norm_g", "final_g")
WEIGHTS = ("norm_g", "w_in", "conv_w", "conv_b", "dt_bias", "a_log", "d_skip", "gnorm_g", "pool_mix_w",
           "pool_mix_b", "pool_scale", "w_branch_a", "w_branch_b", "w_out", "ple_norm_g", "w_ple_gate",
           "w_ple_up", "final_g")


def _params(sem=None, vmem_mb=56):
    kw = dict(vmem_limit_bytes=vmem_mb << 20)
    if sem is not None:
        kw["dimension_semantics"] = sem
    return pltpu.CompilerParams(**kw)


def _sigmoid(v):
    return 0.5 * jnp.tanh(0.5 * v) + 0.5


def _sigmoid_tail(v):
    return 1.0 / (1.0 + jnp.exp(-v))


def _softplus(v):
    return jnp.maximum(v, 0.0) + jnp.log1p(jnp.exp(-jnp.abs(v)))


def _bdot(a, b):
    return jnp.dot(a.astype(BF16), b.astype(BF16), preferred_element_type=F32)


def _bdot_nt(a, b):
    return lax.dot_general(a.astype(BF16), b.astype(BF16), (((1,), (1,)), ((), ())), preferred_element_type=F32)


def _bdot_tn(a, b):
    return lax.dot_general(a.astype(BF16), b.astype(BF16), (((0,), (0,)), ((), ())), preferred_element_type=F32)


def _col_block(col0, width):
    assert col0 % width == 0, (col0, width)
    return col0 // width


def _tile(n, cap, unit=8):
    if n <= cap:
        return n
    best = None
    for cand in range(unit, cap + 1, unit):
        if n % cand == 0:
            best = cand
    assert best is not None, (n, cap)
    return best


def _shift_down(v, j, row):
    return jnp.where(row >= j, pltpu.roll(v, j, 0), 0.0)


def _shift_up(v, j, row):
    n = v.shape[0]
    return jnp.where(row < n - j, pltpu.roll(v, n - j, 0), 0.0)


def _mm(a, w, name, tm=1024, tn=1024):
    t, k = a.shape
    n = w.shape[1]
    tm, tn = min(tm, t), min(tn, n)

    def body(a_ref, w_ref, o_ref):
        o_ref[...] = _bdot(a_ref[...], w_ref[...]).astype(BF16)

    return pl.pallas_call(
        body, name=name, grid=(t // tm, n // tn),
        in_specs=[pl.BlockSpec((tm, k), lambda i, j: (i, 0)), pl.BlockSpec((k, tn), lambda i, j: (0, j))],
        out_specs=pl.BlockSpec((tm, tn), lambda i, j: (i, j)),
        out_shape=jax.ShapeDtypeStruct((t, n), BF16),
        compiler_params=_params(("parallel", "parallel")))(a, w)


def _mm_nt(a, w, name, tm=1024, tk=1024):
    t, k = a.shape
    n = w.shape[0]
    tm, tk = min(tm, t), min(tk, k)

    def body(a_ref, w_ref, o_ref):
        kk = pl.program_id(1)
        part = _bdot_nt(a_ref[...], w_ref[...])

        @pl.when(kk == 0)
        def _():
            o_ref[...] = part

        @pl.when(kk > 0)
        def _():
            o_ref[...] += part

    return pl.pallas_call(
        body, name=name, grid=(t // tm, k // tk),
        in_specs=[pl.BlockSpec((tm, tk), lambda i, j: (i, j)), pl.BlockSpec((n, tk), lambda i, j: (0, j))],
        out_specs=pl.BlockSpec((tm, n), lambda i, j: (i, 0)),
        out_shape=jax.ShapeDtypeStruct((t, n), F32),
        compiler_params=_params(("parallel", "arbitrary")))(a, w)


def _mm_tn(a, b, name, tn=1024, tk=2048, col_blocks=False):
    t, m = a.shape
    n = b.shape[1]
    tn, tk = min(tn, n), min(tk, t)

    def body(a_ref, b_ref, o_ref):
        kk = pl.program_id(1)
        part = _bdot_tn(a_ref[...], b_ref[...])
        part = part[None] if col_blocks else part

        @pl.when(kk == 0)
        def _():
            o_ref[...] = part

        @pl.when(kk > 0)
        def _():
            o_ref[...] += part

    if col_blocks:
        out_spec = pl.BlockSpec((1, m, tn), lambda j, kk: (j, 0, 0))
        out_shape = jax.ShapeDtypeStruct((n // tn, m, tn), F32)
    else:
        out_spec = pl.BlockSpec((m, tn), lambda j, kk: (0, j))
        out_shape = jax.ShapeDtypeStruct((m, n), F32)
    return pl.pallas_call(
        body, name=name, grid=(n // tn, t // tk),
        in_specs=[pl.BlockSpec((tk, m), lambda j, kk: (kk, 0)), pl.BlockSpec((tk, tn), lambda j, kk: (kk, j))],
        out_specs=out_spec, out_shape=out_shape,
        compiler_params=_params(("parallel", "arbitrary")))(a, b)


def _w_in_pieces(d, di, dc, nh, shard_w):
    pgd = d // N_POOL
    o_dt, o_u = di + dc, di + dc + nh
    o_zp, o_ga, o_gb = o_u + d, o_u + 2 * d, o_u + 3 * d
    c_z, c_uz = 2 * d, 2 * d + di + dc
    runs = [(False, 0, o_ga, d), (False, d, o_gb, d), (False, c_z, 0, di + dc), (True, 0, o_dt, nh)]
    for g in range(N_POOL):
        runs.append((False, c_uz + 2 * g * pgd, o_u + g * pgd, pgd))
        runs.append((False, c_uz + (2 * g + 1) * pgd, o_zp + g * pgd, pgd))
    pieces = []
    for is_dt, dst, src, n in runs:
        while n > 0:
            k, off = divmod(src, shard_w)
            m = min(n, shard_w - off)
            pieces.append((is_dt, dst, k, off, m))
            dst, src, n = dst + m, src + m, n - m
    return pieces


def _regroup_w_in(w_sh, d, di, dc, nh, rb=256):
    _, rows, sw = w_sh.shape
    n_main = 4 * d + di + dc
    pieces = _w_in_pieces(d, di, dc, nh, sw)
    rb = min(rb, rows)

    def body(w_ref, main_ref, dt_ref):
        dt_ref[...] = jnp.zeros_like(dt_ref)
        for is_dt, dst, k, off, m in pieces:
            out = dt_ref if is_dt else main_ref
            out[:, dst:dst + m] = w_ref[k, :, off:off + m]

    return pl.pallas_call(
        body, name="regroup_w_in", grid=(rows // rb,),
        in_specs=[pl.BlockSpec((N_SHARD, rb, sw), lambda i: (0, i, 0))],
        out_specs=[pl.BlockSpec((rb, n_main), lambda i: (i, 0)), pl.BlockSpec((rb, LANES), lambda i: (i, 0))],
        out_shape=[jax.ShapeDtypeStruct((rows, n_main), w_sh.dtype), jax.ShapeDtypeStruct((rows, LANES), w_sh.dtype)],
        compiler_params=_params(("parallel",)))(w_sh)


def _ungroup_w_in(d_main, d_dt, d, di, dc, nh, rb=128):
    rows, n_main = d_main.shape
    sw = (n_main + nh) // N_SHARD
    pieces = _w_in_pieces(d, di, dc, nh, sw)
    rb = min(rb, rows)

    def body(main_ref, dt_ref, o_ref, ob_ref):
        for is_dt, dst, k, off, m in pieces:
            piece = (dt_ref if is_dt else main_ref)[:, dst:dst + m]
            o_ref[k, :, off:off + m] = piece
            ob_ref[k, :, off:off + m] = piece.astype(BF16)

    spec = pl.BlockSpec((N_SHARD, rb, sw), lambda i: (0, i, 0))
    return pl.pallas_call(
        body, name="ungroup_w_in", grid=(rows // rb,),
        in_specs=[pl.BlockSpec((rb, n_main), lambda i: (i, 0)), pl.BlockSpec((rb, LANES), lambda i: (i, 0))],
        out_specs=[spec, spec],
        out_shape=[jax.ShapeDtypeStruct((N_SHARD, rows, sw), F32), jax.ShapeDtypeStruct((N_SHARD, rows, sw), BF16)],
        compiler_params=_params(("parallel",)))(d_main, d_dt)


def _inproj(x2, norm_g, w_main, w_dt, tm=1024, tn=1024):
    t, d = x2.shape
    n = w_main.shape[1]
    tm, tn = min(tm, t), min(tn, n)

    def body(x_ref, g_ref, w_ref, wdt_ref, proj_ref, dt_ref, h_ref):
        @pl.when(pl.program_id(1) == 0)
        def _():
            xv = x_ref[...]
            r = lax.rsqrt(jnp.mean(xv * xv, axis=-1, keepdims=True) + EPS)
            h = (xv * r * g_ref[...]).astype(BF16)
            h_ref[...] = h
            dt_ref[...] = jnp.dot(h, wdt_ref[...].astype(BF16), preferred_element_type=F32)

        proj_ref[...] = jnp.dot(h_ref[...], w_ref[...].astype(BF16), preferred_element_type=F32)

    return pl.pallas_call(
        body, name="inproj", grid=(t // tm, n // tn),
        in_specs=[pl.BlockSpec((tm, d), lambda i, j: (i, 0)), pl.BlockSpec((1, d), lambda i, j: (0, 0)),
                  pl.BlockSpec((d, tn), lambda i, j: (0, j)), pl.BlockSpec((d, LANES), lambda i, j: (0, 0))],
        out_specs=[pl.BlockSpec((tm, tn), lambda i, j: (i, j)), pl.BlockSpec((tm, LANES), lambda i, j: (i, 0)),
                   pl.BlockSpec((tm, d), lambda i, j: (i, 0))],
        out_shape=[jax.ShapeDtypeStruct((t, n), F32), jax.ShapeDtypeStruct((t, LANES), F32),
                   jax.ShapeDtypeStruct((t, d), BF16)],
        compiler_params=_params(("parallel", "arbitrary")))(x2, norm_g, w_main, w_dt)


def _conv_w_spec(conv_w, cb, j_axis):
    dc = conv_w.shape[1]
    assert dc % cb == 0, (dc, cb)
    return dc // cb, pl.BlockSpec((CONV_WIDTH, cb), lambda *ij: (0, ij[j_axis]))


CONV_ROWS = 64
CONV_HALO = 8


def _conv_taps(x_ref, t0, rc):
    if t0 == 0:
        cur = x_ref[0:rc, :]
        row = lax.broadcasted_iota(jnp.int32, cur.shape, 0)
        return [cur] + [_shift_down(cur, j, row) for j in range(1, CONV_WIDTH)]
    ext = x_ref[t0 - CONV_HALO:t0 + rc, :]
    return [ext[CONV_HALO:]] + [pltpu.roll(ext, j, 0)[CONV_HALO:] for j in range(1, CONV_WIDTH)]


def _conv_weights(w_ref):
    return [w_ref[CONV_WIDTH - 1 - j:CONV_WIDTH - j, :] for j in range(CONV_WIDTH)]


def _conv_pre(taps, wts, bias):
    acc = bias + taps[0] * wts[0]
    for j in range(1, CONV_WIDTH):
        acc = acc + taps[j] * wts[j]
    return acc


def _conv_fwd(proj, conv_w, conv_b, nb, s, col0, cb=512):
    n_blk, w_spec = _conv_w_spec(conv_w, cb, 1)
    blk0 = _col_block(col0, cb)
    rc = min(CONV_ROWS, s)

    def body(x_ref, w_ref, b_ref, o_ref):
        wts, bias = _conv_weights(w_ref), b_ref[...]
        for t0 in range(0, s, rc):
            acc = _conv_pre(_conv_taps(x_ref, t0, rc), wts, bias)
            o_ref[t0:t0 + rc, :] = acc * _sigmoid(acc)

    return pl.pallas_call(
        body, name="conv_fwd", grid=(nb, n_blk),
        in_specs=[pl.BlockSpec((s, cb), lambda b, j: (b, blk0 + j)), w_spec, pl.BlockSpec((1, cb), lambda b, j: (0, j))],
        out_specs=pl.BlockSpec((s, cb), lambda b, j: (b, j)),
        out_shape=jax.ShapeDtypeStruct((nb * s, n_blk * cb), F32),
        compiler_params=_params(("parallel", "parallel")))(proj, conv_w, conv_b)


def _ssd_consts(di):
    r = lax.broadcasted_iota(jnp.int32, (CHUNK, CHUNK), 0)
    c = lax.broadcasted_iota(jnp.int32, (CHUNK, CHUNK), 1)
    tril = (r >= c).astype(F32)
    head = lax.broadcasted_iota(jnp.int32, (LANES, di), 0)
    chan = lax.broadcasted_iota(jnp.int32, (LANES, di), 1) // HEAD_DIM
    expand = (head == chan).astype(BF16)
    return tril, expand, expand.T


def _expand(v, e, terms=3):
    acc = None
    for _ in range(terms):
        vb = v.astype(BF16)
        part = jnp.dot(vb, e, preferred_element_type=F32)
        acc = part if acc is None else acc + part
        v = v - vb.astype(F32)
    return acc


def _head_sum(t, et, terms=2):
    acc = None
    for _ in range(terms):
        tb = t.astype(BF16)
        part = jnp.dot(tb, et, preferred_element_type=F32)
        acc = part if acc is None else acc + part
        t = t - tb.astype(F32)
    return acc


def _ssd_scalars(dtr_ref, dtb_ref, alog_ref, tri):
    dtpre = dtr_ref[...] + dtb_ref[...]
    dt = _softplus(dtpre)
    a_neg = -jnp.exp(alog_ref[...])
    a_dt = dt * a_neg
    a_cs = jnp.dot(tri, a_dt, precision=HIGHEST, preferred_element_type=F32)
    a_cst = lax.dot_general(a_dt, tri, (((0,), (1,)), ((), ())), precision=HIGHEST, preferred_element_type=F32)
    return dtpre, dt, a_neg, a_cs, a_cst


def _ssd_fwd(proj, xbc, dtraw, dtb, alog, dskx, gn, nb, s, di, z_col0):
    t = nb * s
    nc = s // CHUNK
    hpg = di // HEAD_DIM // SSM_GROUPS
    gw = di // SSM_GROUPS
    gn_w = SSM_GROUPS * D_STATE
    b_blk = _col_block(di, gn_w)
    z_blk = _col_block(z_col0, di)
    L, P, N = CHUNK, HEAD_DIM, D_STATE
    tril, expand, _ = _ssd_consts(di)

    def body(z_ref, x_ref, b_ref, c_ref, dtr_ref, dtb_ref, alog_ref, dskx_ref, gn_ref, tril_ref, e_ref,
             ypre_ref, yan_ref, hp_ref, st_ref, yd_ref, xdt_ref):
        @pl.when(pl.program_id(1) == 0)
        def _():
            st_ref[...] = jnp.zeros_like(st_ref)

        hp_ref[0] = st_ref[...]
        tri = tril_ref[...]
        _, dt, _, a_cs, a_cst = _ssd_scalars(dtr_ref, dtb_ref, alog_ref, tri)
        ev = e_ref[...]
        a_exp = _expand(a_cs, ev)
        xv = x_ref[...]
        xdt = xv * _expand(dt, ev, terms=2)
        xdt_ref[...] = xdt
        a_last = a_exp[L - 1:L, :]
        xe = xdt * jnp.exp(a_last - a_exp)
        ea = jnp.exp(a_exp)
        e_last = jnp.exp(a_last)
        lower = tri > 0.5
        for g in range(SSM_GROUPS):
            gs = slice(g * gw, (g + 1) * gw)
            bg = b_ref[:, g * N:(g + 1) * N].astype(BF16)
            cg = c_ref[:, g * N:(g + 1) * N].astype(BF16)
            gm = _bdot_nt(cg, bg)
            ht = st_ref[:, gs]
            ch = _bdot(cg, ht)
            for e in range(hpg):
                h = g * hpg + e
                hs = slice(h * P, (h + 1) * P)
                decay = jnp.where(lower, jnp.exp(a_cs[:, h:h + 1] - a_cst[h:h + 1, :]), 0.0)
                yd_ref[:, hs] = _bdot(gm * decay, xdt_ref[:, hs])
            st_ref[:, gs] = ht * e_last[:, gs] + _bdot_tn(bg, xe[:, gs])
            ypre = yd_ref[:, gs] + ea[:, gs] * ch + xv[:, gs] * dskx_ref[:, gs]
            ypre_ref[:, gs] = ypre
            zv = z_ref[:, gs]
            v = ypre * zv * _sigmoid(zv)
            r = lax.rsqrt(jnp.mean(v * v, axis=-1, keepdims=True) + EPS)
            yan_ref[:, gs] = (v * r * gn_ref[:, gs]).astype(BF16)

    row = lambda b, c: b * nc + c
    vec = lambda w: pl.BlockSpec((1, w), lambda b, c: (0, 0))
    return pl.pallas_call(
        body, name="ssd_fwd", grid=(nb, nc),
        in_specs=[pl.BlockSpec((L, di), lambda b, c: (row(b, c), z_blk)),
                  pl.BlockSpec((L, di), lambda b, c: (row(b, c), 0)),
                  pl.BlockSpec((L, gn_w), lambda b, c: (row(b, c), b_blk)),
                  pl.BlockSpec((L, gn_w), lambda b, c: (row(b, c), b_blk + 1)),
                  pl.BlockSpec((L, LANES), lambda b, c: (row(b, c), 0)),
                  vec(LANES), vec(LANES), vec(di), vec(di),
                  pl.BlockSpec((L, L), lambda b, c: (0, 0)),
                  pl.BlockSpec((LANES, di), lambda b, c: (0, 0))],
        out_specs=[pl.BlockSpec((L, di), lambda b, c: (row(b, c), 0)),
                   pl.BlockSpec((L, di), lambda b, c: (row(b, c), 0)),
                   pl.BlockSpec((1, N, di), lambda b, c: (row(b, c), 0, 0))],
        out_shape=[jax.ShapeDtypeStruct((t, di), F32), jax.ShapeDtypeStruct((t, di), BF16),
                   jax.ShapeDtypeStruct((nb * nc, N, di), F32)],
        scratch_shapes=[pltpu.VMEM((N, di), F32), pltpu.VMEM((L, di), F32), pltpu.VMEM((L, di), F32)],
        compiler_params=_params(("parallel", "arbitrary")))(
            proj, xbc, xbc, xbc, dtraw, dtb, alog, dskx, gn, tril, expand)


def _pool_sum(v, g, row, shift):
    s2 = v + shift(v, 1, row)
    s4 = s2 + shift(s2, 2, row)
    s8 = s4 + shift(s4, 4, row)
    s16 = s8 + shift(s8, 8, row)
    return jnp.where(g == 0, s2, jnp.where(g == 1, s4, jnp.where(g == 2, s8, s16)))


def _pool_count(g, row):
    return jnp.minimum(row + 1, jnp.left_shift(2, g)).astype(F32)


POOL_ROWS = 128
POOL_HALO = 16


def _roll_sum(v, g, step):
    n = v.shape[0]
    s2 = v + pltpu.roll(v, step % n, 0)
    s4 = s2 + pltpu.roll(s2, (2 * step) % n, 0)
    s8 = s4 + pltpu.roll(s4, (4 * step) % n, 0)
    s16 = s8 + pltpu.roll(s8, (8 * step) % n, 0)
    return jnp.where(g == 0, s2, jnp.where(g == 1, s4, jnp.where(g == 2, s8, s16)))


def _pool_trailing(u_ref, pgd, t0, rc, g):
    if t0 == 0:
        cur = u_ref[0:rc, :pgd]
        row = lax.broadcasted_iota(jnp.int32, cur.shape, 0)
        return cur, _pool_sum(cur, g, row, _shift_down), _pool_count(g, row)
    ext = u_ref[t0 - POOL_HALO:t0 + rc, :pgd]
    return ext[POOL_HALO:], _roll_sum(ext, g, 1)[POOL_HALO:], jnp.left_shift(2, g).astype(F32)


def _pool_fwd(proj, mix_w, mix_b, scale, nb, s, col0):
    pgd = mix_w.shape[-1]
    blk0 = _col_block(col0, 2 * pgd)
    rc = min(POOL_ROWS, s)

    def body(uz_ref, w_ref, b_ref, sc_ref, o_ref):
        g = pl.program_id(1)
        wv = w_ref[:, 0].reshape(pgd, pgd)
        for t0 in range(0, s, rc):
            u, win, cnt = _pool_trailing(uz_ref, pgd, t0, rc, g)
            zp = uz_ref[t0:t0 + rc, pgd:]
            mixed = _bdot(win / cnt - u, wv) + b_ref[...]
            o_ref[t0:t0 + rc, :] = (mixed * sc_ref[...] * zp * _sigmoid(zp)).astype(BF16)

    return pl.pallas_call(
        body, name="pool_fwd", grid=(nb, N_POOL),
        in_specs=[pl.BlockSpec((s, 2 * pgd), lambda b, g: (b, blk0 + g)),
                  pl.BlockSpec((N_SHARD, 1, pgd // N_SHARD, pgd), lambda b, g: (0, g, 0, 0)),
                  pl.BlockSpec((1, pgd), lambda b, g: (0, g)), pl.BlockSpec((1, pgd), lambda b, g: (0, g))],
        out_specs=pl.BlockSpec((s, pgd), lambda b, g: (b, g)),
        out_shape=jax.ShapeDtypeStruct((nb * s, N_POOL * pgd), BF16),
        compiler_params=_params(("parallel", "parallel")))(proj, mix_w, mix_b, scale)


def _mid_fwd(ya, yb, proj, col0, x2, p2, tgt, wo, wpg, wup, ple_g, final_g, tm=256):
    t, d = ya.shape
    tm = min(tm, t)
    blk = _col_block(col0, 2 * d)
    n_up, pdim, up_w = wup.shape

    def body(ya_ref, yb_ref, g_ref, x_ref, p_ref, tgt_ref, wo_ref, wpg_ref, wup_ref, pg_ref, g_fin_ref,
             merged_ref, hn_ref, dpre_ref, dpu_ref, x1_ref, dx2_ref, loss_ref, dg_ref):
        @pl.when(pl.program_id(0) == 0)
        def _():
            loss_ref[...] = jnp.zeros_like(loss_ref)
            dg_ref[...] = jnp.zeros_like(dg_ref)

        merged = (_sigmoid(g_ref[:, :d]) * ya_ref[...] + _sigmoid(g_ref[:, d:]) * yb_ref[...]).astype(BF16)
        merged_ref[...] = merged
        x1 = x_ref[...] + jnp.dot(merged, wo_ref[...], preferred_element_type=F32)
        x1_ref[...] = x1
        r1 = lax.rsqrt(jnp.mean(x1 * x1, axis=-1, keepdims=True) + EPS)
        hn = (x1 * r1 * pg_ref[...]).astype(BF16)
        hn_ref[...] = hn
        gate = _sigmoid(jnp.dot(hn, wpg_ref[...], preferred_element_type=F32))
        pb = p_ref[...].astype(BF16)
        pu = jnp.concatenate([jnp.dot(pb, wup_ref[j], preferred_element_type=F32) for j in range(n_up)], axis=1)
        x2 = x1 + gate * pu
        r = lax.rsqrt(jnp.mean(x2 * x2, axis=-1, keepdims=True) + EPS)
        xn = x2 * r
        fg = g_fin_ref[...]
        err = xn * fg - tgt_ref[...]
        loss_ref[...] += 0.5 * jnp.sum(jnp.mean(err * err, axis=-1, keepdims=True))
        dy = err * (1.0 / d)
        dg_ref[...] += jnp.sum(dy * xn, axis=0, keepdims=True)
        dxn = dy * fg
        dx2 = r * (dxn - xn * jnp.mean(dxn * xn, axis=-1, keepdims=True))
        dx2_ref[...] = dx2
        dpre_ref[...] = (dx2 * pu * gate * (1.0 - gate)).astype(BF16)
        dpu_ref[...] = (dx2 * gate).astype(BF16)

    row = pl.BlockSpec((tm, d), lambda i: (i, 0))
    vec = pl.BlockSpec((1, d), lambda i: (0, 0))
    whole = lambda a: pl.BlockSpec(a.shape, lambda i: (0,) * a.ndim)
    return pl.pallas_call(
        body, name="mid_fwd", grid=(t // tm,),
        in_specs=[row, row, pl.BlockSpec((tm, 2 * d), lambda i: (i, blk)), row,
                  pl.BlockSpec((tm, pdim), lambda i: (i, 0)), row, whole(wo), whole(wpg), whole(wup), vec, vec],
        out_specs=[row] * 6 + [pl.BlockSpec((1, LANES), lambda i: (0, 0)), vec],
        out_shape=[jax.ShapeDtypeStruct((t, d), BF16)] * 4 + [jax.ShapeDtypeStruct((t, d), F32)] * 2 + [
            jax.ShapeDtypeStruct((1, LANES), F32), jax.ShapeDtypeStruct((1, d), F32)],
        compiler_params=_params(("arbitrary",)))(ya, yb, proj, x2, p2, tgt, wo, wpg, wup, ple_g, final_g)


def _rms_grad(xv, dh, g):
    r = lax.rsqrt(jnp.mean(xv * xv, axis=-1, keepdims=True) + EPS)
    xn = xv * r
    dd = dh * g
    return r * (dd - xn * jnp.mean(dd * xn, axis=-1, keepdims=True)), jnp.sum(dh * xn, axis=0, keepdims=True)


def _mid_bwd(dpre, dx2, x1, ya, yb, proj, col0, wpg, wo, ple_g, n_cols, tm=256):
    t, d = ya.shape
    tm = min(tm, t)
    blk = _col_block(col0, 2 * d)

    def body(dpre_ref, dx2_ref, x1_ref, ya_ref, yb_ref, g_ref, wpg_ref, wo_ref, pg_ref,
             dx1_ref, dya_ref, dyb_ref, dg_ref, dpg_ref):
        @pl.when(pl.program_id(0) == 0)
        def _():
            dpg_ref[...] = jnp.zeros_like(dpg_ref)

        dhn = _bdot_nt(dpre_ref[...], wpg_ref[...])
        dx, dpg = _rms_grad(x1_ref[...], dhn, pg_ref[...])
        dpg_ref[...] += dpg
        dx1 = dx2_ref[...] + dx
        dx1_ref[...] = dx1
        dm_v = _bdot_nt(dx1, wo_ref[...])
        sa = _sigmoid(g_ref[:, :d])
        sb = _sigmoid(g_ref[:, d:])
        dya_ref[...] = (dm_v * sa).astype(BF16)
        dyb_ref[...] = (dm_v * sb).astype(BF16)
        dg_ref[:, :d] = (dm_v * ya_ref[...] * sa * (1.0 - sa)).astype(BF16)
        dg_ref[:, d:] = (dm_v * yb_ref[...] * sb * (1.0 - sb)).astype(BF16)

    row = pl.BlockSpec((tm, d), lambda i: (i, 0))
    vec = pl.BlockSpec((1, d), lambda i: (0, 0))
    gspec = pl.BlockSpec((tm, 2 * d), lambda i: (i, blk))
    whole = lambda a: pl.BlockSpec(a.shape, lambda i: (0,) * a.ndim)
    return pl.pallas_call(
        body, name="mid_bwd", grid=(t // tm,),
        in_specs=[row, row, row, row, row, gspec, whole(wpg), whole(wo), vec],
        out_specs=[row, row, row, gspec, vec],
        out_shape=[jax.ShapeDtypeStruct((t, d), F32), jax.ShapeDtypeStruct((t, d), BF16),
                   jax.ShapeDtypeStruct((t, d), BF16), jax.ShapeDtypeStruct((t, n_cols), BF16),
                   jax.ShapeDtypeStruct((1, d), F32)],
        compiler_params=_params(("arbitrary",)))(dpre, dx2, x1, ya, yb, proj, wpg, wo, ple_g)


def _in_bwd(dproj, w_main, ddt, w_dt, x2, dx1, norm_g, after, tm=1024, tk=1024):
    t, k = dproj.shape
    d = x2.shape[1]
    tm, tk = min(tm, t), min(tk, k)
    nk = k // tk

    def body(a_ref, w_ref, ddt_ref, wdt_ref, x_ref, dres_ref, g_ref, _, gx_ref, dg_ref, acc_ref):
        kk = pl.program_id(1)

        @pl.when((pl.program_id(0) == 0) & (kk == 0))
        def _():
            dg_ref[...] = jnp.zeros_like(dg_ref)

        part = _bdot_nt(a_ref[...], w_ref[...])

        @pl.when(kk == 0)
        def _():
            acc_ref[...] = part

        @pl.when(kk > 0)
        def _():
            acc_ref[...] += part

        @pl.when(kk == nk - 1)
        def _():
            dh = acc_ref[...] + _bdot_nt(ddt_ref[...], wdt_ref[...])
            dx, dg = _rms_grad(x_ref[...], dh, g_ref[...])
            dg_ref[...] += dg
            gx_ref[...] = dres_ref[...] + dx

    row = pl.BlockSpec((tm, d), lambda i, j: (i, 0))
    vec = pl.BlockSpec((1, d), lambda i, j: (0, 0))
    return pl.pallas_call(
        body, name="in_bwd", grid=(t // tm, nk),
        in_specs=[pl.BlockSpec((tm, tk), lambda i, j: (i, j)), pl.BlockSpec((d, tk), lambda i, j: (0, j)),
                  pl.BlockSpec((tm, LANES), lambda i, j: (i, 0)), pl.BlockSpec((d, LANES), lambda i, j: (0, 0)),
                  row, row, vec, pl.BlockSpec((8, LANES), lambda i, j: (0, 0))],
        out_specs=[row, vec],
        out_shape=[jax.ShapeDtypeStruct((t, d), F32), jax.ShapeDtypeStruct((1, d), F32)],
        scratch_shapes=[pltpu.VMEM((tm, d), F32)],
        compiler_params=_params(("arbitrary", "arbitrary")))(dproj, w_main, ddt, w_dt, x2, dx1, norm_g, after)


def _pool_bwd(proj, dyb, dproj, mix_w, mix_b, scale, nb, s, col0):
    pgd = mix_w.shape[-1]
    blk0 = _col_block(col0, 2 * pgd)
    rc = min(POOL_ROWS, s)

    def body(uz_ref, dy_ref, _, w_ref, b_ref, sc_ref, duz_ref, dw_ref, db_ref, dsc_ref, dpn_ref, dwacc_ref):
        g = pl.program_id(0)

        @pl.when(pl.program_id(1) == 0)
        def _():
            dw_ref[...] = jnp.zeros_like(dw_ref)
            db_ref[...] = jnp.zeros_like(db_ref)
            dsc_ref[...] = jnp.zeros_like(dsc_ref)

        wv = w_ref[:, 0].reshape(pgd, pgd)
        sc = sc_ref[...]
        fold = lambda v: v.reshape(rc // 8, 8, pgd).sum(axis=0)
        db8 = jnp.zeros((8, pgd), F32)
        dsc8 = jnp.zeros((8, pgd), F32)
        dwacc_ref[...] = jnp.zeros_like(dwacc_ref)
        for t0 in range(0, s, rc):
            u, win, cnt = _pool_trailing(uz_ref, pgd, t0, rc, g)
            pooled = win / cnt - u
            zp = uz_ref[t0:t0 + rc, pgd:]
            mixed = _bdot(pooled, wv) + b_ref[...]
            sg = _sigmoid(zp)
            sz = zp * sg
            dy = dy_ref[t0:t0 + rc, :]
            dsc8 = dsc8 + fold(dy * mixed * sz)
            dmixed = dy * sc * sz
            db8 = db8 + fold(dmixed)
            dwacc_ref[...] += _bdot_tn(pooled, dmixed)
            dpn_ref[t0:t0 + rc, :] = _bdot_nt(dmixed, wv) / cnt
            duz_ref[t0:t0 + rc, pgd:] = (dy * mixed * sc * sg * (1.0 + zp * (1.0 - sg))).astype(BF16)
        dsc_ref[...] += jnp.sum(dsc8, axis=0, keepdims=True)
        db_ref[...] += jnp.sum(db8, axis=0, keepdims=True)
        dw_ref[:, 0] += dwacc_ref[...].reshape(N_SHARD, pgd // N_SHARD, pgd)
        for t0 in range(0, s, rc):
            if t0 + rc < s:
                n = rc + POOL_HALO
                win = dpn_ref[t0:t0 + n, :]
                cur, lead = win[:rc], _roll_sum(win, g, n - 1)[:rc]
            else:
                cur = dpn_ref[t0:t0 + rc, :]
                row = lax.broadcasted_iota(jnp.int32, cur.shape, 0)
                lead = _pool_sum(cur, g, row, _shift_up)
            if t0 == 0:
                cnt = _pool_count(g, lax.broadcasted_iota(jnp.int32, cur.shape, 0))
            else:
                cnt = jnp.left_shift(2, g).astype(F32)
            duz_ref[t0:t0 + rc, :pgd] = (lead - cur * cnt).astype(BF16)

    uz = pl.BlockSpec((s, 2 * pgd), lambda g, b: (b, blk0 + g))
    vec = pl.BlockSpec((1, pgd), lambda g, b: (0, g))
    wspec = pl.BlockSpec((N_SHARD, 1, pgd // N_SHARD, pgd), lambda g, b: (0, g, 0, 0))
    return pl.pallas_call(
        body, name="pool_bwd", grid=(N_POOL, nb),
        in_specs=[uz, pl.BlockSpec((s, pgd), lambda g, b: (b, g)), pl.BlockSpec(memory_space=pl.ANY), wspec, vec, vec],
        out_specs=[uz, wspec, vec, vec],
        out_shape=[jax.ShapeDtypeStruct(dproj.shape, dproj.dtype), jax.ShapeDtypeStruct(mix_w.shape, F32),
                   jax.ShapeDtypeStruct(mix_b.shape, F32), jax.ShapeDtypeStruct(scale.shape, F32)],
        scratch_shapes=[pltpu.VMEM((s, pgd), F32), pltpu.VMEM((pgd, pgd), F32)],
        input_output_aliases={2: 0},
        compiler_params=_params(("parallel", "arbitrary")))(proj, dyb, dproj, mix_w, mix_b, scale)


def _ssd_bwd(dyan, ypre, proj, xbc, dtraw, hp, dproj, dtb, alog, dskx, gn, nb, s, di, z_col0):
    t = nb * s
    nc = s // CHUNK
    hpg = di // HEAD_DIM // SSM_GROUPS
    gw = di // SSM_GROUPS
    gn_w = SSM_GROUPS * D_STATE
    dc = di + 2 * gn_w
    b_blk = _col_block(di, gn_w)
    z_blk = _col_block(z_col0, di)
    L, P, N = CHUNK, HEAD_DIM, D_STATE
    tril, expand, expand_t = _ssd_consts(di)

    def body(dy_ref, ypre_ref, z_ref, x_ref, b_ref, c_ref, dtr_ref, hp_ref, _, dtb_ref, alog_ref, dskx_ref, gn_ref,
             tril_ref, e_ref, et_ref, dz_ref, ddt_ref, dxbc_ref, dgn_ref, ddsk_ref, dalog_ref, ddtb_ref,
             dst_ref, dyp_ref, xdt_ref, dxm_ref, t1_ref, t3_ref, aux_ref):
        @pl.when((pl.program_id(0) == 0) & (pl.program_id(1) == 0))
        def _():
            dgn_ref[...] = jnp.zeros_like(dgn_ref)
            ddsk_ref[...] = jnp.zeros_like(ddsk_ref)
            dalog_ref[...] = jnp.zeros_like(dalog_ref)
            ddtb_ref[...] = jnp.zeros_like(ddtb_ref)

        @pl.when(pl.program_id(1) == 0)
        def _():
            dst_ref[...] = jnp.zeros_like(dst_ref)

        tri = tril_ref[...]
        dtpre, dt, a_neg, a_cs, a_cst = _ssd_scalars(dtr_ref, dtb_ref, alog_ref, tri)
        ev = e_ref[...]
        a_exp = _expand(a_cs, ev)
        dt_exp = _expand(dt, ev, terms=2)
        xv = x_ref[...]
        xdt = xv * dt_exp
        xdt_ref[...] = xdt
        a_last = a_exp[L - 1:L, :]
        dte = jnp.exp(a_last - a_exp)
        xe = xdt * dte
        ea = jnp.exp(a_exp)
        e_last = jnp.exp(a_last)
        lower = tri > 0.5
        aux_ref[...] = jnp.zeros_like(aux_ref)
        for g in range(SSM_GROUPS):
            gs = slice(g * gw, (g + 1) * gw)
            zv = z_ref[:, gs]
            yp = ypre_ref[:, gs]
            sg = _sigmoid(zv)
            sz = zv * sg
            vg = yp * sz
            r = lax.rsqrt(jnp.mean(vg * vg, axis=-1, keepdims=True) + EPS)
            vn = vg * r
            dyg = dy_ref[:, gs]
            dgn_ref[:, gs] += jnp.sum(dyg * vn, axis=0, keepdims=True)
            dvn = dyg * gn_ref[:, gs]
            dv = r * (dvn - vn * jnp.mean(dvn * vn, axis=-1, keepdims=True))
            dy = dv * sz
            dyp_ref[:, gs] = dy
            dz_ref[:, gs] = (dv * yp * sg * (1.0 + zv * (1.0 - sg))).astype(BF16)
            bg = b_ref[:, g * N:(g + 1) * N].astype(BF16)
            cg = c_ref[:, g * N:(g + 1) * N].astype(BF16)
            gm = _bdot_nt(cg, bg)
            ht = hp_ref[0, :, gs]
            dht = dst_ref[:, gs]
            bds = _bdot(bg, dht)
            dye = dy * ea[:, gs]
            xe_g = xe[:, gs]
            dcg = _bdot_nt(dye, ht)
            dbg = _bdot_nt(xe_g, dht)
            dst_ref[:, gs] = e_last[:, gs] * dht + _bdot_tn(cg, dye)
            dgm = jnp.zeros((L, L), F32)
            for e in range(hpg):
                h = g * hpg + e
                hs = slice(h * P, (h + 1) * P)
                decay = jnp.where(lower, jnp.exp(a_cs[:, h:h + 1] - a_cst[h:h + 1, :]), 0.0)
                dy_h = dyp_ref[:, hs]
                dgm = dgm + _bdot_nt(dy_h, xdt_ref[:, hs]) * decay
                dxm_ref[:, hs] = _bdot_tn(gm * decay, dy_h)
            dxbc_ref[:, di + g * N:di + (g + 1) * N] = dbg + _bdot_tn(dgm, cg)
            dxbc_ref[:, di + gn_w + g * N:di + gn_w + (g + 1) * N] = dcg + _bdot(dgm, bg)
            dxm = dxm_ref[:, gs]
            x_g = xv[:, gs]
            dskx = dskx_ref[:, gs]
            xeb = xe_g * bds
            dxdt = dxm + dte[:, gs] * bds
            dxbc_ref[:, gs] = dxdt * dt_exp[:, gs] + dy * dskx
            each = ea[:, gs] * _bdot(cg, ht)
            y_diag = yp - x_g * dskx - each
            rnd = lambda v: v.astype(BF16).astype(F32)
            t1_ref[:, gs] = rnd(dy) * y_diag + dy * each - rnd(xdt[:, gs]) * dxm - xeb
            t3_ref[:, gs] = dxdt * x_g
            aux_ref[0:1, gs] = jnp.sum(dht * ht, axis=0, keepdims=True)
            aux_ref[1:2, gs] = jnp.sum(dy * x_g, axis=0, keepdims=True)
            aux_ref[2:3, gs] = jnp.sum(xeb, axis=0, keepdims=True)
        etv = et_ref[...]
        aux = _head_sum(aux_ref[...], etv)
        rowi = lax.broadcasted_iota(jnp.int32, (L, LANES), 0)
        end = aux[2:3, :] + aux[0:1, :] * jnp.exp(a_cs[L - 1:L, :])
        da = _head_sum(t1_ref[...], etv, terms=3) + jnp.where(rowi == L - 1, end, 0.0)
        rc = lax.dot_general(tri, da, (((0,), (0,)), ((), ())), precision=HIGHEST, preferred_element_type=F32)
        ddt = a_neg * rc + _head_sum(t3_ref[...], etv, terms=1)
        ddtraw = ddt * _sigmoid_tail(dtpre)
        ddt_ref[...] = ddtraw.astype(BF16)
        ddtb_ref[...] += jnp.sum(ddtraw, axis=0, keepdims=True)
        dalog_ref[...] += jnp.sum(dt * rc, axis=0, keepdims=True) * a_neg
        ddsk_ref[...] += aux[1:2, :]

    row = lambda b, c: b * nc + (nc - 1 - c)
    full = lambda w: pl.BlockSpec((L, w), lambda b, c: (row(b, c), 0))
    zspec = pl.BlockSpec((L, di), lambda b, c: (row(b, c), z_blk))
    vec = lambda w: pl.BlockSpec((1, w), lambda b, c: (0, 0))
    slab = lambda shape: pltpu.VMEM(shape, F32)
    return pl.pallas_call(
        body, name="ssd_bwd", grid=(nb, nc),
        in_specs=[full(di), full(di), zspec, full(di),
                  pl.BlockSpec((L, gn_w), lambda b, c: (row(b, c), b_blk)),
                  pl.BlockSpec((L, gn_w), lambda b, c: (row(b, c), b_blk + 1)),
                  full(LANES),
                  pl.BlockSpec((1, N, di), lambda b, c: (row(b, c), 0, 0)),
                  pl.BlockSpec(memory_space=pl.ANY),
                  vec(LANES), vec(LANES), vec(di), vec(di),
                  pl.BlockSpec((L, L), lambda b, c: (0, 0)),
                  pl.BlockSpec((LANES, di), lambda b, c: (0, 0)),
                  pl.BlockSpec((di, LANES), lambda b, c: (0, 0))],
        out_specs=[zspec, full(LANES), full(dc), vec(di), vec(LANES), vec(LANES), vec(LANES)],
        out_shape=[jax.ShapeDtypeStruct(dproj.shape, dproj.dtype), jax.ShapeDtypeStruct((t, LANES), BF16),
                   jax.ShapeDtypeStruct((t, dc), F32), jax.ShapeDtypeStruct((1, di), F32),
                   jax.ShapeDtypeStruct((1, LANES), F32), jax.ShapeDtypeStruct((1, LANES), F32),
                   jax.ShapeDtypeStruct((1, LANES), F32)],
        scratch_shapes=[slab((N, di)), slab((L, di)), slab((L, di)), slab((L, di)), slab((L, di)), slab((L, di)),
                        slab((8, di))],
        input_output_aliases={8: 0},
        compiler_params=_params(("arbitrary", "arbitrary")))(
            dyan, ypre, proj, xbc, xbc, xbc, dtraw, hp, dproj, dtb, alog, dskx, gn, tril, expand, expand_t)


def _conv_bwd(proj, dxbc, dproj, conv_w, conv_b, nb, s, col0, cb=256):
    n_blk, w_spec = _conv_w_spec(conv_w, cb, 0)
    blk0 = _col_block(col0, cb)
    rc = min(CONV_ROWS, s)

    def body(x_ref, dy_ref, _, w_ref, b_ref, dx_ref, dw_ref, db_ref, dacc_ref):
        @pl.when(pl.program_id(1) == 0)
        def _():
            dw_ref[...] = jnp.zeros_like(dw_ref)
            db_ref[...] = jnp.zeros_like(db_ref)

        wts, bias = _conv_weights(w_ref), b_ref[...]
        fold = lambda v: v.reshape(rc // 8, 8, cb).sum(axis=0)
        db8 = jnp.zeros((8, cb), F32)
        dw8 = [jnp.zeros((8, cb), F32) for _ in range(CONV_WIDTH)]
        for t0 in range(0, s, rc):
            taps = _conv_taps(x_ref, t0, rc)
            acc = _conv_pre(taps, wts, bias)
            sg = _sigmoid(acc)
            dacc = dy_ref[t0:t0 + rc, :] * sg * (1.0 + acc * (1.0 - sg))
            dacc_ref[t0:t0 + rc, :] = dacc
            db8 = db8 + fold(dacc)
            dw8 = [dw8[j] + fold(dacc * taps[j]) for j in range(CONV_WIDTH)]
        db_ref[...] += jnp.sum(db8, axis=0, keepdims=True)
        for j in range(CONV_WIDTH):
            dw_ref[CONV_WIDTH - 1 - j:CONV_WIDTH - j, :] += jnp.sum(dw8[j], axis=0, keepdims=True)
        for t0 in range(0, s, rc):
            if t0 + rc < s:
                n = rc + CONV_HALO
                win = dacc_ref[t0:t0 + n, :]
                ups = [win[:rc]] + [pltpu.roll(win, n - j, 0)[:rc] for j in range(1, CONV_WIDTH)]
            else:
                cur = dacc_ref[t0:t0 + rc, :]
                row = lax.broadcasted_iota(jnp.int32, cur.shape, 0)
                ups = [cur] + [_shift_up(cur, j, row) for j in range(1, CONV_WIDTH)]
            dx = ups[0] * wts[0]
            for j in range(1, CONV_WIDTH):
                dx = dx + ups[j] * wts[j]
            dx_ref[t0:t0 + rc, :] = dx.astype(BF16)

    return pl.pallas_call(
        body, name="conv_bwd", grid=(n_blk, nb),
        in_specs=[pl.BlockSpec((s, cb), lambda j, b: (b, blk0 + j)), pl.BlockSpec((s, cb), lambda j, b: (b, j)),
                  pl.BlockSpec(memory_space=pl.ANY), w_spec, pl.BlockSpec((1, cb), lambda j, b: (0, j))],
        out_specs=[pl.BlockSpec((s, cb), lambda j, b: (b, blk0 + j)), w_spec, pl.BlockSpec((1, cb), lambda j, b: (0, j))],
        out_shape=[jax.ShapeDtypeStruct(dproj.shape, dproj.dtype), jax.ShapeDtypeStruct(conv_w.shape, F32),
                   jax.ShapeDtypeStruct(conv_b.shape, F32)],
        scratch_shapes=[pltpu.VMEM((s, cb), F32)],
        input_output_aliases={2: 0},
        compiler_params=_params(("parallel", "arbitrary")))(proj, dxbc, dproj, conv_w, conv_b)


def _local_step(x, p, tgt, wg, small, rest_weights, early_grads, w_in_grad):
    nb, s, d = x.shape
    t = nb * s
    gn_w = SSM_GROUPS * D_STATE
    dc = N_SHARD * wg["conv_w"].shape[2]
    di = dc - 2 * gn_w
    nh = di // HEAD_DIM
    pgd = d // N_POOL
    x2 = x.reshape(t, d)
    p2 = p.reshape(t, p.shape[-1])
    tgt2 = tgt.reshape(t, d)

    w_main, w_dt = _regroup_w_in(wg["w_in"], d, di, dc, nh)
    c_g, c_z, c_xbc, c_uz = 0, 2 * d, 2 * d + di, 2 * d + di + dc
    n_main = w_main.shape[1]

    pad_h = lambda v: jnp.pad(v.reshape(1, nh).astype(F32), ((0, 0), (0, LANES - nh)))
    dtb, alog = pad_h(small["dt_bias"]), pad_h(small["a_log"])
    dskx = jnp.repeat(small["d_skip"].reshape(1, nh).astype(F32), HEAD_DIM, axis=1)
    vec = lambda v: v.reshape(1, -1).astype(F32)
    norm_g, gn, conv_b = vec(small["norm_g"]), vec(small["gnorm_g"]), vec(small["conv_b"])
    mix_b, scale = vec(small["pool_mix_b"]), vec(small["pool_scale"])
    ple_g, final_g = vec(small["ple_norm_g"]), vec(small["final_g"])
    conv_w = jnp.swapaxes(wg["conv_w"], 0, 1).reshape(CONV_WIDTH, dc)

    wide = _tile(n_main, 2304, LANES)
    proj, dtraw, h = _inproj(x2, norm_g, w_main, w_dt, tn=wide)
    xbc = _conv_fwd(proj, conv_w, conv_b, nb, s, c_xbc)
    ypre, yan, hp = _ssd_fwd(proj, xbc, dtraw, dtb, alog, dskx, gn, nb, s, di, c_z)
    wr = rest_weights(yan)
    mix_w = wr["pool_mix_w"].reshape(N_SHARD, N_POOL, pgd // N_SHARD, pgd)
    rows = lambda v: v.reshape(-1, v.shape[-1])
    wa, wb, wo, wpg = rows(wr["w_branch_a"]), rows(wr["w_branch_b"]), rows(wr["w_out"]), rows(wr["w_ple_gate"])
    wup = wr["w_ple_up"]
    ybp = _pool_fwd(proj, mix_w, mix_b, scale, nb, s, c_uz)
    ya = _mm(yan, wa, "branch_a")
    yb = _mm(ybp, wb, "branch_b")
    merged, hn, dpre, dpu, x1, dx2, loss, d_final_g = _mid_fwd(
        ya, yb, proj, c_g, x2, p2, tgt2, wo, wpg, wup, ple_g, final_g)

    d_wpg = _mm_tn(hn, dpre, "d_w_ple_gate")
    d_wup = _mm_tn(p2, dpu, "d_w_ple_up", tn=wup.shape[-1], col_blocks=True)
    dx1, dya, dyb, dproj, d_ple_g = _mid_bwd(dpre, dx2, x1, ya, yb, proj, c_g, wpg, wo, ple_g, n_main)
    d_wo = _mm_tn(merged, dx1, "d_w_out")
    d_wa = _mm_tn(yan, dya, "d_w_branch_a", tk=1024)
    d_wb = _mm_tn(ybp, dyb, "d_w_branch_b")
    dyan = _mm_nt(dya, wa, "d_y_a")
    dybp = _mm_nt(dyb, wb, "d_y_b")
    dproj, d_mix_w, d_mix_b, d_scale = _pool_bwd(proj, dybp, dproj, mix_w, mix_b, scale, nb, s, c_uz)
    shard_major = lambda v: v.reshape(N_SHARD, v.shape[0] // N_SHARD, v.shape[1])
    early = dict(pool_mix_w=d_mix_w.reshape(N_SHARD, pgd, pgd), w_branch_a=shard_major(d_wa),
                 w_branch_b=shard_major(d_wb), w_out=shard_major(d_wo), w_ple_gate=shard_major(d_wpg),
                 w_ple_up=d_wup)
    token = early_grads(early)
    dproj, ddt, dxbc, d_gn, d_dsk, d_alog, d_dtb = _ssd_bwd(
        dyan, ypre, proj, xbc, dtraw, hp, dproj, dtb + token[0:1, 0:1], alog, dskx, gn, nb, s, di, c_z)
    dproj, d_conv_w, d_conv_b = _conv_bwd(proj, dxbc, dproj, conv_w, conv_b, nb, s, c_xbc)
    d_wmain = _mm_tn(h, dproj, "d_w_in", tk=2048)
    d_wdt = _mm_tn(h, ddt, "d_w_dt")
    d_w_in, d_w_in_bf16 = _ungroup_w_in(d_wmain, d_wdt, d, di, dc, nh)
    token = w_in_grad(d_w_in, d_w_in_bf16)
    gx, d_norm_g = _in_bwd(dproj, w_main, ddt, w_dt, x2, dx1, norm_g, token, tk=_tile(n_main, 1536, LANES))

    d_conv_w = jnp.swapaxes(d_conv_w.reshape(CONV_WIDTH, N_SHARD, dc // N_SHARD), 0, 1)
    grads = dict(norm_g=d_norm_g, w_in=d_w_in, conv_w=d_conv_w, conv_b=d_conv_b, dt_bias=d_dtb[:, :nh],
                 a_log=d_alog[:, :nh], d_skip=d_dsk[:, :nh], gnorm_g=d_gn, pool_mix_b=d_mix_b, pool_scale=d_scale,
                 ple_norm_g=d_ple_g, final_g=d_final_g, **early)
    return loss[0, 0], gx.reshape(nb, s, d), grads


def _place():
    return lax.axis_index("x"), lax.axis_index("y"), lax.axis_index("c")


def _other_chips(x, y):
    return [(1 - x, y), (x, 1 - y), (1 - x, 1 - y)]


def _halves(c, rows, align):
    rh = rows // 2
    assert rows % 2 == 0 and rh % align == 0, (rows, align)
    return (pl.ds(pl.multiple_of(c * rh, align), rh), pl.ds(pl.multiple_of((1 - c) * rh, align), rh))


HBM = pl.BlockSpec(memory_space=pl.ANY)


def _into_slot(w2, k, dtype, name):
    rows, cols = w2.shape
    rb = _tile(rows, 256)

    def body(k_ref, w_ref, o_ref):
        o_ref[0] = w_ref[...].astype(dtype)

    return pl.pallas_call(
        body, name=name,
        grid_spec=pltpu.PrefetchScalarGridSpec(
            num_scalar_prefetch=1, grid=(rows // rb,),
            in_specs=[pl.BlockSpec((rb, cols), lambda i, k_ref: (i, 0))],
            out_specs=pl.BlockSpec((1, rb, cols), lambda i, k_ref: (k_ref[0], i, 0))),
        out_shape=jax.ShapeDtypeStruct((N_SHARD, rows, cols), dtype),
        compiler_params=_params(("parallel",)))(k.reshape(1), w2)


def _gather_weights(split, whole):
    n_split, n_all = len(split), len(split) + len(whole)

    def body(*refs):
        bufs = refs[n_all:2 * n_all]
        send_sems, recv_sems = refs[2 * n_all:]
        x, y, c = _place()
        k, k_x, k_y, k_d = 2 * x + y, 2 * (1 - x) + y, 2 * x + (1 - y), 2 * (1 - x) + (1 - y)
        x_nb, y_nb, sib = (1 - x, y, c), (x, 1 - y, c), (x, y, 1 - c)

        def copy(idx, block, to):
            return pltpu.make_async_remote_copy(src_ref=block, dst_ref=block, send_sem=send_sems.at[idx],
                                                recv_sem=recv_sems.at[idx], device_id=to, device_id_type=MESH)

        started = []

        def start(idx, block, to):
            started.append(copy(idx, block, to))
            started[-1].start()

        for i in range(n_split):
            buf, s0 = bufs[i], 8 * i
            rh = buf.shape[1] // 2
            rq = rh // 2
            assert buf.shape[1] == 4 * rq and rq % 16 == 0, buf.shape

            def rows(core, part):
                lo = core * rh + (rq if part == "bottom" else 0)
                return pl.ds(pl.multiple_of(lo, 16), rh if part == "all" else rq)

            start(s0 + 0, buf.at[k, rows(c, "all")], x_nb)
            start(s0 + 1, buf.at[k, rows(c, "all")], y_nb)
            copy(s0 + 0, buf.at[k_x, rows(c, "all")], x_nb).wait_recv()
            start(s0 + 2, buf.at[k_x, rows(c, "top")], y_nb)
            start(s0 + 4, buf.at[k_x, rows(c, "all")], sib)
            copy(s0 + 1, buf.at[k_y, rows(c, "all")], y_nb).wait_recv()
            start(s0 + 3, buf.at[k_y, rows(c, "bottom")], x_nb)
            start(s0 + 5, buf.at[k_y, rows(c, "all")], sib)
            copy(s0 + 2, buf.at[k_d, rows(c, "top")], y_nb).wait_recv()
            start(s0 + 6, buf.at[k_d, rows(c, "top")], sib)
            copy(s0 + 3, buf.at[k_d, rows(c, "bottom")], x_nb).wait_recv()
            start(s0 + 7, buf.at[k_d, rows(c, "bottom")], sib)
            copy(s0 + 4, buf.at[k_x, rows(1 - c, "all")], sib).wait_recv()
            copy(s0 + 5, buf.at[k_y, rows(1 - c, "all")], sib).wait_recv()
            copy(s0 + 6, buf.at[k_d, rows(1 - c, "top")], sib).wait_recv()
            copy(s0 + 7, buf.at[k_d, rows(1 - c, "bottom")], sib).wait_recv()
        for i in range(n_split, n_all):
            s0 = 8 * n_split + 3 * (i - n_split)
            for j, (px, py) in enumerate(_other_chips(x, y)):
                start(s0 + j, bufs[i].at[k], (px, py, c))
            for j, (px, py) in enumerate(_other_chips(x, y)):
                copy(s0 + j, bufs[i].at[2 * px + py], (px, py, c)).wait_recv()
        for cp in started:
            cp.wait_send()

    arrays = list(split) + list(whole)
    n_sem = 8 * n_split + 3 * len(whole)
    return pl.pallas_call(
        body, name="gather_weights",
        in_specs=[HBM] * n_all, out_specs=[HBM] * n_all,
        out_shape=[jax.ShapeDtypeStruct(a.shape, a.dtype) for a in arrays],
        input_output_aliases={i: i for i in range(n_all)},
        scratch_shapes=[pltpu.SemaphoreType.DMA((n_sem,)), pltpu.SemaphoreType.DMA((n_sem,))],
    )(*arrays)


def _swap_halves(gs):
    n = len(gs)

    def body(*refs):
        ins, outs, send_sems, recv_sems = refs[:n], refs[n:2 * n], refs[2 * n], refs[2 * n + 1]
        x, y, c = _place()
        copies = []
        for i in range(n):
            _, theirs = _halves(c, gs[i].shape[1], 8)
            cp = pltpu.make_async_remote_copy(src_ref=ins[i].at[:, theirs], dst_ref=outs[i], send_sem=send_sems.at[i],
                                              recv_sem=recv_sems.at[i], device_id=(x, y, 1 - c), device_id_type=MESH)
            cp.start()
            copies.append(cp)
        for cp in copies:
            cp.wait()

    return pl.pallas_call(
        body, name="swap_halves", in_specs=[HBM] * n, out_specs=[HBM] * n,
        out_shape=[jax.ShapeDtypeStruct((g.shape[0], g.shape[1] // 2, g.shape[2]), g.dtype) for g in gs],
        scratch_shapes=[pltpu.SemaphoreType.DMA((n,)), pltpu.SemaphoreType.DMA((n,))],
    )(*gs)


SEM = pl.BlockSpec(memory_space=pltpu.SEMAPHORE)
IN_HBM = pl.BlockSpec(memory_space=pltpu.HBM)
SPLIT_EFFECT = pltpu.SideEffectType.DATAFLOW_SIDE_EFFECTING


def _split_copies(plan, refs, send_sems, recv_sems):
    pairs = []
    for idx, (src, dst, landing, to) in enumerate(plan(refs)):
        mk = lambda d: pltpu.make_async_remote_copy(src_ref=src, dst_ref=d, send_sem=send_sems.at[idx],
                                                    recv_sem=recv_sems.at[idx], device_id=to, device_id_type=MESH)
        pairs.append((mk(dst), mk(landing)))
    return pairs


def _split_start(name, bufs, after, plan, n_copies):
    n = len(bufs)

    def body(*refs):
        send_sems, recv_sems, token = refs[n + 1], refs[n + 2], refs[-1]
        for send, _ in _split_copies(plan, refs[:n], send_sems, recv_sems):
            send.start()
        token[...] = jnp.zeros_like(token)

    sems = pltpu.SemaphoreType.DMA((n_copies,))
    out = pl.pallas_call(
        body, name=name,
        in_specs=[IN_HBM] * n + [HBM],
        out_specs=[SEM, SEM] + [IN_HBM] * n + [pl.BlockSpec(memory_space=pltpu.VMEM)],
        out_shape=[sems, sems] + [pltpu.HBM(b.shape, b.dtype) for b in bufs] + [jax.ShapeDtypeStruct((8, LANES), F32)],
        input_output_aliases={i: 2 + i for i in range(n)},
        compiler_params=pltpu.CompilerParams(has_side_effects=SPLIT_EFFECT),
    )(*[pltpu.with_memory_space_constraint(b, pltpu.HBM) for b in bufs], after)
    return out[0], out[1], out[2:2 + n], out[-1]


def _split_wait(name, bufs, send_sems, recv_sems, after, plan):
    n = len(bufs)

    def body(*refs):
        for send, recv in _split_copies(plan, refs[:n], refs[n], refs[n + 1]):
            send.wait_send()
            recv.wait_recv()

    return pl.pallas_call(
        body, name=name,
        in_specs=[IN_HBM] * n + [SEM, SEM, HBM],
        out_specs=[IN_HBM] * n,
        out_shape=[pltpu.HBM(b.shape, b.dtype) for b in bufs],
        input_output_aliases={i: i for i in range(n)},
        compiler_params=pltpu.CompilerParams(has_side_effects=SPLIT_EFFECT),
    )(*bufs, send_sems, recv_sems, after)


def _gather_plan(n):
    def plan(refs):
        x, y, c = _place()
        k = 2 * x + y
        return [(refs[i].at[k], refs[i].at[k], refs[i].at[2 * px + py], (px, py, c))
                for i in range(n) for px, py in _other_chips(x, y)]
    return plan


def _scatter_plan(n):
    def plan(refs):
        x, y, c = _place()
        return [(refs[i].at[2 * px + py], refs[n + i].at[j], refs[n + i].at[j], (px, py, c))
                for i in range(n) for j, (px, py) in enumerate(_other_chips(x, y))]
    return plan


IN_VMEM = pl.BlockSpec(memory_space=pltpu.VMEM)


def _to_bf16_all(gs):
    n = len(gs)

    def body(*refs):
        for i in range(n):
            refs[n + i][...] = refs[i][...].astype(BF16)

    return pl.pallas_call(
        body, name="bf16_rest", in_specs=[IN_VMEM] * n, out_specs=[IN_VMEM] * n,
        out_shape=[jax.ShapeDtypeStruct(g.shape, BF16) for g in gs],
        compiler_params=_params())(*gs)


def _add_landed_all(gs, landed, k):
    n = len(gs)

    def body(k_ref, *refs):
        for i in range(n):
            g_ref, l_ref, o_ref = refs[i], refs[n + i], refs[2 * n + i]
            o_ref[...] = g_ref[k_ref[0]] + l_ref[0].astype(F32) + l_ref[1].astype(F32) + l_ref[2].astype(F32)

    return pl.pallas_call(
        body, name="add_landed_rest",
        in_specs=[pl.BlockSpec(memory_space=pltpu.SMEM)] + [IN_VMEM] * (2 * n), out_specs=[IN_VMEM] * n,
        out_shape=[jax.ShapeDtypeStruct(g.shape[1:], F32) for g in gs],
        compiler_params=_params())(k.reshape(1), *gs, *landed)


def _final_exchange(halves, wholes, small):
    nh, nw = len(halves), len(wholes)
    n_in = nh + nw + 1
    rows = small.shape[0]

    def body(*refs):
        w_in, v_ref = refs[nh:nh + nw], refs[nh + nw]
        h_out, w_out, o_ref = refs[n_in:n_in + nh], refs[n_in + nh:n_in + nh + nw], refs[n_in + nh + nw]
        buf_ref, send_sems, recv_sems = refs[2 * n_in:]
        x, y, c = _place()
        sib = (x, y, 1 - c)
        me = 4 * x + 2 * y + c

        def copy(idx, src, dst, to):
            return pltpu.make_async_remote_copy(src_ref=src, dst_ref=dst, send_sem=send_sems.at[idx],
                                                recv_sem=recv_sems.at[idx], device_id=to, device_id_type=MESH)

        started = []
        split = [_halves(c, h_out[i].shape[0], 8) for i in range(nh)]
        for i in range(nh):
            started.append(copy(i, h_out[i].at[split[i][0]], h_out[i].at[split[i][0]], sib))
        for i in range(nw):
            started.append(copy(nh + i, w_in[i], w_out[i], sib))
        buf_ref[me] = v_ref[...]
        for rel in range(1, 8):
            peer = (x ^ (rel >> 2), y ^ ((rel >> 1) & 1), c ^ (rel & 1))
            started.append(copy(nh + nw + rel - 1, v_ref, buf_ref.at[me], peer))
        for cp in started:
            cp.start()
        for i in range(nh):
            copy(i, h_out[i].at[split[i][1]], h_out[i].at[split[i][1]], sib).wait_recv()
        for i in range(nw):
            copy(nh + i, w_in[i], w_out[i], sib).wait_recv()
        for rel in range(1, 8):
            copy(nh + nw + rel - 1, v_ref, buf_ref.at[me ^ rel], (x, y, c)).wait_recv()
        for cp in started:
            cp.wait_send()
        acc = buf_ref[0]
        for i in range(1, 8):
            acc = acc + buf_ref[i]
        o_ref[...] = acc

    n_sem = nh + nw + 7
    out = pl.pallas_call(
        body, name="final_exchange",
        in_specs=[HBM] * (nh + nw) + [IN_VMEM], out_specs=[HBM] * (nh + nw) + [IN_VMEM],
        out_shape=[jax.ShapeDtypeStruct(a.shape, a.dtype) for a in list(halves) + list(wholes)]
        + [jax.ShapeDtypeStruct(small.shape, F32)],
        input_output_aliases={i: i for i in range(nh)},
        scratch_shapes=[pltpu.VMEM((8, rows, LANES), F32), pltpu.SemaphoreType.DMA((n_sem,)),
                        pltpu.SemaphoreType.DMA((n_sem,))],
    )(*halves, *wholes, small)
    return out[:nh], out[nh:nh + nw], out[-1]


def _add_pair(g, got, c, name):
    _, rh, cols = got.shape
    rb = _tile(rh, 256)
    nrb = rh // rb

    def body(c_ref, g_ref, got_ref, o_ref):
        o_ref[...] = (g_ref[...] + got_ref[...]).astype(BF16)

    spec = pl.BlockSpec((1, rb, cols), lambda j, i, c_ref: (j, i, 0))
    return pl.pallas_call(
        body, name=name,
        grid_spec=pltpu.PrefetchScalarGridSpec(
            num_scalar_prefetch=1, grid=(N_SHARD, nrb),
            in_specs=[pl.BlockSpec((1, rb, cols), lambda j, i, c_ref: (j, c_ref[0] * nrb + i, 0)), spec],
            out_specs=spec),
        out_shape=jax.ShapeDtypeStruct(got.shape, BF16),
        compiler_params=_params(("parallel", "parallel")))(c.reshape(1), g, got)


def _add_chips(g, got, landed, k, c, name):
    _, rh, cols = got.shape
    rb = _tile(rh, 256)
    nrb = rh // rb

    def body(kc_ref, g_ref, got_ref, l_ref, o_ref):
        own = g_ref[0] + got_ref[0]
        o_ref[...] = own + l_ref[0].astype(F32) + l_ref[1].astype(F32) + l_ref[2].astype(F32)

    half_c = lambda i, kc: (kc[1] * nrb + i, 0)
    return pl.pallas_call(
        body, name=name,
        grid_spec=pltpu.PrefetchScalarGridSpec(
            num_scalar_prefetch=1, grid=(nrb,),
            in_specs=[pl.BlockSpec((1, rb, cols), lambda i, kc: (kc[0],) + half_c(i, kc)),
                      pl.BlockSpec((1, rb, cols), lambda i, kc: (kc[0], i, 0)),
                      pl.BlockSpec((N_SHARD - 1, rb, cols), lambda i, kc: (0, i, 0))],
            out_specs=pl.BlockSpec((rb, cols), half_c)),
        out_shape=jax.ShapeDtypeStruct((2 * rh, cols), F32),
        compiler_params=_params(("parallel",)))(jnp.stack([k, c]), g, got, landed)


def _adamw_update(w_ref, g_refs, m_ref, v_ref, go_ref, d_ref, nm_ref, nv_ref):
    gv = g_refs[0][...]
    for ref in g_refs[1:]:
        gv = gv + ref[...]
    go_ref[...] = gv
    nm = ADAM_B1 * m_ref[...] + (1.0 - ADAM_B1) * gv
    nv = ADAM_B2 * v_ref[...] + (1.0 - ADAM_B2) * (gv * gv)
    nm_ref[...] = nm
    nv_ref[...] = nv
    m_hat = nm / (1.0 - ADAM_B1 ** ADAM_STEP)
    v_hat = nv / (1.0 - ADAM_B2 ** ADAM_STEP)
    d_ref[...] = -ADAM_LR * (m_hat / (jnp.sqrt(v_hat) + ADAM_EPS) + ADAM_WD * w_ref[...])


def _adamw_all(ws, gs, ms, vs):
    n, n_g = len(ws), len(gs[0])
    flat_g = [part for parts in gs for part in parts]

    def body(*refs):
        w_refs, g_refs = refs[:n], refs[n:n + n * n_g]
        m_refs, v_refs, outs = refs[n + n * n_g:2 * n + n * n_g], refs[2 * n + n * n_g:3 * n + n * n_g], refs[3 * n + n * n_g:]
        for i in range(n):
            _adamw_update(w_refs[i], g_refs[i * n_g:(i + 1) * n_g], m_refs[i], v_refs[i], *outs[4 * i:4 * i + 4])

    out = pl.pallas_call(
        body, name="adamw_rest", in_specs=[IN_VMEM] * (3 * n + n * n_g), out_specs=[IN_VMEM] * (4 * n),
        out_shape=[jax.ShapeDtypeStruct(w.shape, F32) for w in ws for _ in range(4)],
        compiler_params=_params())(*ws, *flat_g, *ms, *vs)
    return [out[4 * i:4 * i + 4] for i in range(n)]


def _adamw(wv, gs, m, v, name):
    rows, cols = wv.shape
    rb = _tile(rows, 256)
    n_g = len(gs)

    def body(*refs):
        _adamw_update(refs[0], refs[1:1 + n_g], *refs[1 + n_g:])

    spec = pl.BlockSpec((rb, cols), lambda i: (i, 0))
    return pl.pallas_call(
        body, name=name, grid=(rows // rb,), in_specs=[spec] * (3 + n_g), out_specs=[spec] * 4,
        out_shape=[jax.ShapeDtypeStruct((rows, cols), F32)] * 4,
        compiler_params=_params(("parallel",)))(wv, *gs, m, v)


def _pack(flats):
    cat = jnp.concatenate([f.reshape(-1) for f in flats])
    n = cat.shape[0]
    rows = -(-n // (8 * LANES)) * 8
    return jnp.pad(cat, (0, rows * LANES - n)).reshape(rows, LANES)


def _unpack(packed, shapes):
    flat = packed.reshape(-1)
    out, off = [], 0
    for shp in shapes:
        n = 1
        for dim in shp:
            n *= dim
        out.append(flat[off:off + n].reshape(shp))
        off += n
    return out


def kernel(x, p, norm_g, w_in, conv_w, conv_b, dt_bias, a_log, d_skip, gnorm_g, pool_mix_w, pool_mix_b, pool_scale, w_branch_a, w_branch_b, w_out, ple_norm_g, w_ple_gate, w_ple_up, final_g, loss_target, m_norm_g, m_w_in, m_conv_w, m_conv_b, m_dt_bias, m_a_log, m_d_skip, m_gnorm_g, m_pool_mix_w, m_pool_mix_b, m_pool_scale, m_w_branch_a, m_w_branch_b, m_w_out, m_ple_norm_g, m_w_ple_gate, m_w_ple_up, m_final_g, v_norm_g, v_w_in, v_conv_w, v_conv_b, v_dt_bias, v_a_log, v_d_skip, v_gnorm_g, v_pool_mix_w, v_pool_mix_b, v_pool_scale, v_w_branch_a, v_w_branch_b, v_w_out, v_ple_norm_g, v_w_ple_gate, v_w_ple_up, v_final_g):
    wts = dict(norm_g=norm_g, w_in=w_in, conv_w=conv_w, conv_b=conv_b, dt_bias=dt_bias, a_log=a_log, d_skip=d_skip,
               gnorm_g=gnorm_g, pool_mix_w=pool_mix_w, pool_mix_b=pool_mix_b, pool_scale=pool_scale,
               w_branch_a=w_branch_a, w_branch_b=w_branch_b, w_out=w_out, ple_norm_g=ple_norm_g,
               w_ple_gate=w_ple_gate, w_ple_up=w_ple_up, final_g=final_g)
    mom_m = dict(norm_g=m_norm_g, w_in=m_w_in, conv_w=m_conv_w, conv_b=m_conv_b, dt_bias=m_dt_bias, a_log=m_a_log,
                 d_skip=m_d_skip, gnorm_g=m_gnorm_g, pool_mix_w=m_pool_mix_w, pool_mix_b=m_pool_mix_b,
                 pool_scale=m_pool_scale, w_branch_a=m_w_branch_a, w_branch_b=m_w_branch_b, w_out=m_w_out,
                 ple_norm_g=m_ple_norm_g, w_ple_gate=m_w_ple_gate, w_ple_up=m_w_ple_up, final_g=m_final_g)
    mom_v = dict(norm_g=v_norm_g, w_in=v_w_in, conv_w=v_conv_w, conv_b=v_conv_b, dt_bias=v_dt_bias, a_log=v_a_log,
                 d_skip=v_d_skip, gnorm_g=v_gnorm_g, pool_mix_w=v_pool_mix_w, pool_mix_b=v_pool_mix_b,
                 pool_scale=v_pool_scale, w_branch_a=v_w_branch_a, w_branch_b=v_w_branch_b, w_out=v_w_out,
                 ple_norm_g=v_ple_norm_g, w_ple_gate=v_w_ple_gate, w_ple_up=v_w_ple_up, final_g=v_final_g)
    c = lax.axis_index("c")
    k = 2 * lax.axis_index("x") + lax.axis_index("y")
    flat2 = lambda a: a.reshape(-1, a.shape[-1])

    slots = {n: _into_slot(flat2(wts[n]), k, BF16, "slot_" + n) for n in BIG}
    w_in_g, conv_g = _gather_weights([slots["w_in"]], [_into_slot(flat2(conv_w), k, F32, "slot_conv_w")])
    n_rest = len(REST)
    gsend, grecv, gbufs, gtoken = _split_start("gather_rest_start", [slots[n] for n in REST], conv_g,
                                               _gather_plan(n_rest), 3 * n_rest)

    def rest_weights(after):
        return dict(zip(REST, _split_wait("gather_rest_wait", gbufs, gsend, grecv, after, _gather_plan(n_rest))))

    flying = {}

    def early_grads(early):
        sends = list(_to_bf16_all([early[n] for n in REST]))
        lands = [pltpu.with_memory_space_constraint(lax.empty((N_SHARD - 1,) + v.shape[1:], BF16), pltpu.HBM)
                 for v in sends]
        ssend, srecv, sbufs, stoken = _split_start("scatter_rest_start", sends + lands, early[REST[0]],
                                                   _scatter_plan(n_rest), 3 * n_rest)
        flying.update(send=ssend, recv=srecv, bufs=sbufs)
        return stoken

    def w_in_grad(g_w_in, g_w_in_bf16):
        got = _swap_halves([g_w_in_bf16])[0]
        pair = _add_pair(g_w_in, got, c, "add_pair_w_in")
        land = pltpu.with_memory_space_constraint(lax.empty((N_SHARD - 1,) + pair.shape[1:], BF16), pltpu.HBM)
        wsend, wrecv, wbufs, wtoken = _split_start("scatter_w_in_start", [pair, land], got, _scatter_plan(1), 3)
        flying.update(w_send=wsend, w_recv=wrecv, w_bufs=wbufs, w_got=got)
        return wtoken

    small = {n: wts[n] for n in SMALL}
    small["norm_g"] = norm_g + gtoken[0, 0]
    loss, grad_x, grads = _local_step(x, p[0], loss_target, dict(w_in=w_in_g, conv_w=conv_g), small,
                                      rest_weights, early_grads, w_in_grad)
    g_w_in = grads["w_in"]
    landed = _split_wait("scatter_w_in_wait", flying["w_bufs"], flying["w_send"], flying["w_recv"], grad_x,
                         _scatter_plan(1))[1]
    w_in_half = _add_chips(g_w_in, flying["w_got"], landed, k, c, "add_chips_w_in")

    sbufs = _split_wait("scatter_rest_wait", flying["bufs"], flying["send"], flying["recv"], g_w_in,
                        _scatter_plan(n_rest))
    mine = _add_landed_all([grads[n] for n in REST], list(sbufs[n_rest:]), k)
    (w_in_sum,), theirs, small_sum = _final_exchange(
        [w_in_half], mine, _pack([grads[n] for n in SMALL] + [grads["conv_w"], loss]))
    g_sums = dict(zip(REST, zip(mine, theirs)))
    g_sums["w_in"] = (w_in_sum,)

    conv_shape = flat2(conv_w).shape
    small_shapes = [wts[n].shape for n in SMALL] + [(N_SHARD,) + conv_shape, (1,)]
    small_g = _unpack(small_sum, small_shapes)
    g_conv = lax.dynamic_index_in_dim(small_g[-2], k, axis=0, keepdims=False)

    outs = {}
    large = ("w_in", "w_branch_a")
    little = [n for n in BIG if n not in large]
    updates = {n: _adamw(flat2(wts[n]), g_sums[n], flat2(mom_m[n]), flat2(mom_v[n]), "adamw_" + n) for n in large}
    updates.update(zip(little, _adamw_all([flat2(wts[n]) for n in little], [g_sums[n] for n in little],
                                          [flat2(mom_m[n]) for n in little], [flat2(mom_v[n]) for n in little])))
    for n in BIG:
        for kind, val in zip(("grad", "delta", "new_m", "new_v"), updates[n]):
            outs[kind, n] = val.reshape(wts[n].shape)
    names = SMALL + ("conv_w",)
    sm = _adamw(_pack([wts[n] for n in names]), (_pack(small_g[:len(SMALL)] + [g_conv]),),
                _pack([mom_m[n] for n in names]), _pack([mom_v[n] for n in names]), "adamw_small")
    sm_shapes = [wts[n].shape for n in names]
    for kind, val in zip(("grad", "delta", "new_m", "new_v"), sm):
        for n, piece in zip(names, _unpack(val, sm_shapes)):
            outs[kind, n] = piece
    return (small_g[-1][0], grad_x, *[outs[kind, n] for kind in ("grad", "delta", "new_m", "new_v") for n in WEIGHTS])
```

```python
import functools

import jax
import jax.numpy as jnp
from jax import lax
from jax.experimental import pallas as pl
from jax.experimental.pallas import tpu as pltpu

F32 = jnp.float32
BF16 = jnp.bfloat16
HIGHEST = lax.Precision.HIGHEST
MESH = pl.DeviceIdType.MESH

EPS = 1e-6
HEAD_DIM = 64
SSM_GROUPS = 4
D_STATE = 128
CONV_WIDTH = 4
CHUNK = 128
N_POOL = 4
LANES = 128
N_SHARD = 4

ADAM_LR = 0.001
ADAM_B1 = 0.9
ADAM_B2 = 0.999
ADAM_EPS = 1e-08
ADAM_WD = 0.01
ADAM_STEP = 10

BIG = ("w_in", "pool_mix_w", "w_branch_a", "w_branch_b", "w_out", "w_ple_gate", "w_ple_up")
REST = BIG[1:]
SMALL = ("norm_g", "conv_b", "dt_bias", "a_log", "d_skip", "gnorm_g", "pool_mix_b", "pool_scale",
         "ple_norm_g", "final_g")
WEIGHTS = ("norm_g", "w_in", "conv_w", "conv_b", "dt_bias", "a_log", "d_skip", "gnorm_g", "pool_mix_w",
           "pool_mix_b", "pool_scale", "w_branch_a", "w_branch_b", "w_out", "ple_norm_g", "w_ple_gate",
           "w_ple_up", "final_g")


def _params(sem=None, vmem_mb=56):
    kw = dict(vmem_limit_bytes=vmem_mb << 20)
    if sem is not None:
        kw["dimension_semantics"] = sem
    return pltpu.CompilerParams(**kw)


def _sigmoid(v):
    return 0.5 * jnp.tanh(0.5 * v) + 0.5


def _sigmoid_tail(v):
    return 1.0 / (1.0 + jnp.exp(-v))


def _softplus(v):
    return jnp.maximum(v, 0.0) + jnp.log1p(jnp.exp(-jnp.abs(v)))


def _bdot(a, b):
    return jnp.dot(a.astype(BF16), b.astype(BF16), preferred_element_type=F32)


def _bdot_nt(a, b):
    return lax.dot_general(a.astype(BF16), b.astype(BF16), (((1,), (1,)), ((), ())), preferred_element_type=F32)


def _bdot_tn(a, b):
    return lax.dot_general(a.astype(BF16), b.astype(BF16), (((0,), (0,)), ((), ())), preferred_element_type=F32)


def _col_block(col0, width):
    assert col0 % width == 0, (col0, width)
    return col0 // width


def _tile(n, cap, unit=8):
    if n <= cap:
        return n
    best = None
    for cand in range(unit, cap + 1, unit):
        if n % cand == 0:
            best = cand
    assert best is not None, (n, cap)
    return best


def _shift_down(v, j, row):
    return jnp.where(row >= j, pltpu.roll(v, j, 0), 0.0)


def _shift_up(v, j, row):
    n = v.shape[0]
    return jnp.where(row < n - j, pltpu.roll(v, n - j, 0), 0.0)


def _mm(a, w, name, tm=1024, tn=1024):
    t, k = a.shape
    n = w.shape[1]
    tm, tn = min(tm, t), min(tn, n)

    def body(a_ref, w_ref, o_ref):
        o_ref[...] = _bdot(a_ref[...], w_ref[...]).astype(BF16)

    return pl.pallas_call(
        body, name=name, grid=(t // tm, n // tn),
        in_specs=[pl.BlockSpec((tm, k), lambda i, j: (i, 0)), pl.BlockSpec((k, tn), lambda i, j: (0, j))],
        out_specs=pl.BlockSpec((tm, tn), lambda i, j: (i, j)),
        out_shape=jax.ShapeDtypeStruct((t, n), BF16),
        compiler_params=_params(("parallel", "parallel")))(a, w)


def _mm_nt(a, w, name, tm=1024, tk=1024):
    t, k = a.shape
    n = w.shape[0]
    tm, tk = min(tm, t), min(tk, k)

    def body(a_ref, w_ref, o_ref):
        kk = pl.program_id(1)
        part = _bdot_nt(a_ref[...], w_ref[...])

        @pl.when(kk == 0)
        def _():
            o_ref[...] = part

        @pl.when(kk > 0)
        def _():
            o_ref[...] += part

    return pl.pallas_call(
        body, name=name, grid=(t // tm, k // tk),
        in_specs=[pl.BlockSpec((tm, tk), lambda i, j: (i, j)), pl.BlockSpec((n, tk), lambda i, j: (0, j))],
        out_specs=pl.BlockSpec((tm, n), lambda i, j: (i, 0)),
        out_shape=jax.ShapeDtypeStruct((t, n), F32),
        compiler_params=_params(("parallel", "arbitrary")))(a, w)


def _mm_tn(a, b, name, tn=1024, tk=2048, col_blocks=False):
    t, m = a.shape
    n = b.shape[1]
    tn, tk = min(tn, n), min(tk, t)

    def body(a_ref, b_ref, o_ref):
        kk = pl.program_id(1)
        part = _bdot_tn(a_ref[...], b_ref[...])
        part = part[None] if col_blocks else part

        @pl.when(kk == 0)
        def _():
            o_ref[...] = part

        @pl.when(kk > 0)
        def _():
            o_ref[...] += part

    if col_blocks:
        out_spec = pl.BlockSpec((1, m, tn), lambda j, kk: (j, 0, 0))
        out_shape = jax.ShapeDtypeStruct((n // tn, m, tn), F32)
    else:
        out_spec = pl.BlockSpec((m, tn), lambda j, kk: (0, j))
        out_shape = jax.ShapeDtypeStruct((m, n), F32)
    return pl.pallas_call(
        body, name=name, grid=(n // tn, t // tk),
        in_specs=[pl.BlockSpec((tk, m), lambda j, kk: (kk, 0)), pl.BlockSpec((tk, tn), lambda j, kk: (kk, j))],
        out_specs=out_spec, out_shape=out_shape,
        compiler_params=_params(("parallel", "arbitrary")))(a, b)


def _w_in_pieces(d, di, dc, nh, shard_w):
    pgd = d // N_POOL
    o_dt, o_u = di + dc, di + dc + nh
    o_zp, o_ga, o_gb = o_u + d, o_u + 2 * d, o_u + 3 * d
    c_z, c_uz = 2 * d, 2 * d + di + dc
    runs = [(False, 0, o_ga, d), (False, d, o_gb, d), (False, c_z, 0, di + dc), (True, 0, o_dt, nh)]
    for g in range(N_POOL):
        runs.append((False, c_uz + 2 * g * pgd, o_u + g * pgd, pgd))
        runs.append((False, c_uz + (2 * g + 1) * pgd, o_zp + g * pgd, pgd))
    pieces = []
    for is_dt, dst, src, n in runs:
        while n > 0:
            k, off = divmod(src, shard_w)
            m = min(n, shard_w - off)
            pieces.append((is_dt, dst, k, off, m))
            dst, src, n = dst + m, src + m, n - m
    return pieces


def _regroup_w_in(w_sh, d, di, dc, nh, rb=256):
    _, rows, sw = w_sh.shape
    n_main = 4 * d + di + dc
    pieces = _w_in_pieces(d, di, dc, nh, sw)
    rb = min(rb, rows)

    def body(w_ref, main_ref, dt_ref):
        dt_ref[...] = jnp.zeros_like(dt_ref)
        for is_dt, dst, k, off, m in pieces:
            out = dt_ref if is_dt else main_ref
            out[:, dst:dst + m] = w_ref[k, :, off:off + m]

    return pl.pallas_call(
        body, name="regroup_w_in", grid=(rows // rb,),
        in_specs=[pl.BlockSpec((N_SHARD, rb, sw), lambda i: (0, i, 0))],
        out_specs=[pl.BlockSpec((rb, n_main), lambda i: (i, 0)), pl.BlockSpec((rb, LANES), lambda i: (i, 0))],
        out_shape=[jax.ShapeDtypeStruct((rows, n_main), w_sh.dtype), jax.ShapeDtypeStruct((rows, LANES), w_sh.dtype)],
        compiler_params=_params(("parallel",)))(w_sh)


def _ungroup_w_in(d_main, d_dt, d, di, dc, nh, rb=128):
    rows, n_main = d_main.shape
    sw = (n_main + nh) // N_SHARD
    pieces = _w_in_pieces(d, di, dc, nh, sw)
    rb = min(rb, rows)

    def body(main_ref, dt_ref, o_ref, ob_ref):
        for is_dt, dst, k, off, m in pieces:
            piece = (dt_ref if is_dt else main_ref)[:, dst:dst + m]
            o_ref[k, :, off:off + m] = piece
            ob_ref[k, :, off:off + m] = piece.astype(BF16)

    spec = pl.BlockSpec((N_SHARD, rb, sw), lambda i: (0, i, 0))
    return pl.pallas_call(
        body, name="ungroup_w_in", grid=(rows // rb,),
        in_specs=[pl.BlockSpec((rb, n_main), lambda i: (i, 0)), pl.BlockSpec((rb, LANES), lambda i: (i, 0))],
        out_specs=[spec, spec],
        out_shape=[jax.ShapeDtypeStruct((N_SHARD, rows, sw), F32), jax.ShapeDtypeStruct((N_SHARD, rows, sw), BF16)],
        compiler_params=_params(("parallel",)))(d_main, d_dt)


def _inproj(x2, norm_g, w_main, w_dt, tm=1024, tn=1024):
    t, d = x2.shape
    n = w_main.shape[1]
    tm, tn = min(tm, t), min(tn, n)

    def body(x_ref, g_ref, w_ref, wdt_ref, proj_ref, dt_ref, h_ref):
        @pl.when(pl.program_id(1) == 0)
        def _():
            xv = x_ref[...]
            r = lax.rsqrt(jnp.mean(xv * xv, axis=-1, keepdims=True) + EPS)
            h = (xv * r * g_ref[...]).astype(BF16)
            h_ref[...] = h
            dt_ref[...] = jnp.dot(h, wdt_ref[...].astype(BF16), preferred_element_type=F32)

        proj_ref[...] = jnp.dot(h_ref[...], w_ref[...].astype(BF16), preferred_element_type=F32)

    return pl.pallas_call(
        body, name="inproj", grid=(t // tm, n // tn),
        in_specs=[pl.BlockSpec((tm, d), lambda i, j: (i, 0)), pl.BlockSpec((1, d), lambda i, j: (0, 0)),
                  pl.BlockSpec((d, tn), lambda i, j: (0, j)), pl.BlockSpec((d, LANES), lambda i, j: (0, 0))],
        out_specs=[pl.BlockSpec((tm, tn), lambda i, j: (i, j)), pl.BlockSpec((tm, LANES), lambda i, j: (i, 0)),
                   pl.BlockSpec((tm, d), lambda i, j: (i, 0))],
        out_shape=[jax.ShapeDtypeStruct((t, n), F32), jax.ShapeDtypeStruct((t, LANES), F32),
                   jax.ShapeDtypeStruct((t, d), BF16)],
        compiler_params=_params(("parallel", "arbitrary")))(x2, norm_g, w_main, w_dt)


def _conv_w_spec(conv_w, cb, j_axis):
    dc = conv_w.shape[1]
    assert dc % cb == 0, (dc, cb)
    return dc // cb, pl.BlockSpec((CONV_WIDTH, cb), lambda *ij: (0, ij[j_axis]))


CONV_ROWS = 64
CONV_HALO = 8


def _conv_taps(x_ref, t0, rc):
    if t0 == 0:
        cur = x_ref[0:rc, :]
        row = lax.broadcasted_iota(jnp.int32, cur.shape, 0)
        return [cur] + [_shift_down(cur, j, row) for j in range(1, CONV_WIDTH)]
    ext = x_ref[t0 - CONV_HALO:t0 + rc, :]
    return [ext[CONV_HALO:]] + [pltpu.roll(ext, j, 0)[CONV_HALO:] for j in range(1, CONV_WIDTH)]


def _conv_weights(w_ref):
    return [w_ref[CONV_WIDTH - 1 - j:CONV_WIDTH - j, :] for j in range(CONV_WIDTH)]


def _conv_pre(taps, wts, bias):
    acc = bias + taps[0] * wts[0]
    for j in range(1, CONV_WIDTH):
        acc = acc + taps[j] * wts[j]
    return acc


def _conv_fwd(proj, conv_w, conv_b, nb, s, col0, cb=512):
    n_blk, w_spec = _conv_w_spec(conv_w, cb, 1)
    blk0 = _col_block(col0, cb)
    rc = min(CONV_ROWS, s)

    def body(x_ref, w_ref, b_ref, o_ref):
        wts, bias = _conv_weights(w_ref), b_ref[...]
        for t0 in range(0, s, rc):
            acc = _conv_pre(_conv_taps(x_ref, t0, rc), wts, bias)
            o_ref[t0:t0 + rc, :] = acc * _sigmoid(acc)

    return pl.pallas_call(
        body, name="conv_fwd", grid=(nb, n_blk),
        in_specs=[pl.BlockSpec((s, cb), lambda b, j: (b, blk0 + j)), w_spec, pl.BlockSpec((1, cb), lambda b, j: (0, j))],
        out_specs=pl.BlockSpec((s, cb), lambda b, j: (b, j)),
        out_shape=jax.ShapeDtypeStruct((nb * s, n_blk * cb), F32),
        compiler_params=_params(("parallel", "parallel")))(proj, conv_w, conv_b)


def _ssd_consts(di):
    r = lax.broadcasted_iota(jnp.int32, (CHUNK, CHUNK), 0)
    c = lax.broadcasted_iota(jnp.int32, (CHUNK, CHUNK), 1)
    tril = (r >= c).astype(F32)
    head = lax.broadcasted_iota(jnp.int32, (LANES, di), 0)
    chan = lax.broadcasted_iota(jnp.int32, (LANES, di), 1) // HEAD_DIM
    expand = (head == chan).astype(BF16)
    return tril, expand, expand.T


def _expand(v, e, terms=3):
    acc = None
    for _ in range(terms):
        vb = v.astype(BF16)
        part = jnp.dot(vb, e, preferred_element_type=F32)
        acc = part if acc is None else acc + part
        v = v - vb.astype(F32)
    return acc


def _head_sum(t, et, terms=2):
    acc = None
    for _ in range(terms):
        tb = t.astype(BF16)
        part = jnp.dot(tb, et, preferred_element_type=F32)
        acc = part if acc is None else acc + part
        t = t - tb.astype(F32)
    return acc


def _ssd_scalars(dtr_ref, dtb_ref, alog_ref, tri):
    dtpre = dtr_ref[...] + dtb_ref[...]
    dt = _softplus(dtpre)
    a_neg = -jnp.exp(alog_ref[...])
    a_dt = dt * a_neg
    a_cs = jnp.dot(tri, a_dt, precision=HIGHEST, preferred_element_type=F32)
    a_cst = lax.dot_general(a_dt, tri, (((0,), (1,)), ((), ())), precision=HIGHEST, preferred_element_type=F32)
    return dtpre, dt, a_neg, a_cs, a_cst


def _ssd_fwd(proj, xbc, dtraw, dtb, alog, dskx, gn, nb, s, di, z_col0):
    t = nb * s
    nc = s // CHUNK
    hpg = di // HEAD_DIM // SSM_GROUPS
    gw = di // SSM_GROUPS
    gn_w = SSM_GROUPS * D_STATE
    b_blk = _col_block(di, gn_w)
    z_blk = _col_block(z_col0, di)
    L, P, N = CHUNK, HEAD_DIM, D_STATE
    tril, expand, _ = _ssd_consts(di)

    def body(z_ref, x_ref, b_ref, c_ref, dtr_ref, dtb_ref, alog_ref, dskx_ref, gn_ref, tril_ref, e_ref,
             ypre_ref, yan_ref, hp_ref, st_ref, yd_ref, xdt_ref):
        @pl.when(pl.program_id(1) == 0)
        def _():
            st_ref[...] = jnp.zeros_like(st_ref)

        hp_ref[0] = st_ref[...]
        tri = tril_ref[...]
        _, dt, _, a_cs, a_cst = _ssd_scalars(dtr_ref, dtb_ref, alog_ref, tri)
        ev = e_ref[...]
        a_exp = _expand(a_cs, ev)
        xv = x_ref[...]
        xdt = xv * _expand(dt, ev, terms=2)
        xdt_ref[...] = xdt
        a_last = a_exp[L - 1:L, :]
        xe = xdt * jnp.exp(a_last - a_exp)
        ea = jnp.exp(a_exp)
        e_last = jnp.exp(a_last)
        lower = tri > 0.5
        for g in range(SSM_GROUPS):
            gs = slice(g * gw, (g + 1) * gw)
            bg = b_ref[:, g * N:(g + 1) * N].astype(BF16)
            cg = c_ref[:, g * N:(g + 1) * N].astype(BF16)
            gm = _bdot_nt(cg, bg)
            ht = st_ref[:, gs]
            ch = _bdot(cg, ht)
            for e in range(hpg):
                h = g * hpg + e
                hs = slice(h * P, (h + 1) * P)
                decay = jnp.where(lower, jnp.exp(a_cs[:, h:h + 1] - a_cst[h:h + 1, :]), 0.0)
                yd_ref[:, hs] = _bdot(gm * decay, xdt_ref[:, hs])
            st_ref[:, gs] = ht * e_last[:, gs] + _bdot_tn(bg, xe[:, gs])
            ypre = yd_ref[:, gs] + ea[:, gs] * ch + xv[:, gs] * dskx_ref[:, gs]
            ypre_ref[:, gs] = ypre
            zv = z_ref[:, gs]
            v = ypre * zv * _sigmoid(zv)
            r = lax.rsqrt(jnp.mean(v * v, axis=-1, keepdims=True) + EPS)
            yan_ref[:, gs] = (v * r * gn_ref[:, gs]).astype(BF16)

    row = lambda b, c: b * nc + c
    vec = lambda w: pl.BlockSpec((1, w), lambda b, c: (0, 0))
    return pl.pallas_call(
        body, name="ssd_fwd", grid=(nb, nc),
        in_specs=[pl.BlockSpec((L, di), lambda b, c: (row(b, c), z_blk)),
                  pl.BlockSpec((L, di), lambda b, c: (row(b, c), 0)),
                  pl.BlockSpec((L, gn_w), lambda b, c: (row(b, c), b_blk)),
                  pl.BlockSpec((L, gn_w), lambda b, c: (row(b, c), b_blk + 1)),
                  pl.BlockSpec((L, LANES), lambda b, c: (row(b, c), 0)),
                  vec(LANES), vec(LANES), vec(di), vec(di),
                  pl.BlockSpec((L, L), lambda b, c: (0, 0)),
                  pl.BlockSpec((LANES, di), lambda b, c: (0, 0))],
        out_specs=[pl.BlockSpec((L, di), lambda b, c: (row(b, c), 0)),
                   pl.BlockSpec((L, di), lambda b, c: (row(b, c), 0)),
                   pl.BlockSpec((1, N, di), lambda b, c: (row(b, c), 0, 0))],
        out_shape=[jax.ShapeDtypeStruct((t, di), F32), jax.ShapeDtypeStruct((t, di), BF16),
                   jax.ShapeDtypeStruct((nb * nc, N, di), F32)],
        scratch_shapes=[pltpu.VMEM((N, di), F32), pltpu.VMEM((L, di), F32), pltpu.VMEM((L, di), F32)],
        compiler_params=_params(("parallel", "arbitrary")))(
            proj, xbc, xbc, xbc, dtraw, dtb, alog, dskx, gn, tril, expand)


def _pool_sum(v, g, row, shift):
    s2 = v + shift(v, 1, row)
    s4 = s2 + shift(s2, 2, row)
    s8 = s4 + shift(s4, 4, row)
    s16 = s8 + shift(s8, 8, row)
    return jnp.where(g == 0, s2, jnp.where(g == 1, s4, jnp.where(g == 2, s8, s16)))


def _pool_count(g, row):
    return jnp.minimum(row + 1, jnp.left_shift(2, g)).astype(F32)


POOL_ROWS = 128
POOL_HALO = 16


def _roll_sum(v, g, step):
    n = v.shape[0]
    s2 = v + pltpu.roll(v, step % n, 0)
    s4 = s2 + pltpu.roll(s2, (2 * step) % n, 0)
    s8 = s4 + pltpu.roll(s4, (4 * step) % n, 0)
    s16 = s8 + pltpu.roll(s8, (8 * step) % n, 0)
    return jnp.where(g == 0, s2, jnp.where(g == 1, s4, jnp.where(g == 2, s8, s16)))


def _pool_trailing(u_ref, pgd, t0, rc, g):
    if t0 == 0:
        cur = u_ref[0:rc, :pgd]
        row = lax.broadcasted_iota(jnp.int32, cur.shape, 0)
        return cur, _pool_sum(cur, g, row, _shift_down), _pool_count(g, row)
    ext = u_ref[t0 - POOL_HALO:t0 + rc, :pgd]
    return ext[POOL_HALO:], _roll_sum(ext, g, 1)[POOL_HALO:], jnp.left_shift(2, g).astype(F32)


def _pool_fwd(proj, mix_w, mix_b, scale, nb, s, col0):
    pgd = mix_w.shape[-1]
    blk0 = _col_block(col0, 2 * pgd)
    rc = min(POOL_ROWS, s)

    def body(uz_ref, w_ref, b_ref, sc_ref, o_ref):
        g = pl.program_id(1)
        wv = w_ref[:, 0].reshape(pgd, pgd)
        for t0 in range(0, s, rc):
            u, win, cnt = _pool_trailing(uz_ref, pgd, t0, rc, g)
            zp = uz_ref[t0:t0 + rc, pgd:]
            mixed = _bdot(win / cnt - u, wv) + b_ref[...]
            o_ref[t0:t0 + rc, :] = (mixed * sc_ref[...] * zp * _sigmoid(zp)).astype(BF16)

    return pl.pallas_call(
        body, name="pool_fwd", grid=(nb, N_POOL),
        in_specs=[pl.BlockSpec((s, 2 * pgd), lambda b, g: (b, blk0 + g)),
                  pl.BlockSpec((N_SHARD, 1, pgd // N_SHARD, pgd), lambda b, g: (0, g, 0, 0)),
                  pl.BlockSpec((1, pgd), lambda b, g: (0, g)), pl.BlockSpec((1, pgd), lambda b, g: (0, g))],
        out_specs=pl.BlockSpec((s, pgd), lambda b, g: (b, g)),
        out_shape=jax.ShapeDtypeStruct((nb * s, N_POOL * pgd), BF16),
        compiler_params=_params(("parallel", "parallel")))(proj, mix_w, mix_b, scale)


def _mid_fwd(ya, yb, proj, col0, x2, p2, tgt, wo, wpg, wup, ple_g, final_g, tm=256):
    t, d = ya.shape
    tm = min(tm, t)
    blk = _col_block(col0, 2 * d)
    n_up, pdim, up_w = wup.shape

    def body(ya_ref, yb_ref, g_ref, x_ref, p_ref, tgt_ref, wo_ref, wpg_ref, wup_ref, pg_ref, g_fin_ref,
             merged_ref, hn_ref, dpre_ref, dpu_ref, x1_ref, dx2_ref, loss_ref, dg_ref):
        @pl.when(pl.program_id(0) == 0)
        def _():
            loss_ref[...] = jnp.zeros_like(loss_ref)
            dg_ref[...] = jnp.zeros_like(dg_ref)

        merged = (_sigmoid(g_ref[:, :d]) * ya_ref[...] + _sigmoid(g_ref[:, d:]) * yb_ref[...]).astype(BF16)
        merged_ref[...] = merged
        x1 = x_ref[...] + jnp.dot(merged, wo_ref[...], preferred_element_type=F32)
        x1_ref[...] = x1
        r1 = lax.rsqrt(jnp.mean(x1 * x1, axis=-1, keepdims=True) + EPS)
        hn = (x1 * r1 * pg_ref[...]).astype(BF16)
        hn_ref[...] = hn
        gate = _sigmoid(jnp.dot(hn, wpg_ref[...], preferred_element_type=F32))
        pb = p_ref[...].astype(BF16)
        pu = jnp.concatenate([jnp.dot(pb, wup_ref[j], preferred_element_type=F32) for j in range(n_up)], axis=1)
        x2 = x1 + gate * pu
        r = lax.rsqrt(jnp.mean(x2 * x2, axis=-1, keepdims=True) + EPS)
        xn = x2 * r
        fg = g_fin_ref[...]
        err = xn * fg - tgt_ref[...]
        loss_ref[...] += 0.5 * jnp.sum(jnp.mean(err * err, axis=-1, keepdims=True))
        dy = err * (1.0 / d)
        dg_ref[...] += jnp.sum(dy * xn, axis=0, keepdims=True)
        dxn = dy * fg
        dx2 = r * (dxn - xn * jnp.mean(dxn * xn, axis=-1, keepdims=True))
        dx2_ref[...] = dx2
        dpre_ref[...] = (dx2 * pu * gate * (1.0 - gate)).astype(BF16)
        dpu_ref[...] = (dx2 * gate).astype(BF16)

    row = pl.BlockSpec((tm, d), lambda i: (i, 0))
    vec = pl.BlockSpec((1, d), lambda i: (0, 0))
    whole = lambda a: pl.BlockSpec(a.shape, lambda i: (0,) * a.ndim)
    return pl.pallas_call(
        body, name="mid_fwd", grid=(t // tm,),
        in_specs=[row, row, pl.BlockSpec((tm, 2 * d), lambda i: (i, blk)), row,
                  pl.BlockSpec((tm, pdim), lambda i: (i, 0)), row, whole(wo), whole(wpg), whole(wup), vec, vec],
        out_specs=[row] * 6 + [pl.BlockSpec((1, LANES), lambda i: (0, 0)), vec],
        out_shape=[jax.ShapeDtypeStruct((t, d), BF16)] * 4 + [jax.ShapeDtypeStruct((t, d), F32)] * 2 + [
            jax.ShapeDtypeStruct((1, LANES), F32), jax.ShapeDtypeStruct((1, d), F32)],
        compiler_params=_params(("arbitrary",)))(ya, yb, proj, x2, p2, tgt, wo, wpg, wup, ple_g, final_g)


def _rms_grad(xv, dh, g):
    r = lax.rsqrt(jnp.mean(xv * xv, axis=-1, keepdims=True) + EPS)
    xn = xv * r
    dd = dh * g
    return r * (dd - xn * jnp.mean(dd * xn, axis=-1, keepdims=True)), jnp.sum(dh * xn, axis=0, keepdims=True)


def _mid_bwd(dpre, dx2, x1, ya, yb, proj, col0, wpg, wo, ple_g, n_cols, tm=256):
    t, d = ya.shape
    tm = min(tm, t)
    blk = _col_block(col0, 2 * d)

    def body(dpre_ref, dx2_ref, x1_ref, ya_ref, yb_ref, g_ref, wpg_ref, wo_ref, pg_ref,
             dx1_ref, dya_ref, dyb_ref, dg_ref, dpg_ref):
        @pl.when(pl.program_id(0) == 0)
        def _():
            dpg_ref[...] = jnp.zeros_like(dpg_ref)

        dhn = _bdot_nt(dpre_ref[...], wpg_ref[...])
        dx, dpg = _rms_grad(x1_ref[...], dhn, pg_ref[...])
        dpg_ref[...] += dpg
        dx1 = dx2_ref[...] + dx
        dx1_ref[...] = dx1
        dm_v = _bdot_nt(dx1, wo_ref[...])
        sa = _sigmoid(g_ref[:, :d])
        sb = _sigmoid(g_ref[:, d:])
        dya_ref[...] = (dm_v * sa).astype(BF16)
        dyb_ref[...] = (dm_v * sb).astype(BF16)
        dg_ref[:, :d] = (dm_v * ya_ref[...] * sa * (1.0 - sa)).astype(BF16)
        dg_ref[:, d:] = (dm_v * yb_ref[...] * sb * (1.0 - sb)).astype(BF16)

    row = pl.BlockSpec((tm, d), lambda i: (i, 0))
    vec = pl.BlockSpec((1, d), lambda i: (0, 0))
    gspec = pl.BlockSpec((tm, 2 * d), lambda i: (i, blk))
    whole = lambda a: pl.BlockSpec(a.shape, lambda i: (0,) * a.ndim)
    return pl.pallas_call(
        body, name="mid_bwd", grid=(t // tm,),
        in_specs=[row, row, row, row, row, gspec, whole(wpg), whole(wo), vec],
        out_specs=[row, row, row, gspec, vec],
        out_shape=[jax.ShapeDtypeStruct((t, d), F32), jax.ShapeDtypeStruct((t, d), BF16),
                   jax.ShapeDtypeStruct((t, d), BF16), jax.ShapeDtypeStruct((t, n_cols), BF16),
                   jax.ShapeDtypeStruct((1, d), F32)],
        compiler_params=_params(("arbitrary",)))(dpre, dx2, x1, ya, yb, proj, wpg, wo, ple_g)


def _in_bwd(dproj, w_main, ddt, w_dt, x2, dx1, norm_g, after, tm=1024, tk=1024):
    t, k = dproj.shape
    d = x2.shape[1]
    tm, tk = min(tm, t), min(tk, k)
    nk = k // tk

    def body(a_ref, w_ref, ddt_ref, wdt_ref, x_ref, dres_ref, g_ref, _, gx_ref, dg_ref, acc_ref):
        kk = pl.program_id(1)

        @pl.when((pl.program_id(0) == 0) & (kk == 0))
        def _():
            dg_ref[...] = jnp.zeros_like(dg_ref)

        part = _bdot_nt(a_ref[...], w_ref[...])

        @pl.when(kk == 0)
        def _():
            acc_ref[...] = part

        @pl.when(kk > 0)
        def _():
            acc_ref[...] += part

        @pl.when(kk == nk - 1)
        def _():
            dh = acc_ref[...] + _bdot_nt(ddt_ref[...], wdt_ref[...])
            dx, dg = _rms_grad(x_ref[...], dh, g_ref[...])
            dg_ref[...] += dg
            gx_ref[...] = dres_ref[...] + dx

    row = pl.BlockSpec((tm, d), lambda i, j: (i, 0))
    vec = pl.BlockSpec((1, d), lambda i, j: (0, 0))
    return pl.pallas_call(
        body, name="in_bwd", grid=(t // tm, nk),
        in_specs=[pl.BlockSpec((tm, tk), lambda i, j: (i, j)), pl.BlockSpec((d, tk), lambda i, j: (0, j)),
                  pl.BlockSpec((tm, LANES), lambda i, j: (i, 0)), pl.BlockSpec((d, LANES), lambda i, j: (0, 0)),
                  row, row, vec, pl.BlockSpec((8, LANES), lambda i, j: (0, 0))],
        out_specs=[row, vec],
        out_shape=[jax.ShapeDtypeStruct((t, d), F32), jax.ShapeDtypeStruct((1, d), F32)],
        scratch_shapes=[pltpu.VMEM((tm, d), F32)],
        compiler_params=_params(("arbitrary", "arbitrary")))(dproj, w_main, ddt, w_dt, x2, dx1, norm_g, after)


def _pool_bwd(proj, dyb, dproj, mix_w, mix_b, scale, nb, s, col0):
    pgd = mix_w.shape[-1]
    blk0 = _col_block(col0, 2 * pgd)
    rc = min(POOL_ROWS, s)

    def body(uz_ref, dy_ref, _, w_ref, b_ref, sc_ref, duz_ref, dw_ref, db_ref, dsc_ref, dpn_ref, dwacc_ref):
        g = pl.program_id(0)

        @pl.when(pl.program_id(1) == 0)
        def _():
            dw_ref[...] = jnp.zeros_like(dw_ref)
            db_ref[...] = jnp.zeros_like(db_ref)
            dsc_ref[...] = jnp.zeros_like(dsc_ref)

        wv = w_ref[:, 0].reshape(pgd, pgd)
        sc = sc_ref[...]
        fold = lambda v: v.reshape(rc // 8, 8, pgd).sum(axis=0)
        db8 = jnp.zeros((8, pgd), F32)
        dsc8 = jnp.zeros((8, pgd), F32)
        dwacc_ref[...] = jnp.zeros_like(dwacc_ref)
        for t0 in range(0, s, rc):
            u, win, cnt = _pool_trailing(uz_ref, pgd, t0, rc, g)
            pooled = win / cnt - u
            zp = uz_ref[t0:t0 + rc, pgd:]
            mixed = _bdot(pooled, wv) + b_ref[...]
            sg = _sigmoid(zp)
            sz = zp * sg
            dy = dy_ref[t0:t0 + rc, :]
            dsc8 = dsc8 + fold(dy * mixed * sz)
            dmixed = dy * sc * sz
            db8 = db8 + fold(dmixed)
            dwacc_ref[...] += _bdot_tn(pooled, dmixed)
            dpn_ref[t0:t0 + rc, :] = _bdot_nt(dmixed, wv) / cnt
            duz_ref[t0:t0 + rc, pgd:] = (dy * mixed * sc * sg * (1.0 + zp * (1.0 - sg))).astype(BF16)
        dsc_ref[...] += jnp.sum(dsc8, axis=0, keepdims=True)
        db_ref[...] += jnp.sum(db8, axis=0, keepdims=True)
        dw_ref[:, 0] += dwacc_ref[...].reshape(N_SHARD, pgd // N_SHARD, pgd)
        for t0 in range(0, s, rc):
            if t0 + rc < s:
                n = rc + POOL_HALO
                win = dpn_ref[t0:t0 + n, :]
                cur, lead = win[:rc], _roll_sum(win, g, n - 1)[:rc]
            else:
                cur = dpn_ref[t0:t0 + rc, :]
                row = lax.broadcasted_iota(jnp.int32, cur.shape, 0)
                lead = _pool_sum(cur, g, row, _shift_up)
            if t0 == 0:
                cnt = _pool_count(g, lax.broadcasted_iota(jnp.int32, cur.shape, 0))
            else:
                cnt = jnp.left_shift(2, g).astype(F32)
            duz_ref[t0:t0 + rc, :pgd] = (lead - cur * cnt).astype(BF16)

    uz = pl.BlockSpec((s, 2 * pgd), lambda g, b: (b, blk0 + g))
    vec = pl.BlockSpec((1, pgd), lambda g, b: (0, g))
    wspec = pl.BlockSpec((N_SHARD, 1, pgd // N_SHARD, pgd), lambda g, b: (0, g, 0, 0))
    return pl.pallas_call(
        body, name="pool_bwd", grid=(N_POOL, nb),
        in_specs=[uz, pl.BlockSpec((s, pgd), lambda g, b: (b, g)), pl.BlockSpec(memory_space=pl.ANY), wspec, vec, vec],
        out_specs=[uz, wspec, vec, vec],
        out_shape=[jax.ShapeDtypeStruct(dproj.shape, dproj.dtype), jax.ShapeDtypeStruct(mix_w.shape, F32),
                   jax.ShapeDtypeStruct(mix_b.shape, F32), jax.ShapeDtypeStruct(scale.shape, F32)],
        scratch_shapes=[pltpu.VMEM((s, pgd), F32), pltpu.VMEM((pgd, pgd), F32)],
        input_output_aliases={2: 0},
        compiler_params=_params(("parallel", "arbitrary")))(proj, dyb, dproj, mix_w, mix_b, scale)


def _ssd_bwd(dyan, ypre, proj, xbc, dtraw, hp, dproj, dtb, alog, dskx, gn, nb, s, di, z_col0):
    t = nb * s
    nc = s // CHUNK
    hpg = di // HEAD_DIM // SSM_GROUPS
    gw = di // SSM_GROUPS
    gn_w = SSM_GROUPS * D_STATE
    dc = di + 2 * gn_w
    b_blk = _col_block(di, gn_w)
    z_blk = _col_block(z_col0, di)
    L, P, N = CHUNK, HEAD_DIM, D_STATE
    tril, expand, expand_t = _ssd_consts(di)

    def body(dy_ref, ypre_ref, z_ref, x_ref, b_ref, c_ref, dtr_ref, hp_ref, _, dtb_ref, alog_ref, dskx_ref, gn_ref,
             tril_ref, e_ref, et_ref, dz_ref, ddt_ref, dxbc_ref, dgn_ref, ddsk_ref, dalog_ref, ddtb_ref,
             dst_ref, dyp_ref, xdt_ref, dxm_ref, t1_ref, t3_ref, aux_ref):
        @pl.when((pl.program_id(0) == 0) & (pl.program_id(1) == 0))
        def _():
            dgn_ref[...] = jnp.zeros_like(dgn_ref)
            ddsk_ref[...] = jnp.zeros_like(ddsk_ref)
            dalog_ref[...] = jnp.zeros_like(dalog_ref)
            ddtb_ref[...] = jnp.zeros_like(ddtb_ref)

        @pl.when(pl.program_id(1) == 0)
        def _():
            dst_ref[...] = jnp.zeros_like(dst_ref)

        tri = tril_ref[...]
        dtpre, dt, a_neg, a_cs, a_cst = _ssd_scalars(dtr_ref, dtb_ref, alog_ref, tri)
        ev = e_ref[...]
        a_exp = _expand(a_cs, ev)
        dt_exp = _expand(dt, ev, terms=2)
        xv = x_ref[...]
        xdt = xv * dt_exp
        xdt_ref[...] = xdt
        a_last = a_exp[L - 1:L, :]
        dte = jnp.exp(a_last - a_exp)
        xe = xdt * dte
        ea = jnp.exp(a_exp)
        e_last = jnp.exp(a_last)
        lower = tri > 0.5
        aux_ref[...] = jnp.zeros_like(aux_ref)
        for g in range(SSM_GROUPS):
            gs = slice(g * gw, (g + 1) * gw)
            zv = z_ref[:, gs]
            yp = ypre_ref[:, gs]
            sg = _sigmoid(zv)
            sz = zv * sg
            vg = yp * sz
            r = lax.rsqrt(jnp.mean(vg * vg, axis=-1, keepdims=True) + EPS)
            vn = vg * r
            dyg = dy_ref[:, gs]
            dgn_ref[:, gs] += jnp.sum(dyg * vn, axis=0, keepdims=True)
            dvn = dyg * gn_ref[:, gs]
            dv = r * (dvn - vn * jnp.mean(dvn * vn, axis=-1, keepdims=True))
            dy = dv * sz
            dyp_ref[:, gs] = dy
            dz_ref[:, gs] = (dv * yp * sg * (1.0 + zv * (1.0 - sg))).astype(BF16)
            bg = b_ref[:, g * N:(g + 1) * N].astype(BF16)
            cg = c_ref[:, g * N:(g + 1) * N].astype(BF16)
            gm = _bdot_nt(cg, bg)
            ht = hp_ref[0, :, gs]
            dht = dst_ref[:, gs]
            bds = _bdot(bg, dht)
            dye = dy * ea[:, gs]
            xe_g = xe[:, gs]
            dcg = _bdot_nt(dye, ht)
            dbg = _bdot_nt(xe_g, dht)
            dst_ref[:, gs] = e_last[:, gs] * dht + _bdot_tn(cg, dye)
            dgm = jnp.zeros((L, L), F32)
            for e in range(hpg):
                h = g * hpg + e
                hs = slice(h * P, (h + 1) * P)
                decay = jnp.where(lower, jnp.exp(a_cs[:, h:h + 1] - a_cst[h:h + 1, :]), 0.0)
                dy_h = dyp_ref[:, hs]
                dgm = dgm + _bdot_nt(dy_h, xdt_ref[:, hs]) * decay
                dxm_ref[:, hs] = _bdot_tn(gm * decay, dy_h)
            dxbc_ref[:, di + g * N:di + (g + 1) * N] = dbg + _bdot_tn(dgm, cg)
            dxbc_ref[:, di + gn_w + g * N:di + gn_w + (g + 1) * N] = dcg + _bdot(dgm, bg)
            dxm = dxm_ref[:, gs]
            x_g = xv[:, gs]
            dskx = dskx_ref[:, gs]
            xeb = xe_g * bds
            dxdt = dxm + dte[:, gs] * bds
            dxbc_ref[:, gs] = dxdt * dt_exp[:, gs] + dy * dskx
            each = ea[:, gs] * _bdot(cg, ht)
            y_diag = yp - x_g * dskx - each
            rnd = lambda v: v.astype(BF16).astype(F32)
            t1_ref[:, gs] = rnd(dy) * y_diag + dy * each - rnd(xdt[:, gs]) * dxm - xeb
            t3_ref[:, gs] = dxdt * x_g
            aux_ref[0:1, gs] = jnp.sum(dht * ht, axis=0, keepdims=True)
            aux_ref[1:2, gs] = jnp.sum(dy * x_g, axis=0, keepdims=True)
            aux_ref[2:3, gs] = jnp.sum(xeb, axis=0, keepdims=True)
        etv = et_ref[...]
        aux = _head_sum(aux_ref[...], etv)
        rowi = lax.broadcasted_iota(jnp.int32, (L, LANES), 0)
        end = aux[2:3, :] + aux[0:1, :] * jnp.exp(a_cs[L - 1:L, :])
        da = _head_sum(t1_ref[...], etv, terms=3) + jnp.where(rowi == L - 1, end, 0.0)
        rc = lax.dot_general(tri, da, (((0,), (0,)), ((), ())), precision=HIGHEST, preferred_element_type=F32)
        ddt = a_neg * rc + _head_sum(t3_ref[...], etv, terms=1)
        ddtraw = ddt * _sigmoid_tail(dtpre)
        ddt_ref[...] = ddtraw.astype(BF16)
        ddtb_ref[...] += jnp.sum(ddtraw, axis=0, keepdims=True)
        dalog_ref[...] += jnp.sum(dt * rc, axis=0, keepdims=True) * a_neg
        ddsk_ref[...] += aux[1:2, :]

    row = lambda b, c: b * nc + (nc - 1 - c)
    full = lambda w: pl.BlockSpec((L, w), lambda b, c: (row(b, c), 0))
    zspec = pl.BlockSpec((L, di), lambda b, c: (row(b, c), z_blk))
    vec = lambda w: pl.BlockSpec((1, w), lambda b, c: (0, 0))
    slab = lambda shape: pltpu.VMEM(shape, F32)
    return pl.pallas_call(
        body, name="ssd_bwd", grid=(nb, nc),
        in_specs=[full(di), full(di), zspec, full(di),
                  pl.BlockSpec((L, gn_w), lambda b, c: (row(b, c), b_blk)),
                  pl.BlockSpec((L, gn_w), lambda b, c: (row(b, c), b_blk + 1)),
                  full(LANES),
                  pl.BlockSpec((1, N, di), lambda b, c: (row(b, c), 0, 0)),
                  pl.BlockSpec(memory_space=pl.ANY),
                  vec(LANES), vec(LANES), vec(di), vec(di),
                  pl.BlockSpec((L, L), lambda b, c: (0, 0)),
                  pl.BlockSpec((LANES, di), lambda b, c: (0, 0)),
                  pl.BlockSpec((di, LANES), lambda b, c: (0, 0))],
        out_specs=[zspec, full(LANES), full(dc), vec(di), vec(LANES), vec(LANES), vec(LANES)],
        out_shape=[jax.ShapeDtypeStruct(dproj.shape, dproj.dtype), jax.ShapeDtypeStruct((t, LANES), BF16),
                   jax.ShapeDtypeStruct((t, dc), F32), jax.ShapeDtypeStruct((1, di), F32),
                   jax.ShapeDtypeStruct((1, LANES), F32), jax.ShapeDtypeStruct((1, LANES), F32),
                   jax.ShapeDtypeStruct((1, LANES), F32)],
        scratch_shapes=[slab((N, di)), slab((L, di)), slab((L, di)), slab((L, di)), slab((L, di)), slab((L, di)),
                        slab((8, di))],
        input_output_aliases={8: 0},
        compiler_params=_params(("arbitrary", "arbitrary")))(
            dyan, ypre, proj, xbc, xbc, xbc, dtraw, hp, dproj, dtb, alog, dskx, gn, tril, expand, expand_t)


def _conv_bwd(proj, dxbc, dproj, conv_w, conv_b, nb, s, col0, cb=512):
    n_blk, w_spec = _conv_w_spec(conv_w, cb, 0)
    blk0 = _col_block(col0, cb)
    rc = min(CONV_ROWS, s)

    def body(x_ref, dy_ref, _, w_ref, b_ref, dx_ref, dw_ref, db_ref, dacc_ref):
        @pl.when(pl.program_id(1) == 0)
        def _():
            dw_ref[...] = jnp.zeros_like(dw_ref)
            db_ref[...] = jnp.zeros_like(db_ref)

        wts, bias = _conv_weights(w_ref), b_ref[...]
        fold = lambda v: v.reshape(rc // 8, 8, cb).sum(axis=0)
        db8 = jnp.zeros((8, cb), F32)
        dw8 = [jnp.zeros((8, cb), F32) for _ in range(CONV_WIDTH)]
        for t0 in range(0, s, rc):
            taps = _conv_taps(x_ref, t0, rc)
            acc = _conv_pre(taps, wts, bias)
            sg = _sigmoid(acc)
            dacc = dy_ref[t0:t0 + rc, :] * sg * (1.0 + acc * (1.0 - sg))
            dacc_ref[t0:t0 + rc, :] = dacc
            db8 = db8 + fold(dacc)
            dw8 = [dw8[j] + fold(dacc * taps[j]) for j in range(CONV_WIDTH)]
        db_ref[...] += jnp.sum(db8, axis=0, keepdims=True)
        for j in range(CONV_WIDTH):
            dw_ref[CONV_WIDTH - 1 - j:CONV_WIDTH - j, :] += jnp.sum(dw8[j], axis=0, keepdims=True)
        for t0 in range(0, s, rc):
            if t0 + rc < s:
                n = rc + CONV_HALO
                win = dacc_ref[t0:t0 + n, :]
                ups = [win[:rc]] + [pltpu.roll(win, n - j, 0)[:rc] for j in range(1, CONV_WIDTH)]
            else:
                cur = dacc_ref[t0:t0 + rc, :]
                row = lax.broadcasted_iota(jnp.int32, cur.shape, 0)
                ups = [cur] + [_shift_up(cur, j, row) for j in range(1, CONV_WIDTH)]
            dx = ups[0] * wts[0]
            for j in range(1, CONV_WIDTH):
                dx = dx + ups[j] * wts[j]
            dx_ref[t0:t0 + rc, :] = dx.astype(BF16)

    return pl.pallas_call(
        body, name="conv_bwd", grid=(n_blk, nb),
        in_specs=[pl.BlockSpec((s, cb), lambda j, b: (b, blk0 + j)), pl.BlockSpec((s, cb), lambda j, b: (b, j)),
                  pl.BlockSpec(memory_space=pl.ANY), w_spec, pl.BlockSpec((1, cb), lambda j, b: (0, j))],
        out_specs=[pl.BlockSpec((s, cb), lambda j, b: (b, blk0 + j)), w_spec, pl.BlockSpec((1, cb), lambda j, b: (0, j))],
        out_shape=[jax.ShapeDtypeStruct(dproj.shape, dproj.dtype), jax.ShapeDtypeStruct(conv_w.shape, F32),
                   jax.ShapeDtypeStruct(conv_b.shape, F32)],
        scratch_shapes=[pltpu.VMEM((s, cb), F32)],
        input_output_aliases={2: 0},
        compiler_params=_params(("parallel", "arbitrary")))(proj, dxbc, dproj, conv_w, conv_b)


def _local_step(x, p, tgt, wg, small, rest_weights, early_grads, w_in_grad):
    nb, s, d = x.shape
    t = nb * s
    gn_w = SSM_GROUPS * D_STATE
    dc = N_SHARD * wg["conv_w"].shape[2]
    di = dc - 2 * gn_w
    nh = di // HEAD_DIM
    pgd = d // N_POOL
    x2 = x.reshape(t, d)
    p2 = p.reshape(t, p.shape[-1])
    tgt2 = tgt.reshape(t, d)

    w_main, w_dt = _regroup_w_in(wg["w_in"], d, di, dc, nh)
    c_g, c_z, c_xbc, c_uz = 0, 2 * d, 2 * d + di, 2 * d + di + dc
    n_main = w_main.shape[1]

    pad_h = lambda v: jnp.pad(v.reshape(1, nh).astype(F32), ((0, 0), (0, LANES - nh)))
    dtb, alog = pad_h(small["dt_bias"]), pad_h(small["a_log"])
    dskx = jnp.repeat(small["d_skip"].reshape(1, nh).astype(F32), HEAD_DIM, axis=1)
    vec = lambda v: v.reshape(1, -1).astype(F32)
    norm_g, gn, conv_b = vec(small["norm_g"]), vec(small["gnorm_g"]), vec(small["conv_b"])
    mix_b, scale = vec(small["pool_mix_b"]), vec(small["pool_scale"])
    ple_g, final_g = vec(small["ple_norm_g"]), vec(small["final_g"])
    conv_w = jnp.swapaxes(wg["conv_w"], 0, 1).reshape(CONV_WIDTH, dc)

    wide = _tile(n_main, 2304, LANES)
    proj, dtraw, h = _inproj(x2, norm_g, w_main, w_dt, tn=wide)
    xbc = _conv_fwd(proj, conv_w, conv_b, nb, s, c_xbc)
    ypre, yan, hp = _ssd_fwd(proj, xbc, dtraw, dtb, alog, dskx, gn, nb, s, di, c_z)
    wr = rest_weights(yan)
    mix_w = wr["pool_mix_w"].reshape(N_SHARD, N_POOL, pgd // N_SHARD, pgd)
    rows = lambda v: v.reshape(-1, v.shape[-1])
    wa, wb, wo, wpg = rows(wr["w_branch_a"]), rows(wr["w_branch_b"]), rows(wr["w_out"]), rows(wr["w_ple_gate"])
    wup = wr["w_ple_up"]
    ybp = _pool_fwd(proj, mix_w, mix_b, scale, nb, s, c_uz)
    ya = _mm(yan, wa, "branch_a")
    yb = _mm(ybp, wb, "branch_b")
    merged, hn, dpre, dpu, x1, dx2, loss, d_final_g = _mid_fwd(
        ya, yb, proj, c_g, x2, p2, tgt2, wo, wpg, wup, ple_g, final_g)

    d_wpg = _mm_tn(hn, dpre, "d_w_ple_gate")
    d_wup = _mm_tn(p2, dpu, "d_w_ple_up", tn=wup.shape[-1], col_blocks=True)
    dx1, dya, dyb, dproj, d_ple_g = _mid_bwd(dpre, dx2, x1, ya, yb, proj, c_g, wpg, wo, ple_g, n_main)
    d_wo = _mm_tn(merged, dx1, "d_w_out")
    d_wa = _mm_tn(yan, dya, "d_w_branch_a", tk=1024)
    d_wb = _mm_tn(ybp, dyb, "d_w_branch_b")
    dyan = _mm_nt(dya, wa, "d_y_a")
    dybp = _mm_nt(dyb, wb, "d_y_b")
    dproj, d_mix_w, d_mix_b, d_scale = _pool_bwd(proj, dybp, dproj, mix_w, mix_b, scale, nb, s, c_uz)
    shard_major = lambda v: v.reshape(N_SHARD, v.shape[0] // N_SHARD, v.shape[1])
    early = dict(pool_mix_w=d_mix_w.reshape(N_SHARD, pgd, pgd), w_branch_a=shard_major(d_wa),
                 w_branch_b=shard_major(d_wb), w_out=shard_major(d_wo), w_ple_gate=shard_major(d_wpg),
                 w_ple_up=d_wup)
    token = early_grads(early)
    dproj, ddt, dxbc, d_gn, d_dsk, d_alog, d_dtb = _ssd_bwd(
        dyan, ypre, proj, xbc, dtraw, hp, dproj, dtb + token[0:1, 0:1], alog, dskx, gn, nb, s, di, c_z)
    dproj, d_conv_w, d_conv_b = _conv_bwd(proj, dxbc, dproj, conv_w, conv_b, nb, s, c_xbc)
    d_wmain = _mm_tn(h, dproj, "d_w_in", tk=2048)
    d_wdt = _mm_tn(h, ddt, "d_w_dt")
    d_w_in, d_w_in_bf16 = _ungroup_w_in(d_wmain, d_wdt, d, di, dc, nh)
    token = w_in_grad(d_w_in, d_w_in_bf16)
    gx, d_norm_g = _in_bwd(dproj, w_main, ddt, w_dt, x2, dx1, norm_g, token, tk=_tile(n_main, 1536, LANES))

    d_conv_w = jnp.swapaxes(d_conv_w.reshape(CONV_WIDTH, N_SHARD, dc // N_SHARD), 0, 1)
    grads = dict(norm_g=d_norm_g, w_in=d_w_in, conv_w=d_conv_w, conv_b=d_conv_b, dt_bias=d_dtb[:, :nh],
                 a_log=d_alog[:, :nh], d_skip=d_dsk[:, :nh], gnorm_g=d_gn, pool_mix_b=d_mix_b, pool_scale=d_scale,
                 ple_norm_g=d_ple_g, final_g=d_final_g, **early)
    return loss[0, 0], gx.reshape(nb, s, d), grads


def _place():
    return lax.axis_index("x"), lax.axis_index("y"), lax.axis_index("c")


def _other_chips(x, y):
    return [(1 - x, y), (x, 1 - y), (1 - x, 1 - y)]


def _halves(c, rows, align):
    rh = rows // 2
    assert rows % 2 == 0 and rh % align == 0, (rows, align)
    return (pl.ds(pl.multiple_of(c * rh, align), rh), pl.ds(pl.multiple_of((1 - c) * rh, align), rh))


HBM = pl.BlockSpec(memory_space=pl.ANY)


def _into_slot(w2, k, dtype, name):
    rows, cols = w2.shape
    rb = _tile(rows, 256)

    def body(k_ref, w_ref, o_ref):
        o_ref[0] = w_ref[...].astype(dtype)

    return pl.pallas_call(
        body, name=name,
        grid_spec=pltpu.PrefetchScalarGridSpec(
            num_scalar_prefetch=1, grid=(rows // rb,),
            in_specs=[pl.BlockSpec((rb, cols), lambda i, k_ref: (i, 0))],
            out_specs=pl.BlockSpec((1, rb, cols), lambda i, k_ref: (k_ref[0], i, 0))),
        out_shape=jax.ShapeDtypeStruct((N_SHARD, rows, cols), dtype),
        compiler_params=_params(("parallel",)))(k.reshape(1), w2)


def _gather_weights(split, whole):
    n_split, n_all = len(split), len(split) + len(whole)

    def body(*refs):
        bufs = refs[n_all:2 * n_all]
        send_sems, recv_sems = refs[2 * n_all:]
        x, y, c = _place()
        k, k_x, k_y, k_d = 2 * x + y, 2 * (1 - x) + y, 2 * x + (1 - y), 2 * (1 - x) + (1 - y)
        x_nb, y_nb, sib = (1 - x, y, c), (x, 1 - y, c), (x, y, 1 - c)

        def copy(idx, block, to):
            return pltpu.make_async_remote_copy(src_ref=block, dst_ref=block, send_sem=send_sems.at[idx],
                                                recv_sem=recv_sems.at[idx], device_id=to, device_id_type=MESH)

        started = []

        def start(idx, block, to):
            started.append(copy(idx, block, to))
            started[-1].start()

        for i in range(n_split):
            buf, s0 = bufs[i], 8 * i
            rh = buf.shape[1] // 2
            rq = rh // 2
            assert buf.shape[1] == 4 * rq and rq % 16 == 0, buf.shape

            def rows(core, part):
                lo = core * rh + (rq if part == "bottom" else 0)
                return pl.ds(pl.multiple_of(lo, 16), rh if part == "all" else rq)

            start(s0 + 0, buf.at[k, rows(c, "all")], x_nb)
            start(s0 + 1, buf.at[k, rows(c, "all")], y_nb)
            copy(s0 + 0, buf.at[k_x, rows(c, "all")], x_nb).wait_recv()
            start(s0 + 2, buf.at[k_x, rows(c, "top")], y_nb)
            start(s0 + 4, buf.at[k_x, rows(c, "all")], sib)
            copy(s0 + 1, buf.at[k_y, rows(c, "all")], y_nb).wait_recv()
            start(s0 + 3, buf.at[k_y, rows(c, "bottom")], x_nb)
            start(s0 + 5, buf.at[k_y, rows(c, "all")], sib)
            copy(s0 + 2, buf.at[k_d, rows(c, "top")], y_nb).wait_recv()
            start(s0 + 6, buf.at[k_d, rows(c, "top")], sib)
            copy(s0 + 3, buf.at[k_d, rows(c, "bottom")], x_nb).wait_recv()
            start(s0 + 7, buf.at[k_d, rows(c, "bottom")], sib)
            copy(s0 + 4, buf.at[k_x, rows(1 - c, "all")], sib).wait_recv()
            copy(s0 + 5, buf.at[k_y, rows(1 - c, "all")], sib).wait_recv()
            copy(s0 + 6, buf.at[k_d, rows(1 - c, "top")], sib).wait_recv()
            copy(s0 + 7, buf.at[k_d, rows(1 - c, "bottom")], sib).wait_recv()
        for i in range(n_split, n_all):
            s0 = 8 * n_split + 3 * (i - n_split)
            for j, (px, py) in enumerate(_other_chips(x, y)):
                start(s0 + j, bufs[i].at[k], (px, py, c))
            for j, (px, py) in enumerate(_other_chips(x, y)):
                copy(s0 + j, bufs[i].at[2 * px + py], (px, py, c)).wait_recv()
        for cp in started:
            cp.wait_send()

    arrays = list(split) + list(whole)
    n_sem = 8 * n_split + 3 * len(whole)
    return pl.pallas_call(
        body, name="gather_weights",
        in_specs=[HBM] * n_all, out_specs=[HBM] * n_all,
        out_shape=[jax.ShapeDtypeStruct(a.shape, a.dtype) for a in arrays],
        input_output_aliases={i: i for i in range(n_all)},
        scratch_shapes=[pltpu.SemaphoreType.DMA((n_sem,)), pltpu.SemaphoreType.DMA((n_sem,))],
    )(*arrays)


def _swap_halves(gs):
    n = len(gs)

    def body(*refs):
        ins, outs, send_sems, recv_sems = refs[:n], refs[n:2 * n], refs[2 * n], refs[2 * n + 1]
        x, y, c = _place()
        copies = []
        for i in range(n):
            _, theirs = _halves(c, gs[i].shape[1], 8)
            cp = pltpu.make_async_remote_copy(src_ref=ins[i].at[:, theirs], dst_ref=outs[i], send_sem=send_sems.at[i],
                                              recv_sem=recv_sems.at[i], device_id=(x, y, 1 - c), device_id_type=MESH)
            cp.start()
            copies.append(cp)
        for cp in copies:
            cp.wait()

    return pl.pallas_call(
        body, name="swap_halves", in_specs=[HBM] * n, out_specs=[HBM] * n,
        out_shape=[jax.ShapeDtypeStruct((g.shape[0], g.shape[1] // 2, g.shape[2]), g.dtype) for g in gs],
        scratch_shapes=[pltpu.SemaphoreType.DMA((n,)), pltpu.SemaphoreType.DMA((n,))],
    )(*gs)


SEM = pl.BlockSpec(memory_space=pltpu.SEMAPHORE)
IN_HBM = pl.BlockSpec(memory_space=pltpu.HBM)
SPLIT_EFFECT = pltpu.SideEffectType.DATAFLOW_SIDE_EFFECTING


def _split_copies(plan, refs, send_sems, recv_sems):
    pairs = []
    for idx, (src, dst, landing, to) in enumerate(plan(refs)):
        mk = lambda d: pltpu.make_async_remote_copy(src_ref=src, dst_ref=d, send_sem=send_sems.at[idx],
                                                    recv_sem=recv_sems.at[idx], device_id=to, device_id_type=MESH)
        pairs.append((mk(dst), mk(landing)))
    return pairs


def _split_start(name, bufs, after, plan, n_copies):
    n = len(bufs)

    def body(*refs):
        send_sems, recv_sems, token = refs[n + 1], refs[n + 2], refs[-1]
        for send, _ in _split_copies(plan, refs[:n], send_sems, recv_sems):
            send.start()
        token[...] = jnp.zeros_like(token)

    sems = pltpu.SemaphoreType.DMA((n_copies,))
    out = pl.pallas_call(
        body, name=name,
        in_specs=[IN_HBM] * n + [HBM],
        out_specs=[SEM, SEM] + [IN_HBM] * n + [pl.BlockSpec(memory_space=pltpu.VMEM)],
        out_shape=[sems, sems] + [pltpu.HBM(b.shape, b.dtype) for b in bufs] + [jax.ShapeDtypeStruct((8, LANES), F32)],
        input_output_aliases={i: 2 + i for i in range(n)},
        compiler_params=pltpu.CompilerParams(has_side_effects=SPLIT_EFFECT),
    )(*[pltpu.with_memory_space_constraint(b, pltpu.HBM) for b in bufs], after)
    return out[0], out[1], out[2:2 + n], out[-1]


def _split_wait(name, bufs, send_sems, recv_sems, after, plan):
    n = len(bufs)

    def body(*refs):
        for send, recv in _split_copies(plan, refs[:n], refs[n], refs[n + 1]):
            send.wait_send()
            recv.wait_recv()

    return pl.pallas_call(
        body, name=name,
        in_specs=[IN_HBM] * n + [SEM, SEM, HBM],
        out_specs=[IN_HBM] * n,
        out_shape=[pltpu.HBM(b.shape, b.dtype) for b in bufs],
        input_output_aliases={i: i for i in range(n)},
        compiler_params=pltpu.CompilerParams(has_side_effects=SPLIT_EFFECT),
    )(*bufs, send_sems, recv_sems, after)


def _gather_plan(n):
    def plan(refs):
        x, y, c = _place()
        k = 2 * x + y
        return [(refs[i].at[k], refs[i].at[k], refs[i].at[2 * px + py], (px, py, c))
                for i in range(n) for px, py in _other_chips(x, y)]
    return plan


def _scatter_plan(n):
    def plan(refs):
        x, y, c = _place()
        return [(refs[i].at[2 * px + py], refs[n + i].at[j], refs[n + i].at[j], (px, py, c))
                for i in range(n) for j, (px, py) in enumerate(_other_chips(x, y))]
    return plan


IN_VMEM = pl.BlockSpec(memory_space=pltpu.VMEM)


def _to_bf16_all(gs):
    n = len(gs)

    def body(*refs):
        for i in range(n):
            refs[n + i][...] = refs[i][...].astype(BF16)

    return pl.pallas_call(
        body, name="bf16_rest", in_specs=[IN_VMEM] * n, out_specs=[IN_VMEM] * n,
        out_shape=[jax.ShapeDtypeStruct(g.shape, BF16) for g in gs],
        compiler_params=_params())(*gs)


def _add_landed_all(gs, landed, k):
    n = len(gs)

    def body(k_ref, *refs):
        for i in range(n):
            g_ref, l_ref, o_ref = refs[i], refs[n + i], refs[2 * n + i]
            o_ref[...] = g_ref[k_ref[0]] + l_ref[0].astype(F32) + l_ref[1].astype(F32) + l_ref[2].astype(F32)

    return pl.pallas_call(
        body, name="add_landed_rest",
        in_specs=[pl.BlockSpec(memory_space=pltpu.SMEM)] + [IN_VMEM] * (2 * n), out_specs=[IN_VMEM] * n,
        out_shape=[jax.ShapeDtypeStruct(g.shape[1:], F32) for g in gs],
        compiler_params=_params())(k.reshape(1), *gs, *landed)


def _final_exchange(halves, wholes, small):
    nh, nw = len(halves), len(wholes)
    n_in = nh + nw + 1
    rows = small.shape[0]

    def body(*refs):
        w_in, v_ref = refs[nh:nh + nw], refs[nh + nw]
        h_out, w_out, o_ref = refs[n_in:n_in + nh], refs[n_in + nh:n_in + nh + nw], refs[n_in + nh + nw]
        buf_ref, send_sems, recv_sems = refs[2 * n_in:]
        x, y, c = _place()
        sib = (x, y, 1 - c)
        me = 4 * x + 2 * y + c

        def copy(idx, src, dst, to):
            return pltpu.make_async_remote_copy(src_ref=src, dst_ref=dst, send_sem=send_sems.at[idx],
                                                recv_sem=recv_sems.at[idx], device_id=to, device_id_type=MESH)

        started = []
        split = [_halves(c, h_out[i].shape[0], 8) for i in range(nh)]
        for i in range(nh):
            started.append(copy(i, h_out[i].at[split[i][0]], h_out[i].at[split[i][0]], sib))
        for i in range(nw):
            started.append(copy(nh + i, w_in[i], w_out[i], sib))
        buf_ref[me] = v_ref[...]
        for rel in range(1, 8):
            peer = (x ^ (rel >> 2), y ^ ((rel >> 1) & 1), c ^ (rel & 1))
            started.append(copy(nh + nw + rel - 1, v_ref, buf_ref.at[me], peer))
        for cp in started:
            cp.start()
        for i in range(nh):
            copy(i, h_out[i].at[split[i][1]], h_out[i].at[split[i][1]], sib).wait_recv()
        for i in range(nw):
            copy(nh + i, w_in[i], w_out[i], sib).wait_recv()
        for rel in range(1, 8):
            copy(nh + nw + rel - 1, v_ref, buf_ref.at[me ^ rel], (x, y, c)).wait_recv()
        for cp in started:
            cp.wait_send()
        acc = buf_ref[0]
        for i in range(1, 8):
            acc = acc + buf_ref[i]
        o_ref[...] = acc

    n_sem = nh + nw + 7
    out = pl.pallas_call(
        body, name="final_exchange",
        in_specs=[HBM] * (nh + nw) + [IN_VMEM], out_specs=[HBM] * (nh + nw) + [IN_VMEM],
        out_shape=[jax.ShapeDtypeStruct(a.shape, a.dtype) for a in list(halves) + list(wholes)]
        + [jax.ShapeDtypeStruct(small.shape, F32)],
        input_output_aliases={i: i for i in range(nh)},
        scratch_shapes=[pltpu.VMEM((8, rows, LANES), F32), pltpu.SemaphoreType.DMA((n_sem,)),
                        pltpu.SemaphoreType.DMA((n_sem,))],
    )(*halves, *wholes, small)
    return out[:nh], out[nh:nh + nw], out[-1]


def _add_pair(g, got, c, name):
    _, rh, cols = got.shape
    rb = _tile(rh, 256)
    nrb = rh // rb

    def body(c_ref, g_ref, got_ref, o_ref):
        o_ref[...] = (g_ref[...] + got_ref[...]).astype(BF16)

    spec = pl.BlockSpec((1, rb, cols), lambda j, i, c_ref: (j, i, 0))
    return pl.pallas_call(
        body, name=name,
        grid_spec=pltpu.PrefetchScalarGridSpec(
            num_scalar_prefetch=1, grid=(N_SHARD, nrb),
            in_specs=[pl.BlockSpec((1, rb, cols), lambda j, i, c_ref: (j, c_ref[0] * nrb + i, 0)), spec],
            out_specs=spec),
        out_shape=jax.ShapeDtypeStruct(got.shape, BF16),
        compiler_params=_params(("parallel", "parallel")))(c.reshape(1), g, got)


def _add_chips(g, got, landed, k, c, name):
    _, rh, cols = got.shape
    rb = _tile(rh, 256)
    nrb = rh // rb

    def body(kc_ref, g_ref, got_ref, l_ref, o_ref):
        own = g_ref[0] + got_ref[0]
        o_ref[...] = own + l_ref[0].astype(F32) + l_ref[1].astype(F32) + l_ref[2].astype(F32)

    half_c = lambda i, kc: (kc[1] * nrb + i, 0)
    return pl.pallas_call(
        body, name=name,
        grid_spec=pltpu.PrefetchScalarGridSpec(
            num_scalar_prefetch=1, grid=(nrb,),
            in_specs=[pl.BlockSpec((1, rb, cols), lambda i, kc: (kc[0],) + half_c(i, kc)),
                      pl.BlockSpec((1, rb, cols), lambda i, kc: (kc[0], i, 0)),
                      pl.BlockSpec((N_SHARD - 1, rb, cols), lambda i, kc: (0, i, 0))],
            out_specs=pl.BlockSpec((rb, cols), half_c)),
        out_shape=jax.ShapeDtypeStruct((2 * rh, cols), F32),
        compiler_params=_params(("parallel",)))(jnp.stack([k, c]), g, got, landed)


def _adamw_update(w_ref, g_refs, m_ref, v_ref, go_ref, d_ref, nm_ref, nv_ref):
    gv = g_refs[0][...]
    for ref in g_refs[1:]:
        gv = gv + ref[...]
    go_ref[...] = gv
    nm = ADAM_B1 * m_ref[...] + (1.0 - ADAM_B1) * gv
    nv = ADAM_B2 * v_ref[...] + (1.0 - ADAM_B2) * (gv * gv)
    nm_ref[...] = nm
    nv_ref[...] = nv
    m_hat = nm / (1.0 - ADAM_B1 ** ADAM_STEP)
    v_hat = nv / (1.0 - ADAM_B2 ** ADAM_STEP)
    d_ref[...] = -ADAM_LR * (m_hat / (jnp.sqrt(v_hat) + ADAM_EPS) + ADAM_WD * w_ref[...])


def _adamw_all(ws, gs, ms, vs):
    n, n_g = len(ws), len(gs[0])
    flat_g = [part for parts in gs for part in parts]

    def body(*refs):
        w_refs, g_refs = refs[:n], refs[n:n + n * n_g]
        m_refs, v_refs, outs = refs[n + n * n_g:2 * n + n * n_g], refs[2 * n + n * n_g:3 * n + n * n_g], refs[3 * n + n * n_g:]
        for i in range(n):
            _adamw_update(w_refs[i], g_refs[i * n_g:(i + 1) * n_g], m_refs[i], v_refs[i], *outs[4 * i:4 * i + 4])

    out = pl.pallas_call(
        body, name="adamw_rest", in_specs=[IN_VMEM] * (3 * n + n * n_g), out_specs=[IN_VMEM] * (4 * n),
        out_shape=[jax.ShapeDtypeStruct(w.shape, F32) for w in ws for _ in range(4)],
        compiler_params=_params())(*ws, *flat_g, *ms, *vs)
    return [out[4 * i:4 * i + 4] for i in range(n)]


def _adamw(wv, gs, m, v, name):
    rows, cols = wv.shape
    rb = _tile(rows, 256)
    n_g = len(gs)

    def body(*refs):
        _adamw_update(refs[0], refs[1:1 + n_g], *refs[1 + n_g:])

    spec = pl.BlockSpec((rb, cols), lambda i: (i, 0))
    return pl.pallas_call(
        body, name=name, grid=(rows // rb,), in_specs=[spec] * (3 + n_g), out_specs=[spec] * 4,
        out_shape=[jax.ShapeDtypeStruct((rows, cols), F32)] * 4,
        compiler_params=_params(("parallel",)))(wv, *gs, m, v)


def _pack(flats):
    cat = jnp.concatenate([f.reshape(-1) for f in flats])
    n = cat.shape[0]
    rows = -(-n // (8 * LANES)) * 8
    return jnp.pad(cat, (0, rows * LANES - n)).reshape(rows, LANES)


def _unpack(packed, shapes):
    flat = packed.reshape(-1)
    out, off = [], 0
    for shp in shapes:
        n = 1
        for dim in shp:
            n *= dim
        out.append(flat[off:off + n].reshape(shp))
        off += n
    return out


def kernel(x, p, norm_g, w_in, conv_w, conv_b, dt_bias, a_log, d_skip, gnorm_g, pool_mix_w, pool_mix_b, pool_scale, w_branch_a, w_branch_b, w_out, ple_norm_g, w_ple_gate, w_ple_up, final_g, loss_target, m_norm_g, m_w_in, m_conv_w, m_conv_b, m_dt_bias, m_a_log, m_d_skip, m_gnorm_g, m_pool_mix_w, m_pool_mix_b, m_pool_scale, m_w_branch_a, m_w_branch_b, m_w_out, m_ple_norm_g, m_w_ple_gate, m_w_ple_up, m_final_g, v_norm_g, v_w_in, v_conv_w, v_conv_b, v_dt_bias, v_a_log, v_d_skip, v_gnorm_g, v_pool_mix_w, v_pool_mix_b, v_pool_scale, v_w_branch_a, v_w_branch_b, v_w_out, v_ple_norm_g, v_w_ple_gate, v_w_ple_up, v_final_g):
    wts = dict(norm_g=norm_g, w_in=w_in, conv_w=conv_w, conv_b=conv_b, dt_bias=dt_bias, a_log=a_log, d_skip=d_skip,
               gnorm_g=gnorm_g, pool_mix_w=pool_mix_w, pool_mix_b=pool_mix_b, pool_scale=pool_scale,
               w_branch_a=w_branch_a, w_branch_b=w_branch_b, w_out=w_out, ple_norm_g=ple_norm_g,
               w_ple_gate=w_ple_gate, w_ple_up=w_ple_up, final_g=final_g)
    mom_m = dict(norm_g=m_norm_g, w_in=m_w_in, conv_w=m_conv_w, conv_b=m_conv_b, dt_bias=m_dt_bias, a_log=m_a_log,
                 d_skip=m_d_skip, gnorm_g=m_gnorm_g, pool_mix_w=m_pool_mix_w, pool_mix_b=m_pool_mix_b,
                 pool_scale=m_pool_scale, w_branch_a=m_w_branch_a, w_branch_b=m_w_branch_b, w_out=m_w_out,
                 ple_norm_g=m_ple_norm_g, w_ple_gate=m_w_ple_gate, w_ple_up=m_w_ple_up, final_g=m_final_g)
    mom_v = dict(norm_g=v_norm_g, w_in=v_w_in, conv_w=v_conv_w, conv_b=v_conv_b, dt_bias=v_dt_bias, a_log=v_a_log,
                 d_skip=v_d_skip, gnorm_g=v_gnorm_g, pool_mix_w=v_pool_mix_w, pool_mix_b=v_pool_mix_b,
                 pool_scale=v_pool_scale, w_branch_a=v_w_branch_a, w_branch_b=v_w_branch_b, w_out=v_w_out,
                 ple_norm_g=v_ple_norm_g, w_ple_gate=v_w_ple_gate, w_ple_up=v_w_ple_up, final_g=v_final_g)
    c = lax.axis_index("c")
    k = 2 * lax.axis_index("x") + lax.axis_index("y")
    flat2 = lambda a: a.reshape(-1, a.shape[-1])

    slots = {n: _into_slot(flat2(wts[n]), k, BF16, "slot_" + n) for n in BIG}
    w_in_g, conv_g = _gather_weights([slots["w_in"]], [_into_slot(flat2(conv_w), k, F32, "slot_conv_w")])
    n_rest = len(REST)
    gsend, grecv, gbufs, gtoken = _split_start("gather_rest_start", [slots[n] for n in REST], conv_g,
                                               _gather_plan(n_rest), 3 * n_rest)

    def rest_weights(after):
        return dict(zip(REST, _split_wait("gather_rest_wait", gbufs, gsend, grecv, after, _gather_plan(n_rest))))

    flying = {}

    def early_grads(early):
        sends = list(_to_bf16_all([early[n] for n in REST]))
        lands = [pltpu.with_memory_space_constraint(lax.empty((N_SHARD - 1,) + v.shape[1:], BF16), pltpu.HBM)
                 for v in sends]
        ssend, srecv, sbufs, stoken = _split_start("scatter_rest_start", sends + lands, early[REST[0]],
                                                   _scatter_plan(n_rest), 3 * n_rest)
        flying.update(send=ssend, recv=srecv, bufs=sbufs)
        return stoken

    def w_in_grad(g_w_in, g_w_in_bf16):
        got = _swap_halves([g_w_in_bf16])[0]
        pair = _add_pair(g_w_in, got, c, "add_pair_w_in")
        land = pltpu.with_memory_space_constraint(lax.empty((N_SHARD - 1,) + pair.shape[1:], BF16), pltpu.HBM)
        wsend, wrecv, wbufs, wtoken = _split_start("scatter_w_in_start", [pair, land], got, _scatter_plan(1), 3)
        flying.update(w_send=wsend, w_recv=wrecv, w_bufs=wbufs, w_got=got)
        return wtoken

    small = {n: wts[n] for n in SMALL}
    small["norm_g"] = norm_g + gtoken[0, 0]
    loss, grad_x, grads = _local_step(x, p[0], loss_target, dict(w_in=w_in_g, conv_w=conv_g), small,
                                      rest_weights, early_grads, w_in_grad)
    g_w_in = grads["w_in"]
    landed = _split_wait("scatter_w_in_wait", flying["w_bufs"], flying["w_send"], flying["w_recv"], grad_x,
                         _scatter_plan(1))[1]
    w_in_half = _add_chips(g_w_in, flying["w_got"], landed, k, c, "add_chips_w_in")

    sbufs = _split_wait("scatter_rest_wait", flying["bufs"], flying["send"], flying["recv"], g_w_in,
                        _scatter_plan(n_rest))
    mine = _add_landed_all([grads[n] for n in REST], list(sbufs[n_rest:]), k)
    (w_in_sum,), theirs, small_sum = _final_exchange(
        [w_in_half], mine, _pack([grads[n] for n in SMALL] + [grads["conv_w"], loss]))
    g_sums = dict(zip(REST, zip(mine, theirs)))
    g_sums["w_in"] = (w_in_sum,)

    conv_shape = flat2(conv_w).shape
    small_shapes = [wts[n].shape for n in SMALL] + [(N_SHARD,) + conv_shape, (1,)]
    small_g = _unpack(small_sum, small_shapes)
    g_conv = lax.dynamic_index_in_dim(small_g[-2], k, axis=0, keepdims=False)

    outs = {}
    large = ("w_in", "w_branch_a")
    little = [n for n in BIG if n not in large]
    updates = {n: _adamw(flat2(wts[n]), g_sums[n], flat2(mom_m[n]), flat2(mom_v[n]), "adamw_" + n) for n in large}
    updates.update(zip(little, _adamw_all([flat2(wts[n]) for n in little], [g_sums[n] for n in little],
                                          [flat2(mom_m[n]) for n in little], [flat2(mom_v[n]) for n in little])))
    for n in BIG:
        for kind, val in zip(("grad", "delta", "new_m", "new_v"), updates[n]):
            outs[kind, n] = val.reshape(wts[n].shape)
    names = SMALL + ("conv_w",)
    sm = _adamw(_pack([wts[n] for n in names]), (_pack(small_g[:len(SMALL)] + [g_conv]),),
                _pack([mom_m[n] for n in names]), _pack([mom_v[n] for n in names]), "adamw_small")
    sm_shapes = [wts[n].shape for n in names]
    for kind, val in zip(("grad", "delta", "new_m", "new_v"), sm):
        for n, piece in zip(names, _unpack(val, sm_shapes)):
            outs[kind, n] = piece
    return (small_g[-1][0], grad_x, *[outs[kind, n] for kind in ("grad", "delta", "new_m", "new_v") for n in WEIGHTS])
```
